```python
import math
import jax
import jax.numpy as jnp
from jax import lax
import numpy as np

D_MODEL = 2048
BATCH = 4
SEQ = 2048
DEPTH = 2

CTX_LEN = 256
GRID_W = 64
NORM_EPS = 1e-6

DN_HEADS = 8
DN_HEAD_DIM = 128
DN_WIDTH = DN_HEADS * DN_HEAD_DIM
DN_CONV = 5
DN_CHUNK = 64
HG_HEADS = 8
HG_KEY_DIM = 128
HG_VAL_DIM = 128
HG_K_WIDTH = HG_HEADS * HG_KEY_DIM
HG_V_WIDTH = HG_HEADS * HG_VAL_DIM
HG_CHUNK = 64
IN0_SIZES = (3 * DN_WIDTH, DN_WIDTH, 2 * DN_HEADS, 2 * DN_HEADS,
             HG_K_WIDTH, 2 * HG_K_WIDTH, HG_V_WIDTH, HG_V_WIDTH)
IN0_WIDTH = sum(IN0_SIZES)
DA_HEADS = 8
DA_HEAD_DIM = 128
DA_Q_BLOCK = 128
ROPE_BASE = 10000.0
N_EXPERTS = 64
TOP_K = 8
N_GROUPS = 8
TOPK_GROUPS = 4
EXPERT_FF = 512
SHARED_FF = 512
ROUTED_SCALE = 2.5
EXPERT_BLOCK = 128

kernel_name = "hybrid_deltanet_hgrn2_diffattn_moe_dit"


def rms_norm(x, w, eps=NORM_EPS):
    xf = x.astype(jnp.float32)
    y = xf * lax.rsqrt(jnp.mean(jnp.square(xf), axis=-1, keepdims=True) + eps)
    return (y * w.astype(jnp.float32)).astype(x.dtype)


def l2_norm(x, eps=1e-6):
    xf = x.astype(jnp.float32)
    return (xf * lax.rsqrt(jnp.sum(xf * xf, axis=-1, keepdims=True) + eps)).astype(x.dtype)


def split_cols(x, sizes):
    cuts = [int(v) for v in np.cumsum(sizes)[:-1]]
    return jnp.split(x, cuts, axis=-1)


def to_heads(t, n_heads):
    b, l, _ = t.shape
    return t.reshape(b, l, n_heads, -1).transpose(0, 2, 1, 3)


def from_heads(t):
    b, h, l, dh = t.shape
    return t.transpose(0, 2, 1, 3).reshape(b, l, h * dh)


def centred_depthwise_conv(x, w):
    c, k = w.shape
    rhs = jnp.transpose(w)[:, None, :].astype(x.dtype)
    return lax.conv_general_dilated(x, rhs, window_strides=(1,), padding=[(k // 2, k // 2)],
                                    dimension_numbers=("NWC", "WIO", "NWC"),
                                    feature_group_count=c)


def latent_grid_positions(n_tokens):
    rows = n_tokens // GRID_W
    row = jnp.repeat(jnp.arange(rows, dtype=jnp.float32), GRID_W)
    col = jnp.tile(jnp.arange(GRID_W, dtype=jnp.float32), rows)
    return row, col


def axial_rope(x, row, col):
    quarter = x.shape[-1] // 4
    inv_freq = ROPE_BASE ** (-jnp.arange(quarter, dtype=jnp.float32) / quarter)

    def rotate(xs, pos):
        ang = pos[:, None] * inv_freq[None, :]
        cos, sin = jnp.cos(ang), jnp.sin(ang)
        x1 = xs[..., :quarter].astype(jnp.float32)
        x2 = xs[..., quarter:].astype(jnp.float32)
        return jnp.concatenate([x1 * cos - x2 * sin, x2 * cos + x1 * sin], axis=-1)

    half = 2 * quarter
    out = jnp.concatenate([rotate(x[..., :half], row), rotate(x[..., half:], col)], axis=-1)
    return out.astype(x.dtype)


def gated_delta_chunked(q, k, v, g, beta, s0):
    b_, h_, l_, kd = q.shape
    vd = v.shape[-1]
    c = DN_CHUNK
    n = l_ // c
    q = q.astype(jnp.float32).reshape(b_, h_, n, c, kd)
    k = k.astype(jnp.float32).reshape(b_, h_, n, c, kd)
    v = v.astype(jnp.float32).reshape(b_, h_, n, c, vd)
    beta = beta.astype(jnp.float32).reshape(b_, h_, n, c)
    g = jnp.cumsum(g.astype(jnp.float32).reshape(b_, h_, n, c), axis=-1)
    incl = jnp.tril(jnp.ones((c, c), bool))
    strict = jnp.tril(jnp.ones((c, c), bool), -1)
    decay = jnp.exp(jnp.where(incl, g[..., :, None] - g[..., None, :], -jnp.inf))
    kb = k * beta[..., None]
    a_low = jnp.where(strict, jnp.einsum("bhnik,bhnjk->bhnij", kb, k) * decay, 0.0)
    t_mat = a_low + jnp.eye(c, dtype=jnp.float32)
    rhs = jnp.concatenate([v * beta[..., None], kb * jnp.exp(g)[..., None]], axis=-1)
    sol = lax.linalg.triangular_solve(t_mat, rhs, left_side=True, lower=True, unit_diagonal=True)
    u0, w = sol[..., :vd], sol[..., vd:]
    attn = jnp.einsum("bhnik,bhnjk->bhnij", q, k) * decay
    q_dec = q * jnp.exp(g)[..., None]
    k_dec = k * jnp.exp(g[..., -1:] - g)[..., None]
    g_last = jnp.exp(g[..., -1])

    def step(s, xs):
        u0c, wc, ac, qd, kdc, gl = xs
        u = u0c - jnp.einsum("bhck,bhkv->bhcv", wc, s)
        o = jnp.einsum("bhck,bhkv->bhcv", qd, s) + jnp.einsum("bhij,bhjv->bhiv", ac, u)
        s = s * gl[..., None, None] + jnp.einsum("bhck,bhcv->bhkv", kdc, u)
        return s, o

    xs = tuple(jnp.moveaxis(a, 2, 0) for a in (u0, w, attn, q_dec, k_dec, g_last))
    s_fin, o = lax.scan(step, s0.astype(jnp.float32), xs)
    return jnp.moveaxis(o, 0, 2).reshape(b_, h_, l_, vd), s_fin


def hgrn2_chunked(q, k, v, log_f, s0):
    b_, h_, l_, kd = q.shape
    vd = v.shape[-1]
    c = HG_CHUNK
    n = l_ // c

    def chunks(t):
        return jnp.moveaxis(t.astype(jnp.float32).reshape(b_, h_, n, c, t.shape[-1]), 2, 0)

    incl = jnp.tril(jnp.ones((c, c), bool))[:, :, None]

    def step(s, xs):
        qc, kc, vc, lf = xs
        cum = jnp.cumsum(lf, axis=2)
        diff = cum[:, :, :, None, :] - cum[:, :, None, :, :]
        dec = jnp.exp(jnp.where(incl, diff, -jnp.inf))
        a = jnp.sum(qc[:, :, :, None, :] * kc[:, :, None, :, :] * dec, axis=-1)
        o = (jnp.einsum("bhik,bhkv->bhiv", qc * jnp.exp(cum), s)
             + jnp.einsum("bhij,bhjv->bhiv", a, vc))
        last = cum[:, :, -1:, :]
        s = (s * jnp.exp(last[:, :, 0, :])[..., None]
             + jnp.einsum("bhjk,bhjv->bhkv", kc * jnp.exp(last - cum), vc))
        return s, o

    s_fin, o = lax.scan(step, s0.astype(jnp.float32), (chunks(q), chunks(k), chunks(v), chunks(log_f)))
    return jnp.moveaxis(o, 0, 2).reshape(b_, h_, l_, vd), s_fin


def two_segment_scan(scan_fn, ctx_args, lat_args, s0, reverse):
    if reverse:
        ctx_args = tuple(jnp.flip(a, axis=2) for a in ctx_args)
        lat_args = tuple(jnp.flip(a, axis=2) for a in lat_args)
    out_ctx, s_ctx = scan_fn(*ctx_args, s0)
    out_lat, _ = scan_fn(*lat_args, s_ctx)
    if reverse:
        out_ctx, out_lat = jnp.flip(out_ctx, axis=2), jnp.flip(out_lat, axis=2)
    return out_ctx, out_lat


def mixer_delta_hgrn(hl, hc, layer_idx, need_ctx, w_in, dn_conv, dn_a_log, dn_dt_bias, dn_norm,
                     hg_lb_logits, hg_norm, w_out):
    lb = jnp.cumsum(jax.nn.softmax(hg_lb_logits.astype(jnp.float32), axis=0), axis=0)[layer_idx]

    def prepare(h):
        b_, l_, _ = h.shape
        dn_qkv, dn_z, dn_a, dn_b, hg_q, hg_f, hg_i, hg_og = split_cols(h @ w_in, IN0_SIZES)
        qkv = jax.nn.silu(centred_depthwise_conv(dn_qkv, dn_conv))
        q, k, v = jnp.split(qkv, 3, axis=-1)
        g = -jnp.exp(dn_a_log.astype(jnp.float32)) * jax.nn.softplus(
            dn_a.reshape(b_, l_, 2, DN_HEADS).astype(jnp.float32) + dn_dt_bias.astype(jnp.float32))
        beta = jax.nn.sigmoid(dn_b.reshape(b_, l_, 2, DN_HEADS).astype(jnp.float32))
        f = lb + (1.0 - lb) * jax.nn.sigmoid(hg_f.reshape(b_, l_, 2, HG_K_WIDTH).astype(jnp.float32))
        f = f.reshape(b_, l_, 2, HG_HEADS, HG_KEY_DIM).transpose(2, 0, 3, 1, 4)
        return dict(
            dq=l2_norm(to_heads(q, DN_HEADS)) * DN_HEAD_DIM ** -0.5,
            dk=l2_norm(to_heads(k, DN_HEADS)),
            dv=to_heads(v, DN_HEADS),
            g=g.transpose(2, 0, 3, 1), beta=beta.transpose(2, 0, 3, 1),
            z=to_heads(dn_z, DN_HEADS),
            hq=to_heads(jax.nn.silu(hg_q), HG_HEADS),
            hk=1.0 - f, logf=jnp.log(f),
            hv=to_heads(hg_i, HG_HEADS),
            og=to_heads(hg_og, HG_HEADS))

    pc, pl = prepare(hc), prepare(hl)
    b_ = hl.shape[0]
    s0_dn = jnp.zeros((b_, DN_HEADS, DN_HEAD_DIM, DN_HEAD_DIM), jnp.float32)
    s0_hg = jnp.zeros((b_, HG_HEADS, HG_KEY_DIM, HG_VAL_DIM), jnp.float32)
    dn_c = dn_l = hg_c = hg_l = 0.0
    for d in range(2):
        rev = d == 1
        oc, ol = two_segment_scan(gated_delta_chunked,
                                  (pc["dq"], pc["dk"], pc["dv"], pc["g"][d], pc["beta"][d]),
                                  (pl["dq"], pl["dk"], pl["dv"], pl["g"][d], pl["beta"][d]),
                                  s0_dn, rev)
        dn_c, dn_l = dn_c + oc, dn_l + ol
        oc, ol = two_segment_scan(hgrn2_chunked,
                                  (pc["hq"], pc["hk"][d], pc["hv"], pc["logf"][d]),
                                  (pl["hq"], pl["hk"][d], pl["hv"], pl["logf"][d]),
                                  s0_hg, rev)
        hg_c, hg_l = hg_c + oc, hg_l + ol

    def output(p, dn_o, hg_o, dtype):
        dn = rms_norm(dn_o, dn_norm) * jax.nn.silu(p["z"].astype(jnp.float32))
        hg = rms_norm(hg_o, hg_norm) * jax.nn.sigmoid(p["og"].astype(jnp.float32))
        y = jnp.concatenate([from_heads(dn), from_heads(hg)], axis=-1).astype(dtype)
        return y @ w_out

    yl = output(pl, dn_l, hg_l, hl.dtype)
    yc = output(pc, dn_c, hg_c, hc.dtype) if need_ctx else None
    return yl, yc


def mixer_diff_attn(hl, hc, layer_idx, need_ctx, w_in, q_norm, k_norm, lam, sub_norm, w_out):
    lam_init = 0.8 - 0.6 * math.exp(-0.3 * layer_idx)
    lf = lam.astype(jnp.float32)
    lmbda = jnp.exp(jnp.sum(lf[0] * lf[1])) - jnp.exp(jnp.sum(lf[2] * lf[3])) + lam_init

    def project(h, pos):
        b_, l_, _ = h.shape
        q, k, v = jnp.split(h @ w_in, 3, axis=-1)
        q = rms_norm(q.reshape(b_, l_, DA_HEADS, 2, DA_HEAD_DIM), q_norm).transpose(0, 2, 3, 1, 4)
        k = rms_norm(k.reshape(b_, l_, DA_HEADS, 2, DA_HEAD_DIM), k_norm).transpose(0, 2, 3, 1, 4)
        if pos is not None:
            q, k = axial_rope(q, *pos), axial_rope(k, *pos)
        return q, k, to_heads(v, DA_HEADS)

    def attend(q, k, v):
        s = jnp.einsum("bhsqd,bhskd->bhsqk", q, k,
                       preferred_element_type=jnp.float32) * DA_HEAD_DIM ** -0.5
        p = jax.nn.softmax(s, axis=-1)
        p = p[:, :, 0] - lmbda * p[:, :, 1]
        return jnp.einsum("bhqk,bhkv->bhqv", p.astype(v.dtype), v)

    def finish(o):
        o = rms_norm(o, sub_norm) * (1.0 - lam_init)
        return from_heads(o) @ w_out

    b_, l_, _ = hl.shape
    ql, kl, vl = project(hl, latent_grid_positions(l_))
    qc, kc, vc = project(hc, None)
    k_all = jnp.concatenate([kl, kc], axis=3)
    v_all = jnp.concatenate([vl, vc], axis=2)
    nb = l_ // DA_Q_BLOCK
    qb = ql.reshape(b_, DA_HEADS, 2, nb, DA_Q_BLOCK, DA_HEAD_DIM).transpose(3, 0, 1, 2, 4, 5)
    ol = lax.map(lambda qi: attend(qi, k_all, v_all), qb)
    ol = ol.transpose(1, 2, 0, 3, 4).reshape(b_, DA_HEADS, l_, 2 * DA_HEAD_DIM)
    yl = finish(ol)
    yc = finish(attend(qc, kc, vc)) if need_ctx else None
    return yl, yc


def route(h, router_w, router_b):
    t = h.shape[0]
    scores = jax.nn.sigmoid(jnp.dot(h.astype(jnp.float32), router_w.astype(jnp.float32)))
    biased = scores + router_b.astype(jnp.float32)
    per_group = N_EXPERTS // N_GROUPS
    group_score = jnp.sum(lax.top_k(biased.reshape(t, N_GROUPS, per_group), 2)[0], axis=-1)
    _, top_groups = lax.top_k(group_score, TOPK_GROUPS)
    group_mask = jnp.any(top_groups[..., None] == jnp.arange(N_GROUPS), axis=-2)
    expert_mask = jnp.repeat(group_mask, per_group, axis=-1)
    _, idx = lax.top_k(jnp.where(expert_mask, biased, -jnp.inf), TOP_K)
    w = jnp.take_along_axis(scores, idx, axis=-1)
    w = w / jnp.sum(w, axis=-1, keepdims=True) * ROUTED_SCALE
    return idx, w


def routed_experts(h, idx, w, gate_up, down):
    t, d = h.shape
    e_count = gate_up.shape[0]
    n_assign = idx.size
    n_blk = (n_assign + e_count * (EXPERT_BLOCK - 1) + EXPERT_BLOCK - 1) // EXPERT_BLOCK
    n_slot = n_blk * EXPERT_BLOCK
    flat_e = idx.reshape(-1)
    order = jnp.argsort(flat_e)
    sorted_e = flat_e[order]
    counts = jnp.bincount(flat_e, length=e_count)
    padded = (counts + EXPERT_BLOCK - 1) // EXPERT_BLOCK * EXPERT_BLOCK
    pad_end = jnp.cumsum(padded)
    pad_start = pad_end - padded
    start = jnp.cumsum(counts) - counts
    dest = pad_start[sorted_e] + jnp.arange(n_assign) - start[sorted_e]
    slot_tok = jnp.full((n_slot,), t, jnp.int32).at[dest].set((order // TOP_K).astype(jnp.int32))
    slot_w = jnp.zeros((n_slot,), h.dtype).at[dest].set(w.reshape(-1)[order].astype(h.dtype))
    blk_e = jnp.minimum(jnp.searchsorted(pad_end, jnp.arange(n_blk) * EXPERT_BLOCK, side="right"),
                        e_count - 1)
    h_pad = jnp.concatenate([h, jnp.zeros((1, d), h.dtype)], axis=0)

    def expert_block(args):
        tok, wt, e = args
        a, b = jnp.split(h_pad[tok] @ gate_up[e], 2, axis=-1)
        return ((jax.nn.silu(a) * b) @ down[e]) * wt[:, None]

    y_blk = lax.map(expert_block, (slot_tok.reshape(n_blk, EXPERT_BLOCK),
                                   slot_w.reshape(n_blk, EXPERT_BLOCK), blk_e))
    return jax.ops.segment_sum(y_blk.reshape(n_slot, d), slot_tok, num_segments=t + 1)[:t]


def swiglu(h, gate_up, down):
    a, b = jnp.split(h @ gate_up, 2, axis=-1)
    return (jax.nn.silu(a) * b) @ down


def moe_ffn(h, router_w, router_b, exp_gate_up, exp_down, shared_gate_up, shared_down):
    idx, w = route(h, router_w, router_b)
    return routed_experts(h, idx, w, exp_gate_up, exp_down) + swiglu(h, shared_gate_up, shared_down)


def modulation(cond, w, b):
    return jnp.split(jax.nn.silu(cond) @ w + b, 6, axis=-1)


def setup_inputs(seed: int = 0) -> dict:
    key = jax.random.key(seed)
    subkeys = jax.random.split(key, 64)
    counter = [0]

    def nxt():
        counter[0] += 1
        return subkeys[counter[0] - 1]

    def nrm(shape, scale):
        return jax.random.normal(nxt(), shape, jnp.float32) * scale

    def gain(n):
        return 1.0 + 0.02 * jax.random.normal(nxt(), (n,), jnp.float32)

    d = D_MODEL
    inp = {}
    inp["x"] = nrm((BATCH, SEQ, d), 1.0)
    inp["c"] = nrm((BATCH, d), 1.0)
    inp["ctx"] = nrm((BATCH, CTX_LEN, d), 1.0)
    inp["c_ctx"] = nrm((d,), 1.0)
    inp["hg_lb_logits"] = nrm((DEPTH + 1, HG_K_WIDTH), 0.1)
    for l in range(DEPTH):
        pre = "l%d_" % l
        inp[pre + "mod_w"] = nrm((d, 6 * d), 0.5 * d ** -0.5)
        inp[pre + "mod_b"] = nrm((6 * d,), 0.02)
        inp[pre + "norm1"] = gain(d)
        inp[pre + "norm2"] = gain(d)
        if l % 2 == 0:
            inp[pre + "w_in"] = nrm((d, IN0_WIDTH), d ** -0.5)
            inp[pre + "dn_conv"] = nrm((3 * DN_WIDTH, DN_CONV), DN_CONV ** -0.5)
            inp[pre + "dn_a_log"] = jnp.log(jax.random.uniform(nxt(), (2, DN_HEADS), jnp.float32, 1.0, 16.0))
            dt = jnp.exp(jax.random.uniform(nxt(), (2, DN_HEADS), jnp.float32,
                                            math.log(1e-3), math.log(1e-1)))
            inp[pre + "dn_dt_bias"] = dt + jnp.log(-jnp.expm1(-dt))
            inp[pre + "dn_norm"] = gain(DN_HEAD_DIM)
            inp[pre + "hg_norm"] = gain(HG_VAL_DIM)
        else:
            inp[pre + "w_in"] = nrm((d, 3 * d), d ** -0.5)
            inp[pre + "q_norm"] = gain(DA_HEAD_DIM)
            inp[pre + "k_norm"] = gain(DA_HEAD_DIM)
            inp[pre + "lambda"] = nrm((4, DA_HEAD_DIM), 0.1)
            inp[pre + "sub_norm"] = gain(2 * DA_HEAD_DIM)
        inp[pre + "w_out"] = nrm((d, d), d ** -0.5)
        inp[pre + "router_w"] = nrm((d, N_EXPERTS), d ** -0.5)
        inp[pre + "router_b"] = nrm((N_EXPERTS,), 0.01)
        inp[pre + "exp_gate_up"] = nrm((N_EXPERTS, d, 2 * EXPERT_FF), d ** -0.5)
        inp[pre + "exp_down"] = nrm((N_EXPERTS, EXPERT_FF, d), EXPERT_FF ** -0.5)
        inp[pre + "shared_gate_up"] = nrm((d, 2 * SHARED_FF), d ** -0.5)
        inp[pre + "shared_down"] = nrm((SHARED_FF, d), SHARED_FF ** -0.5)
    return inp


def reference(x, c, ctx, c_ctx, hg_lb_logits,
              l0_mod_w, l0_mod_b, l0_norm1, l0_norm2, l0_w_in, l0_dn_conv, l0_dn_a_log,
              l0_dn_dt_bias, l0_dn_norm, l0_hg_norm, l0_w_out, l0_router_w, l0_router_b,
              l0_exp_gate_up, l0_exp_down, l0_shared_gate_up, l0_shared_down,
              l1_mod_w, l1_mod_b, l1_norm1, l1_norm2, l1_w_in, l1_q_norm, l1_k_norm, l1_lambda,
              l1_sub_norm, l1_w_out, l1_router_w, l1_router_b, l1_exp_gate_up, l1_exp_down,
              l1_shared_gate_up, l1_shared_down):
    layers = [
        dict(mod_w=l0_mod_w, mod_b=l0_mod_b, norm1=l0_norm1, norm2=l0_norm2,
             mixer=(l0_w_in, l0_dn_conv, l0_dn_a_log, l0_dn_dt_bias, l0_dn_norm,
                    hg_lb_logits, l0_hg_norm, l0_w_out),
             moe=(l0_router_w, l0_router_b, l0_exp_gate_up, l0_exp_down,
                  l0_shared_gate_up, l0_shared_down)),
        dict(mod_w=l1_mod_w, mod_b=l1_mod_b, norm1=l1_norm1, norm2=l1_norm2,
             mixer=(l1_w_in, l1_q_norm, l1_k_norm, l1_lambda, l1_sub_norm, l1_w_out),
             moe=(l1_router_w, l1_router_b, l1_exp_gate_up, l1_exp_down,
                  l1_shared_gate_up, l1_shared_down)),
    ]
    d = x.shape[-1]
    xl, xc = x, ctx
    for l in range(DEPTH):
        p = layers[l]
        last = l == DEPTH - 1
        ml = [m[:, None, :] for m in modulation(c, p["mod_w"], p["mod_b"])]
        mc = modulation(c_ctx, p["mod_w"], p["mod_b"])
        hl = rms_norm(xl, p["norm1"]) * (1.0 + ml[1]) + ml[0]
        hc = rms_norm(xc, p["norm1"]) * (1.0 + mc[1]) + mc[0]
        if l % 2 == 0:
            yl, yc = mixer_delta_hgrn(hl, hc, l, not last, *p["mixer"])
        else:
            yl, yc = mixer_diff_attn(hl, hc, l, not last, *p["mixer"])
        xl = xl + ml[2] * yl
        hl = rms_norm(xl, p["norm2"]) * (1.0 + ml[4]) + ml[3]
        if last:
            y = moe_ffn(hl.reshape(-1, d), *p["moe"]).reshape(hl.shape)
            xl = xl + ml[5] * y
        else:
            xc = xc + mc[2] * yc
            hc = rms_norm(xc, p["norm2"]) * (1.0 + mc[4]) + mc[3]
            n_lat = hl.shape[0] * hl.shape[1]
            y = moe_ffn(jnp.concatenate([hl.reshape(-1, d), hc.reshape(-1, d)], axis=0), *p["moe"])
            xl = xl + ml[5] * y[:n_lat].reshape(hl.shape)
            xc = xc + mc[5] * y[n_lat:].reshape(hc.shape)
    return xl
```

```python
import functools
import math

import jax
import jax.numpy as jnp
import numpy as np
from jax import lax
from jax.experimental import pallas as pl
from jax.experimental.pallas import tpu as pltpu

F32 = jnp.float32
BF16 = jnp.bfloat16

NORM_EPS = 1e-6
GRID_W = 64
ROPE_BASE = 10000.0

DN_HEADS = 8
DN_HEAD_DIM = 128
DN_WIDTH = DN_HEADS * DN_HEAD_DIM
DN_CHUNK = 64
HG_HEADS = 8
HG_KEY_DIM = 128
HG_VAL_DIM = 128
HG_K_WIDTH = HG_HEADS * HG_KEY_DIM
HG_V_WIDTH = HG_HEADS * HG_VAL_DIM
HG_CHUNK = 64
IN0_SIZES = (3 * DN_WIDTH, DN_WIDTH, 2 * DN_HEADS, 2 * DN_HEADS,
             HG_K_WIDTH, 2 * HG_K_WIDTH, HG_V_WIDTH, HG_V_WIDTH)
DA_HEADS = 8
DA_HEAD_DIM = 128
N_EXPERTS = 64
TOP_K = 8
N_GROUPS = 8
TOPK_GROUPS = 4
EXPERT_FF = 512
ROUTED_SCALE = 2.5

VMEM_LIMIT_BYTES = 56 * 1024 * 1024
MOD_ROWS = 8
EXPERT_SLOT_BLOCK = 256


def _params(*sem):
    return pltpu.CompilerParams(dimension_semantics=sem, vmem_limit_bytes=VMEM_LIMIT_BYTES)


def _silu(x):
    return x * (1.0 / (1.0 + jnp.exp(-x)))


def _mod_body(c_ref, w_ref, b_ref, o_ref):
    a = _silu(c_ref[...]).astype(BF16)
    o_ref[...] = jnp.dot(a, w_ref[...].astype(BF16), preferred_element_type=F32) + b_ref[...]


def modulation(cond, w, b, tn=1024):
    m, k = cond.shape
    n = w.shape[1]
    return pl.pallas_call(
        _mod_body, grid=(n // tn,),
        in_specs=[pl.BlockSpec((m, k), lambda j: (0, 0)),
                  pl.BlockSpec((k, tn), lambda j: (0, j)),
                  pl.BlockSpec((1, tn), lambda j: (0, j))],
        out_specs=pl.BlockSpec((m, tn), lambda j: (0, j)),
        out_shape=jax.ShapeDtypeStruct((m, n), F32),
        compiler_params=_params("arbitrary"), name="modulation",
    )(cond, w, b.reshape(1, n))


def _mod_row(tile, tile_rows, n_lat, lat_len, n_batch):
    start = tile * tile_rows
    return jnp.where(start < n_lat, start // lat_len, n_batch)


def _adaln_body(x_ref, nw_ref, shift_ref, scale_ref, o_ref):
    x = x_ref[...]
    y = x * lax.rsqrt(jnp.mean(x * x, axis=-1, keepdims=True) + NORM_EPS) * nw_ref[...]
    o_ref[...] = (y * (1.0 + scale_ref[0]) + shift_ref[0]).astype(o_ref.dtype)


def _adaln_router_body(x_ref, nw_ref, shift_ref, scale_ref, rw_ref, o_ref, lg_ref):
    x = x_ref[...]
    y = x * lax.rsqrt(jnp.mean(x * x, axis=-1, keepdims=True) + NORM_EPS) * nw_ref[...]
    h = y * (1.0 + scale_ref[0]) + shift_ref[0]
    o_ref[...] = h.astype(o_ref.dtype)
    lg_ref[...] = jnp.dot(h, rw_ref[...], preferred_element_type=F32, precision=lax.Precision.HIGHEST)


def adaln(x, norm_w, mod3, shift_idx, geom, router_w=None, tl=256):
    t, d = x.shape
    n_lat, lat_len, n_batch = geom
    row = functools.partial(_mod_row, tile_rows=tl, n_lat=n_lat, lat_len=lat_len, n_batch=n_batch)
    in_specs = [pl.BlockSpec((tl, d), lambda i: (i, 0)),
                pl.BlockSpec((1, d), lambda i: (0, 0)),
                pl.BlockSpec((1, 1, d), lambda i: (row(i) * 6 + shift_idx, 0, 0)),
                pl.BlockSpec((1, 1, d), lambda i: (row(i) * 6 + shift_idx + 1, 0, 0))]
    args = [x, norm_w.reshape(1, d), mod3, mod3]
    if router_w is None:
        return pl.pallas_call(
            _adaln_body, grid=(t // tl,), in_specs=in_specs,
            out_specs=pl.BlockSpec((tl, d), lambda i: (i, 0)),
            out_shape=jax.ShapeDtypeStruct((t, d), BF16),
            compiler_params=_params("arbitrary"), name="adaln")(*args)
    e = router_w.shape[1]
    return pl.pallas_call(
        _adaln_router_body, grid=(t // tl,),
        in_specs=in_specs + [pl.BlockSpec((d, e), lambda i: (0, 0))],
        out_specs=[pl.BlockSpec((tl, d), lambda i: (i, 0)), pl.BlockSpec((tl, e), lambda i: (i, 0))],
        out_shape=[jax.ShapeDtypeStruct((t, d), BF16), jax.ShapeDtypeStruct((t, e), F32)],
        compiler_params=_params("arbitrary"), name="adaln_router")(*args, router_w)


def _mm_body(a_ref, w_ref, o_ref, wb_ref):
    @pl.when(pl.program_id(1) == 0)
    def _():
        wb_ref[...] = w_ref[...].astype(BF16)
    o_ref[...] = jnp.dot(a_ref[...], wb_ref[...], preferred_element_type=F32).astype(o_ref.dtype)


def _mm_res_body(a_ref, w_ref, res_ref, gate_ref, o_ref, wb_ref):
    @pl.when(pl.program_id(1) == 0)
    def _():
        wb_ref[...] = w_ref[...].astype(BF16)
    acc = jnp.dot(a_ref[...], wb_ref[...], preferred_element_type=F32)
    o_ref[...] = res_ref[...] + gate_ref[0] * acc


def matmul(a, w, n_cols=None, col_block0=0, tm=512, tn=512, out_dtype=F32):
    m, k = a.shape
    n = w.shape[1] if n_cols is None else n_cols
    return pl.pallas_call(
        _mm_body, grid=(n // tn, m // tm),
        in_specs=[pl.BlockSpec((tm, k), lambda j, i: (i, 0)),
                  pl.BlockSpec((k, tn), lambda j, i: (0, j + col_block0))],
        out_specs=pl.BlockSpec((tm, tn), lambda j, i: (i, j)),
        out_shape=jax.ShapeDtypeStruct((m, n), out_dtype),
        scratch_shapes=[pltpu.VMEM((k, tn), BF16)],
        compiler_params=_params("arbitrary", "arbitrary"), name="matmul",
    )(a, w)


def matmul_residual(a, w, res, mod3, gate_idx, geom, tm=512, tn=512):
    m, k = a.shape
    n = w.shape[1]
    n_lat, lat_len, n_batch = geom
    row = functools.partial(_mod_row, tile_rows=tm, n_lat=n_lat, lat_len=lat_len, n_batch=n_batch)
    nb = n // tn
    return pl.pallas_call(
        _mm_res_body, grid=(n // tn, m // tm),
        in_specs=[pl.BlockSpec((tm, k), lambda j, i: (i, 0)),
                  pl.BlockSpec((k, tn), lambda j, i: (0, j)),
                  pl.BlockSpec((tm, tn), lambda j, i: (i, j)),
                  pl.BlockSpec((1, 1, tn), lambda j, i: (row(i) * 6 + gate_idx, 0, j))],
        out_specs=pl.BlockSpec((tm, tn), lambda j, i: (i, j)),
        out_shape=jax.ShapeDtypeStruct((m, n), F32),
        scratch_shapes=[pltpu.VMEM((k, tn), BF16)],
        compiler_params=_params("arbitrary", "arbitrary"), name="matmul_residual",
    )(a, w, res, mod3)


def _expert_body(blk_e_ref, blk_new_ref, n_used_ref, x_ref, gu_ref, dn_ref, o_ref, gub_ref, dnb_ref):
    i = pl.program_id(0)

    @pl.when(blk_new_ref[i] == 1)
    def _():
        gub_ref[...] = gu_ref[0].astype(BF16)
        dnb_ref[...] = dn_ref[0].astype(BF16)

    @pl.when(i < n_used_ref[0])
    def _():
        ff = dnb_ref.shape[0]
        h1 = jnp.dot(x_ref[...], gub_ref[...], preferred_element_type=F32)
        act = (_silu(h1[:, :ff]) * h1[:, ff:]).astype(BF16)
        o_ref[...] = jnp.dot(act, dnb_ref[...], preferred_element_type=F32)

    @pl.when(i >= n_used_ref[0])
    def _():
        o_ref[...] = jnp.zeros_like(o_ref)


def expert_blocks(x, gate_up, down, blk_e, blk_new, n_used, blk):
    s, d = x.shape
    _, _, f2 = gate_up.shape
    n_blk = s // blk
    grid_spec = pltpu.PrefetchScalarGridSpec(
        num_scalar_prefetch=3, grid=(n_blk,),
        in_specs=[pl.BlockSpec((blk, d), lambda i, be, bn, nu: (i, 0)),
                  pl.BlockSpec((1, d, f2), lambda i, be, bn, nu: (be[i], 0, 0)),
                  pl.BlockSpec((1, f2 // 2, d), lambda i, be, bn, nu: (be[i], 0, 0))],
        out_specs=pl.BlockSpec((blk, d), lambda i, be, bn, nu: (i, 0)),
        scratch_shapes=[pltpu.VMEM((d, f2), BF16), pltpu.VMEM((f2 // 2, d), BF16)])
    return pl.pallas_call(
        _expert_body, grid_spec=grid_spec,
        out_shape=jax.ShapeDtypeStruct((s, d), F32),
        compiler_params=_params("arbitrary"), name="expert_blocks",
    )(blk_e, blk_new, n_used, x, gate_up, down)


def _qk_prep_body(x_ref, qw_ref, kw_ref, cos_ref, sin_ref, o_ref, *, n_qk_groups, head_dim):
    cos = cos_ref[...]
    sin = sin_ref[...]
    lane = lax.broadcasted_iota(jnp.int32, cos.shape, 1)
    first = (lane % (head_dim // 2)) < (head_dim // 4)
    for g in range(n_qk_groups):
        sl = slice(g * head_dim, (g + 1) * head_dim)
        x = x_ref[:, sl]
        w = qw_ref[...] if g < n_qk_groups // 2 else kw_ref[...]
        y = x * lax.rsqrt(jnp.mean(x * x, axis=-1, keepdims=True) + NORM_EPS) * w
        swapped = jnp.where(first, pltpu.roll(y, head_dim - head_dim // 4, 1), pltpu.roll(y, head_dim // 4, 1))
        o_ref[:, sl] = (y * cos + swapped * sin).astype(o_ref.dtype)
    rest = n_qk_groups * head_dim
    o_ref[:, rest:] = x_ref[:, rest:].astype(o_ref.dtype)


def qk_prep(proj, q_norm, k_norm, cos_t, sin_t, tl=256):
    t, n = proj.shape
    hd = q_norm.shape[0]
    n_groups = (2 * n // 3) // hd
    body = functools.partial(_qk_prep_body, n_qk_groups=n_groups, head_dim=hd)
    return pl.pallas_call(
        body, grid=(t // tl,),
        in_specs=[pl.BlockSpec((tl, n), lambda i: (i, 0)),
                  pl.BlockSpec((1, hd), lambda i: (0, 0)),
                  pl.BlockSpec((1, hd), lambda i: (0, 0)),
                  pl.BlockSpec((tl, hd), lambda i: (i, 0)),
                  pl.BlockSpec((tl, hd), lambda i: (i, 0))],
        out_specs=pl.BlockSpec((tl, n), lambda i: (i, 0)),
        out_shape=jax.ShapeDtypeStruct((t, n), BF16),
        compiler_params=_params("arbitrary"), name="qk_prep",
    )(proj, q_norm.reshape(1, hd), k_norm.reshape(1, hd), cos_t, sin_t)


def rope_tables(n_lat_tokens_per_sample, n_batch, n_ctx_tokens, head_dim):
    quarter = head_dim // 4
    inv_freq = ROPE_BASE ** (-jnp.arange(quarter, dtype=F32) / quarter)
    rows = n_lat_tokens_per_sample // GRID_W
    row = jnp.repeat(jnp.arange(rows, dtype=F32), GRID_W)
    col = jnp.tile(jnp.arange(GRID_W, dtype=F32), rows)
    ang_r = row[:, None] * inv_freq[None, :]
    ang_c = col[:, None] * inv_freq[None, :]
    cos = jnp.concatenate([jnp.cos(ang_r), jnp.cos(ang_r), jnp.cos(ang_c), jnp.cos(ang_c)], axis=-1)
    sin = jnp.concatenate([-jnp.sin(ang_r), jnp.sin(ang_r), -jnp.sin(ang_c), jnp.sin(ang_c)], axis=-1)
    cos = jnp.concatenate([jnp.tile(cos, (n_batch, 1)), jnp.ones((n_ctx_tokens, head_dim), F32)], axis=0)
    sin = jnp.concatenate([jnp.tile(sin, (n_batch, 1)), jnp.zeros((n_ctx_tokens, head_dim), F32)], axis=0)
    return cos, sin


def _diff_attn_body(lam_ref, q_ref, kl_ref, kc_ref, vl_ref, vc_ref, sw_ref, o_ref, *, head_dim, out_scale):
    lam = lam_ref[0]
    scale = head_dim ** -0.5
    contract_last = (((1,), (1,)), ((), ()))
    p_l = p_c = None
    for s in range(2):
        sl = slice(s * head_dim, (s + 1) * head_dim)
        qs = q_ref[:, sl]
        s_l = lax.dot_general(qs, kl_ref[:, sl], contract_last, preferred_element_type=F32) * scale
        s_c = lax.dot_general(qs, kc_ref[:, sl], contract_last, preferred_element_type=F32) * scale
        m = jnp.maximum(jnp.max(s_l, axis=-1, keepdims=True), jnp.max(s_c, axis=-1, keepdims=True))
        e_l = jnp.exp(s_l - m)
        e_c = jnp.exp(s_c - m)
        inv = 1.0 / (jnp.sum(e_l, axis=-1, keepdims=True) + jnp.sum(e_c, axis=-1, keepdims=True))
        if s == 0:
            p_l, p_c = e_l * inv, e_c * inv
        else:
            p_l, p_c = p_l - (lam * inv) * e_l, p_c - (lam * inv) * e_c
    o = (jnp.dot(p_l.astype(BF16), vl_ref[...], preferred_element_type=F32)
         + jnp.dot(p_c.astype(BF16), vc_ref[...], preferred_element_type=F32))
    y = o * lax.rsqrt(jnp.mean(o * o, axis=-1, keepdims=True) + NORM_EPS) * sw_ref[...]
    o_ref[...] = (y * out_scale).astype(o_ref.dtype)


def diff_attention(qkv, lmbda, sub_norm, n_batch, lat_len, ctx_len, n_heads, head_dim, out_scale, tq=256):
    hw = 2 * head_dim
    nq = lat_len // tq
    ctx_blk0 = n_batch * lat_len // ctx_len
    body = functools.partial(_diff_attn_body, head_dim=head_dim, out_scale=out_scale)
    return pl.pallas_call(
        body, grid=(n_batch, n_heads, nq),
        in_specs=[pl.BlockSpec(memory_space=pltpu.SMEM),
                  pl.BlockSpec((tq, hw), lambda b, h, i: (b * nq + i, h)),
                  pl.BlockSpec((lat_len, hw), lambda b, h, i: (b, n_heads + h)),
                  pl.BlockSpec((ctx_len, hw), lambda b, h, i: (ctx_blk0 + b, n_heads + h)),
                  pl.BlockSpec((lat_len, hw), lambda b, h, i: (b, 2 * n_heads + h)),
                  pl.BlockSpec((ctx_len, hw), lambda b, h, i: (ctx_blk0 + b, 2 * n_heads + h)),
                  pl.BlockSpec((1, hw), lambda b, h, i: (0, 0))],
        out_specs=pl.BlockSpec((tq, hw), lambda b, h, i: (b * nq + i, h)),
        out_shape=jax.ShapeDtypeStruct((n_batch * lat_len, n_heads * hw), BF16),
        compiler_params=_params("arbitrary", "arbitrary", "arbitrary"), name="diff_attention",
    )(lmbda.reshape(1), qkv, qkv, qkv, qkv, qkv, sub_norm.reshape(1, hw))


def _rms_norm(x, w, eps=NORM_EPS):
    xf = x.astype(F32)
    return xf * lax.rsqrt(jnp.mean(jnp.square(xf), axis=-1, keepdims=True) + eps) * w.astype(F32)


def _l2_norm(x, eps=1e-6):
    return x * lax.rsqrt(jnp.sum(x * x, axis=-1, keepdims=True) + eps)


def _to_heads(t, n_heads):
    b, l, _ = t.shape
    return t.reshape(b, l, n_heads, -1).transpose(0, 2, 1, 3)


def _from_heads(t):
    b, h, l, dh = t.shape
    return t.transpose(0, 2, 1, 3).reshape(b, l, h * dh)


def _dw_conv(x, w):
    c, k = w.shape
    rhs = jnp.transpose(w)[:, None, :].astype(x.dtype)
    return lax.conv_general_dilated(x, rhs, window_strides=(1,), padding=[(k // 2, k // 2)],
                                    dimension_numbers=("NWC", "WIO", "NWC"), feature_group_count=c)


def _gated_delta_chunked(q, k, v, g, beta, s0):
    b_, h_, l_, kd = q.shape
    vd = v.shape[-1]
    c = DN_CHUNK
    n = l_ // c
    q = q.reshape(b_, h_, n, c, kd)
    k = k.reshape(b_, h_, n, c, kd)
    v = v.reshape(b_, h_, n, c, vd)
    beta = beta.reshape(b_, h_, n, c)
    g = jnp.cumsum(g.reshape(b_, h_, n, c), axis=-1)
    incl = jnp.tril(jnp.ones((c, c), bool))
    strict = jnp.tril(jnp.ones((c, c), bool), -1)
    decay = jnp.exp(jnp.where(incl, g[..., :, None] - g[..., None, :], -jnp.inf))
    kb = k * beta[..., None]
    a_low = jnp.where(strict, jnp.einsum("bhnik,bhnjk->bhnij", kb, k) * decay, 0.0)
    t_mat = a_low + jnp.eye(c, dtype=F32)
    rhs = jnp.concatenate([v * beta[..., None], kb * jnp.exp(g)[..., None]], axis=-1)
    sol = lax.linalg.triangular_solve(t_mat, rhs, left_side=True, lower=True, unit_diagonal=True)
    u0, w = sol[..., :vd], sol[..., vd:]
    attn = jnp.einsum("bhnik,bhnjk->bhnij", q, k) * decay
    q_dec = q * jnp.exp(g)[..., None]
    k_dec = k * jnp.exp(g[..., -1:] - g)[..., None]
    g_last = jnp.exp(g[..., -1])

    def step(s, xs):
        u0c, wc, ac, qd, kdc, gl = xs
        u = u0c - jnp.einsum("bhck,bhkv->bhcv", wc, s)
        o = jnp.einsum("bhck,bhkv->bhcv", qd, s) + jnp.einsum("bhij,bhjv->bhiv", ac, u)
        s = s * gl[..., None, None] + jnp.einsum("bhck,bhcv->bhkv", kdc, u)
        return s, o

    xs = tuple(jnp.moveaxis(a, 2, 0) for a in (u0, w, attn, q_dec, k_dec, g_last))
    s_fin, o = lax.scan(step, s0, xs)
    return jnp.moveaxis(o, 0, 2).reshape(b_, h_, l_, vd), s_fin


def _hgrn2_chunked(q, k, v, log_f, s0):
    b_, h_, l_, kd = q.shape
    vd = v.shape[-1]
    c = HG_CHUNK
    n = l_ // c

    def chunks(t):
        return jnp.moveaxis(t.reshape(b_, h_, n, c, t.shape[-1]), 2, 0)

    incl = jnp.tril(jnp.ones((c, c), bool))[:, :, None]

    def step(s, xs):
        qc, kc, vc, lf = xs
        cum = jnp.cumsum(lf, axis=2)
        diff = cum[:, :, :, None, :] - cum[:, :, None, :, :]
        dec = jnp.exp(jnp.where(incl, diff, -jnp.inf))
        a = jnp.sum(qc[:, :, :, None, :] * kc[:, :, None, :, :] * dec, axis=-1)
        o = (jnp.einsum("bhik,bhkv->bhiv", qc * jnp.exp(cum), s)
             + jnp.einsum("bhij,bhjv->bhiv", a, vc))
        last = cum[:, :, -1:, :]
        s = (s * jnp.exp(last[:, :, 0, :])[..., None]
             + jnp.einsum("bhjk,bhjv->bhkv", kc * jnp.exp(last - cum), vc))
        return s, o

    s_fin, o = lax.scan(step, s0, (chunks(q), chunks(k), chunks(v), chunks(log_f)))
    return jnp.moveaxis(o, 0, 2).reshape(b_, h_, l_, vd), s_fin


def _two_segment_scan(scan_fn, ctx_args, lat_args, s0, reverse):
    if reverse:
        ctx_args = tuple(jnp.flip(a, axis=2) for a in ctx_args)
        lat_args = tuple(jnp.flip(a, axis=2) for a in lat_args)
    out_ctx, s_ctx = scan_fn(*ctx_args, s0)
    out_lat, _ = scan_fn(*lat_args, s_ctx)
    if reverse:
        out_ctx, out_lat = jnp.flip(out_ctx, axis=2), jnp.flip(out_lat, axis=2)
    return out_ctx, out_lat


def _split_cols(x, sizes):
    cuts = [int(v) for v in np.cumsum(sizes)[:-1]]
    return jnp.split(x, cuts, axis=-1)


def _mixer0_core(proj_l, proj_c, layer_idx, dn_conv, dn_a_log, dn_dt_bias, dn_norm, hg_lb_logits, hg_norm):
    lb = jnp.cumsum(jax.nn.softmax(hg_lb_logits, axis=0), axis=0)[layer_idx]

    def prepare(p):
        b_, l_, _ = p.shape
        dn_qkv, dn_z, dn_a, dn_b, hg_q, hg_f, hg_i, hg_og = _split_cols(p, IN0_SIZES)
        qkv = jax.nn.silu(_dw_conv(dn_qkv, dn_conv))
        q, k, v = jnp.split(qkv, 3, axis=-1)
        g = -jnp.exp(dn_a_log) * jax.nn.softplus(dn_a.reshape(b_, l_, 2, DN_HEADS) + dn_dt_bias)
        beta = jax.nn.sigmoid(dn_b.reshape(b_, l_, 2, DN_HEADS))
        f = lb + (1.0 - lb) * jax.nn.sigmoid(hg_f.reshape(b_, l_, 2, HG_K_WIDTH))
        f = f.reshape(b_, l_, 2, HG_HEADS, HG_KEY_DIM).transpose(2, 0, 3, 1, 4)
        return dict(
            dq=_l2_norm(_to_heads(q, DN_HEADS)) * DN_HEAD_DIM ** -0.5,
            dk=_l2_norm(_to_heads(k, DN_HEADS)),
            dv=_to_heads(v, DN_HEADS),
            g=g.transpose(2, 0, 3, 1), beta=beta.transpose(2, 0, 3, 1),
            z=_to_heads(dn_z, DN_HEADS),
            hq=_to_heads(jax.nn.silu(hg_q), HG_HEADS),
            hk=1.0 - f, logf=jnp.log(f),
            hv=_to_heads(hg_i, HG_HEADS),
            og=_to_heads(hg_og, HG_HEADS))

    pc, pl_ = prepare(proj_c), prepare(proj_l)
    b_ = proj_l.shape[0]
    s0_dn = jnp.zeros((b_, DN_HEADS, DN_HEAD_DIM, DN_HEAD_DIM), F32)
    s0_hg = jnp.zeros((b_, HG_HEADS, HG_KEY_DIM, HG_VAL_DIM), F32)
    dn_c = dn_l = hg_c = hg_l = 0.0
    for d in range(2):
        rev = d == 1
        oc, ol = _two_segment_scan(_gated_delta_chunked,
                                   (pc["dq"], pc["dk"], pc["dv"], pc["g"][d], pc["beta"][d]),
                                   (pl_["dq"], pl_["dk"], pl_["dv"], pl_["g"][d], pl_["beta"][d]),
                                   s0_dn, rev)
        dn_c, dn_l = dn_c + oc, dn_l + ol
        oc, ol = _two_segment_scan(_hgrn2_chunked,
                                   (pc["hq"], pc["hk"][d], pc["hv"], pc["logf"][d]),
                                   (pl_["hq"], pl_["hk"][d], pl_["hv"], pl_["logf"][d]),
                                   s0_hg, rev)
        hg_c, hg_l = hg_c + oc, hg_l + ol

    def output(p, dn_o, hg_o):
        dn = _rms_norm(dn_o, dn_norm) * jax.nn.silu(p["z"])
        hg = _rms_norm(hg_o, hg_norm) * jax.nn.sigmoid(p["og"])
        return jnp.concatenate([_from_heads(dn), _from_heads(hg)], axis=-1)

    return output(pl_, dn_l, hg_l), output(pc, dn_c, hg_c)


def _route(logits, router_b):
    t = logits.shape[0]
    scores = jax.nn.sigmoid(logits)
    biased = scores + router_b
    per_group = N_EXPERTS // N_GROUPS
    group_score = jnp.sum(lax.top_k(biased.reshape(t, N_GROUPS, per_group), 2)[0], axis=-1)
    _, top_groups = lax.top_k(group_score, TOPK_GROUPS)
    group_mask = jnp.any(top_groups[..., None] == jnp.arange(N_GROUPS), axis=-2)
    expert_mask = jnp.repeat(group_mask, per_group, axis=-1)
    _, idx = lax.top_k(jnp.where(expert_mask, biased, -jnp.inf), TOP_K)
    w = jnp.take_along_axis(scores, idx, axis=-1)
    w = w / jnp.sum(w, axis=-1, keepdims=True) * ROUTED_SCALE
    return idx, w


def moe_ffn(h_bf, logits, router_b, exp_gate_up, exp_down, shared_gate_up, shared_down):
    t, d = h_bf.shape
    e_count = exp_gate_up.shape[0]
    blk = EXPERT_SLOT_BLOCK
    idx, w = _route(logits, router_b)
    onehot = (idx[:, :, None] == jnp.arange(e_count)[None, None, :]).astype(jnp.int32).sum(axis=1)
    rank = jnp.cumsum(onehot, axis=0) - onehot
    counts = jnp.sum(onehot, axis=0)
    padded = (counts + blk - 1) // blk * blk
    pad_end = jnp.cumsum(padded)
    pad_start = pad_end - padded
    dest = pad_start[idx] + jnp.take_along_axis(rank, idx, axis=1)
    n_blk = (t * TOP_K + e_count * (blk - 1)) // blk + 1
    n_slot = n_blk * blk
    tok = jnp.broadcast_to(jnp.arange(t, dtype=jnp.int32)[:, None], (t, TOP_K))
    slot_tok = jnp.full((n_slot,), t, jnp.int32).at[dest.reshape(-1)].set(tok.reshape(-1))
    blk_starts = jnp.arange(n_blk, dtype=jnp.int32) * blk
    blk_e = jnp.minimum(jnp.searchsorted(pad_end, blk_starts, side="right"), e_count - 1).astype(jnp.int32)
    blk_new = jnp.concatenate([jnp.ones((1,), jnp.int32), (blk_e[1:] != blk_e[:-1]).astype(jnp.int32)])
    n_used = (pad_end[-1] // blk).astype(jnp.int32).reshape(1)
    h_pad = jnp.concatenate([h_bf, jnp.zeros((1, d), h_bf.dtype)], axis=0)
    x_sorted = h_pad[slot_tok]
    y_slot = expert_blocks(x_sorted, exp_gate_up, exp_down, blk_e, blk_new, n_used, blk)
    routed = jnp.sum(y_slot[dest] * w[:, :, None], axis=1)
    one = jnp.ones((t // blk,), jnp.int32)
    shared = expert_blocks(h_bf, shared_gate_up[None], shared_down[None],
                           jnp.zeros((t // blk,), jnp.int32), one.at[1:].set(0),
                           jnp.full((1,), t // blk, jnp.int32), blk)
    return routed + shared


def kernel(x, c, ctx, c_ctx, hg_lb_logits, l0_mod_w, l0_mod_b, l0_norm1, l0_norm2, l0_w_in, l0_dn_conv, l0_dn_a_log, l0_dn_dt_bias, l0_dn_norm, l0_hg_norm, l0_w_out, l0_router_w, l0_router_b, l0_exp_gate_up, l0_exp_down, l0_shared_gate_up, l0_shared_down, l1_mod_w, l1_mod_b, l1_norm1, l1_norm2, l1_w_in, l1_q_norm, l1_k_norm, l1_lambda, l1_sub_norm, l1_w_out, l1_router_w, l1_router_b, l1_exp_gate_up, l1_exp_down, l1_shared_gate_up, l1_shared_down):
    n_batch, lat_len, d = x.shape
    ctx_len = ctx.shape[1]
    n_lat = n_batch * lat_len
    n_ctx = n_batch * ctx_len
    geom = (n_lat, lat_len, n_batch)
    geom_lat_only = (n_lat, lat_len, n_batch)

    xs = jnp.concatenate([x.reshape(n_lat, d), ctx.reshape(n_ctx, d)], axis=0)
    cond = jnp.concatenate([c, c_ctx[None], jnp.zeros((MOD_ROWS - n_batch - 1, d), F32)], axis=0)

    mod3 = modulation(cond, l0_mod_w, l0_mod_b).reshape(MOD_ROWS * 6, 1, d)
    h = adaln(xs, l0_norm1, mod3, 0, geom)
    n_dn = IN0_SIZES[0] + IN0_SIZES[1]
    n_gate = IN0_SIZES[2] + IN0_SIZES[3]
    proj_dn = matmul(h, l0_w_in, n_cols=n_dn)
    w_rest = jnp.concatenate([l0_w_in[:, n_dn + n_gate:],
                              l0_w_in[:, n_dn:n_dn + n_gate],
                              jnp.zeros((d, 512 - n_gate), F32)], axis=1)
    proj_rest = matmul(h, w_rest)
    n_hg = sum(IN0_SIZES[4:])
    proj = jnp.concatenate([proj_dn, proj_rest[:, n_hg:n_hg + n_gate], proj_rest[:, :n_hg]], axis=1)
    yl, yc = _mixer0_core(proj[:n_lat].reshape(n_batch, lat_len, -1), proj[n_lat:].reshape(n_batch, ctx_len, -1),
                          0, l0_dn_conv, l0_dn_a_log, l0_dn_dt_bias, l0_dn_norm, hg_lb_logits, l0_hg_norm)
    y = jnp.concatenate([yl.reshape(n_lat, d), yc.reshape(n_ctx, d)], axis=0).astype(BF16)
    xs = matmul_residual(y, l0_w_out, xs, mod3, 2, geom)
    h, logits = adaln(xs, l0_norm2, mod3, 3, geom, router_w=l0_router_w)
    ff = moe_ffn(h, logits, l0_router_b, l0_exp_gate_up, l0_exp_down, l0_shared_gate_up, l0_shared_down)
    gate2 = jnp.concatenate([jnp.repeat(mod3[5:6 * n_batch:6, 0], lat_len, axis=0),
                             jnp.broadcast_to(mod3[6 * n_batch + 5], (n_ctx, d))], axis=0)
    xs = xs + gate2 * ff

    mod3 = modulation(cond, l1_mod_w, l1_mod_b).reshape(MOD_ROWS * 6, 1, d)
    h = adaln(xs, l1_norm1, mod3, 0, geom)
    proj = matmul(h, l1_w_in)
    cos_t, sin_t = rope_tables(lat_len, n_batch, n_ctx, DA_HEAD_DIM)
    qkv = qk_prep(proj, l1_q_norm, l1_k_norm, cos_t, sin_t)
    lam_init = 0.8 - 0.6 * math.exp(-0.3 * 1)
    lmbda = (jnp.exp(jnp.sum(l1_lambda[0] * l1_lambda[1])) - jnp.exp(jnp.sum(l1_lambda[2] * l1_lambda[3]))
             + lam_init)
    y = diff_attention(qkv, lmbda, l1_sub_norm, n_batch, lat_len, ctx_len, DA_HEADS, DA_HEAD_DIM,
                       1.0 - lam_init)
    xl = matmul_residual(y, l1_w_out, xs[:n_lat], mod3, 2, geom_lat_only)
    h, logits = adaln(xl, l1_norm2, mod3, 3, geom_lat_only, router_w=l1_router_w)
    ff = moe_ffn(h, logits, l1_router_b, l1_exp_gate_up, l1_exp_down, l1_shared_gate_up, l1_shared_down)
    gate2 = jnp.repeat(mod3[5:6 * n_batch:6, 0], lat_len, axis=0)
    xl = xl + gate2 * ff
    return xl.reshape(n_batch, lat_len, d)
```

```python
import functools
import math

import jax
import jax.numpy as jnp
import numpy as np
from jax import lax
from jax.experimental import pallas as pl
from jax.experimental.pallas import tpu as pltpu

F32 = jnp.float32
BF16 = jnp.bfloat16

NORM_EPS = 1e-6
GRID_W = 64
ROPE_BASE = 10000.0

DN_HEADS = 8
DN_HEAD_DIM = 128
DN_WIDTH = DN_HEADS * DN_HEAD_DIM
DN_CHUNK = 64
HG_HEADS = 8
HG_KEY_DIM = 128
HG_VAL_DIM = 128
HG_K_WIDTH = HG_HEADS * HG_KEY_DIM
HG_V_WIDTH = HG_HEADS * HG_VAL_DIM
HG_CHUNK = 64
IN0_SIZES = (3 * DN_WIDTH, DN_WIDTH, 2 * DN_HEADS, 2 * DN_HEADS,
             HG_K_WIDTH, 2 * HG_K_WIDTH, HG_V_WIDTH, HG_V_WIDTH)
DA_HEADS = 8
DA_HEAD_DIM = 128
N_EXPERTS = 64
TOP_K = 8
N_GROUPS = 8
TOPK_GROUPS = 4
EXPERT_FF = 512
ROUTED_SCALE = 2.5

VMEM_LIMIT_BYTES = 56 * 1024 * 1024
MOD_ROWS = 8
EXPERT_SLOT_BLOCK = 256


def _params(*sem):
    return pltpu.CompilerParams(dimension_semantics=sem, vmem_limit_bytes=VMEM_LIMIT_BYTES)


def _silu(x):
    return x * (1.0 / (1.0 + jnp.exp(-x)))


def _mod_body(c_ref, w_ref, b_ref, o_ref):
    a = _silu(c_ref[...]).astype(BF16)
    o_ref[...] = jnp.dot(a, w_ref[...].astype(BF16), preferred_element_type=F32) + b_ref[...]


def modulation(cond, w, b, tn=1024):
    m, k = cond.shape
    n = w.shape[1]
    return pl.pallas_call(
        _mod_body, grid=(n // tn,),
        in_specs=[pl.BlockSpec((m, k), lambda j: (0, 0)),
                  pl.BlockSpec((k, tn), lambda j: (0, j)),
                  pl.BlockSpec((1, tn), lambda j: (0, j))],
        out_specs=pl.BlockSpec((m, tn), lambda j: (0, j)),
        out_shape=jax.ShapeDtypeStruct((m, n), F32),
        compiler_params=_params("arbitrary"), name="modulation",
    )(cond, w, b.reshape(1, n))


def _mod_row(tile, tile_rows, n_lat, lat_len, n_batch):
    start = tile * tile_rows
    return jnp.where(start < n_lat, start // lat_len, n_batch)


def _adaln_body(x_ref, nw_ref, shift_ref, scale_ref, o_ref):
    x = x_ref[...]
    y = x * lax.rsqrt(jnp.mean(x * x, axis=-1, keepdims=True) + NORM_EPS) * nw_ref[...]
    o_ref[...] = (y * (1.0 + scale_ref[0]) + shift_ref[0]).astype(o_ref.dtype)


def _adaln_router_body(x_ref, nw_ref, shift_ref, scale_ref, rw_ref, o_ref, lg_ref):
    x = x_ref[...]
    y = x * lax.rsqrt(jnp.mean(x * x, axis=-1, keepdims=True) + NORM_EPS) * nw_ref[...]
    h = y * (1.0 + scale_ref[0]) + shift_ref[0]
    o_ref[...] = h.astype(o_ref.dtype)
    lg_ref[...] = lax.dot_general(rw_ref[...], h, (((1,), (1,)), ((), ())), preferred_element_type=F32,
                                  precision=lax.Precision.HIGHEST)


def adaln(x, norm_w, mod3, shift_idx, geom, router_w=None, tl=256):
    t, d = x.shape
    n_lat, lat_len, n_batch = geom
    row = functools.partial(_mod_row, tile_rows=tl, n_lat=n_lat, lat_len=lat_len, n_batch=n_batch)
    in_specs = [pl.BlockSpec((tl, d), lambda i: (i, 0)),
                pl.BlockSpec((1, d), lambda i: (0, 0)),
                pl.BlockSpec((1, 1, d), lambda i: (row(i) * 6 + shift_idx, 0, 0)),
                pl.BlockSpec((1, 1, d), lambda i: (row(i) * 6 + shift_idx + 1, 0, 0))]
    args = [x, norm_w.reshape(1, d), mod3, mod3]
    if router_w is None:
        return pl.pallas_call(
            _adaln_body, grid=(t // tl,), in_specs=in_specs,
            out_specs=pl.BlockSpec((tl, d), lambda i: (i, 0)),
            out_shape=jax.ShapeDtypeStruct((t, d), BF16),
            compiler_params=_params("arbitrary"), name="adaln")(*args)
    e = router_w.shape[1]
    return pl.pallas_call(
        _adaln_router_body, grid=(t // tl,),
        in_specs=in_specs + [pl.BlockSpec((e, d), lambda i: (0, 0))],
        out_specs=[pl.BlockSpec((tl, d), lambda i: (i, 0)), pl.BlockSpec((e, tl), lambda i: (0, i))],
        out_shape=[jax.ShapeDtypeStruct((t, d), BF16), jax.ShapeDtypeStruct((e, t), F32)],
        compiler_params=_params("arbitrary"), name="adaln_router")(*args, router_w.T)


def _mm_body(a_ref, w_ref, o_ref, wb_ref):
    @pl.when(pl.program_id(1) == 0)
    def _():
        wb_ref[...] = w_ref[...].astype(BF16)
    o_ref[...] = jnp.dot(a_ref[...], wb_ref[...], preferred_element_type=F32).astype(o_ref.dtype)


def _mm_res_body(a_ref, w_ref, res_ref, gate_ref, o_ref, wb_ref):
    @pl.when(pl.program_id(1) == 0)
    def _():
        wb_ref[...] = w_ref[...].astype(BF16)
    acc = jnp.dot(a_ref[...], wb_ref[...], preferred_element_type=F32)
    o_ref[...] = res_ref[...] + gate_ref[0] * acc


def matmul(a, w, n_cols=None, col_block0=0, tm=512, tn=512, out_dtype=F32):
    m, k = a.shape
    n = w.shape[1] if n_cols is None else n_cols
    return pl.pallas_call(
        _mm_body, grid=(n // tn, m // tm),
        in_specs=[pl.BlockSpec((tm, k), lambda j, i: (i, 0)),
                  pl.BlockSpec((k, tn), lambda j, i: (0, j + col_block0))],
        out_specs=pl.BlockSpec((tm, tn), lambda j, i: (i, j)),
        out_shape=jax.ShapeDtypeStruct((m, n), out_dtype),
        scratch_shapes=[pltpu.VMEM((k, tn), BF16)],
        compiler_params=_params("arbitrary", "arbitrary"), name="matmul",
    )(a, w)


def matmul_residual(a, w, res, mod3, gate_idx, geom, tm=512, tn=512):
    m, k = a.shape
    n = w.shape[1]
    n_lat, lat_len, n_batch = geom
    row = functools.partial(_mod_row, tile_rows=tm, n_lat=n_lat, lat_len=lat_len, n_batch=n_batch)
    nb = n // tn
    return pl.pallas_call(
        _mm_res_body, grid=(n // tn, m // tm),
        in_specs=[pl.BlockSpec((tm, k), lambda j, i: (i, 0)),
                  pl.BlockSpec((k, tn), lambda j, i: (0, j)),
                  pl.BlockSpec((tm, tn), lambda j, i: (i, j)),
                  pl.BlockSpec((1, 1, tn), lambda j, i: (row(i) * 6 + gate_idx, 0, j))],
        out_specs=pl.BlockSpec((tm, tn), lambda j, i: (i, j)),
        out_shape=jax.ShapeDtypeStruct((m, n), F32),
        scratch_shapes=[pltpu.VMEM((k, tn), BF16)],
        compiler_params=_params("arbitrary", "arbitrary"), name="matmul_residual",
    )(a, w, res, mod3)


def _expert_body(blk_e_ref, blk_new_ref, n_used_ref, x_ref, gu_ref, dn_ref, o_ref, gub_ref, dnb_ref):
    i = pl.program_id(0)

    @pl.when(blk_new_ref[i] == 1)
    def _():
        gub_ref[...] = gu_ref[0].astype(BF16)
        dnb_ref[...] = dn_ref[0].astype(BF16)

    @pl.when(i < n_used_ref[0])
    def _():
        ff = dnb_ref.shape[0]
        h1 = jnp.dot(x_ref[...], gub_ref[...], preferred_element_type=F32)
        act = (_silu(h1[:, :ff]) * h1[:, ff:]).astype(BF16)
        o_ref[...] = jnp.dot(act, dnb_ref[...], preferred_element_type=F32)

    @pl.when(i >= n_used_ref[0])
    def _():
        o_ref[...] = jnp.zeros_like(o_ref)


def expert_blocks(x, gate_up, down, blk_e, blk_new, n_used, blk):
    s, d = x.shape
    _, _, f2 = gate_up.shape
    n_blk = s // blk
    grid_spec = pltpu.PrefetchScalarGridSpec(
        num_scalar_prefetch=3, grid=(n_blk,),
        in_specs=[pl.BlockSpec((blk, d), lambda i, be, bn, nu: (i, 0)),
                  pl.BlockSpec((1, d, f2), lambda i, be, bn, nu: (be[i], 0, 0)),
                  pl.BlockSpec((1, f2 // 2, d), lambda i, be, bn, nu: (be[i], 0, 0))],
        out_specs=pl.BlockSpec((blk, d), lambda i, be, bn, nu: (i, 0)),
        scratch_shapes=[pltpu.VMEM((d, f2), BF16), pltpu.VMEM((f2 // 2, d), BF16)])
    return pl.pallas_call(
        _expert_body, grid_spec=grid_spec,
        out_shape=jax.ShapeDtypeStruct((s, d), F32),
        compiler_params=_params("arbitrary"), name="expert_blocks",
    )(blk_e, blk_new, n_used, x, gate_up, down)


def _qk_prep_body(x_ref, qw_ref, kw_ref, cos_ref, sin_ref, o_ref, *, n_qk_groups, head_dim):
    cos = cos_ref[...]
    sin = sin_ref[...]
    lane = lax.broadcasted_iota(jnp.int32, cos.shape, 1)
    first = (lane % (head_dim // 2)) < (head_dim // 4)
    for g in range(n_qk_groups):
        sl = slice(g * head_dim, (g + 1) * head_dim)
        x = x_ref[:, sl]
        w = qw_ref[...] if g < n_qk_groups // 2 else kw_ref[...]
        y = x * lax.rsqrt(jnp.mean(x * x, axis=-1, keepdims=True) + NORM_EPS) * w
        swapped = jnp.where(first, pltpu.roll(y, head_dim - head_dim // 4, 1), pltpu.roll(y, head_dim // 4, 1))
        o_ref[:, sl] = (y * cos + swapped * sin).astype(o_ref.dtype)
    rest = n_qk_groups * head_dim
    o_ref[:, rest:] = x_ref[:, rest:].astype(o_ref.dtype)


def qk_prep(proj, q_norm, k_norm, cos_t, sin_t, tl=256):
    t, n = proj.shape
    hd = q_norm.shape[0]
    n_groups = (2 * n // 3) // hd
    body = functools.partial(_qk_prep_body, n_qk_groups=n_groups, head_dim=hd)
    return pl.pallas_call(
        body, grid=(t // tl,),
        in_specs=[pl.BlockSpec((tl, n), lambda i: (i, 0)),
                  pl.BlockSpec((1, hd), lambda i: (0, 0)),
                  pl.BlockSpec((1, hd), lambda i: (0, 0)),
                  pl.BlockSpec((tl, hd), lambda i: (i, 0)),
                  pl.BlockSpec((tl, hd), lambda i: (i, 0))],
        out_specs=pl.BlockSpec((tl, n), lambda i: (i, 0)),
        out_shape=jax.ShapeDtypeStruct((t, n), BF16),
        compiler_params=_params("arbitrary"), name="qk_prep",
    )(proj, q_norm.reshape(1, hd), k_norm.reshape(1, hd), cos_t, sin_t)


def rope_tables(n_lat_tokens_per_sample, n_batch, n_ctx_tokens, head_dim):
    quarter = head_dim // 4
    inv_freq = ROPE_BASE ** (-jnp.arange(quarter, dtype=F32) / quarter)
    rows = n_lat_tokens_per_sample // GRID_W
    row = jnp.repeat(jnp.arange(rows, dtype=F32), GRID_W)
    col = jnp.tile(jnp.arange(GRID_W, dtype=F32), rows)
    ang_r = row[:, None] * inv_freq[None, :]
    ang_c = col[:, None] * inv_freq[None, :]
    cos = jnp.concatenate([jnp.cos(ang_r), jnp.cos(ang_r), jnp.cos(ang_c), jnp.cos(ang_c)], axis=-1)
    sin = jnp.concatenate([-jnp.sin(ang_r), jnp.sin(ang_r), -jnp.sin(ang_c), jnp.sin(ang_c)], axis=-1)
    cos = jnp.concatenate([jnp.tile(cos, (n_batch, 1)), jnp.ones((n_ctx_tokens, head_dim), F32)], axis=0)
    sin = jnp.concatenate([jnp.tile(sin, (n_batch, 1)), jnp.zeros((n_ctx_tokens, head_dim), F32)], axis=0)
    return cos, sin


def _diff_attn_body(lam_ref, q_ref, kl_ref, kc_ref, vl_ref, vc_ref, sw_ref, o_ref, *, head_dim, out_scale):
    lam = lam_ref[0]
    scale = head_dim ** -0.5
    contract_last = (((1,), (1,)), ((), ()))
    p_l = p_c = None
    for s in range(2):
        sl = slice(s * head_dim, (s + 1) * head_dim)
        qs = q_ref[:, sl]
        s_l = lax.dot_general(qs, kl_ref[:, sl], contract_last, preferred_element_type=F32) * scale
        s_c = lax.dot_general(qs, kc_ref[:, sl], contract_last, preferred_element_type=F32) * scale
        m = jnp.maximum(jnp.max(s_l, axis=-1, keepdims=True), jnp.max(s_c, axis=-1, keepdims=True))
        e_l = jnp.exp(s_l - m)
        e_c = jnp.exp(s_c - m)
        inv = 1.0 / (jnp.sum(e_l, axis=-1, keepdims=True) + jnp.sum(e_c, axis=-1, keepdims=True))
        if s == 0:
            p_l, p_c = e_l * inv, e_c * inv
        else:
            p_l, p_c = p_l - (lam * inv) * e_l, p_c - (lam * inv) * e_c
    o = (jnp.dot(p_l.astype(BF16), vl_ref[...], preferred_element_type=F32)
         + jnp.dot(p_c.astype(BF16), vc_ref[...], preferred_element_type=F32))
    y = o * lax.rsqrt(jnp.mean(o * o, axis=-1, keepdims=True) + NORM_EPS) * sw_ref[...]
    o_ref[...] = (y * out_scale).astype(o_ref.dtype)


def diff_attention(qkv, lmbda, sub_norm, n_batch, lat_len, ctx_len, n_heads, head_dim, out_scale, tq=256):
    hw = 2 * head_dim
    nq = lat_len // tq
    ctx_blk0 = n_batch * lat_len // ctx_len
    body = functools.partial(_diff_attn_body, head_dim=head_dim, out_scale=out_scale)
    return pl.pallas_call(
        body, grid=(n_batch, n_heads, nq),
        in_specs=[pl.BlockSpec(memory_space=pltpu.SMEM),
                  pl.BlockSpec((tq, hw), lambda b, h, i: (b * nq + i, h)),
                  pl.BlockSpec((lat_len, hw), lambda b, h, i: (b, n_heads + h)),
                  pl.BlockSpec((ctx_len, hw), lambda b, h, i: (ctx_blk0 + b, n_heads + h)),
                  pl.BlockSpec((lat_len, hw), lambda b, h, i: (b, 2 * n_heads + h)),
                  pl.BlockSpec((ctx_len, hw), lambda b, h, i: (ctx_blk0 + b, 2 * n_heads + h)),
                  pl.BlockSpec((1, hw), lambda b, h, i: (0, 0))],
        out_specs=pl.BlockSpec((tq, hw), lambda b, h, i: (b * nq + i, h)),
        out_shape=jax.ShapeDtypeStruct((n_batch * lat_len, n_heads * hw), BF16),
        compiler_params=_params("arbitrary", "arbitrary", "arbitrary"), name="diff_attention",
    )(lmbda.reshape(1), qkv, qkv, qkv, qkv, qkv, sub_norm.reshape(1, hw))


SCAN_TILE = 256
CHUNK = 64
SUB = 16
NEG_BIG = -1e30
NEG_INF = float("-inf")
DELTA_INV_PASSES = 1


def _sigmoid(x):
    return 1.0 / (1.0 + jnp.exp(-x))


def _dot(a, b):
    return jnp.dot(a.astype(BF16), b.astype(BF16), preferred_element_type=F32)


def _dot_nt(a, b):
    return lax.dot_general(a.astype(BF16), b.astype(BF16), (((1,), (1,)), ((), ())),
                           preferred_element_type=F32)


def _split3(x):
    hi = x.astype(BF16)
    r = x - hi.astype(F32)
    mid = r.astype(BF16)
    lo = (r - mid.astype(F32)).astype(BF16)
    return hi, mid, lo


def _dot_exact_lhs01(m01, x):
    hi, mid, lo = _split3(x)
    m = m01.astype(BF16)
    return (jnp.dot(m, hi, preferred_element_type=F32) + jnp.dot(m, mid, preferred_element_type=F32)
            + jnp.dot(m, lo, preferred_element_type=F32))


def _dot_exact_rhs01(x, m01):
    hi, mid, lo = _split3(x)
    m = m01.astype(BF16)
    return (jnp.dot(hi, m, preferred_element_type=F32) + jnp.dot(mid, m, preferred_element_type=F32)
            + jnp.dot(lo, m, preferred_element_type=F32))


def _dot3(a, b):
    ah = a.astype(BF16)
    al = (a - ah.astype(F32)).astype(BF16)
    bh = b.astype(BF16)
    bl = (b - bh.astype(F32)).astype(BF16)
    return (jnp.dot(ah, bh, preferred_element_type=F32) + jnp.dot(ah, bl, preferred_element_type=F32)
            + jnp.dot(al, bh, preferred_element_type=F32))


def _tile_masks(n, reverse):
    i = lax.broadcasted_iota(jnp.int32, (n, n), 0)
    j = lax.broadcasted_iota(jnp.int32, (n, n), 1)
    same = (i // CHUNK) == (j // CHUNK)
    if reverse:
        return same & (i <= j), same & (i < j)
    return same & (i >= j), same & (i > j)


def _segment_tile(b, s, reverse, nl, nc, ctx_tile0):
    if reverse:
        return jnp.where(s < nc, ctx_tile0 + b * nc + (nc - 1 - s), b * nl + (nl - 1 - (s - nc)))
    return jnp.where(s < nc, ctx_tile0 + b * nc + s, b * nl + (s - nc))


def _dn_prep_body(x_ref, prev_ref, next_ref, w_ref, o_ref, *, tiles_per_lat_seg, tiles_per_ctx_seg, n_lat_tiles,
                  head_dim, q_scale):
    i = pl.program_id(0)
    j = pl.program_id(1)
    is_lat = i < n_lat_tiles
    pos = jnp.where(is_lat, i % tiles_per_lat_seg, (i - n_lat_tiles) % tiles_per_ctx_seg)
    seg_first = pos == 0
    seg_last = pos == jnp.where(is_lat, tiles_per_lat_seg, tiles_per_ctx_seg) - 1
    x = x_ref[...]
    tl = x.shape[0]
    prev = jnp.where(seg_first, 0.0, prev_ref[...])
    nxt = jnp.where(seg_last, 0.0, next_ref[...])
    xp = jnp.concatenate([prev, x, nxt], axis=0)
    w = w_ref[...]
    n_taps = 5
    acc = None
    for t in range(n_taps):
        off = 8 + t - n_taps // 2
        term = xp[off:off + tl] * w[t:t + 1]
        acc = term if acc is None else acc + term
    y = _silu(acc)
    scale = jnp.where(j == 0, q_scale, 1.0)
    outs = []
    for h in range(y.shape[1] // head_dim):
        yh = y[:, h * head_dim:(h + 1) * head_dim]
        nrm = lax.rsqrt(jnp.sum(yh * yh, axis=-1, keepdims=True) + 1e-6) * scale
        outs.append(yh * jnp.where(j == 2, 1.0, nrm))
    o_ref[...] = jnp.concatenate(outs, axis=1)


def dn_prep(proj_dn, conv_w_t, geom, ctx_len, width, head_dim):
    n_lat, lat_len, n_batch = geom
    t = proj_dn.shape[0]
    tl = SCAN_TILE
    rows8 = tl // 8
    n_tiles = t // tl
    body = functools.partial(_dn_prep_body, tiles_per_lat_seg=lat_len // tl, tiles_per_ctx_seg=ctx_len // tl,
                             n_lat_tiles=n_lat // tl,
                             head_dim=head_dim, q_scale=head_dim ** -0.5)
    last8 = t // 8 - 1
    return pl.pallas_call(
        body, grid=(n_tiles, 3),
        in_specs=[pl.BlockSpec((tl, width), lambda i, j: (i, j)),
                  pl.BlockSpec((8, width), lambda i, j: (jnp.maximum(i * rows8 - 1, 0), j)),
                  pl.BlockSpec((8, width), lambda i, j: (jnp.minimum((i + 1) * rows8, last8), j)),
                  pl.BlockSpec((8, width), lambda i, j: (0, j))],
        out_specs=pl.BlockSpec((tl, width), lambda i, j: (i, j)),
        out_shape=jax.ShapeDtypeStruct((t, 3 * width), F32),
        compiler_params=_params("arbitrary", "arbitrary"), name="dn_prep",
    )(proj_dn, proj_dn, proj_dn, conv_w_t)


def _softplus(x):
    return jnp.maximum(x, 0.0) + jnp.log(1.0 + jnp.exp(-jnp.abs(x)))


def _gate_prep_body(h_ref, wc_ref, wr_ref, alog_c_ref, dtb_c_ref, alog_r_ref, dtb_r_ref, gc_ref, gr_ref, *, n_heads):
    h = h_ref[...]
    tl = h.shape[0]
    nd = 2 * n_heads
    raw_c = jnp.dot(h, wc_ref[...].astype(BF16), preferred_element_type=F32)
    raw_r = lax.dot_general(wr_ref[...].astype(BF16), h, (((1,), (1,)), ((), ())),
                            preferred_element_type=F32)
    g_c = -jnp.exp(alog_c_ref[...]) * _softplus(raw_c[:, :nd] + dtb_c_ref[...])
    g_r = -jnp.exp(alog_r_ref[...]) * _softplus(raw_r[:nd, :] + dtb_r_ref[...])
    incl_f, _ = _tile_masks(tl, False)
    incl_b, _ = _tile_masks(tl, True)
    one_f = jnp.where(incl_f, 1.0, 0.0)
    one_b = jnp.where(incl_b, 1.0, 0.0)
    cum_c = jnp.concatenate([_dot_exact_lhs01(one_f, g_c[:, :n_heads]),
                             _dot_exact_lhs01(one_b, g_c[:, n_heads:])], axis=1)
    cum_r = jnp.concatenate([_dot_exact_rhs01(g_r[:n_heads, :], one_b),
                             _dot_exact_rhs01(g_r[n_heads:, :], one_f)], axis=0)
    beta_c = _sigmoid(raw_c[:, nd:])
    gc_ref[...] = jnp.concatenate([cum_c, beta_c], axis=1)
    gr_ref[...] = jnp.concatenate([cum_r, jnp.zeros_like(cum_r)], axis=0)


def gate_prep(h_bf, w_gate, a_log, dt_bias, n_heads):
    t, d = h_bf.shape
    tl = SCAN_TILE
    nd = 2 * n_heads
    body = functools.partial(_gate_prep_body, n_heads=n_heads)
    full = lambda shape: pl.BlockSpec(shape, lambda i: (0, 0))
    return pl.pallas_call(
        body, grid=(t // tl,),
        in_specs=[pl.BlockSpec((tl, d), lambda i: (i, 0)), full((d, 2 * nd)), full((2 * nd, d)),
                  full((1, nd)), full((1, nd)), full((nd, 1)), full((nd, 1))],
        out_specs=[pl.BlockSpec((tl, 2 * nd), lambda i: (i, 0)), pl.BlockSpec((2 * nd, tl), lambda i: (0, i))],
        out_shape=[jax.ShapeDtypeStruct((t, 2 * nd), F32), jax.ShapeDtypeStruct((2 * nd, t), F32)],
        compiler_params=_params("arbitrary"), name="gate_prep",
    )(h_bf, w_gate, w_gate.T, a_log.reshape(1, nd), dt_bias.reshape(1, nd),
      a_log.reshape(nd, 1), dt_bias.reshape(nd, 1))


def _select_col(x, idx):
    lane = lax.broadcasted_iota(jnp.int32, x.shape, 1)
    return jnp.sum(jnp.where(lane == idx, x, 0.0), axis=1, keepdims=True)


def _select_row(x, idx):
    row = lax.broadcasted_iota(jnp.int32, x.shape, 0)
    return jnp.sum(jnp.where(row == idx, x, 0.0), axis=0, keepdims=True)


def _delta_tile(q, k, v, gc_col, gc_row, beta_col, s_ref, reverse, inv_passes):
    tl, kd_ = k.shape
    n_chunks = tl // CHUNK
    incl, strict = _tile_masks(tl, reverse)
    decay = jnp.exp(jnp.where(incl, gc_col - gc_row, NEG_BIG))
    kb, qb, vb = k.astype(BF16), q.astype(BF16), v.astype(BF16)
    kkt = _dot_nt(kb, kb)
    qkt = _dot_nt(qb, kb)
    x = jnp.where(strict, kkt * (-beta_col) * decay, 0.0)
    dot_inv = _dot3 if inv_passes == 3 else _dot
    ri = lax.broadcasted_iota(jnp.int32, (tl, tl), 0)
    ci = lax.broadcasted_iota(jnp.int32, (tl, tl), 1)
    r = jnp.where(ri == ci, 1.0, 0.0) + x
    n_sq = int(math.log2(CHUNK)) - 1
    for _ in range(n_sq):
        x = dot_inv(x, x)
        r = r + dot_inv(r, x)
    e_g = jnp.exp(gc_col)
    rhs = jnp.concatenate([v * beta_col, k * (beta_col * e_g)], axis=1)
    sol = dot_inv(r, rhs)
    u0, w = sol[:, :v.shape[1]], sol[:, v.shape[1]:]
    attn = (qkt * decay).astype(BF16)
    o0 = _dot(attn, u0)
    qe = q * e_g - _dot(attn, w)
    tot_rows = []
    for c in range(n_chunks):
        last = c * CHUNK if reverse else c * CHUNK + CHUNK - 1
        tot_rows.append(jnp.broadcast_to(gc_col[last:last + 1, :], (CHUNK, 1)))
    tot = jnp.concatenate(tot_rows, axis=0)
    kdec_t = jnp.transpose(k * jnp.exp(tot - gc_col))
    wu = jnp.concatenate([-w, u0], axis=1).astype(BF16)
    lane = lax.broadcasted_iota(jnp.int32, kdec_t.shape, 1)
    order = range(n_chunks - 1, -1, -1) if reverse else range(n_chunks)
    outs = [None] * n_chunks
    for c in order:
        rows = slice(c * CHUNK, (c + 1) * CHUNK)
        pn = _dot(jnp.where((lane // CHUNK) == c, kdec_t, 0.0), wu)
        lhs = jnp.concatenate([qe[rows], pn[:, :kd_]], axis=0)
        s = s_ref[...]
        res = _dot(lhs, s)
        outs[c] = o0[rows] + res[:CHUNK]
        last = c * CHUNK if reverse else c * CHUNK + CHUNK - 1
        gl = jnp.exp(gc_col[last:last + 1, :])
        s_ref[...] = gl * s + res[CHUNK:] + pn[:, kd_:]
    return jnp.concatenate(outs, axis=0)


def _delta_body(qf_ref, kf_ref, vf_ref, gcf_ref, grf_ref, qb_ref, kb_ref, vb_ref, gcb_ref, grb_ref,
                of_ref, ob_ref, sf_ref, sb_ref, *, n_heads, inv_passes):
    h = pl.program_id(1)

    @pl.when(pl.program_id(2) == 0)
    def _():
        sf_ref[...] = jnp.zeros_like(sf_ref)
        sb_ref[...] = jnp.zeros_like(sb_ref)

    for reverse, (q_ref, k_ref, v_ref, gc_ref, gr_ref, o_ref, s_ref) in enumerate(
            [(qf_ref, kf_ref, vf_ref, gcf_ref, grf_ref, of_ref, sf_ref),
             (qb_ref, kb_ref, vb_ref, gcb_ref, grb_ref, ob_ref, sb_ref)]):
        idx = reverse * n_heads + h
        gcs = gc_ref[...]
        gc_col = _select_col(gcs, idx)
        beta_col = _select_col(gcs, 2 * n_heads + idx)
        gc_row = _select_row(gr_ref[...], idx)
        o_ref[...] = _delta_tile(q_ref[...], k_ref[...], v_ref[...], gc_col, gc_row, beta_col, s_ref,
                                 bool(reverse), inv_passes)


def delta_scan(qkv, gc, gr, geom, ctx_len, n_heads, head_dim, inv_passes=3):
    n_lat, lat_len, n_batch = geom
    t = qkv.shape[0]
    tl = SCAN_TILE
    nl, nc, ctx0 = lat_len // tl, ctx_len // tl, n_lat // tl
    tile = functools.partial(_segment_tile, nl=nl, nc=nc, ctx_tile0=ctx0)
    specs = []
    for reverse in (False, True):
        tix = functools.partial(tile, reverse=reverse)
        specs += [pl.BlockSpec((tl, head_dim), lambda b, h, s, tix=tix: (tix(b, s), h)),
                  pl.BlockSpec((tl, head_dim), lambda b, h, s, tix=tix: (tix(b, s), n_heads + h)),
                  pl.BlockSpec((tl, head_dim), lambda b, h, s, tix=tix: (tix(b, s), 2 * n_heads + h)),
                  pl.BlockSpec((tl, 4 * n_heads), lambda b, h, s, tix=tix: (tix(b, s), 0)),
                  pl.BlockSpec((4 * n_heads, tl), lambda b, h, s, tix=tix: (0, tix(b, s)))]
    out_specs = [pl.BlockSpec((tl, head_dim), lambda b, h, s, tix=functools.partial(tile, reverse=r): (tix(b, s), h))
                 for r in (False, True)]
    body = functools.partial(_delta_body, n_heads=n_heads, inv_passes=inv_passes)
    return pl.pallas_call(
        body, grid=(n_batch, n_heads, nl + nc), in_specs=specs, out_specs=out_specs,
        out_shape=[jax.ShapeDtypeStruct((t, n_heads * head_dim), F32)] * 2,
        scratch_shapes=[pltpu.VMEM((head_dim, head_dim), F32)] * 2,
        compiler_params=_params("arbitrary", "arbitrary", "arbitrary"), name="delta_scan",
    )(qkv, qkv, qkv, gc, gr, qkv, qkv, qkv, gc, gr)


def _hgrn_tile(hq, hf, hv, lb, st_ref, reverse):
    tl, kd_ = hq.shape
    n_chunks = tl // CHUNK
    n_sub = tl // SUB
    sub_per_chunk = CHUNK // SUB
    q = _silu(hq)
    f = lb + (1.0 - lb) * _sigmoid(hf)
    k = 1.0 - f
    lf = jnp.log(f)
    incl, _ = _tile_masks(tl, reverse)
    cum = _dot_exact_lhs01(jnp.where(incl, 1.0, 0.0), lf)
    excl = cum - lf

    def bcast_rows(src, row, n):
        return jnp.broadcast_to(src[row:row + 1, :], (n, kd_))

    chunk_last = [(c * CHUNK if reverse else c * CHUNK + CHUNK - 1) for c in range(n_chunks)]
    tot = jnp.concatenate([bcast_rows(cum, chunk_last[c], CHUNK) for c in range(n_chunks)], axis=0)
    sub_first = [(m * SUB + SUB - 1 if reverse else m * SUB) for m in range(n_sub)]
    r_sub = jnp.concatenate([bcast_rows(excl, sub_first[m], SUB) for m in range(n_sub)], axis=0)
    q_t = q * jnp.exp(cum - r_sub)
    qd = q * jnp.exp(cum)
    kd = k * jnp.exp(tot - cum)
    vb = hv.astype(BF16)

    i = lax.broadcasted_iota(jnp.int32, (tl, tl), 0)
    j = lax.broadcasted_iota(jnp.int32, (tl, tl), 1)
    same = (i // CHUNK) == (j // CHUNK)
    pos_i = (i % CHUNK) // SUB
    pos_j = (j % CHUNK) // SUB
    if reverse:
        pos_i, pos_j = sub_per_chunk - 1 - pos_i, sub_per_chunk - 1 - pos_j
    a_off = jnp.zeros((tl, tl), F32)
    for lvl in range(1, sub_per_chunk):
        ref_rows = []
        for c in range(n_chunks):
            m = c * sub_per_chunk + (sub_per_chunk - 1 - lvl if reverse else lvl)
            ref_rows.append(bcast_rows(excl, sub_first[m], CHUNK))
        r_lvl = jnp.concatenate(ref_rows, axis=0)
        k_t = k * jnp.exp(jnp.minimum(r_lvl - cum, 0.0))
        a_l = _dot_nt(q_t, k_t)
        a_off = a_off + jnp.where(same & (pos_i == lvl) & (pos_j < lvl), a_l, 0.0)
    o_intra = _dot(a_off, vb)

    ones = jnp.ones((kd_, kd_), BF16)
    ii = lax.broadcasted_iota(jnp.int32, (SUB, SUB, kd_), 0)
    jj = lax.broadcasted_iota(jnp.int32, (SUB, SUB, kd_), 1)
    tri = (ii <= jj) if reverse else (ii >= jj)
    diag_rows = []
    for m in range(n_sub):
        rows = slice(m * SUB, (m + 1) * SUB)
        cm, qm, km, vm = cum[rows], q[rows], k[rows], hv[rows]
        e = jnp.exp(jnp.minimum(cm[:, None, :] - cm[None, :, :], 0.0))
        tmp = jnp.where(tri, (qm[:, None, :] * km[None, :, :]) * e, 0.0)
        a_rep = jnp.dot(tmp.reshape(SUB * SUB, kd_).astype(BF16), ones, preferred_element_type=F32)
        diag_rows.append(jnp.sum(a_rep.reshape(SUB, SUB, kd_) * vm[None, :, :], axis=1))
    o_intra = o_intra + jnp.concatenate(diag_rows, axis=0)

    v_t = jnp.transpose(hv)
    lane = lax.broadcasted_iota(jnp.int32, v_t.shape, 1)
    kdb = kd.astype(BF16)
    order = range(n_chunks - 1, -1, -1) if reverse else range(n_chunks)
    outs = [None] * n_chunks
    for c in order:
        rows = slice(c * CHUNK, (c + 1) * CHUNK)
        st = st_ref[...]
        outs[c] = o_intra[rows] + _dot_nt(qd[rows], st)
        n_t = _dot(jnp.where((lane // CHUNK) == c, v_t, 0.0), kdb)
        st_ref[...] = st * jnp.exp(cum[chunk_last[c]:chunk_last[c] + 1, :]) + n_t
    return jnp.concatenate(outs, axis=0)


def _hgrn_body(qf_ref, ff_ref, vf_ref, qb_ref, fb_ref, vb_ref, lb_ref, of_ref, ob_ref, sf_ref, sb_ref):
    @pl.when(pl.program_id(2) == 0)
    def _():
        sf_ref[...] = jnp.zeros_like(sf_ref)
        sb_ref[...] = jnp.zeros_like(sb_ref)

    lb = lb_ref[...]
    of_ref[...] = _hgrn_tile(qf_ref[...], ff_ref[...], vf_ref[...], lb, sf_ref, False)
    ob_ref[...] = _hgrn_tile(qb_ref[...], fb_ref[...], vb_ref[...], lb, sb_ref, True)


def hgrn_scan(proj_hg, lb, geom, ctx_len, n_heads, key_dim):
    n_lat, lat_len, n_batch = geom
    t = proj_hg.shape[0]
    tl = SCAN_TILE
    nl, nc, ctx0 = lat_len // tl, ctx_len // tl, n_lat // tl
    tile = functools.partial(_segment_tile, nl=nl, nc=nc, ctx_tile0=ctx0)
    specs = []
    for reverse in (False, True):
        tix = functools.partial(tile, reverse=reverse)
        fcol = (1 + int(reverse)) * n_heads
        specs += [pl.BlockSpec((tl, key_dim), lambda b, h, s, tix=tix: (tix(b, s), h)),
                  pl.BlockSpec((tl, key_dim), lambda b, h, s, tix=tix, fcol=fcol: (tix(b, s), fcol + h)),
                  pl.BlockSpec((tl, key_dim), lambda b, h, s, tix=tix: (tix(b, s), 3 * n_heads + h))]
    specs.append(pl.BlockSpec((1, key_dim), lambda b, h, s: (0, h)))
    out_specs = [pl.BlockSpec((tl, key_dim), lambda b, h, s, tix=functools.partial(tile, reverse=r): (tix(b, s), h))
                 for r in (False, True)]
    return pl.pallas_call(
        _hgrn_body, grid=(n_batch, n_heads, nl + nc), in_specs=specs, out_specs=out_specs,
        out_shape=[jax.ShapeDtypeStruct((t, n_heads * key_dim), F32)] * 2,
        scratch_shapes=[pltpu.VMEM((key_dim, key_dim), F32)] * 2,
        compiler_params=_params("arbitrary", "arbitrary", "arbitrary"), name="hgrn_scan",
    )(proj_hg, proj_hg, proj_hg, proj_hg, proj_hg, proj_hg, lb)


def _mix_out2_body(df_ref, db_ref, hf_ref, hb_ref, z_ref, og_ref, dnw_ref, hgw_ref, o_ref, *, head_dim):
    def normed(o, nw):
        outs = []
        for h in range(o.shape[1] // head_dim):
            oh = o[:, h * head_dim:(h + 1) * head_dim]
            outs.append(oh * lax.rsqrt(jnp.mean(oh * oh, axis=-1, keepdims=True) + NORM_EPS) * nw)
        return jnp.concatenate(outs, axis=1)

    dn = normed(df_ref[...] + db_ref[...], dnw_ref[...]) * _silu(z_ref[...])
    hg = normed(hf_ref[...] + hb_ref[...], hgw_ref[...]) * _sigmoid(og_ref[...])
    half = dn.shape[1]
    o_ref[:, :half] = dn.astype(o_ref.dtype)
    o_ref[:, half:] = hg.astype(o_ref.dtype)


def mix_out(dn_f, dn_b, hg_f, hg_b, z_src, z_blk, og_src, og_blk, dn_norm, hg_norm, head_dim):
    t, w = dn_f.shape
    tl = SCAN_TILE
    row = lambda i: (i, 0)
    body = functools.partial(_mix_out2_body, head_dim=head_dim)
    return pl.pallas_call(
        body, grid=(t // tl,),
        in_specs=[pl.BlockSpec((tl, w), row)] * 4
        + [pl.BlockSpec((tl, w), lambda i: (i, z_blk)), pl.BlockSpec((tl, w), lambda i: (i, og_blk)),
           pl.BlockSpec((1, head_dim), lambda i: (0, 0)), pl.BlockSpec((1, head_dim), lambda i: (0, 0))],
        out_specs=pl.BlockSpec((tl, 2 * w), row),
        out_shape=jax.ShapeDtypeStruct((t, 2 * w), BF16),
        compiler_params=_params("arbitrary"), name="mix_out",
    )(dn_f, dn_b, hg_f, hg_b, z_src, og_src, dn_norm.reshape(1, head_dim), hg_norm.reshape(1, head_dim))


def _first_max(x, ids, n):
    m = jnp.max(x, axis=0, keepdims=True)
    first = jnp.min(jnp.where(x == m, ids, n), axis=0, keepdims=True)
    return m, first


def _route_body(lg_ref, bias_ref, idx_ref, rank_ref, w_ref, cnt_ref, carry_ref, *,
                n_groups, topk_groups, top_k, scale):
    i = pl.program_id(0)

    @pl.when(i == 0)
    def _():
        carry_ref[...] = jnp.zeros_like(carry_ref)

    lg = lg_ref[...]
    n_exp, tl = lg.shape
    per = n_exp // n_groups
    scores = 1.0 / (1.0 + jnp.exp(-lg))
    biased = scores + bias_ref[...]
    sub = lax.broadcasted_iota(jnp.int32, (per, tl), 0)
    g_rows = []
    for g in range(n_groups):
        xg = biased[g * per:(g + 1) * per]
        m1, i1 = _first_max(xg, sub, per)
        m2 = jnp.max(jnp.where(sub == i1, NEG_INF, xg), axis=0, keepdims=True)
        g_rows.append(m1 + m2)
    gscore = jnp.concatenate(g_rows, axis=0)
    gid = lax.broadcasted_iota(jnp.int32, (n_groups, tl), 0)
    gsel = jnp.zeros((n_groups, tl), jnp.bool_)
    for _ in range(topk_groups):
        _, first = _first_max(gscore, gid, n_groups)
        hit = gid == first
        gsel = gsel | hit
        gscore = jnp.where(hit, NEG_INF, gscore)
    eid = lax.broadcasted_iota(jnp.int32, (n_exp, tl), 0)
    gmask = jnp.concatenate([jnp.broadcast_to(gsel[g:g + 1], (per, tl)) for g in range(n_groups)], axis=0)
    masked = jnp.where(gmask, biased, NEG_INF)
    sel = jnp.zeros((n_exp, tl), jnp.bool_)
    hits, firsts = [], []
    for _ in range(top_k):
        _, first = _first_max(masked, eid, n_exp)
        hit = eid == first
        hits.append(hit)
        firsts.append(first)
        sel = sel | hit
        masked = jnp.where(hit, NEG_INF, masked)
    self = jnp.where(sel, 1.0, 0.0)
    ti = lax.broadcasted_iota(jnp.int32, (tl, tl), 0)
    tj = lax.broadcasted_iota(jnp.int32, (tl, tl), 1)
    before = jnp.where(ti < tj, 1.0, 0.0).astype(BF16)
    carry = carry_ref[...]
    rank_full = jnp.dot(self.astype(BF16), before, preferred_element_type=F32) + carry
    w_rows = [jnp.sum(jnp.where(hit, scores, 0.0), axis=0, keepdims=True) for hit in hits]
    r_rows = [jnp.sum(jnp.where(hit, rank_full, 0.0), axis=0, keepdims=True) for hit in hits]
    w8 = jnp.concatenate(w_rows, axis=0)
    idx_ref[...] = jnp.concatenate(firsts, axis=0)
    rank_ref[...] = jnp.concatenate(r_rows, axis=0).astype(jnp.int32)
    w_ref[...] = w8 / jnp.sum(w8, axis=0, keepdims=True) * scale
    carry = carry + jnp.sum(self, axis=1, keepdims=True)
    carry_ref[...] = carry
    cnt_ref[...] = carry.astype(jnp.int32)


def route(logits_t, router_b, tl=256):
    n_exp, t = logits_t.shape
    body = functools.partial(_route_body, n_groups=N_GROUPS, topk_groups=TOPK_GROUPS, top_k=TOP_K,
                             scale=ROUTED_SCALE)
    tok_spec = pl.BlockSpec((TOP_K, tl), lambda i: (0, i))
    return pl.pallas_call(
        body, grid=(t // tl,),
        in_specs=[pl.BlockSpec((n_exp, tl), lambda i: (0, i)), pl.BlockSpec((n_exp, 1), lambda i: (0, 0))],
        out_specs=[tok_spec, tok_spec, tok_spec, pl.BlockSpec((n_exp, 1), lambda i: (0, 0))],
        out_shape=[jax.ShapeDtypeStruct((TOP_K, t), jnp.int32), jax.ShapeDtypeStruct((TOP_K, t), jnp.int32),
                   jax.ShapeDtypeStruct((TOP_K, t), F32), jax.ShapeDtypeStruct((n_exp, 1), jnp.int32)],
        scratch_shapes=[pltpu.VMEM((n_exp, 1), F32)],
        compiler_params=_params("arbitrary"), name="route",
    )(logits_t, router_b.reshape(n_exp, 1))


def _combine_body(g_ref, w_ref, sh_ref, x_ref, gate_ref, o_ref, *, top_k):
    d = x_ref.shape[1]
    acc = sh_ref[...]
    w = w_ref[...]
    for k in range(top_k):
        acc = acc + g_ref[:, k * d:(k + 1) * d] * w[:, k:k + 1]
    o_ref[...] = x_ref[...] + gate_ref[0] * acc


def combine(gathered, w, shared, x, mod3, gate_idx, geom, tl=128):
    t, d = x.shape
    top_k = w.shape[1]
    n_lat, lat_len, n_batch = geom
    row = functools.partial(_mod_row, tile_rows=tl, n_lat=n_lat, lat_len=lat_len, n_batch=n_batch)
    return pl.pallas_call(
        functools.partial(_combine_body, top_k=top_k), grid=(t // tl,),
        in_specs=[pl.BlockSpec((tl, top_k * d), lambda i: (i, 0)),
                  pl.BlockSpec((tl, top_k), lambda i: (i, 0)),
                  pl.BlockSpec((tl, d), lambda i: (i, 0)),
                  pl.BlockSpec((tl, d), lambda i: (i, 0)),
                  pl.BlockSpec((1, 1, d), lambda i: (row(i) * 6 + gate_idx, 0, 0))],
        out_specs=pl.BlockSpec((tl, d), lambda i: (i, 0)),
        out_shape=jax.ShapeDtypeStruct((t, d), F32),
        compiler_params=_params("arbitrary"), name="combine",
    )(gathered, w, shared, x, mod3)


def moe_ffn(x, h_bf, logits_t, router_b, exp_gate_up, exp_down, shared_gate_up, shared_down, mod3, geom):
    t, d = h_bf.shape
    e_count = exp_gate_up.shape[0]
    blk = EXPERT_SLOT_BLOCK
    idx8, rank8, w8, counts = route(logits_t, router_b)
    counts = counts[:, 0]
    padded = (counts + blk - 1) // blk * blk
    pad_end = jnp.cumsum(padded)
    pad_start = pad_end - padded
    dest = (pad_start[idx8] + rank8).T
    n_blk = (t * TOP_K + e_count * (blk - 1)) // blk + 1
    n_slot = n_blk * blk
    tok = jnp.broadcast_to(jnp.arange(t, dtype=jnp.int32)[:, None], (t, TOP_K))
    slot_tok = jnp.full((n_slot,), t, jnp.int32).at[dest.reshape(-1)].set(tok.reshape(-1))
    blk_starts = jnp.arange(n_blk, dtype=jnp.int32) * blk
    blk_e = jnp.minimum(jnp.searchsorted(pad_end, blk_starts, side="right"), e_count - 1).astype(jnp.int32)
    blk_new = jnp.concatenate([jnp.ones((1,), jnp.int32), (blk_e[1:] != blk_e[:-1]).astype(jnp.int32)])
    n_used = (pad_end[-1] // blk).astype(jnp.int32).reshape(1)
    h_pad = jnp.concatenate([h_bf, jnp.zeros((1, d), h_bf.dtype)], axis=0)
    x_sorted = h_pad[slot_tok]
    y_slot = expert_blocks(x_sorted, exp_gate_up, exp_down, blk_e, blk_new, n_used, blk)
    gathered = y_slot[dest.reshape(-1)].reshape(t, TOP_K * d)
    one = jnp.ones((t // blk,), jnp.int32)
    shared = expert_blocks(h_bf, shared_gate_up[None], shared_down[None],
                           jnp.zeros((t // blk,), jnp.int32), one.at[1:].set(0),
                           jnp.full((1,), t // blk, jnp.int32), blk)
    return combine(gathered, w8.T, shared, x, mod3, 5, geom)


def kernel(x, c, ctx, c_ctx, hg_lb_logits, l0_mod_w, l0_mod_b, l0_norm1, l0_norm2, l0_w_in, l0_dn_conv, l0_dn_a_log, l0_dn_dt_bias, l0_dn_norm, l0_hg_norm, l0_w_out, l0_router_w, l0_router_b, l0_exp_gate_up, l0_exp_down, l0_shared_gate_up, l0_shared_down, l1_mod_w, l1_mod_b, l1_norm1, l1_norm2, l1_w_in, l1_q_norm, l1_k_norm, l1_lambda, l1_sub_norm, l1_w_out, l1_router_w, l1_router_b, l1_exp_gate_up, l1_exp_down, l1_shared_gate_up, l1_shared_down):
    n_batch, lat_len, d = x.shape
    ctx_len = ctx.shape[1]
    n_lat = n_batch * lat_len
    n_ctx = n_batch * ctx_len
    geom = (n_lat, lat_len, n_batch)
    geom_lat_only = (n_lat, lat_len, n_batch)

    xs = jnp.concatenate([x.reshape(n_lat, d), ctx.reshape(n_ctx, d)], axis=0)
    cond = jnp.concatenate([c, c_ctx[None], jnp.zeros((MOD_ROWS - n_batch - 1, d), F32)], axis=0)

    mod3 = modulation(cond, l0_mod_w, l0_mod_b).reshape(MOD_ROWS * 6, 1, d)
    h = adaln(xs, l0_norm1, mod3, 0, geom)
    n_dn = IN0_SIZES[0] + IN0_SIZES[1]
    n_gate = IN0_SIZES[2] + IN0_SIZES[3]
    proj_dn = matmul(h, l0_w_in, n_cols=n_dn)
    proj_hg = matmul(h, l0_w_in[:, n_dn + n_gate:])
    gc, gr = gate_prep(h, l0_w_in[:, n_dn:n_dn + n_gate], l0_dn_a_log, l0_dn_dt_bias, DN_HEADS)
    conv_t = jnp.concatenate([l0_dn_conv.T, jnp.zeros((8 - l0_dn_conv.shape[1], 3 * DN_WIDTH), F32)], axis=0)
    qkv = dn_prep(proj_dn, conv_t, geom, ctx_len, DN_WIDTH, DN_HEAD_DIM)
    dn_f, dn_b = delta_scan(qkv, gc, gr, geom, ctx_len, DN_HEADS, DN_HEAD_DIM, inv_passes=DELTA_INV_PASSES)
    lb = jnp.cumsum(jax.nn.softmax(hg_lb_logits, axis=0), axis=0)[0:1]
    hg_f, hg_b = hgrn_scan(proj_hg, lb, geom, ctx_len, HG_HEADS, HG_KEY_DIM)
    y = mix_out(dn_f, dn_b, hg_f, hg_b, proj_dn, 3, proj_hg, 4, l0_dn_norm, l0_hg_norm, DN_HEAD_DIM)
    xs = matmul_residual(y, l0_w_out, xs, mod3, 2, geom)
    h, logits_t = adaln(xs, l0_norm2, mod3, 3, geom, router_w=l0_router_w)
    xs = moe_ffn(xs, h, logits_t, l0_router_b, l0_exp_gate_up, l0_exp_down, l0_shared_gate_up, l0_shared_down,
                 mod3, geom)

    mod3 = modulation(cond, l1_mod_w, l1_mod_b).reshape(MOD_ROWS * 6, 1, d)
    h = adaln(xs, l1_norm1, mod3, 0, geom)
    proj = matmul(h, l1_w_in)
    cos_t, sin_t = rope_tables(lat_len, n_batch, n_ctx, DA_HEAD_DIM)
    qkv = qk_prep(proj, l1_q_norm, l1_k_norm, cos_t, sin_t)
    lam_init = 0.8 - 0.6 * math.exp(-0.3 * 1)
    lmbda = (jnp.exp(jnp.sum(l1_lambda[0] * l1_lambda[1])) - jnp.exp(jnp.sum(l1_lambda[2] * l1_lambda[3]))
             + lam_init)
    y = diff_attention(qkv, lmbda, l1_sub_norm, n_batch, lat_len, ctx_len, DA_HEADS, DA_HEAD_DIM,
                       1.0 - lam_init)
    xl = matmul_residual(y, l1_w_out, xs[:n_lat], mod3, 2, geom_lat_only)
    h, logits_t = adaln(xl, l1_norm2, mod3, 3, geom_lat_only, router_w=l1_router_w)
    xl = moe_ffn(xl, h, logits_t, l1_router_b, l1_exp_gate_up, l1_exp_down, l1_shared_gate_up, l1_shared_down,
                 mod3, geom_lat_only)
    return xl.reshape(n_batch, lat_len, d)
```

```python
import functools
import math

import jax
import jax.numpy as jnp
import numpy as np
from jax import lax
from jax.experimental import pallas as pl
from jax.experimental.pallas import tpu as pltpu

F32 = jnp.float32
BF16 = jnp.bfloat16

NORM_EPS = 1e-6
GRID_W = 64
ROPE_BASE = 10000.0

DN_HEADS = 8
DN_HEAD_DIM = 128
DN_WIDTH = DN_HEADS * DN_HEAD_DIM
DN_CHUNK = 64
HG_HEADS = 8
HG_KEY_DIM = 128
HG_VAL_DIM = 128
HG_K_WIDTH = HG_HEADS * HG_KEY_DIM
HG_V_WIDTH = HG_HEADS * HG_VAL_DIM
HG_CHUNK = 64
IN0_SIZES = (3 * DN_WIDTH, DN_WIDTH, 2 * DN_HEADS, 2 * DN_HEADS,
             HG_K_WIDTH, 2 * HG_K_WIDTH, HG_V_WIDTH, HG_V_WIDTH)
DA_HEADS = 8
DA_HEAD_DIM = 128
N_EXPERTS = 64
TOP_K = 8
N_GROUPS = 8
TOPK_GROUPS = 4
EXPERT_FF = 512
ROUTED_SCALE = 2.5

VMEM_LIMIT_BYTES = 56 * 1024 * 1024
MOD_ROWS = 8
EXPERT_SLOT_BLOCK = 256
DMA_LOOP_UNROLL = 8


def _params(*sem):
    return pltpu.CompilerParams(dimension_semantics=sem, vmem_limit_bytes=VMEM_LIMIT_BYTES)


def _silu(x):
    return x * (1.0 / (1.0 + jnp.exp(-x)))


def _mod_body(c_ref, w_ref, b_ref, o_ref):
    a = _silu(c_ref[...]).astype(BF16)
    o_ref[...] = jnp.dot(a, w_ref[...].astype(BF16), preferred_element_type=F32) + b_ref[...]


def modulation(cond, w, b, tn=1024):
    m, k = cond.shape
    n = w.shape[1]
    return pl.pallas_call(
        _mod_body, grid=(n // tn,),
        in_specs=[pl.BlockSpec((m, k), lambda j: (0, 0)),
                  pl.BlockSpec((k, tn), lambda j: (0, j)),
                  pl.BlockSpec((1, tn), lambda j: (0, j))],
        out_specs=pl.BlockSpec((m, tn), lambda j: (0, j)),
        out_shape=jax.ShapeDtypeStruct((m, n), F32),
        compiler_params=_params("arbitrary"), name="modulation",
    )(cond, w, b.reshape(1, n))


def _mod_row(tile, tile_rows, n_lat, lat_len, n_batch):
    start = tile * tile_rows
    return jnp.where(start < n_lat, start // lat_len, n_batch)


def _pack_bf16_pairs(h):
    half = h.shape[1] // 2
    bits = lax.bitcast_convert_type(h.astype(BF16).astype(F32), jnp.uint32)
    return (bits[:, :half] & jnp.uint32(0xFFFF0000)) | (bits[:, half:] >> jnp.uint32(16))


def _unpack_bf16_pairs(pk):
    hi = lax.bitcast_convert_type(pk & jnp.uint32(0xFFFF0000), F32).astype(BF16)
    lo = lax.bitcast_convert_type(pk << jnp.uint32(16), F32).astype(BF16)
    return hi, lo


def _adaln_body(x_ref, nw_ref, shift_ref, scale_ref, o_ref):
    x = x_ref[...]
    y = x * lax.rsqrt(jnp.mean(x * x, axis=-1, keepdims=True) + NORM_EPS) * nw_ref[...]
    o_ref[...] = (y * (1.0 + scale_ref[0]) + shift_ref[0]).astype(o_ref.dtype)


def _adaln_router_body(x_ref, nw_ref, shift_ref, scale_ref, rw_ref, o_ref, lg_ref):
    x = x_ref[...]
    y = x * lax.rsqrt(jnp.mean(x * x, axis=-1, keepdims=True) + NORM_EPS) * nw_ref[...]
    h = y * (1.0 + scale_ref[0]) + shift_ref[0]
    o_ref[...] = _pack_bf16_pairs(h)
    lg_ref[...] = lax.dot_general(rw_ref[...], h, (((1,), (1,)), ((), ())), preferred_element_type=F32,
                                  precision=lax.Precision.HIGHEST)


def adaln(x, norm_w, mod3, shift_idx, geom, router_w=None, tl=256):
    t, d = x.shape
    n_lat, lat_len, n_batch = geom
    row = functools.partial(_mod_row, tile_rows=tl, n_lat=n_lat, lat_len=lat_len, n_batch=n_batch)
    in_specs = [pl.BlockSpec((tl, d), lambda i: (i, 0)),
                pl.BlockSpec((1, d), lambda i: (0, 0)),
                pl.BlockSpec((1, 1, d), lambda i: (row(i) * 6 + shift_idx, 0, 0)),
                pl.BlockSpec((1, 1, d), lambda i: (row(i) * 6 + shift_idx + 1, 0, 0))]
    args = [x, norm_w.reshape(1, d), mod3, mod3]
    if router_w is None:
        return pl.pallas_call(
            _adaln_body, grid=(t // tl,), in_specs=in_specs,
            out_specs=pl.BlockSpec((tl, d), lambda i: (i, 0)),
            out_shape=jax.ShapeDtypeStruct((t, d), BF16),
            compiler_params=_params("arbitrary"), name="adaln")(*args)
    e = router_w.shape[1]
    return pl.pallas_call(
        _adaln_router_body, grid=(t // tl,),
        in_specs=in_specs + [pl.BlockSpec((e, d), lambda i: (0, 0))],
        out_specs=[pl.BlockSpec((tl, d // 2), lambda i: (i, 0)), pl.BlockSpec((e, tl), lambda i: (0, i))],
        out_shape=[jax.ShapeDtypeStruct((t, d // 2), jnp.uint32), jax.ShapeDtypeStruct((e, t), F32)],
        compiler_params=_params("arbitrary"), name="adaln_router")(*args, router_w.T)


def _mm_body(a_ref, w_ref, o_ref, wb_ref):
    @pl.when(pl.program_id(1) == 0)
    def _():
        wb_ref[...] = w_ref[...].astype(BF16)
    o_ref[...] = jnp.dot(a_ref[...], wb_ref[...], preferred_element_type=F32).astype(o_ref.dtype)


def _mm_res_body(a_ref, w_ref, res_ref, gate_ref, o_ref, wb_ref):
    @pl.when(pl.program_id(1) == 0)
    def _():
        wb_ref[...] = w_ref[...].astype(BF16)
    acc = jnp.dot(a_ref[...], wb_ref[...], preferred_element_type=F32)
    o_ref[...] = res_ref[...] + gate_ref[0] * acc


def matmul(a, w, n_cols=None, col_block0=0, tm=512, tn=512, out_dtype=F32):
    m, k = a.shape
    n = w.shape[1] if n_cols is None else n_cols
    return pl.pallas_call(
        _mm_body, grid=(n // tn, m // tm),
        in_specs=[pl.BlockSpec((tm, k), lambda j, i: (i, 0)),
                  pl.BlockSpec((k, tn), lambda j, i: (0, j + col_block0))],
        out_specs=pl.BlockSpec((tm, tn), lambda j, i: (i, j)),
        out_shape=jax.ShapeDtypeStruct((m, n), out_dtype),
        scratch_shapes=[pltpu.VMEM((k, tn), BF16)],
        compiler_params=_params("arbitrary", "arbitrary"), name="matmul",
    )(a, w)


def matmul_residual(a, w, res, mod3, gate_idx, geom, tm=512, tn=512):
    m, k = a.shape
    n = w.shape[1]
    n_lat, lat_len, n_batch = geom
    row = functools.partial(_mod_row, tile_rows=tm, n_lat=n_lat, lat_len=lat_len, n_batch=n_batch)
    nb = n // tn
    return pl.pallas_call(
        _mm_res_body, grid=(n // tn, m // tm),
        in_specs=[pl.BlockSpec((tm, k), lambda j, i: (i, 0)),
                  pl.BlockSpec((k, tn), lambda j, i: (0, j)),
                  pl.BlockSpec((tm, tn), lambda j, i: (i, j)),
                  pl.BlockSpec((1, 1, tn), lambda j, i: (row(i) * 6 + gate_idx, 0, j))],
        out_specs=pl.BlockSpec((tm, tn), lambda j, i: (i, j)),
        out_shape=jax.ShapeDtypeStruct((m, n), F32),
        scratch_shapes=[pltpu.VMEM((k, tn), BF16)],
        compiler_params=_params("arbitrary", "arbitrary"), name="matmul_residual",
    )(a, w, res, mod3)


def _expert_body(blk_e_ref, blk_new_ref, n_used_ref, x_ref, gu_ref, dn_ref, o_ref, gub_ref, dnb_ref):
    i = pl.program_id(0)

    @pl.when(blk_new_ref[i] == 1)
    def _():
        gub_ref[...] = gu_ref[0].astype(BF16)
        dnb_ref[...] = dn_ref[0].astype(BF16)

    @pl.when(i < n_used_ref[0])
    def _():
        ff = dnb_ref.shape[0]
        half = x_ref.shape[1]
        x_hi, x_lo = _unpack_bf16_pairs(x_ref[...])
        h1 = (jnp.dot(x_hi, gub_ref[:half], preferred_element_type=F32)
              + jnp.dot(x_lo, gub_ref[half:], preferred_element_type=F32))
        act = (_silu(h1[:, :ff]) * h1[:, ff:]).astype(BF16)
        o_ref[...] = jnp.dot(act, dnb_ref[...], preferred_element_type=F32)

    @pl.when(i >= n_used_ref[0])
    def _():
        o_ref[...] = jnp.zeros_like(o_ref)


def expert_blocks(x, gate_up, down, blk_e, blk_new, n_used, blk):
    s = x.shape[0]
    _, d, f2 = gate_up.shape
    n_blk = s // blk
    grid_spec = pltpu.PrefetchScalarGridSpec(
        num_scalar_prefetch=3, grid=(n_blk,),
        in_specs=[pl.BlockSpec((blk, d // 2), lambda i, be, bn, nu: (i, 0)),
                  pl.BlockSpec((1, d, f2), lambda i, be, bn, nu: (be[i], 0, 0)),
                  pl.BlockSpec((1, f2 // 2, d), lambda i, be, bn, nu: (be[i], 0, 0))],
        out_specs=pl.BlockSpec((blk, d), lambda i, be, bn, nu: (i, 0)),
        scratch_shapes=[pltpu.VMEM((d, f2), BF16), pltpu.VMEM((f2 // 2, d), BF16)])
    return pl.pallas_call(
        _expert_body, grid_spec=grid_spec,
        out_shape=jax.ShapeDtypeStruct((s, d), F32),
        compiler_params=_params("arbitrary"), name="expert_blocks",
    )(blk_e, blk_new, n_used, x, gate_up, down)


def _qk_prep_body(x_ref, qw_ref, kw_ref, cos_ref, sin_ref, o_ref, *, n_qk_groups, head_dim):
    cos = cos_ref[...]
    sin = sin_ref[...]
    lane = lax.broadcasted_iota(jnp.int32, cos.shape, 1)
    first = (lane % (head_dim // 2)) < (head_dim // 4)
    for g in range(n_qk_groups):
        sl = slice(g * head_dim, (g + 1) * head_dim)
        x = x_ref[:, sl]
        w = qw_ref[...] if g < n_qk_groups // 2 else kw_ref[...]
        y = x * lax.rsqrt(jnp.mean(x * x, axis=-1, keepdims=True) + NORM_EPS) * w
        swapped = jnp.where(first, pltpu.roll(y, head_dim - head_dim // 4, 1), pltpu.roll(y, head_dim // 4, 1))
        o_ref[:, sl] = (y * cos + swapped * sin).astype(o_ref.dtype)
    rest = n_qk_groups * head_dim
    o_ref[:, rest:] = x_ref[:, rest:].astype(o_ref.dtype)


def qk_prep(proj, q_norm, k_norm, cos_t, sin_t, tl=256):
    t, n = proj.shape
    hd = q_norm.shape[0]
    n_groups = (2 * n // 3) // hd
    body = functools.partial(_qk_prep_body, n_qk_groups=n_groups, head_dim=hd)
    return pl.pallas_call(
        body, grid=(t // tl,),
        in_specs=[pl.BlockSpec((tl, n), lambda i: (i, 0)),
                  pl.BlockSpec((1, hd), lambda i: (0, 0)),
                  pl.BlockSpec((1, hd), lambda i: (0, 0)),
                  pl.BlockSpec((tl, hd), lambda i: (i, 0)),
                  pl.BlockSpec((tl, hd), lambda i: (i, 0))],
        out_specs=pl.BlockSpec((tl, n), lambda i: (i, 0)),
        out_shape=jax.ShapeDtypeStruct((t, n), BF16),
        compiler_params=_params("arbitrary"), name="qk_prep",
    )(proj, q_norm.reshape(1, hd), k_norm.reshape(1, hd), cos_t, sin_t)


def rope_tables(n_lat_tokens_per_sample, n_batch, n_ctx_tokens, head_dim):
    quarter = head_dim // 4
    inv_freq = ROPE_BASE ** (-jnp.arange(quarter, dtype=F32) / quarter)
    rows = n_lat_tokens_per_sample // GRID_W
    row = jnp.repeat(jnp.arange(rows, dtype=F32), GRID_W)
    col = jnp.tile(jnp.arange(GRID_W, dtype=F32), rows)
    ang_r = row[:, None] * inv_freq[None, :]
    ang_c = col[:, None] * inv_freq[None, :]
    cos = jnp.concatenate([jnp.cos(ang_r), jnp.cos(ang_r), jnp.cos(ang_c), jnp.cos(ang_c)], axis=-1)
    sin = jnp.concatenate([-jnp.sin(ang_r), jnp.sin(ang_r), -jnp.sin(ang_c), jnp.sin(ang_c)], axis=-1)
    cos = jnp.concatenate([jnp.tile(cos, (n_batch, 1)), jnp.ones((n_ctx_tokens, head_dim), F32)], axis=0)
    sin = jnp.concatenate([jnp.tile(sin, (n_batch, 1)), jnp.zeros((n_ctx_tokens, head_dim), F32)], axis=0)
    return cos, sin


def _diff_attn_body(lam_ref, q_ref, kl_ref, kc_ref, vl_ref, vc_ref, sw_ref, o_ref, *, head_dim, out_scale):
    lam = lam_ref[0]
    scale = head_dim ** -0.5
    contract_last = (((1,), (1,)), ((), ()))
    p_l = p_c = None
    for s in range(2):
        sl = slice(s * head_dim, (s + 1) * head_dim)
        qs = q_ref[:, sl]
        s_l = lax.dot_general(qs, kl_ref[:, sl], contract_last, preferred_element_type=F32) * scale
        s_c = lax.dot_general(qs, kc_ref[:, sl], contract_last, preferred_element_type=F32) * scale
        m = jnp.maximum(jnp.max(s_l, axis=-1, keepdims=True), jnp.max(s_c, axis=-1, keepdims=True))
        e_l = jnp.exp(s_l - m)
        e_c = jnp.exp(s_c - m)
        inv = 1.0 / (jnp.sum(e_l, axis=-1, keepdims=True) + jnp.sum(e_c, axis=-1, keepdims=True))
        if s == 0:
            p_l, p_c = e_l * inv, e_c * inv
        else:
            p_l, p_c = p_l - (lam * inv) * e_l, p_c - (lam * inv) * e_c
    o = (jnp.dot(p_l.astype(BF16), vl_ref[...], preferred_element_type=F32)
         + jnp.dot(p_c.astype(BF16), vc_ref[...], preferred_element_type=F32))
    y = o * lax.rsqrt(jnp.mean(o * o, axis=-1, keepdims=True) + NORM_EPS) * sw_ref[...]
    o_ref[...] = (y * out_scale).astype(o_ref.dtype)


def diff_attention(qkv, lmbda, sub_norm, n_batch, lat_len, ctx_len, n_heads, head_dim, out_scale, tq=256):
    hw = 2 * head_dim
    nq = lat_len // tq
    ctx_blk0 = n_batch * lat_len // ctx_len
    body = functools.partial(_diff_attn_body, head_dim=head_dim, out_scale=out_scale)
    return pl.pallas_call(
        body, grid=(n_batch, n_heads, nq),
        in_specs=[pl.BlockSpec(memory_space=pltpu.SMEM),
                  pl.BlockSpec((tq, hw), lambda b, h, i: (b * nq + i, h)),
                  pl.BlockSpec((lat_len, hw), lambda b, h, i: (b, n_heads + h)),
                  pl.BlockSpec((ctx_len, hw), lambda b, h, i: (ctx_blk0 + b, n_heads + h)),
                  pl.BlockSpec((lat_len, hw), lambda b, h, i: (b, 2 * n_heads + h)),
                  pl.BlockSpec((ctx_len, hw), lambda b, h, i: (ctx_blk0 + b, 2 * n_heads + h)),
                  pl.BlockSpec((1, hw), lambda b, h, i: (0, 0))],
        out_specs=pl.BlockSpec((tq, hw), lambda b, h, i: (b * nq + i, h)),
        out_shape=jax.ShapeDtypeStruct((n_batch * lat_len, n_heads * hw), BF16),
        compiler_params=_params("arbitrary", "arbitrary", "arbitrary"), name="diff_attention",
    )(lmbda.reshape(1), qkv, qkv, qkv, qkv, qkv, sub_norm.reshape(1, hw))


SCAN_TILE = 256
CHUNK = 64
SUB = 16
NEG_BIG = -1e30
NEG_INF = float("-inf")
DELTA_INV_PASSES = 1


def _sigmoid(x):
    return 1.0 / (1.0 + jnp.exp(-x))


def _dot(a, b):
    return jnp.dot(a.astype(BF16), b.astype(BF16), preferred_element_type=F32)


def _dot_nt(a, b):
    return lax.dot_general(a.astype(BF16), b.astype(BF16), (((1,), (1,)), ((), ())),
                           preferred_element_type=F32)


def _split3(x):
    hi = x.astype(BF16)
    r = x - hi.astype(F32)
    mid = r.astype(BF16)
    lo = (r - mid.astype(F32)).astype(BF16)
    return hi, mid, lo


def _dot_exact_lhs01(m01, x):
    hi, mid, lo = _split3(x)
    m = m01.astype(BF16)
    return (jnp.dot(m, hi, preferred_element_type=F32) + jnp.dot(m, mid, preferred_element_type=F32)
            + jnp.dot(m, lo, preferred_element_type=F32))


def _dot_exact_rhs01(x, m01):
    hi, mid, lo = _split3(x)
    m = m01.astype(BF16)
    return (jnp.dot(hi, m, preferred_element_type=F32) + jnp.dot(mid, m, preferred_element_type=F32)
            + jnp.dot(lo, m, preferred_element_type=F32))


def _dot3(a, b):
    ah = a.astype(BF16)
    al = (a - ah.astype(F32)).astype(BF16)
    bh = b.astype(BF16)
    bl = (b - bh.astype(F32)).astype(BF16)
    return (jnp.dot(ah, bh, preferred_element_type=F32) + jnp.dot(ah, bl, preferred_element_type=F32)
            + jnp.dot(al, bh, preferred_element_type=F32))


def _tile_masks(n, reverse):
    i = lax.broadcasted_iota(jnp.int32, (n, n), 0)
    j = lax.broadcasted_iota(jnp.int32, (n, n), 1)
    same = (i // CHUNK) == (j // CHUNK)
    if reverse:
        return same & (i <= j), same & (i < j)
    return same & (i >= j), same & (i > j)


def _segment_tile(b, s, reverse, nl, nc, ctx_tile0):
    if reverse:
        return jnp.where(s < nc, ctx_tile0 + b * nc + (nc - 1 - s), b * nl + (nl - 1 - (s - nc)))
    return jnp.where(s < nc, ctx_tile0 + b * nc + s, b * nl + (s - nc))


def _dn_prep_body(x_ref, prev_ref, next_ref, w_ref, o_ref, *, tiles_per_lat_seg, tiles_per_ctx_seg, n_lat_tiles,
                  head_dim, q_scale):
    i = pl.program_id(0)
    j = pl.program_id(1)
    is_lat = i < n_lat_tiles
    pos = jnp.where(is_lat, i % tiles_per_lat_seg, (i - n_lat_tiles) % tiles_per_ctx_seg)
    seg_first = pos == 0
    seg_last = pos == jnp.where(is_lat, tiles_per_lat_seg, tiles_per_ctx_seg) - 1
    x = x_ref[...]
    tl = x.shape[0]
    prev = jnp.where(seg_first, 0.0, prev_ref[...])
    nxt = jnp.where(seg_last, 0.0, next_ref[...])
    xp = jnp.concatenate([prev, x, nxt], axis=0)
    w = w_ref[...]
    n_taps = 5
    acc = None
    for t in range(n_taps):
        off = 8 + t - n_taps // 2
        term = xp[off:off + tl] * w[t:t + 1]
        acc = term if acc is None else acc + term
    y = _silu(acc)
    scale = jnp.where(j == 0, q_scale, 1.0)
    outs = []
    for h in range(y.shape[1] // head_dim):
        yh = y[:, h * head_dim:(h + 1) * head_dim]
        nrm = lax.rsqrt(jnp.sum(yh * yh, axis=-1, keepdims=True) + 1e-6) * scale
        outs.append(yh * jnp.where(j == 2, 1.0, nrm))
    o_ref[...] = jnp.concatenate(outs, axis=1)


def dn_prep(proj_dn, conv_w_t, geom, ctx_len, width, head_dim):
    n_lat, lat_len, n_batch = geom
    t = proj_dn.shape[0]
    tl = SCAN_TILE
    rows8 = tl // 8
    n_tiles = t // tl
    body = functools.partial(_dn_prep_body, tiles_per_lat_seg=lat_len // tl, tiles_per_ctx_seg=ctx_len // tl,
                             n_lat_tiles=n_lat // tl,
                             head_dim=head_dim, q_scale=head_dim ** -0.5)
    last8 = t // 8 - 1
    return pl.pallas_call(
        body, grid=(n_tiles, 3),
        in_specs=[pl.BlockSpec((tl, width), lambda i, j: (i, j)),
                  pl.BlockSpec((8, width), lambda i, j: (jnp.maximum(i * rows8 - 1, 0), j)),
                  pl.BlockSpec((8, width), lambda i, j: (jnp.minimum((i + 1) * rows8, last8), j)),
                  pl.BlockSpec((8, width), lambda i, j: (0, j))],
        out_specs=pl.BlockSpec((tl, width), lambda i, j: (i, j)),
        out_shape=jax.ShapeDtypeStruct((t, 3 * width), F32),
        compiler_params=_params("arbitrary", "arbitrary"), name="dn_prep",
    )(proj_dn, proj_dn, proj_dn, conv_w_t)


def _softplus(x):
    return jnp.maximum(x, 0.0) + jnp.log(1.0 + jnp.exp(-jnp.abs(x)))


def _gate_prep_body(h_ref, wc_ref, wr_ref, alog_c_ref, dtb_c_ref, alog_r_ref, dtb_r_ref, gc_ref, gr_ref, *, n_heads):
    h = h_ref[...]
    tl = h.shape[0]
    nd = 2 * n_heads
    raw_c = jnp.dot(h, wc_ref[...].astype(BF16), preferred_element_type=F32)
    raw_r = lax.dot_general(wr_ref[...].astype(BF16), h, (((1,), (1,)), ((), ())),
                            preferred_element_type=F32)
    g_c = -jnp.exp(alog_c_ref[...]) * _softplus(raw_c[:, :nd] + dtb_c_ref[...])
    g_r = -jnp.exp(alog_r_ref[...]) * _softplus(raw_r[:nd, :] + dtb_r_ref[...])
    incl_f, _ = _tile_masks(tl, False)
    incl_b, _ = _tile_masks(tl, True)
    one_f = jnp.where(incl_f, 1.0, 0.0)
    one_b = jnp.where(incl_b, 1.0, 0.0)
    cum_c = jnp.concatenate([_dot_exact_lhs01(one_f, g_c[:, :n_heads]),
                             _dot_exact_lhs01(one_b, g_c[:, n_heads:])], axis=1)
    cum_r = jnp.concatenate([_dot_exact_rhs01(g_r[:n_heads, :], one_b),
                             _dot_exact_rhs01(g_r[n_heads:, :], one_f)], axis=0)
    beta_c = _sigmoid(raw_c[:, nd:])
    gc_ref[...] = jnp.concatenate([cum_c, beta_c], axis=1)
    gr_ref[...] = jnp.concatenate([cum_r, jnp.zeros_like(cum_r)], axis=0)


def gate_prep(h_bf, w_gate, a_log, dt_bias, n_heads):
    t, d = h_bf.shape
    tl = SCAN_TILE
    nd = 2 * n_heads
    body = functools.partial(_gate_prep_body, n_heads=n_heads)
    full = lambda shape: pl.BlockSpec(shape, lambda i: (0, 0))
    return pl.pallas_call(
        body, grid=(t // tl,),
        in_specs=[pl.BlockSpec((tl, d), lambda i: (i, 0)), full((d, 2 * nd)), full((2 * nd, d)),
                  full((1, nd)), full((1, nd)), full((nd, 1)), full((nd, 1))],
        out_specs=[pl.BlockSpec((tl, 2 * nd), lambda i: (i, 0)), pl.BlockSpec((2 * nd, tl), lambda i: (0, i))],
        out_shape=[jax.ShapeDtypeStruct((t, 2 * nd), F32), jax.ShapeDtypeStruct((2 * nd, t), F32)],
        compiler_params=_params("arbitrary"), name="gate_prep",
    )(h_bf, w_gate, w_gate.T, a_log.reshape(1, nd), dt_bias.reshape(1, nd),
      a_log.reshape(nd, 1), dt_bias.reshape(nd, 1))


def _select_col(x, idx):
    lane = lax.broadcasted_iota(jnp.int32, x.shape, 1)
    return jnp.sum(jnp.where(lane == idx, x, 0.0), axis=1, keepdims=True)


def _select_row(x, idx):
    row = lax.broadcasted_iota(jnp.int32, x.shape, 0)
    return jnp.sum(jnp.where(row == idx, x, 0.0), axis=0, keepdims=True)


def _delta_tile(q, k, v, gc_col, gc_row, beta_col, s_ref, reverse, inv_passes):
    tl, kd_ = k.shape
    n_chunks = tl // CHUNK
    incl, strict = _tile_masks(tl, reverse)
    decay = jnp.exp(jnp.where(incl, gc_col - gc_row, NEG_BIG))
    kb, qb, vb = k.astype(BF16), q.astype(BF16), v.astype(BF16)
    kkt = _dot_nt(kb, kb)
    qkt = _dot_nt(qb, kb)
    x = jnp.where(strict, kkt * (-beta_col) * decay, 0.0)
    dot_inv = _dot3 if inv_passes == 3 else _dot
    ri = lax.broadcasted_iota(jnp.int32, (tl, tl), 0)
    ci = lax.broadcasted_iota(jnp.int32, (tl, tl), 1)
    r = jnp.where(ri == ci, 1.0, 0.0) + x
    n_sq = int(math.log2(CHUNK)) - 1
    for _ in range(n_sq):
        x = dot_inv(x, x)
        r = r + dot_inv(r, x)
    e_g = jnp.exp(gc_col)
    rhs = jnp.concatenate([v * beta_col, k * (beta_col * e_g)], axis=1)
    sol = dot_inv(r, rhs)
    u0, w = sol[:, :v.shape[1]], sol[:, v.shape[1]:]
    attn = (qkt * decay).astype(BF16)
    o0 = _dot(attn, u0)
    qe = q * e_g - _dot(attn, w)
    tot_rows = []
    for c in range(n_chunks):
        last = c * CHUNK if reverse else c * CHUNK + CHUNK - 1
        tot_rows.append(jnp.broadcast_to(gc_col[last:last + 1, :], (CHUNK, 1)))
    tot = jnp.concatenate(tot_rows, axis=0)
    kdec_t = jnp.transpose(k * jnp.exp(tot - gc_col))
    wu = jnp.concatenate([-w, u0], axis=1).astype(BF16)
    lane = lax.broadcasted_iota(jnp.int32, kdec_t.shape, 1)
    order = range(n_chunks - 1, -1, -1) if reverse else range(n_chunks)
    outs = [None] * n_chunks
    for c in order:
        rows = slice(c * CHUNK, (c + 1) * CHUNK)
        pn = _dot(jnp.where((lane // CHUNK) == c, kdec_t, 0.0), wu)
        lhs = jnp.concatenate([qe[rows], pn[:, :kd_]], axis=0)
        s = s_ref[...]
        res = _dot(lhs, s)
        outs[c] = o0[rows] + res[:CHUNK]
        last = c * CHUNK if reverse else c * CHUNK + CHUNK - 1
        gl = jnp.exp(gc_col[last:last + 1, :])
        s_ref[...] = gl * s + res[CHUNK:] + pn[:, kd_:]
    return jnp.concatenate(outs, axis=0)


def _delta_body(qf_ref, kf_ref, vf_ref, gcf_ref, grf_ref, qb_ref, kb_ref, vb_ref, gcb_ref, grb_ref,
                of_ref, ob_ref, sf_ref, sb_ref, *, n_heads, inv_passes):
    h = pl.program_id(1)

    @pl.when(pl.program_id(2) == 0)
    def _():
        sf_ref[...] = jnp.zeros_like(sf_ref)
        sb_ref[...] = jnp.zeros_like(sb_ref)

    for reverse, (q_ref, k_ref, v_ref, gc_ref, gr_ref, o_ref, s_ref) in enumerate(
            [(qf_ref, kf_ref, vf_ref, gcf_ref, grf_ref, of_ref, sf_ref),
             (qb_ref, kb_ref, vb_ref, gcb_ref, grb_ref, ob_ref, sb_ref)]):
        idx = reverse * n_heads + h
        gcs = gc_ref[...]
        gc_col = _select_col(gcs, idx)
        beta_col = _select_col(gcs, 2 * n_heads + idx)
        gc_row = _select_row(gr_ref[...], idx)
        o_ref[...] = _delta_tile(q_ref[...], k_ref[...], v_ref[...], gc_col, gc_row, beta_col, s_ref,
                                 bool(reverse), inv_passes)


def delta_scan(qkv, gc, gr, geom, ctx_len, n_heads, head_dim, inv_passes=3):
    n_lat, lat_len, n_batch = geom
    t = qkv.shape[0]
    tl = SCAN_TILE
    nl, nc, ctx0 = lat_len // tl, ctx_len // tl, n_lat // tl
    tile = functools.partial(_segment_tile, nl=nl, nc=nc, ctx_tile0=ctx0)
    specs = []
    for reverse in (False, True):
        tix = functools.partial(tile, reverse=reverse)
        specs += [pl.BlockSpec((tl, head_dim), lambda b, h, s, tix=tix: (tix(b, s), h)),
                  pl.BlockSpec((tl, head_dim), lambda b, h, s, tix=tix: (tix(b, s), n_heads + h)),
                  pl.BlockSpec((tl, head_dim), lambda b, h, s, tix=tix: (tix(b, s), 2 * n_heads + h)),
                  pl.BlockSpec((tl, 4 * n_heads), lambda b, h, s, tix=tix: (tix(b, s), 0)),
                  pl.BlockSpec((4 * n_heads, tl), lambda b, h, s, tix=tix: (0, tix(b, s)))]
    out_specs = [pl.BlockSpec((tl, head_dim), lambda b, h, s, tix=functools.partial(tile, reverse=r): (tix(b, s), h))
                 for r in (False, True)]
    body = functools.partial(_delta_body, n_heads=n_heads, inv_passes=inv_passes)
    return pl.pallas_call(
        body, grid=(n_batch, n_heads, nl + nc), in_specs=specs, out_specs=out_specs,
        out_shape=[jax.ShapeDtypeStruct((t, n_heads * head_dim), F32)] * 2,
        scratch_shapes=[pltpu.VMEM((head_dim, head_dim), F32)] * 2,
        compiler_params=_params("arbitrary", "arbitrary", "arbitrary"), name="delta_scan",
    )(qkv, qkv, qkv, gc, gr, qkv, qkv, qkv, gc, gr)


def _hgrn_tile(hq, hf, hv, lb, st_ref, reverse):
    tl, kd_ = hq.shape
    n_chunks = tl // CHUNK
    n_sub = tl // SUB
    sub_per_chunk = CHUNK // SUB
    q = _silu(hq)
    f = lb + (1.0 - lb) * _sigmoid(hf)
    k = 1.0 - f
    lf = jnp.log(f)
    incl, _ = _tile_masks(tl, reverse)
    cum = _dot_exact_lhs01(jnp.where(incl, 1.0, 0.0), lf)
    excl = cum - lf

    def bcast_rows(src, row, n):
        return jnp.broadcast_to(src[row:row + 1, :], (n, kd_))

    chunk_last = [(c * CHUNK if reverse else c * CHUNK + CHUNK - 1) for c in range(n_chunks)]
    tot = jnp.concatenate([bcast_rows(cum, chunk_last[c], CHUNK) for c in range(n_chunks)], axis=0)
    sub_first = [(m * SUB + SUB - 1 if reverse else m * SUB) for m in range(n_sub)]
    r_sub = jnp.concatenate([bcast_rows(excl, sub_first[m], SUB) for m in range(n_sub)], axis=0)
    q_t = q * jnp.exp(cum - r_sub)
    qd = q * jnp.exp(cum)
    kd = k * jnp.exp(tot - cum)
    vb = hv.astype(BF16)

    i = lax.broadcasted_iota(jnp.int32, (tl, tl), 0)
    j = lax.broadcasted_iota(jnp.int32, (tl, tl), 1)
    same = (i // CHUNK) == (j // CHUNK)
    pos_i = (i % CHUNK) // SUB
    pos_j = (j % CHUNK) // SUB
    if reverse:
        pos_i, pos_j = sub_per_chunk - 1 - pos_i, sub_per_chunk - 1 - pos_j
    a_off = jnp.zeros((tl, tl), F32)
    for lvl in range(1, sub_per_chunk):
        ref_rows = []
        for c in range(n_chunks):
            m = c * sub_per_chunk + (sub_per_chunk - 1 - lvl if reverse else lvl)
            ref_rows.append(bcast_rows(excl, sub_first[m], CHUNK))
        r_lvl = jnp.concatenate(ref_rows, axis=0)
        k_t = k * jnp.exp(jnp.minimum(r_lvl - cum, 0.0))
        a_l = _dot_nt(q_t, k_t)
        a_off = a_off + jnp.where(same & (pos_i == lvl) & (pos_j < lvl), a_l, 0.0)
    o_intra = _dot(a_off, vb)

    ones = jnp.ones((kd_, kd_), BF16)
    ii = lax.broadcasted_iota(jnp.int32, (SUB, SUB, kd_), 0)
    jj = lax.broadcasted_iota(jnp.int32, (SUB, SUB, kd_), 1)
    tri = (ii <= jj) if reverse else (ii >= jj)
    diag_rows = []
    for m in range(n_sub):
        rows = slice(m * SUB, (m + 1) * SUB)
        cm, qm, km, vm = cum[rows], q[rows], k[rows], hv[rows]
        e = jnp.exp(jnp.minimum(cm[:, None, :] - cm[None, :, :], 0.0))
        tmp = jnp.where(tri, (qm[:, None, :] * km[None, :, :]) * e, 0.0)
        a_rep = jnp.dot(tmp.reshape(SUB * SUB, kd_).astype(BF16), ones, preferred_element_type=F32)
        diag_rows.append(jnp.sum(a_rep.reshape(SUB, SUB, kd_) * vm[None, :, :], axis=1))
    o_intra = o_intra + jnp.concatenate(diag_rows, axis=0)

    v_t = jnp.transpose(hv)
    lane = lax.broadcasted_iota(jnp.int32, v_t.shape, 1)
    kdb = kd.astype(BF16)
    order = range(n_chunks - 1, -1, -1) if reverse else range(n_chunks)
    outs = [None] * n_chunks
    for c in order:
        rows = slice(c * CHUNK, (c + 1) * CHUNK)
        st = st_ref[...]
        outs[c] = o_intra[rows] + _dot_nt(qd[rows], st)
        n_t = _dot(jnp.where((lane // CHUNK) == c, v_t, 0.0), kdb)
        st_ref[...] = st * jnp.exp(cum[chunk_last[c]:chunk_last[c] + 1, :]) + n_t
    return jnp.concatenate(outs, axis=0)


def _hgrn_body(qf_ref, ff_ref, vf_ref, qb_ref, fb_ref, vb_ref, lb_ref, of_ref, ob_ref, sf_ref, sb_ref):
    @pl.when(pl.program_id(2) == 0)
    def _():
        sf_ref[...] = jnp.zeros_like(sf_ref)
        sb_ref[...] = jnp.zeros_like(sb_ref)

    lb = lb_ref[...]
    of_ref[...] = _hgrn_tile(qf_ref[...], ff_ref[...], vf_ref[...], lb, sf_ref, False)
    ob_ref[...] = _hgrn_tile(qb_ref[...], fb_ref[...], vb_ref[...], lb, sb_ref, True)


def hgrn_scan(proj_hg, lb, geom, ctx_len, n_heads, key_dim):
    n_lat, lat_len, n_batch = geom
    t = proj_hg.shape[0]
    tl = SCAN_TILE
    nl, nc, ctx0 = lat_len // tl, ctx_len // tl, n_lat // tl
    tile = functools.partial(_segment_tile, nl=nl, nc=nc, ctx_tile0=ctx0)
    specs = []
    for reverse in (False, True):
        tix = functools.partial(tile, reverse=reverse)
        fcol = (1 + int(reverse)) * n_heads
        specs += [pl.BlockSpec((tl, key_dim), lambda b, h, s, tix=tix: (tix(b, s), h)),
                  pl.BlockSpec((tl, key_dim), lambda b, h, s, tix=tix, fcol=fcol: (tix(b, s), fcol + h)),
                  pl.BlockSpec((tl, key_dim), lambda b, h, s, tix=tix: (tix(b, s), 3 * n_heads + h))]
    specs.append(pl.BlockSpec((1, key_dim), lambda b, h, s: (0, h)))
    out_specs = [pl.BlockSpec((tl, key_dim), lambda b, h, s, tix=functools.partial(tile, reverse=r): (tix(b, s), h))
                 for r in (False, True)]
    return pl.pallas_call(
        _hgrn_body, grid=(n_batch, n_heads, nl + nc), in_specs=specs, out_specs=out_specs,
        out_shape=[jax.ShapeDtypeStruct((t, n_heads * key_dim), F32)] * 2,
        scratch_shapes=[pltpu.VMEM((key_dim, key_dim), F32)] * 2,
        compiler_params=_params("arbitrary", "arbitrary", "arbitrary"), name="hgrn_scan",
    )(proj_hg, proj_hg, proj_hg, proj_hg, proj_hg, proj_hg, lb)


def _mix_out2_body(df_ref, db_ref, hf_ref, hb_ref, z_ref, og_ref, dnw_ref, hgw_ref, o_ref, *, head_dim):
    def normed(o, nw):
        outs = []
        for h in range(o.shape[1] // head_dim):
            oh = o[:, h * head_dim:(h + 1) * head_dim]
            outs.append(oh * lax.rsqrt(jnp.mean(oh * oh, axis=-1, keepdims=True) + NORM_EPS) * nw)
        return jnp.concatenate(outs, axis=1)

    dn = normed(df_ref[...] + db_ref[...], dnw_ref[...]) * _silu(z_ref[...])
    hg = normed(hf_ref[...] + hb_ref[...], hgw_ref[...]) * _sigmoid(og_ref[...])
    half = dn.shape[1]
    o_ref[:, :half] = dn.astype(o_ref.dtype)
    o_ref[:, half:] = hg.astype(o_ref.dtype)


def mix_out(dn_f, dn_b, hg_f, hg_b, z_src, z_blk, og_src, og_blk, dn_norm, hg_norm, head_dim):
    t, w = dn_f.shape
    tl = SCAN_TILE
    row = lambda i: (i, 0)
    body = functools.partial(_mix_out2_body, head_dim=head_dim)
    return pl.pallas_call(
        body, grid=(t // tl,),
        in_specs=[pl.BlockSpec((tl, w), row)] * 4
        + [pl.BlockSpec((tl, w), lambda i: (i, z_blk)), pl.BlockSpec((tl, w), lambda i: (i, og_blk)),
           pl.BlockSpec((1, head_dim), lambda i: (0, 0)), pl.BlockSpec((1, head_dim), lambda i: (0, 0))],
        out_specs=pl.BlockSpec((tl, 2 * w), row),
        out_shape=jax.ShapeDtypeStruct((t, 2 * w), BF16),
        compiler_params=_params("arbitrary"), name="mix_out",
    )(dn_f, dn_b, hg_f, hg_b, z_src, og_src, dn_norm.reshape(1, head_dim), hg_norm.reshape(1, head_dim))


def _first_max(x, ids, n):
    m = jnp.max(x, axis=0, keepdims=True)
    first = jnp.min(jnp.where(x == m, ids, n), axis=0, keepdims=True)
    return m, first


def _route_body(lg_ref, bias_ref, idx_ref, rank_ref, w_ref, cnt_ref, carry_ref, *,
                n_groups, topk_groups, top_k, scale):
    i = pl.program_id(0)

    @pl.when(i == 0)
    def _():
        carry_ref[...] = jnp.zeros_like(carry_ref)

    lg = lg_ref[...]
    n_exp, tl = lg.shape
    per = n_exp // n_groups
    scores = 1.0 / (1.0 + jnp.exp(-lg))
    biased = scores + bias_ref[...]
    sub = lax.broadcasted_iota(jnp.int32, (per, tl), 0)
    g_rows = []
    for g in range(n_groups):
        xg = biased[g * per:(g + 1) * per]
        m1, i1 = _first_max(xg, sub, per)
        m2 = jnp.max(jnp.where(sub == i1, NEG_INF, xg), axis=0, keepdims=True)
        g_rows.append(m1 + m2)
    gscore = jnp.concatenate(g_rows, axis=0)
    gid = lax.broadcasted_iota(jnp.int32, (n_groups, tl), 0)
    gsel = jnp.zeros((n_groups, tl), jnp.bool_)
    for _ in range(topk_groups):
        _, first = _first_max(gscore, gid, n_groups)
        hit = gid == first
        gsel = gsel | hit
        gscore = jnp.where(hit, NEG_INF, gscore)
    eid = lax.broadcasted_iota(jnp.int32, (n_exp, tl), 0)
    gmask = jnp.concatenate([jnp.broadcast_to(gsel[g:g + 1], (per, tl)) for g in range(n_groups)], axis=0)
    masked = jnp.where(gmask, biased, NEG_INF)
    sel = jnp.zeros((n_exp, tl), jnp.bool_)
    hits, firsts = [], []
    for _ in range(top_k):
        _, first = _first_max(masked, eid, n_exp)
        hit = eid == first
        hits.append(hit)
        firsts.append(first)
        sel = sel | hit
        masked = jnp.where(hit, NEG_INF, masked)
    self = jnp.where(sel, 1.0, 0.0)
    ti = lax.broadcasted_iota(jnp.int32, (tl, tl), 0)
    tj = lax.broadcasted_iota(jnp.int32, (tl, tl), 1)
    before = jnp.where(ti < tj, 1.0, 0.0).astype(BF16)
    carry = carry_ref[...]
    rank_full = jnp.dot(self.astype(BF16), before, preferred_element_type=F32) + carry
    w_rows = [jnp.sum(jnp.where(hit, scores, 0.0), axis=0, keepdims=True) for hit in hits]
    r_rows = [jnp.sum(jnp.where(hit, rank_full, 0.0), axis=0, keepdims=True) for hit in hits]
    w8 = jnp.concatenate(w_rows, axis=0)
    idx_ref[...] = jnp.concatenate(firsts, axis=0)
    rank_ref[...] = jnp.concatenate(r_rows, axis=0).astype(jnp.int32)
    w_ref[...] = w8 / jnp.sum(w8, axis=0, keepdims=True) * scale
    carry = carry + jnp.sum(self, axis=1, keepdims=True)
    carry_ref[...] = carry
    cnt_ref[...] = carry.astype(jnp.int32)


def route(logits_t, router_b, tl=256):
    n_exp, t = logits_t.shape
    body = functools.partial(_route_body, n_groups=N_GROUPS, topk_groups=TOPK_GROUPS, top_k=TOP_K,
                             scale=ROUTED_SCALE)
    tok_spec = pl.BlockSpec((TOP_K, tl), lambda i: (0, i))
    return pl.pallas_call(
        body, grid=(t // tl,),
        in_specs=[pl.BlockSpec((n_exp, tl), lambda i: (0, i)), pl.BlockSpec((n_exp, 1), lambda i: (0, 0))],
        out_specs=[tok_spec, tok_spec, tok_spec, pl.BlockSpec((n_exp, 1), lambda i: (0, 0))],
        out_shape=[jax.ShapeDtypeStruct((TOP_K, t), jnp.int32), jax.ShapeDtypeStruct((TOP_K, t), jnp.int32),
                   jax.ShapeDtypeStruct((TOP_K, t), F32), jax.ShapeDtypeStruct((n_exp, 1), jnp.int32)],
        scratch_shapes=[pltpu.VMEM((n_exp, 1), F32)],
        compiler_params=_params("arbitrary"), name="route",
    )(logits_t, router_b.reshape(n_exp, 1))


def _row_copy(src_ref, src_row, dst_ref, dst_row, sem):
    return pltpu.make_async_copy(src_ref.at[pl.ds(src_row, 1)], dst_ref.at[pl.ds(dst_row, 1)], sem)


def _dispatch_body(dest_ref, h_ref, init_ref, xs_ref, sem, *, top_k):
    del init_ref
    tl = h_ref.shape[0]

    def issue(r, carry):
        for k in range(top_k):
            _row_copy(h_ref, r, xs_ref, dest_ref[r * top_k + k], sem.at[0]).start()
        return carry

    lax.fori_loop(0, tl, issue, 0, unroll=DMA_LOOP_UNROLL)

    def drain(r, carry):
        for k in range(top_k):
            _row_copy(h_ref, 0, xs_ref, 0, sem.at[0]).wait()
        return carry

    lax.fori_loop(0, tl, drain, 0, unroll=DMA_LOOP_UNROLL)


def dispatch(h_pk, dest_flat, n_slot, top_k, tl=256):
    t, w = h_pk.shape
    return pl.pallas_call(
        functools.partial(_dispatch_body, top_k=top_k), grid=(t // tl,),
        in_specs=[pl.BlockSpec((tl * top_k,), lambda i: (i,), memory_space=pltpu.SMEM),
                  pl.BlockSpec((tl, w), lambda i: (i, 0)),
                  pl.BlockSpec(memory_space=pl.ANY)],
        out_specs=pl.BlockSpec(memory_space=pl.ANY),
        out_shape=jax.ShapeDtypeStruct((n_slot, w), h_pk.dtype),
        scratch_shapes=[pltpu.SemaphoreType.DMA((1,))],
        input_output_aliases={2: 0},
        compiler_params=_params("arbitrary"), name="dispatch",
    )(dest_flat, h_pk, jnp.zeros((n_slot, w), h_pk.dtype))


def _combine_body(dest_ref, dest_next_ref, w_ref, sh_ref, x_ref, gate_ref, y_ref, o_ref, buf_ref, sem, *, top_k):
    i = pl.program_id(0)
    n = pl.num_programs(0)
    tl = x_ref.shape[0]

    def gather(d_ref, slot):
        def issue(r, carry):
            for k in range(top_k):
                _row_copy(y_ref, d_ref[r * top_k + k], buf_ref.at[slot], k * tl + r, sem.at[slot]).start()
            return carry
        lax.fori_loop(0, tl, issue, 0, unroll=DMA_LOOP_UNROLL)

    @pl.when(i == 0)
    def _():
        gather(dest_ref, 0)

    @pl.when(i + 1 < n)
    def _():
        gather(dest_next_ref, (i + 1) % 2)

    slot = i % 2

    def drain(r, carry):
        for k in range(top_k):
            _row_copy(y_ref, 0, buf_ref.at[slot], 0, sem.at[slot]).wait()
        return carry

    lax.fori_loop(0, tl, drain, 0, unroll=DMA_LOOP_UNROLL)
    acc = sh_ref[...]
    w = w_ref[...]
    for k in range(top_k):
        acc = acc + buf_ref[slot, pl.ds(k * tl, tl), :] * w[:, k:k + 1]
    o_ref[...] = x_ref[...] + gate_ref[0] * acc


def combine(y_slot, dest_flat, w, shared, x, mod3, gate_idx, geom, tl=128):
    t, d = x.shape
    top_k = w.shape[1]
    n_lat, lat_len, n_batch = geom
    n_tiles = t // tl
    row = functools.partial(_mod_row, tile_rows=tl, n_lat=n_lat, lat_len=lat_len, n_batch=n_batch)
    return pl.pallas_call(
        functools.partial(_combine_body, top_k=top_k), grid=(n_tiles,),
        in_specs=[pl.BlockSpec((tl * top_k,), lambda i: (i,), memory_space=pltpu.SMEM),
                  pl.BlockSpec((tl * top_k,), lambda i: (jnp.minimum(i + 1, n_tiles - 1),),
                               memory_space=pltpu.SMEM),
                  pl.BlockSpec((tl, top_k), lambda i: (i, 0)),
                  pl.BlockSpec((tl, d), lambda i: (i, 0)),
                  pl.BlockSpec((tl, d), lambda i: (i, 0)),
                  pl.BlockSpec((1, 1, d), lambda i: (row(i) * 6 + gate_idx, 0, 0)),
                  pl.BlockSpec(memory_space=pl.ANY)],
        out_specs=pl.BlockSpec((tl, d), lambda i: (i, 0)),
        out_shape=jax.ShapeDtypeStruct((t, d), F32),
        scratch_shapes=[pltpu.VMEM((2, top_k * tl, d), F32), pltpu.SemaphoreType.DMA((2,))],
        compiler_params=_params("arbitrary"), name="combine",
    )(dest_flat, dest_flat, w, shared, x, mod3, y_slot)


def moe_ffn(x, h_pk, logits_t, router_b, exp_gate_up, exp_down, shared_gate_up, shared_down, mod3, geom):
    t = h_pk.shape[0]
    e_count = exp_gate_up.shape[0]
    blk = EXPERT_SLOT_BLOCK
    idx8, rank8, w8, counts = route(logits_t, router_b)
    counts = counts[:, 0]
    padded = (counts + blk - 1) // blk * blk
    pad_end = jnp.cumsum(padded)
    pad_start = pad_end - padded
    dest_flat = (pad_start[idx8] + rank8).T.reshape(-1)
    n_blk = (t * TOP_K + e_count * (blk - 1)) // blk + 1
    n_slot = n_blk * blk
    blk_starts = jnp.arange(n_blk, dtype=jnp.int32) * blk
    blk_e = jnp.sum((pad_end[None, :] <= blk_starts[:, None]).astype(jnp.int32), axis=1)
    blk_e = jnp.minimum(blk_e, e_count - 1)
    blk_new = jnp.concatenate([jnp.ones((1,), jnp.int32), (blk_e[1:] != blk_e[:-1]).astype(jnp.int32)])
    n_used = (pad_end[-1] // blk).astype(jnp.int32).reshape(1)
    x_sorted = dispatch(h_pk, dest_flat, n_slot, TOP_K)
    y_slot = expert_blocks(x_sorted, exp_gate_up, exp_down, blk_e, blk_new, n_used, blk)
    one = jnp.ones((t // blk,), jnp.int32)
    shared = expert_blocks(h_pk, shared_gate_up[None], shared_down[None],
                           jnp.zeros((t // blk,), jnp.int32), one.at[1:].set(0),
                           jnp.full((1,), t // blk, jnp.int32), blk)
    return combine(y_slot, dest_flat, w8.T, shared, x, mod3, 5, geom)


def kernel(x, c, ctx, c_ctx, hg_lb_logits, l0_mod_w, l0_mod_b, l0_norm1, l0_norm2, l0_w_in, l0_dn_conv, l0_dn_a_log, l0_dn_dt_bias, l0_dn_norm, l0_hg_norm, l0_w_out, l0_router_w, l0_router_b, l0_exp_gate_up, l0_exp_down, l0_shared_gate_up, l0_shared_down, l1_mod_w, l1_mod_b, l1_norm1, l1_norm2, l1_w_in, l1_q_norm, l1_k_norm, l1_lambda, l1_sub_norm, l1_w_out, l1_router_w, l1_router_b, l1_exp_gate_up, l1_exp_down, l1_shared_gate_up, l1_shared_down):
    n_batch, lat_len, d = x.shape
    ctx_len = ctx.shape[1]
    n_lat = n_batch * lat_len
    n_ctx = n_batch * ctx_len
    geom = (n_lat, lat_len, n_batch)
    geom_lat_only = (n_lat, lat_len, n_batch)

    xs = jnp.concatenate([x.reshape(n_lat, d), ctx.reshape(n_ctx, d)], axis=0)
    cond = jnp.concatenate([c, c_ctx[None], jnp.zeros((MOD_ROWS - n_batch - 1, d), F32)], axis=0)

    mod3 = modulation(cond, l0_mod_w, l0_mod_b).reshape(MOD_ROWS * 6, 1, d)
    h = adaln(xs, l0_norm1, mod3, 0, geom)
    n_dn = IN0_SIZES[0] + IN0_SIZES[1]
    n_gate = IN0_SIZES[2] + IN0_SIZES[3]
    proj_dn = matmul(h, l0_w_in, n_cols=n_dn)
    proj_hg = matmul(h, l0_w_in[:, n_dn + n_gate:])
    gc, gr = gate_prep(h, l0_w_in[:, n_dn:n_dn + n_gate], l0_dn_a_log, l0_dn_dt_bias, DN_HEADS)
    conv_t = jnp.concatenate([l0_dn_conv.T, jnp.zeros((8 - l0_dn_conv.shape[1], 3 * DN_WIDTH), F32)], axis=0)
    qkv = dn_prep(proj_dn, conv_t, geom, ctx_len, DN_WIDTH, DN_HEAD_DIM)
    dn_f, dn_b = delta_scan(qkv, gc, gr, geom, ctx_len, DN_HEADS, DN_HEAD_DIM, inv_passes=DELTA_INV_PASSES)
    lb = jnp.cumsum(jax.nn.softmax(hg_lb_logits, axis=0), axis=0)[0:1]
    hg_f, hg_b = hgrn_scan(proj_hg, lb, geom, ctx_len, HG_HEADS, HG_KEY_DIM)
    y = mix_out(dn_f, dn_b, hg_f, hg_b, proj_dn, 3, proj_hg, 4, l0_dn_norm, l0_hg_norm, DN_HEAD_DIM)
    xs = matmul_residual(y, l0_w_out, xs, mod3, 2, geom)
    h, logits_t = adaln(xs, l0_norm2, mod3, 3, geom, router_w=l0_router_w)
    xs = moe_ffn(xs, h, logits_t, l0_router_b, l0_exp_gate_up, l0_exp_down, l0_shared_gate_up, l0_shared_down,
                 mod3, geom)

    mod3 = modulation(cond, l1_mod_w, l1_mod_b).reshape(MOD_ROWS * 6, 1, d)
    h = adaln(xs, l1_norm1, mod3, 0, geom)
    proj = matmul(h, l1_w_in)
    cos_t, sin_t = rope_tables(lat_len, n_batch, n_ctx, DA_HEAD_DIM)
    qkv = qk_prep(proj, l1_q_norm, l1_k_norm, cos_t, sin_t)
    lam_init = 0.8 - 0.6 * math.exp(-0.3 * 1)
    lmbda = (jnp.exp(jnp.sum(l1_lambda[0] * l1_lambda[1])) - jnp.exp(jnp.sum(l1_lambda[2] * l1_lambda[3]))
             + lam_init)
    y = diff_attention(qkv, lmbda, l1_sub_norm, n_batch, lat_len, ctx_len, DA_HEADS, DA_HEAD_DIM,
                       1.0 - lam_init)
    xl = matmul_residual(y, l1_w_out, xs[:n_lat], mod3, 2, geom_lat_only)
    h, logits_t = adaln(xl, l1_norm2, mod3, 3, geom_lat_only, router_w=l1_router_w)
    xl = moe_ffn(xl, h, logits_t, l1_router_b, l1_exp_gate_up, l1_exp_down, l1_shared_gate_up, l1_shared_down,
                 mod3, geom_lat_only)
    return xl.reshape(n_batch, lat_len, d)
```

```python
import functools
import math

import jax
import jax.numpy as jnp
import numpy as np
from jax import lax
from jax.experimental import pallas as pl
from jax.experimental.pallas import tpu as pltpu

F32 = jnp.float32
BF16 = jnp.bfloat16

NORM_EPS = 1e-6
GRID_W = 64
ROPE_BASE = 10000.0

DN_HEADS = 8
DN_HEAD_DIM = 128
DN_WIDTH = DN_HEADS * DN_HEAD_DIM
DN_CHUNK = 64
HG_HEADS = 8
HG_KEY_DIM = 128
HG_VAL_DIM = 128
HG_K_WIDTH = HG_HEADS * HG_KEY_DIM
HG_V_WIDTH = HG_HEADS * HG_VAL_DIM
HG_CHUNK = 64
IN0_SIZES = (3 * DN_WIDTH, DN_WIDTH, 2 * DN_HEADS, 2 * DN_HEADS,
             HG_K_WIDTH, 2 * HG_K_WIDTH, HG_V_WIDTH, HG_V_WIDTH)
DA_HEADS = 8
DA_HEAD_DIM = 128
N_EXPERTS = 64
TOP_K = 8
N_GROUPS = 8
TOPK_GROUPS = 4
EXPERT_FF = 512
ROUTED_SCALE = 2.5

VMEM_LIMIT_BYTES = 56 * 1024 * 1024
MOD_ROWS = 8
EXPERT_SLOT_BLOCK = 256
DMA_LOOP_UNROLL = 8
MM_TILE_M = 1024
MM_TILE_N = 1024


def _params(*sem):
    return pltpu.CompilerParams(dimension_semantics=sem, vmem_limit_bytes=VMEM_LIMIT_BYTES)


def _silu(x):
    return x * (1.0 / (1.0 + jnp.exp(-x)))


def _mod_body(c_ref, w_ref, b_ref, o_ref):
    a = _silu(c_ref[...]).astype(BF16)
    o_ref[...] = jnp.dot(a, w_ref[...].astype(BF16), preferred_element_type=F32) + b_ref[...]


def modulation(cond, w, b, tn=1024):
    m, k = cond.shape
    n = w.shape[1]
    return pl.pallas_call(
        _mod_body, grid=(n // tn,),
        in_specs=[pl.BlockSpec((m, k), lambda j: (0, 0)),
                  pl.BlockSpec((k, tn), lambda j: (0, j)),
                  pl.BlockSpec((1, tn), lambda j: (0, j))],
        out_specs=pl.BlockSpec((m, tn), lambda j: (0, j)),
        out_shape=jax.ShapeDtypeStruct((m, n), F32),
        compiler_params=_params("arbitrary"), name="modulation",
    )(cond, w, b.reshape(1, n))


def _mod_row(tile, tile_rows, n_lat, lat_len, n_batch):
    start = tile * tile_rows
    return jnp.where(start < n_lat, start // lat_len, n_batch)


def _pack_bf16_pairs(h):
    half = h.shape[1] // 2
    bits = lax.bitcast_convert_type(h.astype(BF16).astype(F32), jnp.uint32)
    return (bits[:, :half] & jnp.uint32(0xFFFF0000)) | (bits[:, half:] >> jnp.uint32(16))


def _unpack_bf16_pairs(pk):
    hi = lax.bitcast_convert_type(pk & jnp.uint32(0xFFFF0000), F32).astype(BF16)
    lo = lax.bitcast_convert_type(pk << jnp.uint32(16), F32).astype(BF16)
    return hi, lo


def _adaln_body(x_ref, nw_ref, shift_ref, scale_ref, o_ref):
    x = x_ref[...]
    y = x * lax.rsqrt(jnp.mean(x * x, axis=-1, keepdims=True) + NORM_EPS) * nw_ref[...]
    o_ref[...] = (y * (1.0 + scale_ref[0]) + shift_ref[0]).astype(o_ref.dtype)


def _adaln_router_body(x_ref, nw_ref, shift_ref, scale_ref, rw_ref, o_ref, lg_ref):
    x = x_ref[...]
    y = x * lax.rsqrt(jnp.mean(x * x, axis=-1, keepdims=True) + NORM_EPS) * nw_ref[...]
    h = y * (1.0 + scale_ref[0]) + shift_ref[0]
    o_ref[...] = _pack_bf16_pairs(h)
    lg_ref[...] = lax.dot_general(rw_ref[...], h, (((1,), (1,)), ((), ())), preferred_element_type=F32,
                                  precision=lax.Precision.HIGHEST)


def adaln(x, norm_w, mod3, shift_idx, geom, router_w=None, tl=256):
    t, d = x.shape
    n_lat, lat_len, n_batch = geom
    row = functools.partial(_mod_row, tile_rows=tl, n_lat=n_lat, lat_len=lat_len, n_batch=n_batch)
    in_specs = [pl.BlockSpec((tl, d), lambda i: (i, 0)),
                pl.BlockSpec((1, d), lambda i: (0, 0)),
                pl.BlockSpec((1, 1, d), lambda i: (row(i) * 6 + shift_idx, 0, 0)),
                pl.BlockSpec((1, 1, d), lambda i: (row(i) * 6 + shift_idx + 1, 0, 0))]
    args = [x, norm_w.reshape(1, d), mod3, mod3]
    if router_w is None:
        return pl.pallas_call(
            _adaln_body, grid=(t // tl,), in_specs=in_specs,
            out_specs=pl.BlockSpec((tl, d), lambda i: (i, 0)),
            out_shape=jax.ShapeDtypeStruct((t, d), BF16),
            compiler_params=_params("arbitrary"), name="adaln")(*args)
    e = router_w.shape[1]
    return pl.pallas_call(
        _adaln_router_body, grid=(t // tl,),
        in_specs=in_specs + [pl.BlockSpec((e, d), lambda i: (0, 0))],
        out_specs=[pl.BlockSpec((tl, d // 2), lambda i: (i, 0)), pl.BlockSpec((e, tl), lambda i: (0, i))],
        out_shape=[jax.ShapeDtypeStruct((t, d // 2), jnp.uint32), jax.ShapeDtypeStruct((e, t), F32)],
        compiler_params=_params("arbitrary"), name="adaln_router")(*args, router_w.T)


def _mm_body(a_ref, w_ref, o_ref, wb_ref):
    @pl.when(pl.program_id(1) == 0)
    def _():
        wb_ref[...] = w_ref[...].astype(BF16)
    o_ref[...] = jnp.dot(a_ref[...], wb_ref[...], preferred_element_type=F32).astype(o_ref.dtype)


def _mm_res_body(a_ref, w_ref, res_ref, gate_ref, o_ref, wb_ref):
    @pl.when(pl.program_id(1) == 0)
    def _():
        wb_ref[...] = w_ref[...].astype(BF16)
    acc = jnp.dot(a_ref[...], wb_ref[...], preferred_element_type=F32)
    o_ref[...] = res_ref[...] + gate_ref[0] * acc


def matmul(a, w, n_cols=None, col_block0=0, tm=MM_TILE_M, tn=MM_TILE_N, out_dtype=F32):
    m, k = a.shape
    n = w.shape[1] if n_cols is None else n_cols
    return pl.pallas_call(
        _mm_body, grid=(n // tn, m // tm),
        in_specs=[pl.BlockSpec((tm, k), lambda j, i: (i, 0)),
                  pl.BlockSpec((k, tn), lambda j, i: (0, j + col_block0))],
        out_specs=pl.BlockSpec((tm, tn), lambda j, i: (i, j)),
        out_shape=jax.ShapeDtypeStruct((m, n), out_dtype),
        scratch_shapes=[pltpu.VMEM((k, tn), BF16)],
        compiler_params=_params("arbitrary", "arbitrary"), name="matmul",
    )(a, w)


def matmul_residual(a, w, res, mod3, gate_idx, geom, tm=MM_TILE_M, tn=MM_TILE_N):
    m, k = a.shape
    n = w.shape[1]
    n_lat, lat_len, n_batch = geom
    row = functools.partial(_mod_row, tile_rows=tm, n_lat=n_lat, lat_len=lat_len, n_batch=n_batch)
    nb = n // tn
    return pl.pallas_call(
        _mm_res_body, grid=(n // tn, m // tm),
        in_specs=[pl.BlockSpec((tm, k), lambda j, i: (i, 0)),
                  pl.BlockSpec((k, tn), lambda j, i: (0, j)),
                  pl.BlockSpec((tm, tn), lambda j, i: (i, j)),
                  pl.BlockSpec((1, 1, tn), lambda j, i: (row(i) * 6 + gate_idx, 0, j))],
        out_specs=pl.BlockSpec((tm, tn), lambda j, i: (i, j)),
        out_shape=jax.ShapeDtypeStruct((m, n), F32),
        scratch_shapes=[pltpu.VMEM((k, tn), BF16)],
        compiler_params=_params("arbitrary", "arbitrary"), name="matmul_residual",
    )(a, w, res, mod3)


def _expert_body(blk_e_ref, blk_new_ref, n_used_ref, x_ref, gu_ref, dn_ref, o_ref, gub_ref, dnb_ref):
    i = pl.program_id(0)

    @pl.when(blk_new_ref[i] == 1)
    def _():
        gub_ref[...] = gu_ref[0].astype(BF16)
        dnb_ref[...] = dn_ref[0].astype(BF16)

    @pl.when(i < n_used_ref[0])
    def _():
        ff = dnb_ref.shape[0]
        half = x_ref.shape[1]
        x_hi, x_lo = _unpack_bf16_pairs(x_ref[...])
        h1 = (jnp.dot(x_hi, gub_ref[:half], preferred_element_type=F32)
              + jnp.dot(x_lo, gub_ref[half:], preferred_element_type=F32))
        act = (_silu(h1[:, :ff]) * h1[:, ff:]).astype(BF16)
        o_ref[...] = jnp.dot(act, dnb_ref[...], preferred_element_type=F32)

    @pl.when(i >= n_used_ref[0])
    def _():
        o_ref[...] = jnp.zeros_like(o_ref)


def expert_blocks(x, gate_up, down, blk_e, blk_new, n_used, blk):
    s = x.shape[0]
    _, d, f2 = gate_up.shape
    n_blk = s // blk
    grid_spec = pltpu.PrefetchScalarGridSpec(
        num_scalar_prefetch=3, grid=(n_blk,),
        in_specs=[pl.BlockSpec((blk, d // 2), lambda i, be, bn, nu: (i, 0)),
                  pl.BlockSpec((1, d, f2), lambda i, be, bn, nu: (be[i], 0, 0)),
                  pl.BlockSpec((1, f2 // 2, d), lambda i, be, bn, nu: (be[i], 0, 0))],
        out_specs=pl.BlockSpec((blk, d), lambda i, be, bn, nu: (i, 0)),
        scratch_shapes=[pltpu.VMEM((d, f2), BF16), pltpu.VMEM((f2 // 2, d), BF16)])
    return pl.pallas_call(
        _expert_body, grid_spec=grid_spec,
        out_shape=jax.ShapeDtypeStruct((s, d), F32),
        compiler_params=_params("arbitrary"), name="expert_blocks",
    )(blk_e, blk_new, n_used, x, gate_up, down)


def _qk_prep_body(x_ref, qw_ref, kw_ref, cos_ref, sin_ref, o_ref, *, n_qk_groups, head_dim):
    cos = cos_ref[...]
    sin = sin_ref[...]
    lane = lax.broadcasted_iota(jnp.int32, cos.shape, 1)
    first = (lane % (head_dim // 2)) < (head_dim // 4)
    for g in range(n_qk_groups):
        sl = slice(g * head_dim, (g + 1) * head_dim)
        x = x_ref[:, sl]
        w = qw_ref[...] if g < n_qk_groups // 2 else kw_ref[...]
        y = x * lax.rsqrt(jnp.mean(x * x, axis=-1, keepdims=True) + NORM_EPS) * w
        swapped = jnp.where(first, pltpu.roll(y, head_dim - head_dim // 4, 1), pltpu.roll(y, head_dim // 4, 1))
        o_ref[:, sl] = (y * cos + swapped * sin).astype(o_ref.dtype)
    rest = n_qk_groups * head_dim
    o_ref[:, rest:] = x_ref[:, rest:].astype(o_ref.dtype)


def qk_prep(proj, q_norm, k_norm, cos_t, sin_t, tl=256):
    t, n = proj.shape
    hd = q_norm.shape[0]
    n_groups = (2 * n // 3) // hd
    body = functools.partial(_qk_prep_body, n_qk_groups=n_groups, head_dim=hd)
    return pl.pallas_call(
        body, grid=(t // tl,),
        in_specs=[pl.BlockSpec((tl, n), lambda i: (i, 0)),
                  pl.BlockSpec((1, hd), lambda i: (0, 0)),
                  pl.BlockSpec((1, hd), lambda i: (0, 0)),
                  pl.BlockSpec((tl, hd), lambda i: (i, 0)),
                  pl.BlockSpec((tl, hd), lambda i: (i, 0))],
        out_specs=pl.BlockSpec((tl, n), lambda i: (i, 0)),
        out_shape=jax.ShapeDtypeStruct((t, n), BF16),
        compiler_params=_params("arbitrary"), name="qk_prep",
    )(proj, q_norm.reshape(1, hd), k_norm.reshape(1, hd), cos_t, sin_t)


def rope_tables(n_lat_tokens_per_sample, n_batch, n_ctx_tokens, head_dim):
    quarter = head_dim // 4
    inv_freq = ROPE_BASE ** (-jnp.arange(quarter, dtype=F32) / quarter)
    rows = n_lat_tokens_per_sample // GRID_W
    row = jnp.repeat(jnp.arange(rows, dtype=F32), GRID_W)
    col = jnp.tile(jnp.arange(GRID_W, dtype=F32), rows)
    ang_r = row[:, None] * inv_freq[None, :]
    ang_c = col[:, None] * inv_freq[None, :]
    cos = jnp.concatenate([jnp.cos(ang_r), jnp.cos(ang_r), jnp.cos(ang_c), jnp.cos(ang_c)], axis=-1)
    sin = jnp.concatenate([-jnp.sin(ang_r), jnp.sin(ang_r), -jnp.sin(ang_c), jnp.sin(ang_c)], axis=-1)
    cos = jnp.concatenate([jnp.tile(cos, (n_batch, 1)), jnp.ones((n_ctx_tokens, head_dim), F32)], axis=0)
    sin = jnp.concatenate([jnp.tile(sin, (n_batch, 1)), jnp.zeros((n_ctx_tokens, head_dim), F32)], axis=0)
    return cos, sin


def _diff_attn_body(lam_ref, q_ref, kl_ref, kc_ref, vl_ref, vc_ref, sw_ref, o_ref, *, head_dim, out_scale):
    lam = lam_ref[0]
    scale = head_dim ** -0.5
    contract_last = (((1,), (1,)), ((), ()))
    p_l = p_c = None
    for s in range(2):
        sl = slice(s * head_dim, (s + 1) * head_dim)
        qs = q_ref[:, sl]
        s_l = lax.dot_general(qs, kl_ref[:, sl], contract_last, preferred_element_type=F32) * scale
        s_c = lax.dot_general(qs, kc_ref[:, sl], contract_last, preferred_element_type=F32) * scale
        m = jnp.maximum(jnp.max(s_l, axis=-1, keepdims=True), jnp.max(s_c, axis=-1, keepdims=True))
        e_l = jnp.exp(s_l - m)
        e_c = jnp.exp(s_c - m)
        inv = 1.0 / (jnp.sum(e_l, axis=-1, keepdims=True) + jnp.sum(e_c, axis=-1, keepdims=True))
        if s == 0:
            p_l, p_c = e_l * inv, e_c * inv
        else:
            p_l, p_c = p_l - (lam * inv) * e_l, p_c - (lam * inv) * e_c
    o = (jnp.dot(p_l.astype(BF16), vl_ref[...], preferred_element_type=F32)
         + jnp.dot(p_c.astype(BF16), vc_ref[...], preferred_element_type=F32))
    y = o * lax.rsqrt(jnp.mean(o * o, axis=-1, keepdims=True) + NORM_EPS) * sw_ref[...]
    o_ref[...] = (y * out_scale).astype(o_ref.dtype)


def diff_attention(qkv, lmbda, sub_norm, n_batch, lat_len, ctx_len, n_heads, head_dim, out_scale, tq=256):
    hw = 2 * head_dim
    nq = lat_len // tq
    ctx_blk0 = n_batch * lat_len // ctx_len
    body = functools.partial(_diff_attn_body, head_dim=head_dim, out_scale=out_scale)
    return pl.pallas_call(
        body, grid=(n_batch, n_heads, nq),
        in_specs=[pl.BlockSpec(memory_space=pltpu.SMEM),
                  pl.BlockSpec((tq, hw), lambda b, h, i: (b * nq + i, h)),
                  pl.BlockSpec((lat_len, hw), lambda b, h, i: (b, n_heads + h)),
                  pl.BlockSpec((ctx_len, hw), lambda b, h, i: (ctx_blk0 + b, n_heads + h)),
                  pl.BlockSpec((lat_len, hw), lambda b, h, i: (b, 2 * n_heads + h)),
                  pl.BlockSpec((ctx_len, hw), lambda b, h, i: (ctx_blk0 + b, 2 * n_heads + h)),
                  pl.BlockSpec((1, hw), lambda b, h, i: (0, 0))],
        out_specs=pl.BlockSpec((tq, hw), lambda b, h, i: (b * nq + i, h)),
        out_shape=jax.ShapeDtypeStruct((n_batch * lat_len, n_heads * hw), BF16),
        compiler_params=_params("arbitrary", "arbitrary", "arbitrary"), name="diff_attention",
    )(lmbda.reshape(1), qkv, qkv, qkv, qkv, qkv, sub_norm.reshape(1, hw))


SCAN_TILE = 256
CHUNK = 64
SUB = 16
NEG_BIG = -1e30
NEG_INF = float("-inf")
DELTA_INV_PASSES = 1
DELTA_HEADS_PER_STEP = 4
HGRN_HEADS_PER_STEP = 2


def _sigmoid(x):
    return 1.0 / (1.0 + jnp.exp(-x))


def _dot(a, b):
    return jnp.dot(a.astype(BF16), b.astype(BF16), preferred_element_type=F32)


def _dot_nt(a, b):
    return lax.dot_general(a.astype(BF16), b.astype(BF16), (((1,), (1,)), ((), ())),
                           preferred_element_type=F32)


def _split3(x):
    hi = x.astype(BF16)
    r = x - hi.astype(F32)
    mid = r.astype(BF16)
    lo = (r - mid.astype(F32)).astype(BF16)
    return hi, mid, lo


def _dot_exact_lhs01(m01, x):
    hi, mid, lo = _split3(x)
    m = m01.astype(BF16)
    return (jnp.dot(m, hi, preferred_element_type=F32) + jnp.dot(m, mid, preferred_element_type=F32)
            + jnp.dot(m, lo, preferred_element_type=F32))


def _dot_exact_rhs01(x, m01):
    hi, mid, lo = _split3(x)
    m = m01.astype(BF16)
    return (jnp.dot(hi, m, preferred_element_type=F32) + jnp.dot(mid, m, preferred_element_type=F32)
            + jnp.dot(lo, m, preferred_element_type=F32))


def _dot3(a, b):
    ah = a.astype(BF16)
    al = (a - ah.astype(F32)).astype(BF16)
    bh = b.astype(BF16)
    bl = (b - bh.astype(F32)).astype(BF16)
    return (jnp.dot(ah, bh, preferred_element_type=F32) + jnp.dot(ah, bl, preferred_element_type=F32)
            + jnp.dot(al, bh, preferred_element_type=F32))


def _tile_masks(n, reverse):
    i = lax.broadcasted_iota(jnp.int32, (n, n), 0)
    j = lax.broadcasted_iota(jnp.int32, (n, n), 1)
    same = (i // CHUNK) == (j // CHUNK)
    if reverse:
        return same & (i <= j), same & (i < j)
    return same & (i >= j), same & (i > j)


def _segment_tile(b, s, reverse, nl, nc, ctx_tile0):
    if reverse:
        return jnp.where(s < nc, ctx_tile0 + b * nc + (nc - 1 - s), b * nl + (nl - 1 - (s - nc)))
    return jnp.where(s < nc, ctx_tile0 + b * nc + s, b * nl + (s - nc))


def _dn_prep_body(x_ref, prev_ref, next_ref, w_ref, o_ref, *, tiles_per_lat_seg, tiles_per_ctx_seg, n_lat_tiles,
                  head_dim, q_scale):
    i = pl.program_id(0)
    j = pl.program_id(1)
    is_lat = i < n_lat_tiles
    pos = jnp.where(is_lat, i % tiles_per_lat_seg, (i - n_lat_tiles) % tiles_per_ctx_seg)
    seg_first = pos == 0
    seg_last = pos == jnp.where(is_lat, tiles_per_lat_seg, tiles_per_ctx_seg) - 1
    x = x_ref[...]
    tl = x.shape[0]
    prev = jnp.where(seg_first, 0.0, prev_ref[...])
    nxt = jnp.where(seg_last, 0.0, next_ref[...])
    xp = jnp.concatenate([prev, x, nxt], axis=0)
    w = w_ref[...]
    n_taps = 5
    acc = None
    for t in range(n_taps):
        off = 8 + t - n_taps // 2
        term = xp[off:off + tl] * w[t:t + 1]
        acc = term if acc is None else acc + term
    y = _silu(acc)
    scale = jnp.where(j == 0, q_scale, 1.0)
    outs = []
    for h in range(y.shape[1] // head_dim):
        yh = y[:, h * head_dim:(h + 1) * head_dim]
        nrm = lax.rsqrt(jnp.sum(yh * yh, axis=-1, keepdims=True) + 1e-6) * scale
        outs.append(yh * jnp.where(j == 2, 1.0, nrm))
    o_ref[...] = jnp.concatenate(outs, axis=1)


def dn_prep(proj_dn, conv_w_t, geom, ctx_len, width, head_dim):
    n_lat, lat_len, n_batch = geom
    t = proj_dn.shape[0]
    tl = SCAN_TILE
    rows8 = tl // 8
    n_tiles = t // tl
    body = functools.partial(_dn_prep_body, tiles_per_lat_seg=lat_len // tl, tiles_per_ctx_seg=ctx_len // tl,
                             n_lat_tiles=n_lat // tl,
                             head_dim=head_dim, q_scale=head_dim ** -0.5)
    last8 = t // 8 - 1
    return pl.pallas_call(
        body, grid=(n_tiles, 3),
        in_specs=[pl.BlockSpec((tl, width), lambda i, j: (i, j)),
                  pl.BlockSpec((8, width), lambda i, j: (jnp.maximum(i * rows8 - 1, 0), j)),
                  pl.BlockSpec((8, width), lambda i, j: (jnp.minimum((i + 1) * rows8, last8), j)),
                  pl.BlockSpec((8, width), lambda i, j: (0, j))],
        out_specs=pl.BlockSpec((tl, width), lambda i, j: (i, j)),
        out_shape=jax.ShapeDtypeStruct((t, 3 * width), F32),
        compiler_params=_params("arbitrary", "arbitrary"), name="dn_prep",
    )(proj_dn, proj_dn, proj_dn, conv_w_t)


def _softplus(x):
    return jnp.maximum(x, 0.0) + jnp.log(1.0 + jnp.exp(-jnp.abs(x)))


def _gate_prep_body(h_ref, wc_ref, wr_ref, alog_c_ref, dtb_c_ref, alog_r_ref, dtb_r_ref, gc_ref, gr_ref, *, n_heads):
    h = h_ref[...]
    tl = h.shape[0]
    nd = 2 * n_heads
    raw_c = jnp.dot(h, wc_ref[...].astype(BF16), preferred_element_type=F32)
    raw_r = lax.dot_general(wr_ref[...].astype(BF16), h, (((1,), (1,)), ((), ())),
                            preferred_element_type=F32)
    g_c = -jnp.exp(alog_c_ref[...]) * _softplus(raw_c[:, :nd] + dtb_c_ref[...])
    g_r = -jnp.exp(alog_r_ref[...]) * _softplus(raw_r[:nd, :] + dtb_r_ref[...])
    incl_f, _ = _tile_masks(tl, False)
    incl_b, _ = _tile_masks(tl, True)
    one_f = jnp.where(incl_f, 1.0, 0.0)
    one_b = jnp.where(incl_b, 1.0, 0.0)
    cum_c = jnp.concatenate([_dot_exact_lhs01(one_f, g_c[:, :n_heads]),
                             _dot_exact_lhs01(one_b, g_c[:, n_heads:])], axis=1)
    cum_r = jnp.concatenate([_dot_exact_rhs01(g_r[:n_heads, :], one_b),
                             _dot_exact_rhs01(g_r[n_heads:, :], one_f)], axis=0)
    beta_c = _sigmoid(raw_c[:, nd:])
    gc_ref[...] = jnp.concatenate([cum_c, beta_c], axis=1)
    gr_ref[...] = jnp.concatenate([cum_r, jnp.zeros_like(cum_r)], axis=0)


def gate_prep(h_bf, w_gate, a_log, dt_bias, n_heads):
    t, d = h_bf.shape
    tl = SCAN_TILE
    nd = 2 * n_heads
    body = functools.partial(_gate_prep_body, n_heads=n_heads)
    full = lambda shape: pl.BlockSpec(shape, lambda i: (0, 0))
    return pl.pallas_call(
        body, grid=(t // tl,),
        in_specs=[pl.BlockSpec((tl, d), lambda i: (i, 0)), full((d, 2 * nd)), full((2 * nd, d)),
                  full((1, nd)), full((1, nd)), full((nd, 1)), full((nd, 1))],
        out_specs=[pl.BlockSpec((tl, 2 * nd), lambda i: (i, 0)), pl.BlockSpec((2 * nd, tl), lambda i: (0, i))],
        out_shape=[jax.ShapeDtypeStruct((t, 2 * nd), F32), jax.ShapeDtypeStruct((2 * nd, t), F32)],
        compiler_params=_params("arbitrary"), name="gate_prep",
    )(h_bf, w_gate, w_gate.T, a_log.reshape(1, nd), dt_bias.reshape(1, nd),
      a_log.reshape(nd, 1), dt_bias.reshape(nd, 1))


def _select_col(x, idx):
    lane = lax.broadcasted_iota(jnp.int32, x.shape, 1)
    return jnp.sum(jnp.where(lane == idx, x, 0.0), axis=1, keepdims=True)


def _select_row(x, idx):
    row = lax.broadcasted_iota(jnp.int32, x.shape, 0)
    return jnp.sum(jnp.where(row == idx, x, 0.0), axis=0, keepdims=True)


def _run_interleaved(chains):
    chains = list(chains)
    while chains:
        alive = []
        for ch in chains:
            try:
                next(ch)
                alive.append(ch)
            except StopIteration:
                pass
        chains = alive


def _delta_chain(q, k, v, gc_col, gc_row, beta_col, s_ref, o_ref, cols, reverse, inv_passes):
    tl, kd_ = k.shape
    n_chunks = tl // CHUNK
    incl, strict = _tile_masks(tl, reverse)
    decay = jnp.exp(jnp.where(incl, gc_col - gc_row, NEG_BIG))
    kb, qb = k.astype(BF16), q.astype(BF16)
    kkt = _dot_nt(kb, kb)
    qkt = _dot_nt(qb, kb)
    yield
    x = jnp.where(strict, kkt * (-beta_col) * decay, 0.0)
    dot_inv = _dot3 if inv_passes == 3 else _dot
    ri = lax.broadcasted_iota(jnp.int32, (tl, tl), 0)
    ci = lax.broadcasted_iota(jnp.int32, (tl, tl), 1)
    r = jnp.where(ri == ci, 1.0, 0.0) + x
    n_sq = int(math.log2(CHUNK)) - 1
    for _ in range(n_sq):
        x = dot_inv(x, x)
        r = r + dot_inv(r, x)
        yield
    e_g = jnp.exp(gc_col)
    rhs = jnp.concatenate([v * beta_col, k * (beta_col * e_g)], axis=1)
    sol = dot_inv(r, rhs)
    yield
    u0, w = sol[:, :v.shape[1]], sol[:, v.shape[1]:]
    attn = (qkt * decay).astype(BF16)
    o0 = _dot(attn, u0)
    qe = q * e_g - _dot(attn, w)
    tot_rows = []
    for c in range(n_chunks):
        last = c * CHUNK if reverse else c * CHUNK + CHUNK - 1
        tot_rows.append(jnp.broadcast_to(gc_col[last:last + 1, :], (CHUNK, 1)))
    tot = jnp.concatenate(tot_rows, axis=0)
    kdec_t = jnp.transpose(k * jnp.exp(tot - gc_col))
    wu = jnp.concatenate([-w, u0], axis=1).astype(BF16)
    lane = lax.broadcasted_iota(jnp.int32, kdec_t.shape, 1)
    pns = [_dot(jnp.where((lane // CHUNK) == c, kdec_t, 0.0), wu) for c in range(n_chunks)]
    yield
    order = range(n_chunks - 1, -1, -1) if reverse else range(n_chunks)
    for c in order:
        rows = slice(c * CHUNK, (c + 1) * CHUNK)
        pn = pns[c]
        lhs = jnp.concatenate([qe[rows], pn[:, :kd_]], axis=0)
        s = s_ref[...]
        res = _dot(lhs, s)
        o_ref[rows, cols] = o0[rows] + res[:CHUNK]
        last = c * CHUNK if reverse else c * CHUNK + CHUNK - 1
        gl = jnp.exp(gc_col[last:last + 1, :])
        s_ref[...] = gl * s + res[CHUNK:] + pn[:, kd_:]
        yield


def _delta_body(qf_ref, kf_ref, vf_ref, gcf_ref, grf_ref, qb_ref, kb_ref, vb_ref, gcb_ref, grb_ref,
                of_ref, ob_ref, sf_ref, sb_ref, *, n_heads, head_dim, inv_passes):
    hg = pl.program_id(1)
    heads_per_step = sf_ref.shape[0]

    @pl.when(pl.program_id(2) == 0)
    def _():
        sf_ref[...] = jnp.zeros_like(sf_ref)
        sb_ref[...] = jnp.zeros_like(sb_ref)

    chains = []
    for g in range(heads_per_step):
        cols = slice(g * head_dim, (g + 1) * head_dim)
        for reverse, (q_ref, k_ref, v_ref, gc_ref, gr_ref, o_ref, s_ref) in enumerate(
                [(qf_ref, kf_ref, vf_ref, gcf_ref, grf_ref, of_ref, sf_ref),
                 (qb_ref, kb_ref, vb_ref, gcb_ref, grb_ref, ob_ref, sb_ref)]):
            idx = reverse * n_heads + hg * heads_per_step + g
            gcs = gc_ref[...]
            gc_col = _select_col(gcs, idx)
            beta_col = _select_col(gcs, 2 * n_heads + idx)
            gc_row = _select_row(gr_ref[...], idx)
            chains.append(_delta_chain(q_ref[:, cols], k_ref[:, cols], v_ref[:, cols], gc_col, gc_row, beta_col,
                                       s_ref.at[g], o_ref, cols, bool(reverse), inv_passes))
    _run_interleaved(chains)


def delta_scan(qkv, gc, gr, geom, ctx_len, n_heads, head_dim, inv_passes=3, heads_per_step=DELTA_HEADS_PER_STEP):
    n_lat, lat_len, n_batch = geom
    t = qkv.shape[0]
    tl = SCAN_TILE
    nl, nc, ctx0 = lat_len // tl, ctx_len // tl, n_lat // tl
    n_hg = n_heads // heads_per_step
    gw = heads_per_step * head_dim
    tile = functools.partial(_segment_tile, nl=nl, nc=nc, ctx_tile0=ctx0)
    specs = []
    for reverse in (False, True):
        tix = functools.partial(tile, reverse=reverse)
        specs += [pl.BlockSpec((tl, gw), lambda b, h, s, tix=tix: (tix(b, s), h)),
                  pl.BlockSpec((tl, gw), lambda b, h, s, tix=tix: (tix(b, s), n_hg + h)),
                  pl.BlockSpec((tl, gw), lambda b, h, s, tix=tix: (tix(b, s), 2 * n_hg + h)),
                  pl.BlockSpec((tl, 4 * n_heads), lambda b, h, s, tix=tix: (tix(b, s), 0)),
                  pl.BlockSpec((4 * n_heads, tl), lambda b, h, s, tix=tix: (0, tix(b, s)))]
    out_specs = [pl.BlockSpec((tl, gw), lambda b, h, s, tix=functools.partial(tile, reverse=r): (tix(b, s), h))
                 for r in (False, True)]
    body = functools.partial(_delta_body, n_heads=n_heads, head_dim=head_dim, inv_passes=inv_passes)
    return pl.pallas_call(
        body, grid=(n_batch, n_hg, nl + nc), in_specs=specs, out_specs=out_specs,
        out_shape=[jax.ShapeDtypeStruct((t, n_heads * head_dim), F32)] * 2,
        scratch_shapes=[pltpu.VMEM((heads_per_step, head_dim, head_dim), F32)] * 2,
        compiler_params=_params("arbitrary", "arbitrary", "arbitrary"), name="delta_scan",
    )(qkv, qkv, qkv, gc, gr, qkv, qkv, qkv, gc, gr)


def _hgrn_diag(hq_ref, hf_ref, hv_ref, cols, lb, stage_ref, diag_ref, reverse):
    n_sub = hq_ref.shape[0] // SUB
    for n, ref in enumerate((hq_ref, hf_ref, hv_ref)):
        stage_ref[n] = ref[:, cols]

    def slab(n, r):
        return stage_ref.at[n][pl.ds(r, n_sub, stride=SUB), :]

    q_x = [_silu(slab(0, r)) for r in range(SUB)]
    f_x = [lb + (1.0 - lb) * _sigmoid(slab(1, r)) for r in range(SUB)]
    k_x = [1.0 - f for f in f_x]
    lf_x = [jnp.log(f) for f in f_x]
    v_x = [slab(2, r) for r in range(SUB)]
    p_x = [None] * SUB
    scan_rows = range(SUB - 1, -1, -1) if reverse else range(SUB)
    acc = None
    for r in scan_rows:
        acc = lf_x[r] if acc is None else acc + lf_x[r]
        p_x[r] = acc
    pairs = [(i, j) for i in range(SUB) for j in range(SUB) if (j >= i if reverse else j <= i)]
    terms = []
    for i, j in pairs:
        qk = q_x[i] * k_x[j]
        terms.append((qk if i == j else qk * jnp.exp(p_x[i] - p_x[j])).astype(BF16))
    kd_ = terms[0].shape[1]
    a_rep = jnp.dot(jnp.concatenate(terms, axis=0), jnp.ones((kd_, kd_), BF16), preferred_element_type=F32)
    o_x = [None] * SUB
    for n, (i, j) in enumerate(pairs):
        contrib = a_rep[n * n_sub:(n + 1) * n_sub] * v_x[j]
        o_x[i] = contrib if o_x[i] is None else o_x[i] + contrib
    for r in range(SUB):
        diag_ref[pl.ds(r, n_sub, stride=SUB), :] = o_x[r]


def _hgrn_chain(hq_ref, hf_ref, hv_ref, cols, lb, st_ref, o_ref, stage_ref, diag_ref, reverse):
    hq, hf, hv = hq_ref[:, cols], hf_ref[:, cols], hv_ref[:, cols]
    tl, kd_ = hq.shape
    n_chunks = tl // CHUNK
    n_sub = tl // SUB
    sub_per_chunk = CHUNK // SUB
    q = _silu(hq)
    f = lb + (1.0 - lb) * _sigmoid(hf)
    k = 1.0 - f
    lf = jnp.log(f)
    incl, _ = _tile_masks(tl, reverse)
    cum = _dot_exact_lhs01(jnp.where(incl, 1.0, 0.0), lf)
    excl = cum - lf

    def bcast_rows(src, row, n):
        return jnp.broadcast_to(src[row:row + 1, :], (n, kd_))

    chunk_last = [(c * CHUNK if reverse else c * CHUNK + CHUNK - 1) for c in range(n_chunks)]
    tot = jnp.concatenate([bcast_rows(cum, chunk_last[c], CHUNK) for c in range(n_chunks)], axis=0)
    sub_first = [(m * SUB + SUB - 1 if reverse else m * SUB) for m in range(n_sub)]
    r_sub = jnp.concatenate([bcast_rows(excl, sub_first[m], SUB) for m in range(n_sub)], axis=0)
    q_t = q * jnp.exp(cum - r_sub)
    qd = q * jnp.exp(cum)
    kd = k * jnp.exp(tot - cum)
    vb = hv.astype(BF16)

    i = lax.broadcasted_iota(jnp.int32, (tl, tl), 0)
    j = lax.broadcasted_iota(jnp.int32, (tl, tl), 1)
    same = (i // CHUNK) == (j // CHUNK)
    pos_i = (i % CHUNK) // SUB
    pos_j = (j % CHUNK) // SUB
    if reverse:
        pos_i, pos_j = sub_per_chunk - 1 - pos_i, sub_per_chunk - 1 - pos_j
    a_off = jnp.zeros((tl, tl), F32)
    for lvl in range(1, sub_per_chunk):
        ref_rows = []
        for c in range(n_chunks):
            m = c * sub_per_chunk + (sub_per_chunk - 1 - lvl if reverse else lvl)
            ref_rows.append(bcast_rows(excl, sub_first[m], CHUNK))
        r_lvl = jnp.concatenate(ref_rows, axis=0)
        k_t = k * jnp.exp(jnp.minimum(r_lvl - cum, 0.0))
        a_l = _dot_nt(q_t, k_t)
        a_off = a_off + jnp.where(same & (pos_i == lvl) & (pos_j < lvl), a_l, 0.0)
        yield
    o_intra = _dot(a_off, vb)
    yield

    _hgrn_diag(hq_ref, hf_ref, hv_ref, cols, lb, stage_ref, diag_ref, reverse)
    yield
    o_intra = o_intra + diag_ref[...]

    v_t = jnp.transpose(hv)
    lane = lax.broadcasted_iota(jnp.int32, v_t.shape, 1)
    kdb = kd.astype(BF16)
    n_ts = [_dot(jnp.where((lane // CHUNK) == c, v_t, 0.0), kdb) for c in range(n_chunks)]
    yield
    order = range(n_chunks - 1, -1, -1) if reverse else range(n_chunks)
    for c in order:
        rows = slice(c * CHUNK, (c + 1) * CHUNK)
        st = st_ref[...]
        o_ref[rows, cols] = o_intra[rows] + _dot_nt(qd[rows], st)
        st_ref[...] = st * jnp.exp(cum[chunk_last[c]:chunk_last[c] + 1, :]) + n_ts[c]
        yield


def _hgrn_body(qf_ref, ff_ref, vf_ref, qb_ref, fb_ref, vb_ref, lb_ref, of_ref, ob_ref, sf_ref, sb_ref, stage_ref,
               diag_ref, *, key_dim):
    heads_per_step = sf_ref.shape[0]

    @pl.when(pl.program_id(2) == 0)
    def _():
        sf_ref[...] = jnp.zeros_like(sf_ref)
        sb_ref[...] = jnp.zeros_like(sb_ref)

    chains = []
    for g in range(heads_per_step):
        cols = slice(g * key_dim, (g + 1) * key_dim)
        lb = lb_ref[:, cols]
        chains.append(_hgrn_chain(qf_ref, ff_ref, vf_ref, cols, lb, sf_ref.at[g], of_ref,
                                  stage_ref.at[2 * g], diag_ref.at[2 * g], False))
        chains.append(_hgrn_chain(qb_ref, fb_ref, vb_ref, cols, lb, sb_ref.at[g], ob_ref,
                                  stage_ref.at[2 * g + 1], diag_ref.at[2 * g + 1], True))
    _run_interleaved(chains)


def hgrn_scan(proj_hg, lb, geom, ctx_len, n_heads, key_dim, heads_per_step=HGRN_HEADS_PER_STEP):
    n_lat, lat_len, n_batch = geom
    t = proj_hg.shape[0]
    tl = SCAN_TILE
    nl, nc, ctx0 = lat_len // tl, ctx_len // tl, n_lat // tl
    n_hg = n_heads // heads_per_step
    gw = heads_per_step * key_dim
    tile = functools.partial(_segment_tile, nl=nl, nc=nc, ctx_tile0=ctx0)
    specs = []
    for reverse in (False, True):
        tix = functools.partial(tile, reverse=reverse)
        fcol = (1 + int(reverse)) * n_hg
        specs += [pl.BlockSpec((tl, gw), lambda b, h, s, tix=tix: (tix(b, s), h)),
                  pl.BlockSpec((tl, gw), lambda b, h, s, tix=tix, fcol=fcol: (tix(b, s), fcol + h)),
                  pl.BlockSpec((tl, gw), lambda b, h, s, tix=tix: (tix(b, s), 3 * n_hg + h))]
    specs.append(pl.BlockSpec((1, gw), lambda b, h, s: (0, h)))
    out_specs = [pl.BlockSpec((tl, gw), lambda b, h, s, tix=functools.partial(tile, reverse=r): (tix(b, s), h))
                 for r in (False, True)]
    return pl.pallas_call(
        functools.partial(_hgrn_body, key_dim=key_dim), grid=(n_batch, n_hg, nl + nc),
        in_specs=specs, out_specs=out_specs,
        out_shape=[jax.ShapeDtypeStruct((t, n_heads * key_dim), F32)] * 2,
        scratch_shapes=[pltpu.VMEM((heads_per_step, key_dim, key_dim), F32)] * 2
        + [pltpu.VMEM((2 * heads_per_step, 3, tl, key_dim), F32), pltpu.VMEM((2 * heads_per_step, tl, key_dim), F32)],
        compiler_params=_params("arbitrary", "arbitrary", "arbitrary"), name="hgrn_scan",
    )(proj_hg, proj_hg, proj_hg, proj_hg, proj_hg, proj_hg, lb)


def _mix_out2_body(df_ref, db_ref, hf_ref, hb_ref, z_ref, og_ref, dnw_ref, hgw_ref, o_ref, *, head_dim):
    def normed(o, nw):
        outs = []
        for h in range(o.shape[1] // head_dim):
            oh = o[:, h * head_dim:(h + 1) * head_dim]
            outs.append(oh * lax.rsqrt(jnp.mean(oh * oh, axis=-1, keepdims=True) + NORM_EPS) * nw)
        return jnp.concatenate(outs, axis=1)

    dn = normed(df_ref[...] + db_ref[...], dnw_ref[...]) * _silu(z_ref[...])
    hg = normed(hf_ref[...] + hb_ref[...], hgw_ref[...]) * _sigmoid(og_ref[...])
    half = dn.shape[1]
    o_ref[:, :half] = dn.astype(o_ref.dtype)
    o_ref[:, half:] = hg.astype(o_ref.dtype)


def mix_out(dn_f, dn_b, hg_f, hg_b, z_src, z_blk, og_src, og_blk, dn_norm, hg_norm, head_dim):
    t, w = dn_f.shape
    tl = SCAN_TILE
    row = lambda i: (i, 0)
    body = functools.partial(_mix_out2_body, head_dim=head_dim)
    return pl.pallas_call(
        body, grid=(t // tl,),
        in_specs=[pl.BlockSpec((tl, w), row)] * 4
        + [pl.BlockSpec((tl, w), lambda i: (i, z_blk)), pl.BlockSpec((tl, w), lambda i: (i, og_blk)),
           pl.BlockSpec((1, head_dim), lambda i: (0, 0)), pl.BlockSpec((1, head_dim), lambda i: (0, 0))],
        out_specs=pl.BlockSpec((tl, 2 * w), row),
        out_shape=jax.ShapeDtypeStruct((t, 2 * w), BF16),
        compiler_params=_params("arbitrary"), name="mix_out",
    )(dn_f, dn_b, hg_f, hg_b, z_src, og_src, dn_norm.reshape(1, head_dim), hg_norm.reshape(1, head_dim))


def _first_max(x, ids, n):
    m = jnp.max(x, axis=0, keepdims=True)
    first = jnp.min(jnp.where(x == m, ids, n), axis=0, keepdims=True)
    return m, first


def _route_body(lg_ref, bias_ref, idx_ref, rank_ref, w_ref, cnt_ref, carry_ref, *,
                n_groups, topk_groups, top_k, scale):
    i = pl.program_id(0)

    @pl.when(i == 0)
    def _():
        carry_ref[...] = jnp.zeros_like(carry_ref)

    lg = lg_ref[...]
    n_exp, tl = lg.shape
    per = n_exp // n_groups
    scores = 1.0 / (1.0 + jnp.exp(-lg))
    biased = scores + bias_ref[...]
    sub = lax.broadcasted_iota(jnp.int32, (per, tl), 0)
    g_rows = []
    for g in range(n_groups):
        xg = biased[g * per:(g + 1) * per]
        m1, i1 = _first_max(xg, sub, per)
        m2 = jnp.max(jnp.where(sub == i1, NEG_INF, xg), axis=0, keepdims=True)
        g_rows.append(m1 + m2)
    gscore = jnp.concatenate(g_rows, axis=0)
    gid = lax.broadcasted_iota(jnp.int32, (n_groups, tl), 0)
    gsel = jnp.zeros((n_groups, tl), jnp.bool_)
    for _ in range(topk_groups):
        _, first = _first_max(gscore, gid, n_groups)
        hit = gid == first
        gsel = gsel | hit
        gscore = jnp.where(hit, NEG_INF, gscore)
    eid = lax.broadcasted_iota(jnp.int32, (n_exp, tl), 0)
    gmask = jnp.concatenate([jnp.broadcast_to(gsel[g:g + 1], (per, tl)) for g in range(n_groups)], axis=0)
    masked = jnp.where(gmask, biased, NEG_INF)
    sel = jnp.zeros((n_exp, tl), jnp.bool_)
    hits, firsts = [], []
    for _ in range(top_k):
        _, first = _first_max(masked, eid, n_exp)
        hit = eid == first
        hits.append(hit)
        firsts.append(first)
        sel = sel | hit
        masked = jnp.where(hit, NEG_INF, masked)
    self = jnp.where(sel, 1.0, 0.0)
    ti = lax.broadcasted_iota(jnp.int32, (tl, tl), 0)
    tj = lax.broadcasted_iota(jnp.int32, (tl, tl), 1)
    before = jnp.where(ti < tj, 1.0, 0.0).astype(BF16)
    carry = carry_ref[...]
    rank_full = jnp.dot(self.astype(BF16), before, preferred_element_type=F32) + carry
    w_rows = [jnp.sum(jnp.where(hit, scores, 0.0), axis=0, keepdims=True) for hit in hits]
    r_rows = [jnp.sum(jnp.where(hit, rank_full, 0.0), axis=0, keepdims=True) for hit in hits]
    w8 = jnp.concatenate(w_rows, axis=0)
    idx_ref[...] = jnp.concatenate(firsts, axis=0)
    rank_ref[...] = jnp.concatenate(r_rows, axis=0).astype(jnp.int32)
    w_ref[...] = w8 / jnp.sum(w8, axis=0, keepdims=True) * scale
    carry = carry + jnp.sum(self, axis=1, keepdims=True)
    carry_ref[...] = carry
    cnt_ref[...] = carry.astype(jnp.int32)


def route(logits_t, router_b, tl=256):
    n_exp, t = logits_t.shape
    body = functools.partial(_route_body, n_groups=N_GROUPS, topk_groups=TOPK_GROUPS, top_k=TOP_K,
                             scale=ROUTED_SCALE)
    tok_spec = pl.BlockSpec((TOP_K, tl), lambda i: (0, i))
    return pl.pallas_call(
        body, grid=(t // tl,),
        in_specs=[pl.BlockSpec((n_exp, tl), lambda i: (0, i)), pl.BlockSpec((n_exp, 1), lambda i: (0, 0))],
        out_specs=[tok_spec, tok_spec, tok_spec, pl.BlockSpec((n_exp, 1), lambda i: (0, 0))],
        out_shape=[jax.ShapeDtypeStruct((TOP_K, t), jnp.int32), jax.ShapeDtypeStruct((TOP_K, t), jnp.int32),
                   jax.ShapeDtypeStruct((TOP_K, t), F32), jax.ShapeDtypeStruct((n_exp, 1), jnp.int32)],
        scratch_shapes=[pltpu.VMEM((n_exp, 1), F32)],
        compiler_params=_params("arbitrary"), name="route",
    )(logits_t, router_b.reshape(n_exp, 1))


def _row_copy(src_ref, src_row, dst_ref, dst_row, sem):
    return pltpu.make_async_copy(src_ref.at[pl.ds(src_row, 1)], dst_ref.at[pl.ds(dst_row, 1)], sem)


def _dispatch_body(dest_ref, h_ref, init_ref, xs_ref, sem, *, top_k):
    del init_ref
    tl = h_ref.shape[0]

    def issue(r, carry):
        for k in range(top_k):
            _row_copy(h_ref, r, xs_ref, dest_ref[r * top_k + k], sem.at[0]).start(priority=k % 2)
        return carry

    lax.fori_loop(0, tl, issue, 0, unroll=DMA_LOOP_UNROLL)

    def drain(r, carry):
        for k in range(top_k):
            _row_copy(h_ref, 0, xs_ref, 0, sem.at[0]).wait()
        return carry

    lax.fori_loop(0, tl, drain, 0, unroll=DMA_LOOP_UNROLL)


def dispatch(h_pk, dest_flat, n_slot, top_k, tl=256):
    t, w = h_pk.shape
    return pl.pallas_call(
        functools.partial(_dispatch_body, top_k=top_k), grid=(t // tl,),
        in_specs=[pl.BlockSpec((tl * top_k,), lambda i: (i,), memory_space=pltpu.SMEM),
                  pl.BlockSpec((tl, w), lambda i: (i, 0)),
                  pl.BlockSpec(memory_space=pl.ANY)],
        out_specs=pl.BlockSpec(memory_space=pl.ANY),
        out_shape=jax.ShapeDtypeStruct((n_slot, w), h_pk.dtype),
        scratch_shapes=[pltpu.SemaphoreType.DMA((1,))],
        input_output_aliases={2: 0},
        compiler_params=_params("arbitrary"), name="dispatch",
    )(dest_flat, h_pk, jnp.zeros((n_slot, w), h_pk.dtype))


def _combine_body(dest_ref, dest_next_ref, w_ref, sh_ref, x_ref, gate_ref, y_ref, o_ref, buf_ref, sem, *, top_k):
    i = pl.program_id(0)
    n = pl.num_programs(0)
    tl = x_ref.shape[0]

    def gather(d_ref, slot):
        def issue(r, carry):
            for k in range(top_k):
                _row_copy(y_ref, d_ref[r * top_k + k], buf_ref.at[slot], k * tl + r,
                          sem.at[slot]).start(priority=k % 2)
            return carry
        lax.fori_loop(0, tl, issue, 0, unroll=DMA_LOOP_UNROLL)

    @pl.when(i == 0)
    def _():
        gather(dest_ref, 0)

    @pl.when(i + 1 < n)
    def _():
        gather(dest_next_ref, (i + 1) % 2)

    slot = i % 2

    def drain(r, carry):
        for k in range(top_k):
            _row_copy(y_ref, 0, buf_ref.at[slot], 0, sem.at[slot]).wait()
        return carry

    lax.fori_loop(0, tl, drain, 0, unroll=DMA_LOOP_UNROLL)
    acc = sh_ref[...]
    w = w_ref[...]
    for k in range(top_k):
        acc = acc + buf_ref[slot, pl.ds(k * tl, tl), :] * w[:, k:k + 1]
    o_ref[...] = x_ref[...] + gate_ref[0] * acc


def combine(y_slot, dest_flat, w, shared, x, mod3, gate_idx, geom, tl=128):
    t, d = x.shape
    top_k = w.shape[1]
    n_lat, lat_len, n_batch = geom
    n_tiles = t // tl
    row = functools.partial(_mod_row, tile_rows=tl, n_lat=n_lat, lat_len=lat_len, n_batch=n_batch)
    return pl.pallas_call(
        functools.partial(_combine_body, top_k=top_k), grid=(n_tiles,),
        in_specs=[pl.BlockSpec((tl * top_k,), lambda i: (i,), memory_space=pltpu.SMEM),
                  pl.BlockSpec((tl * top_k,), lambda i: (jnp.minimum(i + 1, n_tiles - 1),),
                               memory_space=pltpu.SMEM),
                  pl.BlockSpec((tl, top_k), lambda i: (i, 0)),
                  pl.BlockSpec((tl, d), lambda i: (i, 0)),
                  pl.BlockSpec((tl, d), lambda i: (i, 0)),
                  pl.BlockSpec((1, 1, d), lambda i: (row(i) * 6 + gate_idx, 0, 0)),
                  pl.BlockSpec(memory_space=pl.ANY)],
        out_specs=pl.BlockSpec((tl, d), lambda i: (i, 0)),
        out_shape=jax.ShapeDtypeStruct((t, d), F32),
        scratch_shapes=[pltpu.VMEM((2, top_k * tl, d), F32), pltpu.SemaphoreType.DMA((2,))],
        compiler_params=_params("arbitrary"), name="combine",
    )(dest_flat, dest_flat, w, shared, x, mod3, y_slot)


def moe_ffn(x, h_pk, logits_t, router_b, exp_gate_up, exp_down, shared_gate_up, shared_down, mod3, geom):
    t = h_pk.shape[0]
    e_count = exp_gate_up.shape[0]
    blk = EXPERT_SLOT_BLOCK
    idx8, rank8, w8, counts = route(logits_t, router_b)
    counts = counts[:, 0]
    padded = (counts + blk - 1) // blk * blk
    pad_end = jnp.cumsum(padded)
    pad_start = pad_end - padded
    start8 = jnp.sum(jnp.where(idx8[None] == jnp.arange(e_count, dtype=jnp.int32)[:, None, None],
                               pad_start[:, None, None], 0), axis=0)
    dest_flat = (start8 + rank8).T.reshape(-1)
    n_blk = (t * TOP_K + e_count * (blk - 1)) // blk + 1
    n_slot = n_blk * blk
    blk_starts = jnp.arange(n_blk, dtype=jnp.int32) * blk
    blk_e = jnp.sum((pad_end[None, :] <= blk_starts[:, None]).astype(jnp.int32), axis=1)
    blk_e = jnp.minimum(blk_e, e_count - 1)
    blk_new = jnp.concatenate([jnp.ones((1,), jnp.int32), (blk_e[1:] != blk_e[:-1]).astype(jnp.int32)])
    n_used = (pad_end[-1] // blk).astype(jnp.int32).reshape(1)
    x_sorted = dispatch(h_pk, dest_flat, n_slot, TOP_K)
    y_slot = expert_blocks(x_sorted, exp_gate_up, exp_down, blk_e, blk_new, n_used, blk)
    one = jnp.ones((t // blk,), jnp.int32)
    shared = expert_blocks(h_pk, shared_gate_up[None], shared_down[None],
                           jnp.zeros((t // blk,), jnp.int32), one.at[1:].set(0),
                           jnp.full((1,), t // blk, jnp.int32), blk)
    return combine(y_slot, dest_flat, w8.T, shared, x, mod3, 5, geom)


def kernel(x, c, ctx, c_ctx, hg_lb_logits, l0_mod_w, l0_mod_b, l0_norm1, l0_norm2, l0_w_in, l0_dn_conv, l0_dn_a_log, l0_dn_dt_bias, l0_dn_norm, l0_hg_norm, l0_w_out, l0_router_w, l0_router_b, l0_exp_gate_up, l0_exp_down, l0_shared_gate_up, l0_shared_down, l1_mod_w, l1_mod_b, l1_norm1, l1_norm2, l1_w_in, l1_q_norm, l1_k_norm, l1_lambda, l1_sub_norm, l1_w_out, l1_router_w, l1_router_b, l1_exp_gate_up, l1_exp_down, l1_shared_gate_up, l1_shared_down):
    n_batch, lat_len, d = x.shape
    ctx_len = ctx.shape[1]
    n_lat = n_batch * lat_len
    n_ctx = n_batch * ctx_len
    geom = (n_lat, lat_len, n_batch)
    geom_lat_only = (n_lat, lat_len, n_batch)

    xs = jnp.concatenate([x.reshape(n_lat, d), ctx.reshape(n_ctx, d)], axis=0)
    cond = jnp.concatenate([c, c_ctx[None], jnp.zeros((MOD_ROWS - n_batch - 1, d), F32)], axis=0)

    mod3 = modulation(cond, l0_mod_w, l0_mod_b).reshape(MOD_ROWS * 6, 1, d)
    h = adaln(xs, l0_norm1, mod3, 0, geom)
    n_dn = IN0_SIZES[0] + IN0_SIZES[1]
    n_gate = IN0_SIZES[2] + IN0_SIZES[3]
    proj_dn = matmul(h, l0_w_in, n_cols=n_dn)
    proj_hg = matmul(h, l0_w_in[:, n_dn + n_gate:])
    gc, gr = gate_prep(h, l0_w_in[:, n_dn:n_dn + n_gate], l0_dn_a_log, l0_dn_dt_bias, DN_HEADS)
    conv_t = jnp.concatenate([l0_dn_conv.T, jnp.zeros((8 - l0_dn_conv.shape[1], 3 * DN_WIDTH), F32)], axis=0)
    qkv = dn_prep(proj_dn, conv_t, geom, ctx_len, DN_WIDTH, DN_HEAD_DIM)
    dn_f, dn_b = delta_scan(qkv, gc, gr, geom, ctx_len, DN_HEADS, DN_HEAD_DIM, inv_passes=DELTA_INV_PASSES)
    lb = jnp.cumsum(jax.nn.softmax(hg_lb_logits, axis=0), axis=0)[0:1]
    hg_f, hg_b = hgrn_scan(proj_hg, lb, geom, ctx_len, HG_HEADS, HG_KEY_DIM)
    y = mix_out(dn_f, dn_b, hg_f, hg_b, proj_dn, 3, proj_hg, 4, l0_dn_norm, l0_hg_norm, DN_HEAD_DIM)
    xs = matmul_residual(y, l0_w_out, xs, mod3, 2, geom)
    h, logits_t = adaln(xs, l0_norm2, mod3, 3, geom, router_w=l0_router_w)
    xs = moe_ffn(xs, h, logits_t, l0_router_b, l0_exp_gate_up, l0_exp_down, l0_shared_gate_up, l0_shared_down,
                 mod3, geom)

    mod3 = modulation(cond, l1_mod_w, l1_mod_b).reshape(MOD_ROWS * 6, 1, d)
    h = adaln(xs, l1_norm1, mod3, 0, geom)
    proj = matmul(h, l1_w_in)
    cos_t, sin_t = rope_tables(lat_len, n_batch, n_ctx, DA_HEAD_DIM)
    qkv = qk_prep(proj, l1_q_norm, l1_k_norm, cos_t, sin_t)
    lam_init = 0.8 - 0.6 * math.exp(-0.3 * 1)
    lmbda = (jnp.exp(jnp.sum(l1_lambda[0] * l1_lambda[1])) - jnp.exp(jnp.sum(l1_lambda[2] * l1_lambda[3]))
             + lam_init)
    y = diff_attention(qkv, lmbda, l1_sub_norm, n_batch, lat_len, ctx_len, DA_HEADS, DA_HEAD_DIM,
                       1.0 - lam_init)
    xl = matmul_residual(y, l1_w_out, xs[:n_lat], mod3, 2, geom_lat_only)
    h, logits_t = adaln(xl, l1_norm2, mod3, 3, geom_lat_only, router_w=l1_router_w)
    xl = moe_ffn(xl, h, logits_t, l1_router_b, l1_exp_gate_up, l1_exp_down, l1_shared_gate_up, l1_shared_down,
                 mod3, geom_lat_only)
    return xl.reshape(n_batch, lat_len, d)
```

```python
import functools
import math

import jax
import jax.numpy as jnp
import numpy as np
from jax import lax
from jax.experimental import pallas as pl
from jax.experimental.pallas import tpu as pltpu

F32 = jnp.float32
BF16 = jnp.bfloat16

NORM_EPS = 1e-6
GRID_W = 64
ROPE_BASE = 10000.0

DN_HEADS = 8
DN_HEAD_DIM = 128
DN_WIDTH = DN_HEADS * DN_HEAD_DIM
DN_CHUNK = 64
HG_HEADS = 8
HG_KEY_DIM = 128
HG_VAL_DIM = 128
HG_K_WIDTH = HG_HEADS * HG_KEY_DIM
HG_V_WIDTH = HG_HEADS * HG_VAL_DIM
HG_CHUNK = 64
IN0_SIZES = (3 * DN_WIDTH, DN_WIDTH, 2 * DN_HEADS, 2 * DN_HEADS,
             HG_K_WIDTH, 2 * HG_K_WIDTH, HG_V_WIDTH, HG_V_WIDTH)
DA_HEADS = 8
DA_HEAD_DIM = 128
N_EXPERTS = 64
TOP_K = 8
N_GROUPS = 8
TOPK_GROUPS = 4
EXPERT_FF = 512
ROUTED_SCALE = 2.5

VMEM_LIMIT_BYTES = 56 * 1024 * 1024
MOD_ROWS = 8
EXPERT_SLOT_BLOCK = 256
DMA_LOOP_UNROLL = 8
MM_TILE_M = 1024
MM_TILE_N = 1024
ATTN_TILE_Q = 512
ATTN_ROW_CHUNK = 128


def _params(*sem):
    return pltpu.CompilerParams(dimension_semantics=sem, vmem_limit_bytes=VMEM_LIMIT_BYTES)


def _silu(x):
    return x * (1.0 / (1.0 + jnp.exp(-x)))


def _mod_body(c_ref, w_ref, b_ref, o_ref):
    a = _silu(c_ref[...]).astype(BF16)
    o_ref[...] = jnp.dot(a, w_ref[...].astype(BF16), preferred_element_type=F32) + b_ref[...]


def modulation(cond, w, b, tn=1024):
    m, k = cond.shape
    n = w.shape[1]
    return pl.pallas_call(
        _mod_body, grid=(n // tn,),
        in_specs=[pl.BlockSpec((m, k), lambda j: (0, 0)),
                  pl.BlockSpec((k, tn), lambda j: (0, j)),
                  pl.BlockSpec((1, tn), lambda j: (0, j))],
        out_specs=pl.BlockSpec((m, tn), lambda j: (0, j)),
        out_shape=jax.ShapeDtypeStruct((m, n), F32),
        compiler_params=_params("arbitrary"), name="modulation",
    )(cond, w, b.reshape(1, n))


def _mod_row(tile, tile_rows, n_lat, lat_len, n_batch):
    start = tile * tile_rows
    return jnp.where(start < n_lat, start // lat_len, n_batch)


def _pack_bf16_pairs(h):
    half = h.shape[1] // 2
    bits = lax.bitcast_convert_type(h.astype(BF16).astype(F32), jnp.uint32)
    return (bits[:, :half] & jnp.uint32(0xFFFF0000)) | (bits[:, half:] >> jnp.uint32(16))


def _unpack_bf16_pairs(pk):
    hi = lax.bitcast_convert_type(pk & jnp.uint32(0xFFFF0000), F32).astype(BF16)
    lo = lax.bitcast_convert_type(pk << jnp.uint32(16), F32).astype(BF16)
    return hi, lo


def _adaln_body(x_ref, nw_ref, shift_ref, scale_ref, o_ref):
    x = x_ref[...]
    y = x * lax.rsqrt(jnp.mean(x * x, axis=-1, keepdims=True) + NORM_EPS) * nw_ref[...]
    o_ref[...] = (y * (1.0 + scale_ref[0]) + shift_ref[0]).astype(o_ref.dtype)


def _adaln_router_body(x_ref, nw_ref, shift_ref, scale_ref, rw_ref, o_ref, lg_ref):
    x = x_ref[...]
    y = x * lax.rsqrt(jnp.mean(x * x, axis=-1, keepdims=True) + NORM_EPS) * nw_ref[...]
    h = y * (1.0 + scale_ref[0]) + shift_ref[0]
    o_ref[...] = _pack_bf16_pairs(h)
    lg_ref[...] = lax.dot_general(rw_ref[...], h, (((1,), (1,)), ((), ())), preferred_element_type=F32,
                                  precision=lax.Precision.HIGHEST)


def adaln(x, norm_w, mod3, shift_idx, geom, router_w=None, tl=256):
    t, d = x.shape
    n_lat, lat_len, n_batch = geom
    row = functools.partial(_mod_row, tile_rows=tl, n_lat=n_lat, lat_len=lat_len, n_batch=n_batch)
    in_specs = [pl.BlockSpec((tl, d), lambda i: (i, 0)),
                pl.BlockSpec((1, d), lambda i: (0, 0)),
                pl.BlockSpec((1, 1, d), lambda i: (row(i) * 6 + shift_idx, 0, 0)),
                pl.BlockSpec((1, 1, d), lambda i: (row(i) * 6 + shift_idx + 1, 0, 0))]
    args = [x, norm_w.reshape(1, d), mod3, mod3]
    if router_w is None:
        return pl.pallas_call(
            _adaln_body, grid=(t // tl,), in_specs=in_specs,
            out_specs=pl.BlockSpec((tl, d), lambda i: (i, 0)),
            out_shape=jax.ShapeDtypeStruct((t, d), BF16),
            compiler_params=_params("arbitrary"), name="adaln")(*args)
    e = router_w.shape[1]
    return pl.pallas_call(
        _adaln_router_body, grid=(t // tl,),
        in_specs=in_specs + [pl.BlockSpec((e, d), lambda i: (0, 0))],
        out_specs=[pl.BlockSpec((tl, d // 2), lambda i: (i, 0)), pl.BlockSpec((e, tl), lambda i: (0, i))],
        out_shape=[jax.ShapeDtypeStruct((t, d // 2), jnp.uint32), jax.ShapeDtypeStruct((e, t), F32)],
        compiler_params=_params("arbitrary"), name="adaln_router")(*args, router_w.T)


def _mm_body(a_ref, w_ref, o_ref, wb_ref):
    @pl.when(pl.program_id(1) == 0)
    def _():
        wb_ref[...] = w_ref[...].astype(BF16)
    o_ref[...] = jnp.dot(a_ref[...], wb_ref[...], preferred_element_type=F32).astype(o_ref.dtype)


def _mm_res_body(a_ref, w_ref, res_ref, gate_ref, o_ref, wb_ref):
    @pl.when(pl.program_id(1) == 0)
    def _():
        wb_ref[...] = w_ref[...].astype(BF16)
    acc = jnp.dot(a_ref[...], wb_ref[...], preferred_element_type=F32)
    o_ref[...] = res_ref[...] + gate_ref[0] * acc


def matmul(a, w, n_cols=None, col_block0=0, tm=MM_TILE_M, tn=MM_TILE_N, out_dtype=F32):
    m, k = a.shape
    n = w.shape[1] if n_cols is None else n_cols
    return pl.pallas_call(
        _mm_body, grid=(n // tn, m // tm),
        in_specs=[pl.BlockSpec((tm, k), lambda j, i: (i, 0)),
                  pl.BlockSpec((k, tn), lambda j, i: (0, j + col_block0))],
        out_specs=pl.BlockSpec((tm, tn), lambda j, i: (i, j)),
        out_shape=jax.ShapeDtypeStruct((m, n), out_dtype),
        scratch_shapes=[pltpu.VMEM((k, tn), BF16)],
        compiler_params=_params("arbitrary", "arbitrary"), name="matmul",
    )(a, w)


def matmul_residual(a, w, res, mod3, gate_idx, geom, tm=MM_TILE_M, tn=MM_TILE_N):
    m, k = a.shape
    n = w.shape[1]
    n_lat, lat_len, n_batch = geom
    row = functools.partial(_mod_row, tile_rows=tm, n_lat=n_lat, lat_len=lat_len, n_batch=n_batch)
    nb = n // tn
    return pl.pallas_call(
        _mm_res_body, grid=(n // tn, m // tm),
        in_specs=[pl.BlockSpec((tm, k), lambda j, i: (i, 0)),
                  pl.BlockSpec((k, tn), lambda j, i: (0, j)),
                  pl.BlockSpec((tm, tn), lambda j, i: (i, j)),
                  pl.BlockSpec((1, 1, tn), lambda j, i: (row(i) * 6 + gate_idx, 0, j))],
        out_specs=pl.BlockSpec((tm, tn), lambda j, i: (i, j)),
        out_shape=jax.ShapeDtypeStruct((m, n), F32),
        scratch_shapes=[pltpu.VMEM((k, tn), BF16)],
        compiler_params=_params("arbitrary", "arbitrary"), name="matmul_residual",
    )(a, w, res, mod3)


def _expert_body(blk_e_ref, blk_new_ref, n_used_ref, x_ref, gu_ref, dn_ref, o_ref, gub_ref, dnb_ref):
    i = pl.program_id(0)

    @pl.when(blk_new_ref[i] == 1)
    def _():
        gub_ref[...] = gu_ref[0].astype(BF16)
        dnb_ref[...] = dn_ref[0].astype(BF16)

    @pl.when(i < n_used_ref[0])
    def _():
        ff = dnb_ref.shape[0]
        half = x_ref.shape[1]
        x_hi, x_lo = _unpack_bf16_pairs(x_ref[...])
        h1 = (jnp.dot(x_hi, gub_ref[:half], preferred_element_type=F32)
              + jnp.dot(x_lo, gub_ref[half:], preferred_element_type=F32))
        act = (_silu(h1[:, :ff]) * h1[:, ff:]).astype(BF16)
        o_ref[...] = jnp.dot(act, dnb_ref[...], preferred_element_type=F32)

    @pl.when(i >= n_used_ref[0])
    def _():
        o_ref[...] = jnp.zeros_like(o_ref)


def expert_blocks(x, gate_up, down, blk_e, blk_new, n_used, blk):
    s = x.shape[0]
    _, d, f2 = gate_up.shape
    n_blk = s // blk
    grid_spec = pltpu.PrefetchScalarGridSpec(
        num_scalar_prefetch=3, grid=(n_blk,),
        in_specs=[pl.BlockSpec((blk, d // 2), lambda i, be, bn, nu: (jnp.minimum(i, nu[0] - 1), 0)),
                  pl.BlockSpec((1, d, f2), lambda i, be, bn, nu: (be[i], 0, 0)),
                  pl.BlockSpec((1, f2 // 2, d), lambda i, be, bn, nu: (be[i], 0, 0))],
        out_specs=pl.BlockSpec((blk, d), lambda i, be, bn, nu: (i, 0)),
        scratch_shapes=[pltpu.VMEM((d, f2), BF16), pltpu.VMEM((f2 // 2, d), BF16)])
    return pl.pallas_call(
        _expert_body, grid_spec=grid_spec,
        out_shape=jax.ShapeDtypeStruct((s, d), F32),
        compiler_params=_params("arbitrary"), name="expert_blocks",
    )(blk_e, blk_new, n_used, x, gate_up, down)


def _qk_prep_body(x_ref, qw_ref, kw_ref, cos_ref, sin_ref, o_ref, *, n_qk_groups, head_dim):
    cos = cos_ref[...]
    sin = sin_ref[...]
    lane = lax.broadcasted_iota(jnp.int32, cos.shape, 1)
    first = (lane % (head_dim // 2)) < (head_dim // 4)
    for g in range(n_qk_groups):
        sl = slice(g * head_dim, (g + 1) * head_dim)
        x = x_ref[:, sl]
        w = qw_ref[...] if g < n_qk_groups // 2 else kw_ref[...]
        y = x * lax.rsqrt(jnp.mean(x * x, axis=-1, keepdims=True) + NORM_EPS) * w
        swapped = jnp.where(first, pltpu.roll(y, head_dim - head_dim // 4, 1), pltpu.roll(y, head_dim // 4, 1))
        o_ref[:, sl] = (y * cos + swapped * sin).astype(o_ref.dtype)
    rest = n_qk_groups * head_dim
    o_ref[:, rest:] = x_ref[:, rest:].astype(o_ref.dtype)


def qk_prep(proj, q_norm, k_norm, cos_t, sin_t, tl=256):
    t, n = proj.shape
    hd = q_norm.shape[0]
    n_groups = (2 * n // 3) // hd
    body = functools.partial(_qk_prep_body, n_qk_groups=n_groups, head_dim=hd)
    return pl.pallas_call(
        body, grid=(t // tl,),
        in_specs=[pl.BlockSpec((tl, n), lambda i: (i, 0)),
                  pl.BlockSpec((1, hd), lambda i: (0, 0)),
                  pl.BlockSpec((1, hd), lambda i: (0, 0)),
                  pl.BlockSpec((tl, hd), lambda i: (i, 0)),
                  pl.BlockSpec((tl, hd), lambda i: (i, 0))],
        out_specs=pl.BlockSpec((tl, n), lambda i: (i, 0)),
        out_shape=jax.ShapeDtypeStruct((t, n), BF16),
        compiler_params=_params("arbitrary"), name="qk_prep",
    )(proj, q_norm.reshape(1, hd), k_norm.reshape(1, hd), cos_t, sin_t)


def rope_tables(n_lat_tokens_per_sample, n_batch, n_ctx_tokens, head_dim):
    quarter = head_dim // 4
    inv_freq = ROPE_BASE ** (-jnp.arange(quarter, dtype=F32) / quarter)
    rows = n_lat_tokens_per_sample // GRID_W
    row = jnp.repeat(jnp.arange(rows, dtype=F32), GRID_W)
    col = jnp.tile(jnp.arange(GRID_W, dtype=F32), rows)
    ang_r = row[:, None] * inv_freq[None, :]
    ang_c = col[:, None] * inv_freq[None, :]
    cos = jnp.concatenate([jnp.cos(ang_r), jnp.cos(ang_r), jnp.cos(ang_c), jnp.cos(ang_c)], axis=-1)
    sin = jnp.concatenate([-jnp.sin(ang_r), jnp.sin(ang_r), -jnp.sin(ang_c), jnp.sin(ang_c)], axis=-1)
    cos = jnp.concatenate([jnp.tile(cos, (n_batch, 1)), jnp.ones((n_ctx_tokens, head_dim), F32)], axis=0)
    sin = jnp.concatenate([jnp.tile(sin, (n_batch, 1)), jnp.zeros((n_ctx_tokens, head_dim), F32)], axis=0)
    return cos, sin


def _diff_attn_rows(lam, q_ref, k_all, v_all, sw_ref, o_ref, rows, head_dim, out_scale):
    c = head_dim ** -0.5 * math.log2(math.e)
    es, invs = [], []
    for s in range(2):
        sl = slice(s * head_dim, (s + 1) * head_dim)
        sc = lax.dot_general(q_ref[rows, sl], k_all[:, sl], (((1,), (1,)), ((), ())), preferred_element_type=F32)
        yield
        e = jnp.exp2((sc - jnp.max(sc, axis=-1, keepdims=True)) * c)
        invs.append(1.0 / jnp.sum(e, axis=-1, keepdims=True))
        es.append(e.astype(BF16))
        yield
    v = v_all[...]
    o0 = jnp.dot(es[0], v, preferred_element_type=F32)
    o1 = jnp.dot(es[1], v, preferred_element_type=F32)
    yield
    o = o0 * invs[0] - (lam * invs[1]) * o1
    y = o * lax.rsqrt(jnp.mean(o * o, axis=-1, keepdims=True) + NORM_EPS) * sw_ref[...]
    o_ref[rows, :] = (y * out_scale).astype(o_ref.dtype)
    yield


def _diff_attn_body(lam_ref, q_ref, kl_ref, kc_ref, vl_ref, vc_ref, sw_ref, o_ref, k_all, v_all, *,
                    head_dim, out_scale, row_chunk):
    @pl.when(pl.program_id(2) == 0)
    def _():
        n_l = kl_ref.shape[0]
        k_all[:n_l] = kl_ref[...]
        k_all[n_l:] = kc_ref[...]
        v_all[:n_l] = vl_ref[...]
        v_all[n_l:] = vc_ref[...]

    lam = lam_ref[0]
    tq = q_ref.shape[0]
    _run_interleaved(
        _diff_attn_rows(lam, q_ref, k_all, v_all, sw_ref, o_ref, slice(r, r + row_chunk), head_dim, out_scale)
        for r in range(0, tq, row_chunk))


def diff_attention(qkv, lmbda, sub_norm, n_batch, lat_len, ctx_len, n_heads, head_dim, out_scale,
                   tq=ATTN_TILE_Q, row_chunk=ATTN_ROW_CHUNK):
    hw = 2 * head_dim
    nq = lat_len // tq
    ctx_blk0 = n_batch * lat_len // ctx_len
    body = functools.partial(_diff_attn_body, head_dim=head_dim, out_scale=out_scale, row_chunk=row_chunk)
    n_keys = lat_len + ctx_len
    return pl.pallas_call(
        body, grid=(n_batch, n_heads, nq),
        in_specs=[pl.BlockSpec(memory_space=pltpu.SMEM),
                  pl.BlockSpec((tq, hw), lambda b, h, i: (b * nq + i, h)),
                  pl.BlockSpec((lat_len, hw), lambda b, h, i: (b, n_heads + h)),
                  pl.BlockSpec((ctx_len, hw), lambda b, h, i: (ctx_blk0 + b, n_heads + h)),
                  pl.BlockSpec((lat_len, hw), lambda b, h, i: (b, 2 * n_heads + h)),
                  pl.BlockSpec((ctx_len, hw), lambda b, h, i: (ctx_blk0 + b, 2 * n_heads + h)),
                  pl.BlockSpec((1, hw), lambda b, h, i: (0, 0))],
        out_specs=pl.BlockSpec((tq, hw), lambda b, h, i: (b * nq + i, h)),
        out_shape=jax.ShapeDtypeStruct((n_batch * lat_len, n_heads * hw), BF16),
        scratch_shapes=[pltpu.VMEM((n_keys, hw), BF16), pltpu.VMEM((n_keys, hw), BF16)],
        compiler_params=_params("arbitrary", "arbitrary", "arbitrary"), name="diff_attention",
    )(lmbda.reshape(1), qkv, qkv, qkv, qkv, qkv, sub_norm.reshape(1, hw))


SCAN_TILE = 256
CHUNK = 64
SUB = 16
NEG_BIG = -1e30
NEG_INF = float("-inf")
DELTA_INV_PASSES = 1
DELTA_HEADS_PER_STEP = 4
HGRN_HEADS_PER_STEP = 2


def _sigmoid(x):
    return 1.0 / (1.0 + jnp.exp(-x))


def _dot(a, b):
    return jnp.dot(a.astype(BF16), b.astype(BF16), preferred_element_type=F32)


def _dot_nt(a, b):
    return lax.dot_general(a.astype(BF16), b.astype(BF16), (((1,), (1,)), ((), ())),
                           preferred_element_type=F32)


def _split3(x):
    hi = x.astype(BF16)
    r = x - hi.astype(F32)
    mid = r.astype(BF16)
    lo = (r - mid.astype(F32)).astype(BF16)
    return hi, mid, lo


def _dot_exact_lhs01(m01, x):
    hi, mid, lo = _split3(x)
    m = m01.astype(BF16)
    return (jnp.dot(m, hi, preferred_element_type=F32) + jnp.dot(m, mid, preferred_element_type=F32)
            + jnp.dot(m, lo, preferred_element_type=F32))


def _dot_exact_rhs01(x, m01):
    hi, mid, lo = _split3(x)
    m = m01.astype(BF16)
    return (jnp.dot(hi, m, preferred_element_type=F32) + jnp.dot(mid, m, preferred_element_type=F32)
            + jnp.dot(lo, m, preferred_element_type=F32))


def _dot3(a, b):
    ah = a.astype(BF16)
    al = (a - ah.astype(F32)).astype(BF16)
    bh = b.astype(BF16)
    bl = (b - bh.astype(F32)).astype(BF16)
    return (jnp.dot(ah, bh, preferred_element_type=F32) + jnp.dot(ah, bl, preferred_element_type=F32)
            + jnp.dot(al, bh, preferred_element_type=F32))


def _tile_masks(n, reverse):
    i = lax.broadcasted_iota(jnp.int32, (n, n), 0)
    j = lax.broadcasted_iota(jnp.int32, (n, n), 1)
    same = (i // CHUNK) == (j // CHUNK)
    if reverse:
        return same & (i <= j), same & (i < j)
    return same & (i >= j), same & (i > j)


def _segment_tile(b, s, reverse, nl, nc, ctx_tile0):
    if reverse:
        return jnp.where(s < nc, ctx_tile0 + b * nc + (nc - 1 - s), b * nl + (nl - 1 - (s - nc)))
    return jnp.where(s < nc, ctx_tile0 + b * nc + s, b * nl + (s - nc))


def _dn_prep_body(x_ref, prev_ref, next_ref, w_ref, o_ref, *, tiles_per_lat_seg, tiles_per_ctx_seg, n_lat_tiles,
                  head_dim, q_scale):
    i = pl.program_id(0)
    j = pl.program_id(1)
    is_lat = i < n_lat_tiles
    pos = jnp.where(is_lat, i % tiles_per_lat_seg, (i - n_lat_tiles) % tiles_per_ctx_seg)
    seg_first = pos == 0
    seg_last = pos == jnp.where(is_lat, tiles_per_lat_seg, tiles_per_ctx_seg) - 1
    x = x_ref[...]
    tl = x.shape[0]
    prev = jnp.where(seg_first, 0.0, prev_ref[...])
    nxt = jnp.where(seg_last, 0.0, next_ref[...])
    xp = jnp.concatenate([prev, x, nxt], axis=0)
    w = w_ref[...]
    n_taps = 5
    acc = None
    for t in range(n_taps):
        off = 8 + t - n_taps // 2
        term = xp[off:off + tl] * w[t:t + 1]
        acc = term if acc is None else acc + term
    y = _silu(acc)
    scale = jnp.where(j == 0, q_scale, 1.0)
    outs = []
    for h in range(y.shape[1] // head_dim):
        yh = y[:, h * head_dim:(h + 1) * head_dim]
        nrm = lax.rsqrt(jnp.sum(yh * yh, axis=-1, keepdims=True) + 1e-6) * scale
        outs.append(yh * jnp.where(j == 2, 1.0, nrm))
    o_ref[...] = jnp.concatenate(outs, axis=1)


def dn_prep(proj_dn, conv_w_t, geom, ctx_len, width, head_dim):
    n_lat, lat_len, n_batch = geom
    t = proj_dn.shape[0]
    tl = SCAN_TILE
    rows8 = tl // 8
    n_tiles = t // tl
    body = functools.partial(_dn_prep_body, tiles_per_lat_seg=lat_len // tl, tiles_per_ctx_seg=ctx_len // tl,
                             n_lat_tiles=n_lat // tl,
                             head_dim=head_dim, q_scale=head_dim ** -0.5)
    last8 = t // 8 - 1
    return pl.pallas_call(
        body, grid=(n_tiles, 3),
        in_specs=[pl.BlockSpec((tl, width), lambda i, j: (i, j)),
                  pl.BlockSpec((8, width), lambda i, j: (jnp.maximum(i * rows8 - 1, 0), j)),
                  pl.BlockSpec((8, width), lambda i, j: (jnp.minimum((i + 1) * rows8, last8), j)),
                  pl.BlockSpec((8, width), lambda i, j: (0, j))],
        out_specs=pl.BlockSpec((tl, width), lambda i, j: (i, j)),
        out_shape=jax.ShapeDtypeStruct((t, 3 * width), F32),
        compiler_params=_params("arbitrary", "arbitrary"), name="dn_prep",
    )(proj_dn, proj_dn, proj_dn, conv_w_t)


def _softplus(x):
    return jnp.maximum(x, 0.0) + jnp.log(1.0 + jnp.exp(-jnp.abs(x)))


def _gate_prep_body(h_ref, wc_ref, wr_ref, alog_c_ref, dtb_c_ref, alog_r_ref, dtb_r_ref, gc_ref, gr_ref, *, n_heads):
    h = h_ref[...]
    tl = h.shape[0]
    nd = 2 * n_heads
    raw_c = jnp.dot(h, wc_ref[...].astype(BF16), preferred_element_type=F32)
    raw_r = lax.dot_general(wr_ref[...].astype(BF16), h, (((1,), (1,)), ((), ())),
                            preferred_element_type=F32)
    g_c = -jnp.exp(alog_c_ref[...]) * _softplus(raw_c[:, :nd] + dtb_c_ref[...])
    g_r = -jnp.exp(alog_r_ref[...]) * _softplus(raw_r[:nd, :] + dtb_r_ref[...])
    incl_f, _ = _tile_masks(tl, False)
    incl_b, _ = _tile_masks(tl, True)
    one_f = jnp.where(incl_f, 1.0, 0.0)
    one_b = jnp.where(incl_b, 1.0, 0.0)
    cum_c = jnp.concatenate([_dot_exact_lhs01(one_f, g_c[:, :n_heads]),
                             _dot_exact_lhs01(one_b, g_c[:, n_heads:])], axis=1)
    cum_r = jnp.concatenate([_dot_exact_rhs01(g_r[:n_heads, :], one_b),
                             _dot_exact_rhs01(g_r[n_heads:, :], one_f)], axis=0)
    beta_c = _sigmoid(raw_c[:, nd:])
    gc_ref[...] = jnp.concatenate([cum_c, beta_c], axis=1)
    gr_ref[...] = jnp.concatenate([cum_r, jnp.zeros_like(cum_r)], axis=0)


def gate_prep(h_bf, w_gate, a_log, dt_bias, n_heads):
    t, d = h_bf.shape
    tl = SCAN_TILE
    nd = 2 * n_heads
    body = functools.partial(_gate_prep_body, n_heads=n_heads)
    full = lambda shape: pl.BlockSpec(shape, lambda i: (0, 0))
    return pl.pallas_call(
        body, grid=(t // tl,),
        in_specs=[pl.BlockSpec((tl, d), lambda i: (i, 0)), full((d, 2 * nd)), full((2 * nd, d)),
                  full((1, nd)), full((1, nd)), full((nd, 1)), full((nd, 1))],
        out_specs=[pl.BlockSpec((tl, 2 * nd), lambda i: (i, 0)), pl.BlockSpec((2 * nd, tl), lambda i: (0, i))],
        out_shape=[jax.ShapeDtypeStruct((t, 2 * nd), F32), jax.ShapeDtypeStruct((2 * nd, t), F32)],
        compiler_params=_params("arbitrary"), name="gate_prep",
    )(h_bf, w_gate, w_gate.T, a_log.reshape(1, nd), dt_bias.reshape(1, nd),
      a_log.reshape(nd, 1), dt_bias.reshape(nd, 1))


def _select_col(x, idx):
    lane = lax.broadcasted_iota(jnp.int32, x.shape, 1)
    return jnp.sum(jnp.where(lane == idx, x, 0.0), axis=1, keepdims=True)


def _select_row(x, idx):
    row = lax.broadcasted_iota(jnp.int32, x.shape, 0)
    return jnp.sum(jnp.where(row == idx, x, 0.0), axis=0, keepdims=True)


def _run_interleaved(chains):
    chains = list(chains)
    while chains:
        alive = []
        for ch in chains:
            try:
                next(ch)
                alive.append(ch)
            except StopIteration:
                pass
        chains = alive


def _delta_chain(q, k, v, gc_col, gc_row, beta_col, s_ref, o_ref, cols, reverse, inv_passes):
    tl, kd_ = k.shape
    n_chunks = tl // CHUNK
    incl, strict = _tile_masks(tl, reverse)
    decay = jnp.exp(jnp.where(incl, gc_col - gc_row, NEG_BIG))
    kb, qb = k.astype(BF16), q.astype(BF16)
    kkt = _dot_nt(kb, kb)
    qkt = _dot_nt(qb, kb)
    yield
    x = jnp.where(strict, kkt * (-beta_col) * decay, 0.0)
    dot_inv = _dot3 if inv_passes == 3 else _dot
    ri = lax.broadcasted_iota(jnp.int32, (tl, tl), 0)
    ci = lax.broadcasted_iota(jnp.int32, (tl, tl), 1)
    r = jnp.where(ri == ci, 1.0, 0.0) + x
    n_sq = int(math.log2(CHUNK)) - 1
    for _ in range(n_sq):
        x = dot_inv(x, x)
        r = r + dot_inv(r, x)
        yield
    e_g = jnp.exp(gc_col)
    rhs = jnp.concatenate([v * beta_col, k * (beta_col * e_g)], axis=1)
    sol = dot_inv(r, rhs)
    yield
    u0, w = sol[:, :v.shape[1]], sol[:, v.shape[1]:]
    attn = (qkt * decay).astype(BF16)
    o0 = _dot(attn, u0)
    qe = q * e_g - _dot(attn, w)
    tot_rows = []
    for c in range(n_chunks):
        last = c * CHUNK if reverse else c * CHUNK + CHUNK - 1
        tot_rows.append(jnp.broadcast_to(gc_col[last:last + 1, :], (CHUNK, 1)))
    tot = jnp.concatenate(tot_rows, axis=0)
    kdec_t = jnp.transpose(k * jnp.exp(tot - gc_col))
    wu = jnp.concatenate([-w, u0], axis=1).astype(BF16)
    lane = lax.broadcasted_iota(jnp.int32, kdec_t.shape, 1)
    pns = [_dot(jnp.where((lane // CHUNK) == c, kdec_t, 0.0), wu) for c in range(n_chunks)]
    yield
    order = range(n_chunks - 1, -1, -1) if reverse else range(n_chunks)
    for c in order:
        rows = slice(c * CHUNK, (c + 1) * CHUNK)
        pn = pns[c]
        lhs = jnp.concatenate([qe[rows], pn[:, :kd_]], axis=0)
        s = s_ref[...]
        res = _dot(lhs, s)
        o_ref[rows, cols] = o0[rows] + res[:CHUNK]
        last = c * CHUNK if reverse else c * CHUNK + CHUNK - 1
        gl = jnp.exp(gc_col[last:last + 1, :])
        s_ref[...] = gl * s + res[CHUNK:] + pn[:, kd_:]
        yield


def _delta_body(qf_ref, kf_ref, vf_ref, gcf_ref, grf_ref, qb_ref, kb_ref, vb_ref, gcb_ref, grb_ref,
                of_ref, ob_ref, sf_ref, sb_ref, *, n_heads, head_dim, inv_passes):
    hg = pl.program_id(1)
    heads_per_step = sf_ref.shape[0]

    @pl.when(pl.program_id(2) == 0)
    def _():
        sf_ref[...] = jnp.zeros_like(sf_ref)
        sb_ref[...] = jnp.zeros_like(sb_ref)

    chains = []
    for g in range(heads_per_step):
        cols = slice(g * head_dim, (g + 1) * head_dim)
        for reverse, (q_ref, k_ref, v_ref, gc_ref, gr_ref, o_ref, s_ref) in enumerate(
                [(qf_ref, kf_ref, vf_ref, gcf_ref, grf_ref, of_ref, sf_ref),
                 (qb_ref, kb_ref, vb_ref, gcb_ref, grb_ref, ob_ref, sb_ref)]):
            idx = reverse * n_heads + hg * heads_per_step + g
            gcs = gc_ref[...]
            gc_col = _select_col(gcs, idx)
            beta_col = _select_col(gcs, 2 * n_heads + idx)
            gc_row = _select_row(gr_ref[...], idx)
            chains.append(_delta_chain(q_ref[:, cols], k_ref[:, cols], v_ref[:, cols], gc_col, gc_row, beta_col,
                                       s_ref.at[g], o_ref, cols, bool(reverse), inv_passes))
    _run_interleaved(chains)


def delta_scan(qkv, gc, gr, geom, ctx_len, n_heads, head_dim, inv_passes=3, heads_per_step=DELTA_HEADS_PER_STEP):
    n_lat, lat_len, n_batch = geom
    t = qkv.shape[0]
    tl = SCAN_TILE
    nl, nc, ctx0 = lat_len // tl, ctx_len // tl, n_lat // tl
    n_hg = n_heads // heads_per_step
    gw = heads_per_step * head_dim
    tile = functools.partial(_segment_tile, nl=nl, nc=nc, ctx_tile0=ctx0)
    specs = []
    for reverse in (False, True):
        tix = functools.partial(tile, reverse=reverse)
        specs += [pl.BlockSpec((tl, gw), lambda b, h, s, tix=tix: (tix(b, s), h)),
                  pl.BlockSpec((tl, gw), lambda b, h, s, tix=tix: (tix(b, s), n_hg + h)),
                  pl.BlockSpec((tl, gw), lambda b, h, s, tix=tix: (tix(b, s), 2 * n_hg + h)),
                  pl.BlockSpec((tl, 4 * n_heads), lambda b, h, s, tix=tix: (tix(b, s), 0)),
                  pl.BlockSpec((4 * n_heads, tl), lambda b, h, s, tix=tix: (0, tix(b, s)))]
    out_specs = [pl.BlockSpec((tl, gw), lambda b, h, s, tix=functools.partial(tile, reverse=r): (tix(b, s), h))
                 for r in (False, True)]
    body = functools.partial(_delta_body, n_heads=n_heads, head_dim=head_dim, inv_passes=inv_passes)
    return pl.pallas_call(
        body, grid=(n_batch, n_hg, nl + nc), in_specs=specs, out_specs=out_specs,
        out_shape=[jax.ShapeDtypeStruct((t, n_heads * head_dim), F32)] * 2,
        scratch_shapes=[pltpu.VMEM((heads_per_step, head_dim, head_dim), F32)] * 2,
        compiler_params=_params("arbitrary", "arbitrary", "arbitrary"), name="delta_scan",
    )(qkv, qkv, qkv, gc, gr, qkv, qkv, qkv, gc, gr)


def _hgrn_diag(hq_ref, hf_ref, hv_ref, cols, lb, stage_ref, diag_ref, reverse):
    n_sub = hq_ref.shape[0] // SUB
    for n, ref in enumerate((hq_ref, hf_ref, hv_ref)):
        stage_ref[n] = ref[:, cols]

    def slab(n, r):
        return stage_ref.at[n][pl.ds(r, n_sub, stride=SUB), :]

    q_x = [_silu(slab(0, r)) for r in range(SUB)]
    f_x = [lb + (1.0 - lb) * _sigmoid(slab(1, r)) for r in range(SUB)]
    k_x = [1.0 - f for f in f_x]
    lf_x = [jnp.log(f) for f in f_x]
    v_x = [slab(2, r) for r in range(SUB)]
    p_x = [None] * SUB
    scan_rows = range(SUB - 1, -1, -1) if reverse else range(SUB)
    acc = None
    for r in scan_rows:
        acc = lf_x[r] if acc is None else acc + lf_x[r]
        p_x[r] = acc
    pairs = [(i, j) for i in range(SUB) for j in range(SUB) if (j >= i if reverse else j <= i)]
    terms = []
    for i, j in pairs:
        qk = q_x[i] * k_x[j]
        terms.append((qk if i == j else qk * jnp.exp(p_x[i] - p_x[j])).astype(BF16))
    kd_ = terms[0].shape[1]
    a_rep = jnp.dot(jnp.concatenate(terms, axis=0), jnp.ones((kd_, kd_), BF16), preferred_element_type=F32)
    o_x = [None] * SUB
    for n, (i, j) in enumerate(pairs):
        contrib = a_rep[n * n_sub:(n + 1) * n_sub] * v_x[j]
        o_x[i] = contrib if o_x[i] is None else o_x[i] + contrib
    for r in range(SUB):
        diag_ref[pl.ds(r, n_sub, stride=SUB), :] = o_x[r]


def _hgrn_chain(hq_ref, hf_ref, hv_ref, cols, lb, st_ref, o_ref, stage_ref, diag_ref, reverse):
    hq, hf, hv = hq_ref[:, cols], hf_ref[:, cols], hv_ref[:, cols]
    tl, kd_ = hq.shape
    n_chunks = tl // CHUNK
    n_sub = tl // SUB
    sub_per_chunk = CHUNK // SUB
    q = _silu(hq)
    f = lb + (1.0 - lb) * _sigmoid(hf)
    k = 1.0 - f
    lf = jnp.log(f)
    incl, _ = _tile_masks(tl, reverse)
    cum = _dot_exact_lhs01(jnp.where(incl, 1.0, 0.0), lf)
    excl = cum - lf

    def bcast_rows(src, row, n):
        return jnp.broadcast_to(src[row:row + 1, :], (n, kd_))

    chunk_last = [(c * CHUNK if reverse else c * CHUNK + CHUNK - 1) for c in range(n_chunks)]
    tot = jnp.concatenate([bcast_rows(cum, chunk_last[c], CHUNK) for c in range(n_chunks)], axis=0)
    sub_first = [(m * SUB + SUB - 1 if reverse else m * SUB) for m in range(n_sub)]
    r_sub = jnp.concatenate([bcast_rows(excl, sub_first[m], SUB) for m in range(n_sub)], axis=0)
    q_t = q * jnp.exp(cum - r_sub)
    qd = q * jnp.exp(cum)
    kd = k * jnp.exp(tot - cum)
    vb = hv.astype(BF16)

    i = lax.broadcasted_iota(jnp.int32, (tl, tl), 0)
    j = lax.broadcasted_iota(jnp.int32, (tl, tl), 1)
    same = (i // CHUNK) == (j // CHUNK)
    pos_i = (i % CHUNK) // SUB
    pos_j = (j % CHUNK) // SUB
    if reverse:
        pos_i, pos_j = sub_per_chunk - 1 - pos_i, sub_per_chunk - 1 - pos_j
    a_off = jnp.zeros((tl, tl), F32)
    for lvl in range(1, sub_per_chunk):
        ref_rows = []
        for c in range(n_chunks):
            m = c * sub_per_chunk + (sub_per_chunk - 1 - lvl if reverse else lvl)
            ref_rows.append(bcast_rows(excl, sub_first[m], CHUNK))
        r_lvl = jnp.concatenate(ref_rows, axis=0)
        k_t = k * jnp.exp(jnp.minimum(r_lvl - cum, 0.0))
        a_l = _dot_nt(q_t, k_t)
        a_off = a_off + jnp.where(same & (pos_i == lvl) & (pos_j < lvl), a_l, 0.0)
        yield
    o_intra = _dot(a_off, vb)
    yield

    _hgrn_diag(hq_ref, hf_ref, hv_ref, cols, lb, stage_ref, diag_ref, reverse)
    yield
    o_intra = o_intra + diag_ref[...]

    v_t = jnp.transpose(hv)
    lane = lax.broadcasted_iota(jnp.int32, v_t.shape, 1)
    kdb = kd.astype(BF16)
    n_ts = [_dot(jnp.where((lane // CHUNK) == c, v_t, 0.0), kdb) for c in range(n_chunks)]
    yield
    order = range(n_chunks - 1, -1, -1) if reverse else range(n_chunks)
    for c in order:
        rows = slice(c * CHUNK, (c + 1) * CHUNK)
        st = st_ref[...]
        o_ref[rows, cols] = o_intra[rows] + _dot_nt(qd[rows], st)
        st_ref[...] = st * jnp.exp(cum[chunk_last[c]:chunk_last[c] + 1, :]) + n_ts[c]
        yield


def _hgrn_body(qf_ref, ff_ref, vf_ref, qb_ref, fb_ref, vb_ref, lb_ref, of_ref, ob_ref, sf_ref, sb_ref, stage_ref,
               diag_ref, *, key_dim):
    heads_per_step = sf_ref.shape[0]

    @pl.when(pl.program_id(2) == 0)
    def _():
        sf_ref[...] = jnp.zeros_like(sf_ref)
        sb_ref[...] = jnp.zeros_like(sb_ref)

    chains = []
    for g in range(heads_per_step):
        cols = slice(g * key_dim, (g + 1) * key_dim)
        lb = lb_ref[:, cols]
        chains.append(_hgrn_chain(qf_ref, ff_ref, vf_ref, cols, lb, sf_ref.at[g], of_ref,
                                  stage_ref.at[2 * g], diag_ref.at[2 * g], False))
        chains.append(_hgrn_chain(qb_ref, fb_ref, vb_ref, cols, lb, sb_ref.at[g], ob_ref,
                                  stage_ref.at[2 * g + 1], diag_ref.at[2 * g + 1], True))
    _run_interleaved(chains)


def hgrn_scan(proj_hg, lb, geom, ctx_len, n_heads, key_dim, heads_per_step=HGRN_HEADS_PER_STEP):
    n_lat, lat_len, n_batch = geom
    t = proj_hg.shape[0]
    tl = SCAN_TILE
    nl, nc, ctx0 = lat_len // tl, ctx_len // tl, n_lat // tl
    n_hg = n_heads // heads_per_step
    gw = heads_per_step * key_dim
    tile = functools.partial(_segment_tile, nl=nl, nc=nc, ctx_tile0=ctx0)
    specs = []
    for reverse in (False, True):
        tix = functools.partial(tile, reverse=reverse)
        fcol = (1 + int(reverse)) * n_hg
        specs += [pl.BlockSpec((tl, gw), lambda b, h, s, tix=tix: (tix(b, s), h)),
                  pl.BlockSpec((tl, gw), lambda b, h, s, tix=tix, fcol=fcol: (tix(b, s), fcol + h)),
                  pl.BlockSpec((tl, gw), lambda b, h, s, tix=tix: (tix(b, s), 3 * n_hg + h))]
    specs.append(pl.BlockSpec((1, gw), lambda b, h, s: (0, h)))
    out_specs = [pl.BlockSpec((tl, gw), lambda b, h, s, tix=functools.partial(tile, reverse=r): (tix(b, s), h))
                 for r in (False, True)]
    return pl.pallas_call(
        functools.partial(_hgrn_body, key_dim=key_dim), grid=(n_batch, n_hg, nl + nc),
        in_specs=specs, out_specs=out_specs,
        out_shape=[jax.ShapeDtypeStruct((t, n_heads * key_dim), F32)] * 2,
        scratch_shapes=[pltpu.VMEM((heads_per_step, key_dim, key_dim), F32)] * 2
        + [pltpu.VMEM((2 * heads_per_step, 3, tl, key_dim), F32), pltpu.VMEM((2 * heads_per_step, tl, key_dim), F32)],
        compiler_params=_params("arbitrary", "arbitrary", "arbitrary"), name="hgrn_scan",
    )(proj_hg, proj_hg, proj_hg, proj_hg, proj_hg, proj_hg, lb)


def _mix_out2_body(df_ref, db_ref, hf_ref, hb_ref, z_ref, og_ref, dnw_ref, hgw_ref, o_ref, *, head_dim):
    def normed(o, nw):
        outs = []
        for h in range(o.shape[1] // head_dim):
            oh = o[:, h * head_dim:(h + 1) * head_dim]
            outs.append(oh * lax.rsqrt(jnp.mean(oh * oh, axis=-1, keepdims=True) + NORM_EPS) * nw)
        return jnp.concatenate(outs, axis=1)

    dn = normed(df_ref[...] + db_ref[...], dnw_ref[...]) * _silu(z_ref[...])
    hg = normed(hf_ref[...] + hb_ref[...], hgw_ref[...]) * _sigmoid(og_ref[...])
    half = dn.shape[1]
    o_ref[:, :half] = dn.astype(o_ref.dtype)
    o_ref[:, half:] = hg.astype(o_ref.dtype)


def mix_out(dn_f, dn_b, hg_f, hg_b, z_src, z_blk, og_src, og_blk, dn_norm, hg_norm, head_dim):
    t, w = dn_f.shape
    tl = SCAN_TILE
    row = lambda i: (i, 0)
    body = functools.partial(_mix_out2_body, head_dim=head_dim)
    return pl.pallas_call(
        body, grid=(t // tl,),
        in_specs=[pl.BlockSpec((tl, w), row)] * 4
        + [pl.BlockSpec((tl, w), lambda i: (i, z_blk)), pl.BlockSpec((tl, w), lambda i: (i, og_blk)),
           pl.BlockSpec((1, head_dim), lambda i: (0, 0)), pl.BlockSpec((1, head_dim), lambda i: (0, 0))],
        out_specs=pl.BlockSpec((tl, 2 * w), row),
        out_shape=jax.ShapeDtypeStruct((t, 2 * w), BF16),
        compiler_params=_params("arbitrary"), name="mix_out",
    )(dn_f, dn_b, hg_f, hg_b, z_src, og_src, dn_norm.reshape(1, head_dim), hg_norm.reshape(1, head_dim))


def _first_max(x, ids, n):
    m = jnp.max(x, axis=0, keepdims=True)
    first = jnp.min(jnp.where(x == m, ids, n), axis=0, keepdims=True)
    return m, first


def _route_body(lg_ref, bias_ref, idx_ref, rank_ref, w_ref, cnt_ref, carry_ref, *,
                n_groups, topk_groups, top_k, scale):
    i = pl.program_id(0)

    @pl.when(i == 0)
    def _():
        carry_ref[...] = jnp.zeros_like(carry_ref)

    lg = lg_ref[...]
    n_exp, tl = lg.shape
    per = n_exp // n_groups
    scores = 1.0 / (1.0 + jnp.exp(-lg))
    biased = scores + bias_ref[...]
    sub = lax.broadcasted_iota(jnp.int32, (per, tl), 0)
    g_rows = []
    for g in range(n_groups):
        xg = biased[g * per:(g + 1) * per]
        m1, i1 = _first_max(xg, sub, per)
        m2 = jnp.max(jnp.where(sub == i1, NEG_INF, xg), axis=0, keepdims=True)
        g_rows.append(m1 + m2)
    gscore = jnp.concatenate(g_rows, axis=0)
    gid = lax.broadcasted_iota(jnp.int32, (n_groups, tl), 0)
    gsel = jnp.zeros((n_groups, tl), jnp.bool_)
    for _ in range(topk_groups):
        _, first = _first_max(gscore, gid, n_groups)
        hit = gid == first
        gsel = gsel | hit
        gscore = jnp.where(hit, NEG_INF, gscore)
    eid = lax.broadcasted_iota(jnp.int32, (n_exp, tl), 0)
    gmask = jnp.concatenate([jnp.broadcast_to(gsel[g:g + 1], (per, tl)) for g in range(n_groups)], axis=0)
    masked = jnp.where(gmask, biased, NEG_INF)
    sel = jnp.zeros((n_exp, tl), jnp.bool_)
    hits, firsts = [], []
    for _ in range(top_k):
        _, first = _first_max(masked, eid, n_exp)
        hit = eid == first
        hits.append(hit)
        firsts.append(first)
        sel = sel | hit
        masked = jnp.where(hit, NEG_INF, masked)
    self = jnp.where(sel, 1.0, 0.0)
    ti = lax.broadcasted_iota(jnp.int32, (tl, tl), 0)
    tj = lax.broadcasted_iota(jnp.int32, (tl, tl), 1)
    before = jnp.where(ti < tj, 1.0, 0.0).astype(BF16)
    carry = carry_ref[...]
    rank_full = jnp.dot(self.astype(BF16), before, preferred_element_type=F32) + carry
    w_rows = [jnp.sum(jnp.where(hit, scores, 0.0), axis=0, keepdims=True) for hit in hits]
    r_rows = [jnp.sum(jnp.where(hit, rank_full, 0.0), axis=0, keepdims=True) for hit in hits]
    w8 = jnp.concatenate(w_rows, axis=0)
    idx_ref[...] = jnp.concatenate(firsts, axis=0)
    rank_ref[...] = jnp.concatenate(r_rows, axis=0).astype(jnp.int32)
    w_ref[...] = w8 / jnp.sum(w8, axis=0, keepdims=True) * scale
    carry = carry + jnp.sum(self, axis=1, keepdims=True)
    carry_ref[...] = carry
    cnt_ref[...] = carry.astype(jnp.int32)


def route(logits_t, router_b, tl=256):
    n_exp, t = logits_t.shape
    body = functools.partial(_route_body, n_groups=N_GROUPS, topk_groups=TOPK_GROUPS, top_k=TOP_K,
                             scale=ROUTED_SCALE)
    tok_spec = pl.BlockSpec((TOP_K, tl), lambda i: (0, i))
    return pl.pallas_call(
        body, grid=(t // tl,),
        in_specs=[pl.BlockSpec((n_exp, tl), lambda i: (0, i)), pl.BlockSpec((n_exp, 1), lambda i: (0, 0))],
        out_specs=[tok_spec, tok_spec, tok_spec, pl.BlockSpec((n_exp, 1), lambda i: (0, 0))],
        out_shape=[jax.ShapeDtypeStruct((TOP_K, t), jnp.int32), jax.ShapeDtypeStruct((TOP_K, t), jnp.int32),
                   jax.ShapeDtypeStruct((TOP_K, t), F32), jax.ShapeDtypeStruct((n_exp, 1), jnp.int32)],
        scratch_shapes=[pltpu.VMEM((n_exp, 1), F32)],
        compiler_params=_params("arbitrary"), name="route",
    )(logits_t, router_b.reshape(n_exp, 1))


def _row_copy(src_ref, src_row, dst_ref, dst_row, sem):
    return pltpu.make_async_copy(src_ref.at[pl.ds(src_row, 1)], dst_ref.at[pl.ds(dst_row, 1)], sem)


def _dispatch_body(dest_ref, pad_ref, h_ref, xs_ref, zero_ref, sem, *, top_k):
    tl = h_ref.shape[0]
    n_exp = pad_ref.shape[1]

    def issue(r, carry):
        for k in range(top_k):
            _row_copy(h_ref, r, xs_ref, dest_ref[r * top_k + k], sem.at[0]).start(priority=k % 2)
        return carry

    lax.fori_loop(0, tl, issue, 0, unroll=DMA_LOOP_UNROLL)

    @pl.when(pl.program_id(0) == 0)
    def _():
        zero_ref[...] = jnp.zeros_like(zero_ref)

        def fill(e, total):
            first, n_pad = pad_ref[0, e], pad_ref[1, e]

            def one(j, carry):
                _row_copy(zero_ref, 0, xs_ref, first + j, sem.at[1]).start()
                return carry

            lax.fori_loop(0, n_pad, one, 0)
            return total + n_pad

        total = lax.fori_loop(0, n_exp, fill, 0)

        def drain_pad(j, carry):
            _row_copy(zero_ref, 0, xs_ref, 0, sem.at[1]).wait()
            return carry

        lax.fori_loop(0, total, drain_pad, 0)

    def drain(r, carry):
        for k in range(top_k):
            _row_copy(h_ref, 0, xs_ref, 0, sem.at[0]).wait()
        return carry

    lax.fori_loop(0, tl, drain, 0, unroll=DMA_LOOP_UNROLL)


def dispatch(h_pk, dest_flat, pad_info, n_slot, top_k, tl=256):
    t, w = h_pk.shape
    return pl.pallas_call(
        functools.partial(_dispatch_body, top_k=top_k), grid=(t // tl,),
        in_specs=[pl.BlockSpec((tl * top_k,), lambda i: (i,), memory_space=pltpu.SMEM),
                  pl.BlockSpec(memory_space=pltpu.SMEM),
                  pl.BlockSpec((tl, w), lambda i: (i, 0))],
        out_specs=pl.BlockSpec(memory_space=pl.ANY),
        out_shape=jax.ShapeDtypeStruct((n_slot, w), h_pk.dtype),
        scratch_shapes=[pltpu.VMEM((8, w), h_pk.dtype), pltpu.SemaphoreType.DMA((2,))],
        compiler_params=_params("arbitrary"), name="dispatch",
    )(dest_flat, pad_info, h_pk)


def _combine_body(dest_ref, dest_next_ref, w_ref, sh_ref, x_ref, gate_ref, y_ref, o_ref, buf_ref, sem, *, top_k):
    i = pl.program_id(0)
    n = pl.num_programs(0)
    tl = x_ref.shape[0]

    def gather(d_ref, slot):
        def issue(r, carry):
            for k in range(top_k):
                _row_copy(y_ref, d_ref[r * top_k + k], buf_ref.at[slot], k * tl + r,
                          sem.at[slot]).start(priority=k % 2)
            return carry
        lax.fori_loop(0, tl, issue, 0, unroll=DMA_LOOP_UNROLL)

    @pl.when(i == 0)
    def _():
        gather(dest_ref, 0)

    @pl.when(i + 1 < n)
    def _():
        gather(dest_next_ref, (i + 1) % 2)

    slot = i % 2

    def drain(r, carry):
        for k in range(top_k):
            _row_copy(y_ref, 0, buf_ref.at[slot], 0, sem.at[slot]).wait()
        return carry

    lax.fori_loop(0, tl, drain, 0, unroll=DMA_LOOP_UNROLL)
    acc = sh_ref[...]
    w = w_ref[...]
    for k in range(top_k):
        acc = acc + buf_ref[slot, pl.ds(k * tl, tl), :] * w[:, k:k + 1]
    o_ref[...] = x_ref[...] + gate_ref[0] * acc


def combine(y_slot, dest_flat, w, shared, x, mod3, gate_idx, geom, tl=128):
    t, d = x.shape
    top_k = w.shape[1]
    n_lat, lat_len, n_batch = geom
    n_tiles = t // tl
    row = functools.partial(_mod_row, tile_rows=tl, n_lat=n_lat, lat_len=lat_len, n_batch=n_batch)
    return pl.pallas_call(
        functools.partial(_combine_body, top_k=top_k), grid=(n_tiles,),
        in_specs=[pl.BlockSpec((tl * top_k,), lambda i: (i,), memory_space=pltpu.SMEM),
                  pl.BlockSpec((tl * top_k,), lambda i: (jnp.minimum(i + 1, n_tiles - 1),),
                               memory_space=pltpu.SMEM),
                  pl.BlockSpec((tl, top_k), lambda i: (i, 0)),
                  pl.BlockSpec((tl, d), lambda i: (i, 0)),
                  pl.BlockSpec((tl, d), lambda i: (i, 0)),
                  pl.BlockSpec((1, 1, d), lambda i: (row(i) * 6 + gate_idx, 0, 0)),
                  pl.BlockSpec(memory_space=pl.ANY)],
        out_specs=pl.BlockSpec((tl, d), lambda i: (i, 0)),
        out_shape=jax.ShapeDtypeStruct((t, d), F32),
        scratch_shapes=[pltpu.VMEM((2, top_k * tl, d), F32), pltpu.SemaphoreType.DMA((2,))],
        compiler_params=_params("arbitrary"), name="combine",
    )(dest_flat, dest_flat, w, shared, x, mod3, y_slot)


def moe_ffn(x, h_pk, logits_t, router_b, exp_gate_up, exp_down, shared_gate_up, shared_down, mod3, geom):
    t = h_pk.shape[0]
    e_count = exp_gate_up.shape[0]
    blk = EXPERT_SLOT_BLOCK
    idx8, rank8, w8, counts = route(logits_t, router_b)
    counts = counts[:, 0]
    padded = (counts + blk - 1) // blk * blk
    pad_end = jnp.cumsum(padded)
    pad_start = pad_end - padded
    start8 = jnp.sum(jnp.where(idx8[None] == jnp.arange(e_count, dtype=jnp.int32)[:, None, None],
                               pad_start[:, None, None], 0), axis=0)
    dest_flat = (start8 + rank8).T.reshape(-1)
    n_blk = (t * TOP_K + e_count * (blk - 1)) // blk + 1
    n_slot = n_blk * blk
    blk_starts = jnp.arange(n_blk, dtype=jnp.int32) * blk
    blk_e = jnp.sum((pad_end[None, :] <= blk_starts[:, None]).astype(jnp.int32), axis=1)
    blk_e = jnp.minimum(blk_e, e_count - 1)
    blk_new = jnp.concatenate([jnp.ones((1,), jnp.int32), (blk_e[1:] != blk_e[:-1]).astype(jnp.int32)])
    n_used = (pad_end[-1] // blk).astype(jnp.int32).reshape(1)
    pad_info = jnp.stack([pad_start + counts, padded - counts]).astype(jnp.int32)
    x_sorted = dispatch(h_pk, dest_flat, pad_info, n_slot, TOP_K)
    y_slot = expert_blocks(x_sorted, exp_gate_up, exp_down, blk_e, blk_new, n_used, blk)
    one = jnp.ones((t // blk,), jnp.int32)
    shared = expert_blocks(h_pk, shared_gate_up[None], shared_down[None],
                           jnp.zeros((t // blk,), jnp.int32), one.at[1:].set(0),
                           jnp.full((1,), t // blk, jnp.int32), blk)
    return combine(y_slot, dest_flat, w8.T, shared, x, mod3, 5, geom)


def kernel(x, c, ctx, c_ctx, hg_lb_logits, l0_mod_w, l0_mod_b, l0_norm1, l0_norm2, l0_w_in, l0_dn_conv, l0_dn_a_log, l0_dn_dt_bias, l0_dn_norm, l0_hg_norm, l0_w_out, l0_router_w, l0_router_b, l0_exp_gate_up, l0_exp_down, l0_shared_gate_up, l0_shared_down, l1_mod_w, l1_mod_b, l1_norm1, l1_norm2, l1_w_in, l1_q_norm, l1_k_norm, l1_lambda, l1_sub_norm, l1_w_out, l1_router_w, l1_router_b, l1_exp_gate_up, l1_exp_down, l1_shared_gate_up, l1_shared_down):
    n_batch, lat_len, d = x.shape
    ctx_len = ctx.shape[1]
    n_lat = n_batch * lat_len
    n_ctx = n_batch * ctx_len
    geom = (n_lat, lat_len, n_batch)
    geom_lat_only = (n_lat, lat_len, n_batch)

    xs = jnp.concatenate([x.reshape(n_lat, d), ctx.reshape(n_ctx, d)], axis=0)
    cond = jnp.concatenate([c, c_ctx[None], jnp.zeros((MOD_ROWS - n_batch - 1, d), F32)], axis=0)

    mod3 = modulation(cond, l0_mod_w, l0_mod_b).reshape(MOD_ROWS * 6, 1, d)
    h = adaln(xs, l0_norm1, mod3, 0, geom)
    n_dn = IN0_SIZES[0] + IN0_SIZES[1]
    n_gate = IN0_SIZES[2] + IN0_SIZES[3]
    proj_dn = matmul(h, l0_w_in, n_cols=n_dn)
    proj_hg = matmul(h, l0_w_in[:, n_dn + n_gate:])
    gc, gr = gate_prep(h, l0_w_in[:, n_dn:n_dn + n_gate], l0_dn_a_log, l0_dn_dt_bias, DN_HEADS)
    conv_t = jnp.concatenate([l0_dn_conv.T, jnp.zeros((8 - l0_dn_conv.shape[1], 3 * DN_WIDTH), F32)], axis=0)
    qkv = dn_prep(proj_dn, conv_t, geom, ctx_len, DN_WIDTH, DN_HEAD_DIM)
    dn_f, dn_b = delta_scan(qkv, gc, gr, geom, ctx_len, DN_HEADS, DN_HEAD_DIM, inv_passes=DELTA_INV_PASSES)
    lb = jnp.cumsum(jax.nn.softmax(hg_lb_logits, axis=0), axis=0)[0:1]
    hg_f, hg_b = hgrn_scan(proj_hg, lb, geom, ctx_len, HG_HEADS, HG_KEY_DIM)
    y = mix_out(dn_f, dn_b, hg_f, hg_b, proj_dn, 3, proj_hg, 4, l0_dn_norm, l0_hg_norm, DN_HEAD_DIM)
    xs = matmul_residual(y, l0_w_out, xs, mod3, 2, geom)
    h, logits_t = adaln(xs, l0_norm2, mod3, 3, geom, router_w=l0_router_w)
    xs = moe_ffn(xs, h, logits_t, l0_router_b, l0_exp_gate_up, l0_exp_down, l0_shared_gate_up, l0_shared_down,
                 mod3, geom)

    mod3 = modulation(cond, l1_mod_w, l1_mod_b).reshape(MOD_ROWS * 6, 1, d)
    h = adaln(xs, l1_norm1, mod3, 0, geom)
    proj = matmul(h, l1_w_in)
    cos_t, sin_t = rope_tables(lat_len, n_batch, n_ctx, DA_HEAD_DIM)
    qkv = qk_prep(proj, l1_q_norm, l1_k_norm, cos_t, sin_t)
    lam_init = 0.8 - 0.6 * math.exp(-0.3 * 1)
    lmbda = (jnp.exp(jnp.sum(l1_lambda[0] * l1_lambda[1])) - jnp.exp(jnp.sum(l1_lambda[2] * l1_lambda[3]))
             + lam_init)
    y = diff_attention(qkv, lmbda, l1_sub_norm, n_batch, lat_len, ctx_len, DA_HEADS, DA_HEAD_DIM,
                       1.0 - lam_init)
    xl = matmul_residual(y, l1_w_out, xs[:n_lat], mod3, 2, geom_lat_only)
    h, logits_t = adaln(xl, l1_norm2, mod3, 3, geom_lat_only, router_w=l1_router_w)
    xl = moe_ffn(xl, h, logits_t, l1_router_b, l1_exp_gate_up, l1_exp_down, l1_shared_gate_up, l1_shared_down,
                 mod3, geom_lat_only)
    return xl.reshape(n_batch, lat_len, d)
```

```python
import functools
import math

import jax
import jax.numpy as jnp
import numpy as np
from jax import lax
from jax.experimental import pallas as pl
from jax.experimental.pallas import tpu as pltpu

F32 = jnp.float32
BF16 = jnp.bfloat16

NORM_EPS = 1e-6
GRID_W = 64
ROPE_BASE = 10000.0

DN_HEADS = 8
DN_HEAD_DIM = 128
DN_WIDTH = DN_HEADS * DN_HEAD_DIM
DN_CHUNK = 64
HG_HEADS = 8
HG_KEY_DIM = 128
HG_VAL_DIM = 128
HG_K_WIDTH = HG_HEADS * HG_KEY_DIM
HG_V_WIDTH = HG_HEADS * HG_VAL_DIM
HG_CHUNK = 64
IN0_SIZES = (3 * DN_WIDTH, DN_WIDTH, 2 * DN_HEADS, 2 * DN_HEADS,
             HG_K_WIDTH, 2 * HG_K_WIDTH, HG_V_WIDTH, HG_V_WIDTH)
DA_HEADS = 8
DA_HEAD_DIM = 128
N_EXPERTS = 64
TOP_K = 8
N_GROUPS = 8
TOPK_GROUPS = 4
EXPERT_FF = 512
ROUTED_SCALE = 2.5

VMEM_LIMIT_BYTES = 56 * 1024 * 1024
MOD_ROWS = 8
EXPERT_SLOT_BLOCK = 256
DMA_LOOP_UNROLL = 8
MM_TILE_M = 1024
MM_TILE_N = 1024
LANES = 128
ATTN_TILE_Q = 512
ATTN_ROW_CHUNK = 128


def _params(*sem):
    return pltpu.CompilerParams(dimension_semantics=sem, vmem_limit_bytes=VMEM_LIMIT_BYTES)


def _silu(x):
    return x * (1.0 / (1.0 + jnp.exp(-x)))


def _mod_body(c_ref, w_ref, b_ref, o_ref):
    a = _silu(c_ref[...]).astype(BF16)
    o_ref[...] = jnp.dot(a, w_ref[...].astype(BF16), preferred_element_type=F32) + b_ref[...]


def modulation(cond, w, b, tn=1024):
    m, k = cond.shape
    n = w.shape[1]
    return pl.pallas_call(
        _mod_body, grid=(n // tn,),
        in_specs=[pl.BlockSpec((m, k), lambda j: (0, 0)),
                  pl.BlockSpec((k, tn), lambda j: (0, j)),
                  pl.BlockSpec((1, tn), lambda j: (0, j))],
        out_specs=pl.BlockSpec((m, tn), lambda j: (0, j)),
        out_shape=jax.ShapeDtypeStruct((m, n), F32),
        compiler_params=_params("arbitrary"), name="modulation",
    )(cond, w, b.reshape(1, n))


def _mod_row(tile, tile_rows, n_lat, lat_len, n_batch):
    start = tile * tile_rows
    return jnp.where(start < n_lat, start // lat_len, n_batch)


def _pack_bf16_pairs(h):
    half = h.shape[1] // 2
    bits = lax.bitcast_convert_type(h.astype(BF16).astype(F32), jnp.uint32)
    return (bits[:, :half] & jnp.uint32(0xFFFF0000)) | (bits[:, half:] >> jnp.uint32(16))


def _unpack_bf16_pairs(pk):
    hi = lax.bitcast_convert_type(pk & jnp.uint32(0xFFFF0000), F32).astype(BF16)
    lo = lax.bitcast_convert_type(pk << jnp.uint32(16), F32).astype(BF16)
    return hi, lo


def _store_slabs(ref, x):
    r, w = x.shape
    c_n = w // LANES
    for c in range(c_n):
        ref[pl.ds(c, r, stride=c_n), :] = x[:, c * LANES:(c + 1) * LANES]


def _load_slabs(ref, row0, r, c_n):
    return jnp.concatenate([ref[pl.ds(row0 * c_n + c, r, stride=c_n), :] for c in range(c_n)], axis=1)


def _adaln_body(x_ref, nw_ref, shift_ref, scale_ref, o_ref):
    x = x_ref[...]
    y = x * lax.rsqrt(jnp.mean(x * x, axis=-1, keepdims=True) + NORM_EPS) * nw_ref[...]
    o_ref[...] = (y * (1.0 + scale_ref[0]) + shift_ref[0]).astype(o_ref.dtype)


def _adaln_router_body(x_ref, nw_ref, shift_ref, scale_ref, rw_ref, o_ref, lg_ref):
    x = x_ref[...]
    y = x * lax.rsqrt(jnp.mean(x * x, axis=-1, keepdims=True) + NORM_EPS) * nw_ref[...]
    h = y * (1.0 + scale_ref[0]) + shift_ref[0]
    _store_slabs(o_ref, _pack_bf16_pairs(h))
    lg_ref[...] = lax.dot_general(rw_ref[...], h, (((1,), (1,)), ((), ())), preferred_element_type=F32,
                                  precision=lax.Precision.HIGHEST)


def adaln(x, norm_w, mod3, shift_idx, geom, router_w=None, tl=256):
    t, d = x.shape
    n_lat, lat_len, n_batch = geom
    row = functools.partial(_mod_row, tile_rows=tl, n_lat=n_lat, lat_len=lat_len, n_batch=n_batch)
    in_specs = [pl.BlockSpec((tl, d), lambda i: (i, 0)),
                pl.BlockSpec((1, d), lambda i: (0, 0)),
                pl.BlockSpec((1, 1, d), lambda i: (row(i) * 6 + shift_idx, 0, 0)),
                pl.BlockSpec((1, 1, d), lambda i: (row(i) * 6 + shift_idx + 1, 0, 0))]
    args = [x, norm_w.reshape(1, d), mod3, mod3]
    if router_w is None:
        return pl.pallas_call(
            _adaln_body, grid=(t // tl,), in_specs=in_specs,
            out_specs=pl.BlockSpec((tl, d), lambda i: (i, 0)),
            out_shape=jax.ShapeDtypeStruct((t, d), BF16),
            compiler_params=_params("arbitrary"), name="adaln")(*args)
    e = router_w.shape[1]
    return pl.pallas_call(
        _adaln_router_body, grid=(t // tl,),
        in_specs=in_specs + [pl.BlockSpec((e, d), lambda i: (0, 0))],
        out_specs=[pl.BlockSpec((tl * (d // 2 // LANES), LANES), lambda i: (i, 0)),
                   pl.BlockSpec((e, tl), lambda i: (0, i))],
        out_shape=[jax.ShapeDtypeStruct((t * (d // 2 // LANES), LANES), jnp.uint32),
                   jax.ShapeDtypeStruct((e, t), F32)],
        compiler_params=_params("arbitrary"), name="adaln_router")(*args, router_w.T)


def _mm_body(a_ref, w_ref, o_ref, wb_ref):
    @pl.when(pl.program_id(1) == 0)
    def _():
        wb_ref[...] = w_ref[...].astype(BF16)
    o_ref[...] = jnp.dot(a_ref[...], wb_ref[...], preferred_element_type=F32).astype(o_ref.dtype)


def _mm_res_body(a_ref, w_ref, res_ref, gate_ref, o_ref, wb_ref):
    @pl.when(pl.program_id(1) == 0)
    def _():
        wb_ref[...] = w_ref[...].astype(BF16)
    acc = jnp.dot(a_ref[...], wb_ref[...], preferred_element_type=F32)
    o_ref[...] = res_ref[...] + gate_ref[0] * acc


def matmul(a, w, n_cols=None, col_block0=0, tm=MM_TILE_M, tn=MM_TILE_N, out_dtype=F32):
    m, k = a.shape
    n = w.shape[1] if n_cols is None else n_cols
    return pl.pallas_call(
        _mm_body, grid=(n // tn, m // tm),
        in_specs=[pl.BlockSpec((tm, k), lambda j, i: (i, 0)),
                  pl.BlockSpec((k, tn), lambda j, i: (0, j + col_block0))],
        out_specs=pl.BlockSpec((tm, tn), lambda j, i: (i, j)),
        out_shape=jax.ShapeDtypeStruct((m, n), out_dtype),
        scratch_shapes=[pltpu.VMEM((k, tn), BF16)],
        compiler_params=_params("arbitrary", "arbitrary"), name="matmul",
    )(a, w)


def matmul_residual(a, w, res, mod3, gate_idx, geom, tm=MM_TILE_M, tn=MM_TILE_N):
    m, k = a.shape
    n = w.shape[1]
    n_lat, lat_len, n_batch = geom
    row = functools.partial(_mod_row, tile_rows=tm, n_lat=n_lat, lat_len=lat_len, n_batch=n_batch)
    nb = n // tn
    return pl.pallas_call(
        _mm_res_body, grid=(n // tn, m // tm),
        in_specs=[pl.BlockSpec((tm, k), lambda j, i: (i, 0)),
                  pl.BlockSpec((k, tn), lambda j, i: (0, j)),
                  pl.BlockSpec((tm, tn), lambda j, i: (i, j)),
                  pl.BlockSpec((1, 1, tn), lambda j, i: (row(i) * 6 + gate_idx, 0, j))],
        out_specs=pl.BlockSpec((tm, tn), lambda j, i: (i, j)),
        out_shape=jax.ShapeDtypeStruct((m, n), F32),
        scratch_shapes=[pltpu.VMEM((k, tn), BF16)],
        compiler_params=_params("arbitrary", "arbitrary"), name="matmul_residual",
    )(a, w, res, mod3)


def _expert_body(blk_e_ref, blk_new_ref, n_used_ref, x_ref, gu_ref, dn_ref, o_ref, gub_ref, dnb_ref):
    i = pl.program_id(0)

    @pl.when(blk_new_ref[i] == 1)
    def _():
        gub_ref[...] = gu_ref[0].astype(BF16)
        dnb_ref[...] = dn_ref[0].astype(BF16)

    @pl.when(i < n_used_ref[0])
    def _():
        ff, d = dnb_ref.shape
        half = d // 2
        blk = x_ref.shape[0] // (half // LANES)
        x_hi, x_lo = _unpack_bf16_pairs(_load_slabs(x_ref, 0, blk, half // LANES))
        h1 = (jnp.dot(x_hi, gub_ref[:half], preferred_element_type=F32)
              + jnp.dot(x_lo, gub_ref[half:], preferred_element_type=F32))
        act = (_silu(h1[:, :ff]) * h1[:, ff:]).astype(BF16)
        _store_slabs(o_ref, jnp.dot(act, dnb_ref[...], preferred_element_type=F32))

    @pl.when(i >= n_used_ref[0])
    def _():
        o_ref[...] = jnp.zeros_like(o_ref)


def expert_blocks(x, gate_up, down, blk_e, blk_new, n_used, blk):
    _, d, f2 = gate_up.shape
    xc, yc = d // 2 // LANES, d // LANES
    s = x.shape[0] // xc
    n_blk = s // blk
    grid_spec = pltpu.PrefetchScalarGridSpec(
        num_scalar_prefetch=3, grid=(n_blk,),
        in_specs=[pl.BlockSpec((blk * xc, LANES), lambda i, be, bn, nu: (jnp.minimum(i, nu[0] - 1), 0)),
                  pl.BlockSpec((1, d, f2), lambda i, be, bn, nu: (be[i], 0, 0)),
                  pl.BlockSpec((1, f2 // 2, d), lambda i, be, bn, nu: (be[i], 0, 0))],
        out_specs=pl.BlockSpec((blk * yc, LANES), lambda i, be, bn, nu: (i, 0)),
        scratch_shapes=[pltpu.VMEM((d, f2), BF16), pltpu.VMEM((f2 // 2, d), BF16)])
    return pl.pallas_call(
        _expert_body, grid_spec=grid_spec,
        out_shape=jax.ShapeDtypeStruct((s * yc, LANES), F32),
        compiler_params=_params("arbitrary"), name="expert_blocks",
    )(blk_e, blk_new, n_used, x, gate_up, down)


def _qk_prep_body(x_ref, qw_ref, kw_ref, cos_ref, sin_ref, o_ref, *, n_qk_groups, head_dim):
    cos = cos_ref[...]
    sin = sin_ref[...]
    lane = lax.broadcasted_iota(jnp.int32, cos.shape, 1)
    first = (lane % (head_dim // 2)) < (head_dim // 4)
    for g in range(n_qk_groups):
        sl = slice(g * head_dim, (g + 1) * head_dim)
        x = x_ref[:, sl]
        w = qw_ref[...] if g < n_qk_groups // 2 else kw_ref[...]
        y = x * lax.rsqrt(jnp.mean(x * x, axis=-1, keepdims=True) + NORM_EPS) * w
        swapped = jnp.where(first, pltpu.roll(y, head_dim - head_dim // 4, 1), pltpu.roll(y, head_dim // 4, 1))
        o_ref[:, sl] = (y * cos + swapped * sin).astype(o_ref.dtype)
    rest = n_qk_groups * head_dim
    o_ref[:, rest:] = x_ref[:, rest:].astype(o_ref.dtype)


def qk_prep(proj, q_norm, k_norm, cos_t, sin_t, tl=256):
    t, n = proj.shape
    hd = q_norm.shape[0]
    n_groups = (2 * n // 3) // hd
    body = functools.partial(_qk_prep_body, n_qk_groups=n_groups, head_dim=hd)
    return pl.pallas_call(
        body, grid=(t // tl,),
        in_specs=[pl.BlockSpec((tl, n), lambda i: (i, 0)),
                  pl.BlockSpec((1, hd), lambda i: (0, 0)),
                  pl.BlockSpec((1, hd), lambda i: (0, 0)),
                  pl.BlockSpec((tl, hd), lambda i: (i, 0)),
                  pl.BlockSpec((tl, hd), lambda i: (i, 0))],
        out_specs=pl.BlockSpec((tl, n), lambda i: (i, 0)),
        out_shape=jax.ShapeDtypeStruct((t, n), BF16),
        compiler_params=_params("arbitrary"), name="qk_prep",
    )(proj, q_norm.reshape(1, hd), k_norm.reshape(1, hd), cos_t, sin_t)


def rope_tables(n_lat_tokens_per_sample, n_batch, n_ctx_tokens, head_dim):
    quarter = head_dim // 4
    inv_freq = ROPE_BASE ** (-jnp.arange(quarter, dtype=F32) / quarter)
    rows = n_lat_tokens_per_sample // GRID_W
    row = jnp.repeat(jnp.arange(rows, dtype=F32), GRID_W)
    col = jnp.tile(jnp.arange(GRID_W, dtype=F32), rows)
    ang_r = row[:, None] * inv_freq[None, :]
    ang_c = col[:, None] * inv_freq[None, :]
    cos = jnp.concatenate([jnp.cos(ang_r), jnp.cos(ang_r), jnp.cos(ang_c), jnp.cos(ang_c)], axis=-1)
    sin = jnp.concatenate([-jnp.sin(ang_r), jnp.sin(ang_r), -jnp.sin(ang_c), jnp.sin(ang_c)], axis=-1)
    cos = jnp.concatenate([jnp.tile(cos, (n_batch, 1)), jnp.ones((n_ctx_tokens, head_dim), F32)], axis=0)
    sin = jnp.concatenate([jnp.tile(sin, (n_batch, 1)), jnp.zeros((n_ctx_tokens, head_dim), F32)], axis=0)
    return cos, sin


def _diff_attn_rows(lam, q_ref, k_all, v_all, sw_ref, o_ref, rows, head_dim, out_scale):
    c = head_dim ** -0.5 * math.log2(math.e)
    es, invs = [], []
    for s in range(2):
        sl = slice(s * head_dim, (s + 1) * head_dim)
        sc = lax.dot_general(q_ref[rows, sl], k_all[:, sl], (((1,), (1,)), ((), ())), preferred_element_type=F32)
        yield
        e = jnp.exp2((sc - jnp.max(sc, axis=-1, keepdims=True)) * c)
        invs.append(1.0 / jnp.sum(e, axis=-1, keepdims=True))
        es.append(e.astype(BF16))
        yield
    v = v_all[...]
    o0 = jnp.dot(es[0], v, preferred_element_type=F32)
    o1 = jnp.dot(es[1], v, preferred_element_type=F32)
    yield
    o = o0 * invs[0] - (lam * invs[1]) * o1
    y = o * lax.rsqrt(jnp.mean(o * o, axis=-1, keepdims=True) + NORM_EPS) * sw_ref[...]
    o_ref[rows, :] = (y * out_scale).astype(o_ref.dtype)
    yield


def _diff_attn_body(lam_ref, q_ref, kl_ref, kc_ref, vl_ref, vc_ref, sw_ref, o_ref, k_all, v_all, *,
                    head_dim, out_scale, row_chunk):
    @pl.when(pl.program_id(2) == 0)
    def _():
        n_l = kl_ref.shape[0]
        k_all[:n_l] = kl_ref[...]
        k_all[n_l:] = kc_ref[...]
        v_all[:n_l] = vl_ref[...]
        v_all[n_l:] = vc_ref[...]

    lam = lam_ref[0]
    tq = q_ref.shape[0]
    _run_interleaved(
        _diff_attn_rows(lam, q_ref, k_all, v_all, sw_ref, o_ref, slice(r, r + row_chunk), head_dim, out_scale)
        for r in range(0, tq, row_chunk))


def diff_attention(qkv, lmbda, sub_norm, n_batch, lat_len, ctx_len, n_heads, head_dim, out_scale,
                   tq=ATTN_TILE_Q, row_chunk=ATTN_ROW_CHUNK):
    hw = 2 * head_dim
    nq = lat_len // tq
    ctx_blk0 = n_batch * lat_len // ctx_len
    body = functools.partial(_diff_attn_body, head_dim=head_dim, out_scale=out_scale, row_chunk=row_chunk)
    n_keys = lat_len + ctx_len
    return pl.pallas_call(
        body, grid=(n_batch, n_heads, nq),
        in_specs=[pl.BlockSpec(memory_space=pltpu.SMEM),
                  pl.BlockSpec((tq, hw), lambda b, h, i: (b * nq + i, h)),
                  pl.BlockSpec((lat_len, hw), lambda b, h, i: (b, n_heads + h)),
                  pl.BlockSpec((ctx_len, hw), lambda b, h, i: (ctx_blk0 + b, n_heads + h)),
                  pl.BlockSpec((lat_len, hw), lambda b, h, i: (b, 2 * n_heads + h)),
                  pl.BlockSpec((ctx_len, hw), lambda b, h, i: (ctx_blk0 + b, 2 * n_heads + h)),
                  pl.BlockSpec((1, hw), lambda b, h, i: (0, 0))],
        out_specs=pl.BlockSpec((tq, hw), lambda b, h, i: (b * nq + i, h)),
        out_shape=jax.ShapeDtypeStruct((n_batch * lat_len, n_heads * hw), BF16),
        scratch_shapes=[pltpu.VMEM((n_keys, hw), BF16), pltpu.VMEM((n_keys, hw), BF16)],
        compiler_params=_params("arbitrary", "arbitrary", "arbitrary"), name="diff_attention",
    )(lmbda.reshape(1), qkv, qkv, qkv, qkv, qkv, sub_norm.reshape(1, hw))


SCAN_TILE = 256
CHUNK = 64
SUB = 16
NEG_BIG = -1e30
NEG_INF = float("-inf")
DELTA_INV_PASSES = 1
DELTA_HEADS_PER_STEP = 4
HGRN_HEADS_PER_STEP = 2


def _sigmoid(x):
    return 1.0 / (1.0 + jnp.exp(-x))


def _dot(a, b):
    return jnp.dot(a.astype(BF16), b.astype(BF16), preferred_element_type=F32)


def _dot_nt(a, b):
    return lax.dot_general(a.astype(BF16), b.astype(BF16), (((1,), (1,)), ((), ())),
                           preferred_element_type=F32)


def _split3(x):
    hi = x.astype(BF16)
    r = x - hi.astype(F32)
    mid = r.astype(BF16)
    lo = (r - mid.astype(F32)).astype(BF16)
    return hi, mid, lo


def _dot_exact_lhs01(m01, x):
    hi, mid, lo = _split3(x)
    m = m01.astype(BF16)
    return (jnp.dot(m, hi, preferred_element_type=F32) + jnp.dot(m, mid, preferred_element_type=F32)
            + jnp.dot(m, lo, preferred_element_type=F32))


def _dot_exact_rhs01(x, m01):
    hi, mid, lo = _split3(x)
    m = m01.astype(BF16)
    return (jnp.dot(hi, m, preferred_element_type=F32) + jnp.dot(mid, m, preferred_element_type=F32)
            + jnp.dot(lo, m, preferred_element_type=F32))


def _dot3(a, b):
    ah = a.astype(BF16)
    al = (a - ah.astype(F32)).astype(BF16)
    bh = b.astype(BF16)
    bl = (b - bh.astype(F32)).astype(BF16)
    return (jnp.dot(ah, bh, preferred_element_type=F32) + jnp.dot(ah, bl, preferred_element_type=F32)
            + jnp.dot(al, bh, preferred_element_type=F32))


def _tile_masks(n, reverse):
    i = lax.broadcasted_iota(jnp.int32, (n, n), 0)
    j = lax.broadcasted_iota(jnp.int32, (n, n), 1)
    same = (i // CHUNK) == (j // CHUNK)
    if reverse:
        return same & (i <= j), same & (i < j)
    return same & (i >= j), same & (i > j)


def _segment_tile(b, s, reverse, nl, nc, ctx_tile0):
    if reverse:
        return jnp.where(s < nc, ctx_tile0 + b * nc + (nc - 1 - s), b * nl + (nl - 1 - (s - nc)))
    return jnp.where(s < nc, ctx_tile0 + b * nc + s, b * nl + (s - nc))


def _dn_prep_body(x_ref, prev_ref, next_ref, w_ref, o_ref, *, tiles_per_lat_seg, tiles_per_ctx_seg, n_lat_tiles,
                  head_dim, q_scale):
    i = pl.program_id(0)
    j = pl.program_id(1)
    is_lat = i < n_lat_tiles
    pos = jnp.where(is_lat, i % tiles_per_lat_seg, (i - n_lat_tiles) % tiles_per_ctx_seg)
    seg_first = pos == 0
    seg_last = pos == jnp.where(is_lat, tiles_per_lat_seg, tiles_per_ctx_seg) - 1
    x = x_ref[...]
    tl = x.shape[0]
    prev = jnp.where(seg_first, 0.0, prev_ref[...])
    nxt = jnp.where(seg_last, 0.0, next_ref[...])
    xp = jnp.concatenate([prev, x, nxt], axis=0)
    w = w_ref[...]
    n_taps = 5
    acc = None
    for t in range(n_taps):
        off = 8 + t - n_taps // 2
        term = xp[off:off + tl] * w[t:t + 1]
        acc = term if acc is None else acc + term
    y = _silu(acc)
    scale = jnp.where(j == 0, q_scale, 1.0)
    outs = []
    for h in range(y.shape[1] // head_dim):
        yh = y[:, h * head_dim:(h + 1) * head_dim]
        nrm = lax.rsqrt(jnp.sum(yh * yh, axis=-1, keepdims=True) + 1e-6) * scale
        outs.append(yh * jnp.where(j == 2, 1.0, nrm))
    o_ref[...] = jnp.concatenate(outs, axis=1)


def dn_prep(proj_dn, conv_w_t, geom, ctx_len, width, head_dim):
    n_lat, lat_len, n_batch = geom
    t = proj_dn.shape[0]
    tl = SCAN_TILE
    rows8 = tl // 8
    n_tiles = t // tl
    body = functools.partial(_dn_prep_body, tiles_per_lat_seg=lat_len // tl, tiles_per_ctx_seg=ctx_len // tl,
                             n_lat_tiles=n_lat // tl,
                             head_dim=head_dim, q_scale=head_dim ** -0.5)
    last8 = t // 8 - 1
    return pl.pallas_call(
        body, grid=(n_tiles, 3),
        in_specs=[pl.BlockSpec((tl, width), lambda i, j: (i, j)),
                  pl.BlockSpec((8, width), lambda i, j: (jnp.maximum(i * rows8 - 1, 0), j)),
                  pl.BlockSpec((8, width), lambda i, j: (jnp.minimum((i + 1) * rows8, last8), j)),
                  pl.BlockSpec((8, width), lambda i, j: (0, j))],
        out_specs=pl.BlockSpec((tl, width), lambda i, j: (i, j)),
        out_shape=jax.ShapeDtypeStruct((t, 3 * width), F32),
        compiler_params=_params("arbitrary", "arbitrary"), name="dn_prep",
    )(proj_dn, proj_dn, proj_dn, conv_w_t)


def _softplus(x):
    return jnp.maximum(x, 0.0) + jnp.log(1.0 + jnp.exp(-jnp.abs(x)))


def _gate_prep_body(h_ref, wc_ref, wr_ref, alog_c_ref, dtb_c_ref, alog_r_ref, dtb_r_ref, gc_ref, gr_ref, *, n_heads):
    h = h_ref[...]
    tl = h.shape[0]
    nd = 2 * n_heads
    raw_c = jnp.dot(h, wc_ref[...].astype(BF16), preferred_element_type=F32)
    raw_r = lax.dot_general(wr_ref[...].astype(BF16), h, (((1,), (1,)), ((), ())),
                            preferred_element_type=F32)
    g_c = -jnp.exp(alog_c_ref[...]) * _softplus(raw_c[:, :nd] + dtb_c_ref[...])
    g_r = -jnp.exp(alog_r_ref[...]) * _softplus(raw_r[:nd, :] + dtb_r_ref[...])
    incl_f, _ = _tile_masks(tl, False)
    incl_b, _ = _tile_masks(tl, True)
    one_f = jnp.where(incl_f, 1.0, 0.0)
    one_b = jnp.where(incl_b, 1.0, 0.0)
    cum_c = jnp.concatenate([_dot_exact_lhs01(one_f, g_c[:, :n_heads]),
                             _dot_exact_lhs01(one_b, g_c[:, n_heads:])], axis=1)
    cum_r = jnp.concatenate([_dot_exact_rhs01(g_r[:n_heads, :], one_b),
                             _dot_exact_rhs01(g_r[n_heads:, :], one_f)], axis=0)
    beta_c = _sigmoid(raw_c[:, nd:])
    gc_ref[...] = jnp.concatenate([cum_c, beta_c], axis=1)
    gr_ref[...] = jnp.concatenate([cum_r, jnp.zeros_like(cum_r)], axis=0)


def gate_prep(h_bf, w_gate, a_log, dt_bias, n_heads):
    t, d = h_bf.shape
    tl = SCAN_TILE
    nd = 2 * n_heads
    body = functools.partial(_gate_prep_body, n_heads=n_heads)
    full = lambda shape: pl.BlockSpec(shape, lambda i: (0, 0))
    return pl.pallas_call(
        body, grid=(t // tl,),
        in_specs=[pl.BlockSpec((tl, d), lambda i: (i, 0)), full((d, 2 * nd)), full((2 * nd, d)),
                  full((1, nd)), full((1, nd)), full((nd, 1)), full((nd, 1))],
        out_specs=[pl.BlockSpec((tl, 2 * nd), lambda i: (i, 0)), pl.BlockSpec((2 * nd, tl), lambda i: (0, i))],
        out_shape=[jax.ShapeDtypeStruct((t, 2 * nd), F32), jax.ShapeDtypeStruct((2 * nd, t), F32)],
        compiler_params=_params("arbitrary"), name="gate_prep",
    )(h_bf, w_gate, w_gate.T, a_log.reshape(1, nd), dt_bias.reshape(1, nd),
      a_log.reshape(nd, 1), dt_bias.reshape(nd, 1))


def _select_col(x, idx):
    lane = lax.broadcasted_iota(jnp.int32, x.shape, 1)
    return jnp.sum(jnp.where(lane == idx, x, 0.0), axis=1, keepdims=True)


def _select_row(x, idx):
    row = lax.broadcasted_iota(jnp.int32, x.shape, 0)
    return jnp.sum(jnp.where(row == idx, x, 0.0), axis=0, keepdims=True)


def _run_interleaved(chains):
    chains = list(chains)
    while chains:
        alive = []
        for ch in chains:
            try:
                next(ch)
                alive.append(ch)
            except StopIteration:
                pass
        chains = alive


def _delta_chain(q, k, v, gc_col, gc_row, beta_col, s_ref, o_ref, cols, reverse, inv_passes):
    tl, kd_ = k.shape
    n_chunks = tl // CHUNK
    incl, strict = _tile_masks(tl, reverse)
    decay = jnp.exp(jnp.where(incl, gc_col - gc_row, NEG_BIG))
    kb, qb = k.astype(BF16), q.astype(BF16)
    kkt = _dot_nt(kb, kb)
    qkt = _dot_nt(qb, kb)
    yield
    x = jnp.where(strict, kkt * (-beta_col) * decay, 0.0)
    dot_inv = _dot3 if inv_passes == 3 else _dot
    ri = lax.broadcasted_iota(jnp.int32, (tl, tl), 0)
    ci = lax.broadcasted_iota(jnp.int32, (tl, tl), 1)
    r = jnp.where(ri == ci, 1.0, 0.0) + x
    n_sq = int(math.log2(CHUNK)) - 1
    for _ in range(n_sq):
        x = dot_inv(x, x)
        r = r + dot_inv(r, x)
        yield
    e_g = jnp.exp(gc_col)
    rhs = jnp.concatenate([v * beta_col, k * (beta_col * e_g)], axis=1)
    sol = dot_inv(r, rhs)
    yield
    u0, w = sol[:, :v.shape[1]], sol[:, v.shape[1]:]
    attn = (qkt * decay).astype(BF16)
    o0 = _dot(attn, u0)
    qe = q * e_g - _dot(attn, w)
    tot_rows = []
    for c in range(n_chunks):
        last = c * CHUNK if reverse else c * CHUNK + CHUNK - 1
        tot_rows.append(jnp.broadcast_to(gc_col[last:last + 1, :], (CHUNK, 1)))
    tot = jnp.concatenate(tot_rows, axis=0)
    kdec_t = jnp.transpose(k * jnp.exp(tot - gc_col))
    wu = jnp.concatenate([-w, u0], axis=1).astype(BF16)
    lane = lax.broadcasted_iota(jnp.int32, kdec_t.shape, 1)
    pns = [_dot(jnp.where((lane // CHUNK) == c, kdec_t, 0.0), wu) for c in range(n_chunks)]
    yield
    order = range(n_chunks - 1, -1, -1) if reverse else range(n_chunks)
    for c in order:
        rows = slice(c * CHUNK, (c + 1) * CHUNK)
        pn = pns[c]
        lhs = jnp.concatenate([qe[rows], pn[:, :kd_]], axis=0)
        s = s_ref[...]
        res = _dot(lhs, s)
        o_ref[rows, cols] = o0[rows] + res[:CHUNK]
        last = c * CHUNK if reverse else c * CHUNK + CHUNK - 1
        gl = jnp.exp(gc_col[last:last + 1, :])
        s_ref[...] = gl * s + res[CHUNK:] + pn[:, kd_:]
        yield


def _delta_body(qf_ref, kf_ref, vf_ref, gcf_ref, grf_ref, qb_ref, kb_ref, vb_ref, gcb_ref, grb_ref,
                of_ref, ob_ref, sf_ref, sb_ref, *, n_heads, head_dim, inv_passes):
    hg = pl.program_id(1)
    heads_per_step = sf_ref.shape[0]

    @pl.when(pl.program_id(2) == 0)
    def _():
        sf_ref[...] = jnp.zeros_like(sf_ref)
        sb_ref[...] = jnp.zeros_like(sb_ref)

    chains = []
    for g in range(heads_per_step):
        cols = slice(g * head_dim, (g + 1) * head_dim)
        for reverse, (q_ref, k_ref, v_ref, gc_ref, gr_ref, o_ref, s_ref) in enumerate(
                [(qf_ref, kf_ref, vf_ref, gcf_ref, grf_ref, of_ref, sf_ref),
                 (qb_ref, kb_ref, vb_ref, gcb_ref, grb_ref, ob_ref, sb_ref)]):
            idx = reverse * n_heads + hg * heads_per_step + g
            gcs = gc_ref[...]
            gc_col = _select_col(gcs, idx)
            beta_col = _select_col(gcs, 2 * n_heads + idx)
            gc_row = _select_row(gr_ref[...], idx)
            chains.append(_delta_chain(q_ref[:, cols], k_ref[:, cols], v_ref[:, cols], gc_col, gc_row, beta_col,
                                       s_ref.at[g], o_ref, cols, bool(reverse), inv_passes))
    _run_interleaved(chains)


def delta_scan(qkv, gc, gr, geom, ctx_len, n_heads, head_dim, inv_passes=3, heads_per_step=DELTA_HEADS_PER_STEP):
    n_lat, lat_len, n_batch = geom
    t = qkv.shape[0]
    tl = SCAN_TILE
    nl, nc, ctx0 = lat_len // tl, ctx_len // tl, n_lat // tl
    n_hg = n_heads // heads_per_step
    gw = heads_per_step * head_dim
    tile = functools.partial(_segment_tile, nl=nl, nc=nc, ctx_tile0=ctx0)
    specs = []
    for reverse in (False, True):
        tix = functools.partial(tile, reverse=reverse)
        specs += [pl.BlockSpec((tl, gw), lambda b, h, s, tix=tix: (tix(b, s), h)),
                  pl.BlockSpec((tl, gw), lambda b, h, s, tix=tix: (tix(b, s), n_hg + h)),
                  pl.BlockSpec((tl, gw), lambda b, h, s, tix=tix: (tix(b, s), 2 * n_hg + h)),
                  pl.BlockSpec((tl, 4 * n_heads), lambda b, h, s, tix=tix: (tix(b, s), 0)),
                  pl.BlockSpec((4 * n_heads, tl), lambda b, h, s, tix=tix: (0, tix(b, s)))]
    out_specs = [pl.BlockSpec((tl, gw), lambda b, h, s, tix=functools.partial(tile, reverse=r): (tix(b, s), h))
                 for r in (False, True)]
    body = functools.partial(_delta_body, n_heads=n_heads, head_dim=head_dim, inv_passes=inv_passes)
    return pl.pallas_call(
        body, grid=(n_batch, n_hg, nl + nc), in_specs=specs, out_specs=out_specs,
        out_shape=[jax.ShapeDtypeStruct((t, n_heads * head_dim), F32)] * 2,
        scratch_shapes=[pltpu.VMEM((heads_per_step, head_dim, head_dim), F32)] * 2,
        compiler_params=_params("arbitrary", "arbitrary", "arbitrary"), name="delta_scan",
    )(qkv, qkv, qkv, gc, gr, qkv, qkv, qkv, gc, gr)


def _hgrn_diag(hq_ref, hf_ref, hv_ref, cols, lb, stage_ref, diag_ref, reverse):
    n_sub = hq_ref.shape[0] // SUB
    for n, ref in enumerate((hq_ref, hf_ref, hv_ref)):
        stage_ref[n] = ref[:, cols]

    def slab(n, r):
        return stage_ref.at[n][pl.ds(r, n_sub, stride=SUB), :]

    q_x = [_silu(slab(0, r)) for r in range(SUB)]
    f_x = [lb + (1.0 - lb) * _sigmoid(slab(1, r)) for r in range(SUB)]
    k_x = [1.0 - f for f in f_x]
    lf_x = [jnp.log(f) for f in f_x]
    v_x = [slab(2, r) for r in range(SUB)]
    p_x = [None] * SUB
    scan_rows = range(SUB - 1, -1, -1) if reverse else range(SUB)
    acc = None
    for r in scan_rows:
        acc = lf_x[r] if acc is None else acc + lf_x[r]
        p_x[r] = acc
    pairs = [(i, j) for i in range(SUB) for j in range(SUB) if (j >= i if reverse else j <= i)]
    terms = []
    for i, j in pairs:
        qk = q_x[i] * k_x[j]
        terms.append((qk if i == j else qk * jnp.exp(p_x[i] - p_x[j])).astype(BF16))
    kd_ = terms[0].shape[1]
    a_rep = jnp.dot(jnp.concatenate(terms, axis=0), jnp.ones((kd_, kd_), BF16), preferred_element_type=F32)
    o_x = [None] * SUB
    for n, (i, j) in enumerate(pairs):
        contrib = a_rep[n * n_sub:(n + 1) * n_sub] * v_x[j]
        o_x[i] = contrib if o_x[i] is None else o_x[i] + contrib
    for r in range(SUB):
        diag_ref[pl.ds(r, n_sub, stride=SUB), :] = o_x[r]


def _hgrn_chain(hq_ref, hf_ref, hv_ref, cols, lb, st_ref, o_ref, stage_ref, diag_ref, reverse):
    hq, hf, hv = hq_ref[:, cols], hf_ref[:, cols], hv_ref[:, cols]
    tl, kd_ = hq.shape
    n_chunks = tl // CHUNK
    n_sub = tl // SUB
    sub_per_chunk = CHUNK // SUB
    q = _silu(hq)
    f = lb + (1.0 - lb) * _sigmoid(hf)
    k = 1.0 - f
    lf = jnp.log(f)
    incl, _ = _tile_masks(tl, reverse)
    cum = _dot_exact_lhs01(jnp.where(incl, 1.0, 0.0), lf)
    excl = cum - lf

    def bcast_rows(src, row, n):
        return jnp.broadcast_to(src[row:row + 1, :], (n, kd_))

    chunk_last = [(c * CHUNK if reverse else c * CHUNK + CHUNK - 1) for c in range(n_chunks)]
    tot = jnp.concatenate([bcast_rows(cum, chunk_last[c], CHUNK) for c in range(n_chunks)], axis=0)
    sub_first = [(m * SUB + SUB - 1 if reverse else m * SUB) for m in range(n_sub)]
    r_sub = jnp.concatenate([bcast_rows(excl, sub_first[m], SUB) for m in range(n_sub)], axis=0)
    q_t = q * jnp.exp(cum - r_sub)
    qd = q * jnp.exp(cum)
    kd = k * jnp.exp(tot - cum)
    vb = hv.astype(BF16)

    i = lax.broadcasted_iota(jnp.int32, (tl, tl), 0)
    j = lax.broadcasted_iota(jnp.int32, (tl, tl), 1)
    same = (i // CHUNK) == (j // CHUNK)
    pos_i = (i % CHUNK) // SUB
    pos_j = (j % CHUNK) // SUB
    if reverse:
        pos_i, pos_j = sub_per_chunk - 1 - pos_i, sub_per_chunk - 1 - pos_j
    a_off = jnp.zeros((tl, tl), F32)
    for lvl in range(1, sub_per_chunk):
        ref_rows = []
        for c in range(n_chunks):
            m = c * sub_per_chunk + (sub_per_chunk - 1 - lvl if reverse else lvl)
            ref_rows.append(bcast_rows(excl, sub_first[m], CHUNK))
        r_lvl = jnp.concatenate(ref_rows, axis=0)
        k_t = k * jnp.exp(jnp.minimum(r_lvl - cum, 0.0))
        a_l = _dot_nt(q_t, k_t)
        a_off = a_off + jnp.where(same & (pos_i == lvl) & (pos_j < lvl), a_l, 0.0)
        yield
    o_intra = _dot(a_off, vb)
    yield

    _hgrn_diag(hq_ref, hf_ref, hv_ref, cols, lb, stage_ref, diag_ref, reverse)
    yield
    o_intra = o_intra + diag_ref[...]

    v_t = jnp.transpose(hv)
    lane = lax.broadcasted_iota(jnp.int32, v_t.shape, 1)
    kdb = kd.astype(BF16)
    n_ts = [_dot(jnp.where((lane // CHUNK) == c, v_t, 0.0), kdb) for c in range(n_chunks)]
    yield
    order = range(n_chunks - 1, -1, -1) if reverse else range(n_chunks)
    for c in order:
        rows = slice(c * CHUNK, (c + 1) * CHUNK)
        st = st_ref[...]
        o_ref[rows, cols] = o_intra[rows] + _dot_nt(qd[rows], st)
        st_ref[...] = st * jnp.exp(cum[chunk_last[c]:chunk_last[c] + 1, :]) + n_ts[c]
        yield


def _hgrn_body(qf_ref, ff_ref, vf_ref, qb_ref, fb_ref, vb_ref, lb_ref, of_ref, ob_ref, sf_ref, sb_ref, stage_ref,
               diag_ref, *, key_dim):
    heads_per_step = sf_ref.shape[0]

    @pl.when(pl.program_id(2) == 0)
    def _():
        sf_ref[...] = jnp.zeros_like(sf_ref)
        sb_ref[...] = jnp.zeros_like(sb_ref)

    chains = []
    for g in range(heads_per_step):
        cols = slice(g * key_dim, (g + 1) * key_dim)
        lb = lb_ref[:, cols]
        chains.append(_hgrn_chain(qf_ref, ff_ref, vf_ref, cols, lb, sf_ref.at[g], of_ref,
                                  stage_ref.at[2 * g], diag_ref.at[2 * g], False))
        chains.append(_hgrn_chain(qb_ref, fb_ref, vb_ref, cols, lb, sb_ref.at[g], ob_ref,
                                  stage_ref.at[2 * g + 1], diag_ref.at[2 * g + 1], True))
    _run_interleaved(chains)


def hgrn_scan(proj_hg, lb, geom, ctx_len, n_heads, key_dim, heads_per_step=HGRN_HEADS_PER_STEP):
    n_lat, lat_len, n_batch = geom
    t = proj_hg.shape[0]
    tl = SCAN_TILE
    nl, nc, ctx0 = lat_len // tl, ctx_len // tl, n_lat // tl
    n_hg = n_heads // heads_per_step
    gw = heads_per_step * key_dim
    tile = functools.partial(_segment_tile, nl=nl, nc=nc, ctx_tile0=ctx0)
    specs = []
    for reverse in (False, True):
        tix = functools.partial(tile, reverse=reverse)
        fcol = (1 + int(reverse)) * n_hg
        specs += [pl.BlockSpec((tl, gw), lambda b, h, s, tix=tix: (tix(b, s), h)),
                  pl.BlockSpec((tl, gw), lambda b, h, s, tix=tix, fcol=fcol: (tix(b, s), fcol + h)),
                  pl.BlockSpec((tl, gw), lambda b, h, s, tix=tix: (tix(b, s), 3 * n_hg + h))]
    specs.append(pl.BlockSpec((1, gw), lambda b, h, s: (0, h)))
    out_specs = [pl.BlockSpec((tl, gw), lambda b, h, s, tix=functools.partial(tile, reverse=r): (tix(b, s), h))
                 for r in (False, True)]
    return pl.pallas_call(
        functools.partial(_hgrn_body, key_dim=key_dim), grid=(n_batch, n_hg, nl + nc),
        in_specs=specs, out_specs=out_specs,
        out_shape=[jax.ShapeDtypeStruct((t, n_heads * key_dim), F32)] * 2,
        scratch_shapes=[pltpu.VMEM((heads_per_step, key_dim, key_dim), F32)] * 2
        + [pltpu.VMEM((2 * heads_per_step, 3, tl, key_dim), F32), pltpu.VMEM((2 * heads_per_step, tl, key_dim), F32)],
        compiler_params=_params("arbitrary", "arbitrary", "arbitrary"), name="hgrn_scan",
    )(proj_hg, proj_hg, proj_hg, proj_hg, proj_hg, proj_hg, lb)


def _mix_out2_body(df_ref, db_ref, hf_ref, hb_ref, z_ref, og_ref, dnw_ref, hgw_ref, o_ref, *, head_dim):
    def normed(o, nw):
        outs = []
        for h in range(o.shape[1] // head_dim):
            oh = o[:, h * head_dim:(h + 1) * head_dim]
            outs.append(oh * lax.rsqrt(jnp.mean(oh * oh, axis=-1, keepdims=True) + NORM_EPS) * nw)
        return jnp.concatenate(outs, axis=1)

    dn = normed(df_ref[...] + db_ref[...], dnw_ref[...]) * _silu(z_ref[...])
    hg = normed(hf_ref[...] + hb_ref[...], hgw_ref[...]) * _sigmoid(og_ref[...])
    half = dn.shape[1]
    o_ref[:, :half] = dn.astype(o_ref.dtype)
    o_ref[:, half:] = hg.astype(o_ref.dtype)


def mix_out(dn_f, dn_b, hg_f, hg_b, z_src, z_blk, og_src, og_blk, dn_norm, hg_norm, head_dim):
    t, w = dn_f.shape
    tl = SCAN_TILE
    row = lambda i: (i, 0)
    body = functools.partial(_mix_out2_body, head_dim=head_dim)
    return pl.pallas_call(
        body, grid=(t // tl,),
        in_specs=[pl.BlockSpec((tl, w), row)] * 4
        + [pl.BlockSpec((tl, w), lambda i: (i, z_blk)), pl.BlockSpec((tl, w), lambda i: (i, og_blk)),
           pl.BlockSpec((1, head_dim), lambda i: (0, 0)), pl.BlockSpec((1, head_dim), lambda i: (0, 0))],
        out_specs=pl.BlockSpec((tl, 2 * w), row),
        out_shape=jax.ShapeDtypeStruct((t, 2 * w), BF16),
        compiler_params=_params("arbitrary"), name="mix_out",
    )(dn_f, dn_b, hg_f, hg_b, z_src, og_src, dn_norm.reshape(1, head_dim), hg_norm.reshape(1, head_dim))


def _first_max(x, ids, n):
    m = jnp.max(x, axis=0, keepdims=True)
    first = jnp.min(jnp.where(x == m, ids, n), axis=0, keepdims=True)
    return m, first


def _route_body(lg_ref, bias_ref, idx_ref, rank_ref, w_ref, cnt_ref, carry_ref, *,
                n_groups, topk_groups, top_k, scale):
    i = pl.program_id(0)

    @pl.when(i == 0)
    def _():
        carry_ref[...] = jnp.zeros_like(carry_ref)

    lg = lg_ref[...]
    n_exp, tl = lg.shape
    per = n_exp // n_groups
    scores = 1.0 / (1.0 + jnp.exp(-lg))
    biased = scores + bias_ref[...]
    sub = lax.broadcasted_iota(jnp.int32, (per, tl), 0)
    g_rows = []
    for g in range(n_groups):
        xg = biased[g * per:(g + 1) * per]
        m1, i1 = _first_max(xg, sub, per)
        m2 = jnp.max(jnp.where(sub == i1, NEG_INF, xg), axis=0, keepdims=True)
        g_rows.append(m1 + m2)
    gscore = jnp.concatenate(g_rows, axis=0)
    gid = lax.broadcasted_iota(jnp.int32, (n_groups, tl), 0)
    gsel = jnp.zeros((n_groups, tl), jnp.bool_)
    for _ in range(topk_groups):
        _, first = _first_max(gscore, gid, n_groups)
        hit = gid == first
        gsel = gsel | hit
        gscore = jnp.where(hit, NEG_INF, gscore)
    eid = lax.broadcasted_iota(jnp.int32, (n_exp, tl), 0)
    gmask = jnp.concatenate([jnp.broadcast_to(gsel[g:g + 1], (per, tl)) for g in range(n_groups)], axis=0)
    masked = jnp.where(gmask, biased, NEG_INF)
    sel = jnp.zeros((n_exp, tl), jnp.bool_)
    hits, firsts = [], []
    for _ in range(top_k):
        _, first = _first_max(masked, eid, n_exp)
        hit = eid == first
        hits.append(hit)
        firsts.append(first)
        sel = sel | hit
        masked = jnp.where(hit, NEG_INF, masked)
    self = jnp.where(sel, 1.0, 0.0)
    ti = lax.broadcasted_iota(jnp.int32, (tl, tl), 0)
    tj = lax.broadcasted_iota(jnp.int32, (tl, tl), 1)
    before = jnp.where(ti < tj, 1.0, 0.0).astype(BF16)
    carry = carry_ref[...]
    rank_full = jnp.dot(self.astype(BF16), before, preferred_element_type=F32) + carry
    w_rows = [jnp.sum(jnp.where(hit, scores, 0.0), axis=0, keepdims=True) for hit in hits]
    r_rows = [jnp.sum(jnp.where(hit, rank_full, 0.0), axis=0, keepdims=True) for hit in hits]
    w8 = jnp.concatenate(w_rows, axis=0)
    idx_ref[...] = jnp.concatenate(firsts, axis=0)
    rank_ref[...] = jnp.concatenate(r_rows, axis=0).astype(jnp.int32)
    w_ref[...] = w8 / jnp.sum(w8, axis=0, keepdims=True) * scale
    carry = carry + jnp.sum(self, axis=1, keepdims=True)
    carry_ref[...] = carry
    cnt_ref[...] = carry.astype(jnp.int32)


def route(logits_t, router_b, tl=256):
    n_exp, t = logits_t.shape
    body = functools.partial(_route_body, n_groups=N_GROUPS, topk_groups=TOPK_GROUPS, top_k=TOP_K,
                             scale=ROUTED_SCALE)
    tok_spec = pl.BlockSpec((TOP_K, tl), lambda i: (0, i))
    return pl.pallas_call(
        body, grid=(t // tl,),
        in_specs=[pl.BlockSpec((n_exp, tl), lambda i: (0, i)), pl.BlockSpec((n_exp, 1), lambda i: (0, 0))],
        out_specs=[tok_spec, tok_spec, tok_spec, pl.BlockSpec((n_exp, 1), lambda i: (0, 0))],
        out_shape=[jax.ShapeDtypeStruct((TOP_K, t), jnp.int32), jax.ShapeDtypeStruct((TOP_K, t), jnp.int32),
                   jax.ShapeDtypeStruct((TOP_K, t), F32), jax.ShapeDtypeStruct((n_exp, 1), jnp.int32)],
        scratch_shapes=[pltpu.VMEM((n_exp, 1), F32)],
        compiler_params=_params("arbitrary"), name="route",
    )(logits_t, router_b.reshape(n_exp, 1))


def _slab_copy(src_ref, src_tok, dst_ref, dst_tok, rows, sem):
    return pltpu.make_async_copy(src_ref.at[pl.ds(pl.multiple_of(src_tok * rows, rows), rows)],
                                 dst_ref.at[pl.ds(pl.multiple_of(dst_tok * rows, rows), rows)], sem)


def _dispatch_body(dest_ref, pad_ref, h_ref, xs_ref, zero_ref, sem, *, top_k, rows):
    tl = h_ref.shape[0] // rows
    n_exp = pad_ref.shape[1]

    def issue(r, carry):
        for k in range(top_k):
            _slab_copy(h_ref, r, xs_ref, dest_ref[r * top_k + k], rows, sem.at[0]).start(priority=k % 2)
        return carry

    lax.fori_loop(0, tl, issue, 0, unroll=DMA_LOOP_UNROLL)

    @pl.when(pl.program_id(0) == 0)
    def _():
        zero_ref[...] = jnp.zeros_like(zero_ref)

        def fill(e, total):
            first, n_pad = pad_ref[0, e], pad_ref[1, e]

            def one(j, carry):
                _slab_copy(zero_ref, 0, xs_ref, first + j, rows, sem.at[1]).start()
                return carry

            lax.fori_loop(0, n_pad, one, 0)
            return total + n_pad

        total = lax.fori_loop(0, n_exp, fill, 0)

        def drain_pad(j, carry):
            _slab_copy(zero_ref, 0, xs_ref, 0, rows, sem.at[1]).wait()
            return carry

        lax.fori_loop(0, total, drain_pad, 0)

    def drain(r, carry):
        for k in range(top_k):
            _slab_copy(h_ref, 0, xs_ref, 0, rows, sem.at[0]).wait()
        return carry

    lax.fori_loop(0, tl, drain, 0, unroll=DMA_LOOP_UNROLL)


def dispatch(h_slab, dest_flat, pad_info, n_slot, top_k, rows, tl=256):
    t = h_slab.shape[0] // rows
    return pl.pallas_call(
        functools.partial(_dispatch_body, top_k=top_k, rows=rows), grid=(t // tl,),
        in_specs=[pl.BlockSpec((tl * top_k,), lambda i: (i,), memory_space=pltpu.SMEM),
                  pl.BlockSpec(memory_space=pltpu.SMEM),
                  pl.BlockSpec((tl * rows, LANES), lambda i: (i, 0))],
        out_specs=pl.BlockSpec(memory_space=pl.ANY),
        out_shape=jax.ShapeDtypeStruct((n_slot * rows, LANES), h_slab.dtype),
        scratch_shapes=[pltpu.VMEM((rows, LANES), h_slab.dtype), pltpu.SemaphoreType.DMA((2,))],
        compiler_params=_params("arbitrary"), name="dispatch",
    )(dest_flat, pad_info, h_slab)


def _combine_body(dest_ref, dest_next_ref, w_ref, sh_ref, x_ref, gate_ref, y_ref, o_ref, buf_ref, acc_ref, sem, *,
                  top_k, rows):
    i = pl.program_id(0)
    n = pl.num_programs(0)
    tl = x_ref.shape[0]

    def gather(d_ref, slot):
        def issue(r, carry):
            for k in range(top_k):
                _slab_copy(y_ref, d_ref[r * top_k + k], buf_ref.at[slot], k * tl + r, rows,
                           sem.at[slot]).start(priority=k % 2)
            return carry
        lax.fori_loop(0, tl, issue, 0, unroll=DMA_LOOP_UNROLL)

    @pl.when(i == 0)
    def _():
        gather(dest_ref, 0)

    @pl.when(i + 1 < n)
    def _():
        gather(dest_next_ref, (i + 1) % 2)

    slot = i % 2

    def drain(r, carry):
        for k in range(top_k):
            _slab_copy(y_ref, 0, buf_ref.at[slot], 0, rows, sem.at[slot]).wait()
        return carry

    lax.fori_loop(0, tl, drain, 0, unroll=DMA_LOOP_UNROLL)
    w = w_ref[...]
    acc = sh_ref[...]
    for k in range(top_k):
        wk = jnp.broadcast_to(w[:, k:k + 1], (tl, LANES))
        wk = jnp.broadcast_to(wk[:, None, :], (tl, rows, LANES)).reshape(tl * rows, LANES)
        acc = acc + buf_ref[slot, pl.ds(k * tl * rows, tl * rows), :] * wk
    acc_ref[...] = acc
    for c in range(rows):
        cols = slice(c * LANES, (c + 1) * LANES)
        o_ref[:, cols] = x_ref[:, cols] + gate_ref[0][:, cols] * acc_ref[pl.ds(c, tl, stride=rows), :]


def combine(y_slab, dest_flat, w, shared_slab, x, mod3, gate_idx, geom, tl=128):
    t, d = x.shape
    rows = d // LANES
    top_k = w.shape[1]
    n_lat, lat_len, n_batch = geom
    n_tiles = t // tl
    row = functools.partial(_mod_row, tile_rows=tl, n_lat=n_lat, lat_len=lat_len, n_batch=n_batch)
    return pl.pallas_call(
        functools.partial(_combine_body, top_k=top_k, rows=rows), grid=(n_tiles,),
        in_specs=[pl.BlockSpec((tl * top_k,), lambda i: (i,), memory_space=pltpu.SMEM),
                  pl.BlockSpec((tl * top_k,), lambda i: (jnp.minimum(i + 1, n_tiles - 1),),
                               memory_space=pltpu.SMEM),
                  pl.BlockSpec((tl, top_k), lambda i: (i, 0)),
                  pl.BlockSpec((tl * rows, LANES), lambda i: (i, 0)),
                  pl.BlockSpec((tl, d), lambda i: (i, 0)),
                  pl.BlockSpec((1, 1, d), lambda i: (row(i) * 6 + gate_idx, 0, 0)),
                  pl.BlockSpec(memory_space=pl.ANY)],
        out_specs=pl.BlockSpec((tl, d), lambda i: (i, 0)),
        out_shape=jax.ShapeDtypeStruct((t, d), F32),
        scratch_shapes=[pltpu.VMEM((2, top_k * tl * rows, LANES), F32), pltpu.VMEM((tl * rows, LANES), F32),
                        pltpu.SemaphoreType.DMA((2,))],
        compiler_params=_params("arbitrary"), name="combine",
    )(dest_flat, dest_flat, w, shared_slab, x, mod3, y_slab)


def moe_ffn(x, h_pk, logits_t, router_b, exp_gate_up, exp_down, shared_gate_up, shared_down, mod3, geom):
    t = x.shape[0]
    slab_rows = h_pk.shape[0] // t
    e_count = exp_gate_up.shape[0]
    blk = EXPERT_SLOT_BLOCK
    idx8, rank8, w8, counts = route(logits_t, router_b)
    counts = counts[:, 0]
    padded = (counts + blk - 1) // blk * blk
    pad_end = jnp.cumsum(padded)
    pad_start = pad_end - padded
    start8 = jnp.sum(jnp.where(idx8[None] == jnp.arange(e_count, dtype=jnp.int32)[:, None, None],
                               pad_start[:, None, None], 0), axis=0)
    dest_flat = (start8 + rank8).T.reshape(-1)
    n_blk = (t * TOP_K + e_count * (blk - 1)) // blk + 1
    n_slot = n_blk * blk
    blk_starts = jnp.arange(n_blk, dtype=jnp.int32) * blk
    blk_e = jnp.sum((pad_end[None, :] <= blk_starts[:, None]).astype(jnp.int32), axis=1)
    blk_e = jnp.minimum(blk_e, e_count - 1)
    blk_new = jnp.concatenate([jnp.ones((1,), jnp.int32), (blk_e[1:] != blk_e[:-1]).astype(jnp.int32)])
    n_used = (pad_end[-1] // blk).astype(jnp.int32).reshape(1)
    pad_info = jnp.stack([pad_start + counts, padded - counts]).astype(jnp.int32)
    x_sorted = dispatch(h_pk, dest_flat, pad_info, n_slot, TOP_K, slab_rows)
    y_slot = expert_blocks(x_sorted, exp_gate_up, exp_down, blk_e, blk_new, n_used, blk)
    one = jnp.ones((t // blk,), jnp.int32)
    shared = expert_blocks(h_pk, shared_gate_up[None], shared_down[None],
                           jnp.zeros((t // blk,), jnp.int32), one.at[1:].set(0),
                           jnp.full((1,), t // blk, jnp.int32), blk)
    return combine(y_slot, dest_flat, w8.T, shared, x, mod3, 5, geom)


def kernel(x, c, ctx, c_ctx, hg_lb_logits, l0_mod_w, l0_mod_b, l0_norm1, l0_norm2, l0_w_in, l0_dn_conv, l0_dn_a_log, l0_dn_dt_bias, l0_dn_norm, l0_hg_norm, l0_w_out, l0_router_w, l0_router_b, l0_exp_gate_up, l0_exp_down, l0_shared_gate_up, l0_shared_down, l1_mod_w, l1_mod_b, l1_norm1, l1_norm2, l1_w_in, l1_q_norm, l1_k_norm, l1_lambda, l1_sub_norm, l1_w_out, l1_router_w, l1_router_b, l1_exp_gate_up, l1_exp_down, l1_shared_gate_up, l1_shared_down):
    n_batch, lat_len, d = x.shape
    ctx_len = ctx.shape[1]
    n_lat = n_batch * lat_len
    n_ctx = n_batch * ctx_len
    geom = (n_lat, lat_len, n_batch)
    geom_lat_only = (n_lat, lat_len, n_batch)

    xs = jnp.concatenate([x.reshape(n_lat, d), ctx.reshape(n_ctx, d)], axis=0)
    cond = jnp.concatenate([c, c_ctx[None], jnp.zeros((MOD_ROWS - n_batch - 1, d), F32)], axis=0)

    mod3 = modulation(cond, l0_mod_w, l0_mod_b).reshape(MOD_ROWS * 6, 1, d)
    h = adaln(xs, l0_norm1, mod3, 0, geom)
    n_dn = IN0_SIZES[0] + IN0_SIZES[1]
    n_gate = IN0_SIZES[2] + IN0_SIZES[3]
    proj_dn = matmul(h, l0_w_in, n_cols=n_dn)
    proj_hg = matmul(h, l0_w_in[:, n_dn + n_gate:])
    gc, gr = gate_prep(h, l0_w_in[:, n_dn:n_dn + n_gate], l0_dn_a_log, l0_dn_dt_bias, DN_HEADS)
    conv_t = jnp.concatenate([l0_dn_conv.T, jnp.zeros((8 - l0_dn_conv.shape[1], 3 * DN_WIDTH), F32)], axis=0)
    qkv = dn_prep(proj_dn, conv_t, geom, ctx_len, DN_WIDTH, DN_HEAD_DIM)
    dn_f, dn_b = delta_scan(qkv, gc, gr, geom, ctx_len, DN_HEADS, DN_HEAD_DIM, inv_passes=DELTA_INV_PASSES)
    lb = jnp.cumsum(jax.nn.softmax(hg_lb_logits, axis=0), axis=0)[0:1]
    hg_f, hg_b = hgrn_scan(proj_hg, lb, geom, ctx_len, HG_HEADS, HG_KEY_DIM)
    y = mix_out(dn_f, dn_b, hg_f, hg_b, proj_dn, 3, proj_hg, 4, l0_dn_norm, l0_hg_norm, DN_HEAD_DIM)
    xs = matmul_residual(y, l0_w_out, xs, mod3, 2, geom)
    h, logits_t = adaln(xs, l0_norm2, mod3, 3, geom, router_w=l0_router_w)
    xs = moe_ffn(xs, h, logits_t, l0_router_b, l0_exp_gate_up, l0_exp_down, l0_shared_gate_up, l0_shared_down,
                 mod3, geom)

    mod3 = modulation(cond, l1_mod_w, l1_mod_b).reshape(MOD_ROWS * 6, 1, d)
    h = adaln(xs, l1_norm1, mod3, 0, geom)
    proj = matmul(h, l1_w_in)
    cos_t, sin_t = rope_tables(lat_len, n_batch, n_ctx, DA_HEAD_DIM)
    qkv = qk_prep(proj, l1_q_norm, l1_k_norm, cos_t, sin_t)
    lam_init = 0.8 - 0.6 * math.exp(-0.3 * 1)
    lmbda = (jnp.exp(jnp.sum(l1_lambda[0] * l1_lambda[1])) - jnp.exp(jnp.sum(l1_lambda[2] * l1_lambda[3]))
             + lam_init)
    y = diff_attention(qkv, lmbda, l1_sub_norm, n_batch, lat_len, ctx_len, DA_HEADS, DA_HEAD_DIM,
                       1.0 - lam_init)
    xl = matmul_residual(y, l1_w_out, xs[:n_lat], mod3, 2, geom_lat_only)
    h, logits_t = adaln(xl, l1_norm2, mod3, 3, geom_lat_only, router_w=l1_router_w)
    xl = moe_ffn(xl, h, logits_t, l1_router_b, l1_exp_gate_up, l1_exp_down, l1_shared_gate_up, l1_shared_down,
                 mod3, geom_lat_only)
    return xl.reshape(n_batch, lat_len, d)
```

```python
import functools
import math

import jax
import jax.numpy as jnp
import numpy as np
from jax import lax
from jax.experimental import pallas as pl
from jax.experimental.pallas import tpu as pltpu

F32 = jnp.float32
BF16 = jnp.bfloat16

NORM_EPS = 1e-6
GRID_W = 64
ROPE_BASE = 10000.0

DN_HEADS = 8
DN_HEAD_DIM = 128
DN_WIDTH = DN_HEADS * DN_HEAD_DIM
DN_CHUNK = 64
HG_HEADS = 8
HG_KEY_DIM = 128
HG_VAL_DIM = 128
HG_K_WIDTH = HG_HEADS * HG_KEY_DIM
HG_V_WIDTH = HG_HEADS * HG_VAL_DIM
HG_CHUNK = 64
IN0_SIZES = (3 * DN_WIDTH, DN_WIDTH, 2 * DN_HEADS, 2 * DN_HEADS,
             HG_K_WIDTH, 2 * HG_K_WIDTH, HG_V_WIDTH, HG_V_WIDTH)
DA_HEADS = 8
DA_HEAD_DIM = 128
N_EXPERTS = 64
TOP_K = 8
N_GROUPS = 8
TOPK_GROUPS = 4
EXPERT_FF = 512
ROUTED_SCALE = 2.5

VMEM_LIMIT_BYTES = 56 * 1024 * 1024
MOD_ROWS = 8
EXPERT_SLOT_BLOCK = 256
DMA_LOOP_UNROLL = 8
MM_TILE_M = 1024
MM_TILE_N = 1024
LANES = 128
ATTN_TILE_Q = 512
ATTN_ROW_CHUNK = 128


def _params(*sem):
    return pltpu.CompilerParams(dimension_semantics=sem, vmem_limit_bytes=VMEM_LIMIT_BYTES)


def _silu(x):
    return x * (1.0 / (1.0 + jnp.exp(-x)))


def _mod_body(c_ref, w_ref, b_ref, o_ref):
    a = _silu(c_ref[...]).astype(BF16)
    o_ref[...] = jnp.dot(a, w_ref[...].astype(BF16), preferred_element_type=F32) + b_ref[...]


def modulation(cond, w, b, tn=1024):
    m, k = cond.shape
    n = w.shape[1]
    return pl.pallas_call(
        _mod_body, grid=(n // tn,),
        in_specs=[pl.BlockSpec((m, k), lambda j: (0, 0)),
                  pl.BlockSpec((k, tn), lambda j: (0, j)),
                  pl.BlockSpec((1, tn), lambda j: (0, j))],
        out_specs=pl.BlockSpec((m, tn), lambda j: (0, j)),
        out_shape=jax.ShapeDtypeStruct((m, n), F32),
        compiler_params=_params("arbitrary"), name="modulation",
    )(cond, w, b.reshape(1, n))


def _mod_row(tile, tile_rows, n_lat, lat_len, n_batch):
    start = tile * tile_rows
    return jnp.where(start < n_lat, start // lat_len, n_batch)


def _pack_bf16_pairs(h):
    half = h.shape[1] // 2
    bits = lax.bitcast_convert_type(h.astype(BF16).astype(F32), jnp.uint32)
    return (bits[:, :half] & jnp.uint32(0xFFFF0000)) | (bits[:, half:] >> jnp.uint32(16))


def _unpack_bf16_pairs(pk):
    hi = lax.bitcast_convert_type(pk & jnp.uint32(0xFFFF0000), F32).astype(BF16)
    lo = lax.bitcast_convert_type(pk << jnp.uint32(16), F32).astype(BF16)
    return hi, lo


def _store_slabs(ref, x):
    r, w = x.shape
    c_n = w // LANES
    for c in range(c_n):
        ref[pl.ds(c, r, stride=c_n), :] = x[:, c * LANES:(c + 1) * LANES]


def _load_slabs(ref, row0, r, c_n):
    return jnp.concatenate([ref[pl.ds(row0 * c_n + c, r, stride=c_n), :] for c in range(c_n)], axis=1)


def _adaln_body(x_ref, nw_ref, shift_ref, scale_ref, o_ref):
    x = x_ref[...]
    y = x * lax.rsqrt(jnp.mean(x * x, axis=-1, keepdims=True) + NORM_EPS) * nw_ref[...]
    o_ref[...] = (y * (1.0 + scale_ref[0]) + shift_ref[0]).astype(o_ref.dtype)


def _adaln_router_body(x_ref, nw_ref, shift_ref, scale_ref, rw_ref, o_ref, lg_ref):
    x = x_ref[...]
    y = x * lax.rsqrt(jnp.mean(x * x, axis=-1, keepdims=True) + NORM_EPS) * nw_ref[...]
    h = y * (1.0 + scale_ref[0]) + shift_ref[0]
    _store_slabs(o_ref, _pack_bf16_pairs(h))
    lg_ref[...] = lax.dot_general(rw_ref[...], h, (((1,), (1,)), ((), ())), preferred_element_type=F32,
                                  precision=lax.Precision.HIGHEST)


def adaln(x, norm_w, mod3, shift_idx, geom, router_w=None, tl=256):
    t, d = x.shape
    n_lat, lat_len, n_batch = geom
    row = functools.partial(_mod_row, tile_rows=tl, n_lat=n_lat, lat_len=lat_len, n_batch=n_batch)
    in_specs = [pl.BlockSpec((tl, d), lambda i: (i, 0)),
                pl.BlockSpec((1, d), lambda i: (0, 0)),
                pl.BlockSpec((1, 1, d), lambda i: (row(i) * 6 + shift_idx, 0, 0)),
                pl.BlockSpec((1, 1, d), lambda i: (row(i) * 6 + shift_idx + 1, 0, 0))]
    args = [x, norm_w.reshape(1, d), mod3, mod3]
    if router_w is None:
        return pl.pallas_call(
            _adaln_body, grid=(t // tl,), in_specs=in_specs,
            out_specs=pl.BlockSpec((tl, d), lambda i: (i, 0)),
            out_shape=jax.ShapeDtypeStruct((t, d), BF16),
            compiler_params=_params("arbitrary"), name="adaln")(*args)
    e = router_w.shape[1]
    return pl.pallas_call(
        _adaln_router_body, grid=(t // tl,),
        in_specs=in_specs + [pl.BlockSpec((e, d), lambda i: (0, 0))],
        out_specs=[pl.BlockSpec((tl * (d // 2 // LANES), LANES), lambda i: (i, 0)),
                   pl.BlockSpec((e, tl), lambda i: (0, i))],
        out_shape=[jax.ShapeDtypeStruct((t * (d // 2 // LANES), LANES), jnp.uint32),
                   jax.ShapeDtypeStruct((e, t), F32)],
        compiler_params=_params("arbitrary"), name="adaln_router")(*args, router_w.T)


def _mm_body(a_ref, w_ref, o_ref, wb_ref):
    @pl.when(pl.program_id(1) == 0)
    def _():
        wb_ref[...] = w_ref[...].astype(BF16)
    o_ref[...] = jnp.dot(a_ref[...], wb_ref[...], preferred_element_type=F32).astype(o_ref.dtype)


def _mm_res_body(a_ref, w_ref, res_ref, gate_ref, o_ref, wb_ref):
    @pl.when(pl.program_id(1) == 0)
    def _():
        wb_ref[...] = w_ref[...].astype(BF16)
    acc = jnp.dot(a_ref[...], wb_ref[...], preferred_element_type=F32)
    o_ref[...] = res_ref[...] + gate_ref[0] * acc


def matmul(a, w, n_cols=None, col_block0=0, tm=MM_TILE_M, tn=MM_TILE_N, out_dtype=F32):
    m, k = a.shape
    n = w.shape[1] if n_cols is None else n_cols
    return pl.pallas_call(
        _mm_body, grid=(n // tn, m // tm),
        in_specs=[pl.BlockSpec((tm, k), lambda j, i: (i, 0)),
                  pl.BlockSpec((k, tn), lambda j, i: (0, j + col_block0))],
        out_specs=pl.BlockSpec((tm, tn), lambda j, i: (i, j)),
        out_shape=jax.ShapeDtypeStruct((m, n), out_dtype),
        scratch_shapes=[pltpu.VMEM((k, tn), BF16)],
        compiler_params=_params("arbitrary", "arbitrary"), name="matmul",
    )(a, w)


def matmul_residual(a, w, res, mod3, gate_idx, geom, tm=MM_TILE_M, tn=MM_TILE_N):
    m, k = a.shape
    n = w.shape[1]
    n_lat, lat_len, n_batch = geom
    row = functools.partial(_mod_row, tile_rows=tm, n_lat=n_lat, lat_len=lat_len, n_batch=n_batch)
    nb = n // tn
    return pl.pallas_call(
        _mm_res_body, grid=(n // tn, m // tm),
        in_specs=[pl.BlockSpec((tm, k), lambda j, i: (i, 0)),
                  pl.BlockSpec((k, tn), lambda j, i: (0, j)),
                  pl.BlockSpec((tm, tn), lambda j, i: (i, j)),
                  pl.BlockSpec((1, 1, tn), lambda j, i: (row(i) * 6 + gate_idx, 0, j))],
        out_specs=pl.BlockSpec((tm, tn), lambda j, i: (i, j)),
        out_shape=jax.ShapeDtypeStruct((m, n), F32),
        scratch_shapes=[pltpu.VMEM((k, tn), BF16)],
        compiler_params=_params("arbitrary", "arbitrary"), name="matmul_residual",
    )(a, w, res, mod3)


def _expert_body(blk_e_ref, blk_new_ref, next_e_ref, stage_ref, n_used_ref, x_ref, gu_hbm, dn_hbm, o_ref,
                 gu_stage, dn_stage, gub_ref, dnb_ref, sem):
    i = pl.program_id(0)

    def weight_copies(e, slot):
        return (pltpu.make_async_copy(gu_hbm.at[e], gu_stage.at[slot], sem.at[0, slot]),
                pltpu.make_async_copy(dn_hbm.at[e], dn_stage.at[slot], sem.at[1, slot]))

    @pl.when(i == 0)
    def _():
        for cp in weight_copies(blk_e_ref[0], 0):
            cp.start()

    @pl.when(blk_new_ref[i] == 1)
    def _():
        slot = stage_ref[i]
        for cp in weight_copies(blk_e_ref[i], slot):
            cp.wait()
        gub_ref[...] = gu_stage[slot].astype(BF16)
        dnb_ref[...] = dn_stage[slot].astype(BF16)

        @pl.when(next_e_ref[i] >= 0)
        def _():
            for cp in weight_copies(next_e_ref[i], 1 - slot):
                cp.start()

    @pl.when(i < n_used_ref[0])
    def _():
        ff, d = dnb_ref.shape
        half = d // 2
        blk = x_ref.shape[0] // (half // LANES)
        x_hi, x_lo = _unpack_bf16_pairs(_load_slabs(x_ref, 0, blk, half // LANES))
        h1 = (jnp.dot(x_hi, gub_ref[:half], preferred_element_type=F32)
              + jnp.dot(x_lo, gub_ref[half:], preferred_element_type=F32))
        act = (_silu(h1[:, :ff]) * h1[:, ff:]).astype(BF16)
        _store_slabs(o_ref, jnp.dot(act, dnb_ref[...], preferred_element_type=F32))

    @pl.when(i >= n_used_ref[0])
    def _():
        o_ref[...] = jnp.zeros_like(o_ref)


def expert_blocks(x, gate_up, down, blk_e, n_used, blk):
    _, d, f2 = gate_up.shape
    xc, yc = d // 2 // LANES, d // LANES
    s = x.shape[0] // xc
    n_blk = s // blk
    pos = jnp.arange(n_blk, dtype=jnp.int32)
    used = pos < n_used[0]
    blk_new = (jnp.concatenate([jnp.ones((1,), bool), blk_e[1:] != blk_e[:-1]]) & used).astype(jnp.int32)
    stage = (jnp.cumsum(blk_new) - 1) % 2
    first_pos = jnp.where(blk_new == 1, pos, n_blk)
    next_first = jnp.concatenate([lax.cummin(first_pos, reverse=True)[1:], jnp.full((1,), n_blk, jnp.int32)])
    next_e = jnp.where(next_first < n_blk, blk_e[jnp.minimum(next_first, n_blk - 1)], -1).astype(jnp.int32)
    grid_spec = pltpu.PrefetchScalarGridSpec(
        num_scalar_prefetch=5, grid=(n_blk,),
        in_specs=[pl.BlockSpec((blk * xc, LANES), lambda i, be, bn, ne, st, nu: (jnp.minimum(i, nu[0] - 1), 0)),
                  pl.BlockSpec(memory_space=pl.ANY),
                  pl.BlockSpec(memory_space=pl.ANY)],
        out_specs=pl.BlockSpec((blk * yc, LANES), lambda i, be, bn, ne, st, nu: (i, 0)),
        scratch_shapes=[pltpu.VMEM((2, d, f2), F32), pltpu.VMEM((2, f2 // 2, d), F32),
                        pltpu.VMEM((d, f2), BF16), pltpu.VMEM((f2 // 2, d), BF16),
                        pltpu.SemaphoreType.DMA((2, 2))])
    return pl.pallas_call(
        _expert_body, grid_spec=grid_spec,
        out_shape=jax.ShapeDtypeStruct((s * yc, LANES), F32),
        compiler_params=_params("arbitrary"), name="expert_blocks",
    )(blk_e, blk_new, next_e, stage.astype(jnp.int32), n_used, x, gate_up, down)


def _qk_prep_body(x_ref, qw_ref, kw_ref, cos_ref, sin_ref, o_ref, *, n_qk_groups, head_dim):
    cos = cos_ref[...]
    sin = sin_ref[...]
    lane = lax.broadcasted_iota(jnp.int32, cos.shape, 1)
    first = (lane % (head_dim // 2)) < (head_dim // 4)
    for g in range(n_qk_groups):
        sl = slice(g * head_dim, (g + 1) * head_dim)
        x = x_ref[:, sl]
        w = qw_ref[...] if g < n_qk_groups // 2 else kw_ref[...]
        y = x * lax.rsqrt(jnp.mean(x * x, axis=-1, keepdims=True) + NORM_EPS) * w
        swapped = jnp.where(first, pltpu.roll(y, head_dim - head_dim // 4, 1), pltpu.roll(y, head_dim // 4, 1))
        o_ref[:, sl] = (y * cos + swapped * sin).astype(o_ref.dtype)
    rest = n_qk_groups * head_dim
    o_ref[:, rest:] = x_ref[:, rest:].astype(o_ref.dtype)


def qk_prep(proj, q_norm, k_norm, cos_t, sin_t, tl=256):
    t, n = proj.shape
    hd = q_norm.shape[0]
    n_groups = (2 * n // 3) // hd
    body = functools.partial(_qk_prep_body, n_qk_groups=n_groups, head_dim=hd)
    return pl.pallas_call(
        body, grid=(t // tl,),
        in_specs=[pl.BlockSpec((tl, n), lambda i: (i, 0)),
                  pl.BlockSpec((1, hd), lambda i: (0, 0)),
                  pl.BlockSpec((1, hd), lambda i: (0, 0)),
                  pl.BlockSpec((tl, hd), lambda i: (i, 0)),
                  pl.BlockSpec((tl, hd), lambda i: (i, 0))],
        out_specs=pl.BlockSpec((tl, n), lambda i: (i, 0)),
        out_shape=jax.ShapeDtypeStruct((t, n), BF16),
        compiler_params=_params("arbitrary"), name="qk_prep",
    )(proj, q_norm.reshape(1, hd), k_norm.reshape(1, hd), cos_t, sin_t)


def rope_tables(n_lat_tokens_per_sample, n_batch, n_ctx_tokens, head_dim):
    quarter = head_dim // 4
    inv_freq = ROPE_BASE ** (-jnp.arange(quarter, dtype=F32) / quarter)
    rows = n_lat_tokens_per_sample // GRID_W
    row = jnp.repeat(jnp.arange(rows, dtype=F32), GRID_W)
    col = jnp.tile(jnp.arange(GRID_W, dtype=F32), rows)
    ang_r = row[:, None] * inv_freq[None, :]
    ang_c = col[:, None] * inv_freq[None, :]
    cos = jnp.concatenate([jnp.cos(ang_r), jnp.cos(ang_r), jnp.cos(ang_c), jnp.cos(ang_c)], axis=-1)
    sin = jnp.concatenate([-jnp.sin(ang_r), jnp.sin(ang_r), -jnp.sin(ang_c), jnp.sin(ang_c)], axis=-1)
    cos = jnp.concatenate([jnp.tile(cos, (n_batch, 1)), jnp.ones((n_ctx_tokens, head_dim), F32)], axis=0)
    sin = jnp.concatenate([jnp.tile(sin, (n_batch, 1)), jnp.zeros((n_ctx_tokens, head_dim), F32)], axis=0)
    return cos, sin


def _diff_attn_rows(lam, q_ref, k_all, v_all, sw_ref, o_ref, rows, head_dim, out_scale):
    c = head_dim ** -0.5 * math.log2(math.e)
    es, invs = [], []
    for s in range(2):
        sl = slice(s * head_dim, (s + 1) * head_dim)
        sc = lax.dot_general(q_ref[rows, sl], k_all[:, sl], (((1,), (1,)), ((), ())), preferred_element_type=F32)
        yield
        e = jnp.exp2((sc - jnp.max(sc, axis=-1, keepdims=True)) * c)
        invs.append(1.0 / jnp.sum(e, axis=-1, keepdims=True))
        es.append(e.astype(BF16))
        yield
    v = v_all[...]
    o0 = jnp.dot(es[0], v, preferred_element_type=F32)
    o1 = jnp.dot(es[1], v, preferred_element_type=F32)
    yield
    o = o0 * invs[0] - (lam * invs[1]) * o1
    y = o * lax.rsqrt(jnp.mean(o * o, axis=-1, keepdims=True) + NORM_EPS) * sw_ref[...]
    o_ref[rows, :] = (y * out_scale).astype(o_ref.dtype)
    yield


def _diff_attn_body(lam_ref, q_ref, kl_ref, kc_ref, vl_ref, vc_ref, sw_ref, o_ref, k_all, v_all, *,
                    head_dim, out_scale, row_chunk):
    @pl.when(pl.program_id(2) == 0)
    def _():
        n_l = kl_ref.shape[0]
        k_all[:n_l] = kl_ref[...]
        k_all[n_l:] = kc_ref[...]
        v_all[:n_l] = vl_ref[...]
        v_all[n_l:] = vc_ref[...]

    lam = lam_ref[0]
    tq = q_ref.shape[0]
    _run_interleaved(
        _diff_attn_rows(lam, q_ref, k_all, v_all, sw_ref, o_ref, slice(r, r + row_chunk), head_dim, out_scale)
        for r in range(0, tq, row_chunk))


def diff_attention(qkv, lmbda, sub_norm, n_batch, lat_len, ctx_len, n_heads, head_dim, out_scale,
                   tq=ATTN_TILE_Q, row_chunk=ATTN_ROW_CHUNK):
    hw = 2 * head_dim
    nq = lat_len // tq
    ctx_blk0 = n_batch * lat_len // ctx_len
    body = functools.partial(_diff_attn_body, head_dim=head_dim, out_scale=out_scale, row_chunk=row_chunk)
    n_keys = lat_len + ctx_len
    return pl.pallas_call(
        body, grid=(n_batch, n_heads, nq),
        in_specs=[pl.BlockSpec(memory_space=pltpu.SMEM),
                  pl.BlockSpec((tq, hw), lambda b, h, i: (b * nq + i, h)),
                  pl.BlockSpec((lat_len, hw), lambda b, h, i: (b, n_heads + h)),
                  pl.BlockSpec((ctx_len, hw), lambda b, h, i: (ctx_blk0 + b, n_heads + h)),
                  pl.BlockSpec((lat_len, hw), lambda b, h, i: (b, 2 * n_heads + h)),
                  pl.BlockSpec((ctx_len, hw), lambda b, h, i: (ctx_blk0 + b, 2 * n_heads + h)),
                  pl.BlockSpec((1, hw), lambda b, h, i: (0, 0))],
        out_specs=pl.BlockSpec((tq, hw), lambda b, h, i: (b * nq + i, h)),
        out_shape=jax.ShapeDtypeStruct((n_batch * lat_len, n_heads * hw), BF16),
        scratch_shapes=[pltpu.VMEM((n_keys, hw), BF16), pltpu.VMEM((n_keys, hw), BF16)],
        compiler_params=_params("arbitrary", "arbitrary", "arbitrary"), name="diff_attention",
    )(lmbda.reshape(1), qkv, qkv, qkv, qkv, qkv, sub_norm.reshape(1, hw))


SCAN_TILE = 256
CHUNK = 64
SUB = 16
NEG_BIG = -1e30
NEG_INF = float("-inf")
DELTA_INV_PASSES = 1
DELTA_HEADS_PER_STEP = 4
HGRN_HEADS_PER_STEP = 2


def _sigmoid(x):
    return 1.0 / (1.0 + jnp.exp(-x))


def _dot(a, b):
    return jnp.dot(a.astype(BF16), b.astype(BF16), preferred_element_type=F32)


def _dot_nt(a, b):
    return lax.dot_general(a.astype(BF16), b.astype(BF16), (((1,), (1,)), ((), ())),
                           preferred_element_type=F32)


def _split3(x):
    hi = x.astype(BF16)
    r = x - hi.astype(F32)
    mid = r.astype(BF16)
    lo = (r - mid.astype(F32)).astype(BF16)
    return hi, mid, lo


def _dot_exact_lhs01(m01, x):
    hi, mid, lo = _split3(x)
    m = m01.astype(BF16)
    return (jnp.dot(m, hi, preferred_element_type=F32) + jnp.dot(m, mid, preferred_element_type=F32)
            + jnp.dot(m, lo, preferred_element_type=F32))


def _dot_exact_rhs01(x, m01):
    hi, mid, lo = _split3(x)
    m = m01.astype(BF16)
    return (jnp.dot(hi, m, preferred_element_type=F32) + jnp.dot(mid, m, preferred_element_type=F32)
            + jnp.dot(lo, m, preferred_element_type=F32))


def _dot3(a, b):
    ah = a.astype(BF16)
    al = (a - ah.astype(F32)).astype(BF16)
    bh = b.astype(BF16)
    bl = (b - bh.astype(F32)).astype(BF16)
    return (jnp.dot(ah, bh, preferred_element_type=F32) + jnp.dot(ah, bl, preferred_element_type=F32)
            + jnp.dot(al, bh, preferred_element_type=F32))


def _tile_masks(n, reverse):
    i = lax.broadcasted_iota(jnp.int32, (n, n), 0)
    j = lax.broadcasted_iota(jnp.int32, (n, n), 1)
    same = (i // CHUNK) == (j // CHUNK)
    if reverse:
        return same & (i <= j), same & (i < j)
    return same & (i >= j), same & (i > j)


def _segment_tile(b, s, reverse, nl, nc, ctx_tile0):
    if reverse:
        return jnp.where(s < nc, ctx_tile0 + b * nc + (nc - 1 - s), b * nl + (nl - 1 - (s - nc)))
    return jnp.where(s < nc, ctx_tile0 + b * nc + s, b * nl + (s - nc))


def _dn_prep_body(x_ref, prev_ref, next_ref, w_ref, o_ref, *, tiles_per_lat_seg, tiles_per_ctx_seg, n_lat_tiles,
                  head_dim, q_scale):
    i = pl.program_id(0)
    j = pl.program_id(1)
    is_lat = i < n_lat_tiles
    pos = jnp.where(is_lat, i % tiles_per_lat_seg, (i - n_lat_tiles) % tiles_per_ctx_seg)
    seg_first = pos == 0
    seg_last = pos == jnp.where(is_lat, tiles_per_lat_seg, tiles_per_ctx_seg) - 1
    x = x_ref[...]
    tl = x.shape[0]
    prev = jnp.where(seg_first, 0.0, prev_ref[...])
    nxt = jnp.where(seg_last, 0.0, next_ref[...])
    xp = jnp.concatenate([prev, x, nxt], axis=0)
    w = w_ref[...]
    n_taps = 5
    acc = None
    for t in range(n_taps):
        off = 8 + t - n_taps // 2
        term = xp[off:off + tl] * w[t:t + 1]
        acc = term if acc is None else acc + term
    y = _silu(acc)
    scale = jnp.where(j == 0, q_scale, 1.0)
    outs = []
    for h in range(y.shape[1] // head_dim):
        yh = y[:, h * head_dim:(h + 1) * head_dim]
        nrm = lax.rsqrt(jnp.sum(yh * yh, axis=-1, keepdims=True) + 1e-6) * scale
        outs.append(yh * jnp.where(j == 2, 1.0, nrm))
    o_ref[...] = jnp.concatenate(outs, axis=1)


def dn_prep(proj_dn, conv_w_t, geom, ctx_len, width, head_dim):
    n_lat, lat_len, n_batch = geom
    t = proj_dn.shape[0]
    tl = SCAN_TILE
    rows8 = tl // 8
    n_tiles = t // tl
    body = functools.partial(_dn_prep_body, tiles_per_lat_seg=lat_len // tl, tiles_per_ctx_seg=ctx_len // tl,
                             n_lat_tiles=n_lat // tl,
                             head_dim=head_dim, q_scale=head_dim ** -0.5)
    last8 = t // 8 - 1
    return pl.pallas_call(
        body, grid=(n_tiles, 3),
        in_specs=[pl.BlockSpec((tl, width), lambda i, j: (i, j)),
                  pl.BlockSpec((8, width), lambda i, j: (jnp.maximum(i * rows8 - 1, 0), j)),
                  pl.BlockSpec((8, width), lambda i, j: (jnp.minimum((i + 1) * rows8, last8), j)),
                  pl.BlockSpec((8, width), lambda i, j: (0, j))],
        out_specs=pl.BlockSpec((tl, width), lambda i, j: (i, j)),
        out_shape=jax.ShapeDtypeStruct((t, 3 * width), F32),
        compiler_params=_params("arbitrary", "arbitrary"), name="dn_prep",
    )(proj_dn, proj_dn, proj_dn, conv_w_t)


def _softplus(x):
    return jnp.maximum(x, 0.0) + jnp.log(1.0 + jnp.exp(-jnp.abs(x)))


def _gate_prep_body(h_ref, wc_ref, wr_ref, alog_c_ref, dtb_c_ref, alog_r_ref, dtb_r_ref, gc_ref, gr_ref, *, n_heads):
    h = h_ref[...]
    tl = h.shape[0]
    nd = 2 * n_heads
    raw_c = jnp.dot(h, wc_ref[...].astype(BF16), preferred_element_type=F32)
    raw_r = lax.dot_general(wr_ref[...].astype(BF16), h, (((1,), (1,)), ((), ())),
                            preferred_element_type=F32)
    g_c = -jnp.exp(alog_c_ref[...]) * _softplus(raw_c[:, :nd] + dtb_c_ref[...])
    g_r = -jnp.exp(alog_r_ref[...]) * _softplus(raw_r[:nd, :] + dtb_r_ref[...])
    incl_f, _ = _tile_masks(tl, False)
    incl_b, _ = _tile_masks(tl, True)
    one_f = jnp.where(incl_f, 1.0, 0.0)
    one_b = jnp.where(incl_b, 1.0, 0.0)
    cum_c = jnp.concatenate([_dot_exact_lhs01(one_f, g_c[:, :n_heads]),
                             _dot_exact_lhs01(one_b, g_c[:, n_heads:])], axis=1)
    cum_r = jnp.concatenate([_dot_exact_rhs01(g_r[:n_heads, :], one_b),
                             _dot_exact_rhs01(g_r[n_heads:, :], one_f)], axis=0)
    beta_c = _sigmoid(raw_c[:, nd:])
    gc_ref[...] = jnp.concatenate([cum_c, beta_c], axis=1)
    gr_ref[...] = jnp.concatenate([cum_r, jnp.zeros_like(cum_r)], axis=0)


def gate_prep(h_bf, w_gate, a_log, dt_bias, n_heads):
    t, d = h_bf.shape
    tl = SCAN_TILE
    nd = 2 * n_heads
    body = functools.partial(_gate_prep_body, n_heads=n_heads)
    full = lambda shape: pl.BlockSpec(shape, lambda i: (0, 0))
    return pl.pallas_call(
        body, grid=(t // tl,),
        in_specs=[pl.BlockSpec((tl, d), lambda i: (i, 0)), full((d, 2 * nd)), full((2 * nd, d)),
                  full((1, nd)), full((1, nd)), full((nd, 1)), full((nd, 1))],
        out_specs=[pl.BlockSpec((tl, 2 * nd), lambda i: (i, 0)), pl.BlockSpec((2 * nd, tl), lambda i: (0, i))],
        out_shape=[jax.ShapeDtypeStruct((t, 2 * nd), F32), jax.ShapeDtypeStruct((2 * nd, t), F32)],
        compiler_params=_params("arbitrary"), name="gate_prep",
    )(h_bf, w_gate, w_gate.T, a_log.reshape(1, nd), dt_bias.reshape(1, nd),
      a_log.reshape(nd, 1), dt_bias.reshape(nd, 1))


def _select_col(x, idx):
    lane = lax.broadcasted_iota(jnp.int32, x.shape, 1)
    return jnp.sum(jnp.where(lane == idx, x, 0.0), axis=1, keepdims=True)


def _select_row(x, idx):
    row = lax.broadcasted_iota(jnp.int32, x.shape, 0)
    return jnp.sum(jnp.where(row == idx, x, 0.0), axis=0, keepdims=True)


def _run_interleaved(chains):
    chains = list(chains)
    while chains:
        alive = []
        for ch in chains:
            try:
                next(ch)
                alive.append(ch)
            except StopIteration:
                pass
        chains = alive


def _delta_chain(q, k, v, gc_col, gc_row, beta_col, s_ref, o_ref, cols, reverse, inv_passes):
    tl, kd_ = k.shape
    n_chunks = tl // CHUNK
    incl, strict = _tile_masks(tl, reverse)
    decay = jnp.exp(jnp.where(incl, gc_col - gc_row, NEG_BIG))
    kb, qb = k.astype(BF16), q.astype(BF16)
    kkt = _dot_nt(kb, kb)
    qkt = _dot_nt(qb, kb)
    yield
    x = jnp.where(strict, kkt * (-beta_col) * decay, 0.0)
    dot_inv = _dot3 if inv_passes == 3 else _dot
    ri = lax.broadcasted_iota(jnp.int32, (tl, tl), 0)
    ci = lax.broadcasted_iota(jnp.int32, (tl, tl), 1)
    r = jnp.where(ri == ci, 1.0, 0.0) + x
    n_sq = int(math.log2(CHUNK)) - 1
    for _ in range(n_sq):
        x = dot_inv(x, x)
        r = r + dot_inv(r, x)
        yield
    e_g = jnp.exp(gc_col)
    rhs = jnp.concatenate([v * beta_col, k * (beta_col * e_g)], axis=1)
    sol = dot_inv(r, rhs)
    yield
    u0, w = sol[:, :v.shape[1]], sol[:, v.shape[1]:]
    attn = (qkt * decay).astype(BF16)
    o0 = _dot(attn, u0)
    qe = q * e_g - _dot(attn, w)
    tot_rows = []
    for c in range(n_chunks):
        last = c * CHUNK if reverse else c * CHUNK + CHUNK - 1
        tot_rows.append(jnp.broadcast_to(gc_col[last:last + 1, :], (CHUNK, 1)))
    tot = jnp.concatenate(tot_rows, axis=0)
    kdec_t = jnp.transpose(k * jnp.exp(tot - gc_col))
    wu = jnp.concatenate([-w, u0], axis=1).astype(BF16)
    lane = lax.broadcasted_iota(jnp.int32, kdec_t.shape, 1)
    pns = [_dot(jnp.where((lane // CHUNK) == c, kdec_t, 0.0), wu) for c in range(n_chunks)]
    yield
    order = range(n_chunks - 1, -1, -1) if reverse else range(n_chunks)
    for c in order:
        rows = slice(c * CHUNK, (c + 1) * CHUNK)
        pn = pns[c]
        lhs = jnp.concatenate([qe[rows], pn[:, :kd_]], axis=0)
        s = s_ref[...]
        res = _dot(lhs, s)
        o_ref[rows, cols] = o0[rows] + res[:CHUNK]
        last = c * CHUNK if reverse else c * CHUNK + CHUNK - 1
        gl = jnp.exp(gc_col[last:last + 1, :])
        s_ref[...] = gl * s + res[CHUNK:] + pn[:, kd_:]
        yield


def _delta_body(qf_ref, kf_ref, vf_ref, gcf_ref, grf_ref, qb_ref, kb_ref, vb_ref, gcb_ref, grb_ref,
                of_ref, ob_ref, sf_ref, sb_ref, *, n_heads, head_dim, inv_passes):
    hg = pl.program_id(1)
    heads_per_step = sf_ref.shape[0]

    @pl.when(pl.program_id(2) == 0)
    def _():
        sf_ref[...] = jnp.zeros_like(sf_ref)
        sb_ref[...] = jnp.zeros_like(sb_ref)

    chains = []
    for g in range(heads_per_step):
        cols = slice(g * head_dim, (g + 1) * head_dim)
        for reverse, (q_ref, k_ref, v_ref, gc_ref, gr_ref, o_ref, s_ref) in enumerate(
                [(qf_ref, kf_ref, vf_ref, gcf_ref, grf_ref, of_ref, sf_ref),
                 (qb_ref, kb_ref, vb_ref, gcb_ref, grb_ref, ob_ref, sb_ref)]):
            idx = reverse * n_heads + hg * heads_per_step + g
            gcs = gc_ref[...]
            gc_col = _select_col(gcs, idx)
            beta_col = _select_col(gcs, 2 * n_heads + idx)
            gc_row = _select_row(gr_ref[...], idx)
            chains.append(_delta_chain(q_ref[:, cols], k_ref[:, cols], v_ref[:, cols], gc_col, gc_row, beta_col,
                                       s_ref.at[g], o_ref, cols, bool(reverse), inv_passes))
    _run_interleaved(chains)


def delta_scan(qkv, gc, gr, geom, ctx_len, n_heads, head_dim, inv_passes=3, heads_per_step=DELTA_HEADS_PER_STEP):
    n_lat, lat_len, n_batch = geom
    t = qkv.shape[0]
    tl = SCAN_TILE
    nl, nc, ctx0 = lat_len // tl, ctx_len // tl, n_lat // tl
    n_hg = n_heads // heads_per_step
    gw = heads_per_step * head_dim
    tile = functools.partial(_segment_tile, nl=nl, nc=nc, ctx_tile0=ctx0)
    specs = []
    for reverse in (False, True):
        tix = functools.partial(tile, reverse=reverse)
        specs += [pl.BlockSpec((tl, gw), lambda b, h, s, tix=tix: (tix(b, s), h)),
                  pl.BlockSpec((tl, gw), lambda b, h, s, tix=tix: (tix(b, s), n_hg + h)),
                  pl.BlockSpec((tl, gw), lambda b, h, s, tix=tix: (tix(b, s), 2 * n_hg + h)),
                  pl.BlockSpec((tl, 4 * n_heads), lambda b, h, s, tix=tix: (tix(b, s), 0)),
                  pl.BlockSpec((4 * n_heads, tl), lambda b, h, s, tix=tix: (0, tix(b, s)))]
    out_specs = [pl.BlockSpec((tl, gw), lambda b, h, s, tix=functools.partial(tile, reverse=r): (tix(b, s), h))
                 for r in (False, True)]
    body = functools.partial(_delta_body, n_heads=n_heads, head_dim=head_dim, inv_passes=inv_passes)
    return pl.pallas_call(
        body, grid=(n_batch, n_hg, nl + nc), in_specs=specs, out_specs=out_specs,
        out_shape=[jax.ShapeDtypeStruct((t, n_heads * head_dim), F32)] * 2,
        scratch_shapes=[pltpu.VMEM((heads_per_step, head_dim, head_dim), F32)] * 2,
        compiler_params=_params("arbitrary", "arbitrary", "arbitrary"), name="delta_scan",
    )(qkv, qkv, qkv, gc, gr, qkv, qkv, qkv, gc, gr)


def _hgrn_diag(hq_ref, hf_ref, hv_ref, cols, lb, stage_ref, diag_ref, reverse):
    n_sub = hq_ref.shape[0] // SUB
    for n, ref in enumerate((hq_ref, hf_ref, hv_ref)):
        stage_ref[n] = ref[:, cols]

    def slab(n, r):
        return stage_ref.at[n][pl.ds(r, n_sub, stride=SUB), :]

    q_x = [_silu(slab(0, r)) for r in range(SUB)]
    f_x = [lb + (1.0 - lb) * _sigmoid(slab(1, r)) for r in range(SUB)]
    k_x = [1.0 - f for f in f_x]
    lf_x = [jnp.log(f) for f in f_x]
    v_x = [slab(2, r) for r in range(SUB)]
    p_x = [None] * SUB
    scan_rows = range(SUB - 1, -1, -1) if reverse else range(SUB)
    acc = None
    for r in scan_rows:
        acc = lf_x[r] if acc is None else acc + lf_x[r]
        p_x[r] = acc
    pairs = [(i, j) for i in range(SUB) for j in range(SUB) if (j >= i if reverse else j <= i)]
    terms = []
    for i, j in pairs:
        qk = q_x[i] * k_x[j]
        terms.append((qk if i == j else qk * jnp.exp(p_x[i] - p_x[j])).astype(BF16))
    kd_ = terms[0].shape[1]
    a_rep = jnp.dot(jnp.concatenate(terms, axis=0), jnp.ones((kd_, kd_), BF16), preferred_element_type=F32)
    o_x = [None] * SUB
    for n, (i, j) in enumerate(pairs):
        contrib = a_rep[n * n_sub:(n + 1) * n_sub] * v_x[j]
        o_x[i] = contrib if o_x[i] is None else o_x[i] + contrib
    for r in range(SUB):
        diag_ref[pl.ds(r, n_sub, stride=SUB), :] = o_x[r]


def _hgrn_chain(hq_ref, hf_ref, hv_ref, cols, lb, st_ref, o_ref, stage_ref, diag_ref, reverse):
    hq, hf, hv = hq_ref[:, cols], hf_ref[:, cols], hv_ref[:, cols]
    tl, kd_ = hq.shape
    n_chunks = tl // CHUNK
    n_sub = tl // SUB
    sub_per_chunk = CHUNK // SUB
    q = _silu(hq)
    f = lb + (1.0 - lb) * _sigmoid(hf)
    k = 1.0 - f
    lf = jnp.log(f)
    incl, _ = _tile_masks(tl, reverse)
    cum = _dot_exact_lhs01(jnp.where(incl, 1.0, 0.0), lf)
    excl = cum - lf

    def bcast_rows(src, row, n):
        return jnp.broadcast_to(src[row:row + 1, :], (n, kd_))

    chunk_last = [(c * CHUNK if reverse else c * CHUNK + CHUNK - 1) for c in range(n_chunks)]
    tot = jnp.concatenate([bcast_rows(cum, chunk_last[c], CHUNK) for c in range(n_chunks)], axis=0)
    sub_first = [(m * SUB + SUB - 1 if reverse else m * SUB) for m in range(n_sub)]
    r_sub = jnp.concatenate([bcast_rows(excl, sub_first[m], SUB) for m in range(n_sub)], axis=0)
    q_t = q * jnp.exp(cum - r_sub)
    qd = q * jnp.exp(cum)
    kd = k * jnp.exp(tot - cum)
    vb = hv.astype(BF16)

    i = lax.broadcasted_iota(jnp.int32, (tl, tl), 0)
    j = lax.broadcasted_iota(jnp.int32, (tl, tl), 1)
    same = (i // CHUNK) == (j // CHUNK)
    pos_i = (i % CHUNK) // SUB
    pos_j = (j % CHUNK) // SUB
    if reverse:
        pos_i, pos_j = sub_per_chunk - 1 - pos_i, sub_per_chunk - 1 - pos_j
    a_off = jnp.zeros((tl, tl), F32)
    for lvl in range(1, sub_per_chunk):
        ref_rows = []
        for c in range(n_chunks):
            m = c * sub_per_chunk + (sub_per_chunk - 1 - lvl if reverse else lvl)
            ref_rows.append(bcast_rows(excl, sub_first[m], CHUNK))
        r_lvl = jnp.concatenate(ref_rows, axis=0)
        k_t = k * jnp.exp(jnp.minimum(r_lvl - cum, 0.0))
        a_l = _dot_nt(q_t, k_t)
        a_off = a_off + jnp.where(same & (pos_i == lvl) & (pos_j < lvl), a_l, 0.0)
        yield
    o_intra = _dot(a_off, vb)
    yield

    _hgrn_diag(hq_ref, hf_ref, hv_ref, cols, lb, stage_ref, diag_ref, reverse)
    yield
    o_intra = o_intra + diag_ref[...]

    v_t = jnp.transpose(hv)
    lane = lax.broadcasted_iota(jnp.int32, v_t.shape, 1)
    kdb = kd.astype(BF16)
    n_ts = [_dot(jnp.where((lane // CHUNK) == c, v_t, 0.0), kdb) for c in range(n_chunks)]
    yield
    order = range(n_chunks - 1, -1, -1) if reverse else range(n_chunks)
    for c in order:
        rows = slice(c * CHUNK, (c + 1) * CHUNK)
        st = st_ref[...]
        o_ref[rows, cols] = o_intra[rows] + _dot_nt(qd[rows], st)
        st_ref[...] = st * jnp.exp(cum[chunk_last[c]:chunk_last[c] + 1, :]) + n_ts[c]
        yield


def _hgrn_body(qf_ref, ff_ref, vf_ref, qb_ref, fb_ref, vb_ref, lb_ref, of_ref, ob_ref, sf_ref, sb_ref, stage_ref,
               diag_ref, *, key_dim):
    heads_per_step = sf_ref.shape[0]

    @pl.when(pl.program_id(2) == 0)
    def _():
        sf_ref[...] = jnp.zeros_like(sf_ref)
        sb_ref[...] = jnp.zeros_like(sb_ref)

    chains = []
    for g in range(heads_per_step):
        cols = slice(g * key_dim, (g + 1) * key_dim)
        lb = lb_ref[:, cols]
        chains.append(_hgrn_chain(qf_ref, ff_ref, vf_ref, cols, lb, sf_ref.at[g], of_ref,
                                  stage_ref.at[2 * g], diag_ref.at[2 * g], False))
        chains.append(_hgrn_chain(qb_ref, fb_ref, vb_ref, cols, lb, sb_ref.at[g], ob_ref,
                                  stage_ref.at[2 * g + 1], diag_ref.at[2 * g + 1], True))
    _run_interleaved(chains)


def hgrn_scan(proj_hg, lb, geom, ctx_len, n_heads, key_dim, heads_per_step=HGRN_HEADS_PER_STEP):
    n_lat, lat_len, n_batch = geom
    t = proj_hg.shape[0]
    tl = SCAN_TILE
    nl, nc, ctx0 = lat_len // tl, ctx_len // tl, n_lat // tl
    n_hg = n_heads // heads_per_step
    gw = heads_per_step * key_dim
    tile = functools.partial(_segment_tile, nl=nl, nc=nc, ctx_tile0=ctx0)
    specs = []
    for reverse in (False, True):
        tix = functools.partial(tile, reverse=reverse)
        fcol = (1 + int(reverse)) * n_hg
        specs += [pl.BlockSpec((tl, gw), lambda b, h, s, tix=tix: (tix(b, s), h)),
                  pl.BlockSpec((tl, gw), lambda b, h, s, tix=tix, fcol=fcol: (tix(b, s), fcol + h)),
                  pl.BlockSpec((tl, gw), lambda b, h, s, tix=tix: (tix(b, s), 3 * n_hg + h))]
    specs.append(pl.BlockSpec((1, gw), lambda b, h, s: (0, h)))
    out_specs = [pl.BlockSpec((tl, gw), lambda b, h, s, tix=functools.partial(tile, reverse=r): (tix(b, s), h))
                 for r in (False, True)]
    return pl.pallas_call(
        functools.partial(_hgrn_body, key_dim=key_dim), grid=(n_batch, n_hg, nl + nc),
        in_specs=specs, out_specs=out_specs,
        out_shape=[jax.ShapeDtypeStruct((t, n_heads * key_dim), F32)] * 2,
        scratch_shapes=[pltpu.VMEM((heads_per_step, key_dim, key_dim), F32)] * 2
        + [pltpu.VMEM((2 * heads_per_step, 3, tl, key_dim), F32), pltpu.VMEM((2 * heads_per_step, tl, key_dim), F32)],
        compiler_params=_params("arbitrary", "arbitrary", "arbitrary"), name="hgrn_scan",
    )(proj_hg, proj_hg, proj_hg, proj_hg, proj_hg, proj_hg, lb)


def _mix_out2_body(df_ref, db_ref, hf_ref, hb_ref, z_ref, og_ref, dnw_ref, hgw_ref, o_ref, *, head_dim):
    def normed(o, nw):
        outs = []
        for h in range(o.shape[1] // head_dim):
            oh = o[:, h * head_dim:(h + 1) * head_dim]
            outs.append(oh * lax.rsqrt(jnp.mean(oh * oh, axis=-1, keepdims=True) + NORM_EPS) * nw)
        return jnp.concatenate(outs, axis=1)

    dn = normed(df_ref[...] + db_ref[...], dnw_ref[...]) * _silu(z_ref[...])
    hg = normed(hf_ref[...] + hb_ref[...], hgw_ref[...]) * _sigmoid(og_ref[...])
    half = dn.shape[1]
    o_ref[:, :half] = dn.astype(o_ref.dtype)
    o_ref[:, half:] = hg.astype(o_ref.dtype)


def mix_out(dn_f, dn_b, hg_f, hg_b, z_src, z_blk, og_src, og_blk, dn_norm, hg_norm, head_dim):
    t, w = dn_f.shape
    tl = SCAN_TILE
    row = lambda i: (i, 0)
    body = functools.partial(_mix_out2_body, head_dim=head_dim)
    return pl.pallas_call(
        body, grid=(t // tl,),
        in_specs=[pl.BlockSpec((tl, w), row)] * 4
        + [pl.BlockSpec((tl, w), lambda i: (i, z_blk)), pl.BlockSpec((tl, w), lambda i: (i, og_blk)),
           pl.BlockSpec((1, head_dim), lambda i: (0, 0)), pl.BlockSpec((1, head_dim), lambda i: (0, 0))],
        out_specs=pl.BlockSpec((tl, 2 * w), row),
        out_shape=jax.ShapeDtypeStruct((t, 2 * w), BF16),
        compiler_params=_params("arbitrary"), name="mix_out",
    )(dn_f, dn_b, hg_f, hg_b, z_src, og_src, dn_norm.reshape(1, head_dim), hg_norm.reshape(1, head_dim))


def _first_max(x, ids, n):
    m = jnp.max(x, axis=0, keepdims=True)
    first = jnp.min(jnp.where(x == m, ids, n), axis=0, keepdims=True)
    return m, first


def _route_body(lg_ref, bias_ref, idx_ref, rank_ref, w_ref, cnt_ref, carry_ref, *,
                n_groups, topk_groups, top_k, scale):
    i = pl.program_id(0)

    @pl.when(i == 0)
    def _():
        carry_ref[...] = jnp.zeros_like(carry_ref)

    lg = lg_ref[...]
    n_exp, tl = lg.shape
    per = n_exp // n_groups
    scores = 1.0 / (1.0 + jnp.exp(-lg))
    biased = scores + bias_ref[...]
    sub = lax.broadcasted_iota(jnp.int32, (per, tl), 0)
    g_rows = []
    for g in range(n_groups):
        xg = biased[g * per:(g + 1) * per]
        m1, i1 = _first_max(xg, sub, per)
        m2 = jnp.max(jnp.where(sub == i1, NEG_INF, xg), axis=0, keepdims=True)
        g_rows.append(m1 + m2)
    gscore = jnp.concatenate(g_rows, axis=0)
    gid = lax.broadcasted_iota(jnp.int32, (n_groups, tl), 0)
    gsel = jnp.zeros((n_groups, tl), jnp.bool_)
    for _ in range(topk_groups):
        _, first = _first_max(gscore, gid, n_groups)
        hit = gid == first
        gsel = gsel | hit
        gscore = jnp.where(hit, NEG_INF, gscore)
    eid = lax.broadcasted_iota(jnp.int32, (n_exp, tl), 0)
    gmask = jnp.concatenate([jnp.broadcast_to(gsel[g:g + 1], (per, tl)) for g in range(n_groups)], axis=0)
    masked = jnp.where(gmask, biased, NEG_INF)
    sel = jnp.zeros((n_exp, tl), jnp.bool_)
    hits, firsts = [], []
    for _ in range(top_k):
        _, first = _first_max(masked, eid, n_exp)
        hit = eid == first
        hits.append(hit)
        firsts.append(first)
        sel = sel | hit
        masked = jnp.where(hit, NEG_INF, masked)
    self = jnp.where(sel, 1.0, 0.0)
    ti = lax.broadcasted_iota(jnp.int32, (tl, tl), 0)
    tj = lax.broadcasted_iota(jnp.int32, (tl, tl), 1)
    before = jnp.where(ti < tj, 1.0, 0.0).astype(BF16)
    carry = carry_ref[...]
    rank_full = jnp.dot(self.astype(BF16), before, preferred_element_type=F32) + carry
    w_rows = [jnp.sum(jnp.where(hit, scores, 0.0), axis=0, keepdims=True) for hit in hits]
    r_rows = [jnp.sum(jnp.where(hit, rank_full, 0.0), axis=0, keepdims=True) for hit in hits]
    w8 = jnp.concatenate(w_rows, axis=0)
    idx_ref[...] = jnp.concatenate(firsts, axis=0)
    rank_ref[...] = jnp.concatenate(r_rows, axis=0).astype(jnp.int32)
    w_ref[...] = w8 / jnp.sum(w8, axis=0, keepdims=True) * scale
    carry = carry + jnp.sum(self, axis=1, keepdims=True)
    carry_ref[...] = carry
    cnt_ref[...] = carry.astype(jnp.int32)


def route(logits_t, router_b, tl=256):
    n_exp, t = logits_t.shape
    body = functools.partial(_route_body, n_groups=N_GROUPS, topk_groups=TOPK_GROUPS, top_k=TOP_K,
                             scale=ROUTED_SCALE)
    tok_spec = pl.BlockSpec((TOP_K, tl), lambda i: (0, i))
    return pl.pallas_call(
        body, grid=(t // tl,),
        in_specs=[pl.BlockSpec((n_exp, tl), lambda i: (0, i)), pl.BlockSpec((n_exp, 1), lambda i: (0, 0))],
        out_specs=[tok_spec, tok_spec, tok_spec, pl.BlockSpec((n_exp, 1), lambda i: (0, 0))],
        out_shape=[jax.ShapeDtypeStruct((TOP_K, t), jnp.int32), jax.ShapeDtypeStruct((TOP_K, t), jnp.int32),
                   jax.ShapeDtypeStruct((TOP_K, t), F32), jax.ShapeDtypeStruct((n_exp, 1), jnp.int32)],
        scratch_shapes=[pltpu.VMEM((n_exp, 1), F32)],
        compiler_params=_params("arbitrary"), name="route",
    )(logits_t, router_b.reshape(n_exp, 1))


def _slab_copy(src_ref, src_tok, dst_ref, dst_tok, rows, sem):
    return pltpu.make_async_copy(src_ref.at[pl.ds(pl.multiple_of(src_tok * rows, rows), rows)],
                                 dst_ref.at[pl.ds(pl.multiple_of(dst_tok * rows, rows), rows)], sem)


def _dispatch_body(dest_ref, pad_ref, h_ref, xs_ref, zero_ref, sem, *, top_k, rows):
    tl = h_ref.shape[0] // rows
    n_exp = pad_ref.shape[1]

    def issue(r, carry):
        for k in range(top_k):
            _slab_copy(h_ref, r, xs_ref, dest_ref[r * top_k + k], rows, sem.at[0]).start(priority=k % 2)
        return carry

    lax.fori_loop(0, tl, issue, 0, unroll=DMA_LOOP_UNROLL)

    @pl.when(pl.program_id(0) == 0)
    def _():
        zero_ref[...] = jnp.zeros_like(zero_ref)

        def fill(e, total):
            first, n_pad = pad_ref[0, e], pad_ref[1, e]

            def one(j, carry):
                _slab_copy(zero_ref, 0, xs_ref, first + j, rows, sem.at[1]).start()
                return carry

            lax.fori_loop(0, n_pad, one, 0)
            return total + n_pad

        total = lax.fori_loop(0, n_exp, fill, 0)

        def drain_pad(j, carry):
            _slab_copy(zero_ref, 0, xs_ref, 0, rows, sem.at[1]).wait()
            return carry

        lax.fori_loop(0, total, drain_pad, 0)

    def drain(r, carry):
        for k in range(top_k):
            _slab_copy(h_ref, 0, xs_ref, 0, rows, sem.at[0]).wait()
        return carry

    lax.fori_loop(0, tl, drain, 0, unroll=DMA_LOOP_UNROLL)


def dispatch(h_slab, dest_flat, pad_info, n_slot, top_k, rows, tl=256):
    t = h_slab.shape[0] // rows
    return pl.pallas_call(
        functools.partial(_dispatch_body, top_k=top_k, rows=rows), grid=(t // tl,),
        in_specs=[pl.BlockSpec((tl * top_k,), lambda i: (i,), memory_space=pltpu.SMEM),
                  pl.BlockSpec(memory_space=pltpu.SMEM),
                  pl.BlockSpec((tl * rows, LANES), lambda i: (i, 0))],
        out_specs=pl.BlockSpec(memory_space=pl.ANY),
        out_shape=jax.ShapeDtypeStruct((n_slot * rows, LANES), h_slab.dtype),
        scratch_shapes=[pltpu.VMEM((rows, LANES), h_slab.dtype), pltpu.SemaphoreType.DMA((2,))],
        compiler_params=_params("arbitrary"), name="dispatch",
    )(dest_flat, pad_info, h_slab)


def _combine_body(dest_ref, dest_next_ref, w_ref, sh_ref, x_ref, gate_ref, y_ref, o_ref, buf_ref, acc_ref, sem, *,
                  top_k, rows):
    i = pl.program_id(0)
    n = pl.num_programs(0)
    tl = x_ref.shape[0]

    def gather(d_ref, slot):
        def issue(r, carry):
            for k in range(top_k):
                _slab_copy(y_ref, d_ref[r * top_k + k], buf_ref.at[slot], k * tl + r, rows,
                           sem.at[slot]).start(priority=k % 2)
            return carry
        lax.fori_loop(0, tl, issue, 0, unroll=DMA_LOOP_UNROLL)

    @pl.when(i == 0)
    def _():
        gather(dest_ref, 0)

    @pl.when(i + 1 < n)
    def _():
        gather(dest_next_ref, (i + 1) % 2)

    slot = i % 2

    def drain(r, carry):
        for k in range(top_k):
            _slab_copy(y_ref, 0, buf_ref.at[slot], 0, rows, sem.at[slot]).wait()
        return carry

    lax.fori_loop(0, tl, drain, 0, unroll=DMA_LOOP_UNROLL)
    w = w_ref[...]
    acc = sh_ref[...]
    for k in range(top_k):
        wk = jnp.broadcast_to(w[:, k:k + 1], (tl, LANES))
        wk = jnp.broadcast_to(wk[:, None, :], (tl, rows, LANES)).reshape(tl * rows, LANES)
        acc = acc + buf_ref[slot, pl.ds(k * tl * rows, tl * rows), :] * wk
    acc_ref[...] = acc
    for c in range(rows):
        cols = slice(c * LANES, (c + 1) * LANES)
        o_ref[:, cols] = x_ref[:, cols] + gate_ref[0][:, cols] * acc_ref[pl.ds(c, tl, stride=rows), :]


def combine(y_slab, dest_flat, w, shared_slab, x, mod3, gate_idx, geom, tl=128):
    t, d = x.shape
    rows = d // LANES
    top_k = w.shape[1]
    n_lat, lat_len, n_batch = geom
    n_tiles = t // tl
    row = functools.partial(_mod_row, tile_rows=tl, n_lat=n_lat, lat_len=lat_len, n_batch=n_batch)
    return pl.pallas_call(
        functools.partial(_combine_body, top_k=top_k, rows=rows), grid=(n_tiles,),
        in_specs=[pl.BlockSpec((tl * top_k,), lambda i: (i,), memory_space=pltpu.SMEM),
                  pl.BlockSpec((tl * top_k,), lambda i: (jnp.minimum(i + 1, n_tiles - 1),),
                               memory_space=pltpu.SMEM),
                  pl.BlockSpec((tl, top_k), lambda i: (i, 0)),
                  pl.BlockSpec((tl * rows, LANES), lambda i: (i, 0)),
                  pl.BlockSpec((tl, d), lambda i: (i, 0)),
                  pl.BlockSpec((1, 1, d), lambda i: (row(i) * 6 + gate_idx, 0, 0)),
                  pl.BlockSpec(memory_space=pl.ANY)],
        out_specs=pl.BlockSpec((tl, d), lambda i: (i, 0)),
        out_shape=jax.ShapeDtypeStruct((t, d), F32),
        scratch_shapes=[pltpu.VMEM((2, top_k * tl * rows, LANES), F32), pltpu.VMEM((tl * rows, LANES), F32),
                        pltpu.SemaphoreType.DMA((2,))],
        compiler_params=_params("arbitrary"), name="combine",
    )(dest_flat, dest_flat, w, shared_slab, x, mod3, y_slab)


def moe_ffn(x, h_pk, logits_t, router_b, exp_gate_up, exp_down, shared_gate_up, shared_down, mod3, geom):
    t = x.shape[0]
    slab_rows = h_pk.shape[0] // t
    e_count = exp_gate_up.shape[0]
    blk = EXPERT_SLOT_BLOCK
    idx8, rank8, w8, counts = route(logits_t, router_b)
    counts = counts[:, 0]
    padded = (counts + blk - 1) // blk * blk
    pad_end = jnp.cumsum(padded)
    pad_start = pad_end - padded
    start8 = jnp.sum(jnp.where(idx8[None] == jnp.arange(e_count, dtype=jnp.int32)[:, None, None],
                               pad_start[:, None, None], 0), axis=0)
    dest_flat = (start8 + rank8).T.reshape(-1)
    n_blk = (t * TOP_K + e_count * (blk - 1)) // blk + 1
    n_slot = n_blk * blk
    blk_starts = jnp.arange(n_blk, dtype=jnp.int32) * blk
    blk_e = jnp.sum((pad_end[None, :] <= blk_starts[:, None]).astype(jnp.int32), axis=1)
    blk_e = jnp.minimum(blk_e, e_count - 1)
    n_used = (pad_end[-1] // blk).astype(jnp.int32).reshape(1)
    pad_info = jnp.stack([pad_start + counts, padded - counts]).astype(jnp.int32)
    x_sorted = dispatch(h_pk, dest_flat, pad_info, n_slot, TOP_K, slab_rows)
    y_slot = expert_blocks(x_sorted, exp_gate_up, exp_down, blk_e, n_used, blk)
    shared = expert_blocks(h_pk, shared_gate_up[None], shared_down[None], jnp.zeros((t // blk,), jnp.int32),
                           jnp.full((1,), t // blk, jnp.int32), blk)
    return combine(y_slot, dest_flat, w8.T, shared, x, mod3, 5, geom)


def kernel(x, c, ctx, c_ctx, hg_lb_logits, l0_mod_w, l0_mod_b, l0_norm1, l0_norm2, l0_w_in, l0_dn_conv, l0_dn_a_log, l0_dn_dt_bias, l0_dn_norm, l0_hg_norm, l0_w_out, l0_router_w, l0_router_b, l0_exp_gate_up, l0_exp_down, l0_shared_gate_up, l0_shared_down, l1_mod_w, l1_mod_b, l1_norm1, l1_norm2, l1_w_in, l1_q_norm, l1_k_norm, l1_lambda, l1_sub_norm, l1_w_out, l1_router_w, l1_router_b, l1_exp_gate_up, l1_exp_down, l1_shared_gate_up, l1_shared_down):
    n_batch, lat_len, d = x.shape
    ctx_len = ctx.shape[1]
    n_lat = n_batch * lat_len
    n_ctx = n_batch * ctx_len
    geom = (n_lat, lat_len, n_batch)
    geom_lat_only = (n_lat, lat_len, n_batch)

    xs = jnp.concatenate([x.reshape(n_lat, d), ctx.reshape(n_ctx, d)], axis=0)
    cond = jnp.concatenate([c, c_ctx[None], jnp.zeros((MOD_ROWS - n_batch - 1, d), F32)], axis=0)

    mod3 = modulation(cond, l0_mod_w, l0_mod_b).reshape(MOD_ROWS * 6, 1, d)
    h = adaln(xs, l0_norm1, mod3, 0, geom)
    n_dn = IN0_SIZES[0] + IN0_SIZES[1]
    n_gate = IN0_SIZES[2] + IN0_SIZES[3]
    proj_dn = matmul(h, l0_w_in, n_cols=n_dn)
    proj_hg = matmul(h, l0_w_in[:, n_dn + n_gate:])
    gc, gr = gate_prep(h, l0_w_in[:, n_dn:n_dn + n_gate], l0_dn_a_log, l0_dn_dt_bias, DN_HEADS)
    conv_t = jnp.concatenate([l0_dn_conv.T, jnp.zeros((8 - l0_dn_conv.shape[1], 3 * DN_WIDTH), F32)], axis=0)
    qkv = dn_prep(proj_dn, conv_t, geom, ctx_len, DN_WIDTH, DN_HEAD_DIM)
    dn_f, dn_b = delta_scan(qkv, gc, gr, geom, ctx_len, DN_HEADS, DN_HEAD_DIM, inv_passes=DELTA_INV_PASSES)
    lb = jnp.cumsum(jax.nn.softmax(hg_lb_logits, axis=0), axis=0)[0:1]
    hg_f, hg_b = hgrn_scan(proj_hg, lb, geom, ctx_len, HG_HEADS, HG_KEY_DIM)
    y = mix_out(dn_f, dn_b, hg_f, hg_b, proj_dn, 3, proj_hg, 4, l0_dn_norm, l0_hg_norm, DN_HEAD_DIM)
    xs = matmul_residual(y, l0_w_out, xs, mod3, 2, geom)
    h, logits_t = adaln(xs, l0_norm2, mod3, 3, geom, router_w=l0_router_w)
    xs = moe_ffn(xs, h, logits_t, l0_router_b, l0_exp_gate_up, l0_exp_down, l0_shared_gate_up, l0_shared_down,
                 mod3, geom)

    mod3 = modulation(cond, l1_mod_w, l1_mod_b).reshape(MOD_ROWS * 6, 1, d)
    h = adaln(xs, l1_norm1, mod3, 0, geom)
    proj = matmul(h, l1_w_in)
    cos_t, sin_t = rope_tables(lat_len, n_batch, n_ctx, DA_HEAD_DIM)
    qkv = qk_prep(proj, l1_q_norm, l1_k_norm, cos_t, sin_t)
    lam_init = 0.8 - 0.6 * math.exp(-0.3 * 1)
    lmbda = (jnp.exp(jnp.sum(l1_lambda[0] * l1_lambda[1])) - jnp.exp(jnp.sum(l1_lambda[2] * l1_lambda[3]))
             + lam_init)
    y = diff_attention(qkv, lmbda, l1_sub_norm, n_batch, lat_len, ctx_len, DA_HEADS, DA_HEAD_DIM,
                       1.0 - lam_init)
    xl = matmul_residual(y, l1_w_out, xs[:n_lat], mod3, 2, geom_lat_only)
    h, logits_t = adaln(xl, l1_norm2, mod3, 3, geom_lat_only, router_w=l1_router_w)
    xl = moe_ffn(xl, h, logits_t, l1_router_b, l1_exp_gate_up, l1_exp_down, l1_shared_gate_up, l1_shared_down,
                 mod3, geom_lat_only)
    return xl.reshape(n_batch, lat_len, d)
```

```python
import functools
import math

import jax
import jax.numpy as jnp
import numpy as np
from jax import lax
from jax.experimental import pallas as pl
from jax.experimental.pallas import tpu as pltpu

F32 = jnp.float32
BF16 = jnp.bfloat16

NORM_EPS = 1e-6
GRID_W = 64
ROPE_BASE = 10000.0

DN_HEADS = 8
DN_HEAD_DIM = 128
DN_WIDTH = DN_HEADS * DN_HEAD_DIM
DN_CHUNK = 64
HG_HEADS = 8
HG_KEY_DIM = 128
HG_VAL_DIM = 128
HG_K_WIDTH = HG_HEADS * HG_KEY_DIM
HG_V_WIDTH = HG_HEADS * HG_VAL_DIM
HG_CHUNK = 64
IN0_SIZES = (3 * DN_WIDTH, DN_WIDTH, 2 * DN_HEADS, 2 * DN_HEADS,
             HG_K_WIDTH, 2 * HG_K_WIDTH, HG_V_WIDTH, HG_V_WIDTH)
DA_HEADS = 8
DA_HEAD_DIM = 128
N_EXPERTS = 64
TOP_K = 8
N_GROUPS = 8
TOPK_GROUPS = 4
EXPERT_FF = 512
ROUTED_SCALE = 2.5

VMEM_LIMIT_BYTES = 56 * 1024 * 1024
MOD_ROWS = 8
EXPERT_SLOT_BLOCK = 256
DMA_LOOP_UNROLL = 8
MM_TILE_M = 1024
MM_TILE_N = 1024
MM_EPILOGUE_CHUNKS = 4
LANES = 128
ATTN_TILE_Q = 512
ATTN_ROW_CHUNK = 128


def _params(*sem):
    return pltpu.CompilerParams(dimension_semantics=sem, vmem_limit_bytes=VMEM_LIMIT_BYTES)


def _silu(x):
    return x * (1.0 / (1.0 + jnp.exp(-x)))


def _mod_body(c_ref, w_ref, b_ref, o_ref):
    a = _silu(c_ref[...]).astype(BF16)
    o_ref[...] = jnp.dot(a, w_ref[...].astype(BF16), preferred_element_type=F32) + b_ref[...]


def modulation(cond, w, b, tn=1024):
    m, k = cond.shape
    n = w.shape[1]
    return pl.pallas_call(
        _mod_body, grid=(n // tn,),
        in_specs=[pl.BlockSpec((m, k), lambda j: (0, 0)),
                  pl.BlockSpec((k, tn), lambda j: (0, j)),
                  pl.BlockSpec((1, tn), lambda j: (0, j))],
        out_specs=pl.BlockSpec((m, tn), lambda j: (0, j)),
        out_shape=jax.ShapeDtypeStruct((m, n), F32),
        compiler_params=_params("arbitrary"), name="modulation",
    )(cond, w, b.reshape(1, n))


def _mod_row(tile, tile_rows, n_lat, lat_len, n_batch):
    start = tile * tile_rows
    return jnp.where(start < n_lat, start // lat_len, n_batch)


def _pack_bf16_pairs(h):
    half = h.shape[1] // 2
    bits = lax.bitcast_convert_type(h.astype(BF16).astype(F32), jnp.uint32)
    return (bits[:, :half] & jnp.uint32(0xFFFF0000)) | (bits[:, half:] >> jnp.uint32(16))


def _unpack_bf16_pairs(pk):
    hi = lax.bitcast_convert_type(pk & jnp.uint32(0xFFFF0000), F32).astype(BF16)
    lo = lax.bitcast_convert_type(pk << jnp.uint32(16), F32).astype(BF16)
    return hi, lo


def _store_slabs(ref, x):
    r, w = x.shape
    c_n = w // LANES
    for c in range(c_n):
        ref[pl.ds(c, r, stride=c_n), :] = x[:, c * LANES:(c + 1) * LANES]


def _load_slabs(ref, row0, r, c_n):
    return jnp.concatenate([ref[pl.ds(row0 * c_n + c, r, stride=c_n), :] for c in range(c_n)], axis=1)


def _adaln_body(x_ref, nw_ref, shift_ref, scale_ref, o_ref):
    x = x_ref[...]
    y = x * lax.rsqrt(jnp.mean(x * x, axis=-1, keepdims=True) + NORM_EPS) * nw_ref[...]
    o_ref[...] = (y * (1.0 + scale_ref[0]) + shift_ref[0]).astype(o_ref.dtype)


def _adaln_router_body(x_ref, nw_ref, shift_ref, scale_ref, rw_ref, o_ref, lg_ref):
    x = x_ref[...]
    y = x * lax.rsqrt(jnp.mean(x * x, axis=-1, keepdims=True) + NORM_EPS) * nw_ref[...]
    h = y * (1.0 + scale_ref[0]) + shift_ref[0]
    _store_slabs(o_ref, _pack_bf16_pairs(h))
    lg_ref[...] = lax.dot_general(rw_ref[...], h, (((1,), (1,)), ((), ())), preferred_element_type=F32,
                                  precision=lax.Precision.HIGHEST)


def adaln(x, norm_w, mod3, shift_idx, geom, router_w=None, tl=256):
    t, d = x.shape
    n_lat, lat_len, n_batch = geom
    row = functools.partial(_mod_row, tile_rows=tl, n_lat=n_lat, lat_len=lat_len, n_batch=n_batch)
    in_specs = [pl.BlockSpec((tl, d), lambda i: (i, 0)),
                pl.BlockSpec((1, d), lambda i: (0, 0)),
                pl.BlockSpec((1, 1, d), lambda i: (row(i) * 6 + shift_idx, 0, 0)),
                pl.BlockSpec((1, 1, d), lambda i: (row(i) * 6 + shift_idx + 1, 0, 0))]
    args = [x, norm_w.reshape(1, d), mod3, mod3]
    if router_w is None:
        return pl.pallas_call(
            _adaln_body, grid=(t // tl,), in_specs=in_specs,
            out_specs=pl.BlockSpec((tl, d), lambda i: (i, 0)),
            out_shape=jax.ShapeDtypeStruct((t, d), BF16),
            compiler_params=_params("arbitrary"), name="adaln")(*args)
    e = router_w.shape[1]
    return pl.pallas_call(
        _adaln_router_body, grid=(t // tl,),
        in_specs=in_specs + [pl.BlockSpec((e, d), lambda i: (0, 0))],
        out_specs=[pl.BlockSpec((tl * (d // 2 // LANES), LANES), lambda i: (i, 0)),
                   pl.BlockSpec((e, tl), lambda i: (0, i))],
        out_shape=[jax.ShapeDtypeStruct((t * (d // 2 // LANES), LANES), jnp.uint32),
                   jax.ShapeDtypeStruct((e, t), F32)],
        compiler_params=_params("arbitrary"), name="adaln_router")(*args, router_w.T)


def _mm_body(a_ref, w_ref, o_ref, wb_ref):
    @pl.when(pl.program_id(1) == 0)
    def _():
        wb_ref[...] = w_ref[...].astype(BF16)
    o_ref[...] = jnp.dot(a_ref[...], wb_ref[...], preferred_element_type=F32).astype(o_ref.dtype)


def _mm_res_body(a_ref, w_ref, res_ref, gate_ref, o_ref, wb_ref):
    @pl.when(pl.program_id(1) == 0)
    def _():
        wb_ref[...] = w_ref[...].astype(BF16)
    acc = jnp.dot(a_ref[...], wb_ref[...], preferred_element_type=F32)
    o_ref[...] = res_ref[...] + gate_ref[0] * acc


def matmul(a, w, n_cols=None, col_block0=0, tm=MM_TILE_M, tn=MM_TILE_N, out_dtype=F32):
    m, k = a.shape
    n = w.shape[1] if n_cols is None else n_cols
    return pl.pallas_call(
        _mm_body, grid=(n // tn, m // tm),
        in_specs=[pl.BlockSpec((tm, k), lambda j, i: (i, 0)),
                  pl.BlockSpec((k, tn), lambda j, i: (0, j + col_block0))],
        out_specs=pl.BlockSpec((tm, tn), lambda j, i: (i, j)),
        out_shape=jax.ShapeDtypeStruct((m, n), out_dtype),
        scratch_shapes=[pltpu.VMEM((k, tn), BF16)],
        compiler_params=_params("arbitrary", "arbitrary"), name="matmul",
    )(a, w)


def matmul_residual(a, w, res, mod3, gate_idx, geom, tm=MM_TILE_M, tn=MM_TILE_N):
    m, k = a.shape
    n = w.shape[1]
    n_lat, lat_len, n_batch = geom
    row = functools.partial(_mod_row, tile_rows=tm, n_lat=n_lat, lat_len=lat_len, n_batch=n_batch)
    nb = n // tn
    return pl.pallas_call(
        _mm_res_body, grid=(n // tn, m // tm),
        in_specs=[pl.BlockSpec((tm, k), lambda j, i: (i, 0)),
                  pl.BlockSpec((k, tn), lambda j, i: (0, j)),
                  pl.BlockSpec((tm, tn), lambda j, i: (i, j)),
                  pl.BlockSpec((1, 1, tn), lambda j, i: (row(i) * 6 + gate_idx, 0, j))],
        out_specs=pl.BlockSpec((tm, tn), lambda j, i: (i, j)),
        out_shape=jax.ShapeDtypeStruct((m, n), F32),
        scratch_shapes=[pltpu.VMEM((k, tn), BF16)],
        compiler_params=_params("arbitrary", "arbitrary"), name="matmul_residual",
    )(a, w, res, mod3)


def _expert_body(blk_e_ref, blk_new_ref, next_e_ref, stage_ref, n_used_ref, x_ref, gu_hbm, dn_hbm, o_ref,
                 gu_stage, dn_stage, gub_ref, dnb_ref, sem):
    i = pl.program_id(0)

    def weight_copies(e, slot):
        return (pltpu.make_async_copy(gu_hbm.at[e], gu_stage.at[slot], sem.at[0, slot]),
                pltpu.make_async_copy(dn_hbm.at[e], dn_stage.at[slot], sem.at[1, slot]))

    @pl.when(i == 0)
    def _():
        for cp in weight_copies(blk_e_ref[0], 0):
            cp.start()

    @pl.when(blk_new_ref[i] == 1)
    def _():
        slot = stage_ref[i]
        for cp in weight_copies(blk_e_ref[i], slot):
            cp.wait()
        gub_ref[...] = gu_stage[slot].astype(BF16)
        dnb_ref[...] = dn_stage[slot].astype(BF16)

        @pl.when(next_e_ref[i] >= 0)
        def _():
            for cp in weight_copies(next_e_ref[i], 1 - slot):
                cp.start()

    @pl.when(i < n_used_ref[0])
    def _():
        ff, d = dnb_ref.shape
        half = d // 2
        blk = x_ref.shape[0] // (half // LANES)
        x_hi, x_lo = _unpack_bf16_pairs(_load_slabs(x_ref, 0, blk, half // LANES))
        h1 = (jnp.dot(x_hi, gub_ref[:half], preferred_element_type=F32)
              + jnp.dot(x_lo, gub_ref[half:], preferred_element_type=F32))
        act = (_silu(h1[:, :ff]) * h1[:, ff:]).astype(BF16)
        _store_slabs(o_ref, jnp.dot(act, dnb_ref[...], preferred_element_type=F32))

    @pl.when(i >= n_used_ref[0])
    def _():
        o_ref[...] = jnp.zeros_like(o_ref)


def expert_blocks(x, gate_up, down, blk_e, n_used, blk):
    _, d, f2 = gate_up.shape
    xc, yc = d // 2 // LANES, d // LANES
    s = x.shape[0] // xc
    n_blk = s // blk
    pos = jnp.arange(n_blk, dtype=jnp.int32)
    used = pos < n_used[0]
    blk_new = (jnp.concatenate([jnp.ones((1,), bool), blk_e[1:] != blk_e[:-1]]) & used).astype(jnp.int32)
    stage = (jnp.cumsum(blk_new) - 1) % 2
    first_pos = jnp.where(blk_new == 1, pos, n_blk)
    next_first = jnp.concatenate([lax.cummin(first_pos, reverse=True)[1:], jnp.full((1,), n_blk, jnp.int32)])
    next_e = jnp.where(next_first < n_blk, blk_e[jnp.minimum(next_first, n_blk - 1)], -1).astype(jnp.int32)
    grid_spec = pltpu.PrefetchScalarGridSpec(
        num_scalar_prefetch=5, grid=(n_blk,),
        in_specs=[pl.BlockSpec((blk * xc, LANES), lambda i, be, bn, ne, st, nu: (jnp.minimum(i, nu[0] - 1), 0)),
                  pl.BlockSpec(memory_space=pl.ANY),
                  pl.BlockSpec(memory_space=pl.ANY)],
        out_specs=pl.BlockSpec((blk * yc, LANES), lambda i, be, bn, ne, st, nu: (i, 0)),
        scratch_shapes=[pltpu.VMEM((2, d, f2), F32), pltpu.VMEM((2, f2 // 2, d), F32),
                        pltpu.VMEM((d, f2), BF16), pltpu.VMEM((f2 // 2, d), BF16),
                        pltpu.SemaphoreType.DMA((2, 2))])
    return pl.pallas_call(
        _expert_body, grid_spec=grid_spec,
        out_shape=jax.ShapeDtypeStruct((s * yc, LANES), F32),
        compiler_params=_params("arbitrary"), name="expert_blocks",
    )(blk_e, blk_new, next_e, stage.astype(jnp.int32), n_used, x, gate_up, down)


def _norm_rope(x, w, cos, sin, first, head_dim):
    y = x * lax.rsqrt(jnp.mean(x * x, axis=-1, keepdims=True) + NORM_EPS) * w
    swapped = jnp.where(first, pltpu.roll(y, head_dim - head_dim // 4, 1), pltpu.roll(y, head_dim // 4, 1))
    return y * cos + swapped * sin


def _mm_qkv_body(a_ref, w_ref, qw_ref, kw_ref, cos_ref, sin_ref, o_ref, wb_ref, *, head_dim, n_q_tiles, n_k_tiles):
    j = pl.program_id(0)

    @pl.when(pl.program_id(1) == 0)
    def _():
        wb_ref[...] = w_ref[...].astype(BF16)

    tm, tn = o_ref.shape
    row_chunk = tm // MM_EPILOGUE_CHUNKS

    def rows_chain(rows, nw_ref):
        acc = jnp.dot(a_ref[rows, :], wb_ref[...], preferred_element_type=F32)
        yield
        if nw_ref is None:
            o_ref[rows, :] = acc.astype(o_ref.dtype)
            return
        cos, sin = cos_ref[rows, :], sin_ref[rows, :]
        lane = lax.broadcasted_iota(jnp.int32, cos.shape, 1)
        first = (lane % (head_dim // 2)) < (head_dim // 4)
        for g in range(tn // head_dim):
            sl = slice(g * head_dim, (g + 1) * head_dim)
            o_ref[rows, sl] = _norm_rope(acc[:, sl], nw_ref[...], cos, sin, first, head_dim).astype(o_ref.dtype)
            yield

    def tile(nw_ref):
        _run_interleaved(rows_chain(slice(r, r + row_chunk), nw_ref) for r in range(0, tm, row_chunk))

    @pl.when(j < n_q_tiles)
    def _():
        tile(qw_ref)

    @pl.when((j >= n_q_tiles) & (j < n_q_tiles + n_k_tiles))
    def _():
        tile(kw_ref)

    @pl.when(j >= n_q_tiles + n_k_tiles)
    def _():
        tile(None)


def matmul_qkv(a, w, q_norm, k_norm, cos_t, sin_t, tm=MM_TILE_M, tn=MM_TILE_N):
    m, k = a.shape
    n = w.shape[1]
    hd = q_norm.shape[0]
    body = functools.partial(_mm_qkv_body, head_dim=hd, n_q_tiles=n // 3 // tn, n_k_tiles=n // 3 // tn)
    return pl.pallas_call(
        body, grid=(n // tn, m // tm),
        in_specs=[pl.BlockSpec((tm, k), lambda j, i: (i, 0)),
                  pl.BlockSpec((k, tn), lambda j, i: (0, j)),
                  pl.BlockSpec((1, hd), lambda j, i: (0, 0)),
                  pl.BlockSpec((1, hd), lambda j, i: (0, 0)),
                  pl.BlockSpec((tm, hd), lambda j, i: (i, 0)),
                  pl.BlockSpec((tm, hd), lambda j, i: (i, 0))],
        out_specs=pl.BlockSpec((tm, tn), lambda j, i: (i, j)),
        out_shape=jax.ShapeDtypeStruct((m, n), BF16),
        scratch_shapes=[pltpu.VMEM((k, tn), BF16)],
        compiler_params=_params("arbitrary", "arbitrary"), name="matmul_qkv",
    )(a, w, q_norm.reshape(1, hd), k_norm.reshape(1, hd), cos_t, sin_t)


def rope_tables(n_lat_tokens_per_sample, n_batch, n_ctx_tokens, head_dim):
    quarter = head_dim // 4
    inv_freq = ROPE_BASE ** (-jnp.arange(quarter, dtype=F32) / quarter)
    rows = n_lat_tokens_per_sample // GRID_W
    row = jnp.repeat(jnp.arange(rows, dtype=F32), GRID_W)
    col = jnp.tile(jnp.arange(GRID_W, dtype=F32), rows)
    ang_r = row[:, None] * inv_freq[None, :]
    ang_c = col[:, None] * inv_freq[None, :]
    cos = jnp.concatenate([jnp.cos(ang_r), jnp.cos(ang_r), jnp.cos(ang_c), jnp.cos(ang_c)], axis=-1)
    sin = jnp.concatenate([-jnp.sin(ang_r), jnp.sin(ang_r), -jnp.sin(ang_c), jnp.sin(ang_c)], axis=-1)
    cos = jnp.concatenate([jnp.tile(cos, (n_batch, 1)), jnp.ones((n_ctx_tokens, head_dim), F32)], axis=0)
    sin = jnp.concatenate([jnp.tile(sin, (n_batch, 1)), jnp.zeros((n_ctx_tokens, head_dim), F32)], axis=0)
    return cos, sin


def _diff_attn_rows(lam, q_ref, k_all, v_all, sw_ref, o_ref, rows, head_dim, out_scale):
    c = head_dim ** -0.5 * math.log2(math.e)
    es, invs = [], []
    for s in range(2):
        sl = slice(s * head_dim, (s + 1) * head_dim)
        sc = lax.dot_general(q_ref[rows, sl], k_all[:, sl], (((1,), (1,)), ((), ())), preferred_element_type=F32)
        yield
        e = jnp.exp2((sc - jnp.max(sc, axis=-1, keepdims=True)) * c)
        invs.append(1.0 / jnp.sum(e, axis=-1, keepdims=True))
        es.append(e.astype(BF16))
        yield
    v = v_all[...]
    o0 = jnp.dot(es[0], v, preferred_element_type=F32)
    o1 = jnp.dot(es[1], v, preferred_element_type=F32)
    yield
    o = o0 * invs[0] - (lam * invs[1]) * o1
    y = o * lax.rsqrt(jnp.mean(o * o, axis=-1, keepdims=True) + NORM_EPS) * sw_ref[...]
    o_ref[rows, :] = (y * out_scale).astype(o_ref.dtype)
    yield


def _diff_attn_body(lam_ref, q_ref, kl_ref, kc_ref, vl_ref, vc_ref, sw_ref, o_ref, k_all, v_all, *,
                    head_dim, out_scale, row_chunk):
    @pl.when(pl.program_id(2) == 0)
    def _():
        n_l = kl_ref.shape[0]
        k_all[:n_l] = kl_ref[...]
        k_all[n_l:] = kc_ref[...]
        v_all[:n_l] = vl_ref[...]
        v_all[n_l:] = vc_ref[...]

    lam = lam_ref[0]
    tq = q_ref.shape[0]
    _run_interleaved(
        _diff_attn_rows(lam, q_ref, k_all, v_all, sw_ref, o_ref, slice(r, r + row_chunk), head_dim, out_scale)
        for r in range(0, tq, row_chunk))


def diff_attention(qkv, lmbda, sub_norm, n_batch, lat_len, ctx_len, n_heads, head_dim, out_scale,
                   tq=ATTN_TILE_Q, row_chunk=ATTN_ROW_CHUNK):
    hw = 2 * head_dim
    nq = lat_len // tq
    ctx_blk0 = n_batch * lat_len // ctx_len
    body = functools.partial(_diff_attn_body, head_dim=head_dim, out_scale=out_scale, row_chunk=row_chunk)
    n_keys = lat_len + ctx_len
    return pl.pallas_call(
        body, grid=(n_batch, n_heads, nq),
        in_specs=[pl.BlockSpec(memory_space=pltpu.SMEM),
                  pl.BlockSpec((tq, hw), lambda b, h, i: (b * nq + i, h)),
                  pl.BlockSpec((lat_len, hw), lambda b, h, i: (b, n_heads + h)),
                  pl.BlockSpec((ctx_len, hw), lambda b, h, i: (ctx_blk0 + b, n_heads + h)),
                  pl.BlockSpec((lat_len, hw), lambda b, h, i: (b, 2 * n_heads + h)),
                  pl.BlockSpec((ctx_len, hw), lambda b, h, i: (ctx_blk0 + b, 2 * n_heads + h)),
                  pl.BlockSpec((1, hw), lambda b, h, i: (0, 0))],
        out_specs=pl.BlockSpec((tq, hw), lambda b, h, i: (b * nq + i, h)),
        out_shape=jax.ShapeDtypeStruct((n_batch * lat_len, n_heads * hw), BF16),
        scratch_shapes=[pltpu.VMEM((n_keys, hw), BF16), pltpu.VMEM((n_keys, hw), BF16)],
        compiler_params=_params("arbitrary", "arbitrary", "arbitrary"), name="diff_attention",
    )(lmbda.reshape(1), qkv, qkv, qkv, qkv, qkv, sub_norm.reshape(1, hw))


SCAN_TILE = 256
CHUNK = 64
SUB = 16
NEG_BIG = -1e30
NEG_INF = float("-inf")
DELTA_INV_PASSES = 1
DELTA_HEADS_PER_STEP = 4
HGRN_HEADS_PER_STEP = 2


def _sigmoid(x):
    return 1.0 / (1.0 + jnp.exp(-x))


def _dot(a, b):
    return jnp.dot(a.astype(BF16), b.astype(BF16), preferred_element_type=F32)


def _dot_nt(a, b):
    return lax.dot_general(a.astype(BF16), b.astype(BF16), (((1,), (1,)), ((), ())),
                           preferred_element_type=F32)


def _split3(x):
    hi = x.astype(BF16)
    r = x - hi.astype(F32)
    mid = r.astype(BF16)
    lo = (r - mid.astype(F32)).astype(BF16)
    return hi, mid, lo


def _dot_exact_lhs01(m01, x):
    hi, mid, lo = _split3(x)
    m = m01.astype(BF16)
    return (jnp.dot(m, hi, preferred_element_type=F32) + jnp.dot(m, mid, preferred_element_type=F32)
            + jnp.dot(m, lo, preferred_element_type=F32))


def _dot_exact_rhs01(x, m01):
    hi, mid, lo = _split3(x)
    m = m01.astype(BF16)
    return (jnp.dot(hi, m, preferred_element_type=F32) + jnp.dot(mid, m, preferred_element_type=F32)
            + jnp.dot(lo, m, preferred_element_type=F32))


def _dot3(a, b):
    ah = a.astype(BF16)
    al = (a - ah.astype(F32)).astype(BF16)
    bh = b.astype(BF16)
    bl = (b - bh.astype(F32)).astype(BF16)
    return (jnp.dot(ah, bh, preferred_element_type=F32) + jnp.dot(ah, bl, preferred_element_type=F32)
            + jnp.dot(al, bh, preferred_element_type=F32))


def _tile_masks(n, reverse):
    i = lax.broadcasted_iota(jnp.int32, (n, n), 0)
    j = lax.broadcasted_iota(jnp.int32, (n, n), 1)
    same = (i // CHUNK) == (j // CHUNK)
    if reverse:
        return same & (i <= j), same & (i < j)
    return same & (i >= j), same & (i > j)


def _segment_tile(b, s, reverse, nl, nc, ctx_tile0):
    if reverse:
        return jnp.where(s < nc, ctx_tile0 + b * nc + (nc - 1 - s), b * nl + (nl - 1 - (s - nc)))
    return jnp.where(s < nc, ctx_tile0 + b * nc + s, b * nl + (s - nc))


def _dn_prep_body(x_ref, prev_ref, next_ref, w_ref, o_ref, *, tiles_per_lat_seg, tiles_per_ctx_seg, n_lat_tiles,
                  head_dim, q_scale):
    i = pl.program_id(0)
    j = pl.program_id(1)
    is_lat = i < n_lat_tiles
    pos = jnp.where(is_lat, i % tiles_per_lat_seg, (i - n_lat_tiles) % tiles_per_ctx_seg)
    seg_first = pos == 0
    seg_last = pos == jnp.where(is_lat, tiles_per_lat_seg, tiles_per_ctx_seg) - 1
    x = x_ref[...]
    tl = x.shape[0]
    prev = jnp.where(seg_first, 0.0, prev_ref[...])
    nxt = jnp.where(seg_last, 0.0, next_ref[...])
    xp = jnp.concatenate([prev, x, nxt], axis=0)
    w = w_ref[...]
    n_taps = 5
    acc = None
    for t in range(n_taps):
        off = 8 + t - n_taps // 2
        term = xp[off:off + tl] * w[t:t + 1]
        acc = term if acc is None else acc + term
    y = _silu(acc)
    scale = jnp.where(j == 0, q_scale, 1.0)
    outs = []
    for h in range(y.shape[1] // head_dim):
        yh = y[:, h * head_dim:(h + 1) * head_dim]
        nrm = lax.rsqrt(jnp.sum(yh * yh, axis=-1, keepdims=True) + 1e-6) * scale
        outs.append(yh * jnp.where(j == 2, 1.0, nrm))
    o_ref[...] = jnp.concatenate(outs, axis=1)


def dn_prep(proj_dn, conv_w_t, geom, ctx_len, width, head_dim):
    n_lat, lat_len, n_batch = geom
    t = proj_dn.shape[0]
    tl = SCAN_TILE
    rows8 = tl // 8
    n_tiles = t // tl
    body = functools.partial(_dn_prep_body, tiles_per_lat_seg=lat_len // tl, tiles_per_ctx_seg=ctx_len // tl,
                             n_lat_tiles=n_lat // tl,
                             head_dim=head_dim, q_scale=head_dim ** -0.5)
    last8 = t // 8 - 1
    return pl.pallas_call(
        body, grid=(n_tiles, 3),
        in_specs=[pl.BlockSpec((tl, width), lambda i, j: (i, j)),
                  pl.BlockSpec((8, width), lambda i, j: (jnp.maximum(i * rows8 - 1, 0), j)),
                  pl.BlockSpec((8, width), lambda i, j: (jnp.minimum((i + 1) * rows8, last8), j)),
                  pl.BlockSpec((8, width), lambda i, j: (0, j))],
        out_specs=pl.BlockSpec((tl, width), lambda i, j: (i, j)),
        out_shape=jax.ShapeDtypeStruct((t, 3 * width), F32),
        compiler_params=_params("arbitrary", "arbitrary"), name="dn_prep",
    )(proj_dn, proj_dn, proj_dn, conv_w_t)


def _softplus(x):
    return jnp.maximum(x, 0.0) + jnp.log(1.0 + jnp.exp(-jnp.abs(x)))


def _gate_prep_body(h_ref, wc_ref, wr_ref, alog_c_ref, dtb_c_ref, alog_r_ref, dtb_r_ref, gc_ref, gr_ref, *, n_heads):
    h = h_ref[...]
    tl = h.shape[0]
    nd = 2 * n_heads
    raw_c = jnp.dot(h, wc_ref[...].astype(BF16), preferred_element_type=F32)
    raw_r = lax.dot_general(wr_ref[...].astype(BF16), h, (((1,), (1,)), ((), ())),
                            preferred_element_type=F32)
    g_c = -jnp.exp(alog_c_ref[...]) * _softplus(raw_c[:, :nd] + dtb_c_ref[...])
    g_r = -jnp.exp(alog_r_ref[...]) * _softplus(raw_r[:nd, :] + dtb_r_ref[...])
    incl_f, _ = _tile_masks(tl, False)
    incl_b, _ = _tile_masks(tl, True)
    one_f = jnp.where(incl_f, 1.0, 0.0)
    one_b = jnp.where(incl_b, 1.0, 0.0)
    cum_c = jnp.concatenate([_dot_exact_lhs01(one_f, g_c[:, :n_heads]),
                             _dot_exact_lhs01(one_b, g_c[:, n_heads:])], axis=1)
    cum_r = jnp.concatenate([_dot_exact_rhs01(g_r[:n_heads, :], one_b),
                             _dot_exact_rhs01(g_r[n_heads:, :], one_f)], axis=0)
    beta_c = _sigmoid(raw_c[:, nd:])
    gc_ref[...] = jnp.concatenate([cum_c, beta_c], axis=1)
    gr_ref[...] = jnp.concatenate([cum_r, jnp.zeros_like(cum_r)], axis=0)


def gate_prep(h_bf, w_gate, a_log, dt_bias, n_heads):
    t, d = h_bf.shape
    tl = SCAN_TILE
    nd = 2 * n_heads
    body = functools.partial(_gate_prep_body, n_heads=n_heads)
    full = lambda shape: pl.BlockSpec(shape, lambda i: (0, 0))
    return pl.pallas_call(
        body, grid=(t // tl,),
        in_specs=[pl.BlockSpec((tl, d), lambda i: (i, 0)), full((d, 2 * nd)), full((2 * nd, d)),
                  full((1, nd)), full((1, nd)), full((nd, 1)), full((nd, 1))],
        out_specs=[pl.BlockSpec((tl, 2 * nd), lambda i: (i, 0)), pl.BlockSpec((2 * nd, tl), lambda i: (0, i))],
        out_shape=[jax.ShapeDtypeStruct((t, 2 * nd), F32), jax.ShapeDtypeStruct((2 * nd, t), F32)],
        compiler_params=_params("arbitrary"), name="gate_prep",
    )(h_bf, w_gate, w_gate.T, a_log.reshape(1, nd), dt_bias.reshape(1, nd),
      a_log.reshape(nd, 1), dt_bias.reshape(nd, 1))


def _select_col(x, idx):
    lane = lax.broadcasted_iota(jnp.int32, x.shape, 1)
    return jnp.sum(jnp.where(lane == idx, x, 0.0), axis=1, keepdims=True)


def _select_row(x, idx):
    row = lax.broadcasted_iota(jnp.int32, x.shape, 0)
    return jnp.sum(jnp.where(row == idx, x, 0.0), axis=0, keepdims=True)


def _run_interleaved(chains):
    chains = list(chains)
    while chains:
        alive = []
        for ch in chains:
            try:
                next(ch)
                alive.append(ch)
            except StopIteration:
                pass
        chains = alive


def _delta_chain(q, k, v, gc_col, gc_row, beta_col, s_ref, o_ref, cols, reverse, inv_passes):
    tl, kd_ = k.shape
    n_chunks = tl // CHUNK
    incl, strict = _tile_masks(tl, reverse)
    decay = jnp.exp(jnp.where(incl, gc_col - gc_row, NEG_BIG))
    kb, qb = k.astype(BF16), q.astype(BF16)
    kkt = _dot_nt(kb, kb)
    qkt = _dot_nt(qb, kb)
    yield
    x = jnp.where(strict, kkt * (-beta_col) * decay, 0.0)
    dot_inv = _dot3 if inv_passes == 3 else _dot
    ri = lax.broadcasted_iota(jnp.int32, (tl, tl), 0)
    ci = lax.broadcasted_iota(jnp.int32, (tl, tl), 1)
    r = jnp.where(ri == ci, 1.0, 0.0) + x
    n_sq = int(math.log2(CHUNK)) - 1
    for _ in range(n_sq):
        x = dot_inv(x, x)
        r = r + dot_inv(r, x)
        yield
    e_g = jnp.exp(gc_col)
    rhs = jnp.concatenate([v * beta_col, k * (beta_col * e_g)], axis=1)
    sol = dot_inv(r, rhs)
    yield
    u0, w = sol[:, :v.shape[1]], sol[:, v.shape[1]:]
    attn = (qkt * decay).astype(BF16)
    o0 = _dot(attn, u0)
    qe = q * e_g - _dot(attn, w)
    tot_rows = []
    for c in range(n_chunks):
        last = c * CHUNK if reverse else c * CHUNK + CHUNK - 1
        tot_rows.append(jnp.broadcast_to(gc_col[last:last + 1, :], (CHUNK, 1)))
    tot = jnp.concatenate(tot_rows, axis=0)
    kdec_t = jnp.transpose(k * jnp.exp(tot - gc_col))
    wu = jnp.concatenate([-w, u0], axis=1).astype(BF16)
    lane = lax.broadcasted_iota(jnp.int32, kdec_t.shape, 1)
    pns = [_dot(jnp.where((lane // CHUNK) == c, kdec_t, 0.0), wu) for c in range(n_chunks)]
    yield
    order = range(n_chunks - 1, -1, -1) if reverse else range(n_chunks)
    for c in order:
        rows = slice(c * CHUNK, (c + 1) * CHUNK)
        pn = pns[c]
        lhs = jnp.concatenate([qe[rows], pn[:, :kd_]], axis=0)
        s = s_ref[...]
        res = _dot(lhs, s)
        o_ref[rows, cols] = o0[rows] + res[:CHUNK]
        last = c * CHUNK if reverse else c * CHUNK + CHUNK - 1
        gl = jnp.exp(gc_col[last:last + 1, :])
        s_ref[...] = gl * s + res[CHUNK:] + pn[:, kd_:]
        yield


def _delta_body(qf_ref, kf_ref, vf_ref, gcf_ref, grf_ref, qb_ref, kb_ref, vb_ref, gcb_ref, grb_ref,
                of_ref, ob_ref, sf_ref, sb_ref, *, n_heads, head_dim, inv_passes):
    hg = pl.program_id(1)
    heads_per_step = sf_ref.shape[0]

    @pl.when(pl.program_id(2) == 0)
    def _():
        sf_ref[...] = jnp.zeros_like(sf_ref)
        sb_ref[...] = jnp.zeros_like(sb_ref)

    chains = []
    for g in range(heads_per_step):
        cols = slice(g * head_dim, (g + 1) * head_dim)
        for reverse, (q_ref, k_ref, v_ref, gc_ref, gr_ref, o_ref, s_ref) in enumerate(
                [(qf_ref, kf_ref, vf_ref, gcf_ref, grf_ref, of_ref, sf_ref),
                 (qb_ref, kb_ref, vb_ref, gcb_ref, grb_ref, ob_ref, sb_ref)]):
            idx = reverse * n_heads + hg * heads_per_step + g
            gcs = gc_ref[...]
            gc_col = _select_col(gcs, idx)
            beta_col = _select_col(gcs, 2 * n_heads + idx)
            gc_row = _select_row(gr_ref[...], idx)
            chains.append(_delta_chain(q_ref[:, cols], k_ref[:, cols], v_ref[:, cols], gc_col, gc_row, beta_col,
                                       s_ref.at[g], o_ref, cols, bool(reverse), inv_passes))
    _run_interleaved(chains)


def delta_scan(qkv, gc, gr, geom, ctx_len, n_heads, head_dim, inv_passes=3, heads_per_step=DELTA_HEADS_PER_STEP):
    n_lat, lat_len, n_batch = geom
    t = qkv.shape[0]
    tl = SCAN_TILE
    nl, nc, ctx0 = lat_len // tl, ctx_len // tl, n_lat // tl
    n_hg = n_heads // heads_per_step
    gw = heads_per_step * head_dim
    tile = functools.partial(_segment_tile, nl=nl, nc=nc, ctx_tile0=ctx0)
    specs = []
    for reverse in (False, True):
        tix = functools.partial(tile, reverse=reverse)
        specs += [pl.BlockSpec((tl, gw), lambda b, h, s, tix=tix: (tix(b, s), h)),
                  pl.BlockSpec((tl, gw), lambda b, h, s, tix=tix: (tix(b, s), n_hg + h)),
                  pl.BlockSpec((tl, gw), lambda b, h, s, tix=tix: (tix(b, s), 2 * n_hg + h)),
                  pl.BlockSpec((tl, 4 * n_heads), lambda b, h, s, tix=tix: (tix(b, s), 0)),
                  pl.BlockSpec((4 * n_heads, tl), lambda b, h, s, tix=tix: (0, tix(b, s)))]
    out_specs = [pl.BlockSpec((tl, gw), lambda b, h, s, tix=functools.partial(tile, reverse=r): (tix(b, s), h))
                 for r in (False, True)]
    body = functools.partial(_delta_body, n_heads=n_heads, head_dim=head_dim, inv_passes=inv_passes)
    return pl.pallas_call(
        body, grid=(n_batch, n_hg, nl + nc), in_specs=specs, out_specs=out_specs,
        out_shape=[jax.ShapeDtypeStruct((t, n_heads * head_dim), F32)] * 2,
        scratch_shapes=[pltpu.VMEM((heads_per_step, head_dim, head_dim), F32)] * 2,
        compiler_params=_params("arbitrary", "arbitrary", "arbitrary"), name="delta_scan",
    )(qkv, qkv, qkv, gc, gr, qkv, qkv, qkv, gc, gr)


def _hgrn_diag(q, k, cum, v, stage_ref, diag_ref, reverse):
    n_sub = q.shape[0] // SUB
    for n, val in enumerate((q, k, cum, v)):
        stage_ref[n] = val

    def slabs(n):
        return [stage_ref.at[n][pl.ds(r, n_sub, stride=SUB), :] for r in range(SUB)]

    q_x, k_x, p_x, v_x = slabs(0), slabs(1), slabs(2), slabs(3)
    pairs = [(i, j) for i in range(SUB) for j in range(SUB) if (j >= i if reverse else j <= i)]
    terms = []
    for i, j in pairs:
        qk = q_x[i] * k_x[j]
        terms.append((qk if i == j else qk * jnp.exp(p_x[i] - p_x[j])).astype(BF16))
    kd_ = terms[0].shape[1]
    a_rep = jnp.dot(jnp.concatenate(terms, axis=0), jnp.ones((kd_, kd_), BF16), preferred_element_type=F32)
    o_x = [None] * SUB
    for n, (i, j) in enumerate(pairs):
        contrib = a_rep[n * n_sub:(n + 1) * n_sub] * v_x[j]
        o_x[i] = contrib if o_x[i] is None else o_x[i] + contrib
    for r in range(SUB):
        diag_ref[pl.ds(r, n_sub, stride=SUB), :] = o_x[r]


def _hgrn_chain(hq_ref, hf_ref, hv_ref, cols, lb, st_ref, o_ref, stage_ref, diag_ref, reverse):
    hq, hf, hv = hq_ref[:, cols], hf_ref[:, cols], hv_ref[:, cols]
    tl, kd_ = hq.shape
    n_chunks = tl // CHUNK
    n_sub = tl // SUB
    sub_per_chunk = CHUNK // SUB
    q = _silu(hq)
    f = lb + (1.0 - lb) * _sigmoid(hf)
    k = 1.0 - f
    lf = jnp.log(f)
    incl, _ = _tile_masks(tl, reverse)
    cum = _dot_exact_lhs01(jnp.where(incl, 1.0, 0.0), lf)
    excl = cum - lf

    def bcast_rows(src, row, n):
        return jnp.broadcast_to(src[row:row + 1, :], (n, kd_))

    chunk_last = [(c * CHUNK if reverse else c * CHUNK + CHUNK - 1) for c in range(n_chunks)]
    tot = jnp.concatenate([bcast_rows(cum, chunk_last[c], CHUNK) for c in range(n_chunks)], axis=0)
    sub_first = [(m * SUB + SUB - 1 if reverse else m * SUB) for m in range(n_sub)]
    r_sub = jnp.concatenate([bcast_rows(excl, sub_first[m], SUB) for m in range(n_sub)], axis=0)
    q_t = q * jnp.exp(cum - r_sub)
    qd = q * jnp.exp(cum)
    kd = k * jnp.exp(tot - cum)
    vb = hv.astype(BF16)

    i = lax.broadcasted_iota(jnp.int32, (tl, tl), 0)
    j = lax.broadcasted_iota(jnp.int32, (tl, tl), 1)
    same = (i // CHUNK) == (j // CHUNK)
    pos_i = (i % CHUNK) // SUB
    pos_j = (j % CHUNK) // SUB
    if reverse:
        pos_i, pos_j = sub_per_chunk - 1 - pos_i, sub_per_chunk - 1 - pos_j
    a_off = jnp.zeros((tl, tl), F32)
    for lvl in range(1, sub_per_chunk):
        ref_rows = []
        for c in range(n_chunks):
            m = c * sub_per_chunk + (sub_per_chunk - 1 - lvl if reverse else lvl)
            ref_rows.append(bcast_rows(excl, sub_first[m], CHUNK))
        r_lvl = jnp.concatenate(ref_rows, axis=0)
        k_t = k * jnp.exp(jnp.minimum(r_lvl - cum, 0.0))
        a_l = _dot_nt(q_t, k_t)
        a_off = a_off + jnp.where(same & (pos_i == lvl) & (pos_j < lvl), a_l, 0.0)
        yield
    o_intra = _dot(a_off, vb)
    yield

    _hgrn_diag(q, k, cum, hv, stage_ref, diag_ref, reverse)
    yield
    o_intra = o_intra + diag_ref[...]

    v_t = jnp.transpose(hv)
    lane = lax.broadcasted_iota(jnp.int32, v_t.shape, 1)
    kdb = kd.astype(BF16)
    n_ts = [_dot(jnp.where((lane // CHUNK) == c, v_t, 0.0), kdb) for c in range(n_chunks)]
    yield
    order = range(n_chunks - 1, -1, -1) if reverse else range(n_chunks)
    for c in order:
        rows = slice(c * CHUNK, (c + 1) * CHUNK)
        st = st_ref[...]
        o_ref[rows, cols] = o_intra[rows] + _dot_nt(qd[rows], st)
        st_ref[...] = st * jnp.exp(cum[chunk_last[c]:chunk_last[c] + 1, :]) + n_ts[c]
        yield


def _hgrn_body(qf_ref, ff_ref, vf_ref, qb_ref, fb_ref, vb_ref, lb_ref, of_ref, ob_ref, sf_ref, sb_ref, stage_ref,
               diag_ref, *, key_dim):
    heads_per_step = sf_ref.shape[0]

    @pl.when(pl.program_id(2) == 0)
    def _():
        sf_ref[...] = jnp.zeros_like(sf_ref)
        sb_ref[...] = jnp.zeros_like(sb_ref)

    chains = []
    for g in range(heads_per_step):
        cols = slice(g * key_dim, (g + 1) * key_dim)
        lb = lb_ref[:, cols]
        chains.append(_hgrn_chain(qf_ref, ff_ref, vf_ref, cols, lb, sf_ref.at[g], of_ref,
                                  stage_ref.at[2 * g], diag_ref.at[2 * g], False))
        chains.append(_hgrn_chain(qb_ref, fb_ref, vb_ref, cols, lb, sb_ref.at[g], ob_ref,
                                  stage_ref.at[2 * g + 1], diag_ref.at[2 * g + 1], True))
    _run_interleaved(chains)


def hgrn_scan(proj_hg, lb, geom, ctx_len, n_heads, key_dim, heads_per_step=HGRN_HEADS_PER_STEP):
    n_lat, lat_len, n_batch = geom
    t = proj_hg.shape[0]
    tl = SCAN_TILE
    nl, nc, ctx0 = lat_len // tl, ctx_len // tl, n_lat // tl
    n_hg = n_heads // heads_per_step
    gw = heads_per_step * key_dim
    tile = functools.partial(_segment_tile, nl=nl, nc=nc, ctx_tile0=ctx0)
    specs = []
    for reverse in (False, True):
        tix = functools.partial(tile, reverse=reverse)
        fcol = (1 + int(reverse)) * n_hg
        specs += [pl.BlockSpec((tl, gw), lambda b, h, s, tix=tix: (tix(b, s), h)),
                  pl.BlockSpec((tl, gw), lambda b, h, s, tix=tix, fcol=fcol: (tix(b, s), fcol + h)),
                  pl.BlockSpec((tl, gw), lambda b, h, s, tix=tix: (tix(b, s), 3 * n_hg + h))]
    specs.append(pl.BlockSpec((1, gw), lambda b, h, s: (0, h)))
    out_specs = [pl.BlockSpec((tl, gw), lambda b, h, s, tix=functools.partial(tile, reverse=r): (tix(b, s), h))
                 for r in (False, True)]
    return pl.pallas_call(
        functools.partial(_hgrn_body, key_dim=key_dim), grid=(n_batch, n_hg, nl + nc),
        in_specs=specs, out_specs=out_specs,
        out_shape=[jax.ShapeDtypeStruct((t, n_heads * key_dim), F32)] * 2,
        scratch_shapes=[pltpu.VMEM((heads_per_step, key_dim, key_dim), F32)] * 2
        + [pltpu.VMEM((2 * heads_per_step, 4, tl, key_dim), F32), pltpu.VMEM((2 * heads_per_step, tl, key_dim), F32)],
        compiler_params=_params("arbitrary", "arbitrary", "arbitrary"), name="hgrn_scan",
    )(proj_hg, proj_hg, proj_hg, proj_hg, proj_hg, proj_hg, lb)


def _mix_out2_body(df_ref, db_ref, hf_ref, hb_ref, z_ref, og_ref, dnw_ref, hgw_ref, o_ref, *, head_dim):
    def normed(o, nw):
        outs = []
        for h in range(o.shape[1] // head_dim):
            oh = o[:, h * head_dim:(h + 1) * head_dim]
            outs.append(oh * lax.rsqrt(jnp.mean(oh * oh, axis=-1, keepdims=True) + NORM_EPS) * nw)
        return jnp.concatenate(outs, axis=1)

    dn = normed(df_ref[...] + db_ref[...], dnw_ref[...]) * _silu(z_ref[...])
    hg = normed(hf_ref[...] + hb_ref[...], hgw_ref[...]) * _sigmoid(og_ref[...])
    half = dn.shape[1]
    o_ref[:, :half] = dn.astype(o_ref.dtype)
    o_ref[:, half:] = hg.astype(o_ref.dtype)


def mix_out(dn_f, dn_b, hg_f, hg_b, z_src, z_blk, og_src, og_blk, dn_norm, hg_norm, head_dim):
    t, w = dn_f.shape
    tl = SCAN_TILE
    row = lambda i: (i, 0)
    body = functools.partial(_mix_out2_body, head_dim=head_dim)
    return pl.pallas_call(
        body, grid=(t // tl,),
        in_specs=[pl.BlockSpec((tl, w), row)] * 4
        + [pl.BlockSpec((tl, w), lambda i: (i, z_blk)), pl.BlockSpec((tl, w), lambda i: (i, og_blk)),
           pl.BlockSpec((1, head_dim), lambda i: (0, 0)), pl.BlockSpec((1, head_dim), lambda i: (0, 0))],
        out_specs=pl.BlockSpec((tl, 2 * w), row),
        out_shape=jax.ShapeDtypeStruct((t, 2 * w), BF16),
        compiler_params=_params("arbitrary"), name="mix_out",
    )(dn_f, dn_b, hg_f, hg_b, z_src, og_src, dn_norm.reshape(1, head_dim), hg_norm.reshape(1, head_dim))


def _first_max(x, ids, n):
    m = jnp.max(x, axis=0, keepdims=True)
    first = jnp.min(jnp.where(x == m, ids, n), axis=0, keepdims=True)
    return m, first


def _route_body(lg_ref, bias_ref, idx_ref, rank_ref, w_ref, cnt_ref, carry_ref, *,
                n_groups, topk_groups, top_k, scale):
    i = pl.program_id(0)

    @pl.when(i == 0)
    def _():
        carry_ref[...] = jnp.zeros_like(carry_ref)

    lg = lg_ref[...]
    n_exp, tl = lg.shape
    per = n_exp // n_groups
    scores = 1.0 / (1.0 + jnp.exp(-lg))
    biased = scores + bias_ref[...]
    sub = lax.broadcasted_iota(jnp.int32, (per, tl), 0)
    g_rows = []
    for g in range(n_groups):
        xg = biased[g * per:(g + 1) * per]
        m1, i1 = _first_max(xg, sub, per)
        m2 = jnp.max(jnp.where(sub == i1, NEG_INF, xg), axis=0, keepdims=True)
        g_rows.append(m1 + m2)
    gscore = jnp.concatenate(g_rows, axis=0)
    gid = lax.broadcasted_iota(jnp.int32, (n_groups, tl), 0)
    gsel = jnp.zeros((n_groups, tl), jnp.bool_)
    for _ in range(topk_groups):
        _, first = _first_max(gscore, gid, n_groups)
        hit = gid == first
        gsel = gsel | hit
        gscore = jnp.where(hit, NEG_INF, gscore)
    eid = lax.broadcasted_iota(jnp.int32, (n_exp, tl), 0)
    gmask = jnp.concatenate([jnp.broadcast_to(gsel[g:g + 1], (per, tl)) for g in range(n_groups)], axis=0)
    masked = jnp.where(gmask, biased, NEG_INF)
    sel = jnp.zeros((n_exp, tl), jnp.bool_)
    hits, firsts = [], []
    for _ in range(top_k):
        _, first = _first_max(masked, eid, n_exp)
        hit = eid == first
        hits.append(hit)
        firsts.append(first)
        sel = sel | hit
        masked = jnp.where(hit, NEG_INF, masked)
    self = jnp.where(sel, 1.0, 0.0)
    ti = lax.broadcasted_iota(jnp.int32, (tl, tl), 0)
    tj = lax.broadcasted_iota(jnp.int32, (tl, tl), 1)
    before = jnp.where(ti < tj, 1.0, 0.0).astype(BF16)
    carry = carry_ref[...]
    rank_full = jnp.dot(self.astype(BF16), before, preferred_element_type=F32) + carry
    w_rows = [jnp.sum(jnp.where(hit, scores, 0.0), axis=0, keepdims=True) for hit in hits]
    r_rows = [jnp.sum(jnp.where(hit, rank_full, 0.0), axis=0, keepdims=True) for hit in hits]
    w8 = jnp.concatenate(w_rows, axis=0)
    idx_ref[...] = jnp.concatenate(firsts, axis=0)
    rank_ref[...] = jnp.concatenate(r_rows, axis=0).astype(jnp.int32)
    w_ref[...] = w8 / jnp.sum(w8, axis=0, keepdims=True) * scale
    carry = carry + jnp.sum(self, axis=1, keepdims=True)
    carry_ref[...] = carry
    cnt_ref[...] = carry.astype(jnp.int32)


def route(logits_t, router_b, tl=256):
    n_exp, t = logits_t.shape
    body = functools.partial(_route_body, n_groups=N_GROUPS, topk_groups=TOPK_GROUPS, top_k=TOP_K,
                             scale=ROUTED_SCALE)
    tok_spec = pl.BlockSpec((TOP_K, tl), lambda i: (0, i))
    return pl.pallas_call(
        body, grid=(t // tl,),
        in_specs=[pl.BlockSpec((n_exp, tl), lambda i: (0, i)), pl.BlockSpec((n_exp, 1), lambda i: (0, 0))],
        out_specs=[tok_spec, tok_spec, tok_spec, pl.BlockSpec((n_exp, 1), lambda i: (0, 0))],
        out_shape=[jax.ShapeDtypeStruct((TOP_K, t), jnp.int32), jax.ShapeDtypeStruct((TOP_K, t), jnp.int32),
                   jax.ShapeDtypeStruct((TOP_K, t), F32), jax.ShapeDtypeStruct((n_exp, 1), jnp.int32)],
        scratch_shapes=[pltpu.VMEM((n_exp, 1), F32)],
        compiler_params=_params("arbitrary"), name="route",
    )(logits_t, router_b.reshape(n_exp, 1))


def _slab_copy(src_ref, src_tok, dst_ref, dst_tok, rows, sem):
    return pltpu.make_async_copy(src_ref.at[pl.ds(pl.multiple_of(src_tok * rows, rows), rows)],
                                 dst_ref.at[pl.ds(pl.multiple_of(dst_tok * rows, rows), rows)], sem)


def _dispatch_body(dest_ref, pad_ref, h_ref, xs_ref, zero_ref, sem, *, top_k, rows):
    tl = h_ref.shape[0] // rows
    n_exp = pad_ref.shape[1]

    def issue(r, carry):
        for k in range(top_k):
            _slab_copy(h_ref, r, xs_ref, dest_ref[r * top_k + k], rows, sem.at[0]).start(priority=k % 2)
        return carry

    lax.fori_loop(0, tl, issue, 0, unroll=DMA_LOOP_UNROLL)

    @pl.when(pl.program_id(0) == 0)
    def _():
        zero_ref[...] = jnp.zeros_like(zero_ref)
        largest = zero_ref.shape[0] // rows
        pieces = [largest >> b for b in range(largest.bit_length())]

        def pad_copies(e, wait):
            first, n_pad = pad_ref[0, e], pad_ref[1, e]
            for size in pieces:
                @pl.when((n_pad & size) != 0)
                def _():
                    slot0 = first + (n_pad & ~(2 * size - 1))
                    cp = pltpu.make_async_copy(
                        zero_ref.at[pl.ds(0, size * rows)],
                        xs_ref.at[pl.ds(pl.multiple_of(slot0 * rows, rows), size * rows)], sem.at[1])
                    if wait:
                        cp.wait()
                    else:
                        cp.start()

        def fill(e, carry):
            pad_copies(e, False)
            return carry

        def drain_pad(e, carry):
            pad_copies(e, True)
            return carry

        lax.fori_loop(0, n_exp, fill, 0)
        lax.fori_loop(0, n_exp, drain_pad, 0)

    def drain(r, carry):
        for k in range(top_k):
            _slab_copy(h_ref, 0, xs_ref, 0, rows, sem.at[0]).wait()
        return carry

    lax.fori_loop(0, tl, drain, 0, unroll=DMA_LOOP_UNROLL)


def dispatch(h_slab, dest_flat, pad_info, n_slot, top_k, rows, tl=256):
    t = h_slab.shape[0] // rows
    return pl.pallas_call(
        functools.partial(_dispatch_body, top_k=top_k, rows=rows), grid=(t // tl,),
        in_specs=[pl.BlockSpec((tl * top_k,), lambda i: (i,), memory_space=pltpu.SMEM),
                  pl.BlockSpec(memory_space=pltpu.SMEM),
                  pl.BlockSpec((tl * rows, LANES), lambda i: (i, 0))],
        out_specs=pl.BlockSpec(memory_space=pl.ANY),
        out_shape=jax.ShapeDtypeStruct((n_slot * rows, LANES), h_slab.dtype),
        scratch_shapes=[pltpu.VMEM((EXPERT_SLOT_BLOCK // 2 * rows, LANES), h_slab.dtype),
                        pltpu.SemaphoreType.DMA((2,))],
        compiler_params=_params("arbitrary"), name="dispatch",
    )(dest_flat, pad_info, h_slab)


def _combine_body(dest_ref, dest_next_ref, w_ref, sh_ref, x_ref, gate_ref, y_ref, o_ref, buf_ref, acc_ref, sem, *,
                  top_k, rows):
    i = pl.program_id(0)
    n = pl.num_programs(0)
    tl = x_ref.shape[0]

    def gather(d_ref, slot):
        def issue(r, carry):
            for k in range(top_k):
                _slab_copy(y_ref, d_ref[r * top_k + k], buf_ref.at[slot], k * tl + r, rows,
                           sem.at[slot]).start(priority=k % 2)
            return carry
        lax.fori_loop(0, tl, issue, 0, unroll=DMA_LOOP_UNROLL)

    @pl.when(i == 0)
    def _():
        gather(dest_ref, 0)

    @pl.when(i + 1 < n)
    def _():
        gather(dest_next_ref, (i + 1) % 2)

    slot = i % 2

    def drain(r, carry):
        for k in range(top_k):
            _slab_copy(y_ref, 0, buf_ref.at[slot], 0, rows, sem.at[slot]).wait()
        return carry

    lax.fori_loop(0, tl, drain, 0, unroll=DMA_LOOP_UNROLL)
    w = w_ref[...]
    acc = sh_ref[...]
    for k in range(top_k):
        wk = jnp.broadcast_to(w[:, k:k + 1], (tl, LANES))
        wk = jnp.broadcast_to(wk[:, None, :], (tl, rows, LANES)).reshape(tl * rows, LANES)
        acc = acc + buf_ref[slot, pl.ds(k * tl * rows, tl * rows), :] * wk
    acc_ref[...] = acc
    for c in range(rows):
        cols = slice(c * LANES, (c + 1) * LANES)
        o_ref[:, cols] = x_ref[:, cols] + gate_ref[0][:, cols] * acc_ref[pl.ds(c, tl, stride=rows), :]


def combine(y_slab, dest_flat, w, shared_slab, x, mod3, gate_idx, geom, tl=128):
    t, d = x.shape
    rows = d // LANES
    top_k = w.shape[1]
    n_lat, lat_len, n_batch = geom
    n_tiles = t // tl
    row = functools.partial(_mod_row, tile_rows=tl, n_lat=n_lat, lat_len=lat_len, n_batch=n_batch)
    return pl.pallas_call(
        functools.partial(_combine_body, top_k=top_k, rows=rows), grid=(n_tiles,),
        in_specs=[pl.BlockSpec((tl * top_k,), lambda i: (i,), memory_space=pltpu.SMEM),
                  pl.BlockSpec((tl * top_k,), lambda i: (jnp.minimum(i + 1, n_tiles - 1),),
                               memory_space=pltpu.SMEM),
                  pl.BlockSpec((tl, top_k), lambda i: (i, 0)),
                  pl.BlockSpec((tl * rows, LANES), lambda i: (i, 0)),
                  pl.BlockSpec((tl, d), lambda i: (i, 0)),
                  pl.BlockSpec((1, 1, d), lambda i: (row(i) * 6 + gate_idx, 0, 0)),
                  pl.BlockSpec(memory_space=pl.ANY)],
        out_specs=pl.BlockSpec((tl, d), lambda i: (i, 0)),
        out_shape=jax.ShapeDtypeStruct((t, d), F32),
        scratch_shapes=[pltpu.VMEM((2, top_k * tl * rows, LANES), F32), pltpu.VMEM((tl * rows, LANES), F32),
                        pltpu.SemaphoreType.DMA((2,))],
        compiler_params=_params("arbitrary"), name="combine",
    )(dest_flat, dest_flat, w, shared_slab, x, mod3, y_slab)


def moe_ffn(x, h_pk, logits_t, router_b, exp_gate_up, exp_down, shared_gate_up, shared_down, mod3, geom):
    t = x.shape[0]
    slab_rows = h_pk.shape[0] // t
    e_count = exp_gate_up.shape[0]
    blk = EXPERT_SLOT_BLOCK
    idx8, rank8, w8, counts = route(logits_t, router_b)
    counts = counts[:, 0]
    padded = (counts + blk - 1) // blk * blk
    pad_end = jnp.cumsum(padded)
    pad_start = pad_end - padded
    start8 = jnp.sum(jnp.where(idx8[None] == jnp.arange(e_count, dtype=jnp.int32)[:, None, None],
                               pad_start[:, None, None], 0), axis=0)
    dest_flat = (start8 + rank8).T.reshape(-1)
    n_blk = (t * TOP_K + e_count * (blk - 1)) // blk + 1
    n_slot = n_blk * blk
    blk_starts = jnp.arange(n_blk, dtype=jnp.int32) * blk
    blk_e = jnp.sum((pad_end[None, :] <= blk_starts[:, None]).astype(jnp.int32), axis=1)
    blk_e = jnp.minimum(blk_e, e_count - 1)
    n_used = (pad_end[-1] // blk).astype(jnp.int32).reshape(1)
    pad_info = jnp.stack([pad_start + counts, padded - counts]).astype(jnp.int32)
    x_sorted = dispatch(h_pk, dest_flat, pad_info, n_slot, TOP_K, slab_rows)
    y_slot = expert_blocks(x_sorted, exp_gate_up, exp_down, blk_e, n_used, blk)
    shared = expert_blocks(h_pk, shared_gate_up[None], shared_down[None], jnp.zeros((t // blk,), jnp.int32),
                           jnp.full((1,), t // blk, jnp.int32), blk)
    return combine(y_slot, dest_flat, w8.T, shared, x, mod3, 5, geom)


def kernel(x, c, ctx, c_ctx, hg_lb_logits, l0_mod_w, l0_mod_b, l0_norm1, l0_norm2, l0_w_in, l0_dn_conv, l0_dn_a_log, l0_dn_dt_bias, l0_dn_norm, l0_hg_norm, l0_w_out, l0_router_w, l0_router_b, l0_exp_gate_up, l0_exp_down, l0_shared_gate_up, l0_shared_down, l1_mod_w, l1_mod_b, l1_norm1, l1_norm2, l1_w_in, l1_q_norm, l1_k_norm, l1_lambda, l1_sub_norm, l1_w_out, l1_router_w, l1_router_b, l1_exp_gate_up, l1_exp_down, l1_shared_gate_up, l1_shared_down):
    n_batch, lat_len, d = x.shape
    ctx_len = ctx.shape[1]
    n_lat = n_batch * lat_len
    n_ctx = n_batch * ctx_len
    geom = (n_lat, lat_len, n_batch)
    geom_lat_only = (n_lat, lat_len, n_batch)

    xs = jnp.concatenate([x.reshape(n_lat, d), ctx.reshape(n_ctx, d)], axis=0)
    cond = jnp.concatenate([c, c_ctx[None], jnp.zeros((MOD_ROWS - n_batch - 1, d), F32)], axis=0)

    mod3 = modulation(cond, l0_mod_w, l0_mod_b).reshape(MOD_ROWS * 6, 1, d)
    h = adaln(xs, l0_norm1, mod3, 0, geom)
    n_dn = IN0_SIZES[0] + IN0_SIZES[1]
    n_gate = IN0_SIZES[2] + IN0_SIZES[3]
    proj_dn = matmul(h, l0_w_in, n_cols=n_dn)
    proj_hg = matmul(h, l0_w_in[:, n_dn + n_gate:])
    gc, gr = gate_prep(h, l0_w_in[:, n_dn:n_dn + n_gate], l0_dn_a_log, l0_dn_dt_bias, DN_HEADS)
    conv_t = jnp.concatenate([l0_dn_conv.T, jnp.zeros((8 - l0_dn_conv.shape[1], 3 * DN_WIDTH), F32)], axis=0)
    qkv = dn_prep(proj_dn, conv_t, geom, ctx_len, DN_WIDTH, DN_HEAD_DIM)
    dn_f, dn_b = delta_scan(qkv, gc, gr, geom, ctx_len, DN_HEADS, DN_HEAD_DIM, inv_passes=DELTA_INV_PASSES)
    lb = jnp.cumsum(jax.nn.softmax(hg_lb_logits, axis=0), axis=0)[0:1]
    hg_f, hg_b = hgrn_scan(proj_hg, lb, geom, ctx_len, HG_HEADS, HG_KEY_DIM)
    y = mix_out(dn_f, dn_b, hg_f, hg_b, proj_dn, 3, proj_hg, 4, l0_dn_norm, l0_hg_norm, DN_HEAD_DIM)
    xs = matmul_residual(y, l0_w_out, xs, mod3, 2, geom)
    h, logits_t = adaln(xs, l0_norm2, mod3, 3, geom, router_w=l0_router_w)
    xs = moe_ffn(xs, h, logits_t, l0_router_b, l0_exp_gate_up, l0_exp_down, l0_shared_gate_up, l0_shared_down,
                 mod3, geom)

    mod3 = modulation(cond, l1_mod_w, l1_mod_b).reshape(MOD_ROWS * 6, 1, d)
    h = adaln(xs, l1_norm1, mod3, 0, geom)
    cos_t, sin_t = rope_tables(lat_len, n_batch, n_ctx, DA_HEAD_DIM)
    qkv = matmul_qkv(h, l1_w_in, l1_q_norm, l1_k_norm, cos_t, sin_t)
    lam_init = 0.8 - 0.6 * math.exp(-0.3 * 1)
    lmbda = (jnp.exp(jnp.sum(l1_lambda[0] * l1_lambda[1])) - jnp.exp(jnp.sum(l1_lambda[2] * l1_lambda[3]))
             + lam_init)
    y = diff_attention(qkv, lmbda, l1_sub_norm, n_batch, lat_len, ctx_len, DA_HEADS, DA_HEAD_DIM,
                       1.0 - lam_init)
    xl = matmul_residual(y, l1_w_out, xs, mod3, 2, geom_lat_only)
    h, logits_t = adaln(xl, l1_norm2, mod3, 3, geom_lat_only, router_w=l1_router_w)
    xl = moe_ffn(xl, h, logits_t, l1_router_b, l1_exp_gate_up, l1_exp_down, l1_shared_gate_up, l1_shared_down,
                 mod3, geom_lat_only)
    return xl.reshape(n_batch, lat_len, d)
```

```python
import functools
import math

import jax
import jax.numpy as jnp
import numpy as np
from jax import lax
from jax.experimental import pallas as pl
from jax.experimental.pallas import tpu as pltpu

F32 = jnp.float32
BF16 = jnp.bfloat16

NORM_EPS = 1e-6
GRID_W = 64
ROPE_BASE = 10000.0

DN_HEADS = 8
DN_HEAD_DIM = 128
DN_WIDTH = DN_HEADS * DN_HEAD_DIM
DN_CHUNK = 64
HG_HEADS = 8
HG_KEY_DIM = 128
HG_VAL_DIM = 128
HG_K_WIDTH = HG_HEADS * HG_KEY_DIM
HG_V_WIDTH = HG_HEADS * HG_VAL_DIM
HG_CHUNK = 64
IN0_SIZES = (3 * DN_WIDTH, DN_WIDTH, 2 * DN_HEADS, 2 * DN_HEADS,
             HG_K_WIDTH, 2 * HG_K_WIDTH, HG_V_WIDTH, HG_V_WIDTH)
DA_HEADS = 8
DA_HEAD_DIM = 128
N_EXPERTS = 64
TOP_K = 8
N_GROUPS = 8
TOPK_GROUPS = 4
EXPERT_FF = 512
ROUTED_SCALE = 2.5

VMEM_LIMIT_BYTES = 56 * 1024 * 1024
MOD_ROWS = 8
EXPERT_SLOT_BLOCK = 256
DMA_LOOP_UNROLL = 8
MM_TILE_M = 1024
MM_TILE_N = 1024
EXPERT_ROW_CHUNKS = 2
MM_EPILOGUE_CHUNKS = 4
LANES = 128
ATTN_TILE_Q = 512
ATTN_ROW_CHUNK = 128


def _params(*sem):
    return pltpu.CompilerParams(dimension_semantics=sem, vmem_limit_bytes=VMEM_LIMIT_BYTES)


def _sigmoid(x):
    return 0.5 * jnp.tanh(0.5 * x) + 0.5


def _silu(x):
    return x * _sigmoid(x)


def _mod_body(c_ref, w_ref, b_ref, o_ref):
    a = _silu(c_ref[...]).astype(BF16)
    o_ref[...] = jnp.dot(a, w_ref[...].astype(BF16), preferred_element_type=F32) + b_ref[...]


def modulation(cond, w, b, tn=1024):
    m, k = cond.shape
    n = w.shape[1]
    return pl.pallas_call(
        _mod_body, grid=(n // tn,),
        in_specs=[pl.BlockSpec((m, k), lambda j: (0, 0)),
                  pl.BlockSpec((k, tn), lambda j: (0, j)),
                  pl.BlockSpec((1, tn), lambda j: (0, j))],
        out_specs=pl.BlockSpec((m, tn), lambda j: (0, j)),
        out_shape=jax.ShapeDtypeStruct((m, n), F32),
        compiler_params=_params("arbitrary"), name="modulation",
    )(cond, w, b.reshape(1, n))


def _mod_row(tile, tile_rows, n_lat, lat_len, n_batch):
    start = tile * tile_rows
    return jnp.where(start < n_lat, start // lat_len, n_batch)


def _pack_bf16_pairs(h):
    half = h.shape[1] // 2
    bits = lax.bitcast_convert_type(h.astype(BF16).astype(F32), jnp.uint32)
    return (bits[:, :half] & jnp.uint32(0xFFFF0000)) | (bits[:, half:] >> jnp.uint32(16))


def _unpack_bf16_pairs(pk):
    hi = lax.bitcast_convert_type(pk & jnp.uint32(0xFFFF0000), F32).astype(BF16)
    lo = lax.bitcast_convert_type(pk << jnp.uint32(16), F32).astype(BF16)
    return hi, lo


def _store_slabs(ref, x, row0=0):
    r, w = x.shape
    c_n = w // LANES
    for c in range(c_n):
        ref[pl.ds(row0 * c_n + c, r, stride=c_n), :] = x[:, c * LANES:(c + 1) * LANES]


def _load_slabs(ref, row0, r, c_n):
    return jnp.concatenate([ref[pl.ds(row0 * c_n + c, r, stride=c_n), :] for c in range(c_n)], axis=1)


def _adaln_body(x_ref, nw_ref, shift_ref, scale_ref, o_ref):
    x = x_ref[...]
    y = x * lax.rsqrt(jnp.mean(x * x, axis=-1, keepdims=True) + NORM_EPS) * nw_ref[...]
    o_ref[...] = (y * (1.0 + scale_ref[0]) + shift_ref[0]).astype(o_ref.dtype)


def _adaln_router_body(x_ref, nw_ref, shift_ref, scale_ref, rw_ref, o_ref, lg_ref):
    x = x_ref[...]
    y = x * lax.rsqrt(jnp.mean(x * x, axis=-1, keepdims=True) + NORM_EPS) * nw_ref[...]
    h = y * (1.0 + scale_ref[0]) + shift_ref[0]
    _store_slabs(o_ref, _pack_bf16_pairs(h))
    lg_ref[...] = lax.dot_general(rw_ref[...], h, (((1,), (1,)), ((), ())), preferred_element_type=F32,
                                  precision=lax.Precision.HIGHEST)


def adaln(x, norm_w, mod3, shift_idx, geom, router_w=None, tl=256):
    t, d = x.shape
    n_lat, lat_len, n_batch = geom
    row = functools.partial(_mod_row, tile_rows=tl, n_lat=n_lat, lat_len=lat_len, n_batch=n_batch)
    in_specs = [pl.BlockSpec((tl, d), lambda i: (i, 0)),
                pl.BlockSpec((1, d), lambda i: (0, 0)),
                pl.BlockSpec((1, 1, d), lambda i: (row(i) * 6 + shift_idx, 0, 0)),
                pl.BlockSpec((1, 1, d), lambda i: (row(i) * 6 + shift_idx + 1, 0, 0))]
    args = [x, norm_w.reshape(1, d), mod3, mod3]
    if router_w is None:
        return pl.pallas_call(
            _adaln_body, grid=(t // tl,), in_specs=in_specs,
            out_specs=pl.BlockSpec((tl, d), lambda i: (i, 0)),
            out_shape=jax.ShapeDtypeStruct((t, d), BF16),
            compiler_params=_params("arbitrary"), name="adaln")(*args)
    e = router_w.shape[1]
    return pl.pallas_call(
        _adaln_router_body, grid=(t // tl,),
        in_specs=in_specs + [pl.BlockSpec((e, d), lambda i: (0, 0))],
        out_specs=[pl.BlockSpec((tl * (d // 2 // LANES), LANES), lambda i: (i, 0)),
                   pl.BlockSpec((e, tl), lambda i: (0, i))],
        out_shape=[jax.ShapeDtypeStruct((t * (d // 2 // LANES), LANES), jnp.uint32),
                   jax.ShapeDtypeStruct((e, t), F32)],
        compiler_params=_params("arbitrary"), name="adaln_router")(*args, router_w.T)


def _mm_body(a_ref, w_ref, o_ref, wb_ref):
    @pl.when(pl.program_id(1) == 0)
    def _():
        wb_ref[...] = w_ref[...].astype(BF16)
    o_ref[...] = jnp.dot(a_ref[...], wb_ref[...], preferred_element_type=F32).astype(o_ref.dtype)


def _mm_res_body(a_ref, w_ref, res_ref, gate_ref, o_ref, wb_ref):
    @pl.when(pl.program_id(1) == 0)
    def _():
        wb_ref[...] = w_ref[...].astype(BF16)
    acc = jnp.dot(a_ref[...], wb_ref[...], preferred_element_type=F32)
    o_ref[...] = res_ref[...] + gate_ref[0] * acc


def matmul(a, w, n_cols=None, col_block0=0, tm=MM_TILE_M, tn=MM_TILE_N, out_dtype=F32):
    m, k = a.shape
    n = w.shape[1] if n_cols is None else n_cols
    return pl.pallas_call(
        _mm_body, grid=(n // tn, m // tm),
        in_specs=[pl.BlockSpec((tm, k), lambda j, i: (i, 0)),
                  pl.BlockSpec((k, tn), lambda j, i: (0, j + col_block0))],
        out_specs=pl.BlockSpec((tm, tn), lambda j, i: (i, j)),
        out_shape=jax.ShapeDtypeStruct((m, n), out_dtype),
        scratch_shapes=[pltpu.VMEM((k, tn), BF16)],
        compiler_params=_params("arbitrary", "arbitrary"), name="matmul",
    )(a, w)


def matmul_residual(a, w, res, mod3, gate_idx, geom, tm=MM_TILE_M, tn=MM_TILE_N):
    m, k = a.shape
    n = w.shape[1]
    n_lat, lat_len, n_batch = geom
    row = functools.partial(_mod_row, tile_rows=tm, n_lat=n_lat, lat_len=lat_len, n_batch=n_batch)
    nb = n // tn
    return pl.pallas_call(
        _mm_res_body, grid=(n // tn, m // tm),
        in_specs=[pl.BlockSpec((tm, k), lambda j, i: (i, 0)),
                  pl.BlockSpec((k, tn), lambda j, i: (0, j)),
                  pl.BlockSpec((tm, tn), lambda j, i: (i, j)),
                  pl.BlockSpec((1, 1, tn), lambda j, i: (row(i) * 6 + gate_idx, 0, j))],
        out_specs=pl.BlockSpec((tm, tn), lambda j, i: (i, j)),
        out_shape=jax.ShapeDtypeStruct((m, n), F32),
        scratch_shapes=[pltpu.VMEM((k, tn), BF16)],
        compiler_params=_params("arbitrary", "arbitrary"), name="matmul_residual",
    )(a, w, res, mod3)


def _expert_body(blk_e_ref, blk_new_ref, next_e_ref, stage_ref, n_used_ref, x_ref, gu_hbm, dn_hbm, o_ref,
                 gu_stage, dn_stage, gub_ref, dnb_ref, sem):
    i = pl.program_id(0)

    def weight_copies(e, slot):
        return (pltpu.make_async_copy(gu_hbm.at[e], gu_stage.at[slot], sem.at[0, slot]),
                pltpu.make_async_copy(dn_hbm.at[e], dn_stage.at[slot], sem.at[1, slot]))

    @pl.when(i == 0)
    def _():
        for cp in weight_copies(blk_e_ref[0], 0):
            cp.start()

    @pl.when(blk_new_ref[i] == 1)
    def _():
        slot = stage_ref[i]
        for cp in weight_copies(blk_e_ref[i], slot):
            cp.wait()
        gub_ref[...] = gu_stage[slot].astype(BF16)
        dnb_ref[...] = dn_stage[slot].astype(BF16)

        @pl.when(next_e_ref[i] >= 0)
        def _():
            for cp in weight_copies(next_e_ref[i], 1 - slot):
                cp.start()

    @pl.when(i < n_used_ref[0])
    def _():
        ff, d = dnb_ref.shape
        half = d // 2
        blk = x_ref.shape[0] // (half // LANES)
        rc = blk // EXPERT_ROW_CHUNKS

        def rows_chain(r0):
            x_hi, x_lo = _unpack_bf16_pairs(_load_slabs(x_ref, r0, rc, half // LANES))
            yield
            h1 = (jnp.dot(x_hi, gub_ref[:half], preferred_element_type=F32)
                  + jnp.dot(x_lo, gub_ref[half:], preferred_element_type=F32))
            yield
            act = (_silu(h1[:, :ff]) * h1[:, ff:]).astype(BF16)
            y = jnp.dot(act, dnb_ref[...], preferred_element_type=F32)
            yield
            _store_slabs(o_ref, y, r0)

        _run_interleaved(rows_chain(r0) for r0 in range(0, blk, rc))

    @pl.when(i >= n_used_ref[0])
    def _():
        o_ref[...] = jnp.zeros_like(o_ref)


def expert_blocks(x, gate_up, down, blk_e, n_used, blk):
    _, d, f2 = gate_up.shape
    xc, yc = d // 2 // LANES, d // LANES
    s = x.shape[0] // xc
    n_blk = s // blk
    pos = jnp.arange(n_blk, dtype=jnp.int32)
    used = pos < n_used[0]
    blk_new = (jnp.concatenate([jnp.ones((1,), bool), blk_e[1:] != blk_e[:-1]]) & used).astype(jnp.int32)
    stage = (jnp.cumsum(blk_new) - 1) % 2
    first_pos = jnp.where(blk_new == 1, pos, n_blk)
    next_first = jnp.concatenate([lax.cummin(first_pos, reverse=True)[1:], jnp.full((1,), n_blk, jnp.int32)])
    next_e = jnp.where(next_first < n_blk, blk_e[jnp.minimum(next_first, n_blk - 1)], -1).astype(jnp.int32)
    grid_spec = pltpu.PrefetchScalarGridSpec(
        num_scalar_prefetch=5, grid=(n_blk,),
        in_specs=[pl.BlockSpec((blk * xc, LANES), lambda i, be, bn, ne, st, nu: (jnp.minimum(i, nu[0] - 1), 0)),
                  pl.BlockSpec(memory_space=pl.ANY),
                  pl.BlockSpec(memory_space=pl.ANY)],
        out_specs=pl.BlockSpec((blk * yc, LANES), lambda i, be, bn, ne, st, nu: (i, 0)),
        scratch_shapes=[pltpu.VMEM((2, d, f2), F32), pltpu.VMEM((2, f2 // 2, d), F32),
                        pltpu.VMEM((d, f2), BF16), pltpu.VMEM((f2 // 2, d), BF16),
                        pltpu.SemaphoreType.DMA((2, 2))])
    return pl.pallas_call(
        _expert_body, grid_spec=grid_spec,
        out_shape=jax.ShapeDtypeStruct((s * yc, LANES), F32),
        compiler_params=_params("arbitrary"), name="expert_blocks",
    )(blk_e, blk_new, next_e, stage.astype(jnp.int32), n_used, x, gate_up, down)


def _norm_rope(x, w, cos, sin, first, head_dim):
    y = x * lax.rsqrt(jnp.mean(x * x, axis=-1, keepdims=True) + NORM_EPS) * w
    swapped = jnp.where(first, pltpu.roll(y, head_dim - head_dim // 4, 1), pltpu.roll(y, head_dim // 4, 1))
    return y * cos + swapped * sin


def _mm_qkv_body(a_ref, w_ref, qw_ref, kw_ref, cos_ref, sin_ref, o_ref, wb_ref, *, head_dim, n_q_tiles, n_k_tiles):
    j = pl.program_id(0)

    @pl.when(pl.program_id(1) == 0)
    def _():
        wb_ref[...] = w_ref[...].astype(BF16)

    tm, tn = o_ref.shape
    row_chunk = tm // MM_EPILOGUE_CHUNKS

    def rows_chain(rows, nw_ref):
        acc = jnp.dot(a_ref[rows, :], wb_ref[...], preferred_element_type=F32)
        yield
        if nw_ref is None:
            o_ref[rows, :] = acc.astype(o_ref.dtype)
            return
        cos, sin = cos_ref[rows, :], sin_ref[rows, :]
        lane = lax.broadcasted_iota(jnp.int32, cos.shape, 1)
        first = (lane % (head_dim // 2)) < (head_dim // 4)
        for g in range(tn // head_dim):
            sl = slice(g * head_dim, (g + 1) * head_dim)
            o_ref[rows, sl] = _norm_rope(acc[:, sl], nw_ref[...], cos, sin, first, head_dim).astype(o_ref.dtype)
            yield

    def tile(nw_ref):
        _run_interleaved(rows_chain(slice(r, r + row_chunk), nw_ref) for r in range(0, tm, row_chunk))

    @pl.when(j < n_q_tiles)
    def _():
        tile(qw_ref)

    @pl.when((j >= n_q_tiles) & (j < n_q_tiles + n_k_tiles))
    def _():
        tile(kw_ref)

    @pl.when(j >= n_q_tiles + n_k_tiles)
    def _():
        tile(None)


def matmul_qkv(a, w, q_norm, k_norm, cos_t, sin_t, tm=MM_TILE_M, tn=MM_TILE_N):
    m, k = a.shape
    n = w.shape[1]
    hd = q_norm.shape[0]
    body = functools.partial(_mm_qkv_body, head_dim=hd, n_q_tiles=n // 3 // tn, n_k_tiles=n // 3 // tn)
    return pl.pallas_call(
        body, grid=(n // tn, m // tm),
        in_specs=[pl.BlockSpec((tm, k), lambda j, i: (i, 0)),
                  pl.BlockSpec((k, tn), lambda j, i: (0, j)),
                  pl.BlockSpec((1, hd), lambda j, i: (0, 0)),
                  pl.BlockSpec((1, hd), lambda j, i: (0, 0)),
                  pl.BlockSpec((tm, hd), lambda j, i: (i, 0)),
                  pl.BlockSpec((tm, hd), lambda j, i: (i, 0))],
        out_specs=pl.BlockSpec((tm, tn), lambda j, i: (i, j)),
        out_shape=jax.ShapeDtypeStruct((m, n), BF16),
        scratch_shapes=[pltpu.VMEM((k, tn), BF16)],
        compiler_params=_params("arbitrary", "arbitrary"), name="matmul_qkv",
    )(a, w, q_norm.reshape(1, hd), k_norm.reshape(1, hd), cos_t, sin_t)


def rope_tables(n_lat_tokens_per_sample, n_batch, n_ctx_tokens, head_dim):
    quarter = head_dim // 4
    inv_freq = ROPE_BASE ** (-jnp.arange(quarter, dtype=F32) / quarter)
    rows = n_lat_tokens_per_sample // GRID_W
    row = jnp.repeat(jnp.arange(rows, dtype=F32), GRID_W)
    col = jnp.tile(jnp.arange(GRID_W, dtype=F32), rows)
    ang_r = row[:, None] * inv_freq[None, :]
    ang_c = col[:, None] * inv_freq[None, :]
    cos = jnp.concatenate([jnp.cos(ang_r), jnp.cos(ang_r), jnp.cos(ang_c), jnp.cos(ang_c)], axis=-1)
    sin = jnp.concatenate([-jnp.sin(ang_r), jnp.sin(ang_r), -jnp.sin(ang_c), jnp.sin(ang_c)], axis=-1)
    cos = jnp.concatenate([jnp.tile(cos, (n_batch, 1)), jnp.ones((n_ctx_tokens, head_dim), F32)], axis=0)
    sin = jnp.concatenate([jnp.tile(sin, (n_batch, 1)), jnp.zeros((n_ctx_tokens, head_dim), F32)], axis=0)
    return cos, sin


def _diff_attn_rows(lam, q_ref, k_all, v_all, sw_ref, o_ref, rows, head_dim, out_scale):
    c = head_dim ** -0.5 * math.log2(math.e)
    es, invs = [], []
    for s in range(2):
        sl = slice(s * head_dim, (s + 1) * head_dim)
        sc = lax.dot_general(q_ref[rows, sl], k_all[:, sl], (((1,), (1,)), ((), ())), preferred_element_type=F32)
        yield
        e = jnp.exp2((sc - jnp.max(sc, axis=-1, keepdims=True)) * c)
        invs.append(1.0 / jnp.sum(e, axis=-1, keepdims=True))
        es.append(e.astype(BF16))
        yield
    v = v_all[...]
    o0 = jnp.dot(es[0], v, preferred_element_type=F32)
    o1 = jnp.dot(es[1], v, preferred_element_type=F32)
    yield
    o = o0 * invs[0] - (lam * invs[1]) * o1
    y = o * lax.rsqrt(jnp.mean(o * o, axis=-1, keepdims=True) + NORM_EPS) * sw_ref[...]
    o_ref[rows, :] = (y * out_scale).astype(o_ref.dtype)
    yield


def _diff_attn_body(lam_ref, q_ref, kl_ref, kc_ref, vl_ref, vc_ref, sw_ref, o_ref, k_all, v_all, *,
                    head_dim, out_scale, row_chunk):
    @pl.when(pl.program_id(2) == 0)
    def _():
        n_l = kl_ref.shape[0]
        k_all[:n_l] = kl_ref[...]
        k_all[n_l:] = kc_ref[...]
        v_all[:n_l] = vl_ref[...]
        v_all[n_l:] = vc_ref[...]

    lam = lam_ref[0]
    tq = q_ref.shape[0]
    _run_interleaved(
        _diff_attn_rows(lam, q_ref, k_all, v_all, sw_ref, o_ref, slice(r, r + row_chunk), head_dim, out_scale)
        for r in range(0, tq, row_chunk))


def diff_attention(qkv, lmbda, sub_norm, n_batch, lat_len, ctx_len, n_heads, head_dim, out_scale,
                   tq=ATTN_TILE_Q, row_chunk=ATTN_ROW_CHUNK):
    hw = 2 * head_dim
    nq = lat_len // tq
    ctx_blk0 = n_batch * lat_len // ctx_len
    body = functools.partial(_diff_attn_body, head_dim=head_dim, out_scale=out_scale, row_chunk=row_chunk)
    n_keys = lat_len + ctx_len
    return pl.pallas_call(
        body, grid=(n_batch, n_heads, nq),
        in_specs=[pl.BlockSpec(memory_space=pltpu.SMEM),
                  pl.BlockSpec((tq, hw), lambda b, h, i: (b * nq + i, h)),
                  pl.BlockSpec((lat_len, hw), lambda b, h, i: (b, n_heads + h)),
                  pl.BlockSpec((ctx_len, hw), lambda b, h, i: (ctx_blk0 + b, n_heads + h)),
                  pl.BlockSpec((lat_len, hw), lambda b, h, i: (b, 2 * n_heads + h)),
                  pl.BlockSpec((ctx_len, hw), lambda b, h, i: (ctx_blk0 + b, 2 * n_heads + h)),
                  pl.BlockSpec((1, hw), lambda b, h, i: (0, 0))],
        out_specs=pl.BlockSpec((tq, hw), lambda b, h, i: (b * nq + i, h)),
        out_shape=jax.ShapeDtypeStruct((n_batch * lat_len, n_heads * hw), BF16),
        scratch_shapes=[pltpu.VMEM((n_keys, hw), BF16), pltpu.VMEM((n_keys, hw), BF16)],
        compiler_params=_params("arbitrary", "arbitrary", "arbitrary"), name="diff_attention",
    )(lmbda.reshape(1), qkv, qkv, qkv, qkv, qkv, sub_norm.reshape(1, hw))


SCAN_TILE = 256
CHUNK = 64
SUB = 16
NEG_BIG = -1e30
NEG_INF = float("-inf")
DELTA_INV_PASSES = 1
DELTA_HEADS_PER_STEP = 4
HGRN_HEADS_PER_STEP = 2


def _dot(a, b):
    return jnp.dot(a.astype(BF16), b.astype(BF16), preferred_element_type=F32)


def _dot_nt(a, b):
    return lax.dot_general(a.astype(BF16), b.astype(BF16), (((1,), (1,)), ((), ())),
                           preferred_element_type=F32)


def _split3(x):
    hi = x.astype(BF16)
    r = x - hi.astype(F32)
    mid = r.astype(BF16)
    lo = (r - mid.astype(F32)).astype(BF16)
    return hi, mid, lo


def _dot_exact_lhs01(m01, x):
    hi, mid, lo = _split3(x)
    m = m01.astype(BF16)
    return (jnp.dot(m, hi, preferred_element_type=F32) + jnp.dot(m, mid, preferred_element_type=F32)
            + jnp.dot(m, lo, preferred_element_type=F32))


def _dot_exact_rhs01(x, m01):
    hi, mid, lo = _split3(x)
    m = m01.astype(BF16)
    return (jnp.dot(hi, m, preferred_element_type=F32) + jnp.dot(mid, m, preferred_element_type=F32)
            + jnp.dot(lo, m, preferred_element_type=F32))


def _dot3(a, b):
    ah = a.astype(BF16)
    al = (a - ah.astype(F32)).astype(BF16)
    bh = b.astype(BF16)
    bl = (b - bh.astype(F32)).astype(BF16)
    return (jnp.dot(ah, bh, preferred_element_type=F32) + jnp.dot(ah, bl, preferred_element_type=F32)
            + jnp.dot(al, bh, preferred_element_type=F32))


def _tile_masks(n, reverse):
    i = lax.broadcasted_iota(jnp.int32, (n, n), 0)
    j = lax.broadcasted_iota(jnp.int32, (n, n), 1)
    same = (i // CHUNK) == (j // CHUNK)
    if reverse:
        return same & (i <= j), same & (i < j)
    return same & (i >= j), same & (i > j)


def _segment_tile(b, s, reverse, nl, nc, ctx_tile0):
    if reverse:
        return jnp.where(s < nc, ctx_tile0 + b * nc + (nc - 1 - s), b * nl + (nl - 1 - (s - nc)))
    return jnp.where(s < nc, ctx_tile0 + b * nc + s, b * nl + (s - nc))


def _dn_prep_body(x_ref, prev_ref, next_ref, w_ref, o_ref, *, tiles_per_lat_seg, tiles_per_ctx_seg, n_lat_tiles,
                  head_dim, q_scale):
    i = pl.program_id(0)
    j = pl.program_id(1)
    is_lat = i < n_lat_tiles
    pos = jnp.where(is_lat, i % tiles_per_lat_seg, (i - n_lat_tiles) % tiles_per_ctx_seg)
    seg_first = pos == 0
    seg_last = pos == jnp.where(is_lat, tiles_per_lat_seg, tiles_per_ctx_seg) - 1
    x = x_ref[...]
    tl = x.shape[0]
    prev = jnp.where(seg_first, 0.0, prev_ref[...])
    nxt = jnp.where(seg_last, 0.0, next_ref[...])
    xp = jnp.concatenate([prev, x, nxt], axis=0)
    w = w_ref[...]
    n_taps = 5
    acc = None
    for t in range(n_taps):
        off = 8 + t - n_taps // 2
        term = xp[off:off + tl] * w[t:t + 1]
        acc = term if acc is None else acc + term
    y = _silu(acc)
    scale = jnp.where(j == 0, q_scale, 1.0)
    outs = []
    for h in range(y.shape[1] // head_dim):
        yh = y[:, h * head_dim:(h + 1) * head_dim]
        nrm = lax.rsqrt(jnp.sum(yh * yh, axis=-1, keepdims=True) + 1e-6) * scale
        outs.append(yh * jnp.where(j == 2, 1.0, nrm))
    o_ref[...] = jnp.concatenate(outs, axis=1)


def dn_prep(proj_dn, conv_w_t, geom, ctx_len, width, head_dim):
    n_lat, lat_len, n_batch = geom
    t = proj_dn.shape[0]
    tl = SCAN_TILE
    rows8 = tl // 8
    n_tiles = t // tl
    body = functools.partial(_dn_prep_body, tiles_per_lat_seg=lat_len // tl, tiles_per_ctx_seg=ctx_len // tl,
                             n_lat_tiles=n_lat // tl,
                             head_dim=head_dim, q_scale=head_dim ** -0.5)
    last8 = t // 8 - 1
    return pl.pallas_call(
        body, grid=(n_tiles, 3),
        in_specs=[pl.BlockSpec((tl, width), lambda i, j: (i, j)),
                  pl.BlockSpec((8, width), lambda i, j: (jnp.maximum(i * rows8 - 1, 0), j)),
                  pl.BlockSpec((8, width), lambda i, j: (jnp.minimum((i + 1) * rows8, last8), j)),
                  pl.BlockSpec((8, width), lambda i, j: (0, j))],
        out_specs=pl.BlockSpec((tl, width), lambda i, j: (i, j)),
        out_shape=jax.ShapeDtypeStruct((t, 3 * width), F32),
        compiler_params=_params("arbitrary", "arbitrary"), name="dn_prep",
    )(proj_dn, proj_dn, proj_dn, conv_w_t)


def _softplus(x):
    return jnp.maximum(x, 0.0) + jnp.log(1.0 + jnp.exp(-jnp.abs(x)))


def _gate_prep_body(h_ref, wc_ref, wr_ref, alog_c_ref, dtb_c_ref, alog_r_ref, dtb_r_ref, gc_ref, gr_ref, *, n_heads):
    h = h_ref[...]
    tl = h.shape[0]
    nd = 2 * n_heads
    raw_c = jnp.dot(h, wc_ref[...].astype(BF16), preferred_element_type=F32)
    raw_r = lax.dot_general(wr_ref[...].astype(BF16), h, (((1,), (1,)), ((), ())),
                            preferred_element_type=F32)
    g_c = -jnp.exp(alog_c_ref[...]) * _softplus(raw_c[:, :nd] + dtb_c_ref[...])
    g_r = -jnp.exp(alog_r_ref[...]) * _softplus(raw_r[:nd, :] + dtb_r_ref[...])
    incl_f, _ = _tile_masks(tl, False)
    incl_b, _ = _tile_masks(tl, True)
    one_f = jnp.where(incl_f, 1.0, 0.0)
    one_b = jnp.where(incl_b, 1.0, 0.0)
    cum_c = jnp.concatenate([_dot_exact_lhs01(one_f, g_c[:, :n_heads]),
                             _dot_exact_lhs01(one_b, g_c[:, n_heads:])], axis=1)
    cum_r = jnp.concatenate([_dot_exact_rhs01(g_r[:n_heads, :], one_b),
                             _dot_exact_rhs01(g_r[n_heads:, :], one_f)], axis=0)
    beta_c = _sigmoid(raw_c[:, nd:])
    gc_ref[...] = jnp.concatenate([cum_c, beta_c], axis=1)
    gr_ref[...] = jnp.concatenate([cum_r, jnp.zeros_like(cum_r)], axis=0)


def gate_prep(h_bf, w_gate, a_log, dt_bias, n_heads):
    t, d = h_bf.shape
    tl = SCAN_TILE
    nd = 2 * n_heads
    body = functools.partial(_gate_prep_body, n_heads=n_heads)
    full = lambda shape: pl.BlockSpec(shape, lambda i: (0, 0))
    return pl.pallas_call(
        body, grid=(t // tl,),
        in_specs=[pl.BlockSpec((tl, d), lambda i: (i, 0)), full((d, 2 * nd)), full((2 * nd, d)),
                  full((1, nd)), full((1, nd)), full((nd, 1)), full((nd, 1))],
        out_specs=[pl.BlockSpec((tl, 2 * nd), lambda i: (i, 0)), pl.BlockSpec((2 * nd, tl), lambda i: (0, i))],
        out_shape=[jax.ShapeDtypeStruct((t, 2 * nd), F32), jax.ShapeDtypeStruct((2 * nd, t), F32)],
        compiler_params=_params("arbitrary"), name="gate_prep",
    )(h_bf, w_gate, w_gate.T, a_log.reshape(1, nd), dt_bias.reshape(1, nd),
      a_log.reshape(nd, 1), dt_bias.reshape(nd, 1))


def _select_col(x, idx):
    lane = lax.broadcasted_iota(jnp.int32, x.shape, 1)
    return jnp.sum(jnp.where(lane == idx, x, 0.0), axis=1, keepdims=True)


def _select_row(x, idx):
    row = lax.broadcasted_iota(jnp.int32, x.shape, 0)
    return jnp.sum(jnp.where(row == idx, x, 0.0), axis=0, keepdims=True)


def _run_interleaved(chains):
    chains = list(chains)
    while chains:
        alive = []
        for ch in chains:
            try:
                next(ch)
                alive.append(ch)
            except StopIteration:
                pass
        chains = alive


def _delta_chain(q, k, v, gc_col, gc_row, beta_col, s_ref, o_ref, cols, reverse, inv_passes):
    tl, kd_ = k.shape
    n_chunks = tl // CHUNK
    incl, strict = _tile_masks(tl, reverse)
    decay = jnp.exp(jnp.where(incl, gc_col - gc_row, NEG_BIG))
    kb, qb = k.astype(BF16), q.astype(BF16)
    kkt = _dot_nt(kb, kb)
    qkt = _dot_nt(qb, kb)
    yield
    x = jnp.where(strict, kkt * (-beta_col) * decay, 0.0)
    dot_inv = _dot3 if inv_passes == 3 else _dot
    ri = lax.broadcasted_iota(jnp.int32, (tl, tl), 0)
    ci = lax.broadcasted_iota(jnp.int32, (tl, tl), 1)
    r = jnp.where(ri == ci, 1.0, 0.0) + x
    n_sq = int(math.log2(CHUNK)) - 1
    for _ in range(n_sq):
        x = dot_inv(x, x)
        r = r + dot_inv(r, x)
        yield
    e_g = jnp.exp(gc_col)
    rhs = jnp.concatenate([v * beta_col, k * (beta_col * e_g)], axis=1)
    sol = dot_inv(r, rhs)
    yield
    u0, w = sol[:, :v.shape[1]], sol[:, v.shape[1]:]
    attn = (qkt * decay).astype(BF16)
    o0 = _dot(attn, u0)
    qe = q * e_g - _dot(attn, w)
    tot_rows = []
    for c in range(n_chunks):
        last = c * CHUNK if reverse else c * CHUNK + CHUNK - 1
        tot_rows.append(jnp.broadcast_to(gc_col[last:last + 1, :], (CHUNK, 1)))
    tot = jnp.concatenate(tot_rows, axis=0)
    kdec_t = jnp.transpose(k * jnp.exp(tot - gc_col))
    wu = jnp.concatenate([-w, u0], axis=1).astype(BF16)
    lane = lax.broadcasted_iota(jnp.int32, kdec_t.shape, 1)
    pns = [_dot(jnp.where((lane // CHUNK) == c, kdec_t, 0.0), wu) for c in range(n_chunks)]
    yield
    order = range(n_chunks - 1, -1, -1) if reverse else range(n_chunks)
    for c in order:
        rows = slice(c * CHUNK, (c + 1) * CHUNK)
        pn = pns[c]
        lhs = jnp.concatenate([qe[rows], pn[:, :kd_]], axis=0)
        s = s_ref[...]
        res = _dot(lhs, s)
        o_ref[rows, cols] = o0[rows] + res[:CHUNK]
        last = c * CHUNK if reverse else c * CHUNK + CHUNK - 1
        gl = jnp.exp(gc_col[last:last + 1, :])
        s_ref[...] = gl * s + res[CHUNK:] + pn[:, kd_:]
        yield


def _delta_body(qf_ref, kf_ref, vf_ref, gcf_ref, grf_ref, qb_ref, kb_ref, vb_ref, gcb_ref, grb_ref,
                of_ref, ob_ref, sf_ref, sb_ref, *, n_heads, head_dim, inv_passes):
    hg = pl.program_id(1)
    heads_per_step = sf_ref.shape[0]

    @pl.when(pl.program_id(2) == 0)
    def _():
        sf_ref[...] = jnp.zeros_like(sf_ref)
        sb_ref[...] = jnp.zeros_like(sb_ref)

    chains = []
    for g in range(heads_per_step):
        cols = slice(g * head_dim, (g + 1) * head_dim)
        for reverse, (q_ref, k_ref, v_ref, gc_ref, gr_ref, o_ref, s_ref) in enumerate(
                [(qf_ref, kf_ref, vf_ref, gcf_ref, grf_ref, of_ref, sf_ref),
                 (qb_ref, kb_ref, vb_ref, gcb_ref, grb_ref, ob_ref, sb_ref)]):
            idx = reverse * n_heads + hg * heads_per_step + g
            gcs = gc_ref[...]
            gc_col = _select_col(gcs, idx)
            beta_col = _select_col(gcs, 2 * n_heads + idx)
            gc_row = _select_row(gr_ref[...], idx)
            chains.append(_delta_chain(q_ref[:, cols], k_ref[:, cols], v_ref[:, cols], gc_col, gc_row, beta_col,
                                       s_ref.at[g], o_ref, cols, bool(reverse), inv_passes))
    _run_interleaved(chains)


def delta_scan(qkv, gc, gr, geom, ctx_len, n_heads, head_dim, inv_passes=3, heads_per_step=DELTA_HEADS_PER_STEP):
    n_lat, lat_len, n_batch = geom
    t = qkv.shape[0]
    tl = SCAN_TILE
    nl, nc, ctx0 = lat_len // tl, ctx_len // tl, n_lat // tl
    n_hg = n_heads // heads_per_step
    gw = heads_per_step * head_dim
    tile = functools.partial(_segment_tile, nl=nl, nc=nc, ctx_tile0=ctx0)
    specs = []
    for reverse in (False, True):
        tix = functools.partial(tile, reverse=reverse)
        specs += [pl.BlockSpec((tl, gw), lambda b, h, s, tix=tix: (tix(b, s), h)),
                  pl.BlockSpec((tl, gw), lambda b, h, s, tix=tix: (tix(b, s), n_hg + h)),
                  pl.BlockSpec((tl, gw), lambda b, h, s, tix=tix: (tix(b, s), 2 * n_hg + h)),
                  pl.BlockSpec((tl, 4 * n_heads), lambda b, h, s, tix=tix: (tix(b, s), 0)),
                  pl.BlockSpec((4 * n_heads, tl), lambda b, h, s, tix=tix: (0, tix(b, s)))]
    out_specs = [pl.BlockSpec((tl, gw), lambda b, h, s, tix=functools.partial(tile, reverse=r): (tix(b, s), h))
                 for r in (False, True)]
    body = functools.partial(_delta_body, n_heads=n_heads, head_dim=head_dim, inv_passes=inv_passes)
    return pl.pallas_call(
        body, grid=(n_batch, n_hg, nl + nc), in_specs=specs, out_specs=out_specs,
        out_shape=[jax.ShapeDtypeStruct((t, n_heads * head_dim), F32)] * 2,
        scratch_shapes=[pltpu.VMEM((heads_per_step, head_dim, head_dim), F32)] * 2,
        compiler_params=_params("arbitrary", "arbitrary", "arbitrary"), name="delta_scan",
    )(qkv, qkv, qkv, gc, gr, qkv, qkv, qkv, gc, gr)


def _hgrn_diag(q, k, cum, v, stage_ref, diag_ref, reverse):
    n_sub = q.shape[0] // SUB
    for n, val in enumerate((q, k, cum, v)):
        stage_ref[n] = val

    def slabs(n):
        return [stage_ref.at[n][pl.ds(r, n_sub, stride=SUB), :] for r in range(SUB)]

    q_x, k_x, p_x, v_x = slabs(0), slabs(1), slabs(2), slabs(3)
    pairs = [(i, j) for i in range(SUB) for j in range(SUB) if (j >= i if reverse else j <= i)]
    terms = []
    for i, j in pairs:
        qk = q_x[i] * k_x[j]
        terms.append((qk if i == j else qk * jnp.exp(p_x[i] - p_x[j])).astype(BF16))
    kd_ = terms[0].shape[1]
    a_rep = jnp.dot(jnp.concatenate(terms, axis=0), jnp.ones((kd_, kd_), BF16), preferred_element_type=F32)
    o_x = [None] * SUB
    for n, (i, j) in enumerate(pairs):
        contrib = a_rep[n * n_sub:(n + 1) * n_sub] * v_x[j]
        o_x[i] = contrib if o_x[i] is None else o_x[i] + contrib
    for r in range(SUB):
        diag_ref[pl.ds(r, n_sub, stride=SUB), :] = o_x[r]


def _hgrn_chain(hq_ref, hf_ref, hv_ref, cols, lb, st_ref, o_ref, stage_ref, diag_ref, reverse):
    hq, hf, hv = hq_ref[:, cols], hf_ref[:, cols], hv_ref[:, cols]
    tl, kd_ = hq.shape
    n_chunks = tl // CHUNK
    n_sub = tl // SUB
    sub_per_chunk = CHUNK // SUB
    q = _silu(hq)
    f = lb + (1.0 - lb) * _sigmoid(hf)
    k = 1.0 - f
    lf = jnp.log(f)
    incl, _ = _tile_masks(tl, reverse)
    cum = _dot_exact_lhs01(jnp.where(incl, 1.0, 0.0), lf)
    excl = cum - lf

    def bcast_rows(src, row, n):
        return jnp.broadcast_to(src[row:row + 1, :], (n, kd_))

    chunk_last = [(c * CHUNK if reverse else c * CHUNK + CHUNK - 1) for c in range(n_chunks)]
    tot = jnp.concatenate([bcast_rows(cum, chunk_last[c], CHUNK) for c in range(n_chunks)], axis=0)
    sub_first = [(m * SUB + SUB - 1 if reverse else m * SUB) for m in range(n_sub)]
    r_sub = jnp.concatenate([bcast_rows(excl, sub_first[m], SUB) for m in range(n_sub)], axis=0)
    q_t = q * jnp.exp(cum - r_sub)
    qd = q * jnp.exp(cum)
    kd = k * jnp.exp(tot - cum)
    vb = hv.astype(BF16)

    i = lax.broadcasted_iota(jnp.int32, (tl, tl), 0)
    j = lax.broadcasted_iota(jnp.int32, (tl, tl), 1)
    same = (i // CHUNK) == (j // CHUNK)
    pos_i = (i % CHUNK) // SUB
    pos_j = (j % CHUNK) // SUB
    if reverse:
        pos_i, pos_j = sub_per_chunk - 1 - pos_i, sub_per_chunk - 1 - pos_j
    a_off = jnp.zeros((tl, tl), F32)
    for lvl in range(1, sub_per_chunk):
        ref_rows = []
        for c in range(n_chunks):
            m = c * sub_per_chunk + (sub_per_chunk - 1 - lvl if reverse else lvl)
            ref_rows.append(bcast_rows(excl, sub_first[m], CHUNK))
        r_lvl = jnp.concatenate(ref_rows, axis=0)
        k_t = k * jnp.exp(jnp.minimum(r_lvl - cum, 0.0))
        a_l = _dot_nt(q_t, k_t)
        a_off = a_off + jnp.where(same & (pos_i == lvl) & (pos_j < lvl), a_l, 0.0)
        yield
    o_intra = _dot(a_off, vb)
    yield

    _hgrn_diag(q, k, cum, hv, stage_ref, diag_ref, reverse)
    yield
    o_intra = o_intra + diag_ref[...]

    v_t = jnp.transpose(hv)
    lane = lax.broadcasted_iota(jnp.int32, v_t.shape, 1)
    kdb = kd.astype(BF16)
    n_ts = [_dot(jnp.where((lane // CHUNK) == c, v_t, 0.0), kdb) for c in range(n_chunks)]
    yield
    order = range(n_chunks - 1, -1, -1) if reverse else range(n_chunks)
    for c in order:
        rows = slice(c * CHUNK, (c + 1) * CHUNK)
        st = st_ref[...]
        o_ref[rows, cols] = o_intra[rows] + _dot_nt(qd[rows], st)
        st_ref[...] = st * jnp.exp(cum[chunk_last[c]:chunk_last[c] + 1, :]) + n_ts[c]
        yield


def _hgrn_body(qf_ref, ff_ref, vf_ref, qb_ref, fb_ref, vb_ref, lb_ref, of_ref, ob_ref, sf_ref, sb_ref, stage_ref,
               diag_ref, *, key_dim):
    heads_per_step = sf_ref.shape[0]

    @pl.when(pl.program_id(2) == 0)
    def _():
        sf_ref[...] = jnp.zeros_like(sf_ref)
        sb_ref[...] = jnp.zeros_like(sb_ref)

    chains = []
    for g in range(heads_per_step):
        cols = slice(g * key_dim, (g + 1) * key_dim)
        lb = lb_ref[:, cols]
        chains.append(_hgrn_chain(qf_ref, ff_ref, vf_ref, cols, lb, sf_ref.at[g], of_ref,
                                  stage_ref.at[2 * g], diag_ref.at[2 * g], False))
        chains.append(_hgrn_chain(qb_ref, fb_ref, vb_ref, cols, lb, sb_ref.at[g], ob_ref,
                                  stage_ref.at[2 * g + 1], diag_ref.at[2 * g + 1], True))
    _run_interleaved(chains)


def hgrn_scan(proj_hg, lb, geom, ctx_len, n_heads, key_dim, heads_per_step=HGRN_HEADS_PER_STEP):
    n_lat, lat_len, n_batch = geom
    t = proj_hg.shape[0]
    tl = SCAN_TILE
    nl, nc, ctx0 = lat_len // tl, ctx_len // tl, n_lat // tl
    n_hg = n_heads // heads_per_step
    gw = heads_per_step * key_dim
    tile = functools.partial(_segment_tile, nl=nl, nc=nc, ctx_tile0=ctx0)
    specs = []
    for reverse in (False, True):
        tix = functools.partial(tile, reverse=reverse)
        fcol = (1 + int(reverse)) * n_hg
        specs += [pl.BlockSpec((tl, gw), lambda b, h, s, tix=tix: (tix(b, s), h)),
                  pl.BlockSpec((tl, gw), lambda b, h, s, tix=tix, fcol=fcol: (tix(b, s), fcol + h)),
                  pl.BlockSpec((tl, gw), lambda b, h, s, tix=tix: (tix(b, s), 3 * n_hg + h))]
    specs.append(pl.BlockSpec((1, gw), lambda b, h, s: (0, h)))
    out_specs = [pl.BlockSpec((tl, gw), lambda b, h, s, tix=functools.partial(tile, reverse=r): (tix(b, s), h))
                 for r in (False, True)]
    return pl.pallas_call(
        functools.partial(_hgrn_body, key_dim=key_dim), grid=(n_batch, n_hg, nl + nc),
        in_specs=specs, out_specs=out_specs,
        out_shape=[jax.ShapeDtypeStruct((t, n_heads * key_dim), F32)] * 2,
        scratch_shapes=[pltpu.VMEM((heads_per_step, key_dim, key_dim), F32)] * 2
        + [pltpu.VMEM((2 * heads_per_step, 4, tl, key_dim), F32), pltpu.VMEM((2 * heads_per_step, tl, key_dim), F32)],
        compiler_params=_params("arbitrary", "arbitrary", "arbitrary"), name="hgrn_scan",
    )(proj_hg, proj_hg, proj_hg, proj_hg, proj_hg, proj_hg, lb)


def _mix_out2_body(df_ref, db_ref, hf_ref, hb_ref, z_ref, og_ref, dnw_ref, hgw_ref, o_ref, *, head_dim):
    def normed(o, nw):
        outs = []
        for h in range(o.shape[1] // head_dim):
            oh = o[:, h * head_dim:(h + 1) * head_dim]
            outs.append(oh * lax.rsqrt(jnp.mean(oh * oh, axis=-1, keepdims=True) + NORM_EPS) * nw)
        return jnp.concatenate(outs, axis=1)

    dn = normed(df_ref[...] + db_ref[...], dnw_ref[...]) * _silu(z_ref[...])
    hg = normed(hf_ref[...] + hb_ref[...], hgw_ref[...]) * _sigmoid(og_ref[...])
    half = dn.shape[1]
    o_ref[:, :half] = dn.astype(o_ref.dtype)
    o_ref[:, half:] = hg.astype(o_ref.dtype)


def mix_out(dn_f, dn_b, hg_f, hg_b, z_src, z_blk, og_src, og_blk, dn_norm, hg_norm, head_dim):
    t, w = dn_f.shape
    tl = SCAN_TILE
    row = lambda i: (i, 0)
    body = functools.partial(_mix_out2_body, head_dim=head_dim)
    return pl.pallas_call(
        body, grid=(t // tl,),
        in_specs=[pl.BlockSpec((tl, w), row)] * 4
        + [pl.BlockSpec((tl, w), lambda i: (i, z_blk)), pl.BlockSpec((tl, w), lambda i: (i, og_blk)),
           pl.BlockSpec((1, head_dim), lambda i: (0, 0)), pl.BlockSpec((1, head_dim), lambda i: (0, 0))],
        out_specs=pl.BlockSpec((tl, 2 * w), row),
        out_shape=jax.ShapeDtypeStruct((t, 2 * w), BF16),
        compiler_params=_params("arbitrary"), name="mix_out",
    )(dn_f, dn_b, hg_f, hg_b, z_src, og_src, dn_norm.reshape(1, head_dim), hg_norm.reshape(1, head_dim))


def _first_max(x, ids, n):
    m = jnp.max(x, axis=0, keepdims=True)
    first = jnp.min(jnp.where(x == m, ids, n), axis=0, keepdims=True)
    return m, first


def _route_body(lg_ref, bias_ref, idx_ref, rank_ref, w_ref, cnt_ref, carry_ref, *,
                n_groups, topk_groups, top_k, scale):
    i = pl.program_id(0)

    @pl.when(i == 0)
    def _():
        carry_ref[...] = jnp.zeros_like(carry_ref)

    lg = lg_ref[...]
    n_exp, tl = lg.shape
    per = n_exp // n_groups
    scores = 1.0 / (1.0 + jnp.exp(-lg))
    biased = scores + bias_ref[...]
    sub = lax.broadcasted_iota(jnp.int32, (per, tl), 0)
    g_rows = []
    for g in range(n_groups):
        xg = biased[g * per:(g + 1) * per]
        m1, i1 = _first_max(xg, sub, per)
        m2 = jnp.max(jnp.where(sub == i1, NEG_INF, xg), axis=0, keepdims=True)
        g_rows.append(m1 + m2)
    gscore = jnp.concatenate(g_rows, axis=0)
    gid = lax.broadcasted_iota(jnp.int32, (n_groups, tl), 0)
    gsel = jnp.zeros((n_groups, tl), jnp.bool_)
    for _ in range(topk_groups):
        _, first = _first_max(gscore, gid, n_groups)
        hit = gid == first
        gsel = gsel | hit
        gscore = jnp.where(hit, NEG_INF, gscore)
    eid = lax.broadcasted_iota(jnp.int32, (n_exp, tl), 0)
    gmask = jnp.concatenate([jnp.broadcast_to(gsel[g:g + 1], (per, tl)) for g in range(n_groups)], axis=0)
    masked = jnp.where(gmask, biased, NEG_INF)
    sel = jnp.zeros((n_exp, tl), jnp.bool_)
    hits, firsts = [], []
    for _ in range(top_k):
        _, first = _first_max(masked, eid, n_exp)
        hit = eid == first
        hits.append(hit)
        firsts.append(first)
        sel = sel | hit
        masked = jnp.where(hit, NEG_INF, masked)
    self = jnp.where(sel, 1.0, 0.0)
    ti = lax.broadcasted_iota(jnp.int32, (tl, tl), 0)
    tj = lax.broadcasted_iota(jnp.int32, (tl, tl), 1)
    before = jnp.where(ti < tj, 1.0, 0.0).astype(BF16)
    carry = carry_ref[...]
    rank_full = jnp.dot(self.astype(BF16), before, preferred_element_type=F32) + carry
    w_rows = [jnp.sum(jnp.where(hit, scores, 0.0), axis=0, keepdims=True) for hit in hits]
    r_rows = [jnp.sum(jnp.where(hit, rank_full, 0.0), axis=0, keepdims=True) for hit in hits]
    w8 = jnp.concatenate(w_rows, axis=0)
    idx_ref[...] = jnp.concatenate(firsts, axis=0)
    rank_ref[...] = jnp.concatenate(r_rows, axis=0).astype(jnp.int32)
    w_ref[...] = w8 / jnp.sum(w8, axis=0, keepdims=True) * scale
    carry = carry + jnp.sum(self, axis=1, keepdims=True)
    carry_ref[...] = carry
    cnt_ref[...] = carry.astype(jnp.int32)


def route(logits_t, router_b, tl=256):
    n_exp, t = logits_t.shape
    body = functools.partial(_route_body, n_groups=N_GROUPS, topk_groups=TOPK_GROUPS, top_k=TOP_K,
                             scale=ROUTED_SCALE)
    tok_spec = pl.BlockSpec((TOP_K, tl), lambda i: (0, i))
    return pl.pallas_call(
        body, grid=(t // tl,),
        in_specs=[pl.BlockSpec((n_exp, tl), lambda i: (0, i)), pl.BlockSpec((n_exp, 1), lambda i: (0, 0))],
        out_specs=[tok_spec, tok_spec, tok_spec, pl.BlockSpec((n_exp, 1), lambda i: (0, 0))],
        out_shape=[jax.ShapeDtypeStruct((TOP_K, t), jnp.int32), jax.ShapeDtypeStruct((TOP_K, t), jnp.int32),
                   jax.ShapeDtypeStruct((TOP_K, t), F32), jax.ShapeDtypeStruct((n_exp, 1), jnp.int32)],
        scratch_shapes=[pltpu.VMEM((n_exp, 1), F32)],
        compiler_params=_params("arbitrary"), name="route",
    )(logits_t, router_b.reshape(n_exp, 1))


def _slab_copy(src_ref, src_tok, dst_ref, dst_tok, rows, sem):
    return pltpu.make_async_copy(src_ref.at[pl.ds(pl.multiple_of(src_tok * rows, rows), rows)],
                                 dst_ref.at[pl.ds(pl.multiple_of(dst_tok * rows, rows), rows)], sem)


def _dispatch_body(dest_ref, pad_ref, h_ref, xs_ref, zero_ref, sem, *, top_k, rows):
    tl = h_ref.shape[0] // rows
    n_exp = pad_ref.shape[1]

    def issue(r, carry):
        for k in range(top_k):
            _slab_copy(h_ref, r, xs_ref, dest_ref[r * top_k + k], rows, sem.at[0]).start(priority=k % 2)
        return carry

    lax.fori_loop(0, tl, issue, 0, unroll=DMA_LOOP_UNROLL)

    @pl.when(pl.program_id(0) == 0)
    def _():
        zero_ref[...] = jnp.zeros_like(zero_ref)
        largest = zero_ref.shape[0] // rows
        pieces = [largest >> b for b in range(largest.bit_length())]

        def pad_copies(e, wait):
            first, n_pad = pad_ref[0, e], pad_ref[1, e]
            for size in pieces:
                @pl.when((n_pad & size) != 0)
                def _():
                    slot0 = first + (n_pad & ~(2 * size - 1))
                    cp = pltpu.make_async_copy(
                        zero_ref.at[pl.ds(0, size * rows)],
                        xs_ref.at[pl.ds(pl.multiple_of(slot0 * rows, rows), size * rows)], sem.at[1])
                    if wait:
                        cp.wait()
                    else:
                        cp.start()

        def fill(e, carry):
            pad_copies(e, False)
            return carry

        def drain_pad(e, carry):
            pad_copies(e, True)
            return carry

        lax.fori_loop(0, n_exp, fill, 0)
        lax.fori_loop(0, n_exp, drain_pad, 0)

    def drain(r, carry):
        for k in range(top_k):
            _slab_copy(h_ref, 0, xs_ref, 0, rows, sem.at[0]).wait()
        return carry

    lax.fori_loop(0, tl, drain, 0, unroll=DMA_LOOP_UNROLL)


def dispatch(h_slab, dest_flat, pad_info, n_slot, top_k, rows, tl=256):
    t = h_slab.shape[0] // rows
    return pl.pallas_call(
        functools.partial(_dispatch_body, top_k=top_k, rows=rows), grid=(t // tl,),
        in_specs=[pl.BlockSpec((tl * top_k,), lambda i: (i,), memory_space=pltpu.SMEM),
                  pl.BlockSpec(memory_space=pltpu.SMEM),
                  pl.BlockSpec((tl * rows, LANES), lambda i: (i, 0))],
        out_specs=pl.BlockSpec(memory_space=pl.ANY),
        out_shape=jax.ShapeDtypeStruct((n_slot * rows, LANES), h_slab.dtype),
        scratch_shapes=[pltpu.VMEM((EXPERT_SLOT_BLOCK // 2 * rows, LANES), h_slab.dtype),
                        pltpu.SemaphoreType.DMA((2,))],
        compiler_params=_params("arbitrary"), name="dispatch",
    )(dest_flat, pad_info, h_slab)


def _combine_body(dest_ref, dest_next_ref, w_ref, sh_ref, x_ref, gate_ref, y_ref, o_ref, buf_ref, acc_ref, sem, *,
                  top_k, rows):
    i = pl.program_id(0)
    n = pl.num_programs(0)
    tl = x_ref.shape[0]

    def gather(d_ref, slot):
        def issue(r, carry):
            for k in range(top_k):
                _slab_copy(y_ref, d_ref[r * top_k + k], buf_ref.at[slot], k * tl + r, rows,
                           sem.at[slot]).start(priority=k % 2)
            return carry
        lax.fori_loop(0, tl, issue, 0, unroll=DMA_LOOP_UNROLL)

    @pl.when(i == 0)
    def _():
        gather(dest_ref, 0)

    @pl.when(i + 1 < n)
    def _():
        gather(dest_next_ref, (i + 1) % 2)

    slot = i % 2

    def drain(r, carry):
        for k in range(top_k):
            _slab_copy(y_ref, 0, buf_ref.at[slot], 0, rows, sem.at[slot]).wait()
        return carry

    lax.fori_loop(0, tl, drain, 0, unroll=DMA_LOOP_UNROLL)
    w = w_ref[...]
    acc = sh_ref[...]
    for k in range(top_k):
        wk = jnp.broadcast_to(w[:, k:k + 1], (tl, LANES))
        wk = jnp.broadcast_to(wk[:, None, :], (tl, rows, LANES)).reshape(tl * rows, LANES)
        acc = acc + buf_ref[slot, pl.ds(k * tl * rows, tl * rows), :] * wk
    acc_ref[...] = acc
    for c in range(rows):
        cols = slice(c * LANES, (c + 1) * LANES)
        o_ref[:, cols] = x_ref[:, cols] + gate_ref[0][:, cols] * acc_ref[pl.ds(c, tl, stride=rows), :]


def combine(y_slab, dest_flat, w, shared_slab, x, mod3, gate_idx, geom, tl=128):
    t, d = x.shape
    rows = d // LANES
    top_k = w.shape[1]
    n_lat, lat_len, n_batch = geom
    n_tiles = t // tl
    row = functools.partial(_mod_row, tile_rows=tl, n_lat=n_lat, lat_len=lat_len, n_batch=n_batch)
    return pl.pallas_call(
        functools.partial(_combine_body, top_k=top_k, rows=rows), grid=(n_tiles,),
        in_specs=[pl.BlockSpec((tl * top_k,), lambda i: (i,), memory_space=pltpu.SMEM),
                  pl.BlockSpec((tl * top_k,), lambda i: (jnp.minimum(i + 1, n_tiles - 1),),
                               memory_space=pltpu.SMEM),
                  pl.BlockSpec((tl, top_k), lambda i: (i, 0)),
                  pl.BlockSpec((tl * rows, LANES), lambda i: (i, 0)),
                  pl.BlockSpec((tl, d), lambda i: (i, 0)),
                  pl.BlockSpec((1, 1, d), lambda i: (row(i) * 6 + gate_idx, 0, 0)),
                  pl.BlockSpec(memory_space=pl.ANY)],
        out_specs=pl.BlockSpec((tl, d), lambda i: (i, 0)),
        out_shape=jax.ShapeDtypeStruct((t, d), F32),
        scratch_shapes=[pltpu.VMEM((2, top_k * tl * rows, LANES), F32), pltpu.VMEM((tl * rows, LANES), F32),
                        pltpu.SemaphoreType.DMA((2,))],
        compiler_params=_params("arbitrary"), name="combine",
    )(dest_flat, dest_flat, w, shared_slab, x, mod3, y_slab)


def moe_ffn(x, h_pk, logits_t, router_b, exp_gate_up, exp_down, shared_gate_up, shared_down, mod3, geom):
    t = x.shape[0]
    slab_rows = h_pk.shape[0] // t
    e_count = exp_gate_up.shape[0]
    blk = EXPERT_SLOT_BLOCK
    idx8, rank8, w8, counts = route(logits_t, router_b)
    counts = counts[:, 0]
    padded = (counts + blk - 1) // blk * blk
    pad_end = jnp.cumsum(padded)
    pad_start = pad_end - padded
    start8 = jnp.sum(jnp.where(idx8[None] == jnp.arange(e_count, dtype=jnp.int32)[:, None, None],
                               pad_start[:, None, None], 0), axis=0)
    dest_flat = (start8 + rank8).T.reshape(-1)
    n_blk = (t * TOP_K + e_count * (blk - 1)) // blk + 1
    n_slot = n_blk * blk
    blk_starts = jnp.arange(n_blk, dtype=jnp.int32) * blk
    blk_e = jnp.sum((pad_end[None, :] <= blk_starts[:, None]).astype(jnp.int32), axis=1)
    blk_e = jnp.minimum(blk_e, e_count - 1)
    n_used = (pad_end[-1] // blk).astype(jnp.int32).reshape(1)
    pad_info = jnp.stack([pad_start + counts, padded - counts]).astype(jnp.int32)
    x_sorted = dispatch(h_pk, dest_flat, pad_info, n_slot, TOP_K, slab_rows)
    y_slot = expert_blocks(x_sorted, exp_gate_up, exp_down, blk_e, n_used, blk)
    shared = expert_blocks(h_pk, shared_gate_up[None], shared_down[None], jnp.zeros((t // blk,), jnp.int32),
                           jnp.full((1,), t // blk, jnp.int32), blk)
    return combine(y_slot, dest_flat, w8.T, shared, x, mod3, 5, geom)


def kernel(x, c, ctx, c_ctx, hg_lb_logits, l0_mod_w, l0_mod_b, l0_norm1, l0_norm2, l0_w_in, l0_dn_conv, l0_dn_a_log, l0_dn_dt_bias, l0_dn_norm, l0_hg_norm, l0_w_out, l0_router_w, l0_router_b, l0_exp_gate_up, l0_exp_down, l0_shared_gate_up, l0_shared_down, l1_mod_w, l1_mod_b, l1_norm1, l1_norm2, l1_w_in, l1_q_norm, l1_k_norm, l1_lambda, l1_sub_norm, l1_w_out, l1_router_w, l1_router_b, l1_exp_gate_up, l1_exp_down, l1_shared_gate_up, l1_shared_down):
    n_batch, lat_len, d = x.shape
    ctx_len = ctx.shape[1]
    n_lat = n_batch * lat_len
    n_ctx = n_batch * ctx_len
    geom = (n_lat, lat_len, n_batch)
    geom_lat_only = (n_lat, lat_len, n_batch)

    xs = jnp.concatenate([x.reshape(n_lat, d), ctx.reshape(n_ctx, d)], axis=0)
    cond = jnp.concatenate([c, c_ctx[None], jnp.zeros((MOD_ROWS - n_batch - 1, d), F32)], axis=0)

    mod3 = modulation(cond, l0_mod_w, l0_mod_b).reshape(MOD_ROWS * 6, 1, d)
    h = adaln(xs, l0_norm1, mod3, 0, geom)
    n_dn = IN0_SIZES[0] + IN0_SIZES[1]
    n_gate = IN0_SIZES[2] + IN0_SIZES[3]
    proj_dn = matmul(h, l0_w_in, n_cols=n_dn)
    proj_hg = matmul(h, l0_w_in[:, n_dn + n_gate:].astype(BF16))
    gc, gr = gate_prep(h, l0_w_in[:, n_dn:n_dn + n_gate], l0_dn_a_log, l0_dn_dt_bias, DN_HEADS)
    conv_t = jnp.concatenate([l0_dn_conv.T, jnp.zeros((8 - l0_dn_conv.shape[1], 3 * DN_WIDTH), F32)], axis=0)
    qkv = dn_prep(proj_dn, conv_t, geom, ctx_len, DN_WIDTH, DN_HEAD_DIM)
    dn_f, dn_b = delta_scan(qkv, gc, gr, geom, ctx_len, DN_HEADS, DN_HEAD_DIM, inv_passes=DELTA_INV_PASSES)
    lb = jnp.cumsum(jax.nn.softmax(hg_lb_logits, axis=0), axis=0)[0:1]
    hg_f, hg_b = hgrn_scan(proj_hg, lb, geom, ctx_len, HG_HEADS, HG_KEY_DIM)
    y = mix_out(dn_f, dn_b, hg_f, hg_b, proj_dn, 3, proj_hg, 4, l0_dn_norm, l0_hg_norm, DN_HEAD_DIM)
    xs = matmul_residual(y, l0_w_out, xs, mod3, 2, geom)
    h, logits_t = adaln(xs, l0_norm2, mod3, 3, geom, router_w=l0_router_w)
    xs = moe_ffn(xs, h, logits_t, l0_router_b, l0_exp_gate_up, l0_exp_down, l0_shared_gate_up, l0_shared_down,
                 mod3, geom)

    mod3 = modulation(cond, l1_mod_w, l1_mod_b).reshape(MOD_ROWS * 6, 1, d)
    h = adaln(xs, l1_norm1, mod3, 0, geom)
    cos_t, sin_t = rope_tables(lat_len, n_batch, n_ctx, DA_HEAD_DIM)
    qkv = matmul_qkv(h, l1_w_in, l1_q_norm, l1_k_norm, cos_t, sin_t)
    lam_init = 0.8 - 0.6 * math.exp(-0.3 * 1)
    lmbda = (jnp.exp(jnp.sum(l1_lambda[0] * l1_lambda[1])) - jnp.exp(jnp.sum(l1_lambda[2] * l1_lambda[3]))
             + lam_init)
    y = diff_attention(qkv, lmbda, l1_sub_norm, n_batch, lat_len, ctx_len, DA_HEADS, DA_HEAD_DIM,
                       1.0 - lam_init)
    xl = matmul_residual(y, l1_w_out, xs, mod3, 2, geom_lat_only)
    h, logits_t = adaln(xl, l1_norm2, mod3, 3, geom_lat_only, router_w=l1_router_w)
    xl = moe_ffn(xl, h, logits_t, l1_router_b, l1_exp_gate_up, l1_exp_down, l1_shared_gate_up, l1_shared_down,
                 mod3, geom_lat_only)
    return xl.reshape(n_batch, lat_len, d)
```

```python
import functools
import math

import jax
import jax.numpy as jnp
import numpy as np
from jax import lax
from jax.experimental import pallas as pl
from jax.experimental.pallas import tpu as pltpu

F32 = jnp.float32
BF16 = jnp.bfloat16

NORM_EPS = 1e-6
GRID_W = 64
ROPE_BASE = 10000.0

DN_HEADS = 8
DN_HEAD_DIM = 128
DN_WIDTH = DN_HEADS * DN_HEAD_DIM
DN_CHUNK = 64
HG_HEADS = 8
HG_KEY_DIM = 128
HG_VAL_DIM = 128
HG_K_WIDTH = HG_HEADS * HG_KEY_DIM
HG_V_WIDTH = HG_HEADS * HG_VAL_DIM
HG_CHUNK = 64
IN0_SIZES = (3 * DN_WIDTH, DN_WIDTH, 2 * DN_HEADS, 2 * DN_HEADS,
             HG_K_WIDTH, 2 * HG_K_WIDTH, HG_V_WIDTH, HG_V_WIDTH)
DA_HEADS = 8
DA_HEAD_DIM = 128
N_EXPERTS = 64
TOP_K = 8
N_GROUPS = 8
TOPK_GROUPS = 4
EXPERT_FF = 512
ROUTED_SCALE = 2.5

VMEM_LIMIT_BYTES = 56 * 1024 * 1024
MOD_ROWS = 8
EXPERT_SLOT_BLOCK = 256
DMA_LOOP_UNROLL = 8
MM_TILE_M = 1024
MM_TILE_N = 1024
MM_EPILOGUE_CHUNKS = 4
LANES = 128
ATTN_TILE_Q = 512
ATTN_ROW_CHUNK = 128


def _params(*sem):
    return pltpu.CompilerParams(dimension_semantics=sem, vmem_limit_bytes=VMEM_LIMIT_BYTES)


def _sigmoid(x):
    return 0.5 * jnp.tanh(0.5 * x) + 0.5


def _silu(x):
    return x * _sigmoid(x)


def _mod_body(c_ref, w_ref, b_ref, o_ref):
    a = _silu(c_ref[...]).astype(BF16)
    o_ref[...] = jnp.dot(a, w_ref[...].astype(BF16), preferred_element_type=F32) + b_ref[...]


def modulation(cond, w, b, tn=1024):
    m, k = cond.shape
    n = w.shape[1]
    return pl.pallas_call(
        _mod_body, grid=(n // tn,),
        in_specs=[pl.BlockSpec((m, k), lambda j: (0, 0)),
                  pl.BlockSpec((k, tn), lambda j: (0, j)),
                  pl.BlockSpec((1, tn), lambda j: (0, j))],
        out_specs=pl.BlockSpec((m, tn), lambda j: (0, j)),
        out_shape=jax.ShapeDtypeStruct((m, n), F32),
        compiler_params=_params("arbitrary"), name="modulation",
    )(cond, w, b.reshape(1, n))


def _mod_row(tile, tile_rows, n_lat, lat_len, n_batch):
    start = tile * tile_rows
    return jnp.where(start < n_lat, start // lat_len, n_batch)


def _pack_bf16_pairs(h):
    half = h.shape[1] // 2
    bits = lax.bitcast_convert_type(h.astype(BF16).astype(F32), jnp.uint32)
    return (bits[:, :half] & jnp.uint32(0xFFFF0000)) | (bits[:, half:] >> jnp.uint32(16))


def _unpack_bf16_pairs(pk):
    hi = lax.bitcast_convert_type(pk & jnp.uint32(0xFFFF0000), F32).astype(BF16)
    lo = lax.bitcast_convert_type(pk << jnp.uint32(16), F32).astype(BF16)
    return hi, lo


def _store_slabs(ref, x, row0=0):
    r, w = x.shape
    c_n = w // LANES
    for c in range(c_n):
        ref[pl.ds(row0 * c_n + c, r, stride=c_n), :] = x[:, c * LANES:(c + 1) * LANES]


def _adaln_body(x_ref, nw_ref, shift_ref, scale_ref, o_ref):
    x = x_ref[...]
    y = x * lax.rsqrt(jnp.mean(x * x, axis=-1, keepdims=True) + NORM_EPS) * nw_ref[...]
    o_ref[...] = (y * (1.0 + scale_ref[0]) + shift_ref[0]).astype(o_ref.dtype)


def _adaln_router_body(x_ref, nw_ref, shift_ref, scale_ref, rw_ref, o_ref, lg_ref):
    x = x_ref[...]
    y = x * lax.rsqrt(jnp.mean(x * x, axis=-1, keepdims=True) + NORM_EPS) * nw_ref[...]
    h = y * (1.0 + scale_ref[0]) + shift_ref[0]
    _store_slabs(o_ref, _pack_bf16_pairs(h))
    lg_ref[...] = lax.dot_general(rw_ref[...], h, (((1,), (1,)), ((), ())), preferred_element_type=F32,
                                  precision=lax.Precision.HIGHEST)


def adaln(x, norm_w, mod3, shift_idx, geom, router_w=None, tl=256):
    t, d = x.shape
    n_lat, lat_len, n_batch = geom
    row = functools.partial(_mod_row, tile_rows=tl, n_lat=n_lat, lat_len=lat_len, n_batch=n_batch)
    in_specs = [pl.BlockSpec((tl, d), lambda i: (i, 0)),
                pl.BlockSpec((1, d), lambda i: (0, 0)),
                pl.BlockSpec((1, 1, d), lambda i: (row(i) * 6 + shift_idx, 0, 0)),
                pl.BlockSpec((1, 1, d), lambda i: (row(i) * 6 + shift_idx + 1, 0, 0))]
    args = [x, norm_w.reshape(1, d), mod3, mod3]
    if router_w is None:
        return pl.pallas_call(
            _adaln_body, grid=(t // tl,), in_specs=in_specs,
            out_specs=pl.BlockSpec((tl, d), lambda i: (i, 0)),
            out_shape=jax.ShapeDtypeStruct((t, d), BF16),
            compiler_params=_params("arbitrary"), name="adaln")(*args)
    e = router_w.shape[1]
    return pl.pallas_call(
        _adaln_router_body, grid=(t // tl,),
        in_specs=in_specs + [pl.BlockSpec((e, d), lambda i: (0, 0))],
        out_specs=[pl.BlockSpec((tl * (d // 2 // LANES), LANES), lambda i: (i, 0)),
                   pl.BlockSpec((e, tl), lambda i: (0, i))],
        out_shape=[jax.ShapeDtypeStruct((t * (d // 2 // LANES), LANES), jnp.uint32),
                   jax.ShapeDtypeStruct((e, t), F32)],
        compiler_params=_params("arbitrary"), name="adaln_router")(*args, router_w.T)


def _mm_body(a_ref, w_ref, o_ref, wb_ref):
    @pl.when(pl.program_id(1) == 0)
    def _():
        wb_ref[...] = w_ref[...].astype(BF16)
    o_ref[...] = jnp.dot(a_ref[...], wb_ref[...], preferred_element_type=F32).astype(o_ref.dtype)


def _mm_res_body(a_ref, w_ref, res_ref, gate_ref, o_ref, wb_ref):
    @pl.when(pl.program_id(1) == 0)
    def _():
        wb_ref[...] = w_ref[...].astype(BF16)
    acc = jnp.dot(a_ref[...], wb_ref[...], preferred_element_type=F32)
    o_ref[...] = res_ref[...] + gate_ref[0] * acc


def matmul(a, w, n_cols=None, col_block0=0, tm=MM_TILE_M, tn=MM_TILE_N, out_dtype=F32):
    m, k = a.shape
    n = w.shape[1] if n_cols is None else n_cols
    return pl.pallas_call(
        _mm_body, grid=(n // tn, m // tm),
        in_specs=[pl.BlockSpec((tm, k), lambda j, i: (i, 0)),
                  pl.BlockSpec((k, tn), lambda j, i: (0, j + col_block0))],
        out_specs=pl.BlockSpec((tm, tn), lambda j, i: (i, j)),
        out_shape=jax.ShapeDtypeStruct((m, n), out_dtype),
        scratch_shapes=[pltpu.VMEM((k, tn), BF16)],
        compiler_params=_params("arbitrary", "arbitrary"), name="matmul",
    )(a, w)


def matmul_residual(a, w, res, mod3, gate_idx, geom, tm=MM_TILE_M, tn=MM_TILE_N):
    m, k = a.shape
    n = w.shape[1]
    n_lat, lat_len, n_batch = geom
    row = functools.partial(_mod_row, tile_rows=tm, n_lat=n_lat, lat_len=lat_len, n_batch=n_batch)
    nb = n // tn
    return pl.pallas_call(
        _mm_res_body, grid=(n // tn, m // tm),
        in_specs=[pl.BlockSpec((tm, k), lambda j, i: (i, 0)),
                  pl.BlockSpec((k, tn), lambda j, i: (0, j)),
                  pl.BlockSpec((tm, tn), lambda j, i: (i, j)),
                  pl.BlockSpec((1, 1, tn), lambda j, i: (row(i) * 6 + gate_idx, 0, j))],
        out_specs=pl.BlockSpec((tm, tn), lambda j, i: (i, j)),
        out_shape=jax.ShapeDtypeStruct((m, n), F32),
        scratch_shapes=[pltpu.VMEM((k, tn), BF16)],
        compiler_params=_params("arbitrary", "arbitrary"), name="matmul_residual",
    )(a, w, res, mod3)


def _expert_body(blk_e_ref, blk_new_ref, next_e_ref, stage_ref, n_used_ref, x_ref, gu_hbm, dn_hbm, o_ref,
                 gu_stage, dn_stage, gub_ref, dnb_ref, sem):
    i = pl.program_id(0)

    def weight_copies(e, slot):
        return (pltpu.make_async_copy(gu_hbm.at[e], gu_stage.at[slot], sem.at[0, slot]),
                pltpu.make_async_copy(dn_hbm.at[e], dn_stage.at[slot], sem.at[1, slot]))

    @pl.when(i == 0)
    def _():
        for cp in weight_copies(blk_e_ref[0], 0):
            cp.start()

    @pl.when(blk_new_ref[i] == 1)
    def _():
        slot = stage_ref[i]
        for cp in weight_copies(blk_e_ref[i], slot):
            cp.wait()
        gub_ref[...] = gu_stage[slot].astype(BF16)
        dnb_ref[...] = dn_stage[slot].astype(BF16)

        @pl.when(next_e_ref[i] >= 0)
        def _():
            for cp in weight_copies(next_e_ref[i], 1 - slot):
                cp.start()

    @pl.when(i < n_used_ref[0])
    def _():
        ff, d = dnb_ref.shape
        half = d // 2
        blk = x_ref.shape[0] // (half // LANES)
        xc = half // LANES
        h1 = None
        for p in range(2):
            cs = range(p * xc // 2, (p + 1) * xc // 2)
            pk = jnp.concatenate([x_ref[pl.ds(c, blk, stride=xc), :] for c in cs], axis=1)
            x_hi, x_lo = _unpack_bf16_pairs(pk)
            k0 = p * half // 2
            part = (jnp.dot(x_hi, gub_ref[k0:k0 + half // 2], preferred_element_type=F32)
                    + jnp.dot(x_lo, gub_ref[half + k0:half + k0 + half // 2], preferred_element_type=F32))
            h1 = part if h1 is None else h1 + part
        act = (_silu(h1[:, :ff]) * h1[:, ff:]).astype(BF16)
        y = jnp.concatenate([jnp.dot(act, dnb_ref[:, p * half:(p + 1) * half], preferred_element_type=F32)
                             for p in range(2)], axis=1)
        _store_slabs(o_ref, _pack_bf16_pairs(y))

    @pl.when(i >= n_used_ref[0])
    def _():
        o_ref[...] = jnp.zeros_like(o_ref)


def expert_blocks(x, gate_up, down, blk_e, n_used, blk):
    _, d, f2 = gate_up.shape
    xc = yc = d // 2 // LANES
    s = x.shape[0] // xc
    n_blk = s // blk
    pos = jnp.arange(n_blk, dtype=jnp.int32)
    used = pos < n_used[0]
    blk_new = (jnp.concatenate([jnp.ones((1,), bool), blk_e[1:] != blk_e[:-1]]) & used).astype(jnp.int32)
    stage = (jnp.cumsum(blk_new) - 1) % 2
    first_pos = jnp.where(blk_new == 1, pos, n_blk)
    next_first = jnp.concatenate([lax.cummin(first_pos, reverse=True)[1:], jnp.full((1,), n_blk, jnp.int32)])
    next_e = jnp.where(next_first < n_blk, blk_e[jnp.minimum(next_first, n_blk - 1)], -1).astype(jnp.int32)
    grid_spec = pltpu.PrefetchScalarGridSpec(
        num_scalar_prefetch=5, grid=(n_blk,),
        in_specs=[pl.BlockSpec((blk * xc, LANES), lambda i, be, bn, ne, st, nu: (jnp.minimum(i, nu[0] - 1), 0)),
                  pl.BlockSpec(memory_space=pl.ANY),
                  pl.BlockSpec(memory_space=pl.ANY)],
        out_specs=pl.BlockSpec((blk * yc, LANES), lambda i, be, bn, ne, st, nu: (i, 0)),
        scratch_shapes=[pltpu.VMEM((2, d, f2), F32), pltpu.VMEM((2, f2 // 2, d), F32),
                        pltpu.VMEM((d, f2), BF16), pltpu.VMEM((f2 // 2, d), BF16),
                        pltpu.SemaphoreType.DMA((2, 2))])
    return pl.pallas_call(
        _expert_body, grid_spec=grid_spec,
        out_shape=jax.ShapeDtypeStruct((s * yc, LANES), jnp.uint32),
        compiler_params=_params("arbitrary"), name="expert_blocks",
    )(blk_e, blk_new, next_e, stage.astype(jnp.int32), n_used, x, gate_up, down)


def _norm_rope(x, w, cos, sin, first, head_dim):
    y = x * lax.rsqrt(jnp.mean(x * x, axis=-1, keepdims=True) + NORM_EPS) * w
    swapped = jnp.where(first, pltpu.roll(y, head_dim - head_dim // 4, 1), pltpu.roll(y, head_dim // 4, 1))
    return y * cos + swapped * sin


def _mm_qkv_body(a_ref, w_ref, qw_ref, kw_ref, cos_ref, sin_ref, o_ref, wb_ref, *, head_dim, n_q_tiles, n_k_tiles):
    j = pl.program_id(0)

    @pl.when(pl.program_id(1) == 0)
    def _():
        wb_ref[...] = w_ref[...].astype(BF16)

    tm, tn = o_ref.shape
    row_chunk = tm // MM_EPILOGUE_CHUNKS

    def rows_chain(rows, nw_ref):
        acc = jnp.dot(a_ref[rows, :], wb_ref[...], preferred_element_type=F32)
        yield
        if nw_ref is None:
            o_ref[rows, :] = acc.astype(o_ref.dtype)
            return
        cos, sin = cos_ref[rows, :], sin_ref[rows, :]
        lane = lax.broadcasted_iota(jnp.int32, cos.shape, 1)
        first = (lane % (head_dim // 2)) < (head_dim // 4)
        for g in range(tn // head_dim):
            sl = slice(g * head_dim, (g + 1) * head_dim)
            o_ref[rows, sl] = _norm_rope(acc[:, sl], nw_ref[...], cos, sin, first, head_dim).astype(o_ref.dtype)
            yield

    def tile(nw_ref):
        _run_interleaved(rows_chain(slice(r, r + row_chunk), nw_ref) for r in range(0, tm, row_chunk))

    @pl.when(j < n_q_tiles)
    def _():
        tile(qw_ref)

    @pl.when((j >= n_q_tiles) & (j < n_q_tiles + n_k_tiles))
    def _():
        tile(kw_ref)

    @pl.when(j >= n_q_tiles + n_k_tiles)
    def _():
        tile(None)


def matmul_qkv(a, w, q_norm, k_norm, cos_t, sin_t, tm=MM_TILE_M, tn=MM_TILE_N):
    m, k = a.shape
    n = w.shape[1]
    hd = q_norm.shape[0]
    body = functools.partial(_mm_qkv_body, head_dim=hd, n_q_tiles=n // 3 // tn, n_k_tiles=n // 3 // tn)
    return pl.pallas_call(
        body, grid=(n // tn, m // tm),
        in_specs=[pl.BlockSpec((tm, k), lambda j, i: (i, 0)),
                  pl.BlockSpec((k, tn), lambda j, i: (0, j)),
                  pl.BlockSpec((1, hd), lambda j, i: (0, 0)),
                  pl.BlockSpec((1, hd), lambda j, i: (0, 0)),
                  pl.BlockSpec((tm, hd), lambda j, i: (i, 0)),
                  pl.BlockSpec((tm, hd), lambda j, i: (i, 0))],
        out_specs=pl.BlockSpec((tm, tn), lambda j, i: (i, j)),
        out_shape=jax.ShapeDtypeStruct((m, n), BF16),
        scratch_shapes=[pltpu.VMEM((k, tn), BF16)],
        compiler_params=_params("arbitrary", "arbitrary"), name="matmul_qkv",
    )(a, w, q_norm.reshape(1, hd), k_norm.reshape(1, hd), cos_t, sin_t)


def rope_tables(n_lat_tokens_per_sample, n_batch, n_ctx_tokens, head_dim):
    quarter = head_dim // 4
    inv_freq = ROPE_BASE ** (-jnp.arange(quarter, dtype=F32) / quarter)
    rows = n_lat_tokens_per_sample // GRID_W
    row = jnp.repeat(jnp.arange(rows, dtype=F32), GRID_W)
    col = jnp.tile(jnp.arange(GRID_W, dtype=F32), rows)
    ang_r = row[:, None] * inv_freq[None, :]
    ang_c = col[:, None] * inv_freq[None, :]
    cos = jnp.concatenate([jnp.cos(ang_r), jnp.cos(ang_r), jnp.cos(ang_c), jnp.cos(ang_c)], axis=-1)
    sin = jnp.concatenate([-jnp.sin(ang_r), jnp.sin(ang_r), -jnp.sin(ang_c), jnp.sin(ang_c)], axis=-1)
    cos = jnp.concatenate([jnp.tile(cos, (n_batch, 1)), jnp.ones((n_ctx_tokens, head_dim), F32)], axis=0)
    sin = jnp.concatenate([jnp.tile(sin, (n_batch, 1)), jnp.zeros((n_ctx_tokens, head_dim), F32)], axis=0)
    return cos, sin


def _diff_attn_rows(lam, q_ref, k_all, v_all, sw_ref, o_ref, rows, head_dim, out_scale):
    c = head_dim ** -0.5 * math.log2(math.e)
    es, invs = [], []
    for s in range(2):
        sl = slice(s * head_dim, (s + 1) * head_dim)
        sc = lax.dot_general(q_ref[rows, sl], k_all[:, sl], (((1,), (1,)), ((), ())), preferred_element_type=F32)
        yield
        e = jnp.exp2((sc - jnp.max(sc, axis=-1, keepdims=True)) * c)
        invs.append(1.0 / jnp.sum(e, axis=-1, keepdims=True))
        es.append(e.astype(BF16))
        yield
    v = v_all[...]
    o0 = jnp.dot(es[0], v, preferred_element_type=F32)
    o1 = jnp.dot(es[1], v, preferred_element_type=F32)
    yield
    o = o0 * invs[0] - (lam * invs[1]) * o1
    y = o * lax.rsqrt(jnp.mean(o * o, axis=-1, keepdims=True) + NORM_EPS) * sw_ref[...]
    o_ref[rows, :] = (y * out_scale).astype(o_ref.dtype)
    yield


def _diff_attn_body(lam_ref, q_ref, kl_ref, kc_ref, vl_ref, vc_ref, sw_ref, o_ref, k_all, v_all, *,
                    head_dim, out_scale, row_chunk):
    @pl.when(pl.program_id(2) == 0)
    def _():
        n_l = kl_ref.shape[0]
        k_all[:n_l] = kl_ref[...]
        k_all[n_l:] = kc_ref[...]
        v_all[:n_l] = vl_ref[...]
        v_all[n_l:] = vc_ref[...]

    lam = lam_ref[0]
    tq = q_ref.shape[0]
    _run_interleaved(
        _diff_attn_rows(lam, q_ref, k_all, v_all, sw_ref, o_ref, slice(r, r + row_chunk), head_dim, out_scale)
        for r in range(0, tq, row_chunk))


def diff_attention(qkv, lmbda, sub_norm, n_batch, lat_len, ctx_len, n_heads, head_dim, out_scale,
                   tq=ATTN_TILE_Q, row_chunk=ATTN_ROW_CHUNK):
    hw = 2 * head_dim
    nq = lat_len // tq
    ctx_blk0 = n_batch * lat_len // ctx_len
    body = functools.partial(_diff_attn_body, head_dim=head_dim, out_scale=out_scale, row_chunk=row_chunk)
    n_keys = lat_len + ctx_len
    return pl.pallas_call(
        body, grid=(n_batch, n_heads, nq),
        in_specs=[pl.BlockSpec(memory_space=pltpu.SMEM),
                  pl.BlockSpec((tq, hw), lambda b, h, i: (b * nq + i, h)),
                  pl.BlockSpec((lat_len, hw), lambda b, h, i: (b, n_heads + h)),
                  pl.BlockSpec((ctx_len, hw), lambda b, h, i: (ctx_blk0 + b, n_heads + h)),
                  pl.BlockSpec((lat_len, hw), lambda b, h, i: (b, 2 * n_heads + h)),
                  pl.BlockSpec((ctx_len, hw), lambda b, h, i: (ctx_blk0 + b, 2 * n_heads + h)),
                  pl.BlockSpec((1, hw), lambda b, h, i: (0, 0))],
        out_specs=pl.BlockSpec((tq, hw), lambda b, h, i: (b * nq + i, h)),
        out_shape=jax.ShapeDtypeStruct((n_batch * lat_len, n_heads * hw), BF16),
        scratch_shapes=[pltpu.VMEM((n_keys, hw), BF16), pltpu.VMEM((n_keys, hw), BF16)],
        compiler_params=_params("arbitrary", "arbitrary", "arbitrary"), name="diff_attention",
    )(lmbda.reshape(1), qkv, qkv, qkv, qkv, qkv, sub_norm.reshape(1, hw))


SCAN_TILE = 256
CHUNK = 64
SUB = 16
NEG_BIG = -1e30
NEG_INF = float("-inf")
DELTA_INV_PASSES = 1
DELTA_HEADS_PER_STEP = 4
HGRN_HEADS_PER_STEP = 2


def _dot(a, b):
    return jnp.dot(a.astype(BF16), b.astype(BF16), preferred_element_type=F32)


def _dot_nt(a, b):
    return lax.dot_general(a.astype(BF16), b.astype(BF16), (((1,), (1,)), ((), ())),
                           preferred_element_type=F32)


def _split3(x):
    hi = x.astype(BF16)
    r = x - hi.astype(F32)
    mid = r.astype(BF16)
    lo = (r - mid.astype(F32)).astype(BF16)
    return hi, mid, lo


def _dot_exact_lhs01(m01, x):
    hi, mid, lo = _split3(x)
    m = m01.astype(BF16)
    return (jnp.dot(m, hi, preferred_element_type=F32) + jnp.dot(m, mid, preferred_element_type=F32)
            + jnp.dot(m, lo, preferred_element_type=F32))


def _dot_exact_rhs01(x, m01):
    hi, mid, lo = _split3(x)
    m = m01.astype(BF16)
    return (jnp.dot(hi, m, preferred_element_type=F32) + jnp.dot(mid, m, preferred_element_type=F32)
            + jnp.dot(lo, m, preferred_element_type=F32))


def _dot3(a, b):
    ah = a.astype(BF16)
    al = (a - ah.astype(F32)).astype(BF16)
    bh = b.astype(BF16)
    bl = (b - bh.astype(F32)).astype(BF16)
    return (jnp.dot(ah, bh, preferred_element_type=F32) + jnp.dot(ah, bl, preferred_element_type=F32)
            + jnp.dot(al, bh, preferred_element_type=F32))


def _tile_masks(n, reverse):
    i = lax.broadcasted_iota(jnp.int32, (n, n), 0)
    j = lax.broadcasted_iota(jnp.int32, (n, n), 1)
    same = (i // CHUNK) == (j // CHUNK)
    if reverse:
        return same & (i <= j), same & (i < j)
    return same & (i >= j), same & (i > j)


def _segment_tile(b, s, reverse, nl, nc, ctx_tile0):
    if reverse:
        return jnp.where(s < nc, ctx_tile0 + b * nc + (nc - 1 - s), b * nl + (nl - 1 - (s - nc)))
    return jnp.where(s < nc, ctx_tile0 + b * nc + s, b * nl + (s - nc))


def _dn_prep_body(x_ref, prev_ref, next_ref, w_ref, o_ref, *, tiles_per_lat_seg, tiles_per_ctx_seg, n_lat_tiles,
                  head_dim, q_scale):
    i = pl.program_id(0)
    j = pl.program_id(1)
    is_lat = i < n_lat_tiles
    pos = jnp.where(is_lat, i % tiles_per_lat_seg, (i - n_lat_tiles) % tiles_per_ctx_seg)
    seg_first = pos == 0
    seg_last = pos == jnp.where(is_lat, tiles_per_lat_seg, tiles_per_ctx_seg) - 1
    x = x_ref[...]
    tl = x.shape[0]
    prev = jnp.where(seg_first, 0.0, prev_ref[...])
    nxt = jnp.where(seg_last, 0.0, next_ref[...])
    xp = jnp.concatenate([prev, x, nxt], axis=0)
    w = w_ref[...]
    n_taps = 5
    acc = None
    for t in range(n_taps):
        off = 8 + t - n_taps // 2
        term = xp[off:off + tl] * w[t:t + 1]
        acc = term if acc is None else acc + term
    y = _silu(acc)
    scale = jnp.where(j == 0, q_scale, 1.0)
    outs = []
    for h in range(y.shape[1] // head_dim):
        yh = y[:, h * head_dim:(h + 1) * head_dim]
        nrm = lax.rsqrt(jnp.sum(yh * yh, axis=-1, keepdims=True) + 1e-6) * scale
        outs.append(yh * jnp.where(j == 2, 1.0, nrm))
    o_ref[...] = jnp.concatenate(outs, axis=1)


def dn_prep(proj_dn, conv_w_t, geom, ctx_len, width, head_dim):
    n_lat, lat_len, n_batch = geom
    t = proj_dn.shape[0]
    tl = SCAN_TILE
    rows8 = tl // 8
    n_tiles = t // tl
    body = functools.partial(_dn_prep_body, tiles_per_lat_seg=lat_len // tl, tiles_per_ctx_seg=ctx_len // tl,
                             n_lat_tiles=n_lat // tl,
                             head_dim=head_dim, q_scale=head_dim ** -0.5)
    last8 = t // 8 - 1
    return pl.pallas_call(
        body, grid=(n_tiles, 3),
        in_specs=[pl.BlockSpec((tl, width), lambda i, j: (i, j)),
                  pl.BlockSpec((8, width), lambda i, j: (jnp.maximum(i * rows8 - 1, 0), j)),
                  pl.BlockSpec((8, width), lambda i, j: (jnp.minimum((i + 1) * rows8, last8), j)),
                  pl.BlockSpec((8, width), lambda i, j: (0, j))],
        out_specs=pl.BlockSpec((tl, width), lambda i, j: (i, j)),
        out_shape=jax.ShapeDtypeStruct((t, 3 * width), F32),
        compiler_params=_params("arbitrary", "arbitrary"), name="dn_prep",
    )(proj_dn, proj_dn, proj_dn, conv_w_t)


def _softplus(x):
    return jnp.maximum(x, 0.0) + jnp.log(1.0 + jnp.exp(-jnp.abs(x)))


def _gate_prep_body(h_ref, wc_ref, wr_ref, alog_c_ref, dtb_c_ref, alog_r_ref, dtb_r_ref, gc_ref, gr_ref, *, n_heads):
    h = h_ref[...]
    tl = h.shape[0]
    nd = 2 * n_heads
    raw_c = jnp.dot(h, wc_ref[...].astype(BF16), preferred_element_type=F32)
    raw_r = lax.dot_general(wr_ref[...].astype(BF16), h, (((1,), (1,)), ((), ())),
                            preferred_element_type=F32)
    g_c = -jnp.exp(alog_c_ref[...]) * _softplus(raw_c[:, :nd] + dtb_c_ref[...])
    g_r = -jnp.exp(alog_r_ref[...]) * _softplus(raw_r[:nd, :] + dtb_r_ref[...])
    incl_f, _ = _tile_masks(tl, False)
    incl_b, _ = _tile_masks(tl, True)
    one_f = jnp.where(incl_f, 1.0, 0.0)
    one_b = jnp.where(incl_b, 1.0, 0.0)
    cum_c = jnp.concatenate([_dot_exact_lhs01(one_f, g_c[:, :n_heads]),
                             _dot_exact_lhs01(one_b, g_c[:, n_heads:])], axis=1)
    cum_r = jnp.concatenate([_dot_exact_rhs01(g_r[:n_heads, :], one_b),
                             _dot_exact_rhs01(g_r[n_heads:, :], one_f)], axis=0)
    beta_c = _sigmoid(raw_c[:, nd:])
    gc_ref[...] = jnp.concatenate([cum_c, beta_c], axis=1)
    gr_ref[...] = jnp.concatenate([cum_r, jnp.zeros_like(cum_r)], axis=0)


def gate_prep(h_bf, w_gate, a_log, dt_bias, n_heads):
    t, d = h_bf.shape
    tl = SCAN_TILE
    nd = 2 * n_heads
    body = functools.partial(_gate_prep_body, n_heads=n_heads)
    full = lambda shape: pl.BlockSpec(shape, lambda i: (0, 0))
    return pl.pallas_call(
        body, grid=(t // tl,),
        in_specs=[pl.BlockSpec((tl, d), lambda i: (i, 0)), full((d, 2 * nd)), full((2 * nd, d)),
                  full((1, nd)), full((1, nd)), full((nd, 1)), full((nd, 1))],
        out_specs=[pl.BlockSpec((tl, 2 * nd), lambda i: (i, 0)), pl.BlockSpec((2 * nd, tl), lambda i: (0, i))],
        out_shape=[jax.ShapeDtypeStruct((t, 2 * nd), F32), jax.ShapeDtypeStruct((2 * nd, t), F32)],
        compiler_params=_params("arbitrary"), name="gate_prep",
    )(h_bf, w_gate, w_gate.T, a_log.reshape(1, nd), dt_bias.reshape(1, nd),
      a_log.reshape(nd, 1), dt_bias.reshape(nd, 1))


def _select_col(x, idx):
    lane = lax.broadcasted_iota(jnp.int32, x.shape, 1)
    return jnp.sum(jnp.where(lane == idx, x, 0.0), axis=1, keepdims=True)


def _select_row(x, idx):
    row = lax.broadcasted_iota(jnp.int32, x.shape, 0)
    return jnp.sum(jnp.where(row == idx, x, 0.0), axis=0, keepdims=True)


def _run_interleaved(chains):
    chains = list(chains)
    while chains:
        alive = []
        for ch in chains:
            try:
                next(ch)
                alive.append(ch)
            except StopIteration:
                pass
        chains = alive


def _delta_chain(q, k, v, gc_col, gc_row, beta_col, s_ref, o_ref, cols, reverse, inv_passes):
    tl, kd_ = k.shape
    n_chunks = tl // CHUNK
    incl, strict = _tile_masks(tl, reverse)
    decay = jnp.exp(jnp.where(incl, gc_col - gc_row, NEG_BIG))
    kb, qb = k.astype(BF16), q.astype(BF16)
    kkt = _dot_nt(kb, kb)
    qkt = _dot_nt(qb, kb)
    yield
    x = jnp.where(strict, kkt * (-beta_col) * decay, 0.0)
    dot_inv = _dot3 if inv_passes == 3 else _dot
    ri = lax.broadcasted_iota(jnp.int32, (tl, tl), 0)
    ci = lax.broadcasted_iota(jnp.int32, (tl, tl), 1)
    r = jnp.where(ri == ci, 1.0, 0.0) + x
    n_sq = int(math.log2(CHUNK)) - 1
    for _ in range(n_sq):
        x = dot_inv(x, x)
        r = r + dot_inv(r, x)
        yield
    e_g = jnp.exp(gc_col)
    rhs = jnp.concatenate([v * beta_col, k * (beta_col * e_g)], axis=1)
    sol = dot_inv(r, rhs)
    yield
    u0, w = sol[:, :v.shape[1]], sol[:, v.shape[1]:]
    attn = (qkt * decay).astype(BF16)
    o0 = _dot(attn, u0)
    qe = q * e_g - _dot(attn, w)
    tot_rows = []
    for c in range(n_chunks):
        last = c * CHUNK if reverse else c * CHUNK + CHUNK - 1
        tot_rows.append(jnp.broadcast_to(gc_col[last:last + 1, :], (CHUNK, 1)))
    tot = jnp.concatenate(tot_rows, axis=0)
    kdec_t = jnp.transpose(k * jnp.exp(tot - gc_col))
    wu = jnp.concatenate([-w, u0], axis=1).astype(BF16)
    lane = lax.broadcasted_iota(jnp.int32, kdec_t.shape, 1)
    pns = [_dot(jnp.where((lane // CHUNK) == c, kdec_t, 0.0), wu) for c in range(n_chunks)]
    yield
    order = range(n_chunks - 1, -1, -1) if reverse else range(n_chunks)
    for c in order:
        rows = slice(c * CHUNK, (c + 1) * CHUNK)
        pn = pns[c]
        lhs = jnp.concatenate([qe[rows], pn[:, :kd_]], axis=0)
        s = s_ref[...]
        res = _dot(lhs, s)
        o_ref[rows, cols] = o0[rows] + res[:CHUNK]
        last = c * CHUNK if reverse else c * CHUNK + CHUNK - 1
        gl = jnp.exp(gc_col[last:last + 1, :])
        s_ref[...] = gl * s + res[CHUNK:] + pn[:, kd_:]
        yield


def _delta_body(qf_ref, kf_ref, vf_ref, gcf_ref, grf_ref, qb_ref, kb_ref, vb_ref, gcb_ref, grb_ref,
                of_ref, ob_ref, sf_ref, sb_ref, *, n_heads, head_dim, inv_passes):
    hg = pl.program_id(1)
    heads_per_step = sf_ref.shape[0]

    @pl.when(pl.program_id(2) == 0)
    def _():
        sf_ref[...] = jnp.zeros_like(sf_ref)
        sb_ref[...] = jnp.zeros_like(sb_ref)

    chains = []
    for g in range(heads_per_step):
        cols = slice(g * head_dim, (g + 1) * head_dim)
        for reverse, (q_ref, k_ref, v_ref, gc_ref, gr_ref, o_ref, s_ref) in enumerate(
                [(qf_ref, kf_ref, vf_ref, gcf_ref, grf_ref, of_ref, sf_ref),
                 (qb_ref, kb_ref, vb_ref, gcb_ref, grb_ref, ob_ref, sb_ref)]):
            idx = reverse * n_heads + hg * heads_per_step + g
            gcs = gc_ref[...]
            gc_col = _select_col(gcs, idx)
            beta_col = _select_col(gcs, 2 * n_heads + idx)
            gc_row = _select_row(gr_ref[...], idx)
            chains.append(_delta_chain(q_ref[:, cols], k_ref[:, cols], v_ref[:, cols], gc_col, gc_row, beta_col,
                                       s_ref.at[g], o_ref, cols, bool(reverse), inv_passes))
    _run_interleaved(chains)


def delta_scan(qkv, gc, gr, geom, ctx_len, n_heads, head_dim, inv_passes=3, heads_per_step=DELTA_HEADS_PER_STEP):
    n_lat, lat_len, n_batch = geom
    t = qkv.shape[0]
    tl = SCAN_TILE
    nl, nc, ctx0 = lat_len // tl, ctx_len // tl, n_lat // tl
    n_hg = n_heads // heads_per_step
    gw = heads_per_step * head_dim
    tile = functools.partial(_segment_tile, nl=nl, nc=nc, ctx_tile0=ctx0)
    specs = []
    for reverse in (False, True):
        tix = functools.partial(tile, reverse=reverse)
        specs += [pl.BlockSpec((tl, gw), lambda b, h, s, tix=tix: (tix(b, s), h)),
                  pl.BlockSpec((tl, gw), lambda b, h, s, tix=tix: (tix(b, s), n_hg + h)),
                  pl.BlockSpec((tl, gw), lambda b, h, s, tix=tix: (tix(b, s), 2 * n_hg + h)),
                  pl.BlockSpec((tl, 4 * n_heads), lambda b, h, s, tix=tix: (tix(b, s), 0)),
                  pl.BlockSpec((4 * n_heads, tl), lambda b, h, s, tix=tix: (0, tix(b, s)))]
    out_specs = [pl.BlockSpec((tl, gw), lambda b, h, s, tix=functools.partial(tile, reverse=r): (tix(b, s), h))
                 for r in (False, True)]
    body = functools.partial(_delta_body, n_heads=n_heads, head_dim=head_dim, inv_passes=inv_passes)
    return pl.pallas_call(
        body, grid=(n_batch, n_hg, nl + nc), in_specs=specs, out_specs=out_specs,
        out_shape=[jax.ShapeDtypeStruct((t, n_heads * head_dim), F32)] * 2,
        scratch_shapes=[pltpu.VMEM((heads_per_step, head_dim, head_dim), F32)] * 2,
        compiler_params=_params("arbitrary", "arbitrary", "arbitrary"), name="delta_scan",
    )(qkv, qkv, qkv, gc, gr, qkv, qkv, qkv, gc, gr)


def _hgrn_diag(q, k, cum, v, stage_ref, diag_ref, reverse):
    n_sub = q.shape[0] // SUB
    for n, val in enumerate((q, k, cum, v)):
        stage_ref[n] = val

    def slabs(n):
        return [stage_ref.at[n][pl.ds(r, n_sub, stride=SUB), :] for r in range(SUB)]

    q_x, k_x, p_x, v_x = slabs(0), slabs(1), slabs(2), slabs(3)
    pairs = [(i, j) for i in range(SUB) for j in range(SUB) if (j >= i if reverse else j <= i)]
    terms = []
    for i, j in pairs:
        qk = q_x[i] * k_x[j]
        terms.append((qk if i == j else qk * jnp.exp(p_x[i] - p_x[j])).astype(BF16))
    kd_ = terms[0].shape[1]
    a_rep = jnp.dot(jnp.concatenate(terms, axis=0), jnp.ones((kd_, kd_), BF16), preferred_element_type=F32)
    o_x = [None] * SUB
    for n, (i, j) in enumerate(pairs):
        contrib = a_rep[n * n_sub:(n + 1) * n_sub] * v_x[j]
        o_x[i] = contrib if o_x[i] is None else o_x[i] + contrib
    for r in range(SUB):
        diag_ref[pl.ds(r, n_sub, stride=SUB), :] = o_x[r]


def _hgrn_chain(hq_ref, hf_ref, hv_ref, cols, lb, st_ref, o_ref, stage_ref, diag_ref, reverse):
    hq, hf, hv = hq_ref[:, cols], hf_ref[:, cols], hv_ref[:, cols]
    tl, kd_ = hq.shape
    n_chunks = tl // CHUNK
    n_sub = tl // SUB
    sub_per_chunk = CHUNK // SUB
    q = _silu(hq)
    f = lb + (1.0 - lb) * _sigmoid(hf)
    k = 1.0 - f
    lf = jnp.log(f)
    incl, _ = _tile_masks(tl, reverse)
    cum = _dot_exact_lhs01(jnp.where(incl, 1.0, 0.0), lf)
    excl = cum - lf

    def bcast_rows(src, row, n):
        return jnp.broadcast_to(src[row:row + 1, :], (n, kd_))

    chunk_last = [(c * CHUNK if reverse else c * CHUNK + CHUNK - 1) for c in range(n_chunks)]
    tot = jnp.concatenate([bcast_rows(cum, chunk_last[c], CHUNK) for c in range(n_chunks)], axis=0)
    sub_first = [(m * SUB + SUB - 1 if reverse else m * SUB) for m in range(n_sub)]
    r_sub = jnp.concatenate([bcast_rows(excl, sub_first[m], SUB) for m in range(n_sub)], axis=0)
    q_t = q * jnp.exp(cum - r_sub)
    qd = q * jnp.exp(cum)
    kd = k * jnp.exp(tot - cum)
    vb = hv.astype(BF16)

    i = lax.broadcasted_iota(jnp.int32, (tl, tl), 0)
    j = lax.broadcasted_iota(jnp.int32, (tl, tl), 1)
    same = (i // CHUNK) == (j // CHUNK)
    pos_i = (i % CHUNK) // SUB
    pos_j = (j % CHUNK) // SUB
    if reverse:
        pos_i, pos_j = sub_per_chunk - 1 - pos_i, sub_per_chunk - 1 - pos_j
    a_off = jnp.zeros((tl, tl), F32)
    for lvl in range(1, sub_per_chunk):
        ref_rows = []
        for c in range(n_chunks):
            m = c * sub_per_chunk + (sub_per_chunk - 1 - lvl if reverse else lvl)
            ref_rows.append(bcast_rows(excl, sub_first[m], CHUNK))
        r_lvl = jnp.concatenate(ref_rows, axis=0)
        k_t = k * jnp.exp(jnp.minimum(r_lvl - cum, 0.0))
        a_l = _dot_nt(q_t, k_t)
        a_off = a_off + jnp.where(same & (pos_i == lvl) & (pos_j < lvl), a_l, 0.0)
        yield
    o_intra = _dot(a_off, vb)
    yield

    _hgrn_diag(q, k, cum, hv, stage_ref, diag_ref, reverse)
    yield
    o_intra = o_intra + diag_ref[...]

    v_t = jnp.transpose(hv)
    lane = lax.broadcasted_iota(jnp.int32, v_t.shape, 1)
    kdb = kd.astype(BF16)
    n_ts = [_dot(jnp.where((lane // CHUNK) == c, v_t, 0.0), kdb) for c in range(n_chunks)]
    yield
    order = range(n_chunks - 1, -1, -1) if reverse else range(n_chunks)
    for c in order:
        rows = slice(c * CHUNK, (c + 1) * CHUNK)
        st = st_ref[...]
        o_ref[rows, cols] = o_intra[rows] + _dot_nt(qd[rows], st)
        st_ref[...] = st * jnp.exp(cum[chunk_last[c]:chunk_last[c] + 1, :]) + n_ts[c]
        yield


def _hgrn_body(qf_ref, ff_ref, vf_ref, qb_ref, fb_ref, vb_ref, lb_ref, of_ref, ob_ref, sf_ref, sb_ref, stage_ref,
               diag_ref, *, key_dim):
    heads_per_step = sf_ref.shape[0]

    @pl.when(pl.program_id(2) == 0)
    def _():
        sf_ref[...] = jnp.zeros_like(sf_ref)
        sb_ref[...] = jnp.zeros_like(sb_ref)

    chains = []
    for g in range(heads_per_step):
        cols = slice(g * key_dim, (g + 1) * key_dim)
        lb = lb_ref[:, cols]
        chains.append(_hgrn_chain(qf_ref, ff_ref, vf_ref, cols, lb, sf_ref.at[g], of_ref,
                                  stage_ref.at[2 * g], diag_ref.at[2 * g], False))
        chains.append(_hgrn_chain(qb_ref, fb_ref, vb_ref, cols, lb, sb_ref.at[g], ob_ref,
                                  stage_ref.at[2 * g + 1], diag_ref.at[2 * g + 1], True))
    _run_interleaved(chains)


def hgrn_scan(proj_hg, lb, geom, ctx_len, n_heads, key_dim, heads_per_step=HGRN_HEADS_PER_STEP):
    n_lat, lat_len, n_batch = geom
    t = proj_hg.shape[0]
    tl = SCAN_TILE
    nl, nc, ctx0 = lat_len // tl, ctx_len // tl, n_lat // tl
    n_hg = n_heads // heads_per_step
    gw = heads_per_step * key_dim
    tile = functools.partial(_segment_tile, nl=nl, nc=nc, ctx_tile0=ctx0)
    specs = []
    for reverse in (False, True):
        tix = functools.partial(tile, reverse=reverse)
        fcol = (1 + int(reverse)) * n_hg
        specs += [pl.BlockSpec((tl, gw), lambda b, h, s, tix=tix: (tix(b, s), h)),
                  pl.BlockSpec((tl, gw), lambda b, h, s, tix=tix, fcol=fcol: (tix(b, s), fcol + h)),
                  pl.BlockSpec((tl, gw), lambda b, h, s, tix=tix: (tix(b, s), 3 * n_hg + h))]
    specs.append(pl.BlockSpec((1, gw), lambda b, h, s: (0, h)))
    out_specs = [pl.BlockSpec((tl, gw), lambda b, h, s, tix=functools.partial(tile, reverse=r): (tix(b, s), h))
                 for r in (False, True)]
    return pl.pallas_call(
        functools.partial(_hgrn_body, key_dim=key_dim), grid=(n_batch, n_hg, nl + nc),
        in_specs=specs, out_specs=out_specs,
        out_shape=[jax.ShapeDtypeStruct((t, n_heads * key_dim), F32)] * 2,
        scratch_shapes=[pltpu.VMEM((heads_per_step, key_dim, key_dim), F32)] * 2
        + [pltpu.VMEM((2 * heads_per_step, 4, tl, key_dim), F32), pltpu.VMEM((2 * heads_per_step, tl, key_dim), F32)],
        compiler_params=_params("arbitrary", "arbitrary", "arbitrary"), name="hgrn_scan",
    )(proj_hg, proj_hg, proj_hg, proj_hg, proj_hg, proj_hg, lb)


def _mix_out2_body(df_ref, db_ref, hf_ref, hb_ref, z_ref, og_ref, dnw_ref, hgw_ref, o_ref, *, head_dim):
    def normed(o, nw):
        outs = []
        for h in range(o.shape[1] // head_dim):
            oh = o[:, h * head_dim:(h + 1) * head_dim]
            outs.append(oh * lax.rsqrt(jnp.mean(oh * oh, axis=-1, keepdims=True) + NORM_EPS) * nw)
        return jnp.concatenate(outs, axis=1)

    dn = normed(df_ref[...] + db_ref[...], dnw_ref[...]) * _silu(z_ref[...])
    hg = normed(hf_ref[...] + hb_ref[...], hgw_ref[...]) * _sigmoid(og_ref[...])
    half = dn.shape[1]
    o_ref[:, :half] = dn.astype(o_ref.dtype)
    o_ref[:, half:] = hg.astype(o_ref.dtype)


def mix_out(dn_f, dn_b, hg_f, hg_b, z_src, z_blk, og_src, og_blk, dn_norm, hg_norm, head_dim):
    t, w = dn_f.shape
    tl = SCAN_TILE
    row = lambda i: (i, 0)
    body = functools.partial(_mix_out2_body, head_dim=head_dim)
    return pl.pallas_call(
        body, grid=(t // tl,),
        in_specs=[pl.BlockSpec((tl, w), row)] * 4
        + [pl.BlockSpec((tl, w), lambda i: (i, z_blk)), pl.BlockSpec((tl, w), lambda i: (i, og_blk)),
           pl.BlockSpec((1, head_dim), lambda i: (0, 0)), pl.BlockSpec((1, head_dim), lambda i: (0, 0))],
        out_specs=pl.BlockSpec((tl, 2 * w), row),
        out_shape=jax.ShapeDtypeStruct((t, 2 * w), BF16),
        compiler_params=_params("arbitrary"), name="mix_out",
    )(dn_f, dn_b, hg_f, hg_b, z_src, og_src, dn_norm.reshape(1, head_dim), hg_norm.reshape(1, head_dim))


def _first_max(x, ids, n):
    m = jnp.max(x, axis=0, keepdims=True)
    first = jnp.min(jnp.where(x == m, ids, n), axis=0, keepdims=True)
    return m, first


def _route_body(lg_ref, bias_ref, idx_ref, rank_ref, w_ref, cnt_ref, carry_ref, *,
                n_groups, topk_groups, top_k, scale):
    i = pl.program_id(0)

    @pl.when(i == 0)
    def _():
        carry_ref[...] = jnp.zeros_like(carry_ref)

    lg = lg_ref[...]
    n_exp, tl = lg.shape
    per = n_exp // n_groups
    scores = 1.0 / (1.0 + jnp.exp(-lg))
    biased = scores + bias_ref[...]
    sub = lax.broadcasted_iota(jnp.int32, (per, tl), 0)
    g_rows = []
    for g in range(n_groups):
        xg = biased[g * per:(g + 1) * per]
        m1, i1 = _first_max(xg, sub, per)
        m2 = jnp.max(jnp.where(sub == i1, NEG_INF, xg), axis=0, keepdims=True)
        g_rows.append(m1 + m2)
    gscore = jnp.concatenate(g_rows, axis=0)
    gid = lax.broadcasted_iota(jnp.int32, (n_groups, tl), 0)
    gsel = jnp.zeros((n_groups, tl), jnp.bool_)
    for _ in range(topk_groups):
        _, first = _first_max(gscore, gid, n_groups)
        hit = gid == first
        gsel = gsel | hit
        gscore = jnp.where(hit, NEG_INF, gscore)
    eid = lax.broadcasted_iota(jnp.int32, (n_exp, tl), 0)
    gmask = jnp.concatenate([jnp.broadcast_to(gsel[g:g + 1], (per, tl)) for g in range(n_groups)], axis=0)
    masked = jnp.where(gmask, biased, NEG_INF)
    sel = jnp.zeros((n_exp, tl), jnp.bool_)
    hits, firsts = [], []
    for _ in range(top_k):
        _, first = _first_max(masked, eid, n_exp)
        hit = eid == first
        hits.append(hit)
        firsts.append(first)
        sel = sel | hit
        masked = jnp.where(hit, NEG_INF, masked)
    self = jnp.where(sel, 1.0, 0.0)
    ti = lax.broadcasted_iota(jnp.int32, (tl, tl), 0)
    tj = lax.broadcasted_iota(jnp.int32, (tl, tl), 1)
    before = jnp.where(ti < tj, 1.0, 0.0).astype(BF16)
    carry = carry_ref[...]
    rank_full = jnp.dot(self.astype(BF16), before, preferred_element_type=F32) + carry
    w_rows = [jnp.sum(jnp.where(hit, scores, 0.0), axis=0, keepdims=True) for hit in hits]
    r_rows = [jnp.sum(jnp.where(hit, rank_full, 0.0), axis=0, keepdims=True) for hit in hits]
    w8 = jnp.concatenate(w_rows, axis=0)
    idx_ref[...] = jnp.concatenate(firsts, axis=0)
    rank_ref[...] = jnp.concatenate(r_rows, axis=0).astype(jnp.int32)
    w_ref[...] = w8 / jnp.sum(w8, axis=0, keepdims=True) * scale
    carry = carry + jnp.sum(self, axis=1, keepdims=True)
    carry_ref[...] = carry
    cnt_ref[...] = carry.astype(jnp.int32)


def route(logits_t, router_b, tl=256):
    n_exp, t = logits_t.shape
    body = functools.partial(_route_body, n_groups=N_GROUPS, topk_groups=TOPK_GROUPS, top_k=TOP_K,
                             scale=ROUTED_SCALE)
    tok_spec = pl.BlockSpec((TOP_K, tl), lambda i: (0, i))
    return pl.pallas_call(
        body, grid=(t // tl,),
        in_specs=[pl.BlockSpec((n_exp, tl), lambda i: (0, i)), pl.BlockSpec((n_exp, 1), lambda i: (0, 0))],
        out_specs=[tok_spec, tok_spec, tok_spec, pl.BlockSpec((n_exp, 1), lambda i: (0, 0))],
        out_shape=[jax.ShapeDtypeStruct((TOP_K, t), jnp.int32), jax.ShapeDtypeStruct((TOP_K, t), jnp.int32),
                   jax.ShapeDtypeStruct((TOP_K, t), F32), jax.ShapeDtypeStruct((n_exp, 1), jnp.int32)],
        scratch_shapes=[pltpu.VMEM((n_exp, 1), F32)],
        compiler_params=_params("arbitrary"), name="route",
    )(logits_t, router_b.reshape(n_exp, 1))


def _slab_copy(src_ref, src_tok, dst_ref, dst_tok, rows, sem):
    return pltpu.make_async_copy(src_ref.at[pl.ds(pl.multiple_of(src_tok * rows, rows), rows)],
                                 dst_ref.at[pl.ds(pl.multiple_of(dst_tok * rows, rows), rows)], sem)


def _dispatch_body(dest_ref, pad_ref, h_ref, xs_ref, zero_ref, sem, *, top_k, rows):
    tl = h_ref.shape[0] // rows
    n_exp = pad_ref.shape[1]

    def issue(r, carry):
        for k in range(top_k):
            _slab_copy(h_ref, r, xs_ref, dest_ref[r * top_k + k], rows, sem.at[0]).start(priority=k % 2)
        return carry

    lax.fori_loop(0, tl, issue, 0, unroll=DMA_LOOP_UNROLL)

    @pl.when(pl.program_id(0) == 0)
    def _():
        zero_ref[...] = jnp.zeros_like(zero_ref)
        largest = zero_ref.shape[0] // rows
        pieces = [largest >> b for b in range(largest.bit_length())]

        def pad_copies(e, wait):
            first, n_pad = pad_ref[0, e], pad_ref[1, e]
            for size in pieces:
                @pl.when((n_pad & size) != 0)
                def _():
                    slot0 = first + (n_pad & ~(2 * size - 1))
                    cp = pltpu.make_async_copy(
                        zero_ref.at[pl.ds(0, size * rows)],
                        xs_ref.at[pl.ds(pl.multiple_of(slot0 * rows, rows), size * rows)], sem.at[1])
                    if wait:
                        cp.wait()
                    else:
                        cp.start()

        def fill(e, carry):
            pad_copies(e, False)
            return carry

        def drain_pad(e, carry):
            pad_copies(e, True)
            return carry

        lax.fori_loop(0, n_exp, fill, 0)
        lax.fori_loop(0, n_exp, drain_pad, 0)

    def drain(r, carry):
        for k in range(top_k):
            _slab_copy(h_ref, 0, xs_ref, 0, rows, sem.at[0]).wait()
        return carry

    lax.fori_loop(0, tl, drain, 0, unroll=DMA_LOOP_UNROLL)


def dispatch(h_slab, dest_flat, pad_info, n_slot, top_k, rows, tl=256):
    t = h_slab.shape[0] // rows
    return pl.pallas_call(
        functools.partial(_dispatch_body, top_k=top_k, rows=rows), grid=(t // tl,),
        in_specs=[pl.BlockSpec((tl * top_k,), lambda i: (i,), memory_space=pltpu.SMEM),
                  pl.BlockSpec(memory_space=pltpu.SMEM),
                  pl.BlockSpec((tl * rows, LANES), lambda i: (i, 0))],
        out_specs=pl.BlockSpec(memory_space=pl.ANY),
        out_shape=jax.ShapeDtypeStruct((n_slot * rows, LANES), h_slab.dtype),
        scratch_shapes=[pltpu.VMEM((EXPERT_SLOT_BLOCK // 2 * rows, LANES), h_slab.dtype),
                        pltpu.SemaphoreType.DMA((2,))],
        compiler_params=_params("arbitrary"), name="dispatch",
    )(dest_flat, pad_info, h_slab)


def _combine_body(dest_ref, dest_next_ref, w_ref, sh_ref, x_ref, gate_ref, y_ref, o_ref, buf_ref, acc_ref, sem, *,
                  top_k, rows):
    i = pl.program_id(0)
    n = pl.num_programs(0)
    tl = x_ref.shape[0]

    def gather(d_ref, slot):
        def issue(r, carry):
            for k in range(top_k):
                _slab_copy(y_ref, d_ref[r * top_k + k], buf_ref.at[slot], k * tl + r, rows,
                           sem.at[slot]).start(priority=k % 2)
            return carry
        lax.fori_loop(0, tl, issue, 0, unroll=DMA_LOOP_UNROLL)

    @pl.when(i == 0)
    def _():
        gather(dest_ref, 0)

    @pl.when(i + 1 < n)
    def _():
        gather(dest_next_ref, (i + 1) % 2)

    slot = i % 2

    def drain(r, carry):
        for k in range(top_k):
            _slab_copy(y_ref, 0, buf_ref.at[slot], 0, rows, sem.at[slot]).wait()
        return carry

    lax.fori_loop(0, tl, drain, 0, unroll=DMA_LOOP_UNROLL)
    def unpack_f32(pk):
        return (lax.bitcast_convert_type(pk & jnp.uint32(0xFFFF0000), F32),
                lax.bitcast_convert_type(pk << jnp.uint32(16), F32))

    w = w_ref[...]
    acc_hi, acc_lo = unpack_f32(sh_ref[...])
    for k in range(top_k):
        wk = jnp.broadcast_to(w[:, k:k + 1], (tl, LANES))
        wk = jnp.broadcast_to(wk[:, None, :], (tl, rows, LANES)).reshape(tl * rows, LANES)
        y_hi, y_lo = unpack_f32(buf_ref[slot, pl.ds(k * tl * rows, tl * rows), :])
        acc_hi = acc_hi + y_hi * wk
        acc_lo = acc_lo + y_lo * wk
    acc_ref[0] = acc_hi
    acc_ref[1] = acc_lo
    for half in range(2):
        for c in range(rows):
            cols = slice((half * rows + c) * LANES, (half * rows + c + 1) * LANES)
            o_ref[:, cols] = (x_ref[:, cols]
                              + gate_ref[0][:, cols] * acc_ref.at[half][pl.ds(c, tl, stride=rows), :])


def combine(y_slab, dest_flat, w, shared_slab, x, mod3, gate_idx, geom, tl=128):
    t, d = x.shape
    rows = d // 2 // LANES
    top_k = w.shape[1]
    n_lat, lat_len, n_batch = geom
    n_tiles = t // tl
    row = functools.partial(_mod_row, tile_rows=tl, n_lat=n_lat, lat_len=lat_len, n_batch=n_batch)
    return pl.pallas_call(
        functools.partial(_combine_body, top_k=top_k, rows=rows), grid=(n_tiles,),
        in_specs=[pl.BlockSpec((tl * top_k,), lambda i: (i,), memory_space=pltpu.SMEM),
                  pl.BlockSpec((tl * top_k,), lambda i: (jnp.minimum(i + 1, n_tiles - 1),),
                               memory_space=pltpu.SMEM),
                  pl.BlockSpec((tl, top_k), lambda i: (i, 0)),
                  pl.BlockSpec((tl * rows, LANES), lambda i: (i, 0)),
                  pl.BlockSpec((tl, d), lambda i: (i, 0)),
                  pl.BlockSpec((1, 1, d), lambda i: (row(i) * 6 + gate_idx, 0, 0)),
                  pl.BlockSpec(memory_space=pl.ANY)],
        out_specs=pl.BlockSpec((tl, d), lambda i: (i, 0)),
        out_shape=jax.ShapeDtypeStruct((t, d), F32),
        scratch_shapes=[pltpu.VMEM((2, top_k * tl * rows, LANES), jnp.uint32),
                        pltpu.VMEM((2, tl * rows, LANES), F32), pltpu.SemaphoreType.DMA((2,))],
        compiler_params=_params("arbitrary"), name="combine",
    )(dest_flat, dest_flat, w, shared_slab, x, mod3, y_slab)


def moe_ffn(x, h_pk, logits_t, router_b, exp_gate_up, exp_down, shared_gate_up, shared_down, mod3, geom):
    t = x.shape[0]
    slab_rows = h_pk.shape[0] // t
    e_count = exp_gate_up.shape[0]
    blk = EXPERT_SLOT_BLOCK
    idx8, rank8, w8, counts = route(logits_t, router_b)
    counts = counts[:, 0]
    padded = (counts + blk - 1) // blk * blk
    pad_end = jnp.cumsum(padded)
    pad_start = pad_end - padded
    start8 = jnp.sum(jnp.where(idx8[None] == jnp.arange(e_count, dtype=jnp.int32)[:, None, None],
                               pad_start[:, None, None], 0), axis=0)
    dest_flat = (start8 + rank8).T.reshape(-1)
    n_blk = (t * TOP_K + e_count * (blk - 1)) // blk + 1
    n_slot = n_blk * blk
    blk_starts = jnp.arange(n_blk, dtype=jnp.int32) * blk
    blk_e = jnp.sum((pad_end[None, :] <= blk_starts[:, None]).astype(jnp.int32), axis=1)
    blk_e = jnp.minimum(blk_e, e_count - 1)
    n_used = (pad_end[-1] // blk).astype(jnp.int32).reshape(1)
    pad_info = jnp.stack([pad_start + counts, padded - counts]).astype(jnp.int32)
    x_sorted = dispatch(h_pk, dest_flat, pad_info, n_slot, TOP_K, slab_rows)
    y_slot = expert_blocks(x_sorted, exp_gate_up, exp_down, blk_e, n_used, blk)
    shared = expert_blocks(h_pk, shared_gate_up[None], shared_down[None], jnp.zeros((t // blk,), jnp.int32),
                           jnp.full((1,), t // blk, jnp.int32), blk)
    return combine(y_slot, dest_flat, w8.T, shared, x, mod3, 5, geom)


def kernel(x, c, ctx, c_ctx, hg_lb_logits, l0_mod_w, l0_mod_b, l0_norm1, l0_norm2, l0_w_in, l0_dn_conv, l0_dn_a_log, l0_dn_dt_bias, l0_dn_norm, l0_hg_norm, l0_w_out, l0_router_w, l0_router_b, l0_exp_gate_up, l0_exp_down, l0_shared_gate_up, l0_shared_down, l1_mod_w, l1_mod_b, l1_norm1, l1_norm2, l1_w_in, l1_q_norm, l1_k_norm, l1_lambda, l1_sub_norm, l1_w_out, l1_router_w, l1_router_b, l1_exp_gate_up, l1_exp_down, l1_shared_gate_up, l1_shared_down):
    n_batch, lat_len, d = x.shape
    ctx_len = ctx.shape[1]
    n_lat = n_batch * lat_len
    n_ctx = n_batch * ctx_len
    geom = (n_lat, lat_len, n_batch)
    geom_lat_only = (n_lat, lat_len, n_batch)

    xs = jnp.concatenate([x.reshape(n_lat, d), ctx.reshape(n_ctx, d)], axis=0)
    cond = jnp.concatenate([c, c_ctx[None], jnp.zeros((MOD_ROWS - n_batch - 1, d), F32)], axis=0)

    mod3 = modulation(cond, l0_mod_w, l0_mod_b).reshape(MOD_ROWS * 6, 1, d)
    h = adaln(xs, l0_norm1, mod3, 0, geom)
    n_dn = IN0_SIZES[0] + IN0_SIZES[1]
    n_gate = IN0_SIZES[2] + IN0_SIZES[3]
    proj_dn = matmul(h, l0_w_in, n_cols=n_dn)
    proj_hg = matmul(h, l0_w_in[:, n_dn + n_gate:])
    gc, gr = gate_prep(h, l0_w_in[:, n_dn:n_dn + n_gate], l0_dn_a_log, l0_dn_dt_bias, DN_HEADS)
    conv_t = jnp.concatenate([l0_dn_conv.T, jnp.zeros((8 - l0_dn_conv.shape[1], 3 * DN_WIDTH), F32)], axis=0)
    qkv = dn_prep(proj_dn, conv_t, geom, ctx_len, DN_WIDTH, DN_HEAD_DIM)
    dn_f, dn_b = delta_scan(qkv, gc, gr, geom, ctx_len, DN_HEADS, DN_HEAD_DIM, inv_passes=DELTA_INV_PASSES)
    lb = jnp.cumsum(jax.nn.softmax(hg_lb_logits, axis=0), axis=0)[0:1]
    hg_f, hg_b = hgrn_scan(proj_hg, lb, geom, ctx_len, HG_HEADS, HG_KEY_DIM)
    y = mix_out(dn_f, dn_b, hg_f, hg_b, proj_dn, 3, proj_hg, 4, l0_dn_norm, l0_hg_norm, DN_HEAD_DIM)
    xs = matmul_residual(y, l0_w_out, xs, mod3, 2, geom)
    h, logits_t = adaln(xs, l0_norm2, mod3, 3, geom, router_w=l0_router_w)
    xs = moe_ffn(xs, h, logits_t, l0_router_b, l0_exp_gate_up, l0_exp_down, l0_shared_gate_up, l0_shared_down,
                 mod3, geom)

    mod3 = modulation(cond, l1_mod_w, l1_mod_b).reshape(MOD_ROWS * 6, 1, d)
    h = adaln(xs, l1_norm1, mod3, 0, geom)
    cos_t, sin_t = rope_tables(lat_len, n_batch, n_ctx, DA_HEAD_DIM)
    qkv = matmul_qkv(h, l1_w_in, l1_q_norm, l1_k_norm, cos_t, sin_t)
    lam_init = 0.8 - 0.6 * math.exp(-0.3 * 1)
    lmbda = (jnp.exp(jnp.sum(l1_lambda[0] * l1_lambda[1])) - jnp.exp(jnp.sum(l1_lambda[2] * l1_lambda[3]))
             + lam_init)
    y = diff_attention(qkv, lmbda, l1_sub_norm, n_batch, lat_len, ctx_len, DA_HEADS, DA_HEAD_DIM,
                       1.0 - lam_init)
    xl = matmul_residual(y, l1_w_out, xs, mod3, 2, geom_lat_only)
    h, logits_t = adaln(xl, l1_norm2, mod3, 3, geom_lat_only, router_w=l1_router_w)
    xl = moe_ffn(xl, h, logits_t, l1_router_b, l1_exp_gate_up, l1_exp_down, l1_shared_gate_up, l1_shared_down,
                 mod3, geom_lat_only)
    return xl.reshape(n_batch, lat_len, d)
```

```python
import functools
import math

import jax
import jax.numpy as jnp
import numpy as np
from jax import lax
from jax.experimental import pallas as pl
from jax.experimental.pallas import tpu as pltpu

F32 = jnp.float32
BF16 = jnp.bfloat16

NORM_EPS = 1e-6
GRID_W = 64
ROPE_BASE = 10000.0

DN_HEADS = 8
DN_HEAD_DIM = 128
DN_WIDTH = DN_HEADS * DN_HEAD_DIM
DN_CHUNK = 64
HG_HEADS = 8
HG_KEY_DIM = 128
HG_VAL_DIM = 128
HG_K_WIDTH = HG_HEADS * HG_KEY_DIM
HG_V_WIDTH = HG_HEADS * HG_VAL_DIM
HG_CHUNK = 64
IN0_SIZES = (3 * DN_WIDTH, DN_WIDTH, 2 * DN_HEADS, 2 * DN_HEADS,
             HG_K_WIDTH, 2 * HG_K_WIDTH, HG_V_WIDTH, HG_V_WIDTH)
DA_HEADS = 8
DA_HEAD_DIM = 128
N_EXPERTS = 64
TOP_K = 8
N_GROUPS = 8
TOPK_GROUPS = 4
EXPERT_FF = 512
ROUTED_SCALE = 2.5

VMEM_LIMIT_BYTES = 56 * 1024 * 1024
MOD_ROWS = 8
EXPERT_SLOT_BLOCK = 256
DMA_LOOP_UNROLL = 8
MM_TILE_M = 1024
MM_TILE_N = 1024
EXPERT_BLOCKS_PER_STEP = 4
MM_EPILOGUE_CHUNKS = 4
LANES = 128
ATTN_TILE_Q = 512
ATTN_ROW_CHUNK = 128


def _params(*sem):
    return pltpu.CompilerParams(dimension_semantics=sem, vmem_limit_bytes=VMEM_LIMIT_BYTES)


def _sigmoid(x):
    return 0.5 * jnp.tanh(0.5 * x) + 0.5


def _silu(x):
    return x * _sigmoid(x)


def _mod_body(c_ref, w_ref, b_ref, o_ref):
    a = _silu(c_ref[...]).astype(BF16)
    o_ref[...] = jnp.dot(a, w_ref[...].astype(BF16), preferred_element_type=F32) + b_ref[...]


def modulation(cond, w, b, tn=1024):
    m, k = cond.shape
    n = w.shape[1]
    return pl.pallas_call(
        _mod_body, grid=(n // tn,),
        in_specs=[pl.BlockSpec((m, k), lambda j: (0, 0)),
                  pl.BlockSpec((k, tn), lambda j: (0, j)),
                  pl.BlockSpec((1, tn), lambda j: (0, j))],
        out_specs=pl.BlockSpec((m, tn), lambda j: (0, j)),
        out_shape=jax.ShapeDtypeStruct((m, n), F32),
        compiler_params=_params("arbitrary"), name="modulation",
    )(cond, w, b.reshape(1, n))


def _mod_row(tile, tile_rows, n_lat, lat_len, n_batch):
    start = tile * tile_rows
    return jnp.where(start < n_lat, start // lat_len, n_batch)


def _pack_bf16_pairs(h):
    half = h.shape[1] // 2
    bits = lax.bitcast_convert_type(h.astype(BF16).astype(F32), jnp.uint32)
    return (bits[:, :half] & jnp.uint32(0xFFFF0000)) | (bits[:, half:] >> jnp.uint32(16))


def _unpack_bf16_pairs(pk):
    hi = lax.bitcast_convert_type(pk & jnp.uint32(0xFFFF0000), F32).astype(BF16)
    lo = lax.bitcast_convert_type(pk << jnp.uint32(16), F32).astype(BF16)
    return hi, lo


def _store_slabs(ref, x, row0=0):
    r, w = x.shape
    c_n = w // LANES
    for c in range(c_n):
        ref[pl.ds(row0 * c_n + c, r, stride=c_n), :] = x[:, c * LANES:(c + 1) * LANES]


def _adaln_body(x_ref, nw_ref, shift_ref, scale_ref, o_ref):
    x = x_ref[...]
    y = x * lax.rsqrt(jnp.mean(x * x, axis=-1, keepdims=True) + NORM_EPS) * nw_ref[...]
    o_ref[...] = (y * (1.0 + scale_ref[0]) + shift_ref[0]).astype(o_ref.dtype)


def _adaln_router_body(x_ref, nw_ref, shift_ref, scale_ref, rw_ref, o_ref, lg_ref):
    x = x_ref[...]
    y = x * lax.rsqrt(jnp.mean(x * x, axis=-1, keepdims=True) + NORM_EPS) * nw_ref[...]
    h = y * (1.0 + scale_ref[0]) + shift_ref[0]
    _store_slabs(o_ref, _pack_bf16_pairs(h))
    lg_ref[...] = lax.dot_general(rw_ref[...], h, (((1,), (1,)), ((), ())), preferred_element_type=F32,
                                  precision=lax.Precision.HIGHEST)


def adaln(x, norm_w, mod3, shift_idx, geom, router_w=None, tl=256):
    t, d = x.shape
    n_lat, lat_len, n_batch = geom
    row = functools.partial(_mod_row, tile_rows=tl, n_lat=n_lat, lat_len=lat_len, n_batch=n_batch)
    in_specs = [pl.BlockSpec((tl, d), lambda i: (i, 0)),
                pl.BlockSpec((1, d), lambda i: (0, 0)),
                pl.BlockSpec((1, 1, d), lambda i: (row(i) * 6 + shift_idx, 0, 0)),
                pl.BlockSpec((1, 1, d), lambda i: (row(i) * 6 + shift_idx + 1, 0, 0))]
    args = [x, norm_w.reshape(1, d), mod3, mod3]
    if router_w is None:
        return pl.pallas_call(
            _adaln_body, grid=(t // tl,), in_specs=in_specs,
            out_specs=pl.BlockSpec((tl, d), lambda i: (i, 0)),
            out_shape=jax.ShapeDtypeStruct((t, d), BF16),
            compiler_params=_params("arbitrary"), name="adaln")(*args)
    e = router_w.shape[1]
    return pl.pallas_call(
        _adaln_router_body, grid=(t // tl,),
        in_specs=in_specs + [pl.BlockSpec((e, d), lambda i: (0, 0))],
        out_specs=[pl.BlockSpec((tl * (d // 2 // LANES), LANES), lambda i: (i, 0)),
                   pl.BlockSpec((e, tl), lambda i: (0, i))],
        out_shape=[jax.ShapeDtypeStruct((t * (d // 2 // LANES), LANES), jnp.uint32),
                   jax.ShapeDtypeStruct((e, t), F32)],
        compiler_params=_params("arbitrary"), name="adaln_router")(*args, router_w.T)


def _mm_body(a_ref, w_ref, o_ref, wb_ref):
    @pl.when(pl.program_id(1) == 0)
    def _():
        wb_ref[...] = w_ref[...].astype(BF16)
    o_ref[...] = jnp.dot(a_ref[...], wb_ref[...], preferred_element_type=F32).astype(o_ref.dtype)


def _mm_res_body(a_ref, w_ref, res_ref, gate_ref, o_ref, wb_ref):
    @pl.when(pl.program_id(1) == 0)
    def _():
        wb_ref[...] = w_ref[...].astype(BF16)
    acc = jnp.dot(a_ref[...], wb_ref[...], preferred_element_type=F32)
    o_ref[...] = res_ref[...] + gate_ref[0] * acc


def matmul(a, w, n_cols=None, col_block0=0, tm=MM_TILE_M, tn=MM_TILE_N, out_dtype=F32):
    m, k = a.shape
    n = w.shape[1] if n_cols is None else n_cols
    return pl.pallas_call(
        _mm_body, grid=(n // tn, m // tm),
        in_specs=[pl.BlockSpec((tm, k), lambda j, i: (i, 0)),
                  pl.BlockSpec((k, tn), lambda j, i: (0, j + col_block0))],
        out_specs=pl.BlockSpec((tm, tn), lambda j, i: (i, j)),
        out_shape=jax.ShapeDtypeStruct((m, n), out_dtype),
        scratch_shapes=[pltpu.VMEM((k, tn), BF16)],
        compiler_params=_params("arbitrary", "arbitrary"), name="matmul",
    )(a, w)


def matmul_residual(a, w, res, mod3, gate_idx, geom, tm=MM_TILE_M, tn=MM_TILE_N):
    m, k = a.shape
    n = w.shape[1]
    n_lat, lat_len, n_batch = geom
    row = functools.partial(_mod_row, tile_rows=tm, n_lat=n_lat, lat_len=lat_len, n_batch=n_batch)
    nb = n // tn
    return pl.pallas_call(
        _mm_res_body, grid=(n // tn, m // tm),
        in_specs=[pl.BlockSpec((tm, k), lambda j, i: (i, 0)),
                  pl.BlockSpec((k, tn), lambda j, i: (0, j)),
                  pl.BlockSpec((tm, tn), lambda j, i: (i, j)),
                  pl.BlockSpec((1, 1, tn), lambda j, i: (row(i) * 6 + gate_idx, 0, j))],
        out_specs=pl.BlockSpec((tm, tn), lambda j, i: (i, j)),
        out_shape=jax.ShapeDtypeStruct((m, n), F32),
        scratch_shapes=[pltpu.VMEM((k, tn), BF16)],
        compiler_params=_params("arbitrary", "arbitrary"), name="matmul_residual",
    )(a, w, res, mod3)


def _expert_body(blk_e_ref, blk_new_ref, next_e_ref, stage_ref, n_used_ref, x_ref, gu_hbm, dn_hbm, o_ref,
                 gu_stage, dn_stage, gub_ref, dnb_ref, sem):
    ff, d = dnb_ref.shape
    half = d // 2
    xc = half // LANES
    blk = x_ref.shape[0] // xc // EXPERT_BLOCKS_PER_STEP

    def weight_copies(e, slot):
        return (pltpu.make_async_copy(gu_hbm.at[e], gu_stage.at[slot], sem.at[0, slot]),
                pltpu.make_async_copy(dn_hbm.at[e], dn_stage.at[slot], sem.at[1, slot]))

    @pl.when(pl.program_id(0) == 0)
    def _():
        for cp in weight_copies(blk_e_ref[0], 0):
            cp.start()

    def one_block(b, sub):
        @pl.when(blk_new_ref[b] == 1)
        def _():
            slot = stage_ref[b]
            for cp in weight_copies(blk_e_ref[b], slot):
                cp.wait()
            gub_ref[...] = gu_stage[slot].astype(BF16)
            dnb_ref[...] = dn_stage[slot].astype(BF16)

            @pl.when(next_e_ref[b] >= 0)
            def _():
                for cp in weight_copies(next_e_ref[b], 1 - slot):
                    cp.start()

        @pl.when(b < n_used_ref[0])
        def _():
            h1 = None
            for p in range(2):
                cs = range(p * xc // 2, (p + 1) * xc // 2)
                pk = jnp.concatenate([x_ref[pl.ds(sub * blk * xc + c, blk, stride=xc), :] for c in cs], axis=1)
                x_hi, x_lo = _unpack_bf16_pairs(pk)
                k0 = p * half // 2
                part = (jnp.dot(x_hi, gub_ref[k0:k0 + half // 2], preferred_element_type=F32)
                        + jnp.dot(x_lo, gub_ref[half + k0:half + k0 + half // 2], preferred_element_type=F32))
                h1 = part if h1 is None else h1 + part
            act = (_silu(h1[:, :ff]) * h1[:, ff:]).astype(BF16)
            y = jnp.concatenate([jnp.dot(act, dnb_ref[:, p * half:(p + 1) * half], preferred_element_type=F32)
                                 for p in range(2)], axis=1)
            _store_slabs(o_ref, _pack_bf16_pairs(y), sub * blk)

        @pl.when(b >= n_used_ref[0])
        def _():
            o_ref[pl.ds(sub * blk * xc, blk * xc), :] = jnp.zeros((blk * xc, LANES), o_ref.dtype)

    for sub in range(EXPERT_BLOCKS_PER_STEP):
        one_block(pl.program_id(0) * EXPERT_BLOCKS_PER_STEP + sub, sub)


def expert_blocks(x, gate_up, down, blk_e, n_used, blk):
    _, d, f2 = gate_up.shape
    xc = yc = d // 2 // LANES
    s = x.shape[0] // xc
    n_blk = s // blk
    assert n_blk % EXPERT_BLOCKS_PER_STEP == 0, (n_blk, EXPERT_BLOCKS_PER_STEP)
    pos = jnp.arange(n_blk, dtype=jnp.int32)
    used = pos < n_used[0]
    blk_new = (jnp.concatenate([jnp.ones((1,), bool), blk_e[1:] != blk_e[:-1]]) & used).astype(jnp.int32)
    stage = (jnp.cumsum(blk_new) - 1) % 2
    first_pos = jnp.where(blk_new == 1, pos, n_blk)
    next_first = jnp.concatenate([lax.cummin(first_pos, reverse=True)[1:], jnp.full((1,), n_blk, jnp.int32)])
    next_e = jnp.where(next_first < n_blk, blk_e[jnp.minimum(next_first, n_blk - 1)], -1).astype(jnp.int32)
    bps = EXPERT_BLOCKS_PER_STEP
    grid_spec = pltpu.PrefetchScalarGridSpec(
        num_scalar_prefetch=5, grid=(n_blk // bps,),
        in_specs=[pl.BlockSpec((bps * blk * xc, LANES),
                               lambda i, be, bn, ne, st, nu: (jnp.minimum(i, (nu[0] - 1) // bps), 0)),
                  pl.BlockSpec(memory_space=pl.ANY),
                  pl.BlockSpec(memory_space=pl.ANY)],
        out_specs=pl.BlockSpec((bps * blk * yc, LANES), lambda i, be, bn, ne, st, nu: (i, 0)),
        scratch_shapes=[pltpu.VMEM((2, d, f2), F32), pltpu.VMEM((2, f2 // 2, d), F32),
                        pltpu.VMEM((d, f2), BF16), pltpu.VMEM((f2 // 2, d), BF16),
                        pltpu.SemaphoreType.DMA((2, 2))])
    return pl.pallas_call(
        _expert_body, grid_spec=grid_spec,
        out_shape=jax.ShapeDtypeStruct((s * yc, LANES), jnp.uint32),
        compiler_params=_params("arbitrary"), name="expert_blocks",
    )(blk_e, blk_new, next_e, stage.astype(jnp.int32), n_used, x, gate_up, down)


def _norm_rope(x, w, cos, sin, first, head_dim):
    y = x * lax.rsqrt(jnp.mean(x * x, axis=-1, keepdims=True) + NORM_EPS) * w
    swapped = jnp.where(first, pltpu.roll(y, head_dim - head_dim // 4, 1), pltpu.roll(y, head_dim // 4, 1))
    return y * cos + swapped * sin


def _mm_qkv_body(a_ref, w_ref, qw_ref, kw_ref, cos_ref, sin_ref, o_ref, wb_ref, *, head_dim, n_q_tiles, n_k_tiles):
    j = pl.program_id(0)

    @pl.when(pl.program_id(1) == 0)
    def _():
        wb_ref[...] = w_ref[...].astype(BF16)

    tm, tn = o_ref.shape
    row_chunk = tm // MM_EPILOGUE_CHUNKS

    def rows_chain(rows, nw_ref):
        acc = jnp.dot(a_ref[rows, :], wb_ref[...], preferred_element_type=F32)
        yield
        if nw_ref is None:
            o_ref[rows, :] = acc.astype(o_ref.dtype)
            return
        cos, sin = cos_ref[rows, :], sin_ref[rows, :]
        lane = lax.broadcasted_iota(jnp.int32, cos.shape, 1)
        first = (lane % (head_dim // 2)) < (head_dim // 4)
        for g in range(tn // head_dim):
            sl = slice(g * head_dim, (g + 1) * head_dim)
            o_ref[rows, sl] = _norm_rope(acc[:, sl], nw_ref[...], cos, sin, first, head_dim).astype(o_ref.dtype)
            yield

    def tile(nw_ref):
        _run_interleaved(rows_chain(slice(r, r + row_chunk), nw_ref) for r in range(0, tm, row_chunk))

    @pl.when(j < n_q_tiles)
    def _():
        tile(qw_ref)

    @pl.when((j >= n_q_tiles) & (j < n_q_tiles + n_k_tiles))
    def _():
        tile(kw_ref)

    @pl.when(j >= n_q_tiles + n_k_tiles)
    def _():
        tile(None)


def matmul_qkv(a, w, q_norm, k_norm, cos_t, sin_t, tm=MM_TILE_M, tn=MM_TILE_N):
    m, k = a.shape
    n = w.shape[1]
    hd = q_norm.shape[0]
    body = functools.partial(_mm_qkv_body, head_dim=hd, n_q_tiles=n // 3 // tn, n_k_tiles=n // 3 // tn)
    return pl.pallas_call(
        body, grid=(n // tn, m // tm),
        in_specs=[pl.BlockSpec((tm, k), lambda j, i: (i, 0)),
                  pl.BlockSpec((k, tn), lambda j, i: (0, j)),
                  pl.BlockSpec((1, hd), lambda j, i: (0, 0)),
                  pl.BlockSpec((1, hd), lambda j, i: (0, 0)),
                  pl.BlockSpec((tm, hd), lambda j, i: (i, 0)),
                  pl.BlockSpec((tm, hd), lambda j, i: (i, 0))],
        out_specs=pl.BlockSpec((tm, tn), lambda j, i: (i, j)),
        out_shape=jax.ShapeDtypeStruct((m, n), BF16),
        scratch_shapes=[pltpu.VMEM((k, tn), BF16)],
        compiler_params=_params("arbitrary", "arbitrary"), name="matmul_qkv",
    )(a, w, q_norm.reshape(1, hd), k_norm.reshape(1, hd), cos_t, sin_t)


def rope_tables(n_lat_tokens_per_sample, n_batch, n_ctx_tokens, head_dim):
    quarter = head_dim // 4
    inv_freq = ROPE_BASE ** (-jnp.arange(quarter, dtype=F32) / quarter)
    rows = n_lat_tokens_per_sample // GRID_W
    row = jnp.repeat(jnp.arange(rows, dtype=F32), GRID_W)
    col = jnp.tile(jnp.arange(GRID_W, dtype=F32), rows)
    ang_r = row[:, None] * inv_freq[None, :]
    ang_c = col[:, None] * inv_freq[None, :]
    cos = jnp.concatenate([jnp.cos(ang_r), jnp.cos(ang_r), jnp.cos(ang_c), jnp.cos(ang_c)], axis=-1)
    sin = jnp.concatenate([-jnp.sin(ang_r), jnp.sin(ang_r), -jnp.sin(ang_c), jnp.sin(ang_c)], axis=-1)
    cos = jnp.concatenate([jnp.tile(cos, (n_batch, 1)), jnp.ones((n_ctx_tokens, head_dim), F32)], axis=0)
    sin = jnp.concatenate([jnp.tile(sin, (n_batch, 1)), jnp.zeros((n_ctx_tokens, head_dim), F32)], axis=0)
    return cos, sin


def _diff_attn_rows(lam, q_ref, k_all, v_all, sw_ref, o_ref, rows, head_dim, out_scale):
    c = head_dim ** -0.5 * math.log2(math.e)
    es, invs = [], []
    for s in range(2):
        sl = slice(s * head_dim, (s + 1) * head_dim)
        sc = lax.dot_general(q_ref[rows, sl], k_all[:, sl], (((1,), (1,)), ((), ())), preferred_element_type=F32)
        yield
        e = jnp.exp2((sc - jnp.max(sc, axis=-1, keepdims=True)) * c)
        invs.append(1.0 / jnp.sum(e, axis=-1, keepdims=True))
        es.append(e.astype(BF16))
        yield
    v = v_all[...]
    o0 = jnp.dot(es[0], v, preferred_element_type=F32)
    o1 = jnp.dot(es[1], v, preferred_element_type=F32)
    yield
    o = o0 * invs[0] - (lam * invs[1]) * o1
    y = o * lax.rsqrt(jnp.mean(o * o, axis=-1, keepdims=True) + NORM_EPS) * sw_ref[...]
    o_ref[rows, :] = (y * out_scale).astype(o_ref.dtype)
    yield


def _diff_attn_body(lam_ref, q_ref, kl_ref, kc_ref, vl_ref, vc_ref, sw_ref, o_ref, k_all, v_all, *,
                    head_dim, out_scale, row_chunk):
    @pl.when(pl.program_id(2) == 0)
    def _():
        n_l = kl_ref.shape[0]
        k_all[:n_l] = kl_ref[...]
        k_all[n_l:] = kc_ref[...]
        v_all[:n_l] = vl_ref[...]
        v_all[n_l:] = vc_ref[...]

    lam = lam_ref[0]
    tq = q_ref.shape[0]
    _run_interleaved(
        _diff_attn_rows(lam, q_ref, k_all, v_all, sw_ref, o_ref, slice(r, r + row_chunk), head_dim, out_scale)
        for r in range(0, tq, row_chunk))


def diff_attention(qkv, lmbda, sub_norm, n_batch, lat_len, ctx_len, n_heads, head_dim, out_scale,
                   tq=ATTN_TILE_Q, row_chunk=ATTN_ROW_CHUNK):
    hw = 2 * head_dim
    nq = lat_len // tq
    ctx_blk0 = n_batch * lat_len // ctx_len
    body = functools.partial(_diff_attn_body, head_dim=head_dim, out_scale=out_scale, row_chunk=row_chunk)
    n_keys = lat_len + ctx_len
    return pl.pallas_call(
        body, grid=(n_batch, n_heads, nq),
        in_specs=[pl.BlockSpec(memory_space=pltpu.SMEM),
                  pl.BlockSpec((tq, hw), lambda b, h, i: (b * nq + i, h)),
                  pl.BlockSpec((lat_len, hw), lambda b, h, i: (b, n_heads + h)),
                  pl.BlockSpec((ctx_len, hw), lambda b, h, i: (ctx_blk0 + b, n_heads + h)),
                  pl.BlockSpec((lat_len, hw), lambda b, h, i: (b, 2 * n_heads + h)),
                  pl.BlockSpec((ctx_len, hw), lambda b, h, i: (ctx_blk0 + b, 2 * n_heads + h)),
                  pl.BlockSpec((1, hw), lambda b, h, i: (0, 0))],
        out_specs=pl.BlockSpec((tq, hw), lambda b, h, i: (b * nq + i, h)),
        out_shape=jax.ShapeDtypeStruct((n_batch * lat_len, n_heads * hw), BF16),
        scratch_shapes=[pltpu.VMEM((n_keys, hw), BF16), pltpu.VMEM((n_keys, hw), BF16)],
        compiler_params=_params("arbitrary", "arbitrary", "arbitrary"), name="diff_attention",
    )(lmbda.reshape(1), qkv, qkv, qkv, qkv, qkv, sub_norm.reshape(1, hw))


SCAN_TILE = 256
CHUNK = 64
SUB = 16
NEG_BIG = -1e30
NEG_INF = float("-inf")
DELTA_INV_PASSES = 1
DELTA_HEADS_PER_STEP = 4
HGRN_HEADS_PER_STEP = 2


def _dot(a, b):
    return jnp.dot(a.astype(BF16), b.astype(BF16), preferred_element_type=F32)


def _dot_nt(a, b):
    return lax.dot_general(a.astype(BF16), b.astype(BF16), (((1,), (1,)), ((), ())),
                           preferred_element_type=F32)


def _split3(x):
    hi = x.astype(BF16)
    r = x - hi.astype(F32)
    mid = r.astype(BF16)
    lo = (r - mid.astype(F32)).astype(BF16)
    return hi, mid, lo


def _dot_exact_lhs01(m01, x):
    hi, mid, lo = _split3(x)
    m = m01.astype(BF16)
    return (jnp.dot(m, hi, preferred_element_type=F32) + jnp.dot(m, mid, preferred_element_type=F32)
            + jnp.dot(m, lo, preferred_element_type=F32))


def _dot_exact_rhs01(x, m01):
    hi, mid, lo = _split3(x)
    m = m01.astype(BF16)
    return (jnp.dot(hi, m, preferred_element_type=F32) + jnp.dot(mid, m, preferred_element_type=F32)
            + jnp.dot(lo, m, preferred_element_type=F32))


def _dot3(a, b):
    ah = a.astype(BF16)
    al = (a - ah.astype(F32)).astype(BF16)
    bh = b.astype(BF16)
    bl = (b - bh.astype(F32)).astype(BF16)
    return (jnp.dot(ah, bh, preferred_element_type=F32) + jnp.dot(ah, bl, preferred_element_type=F32)
            + jnp.dot(al, bh, preferred_element_type=F32))


def _tile_masks(n, reverse):
    i = lax.broadcasted_iota(jnp.int32, (n, n), 0)
    j = lax.broadcasted_iota(jnp.int32, (n, n), 1)
    same = (i // CHUNK) == (j // CHUNK)
    if reverse:
        return same & (i <= j), same & (i < j)
    return same & (i >= j), same & (i > j)


def _segment_tile(b, s, reverse, nl, nc, ctx_tile0):
    if reverse:
        return jnp.where(s < nc, ctx_tile0 + b * nc + (nc - 1 - s), b * nl + (nl - 1 - (s - nc)))
    return jnp.where(s < nc, ctx_tile0 + b * nc + s, b * nl + (s - nc))


def _dn_prep_body(x_ref, prev_ref, next_ref, w_ref, o_ref, *, tiles_per_lat_seg, tiles_per_ctx_seg, n_lat_tiles,
                  head_dim, q_scale):
    i = pl.program_id(0)
    j = pl.program_id(1)
    is_lat = i < n_lat_tiles
    pos = jnp.where(is_lat, i % tiles_per_lat_seg, (i - n_lat_tiles) % tiles_per_ctx_seg)
    seg_first = pos == 0
    seg_last = pos == jnp.where(is_lat, tiles_per_lat_seg, tiles_per_ctx_seg) - 1
    x = x_ref[...]
    tl = x.shape[0]
    prev = jnp.where(seg_first, 0.0, prev_ref[...])
    nxt = jnp.where(seg_last, 0.0, next_ref[...])
    xp = jnp.concatenate([prev, x, nxt], axis=0)
    w = w_ref[...]
    n_taps = 5
    acc = None
    for t in range(n_taps):
        off = 8 + t - n_taps // 2
        term = xp[off:off + tl] * w[t:t + 1]
        acc = term if acc is None else acc + term
    y = _silu(acc)
    scale = jnp.where(j == 0, q_scale, 1.0)
    outs = []
    for h in range(y.shape[1] // head_dim):
        yh = y[:, h * head_dim:(h + 1) * head_dim]
        nrm = lax.rsqrt(jnp.sum(yh * yh, axis=-1, keepdims=True) + 1e-6) * scale
        outs.append(yh * jnp.where(j == 2, 1.0, nrm))
    o_ref[...] = jnp.concatenate(outs, axis=1)


def dn_prep(proj_dn, conv_w_t, geom, ctx_len, width, head_dim):
    n_lat, lat_len, n_batch = geom
    t = proj_dn.shape[0]
    tl = SCAN_TILE
    rows8 = tl // 8
    n_tiles = t // tl
    body = functools.partial(_dn_prep_body, tiles_per_lat_seg=lat_len // tl, tiles_per_ctx_seg=ctx_len // tl,
                             n_lat_tiles=n_lat // tl,
                             head_dim=head_dim, q_scale=head_dim ** -0.5)
    last8 = t // 8 - 1
    return pl.pallas_call(
        body, grid=(n_tiles, 3),
        in_specs=[pl.BlockSpec((tl, width), lambda i, j: (i, j)),
                  pl.BlockSpec((8, width), lambda i, j: (jnp.maximum(i * rows8 - 1, 0), j)),
                  pl.BlockSpec((8, width), lambda i, j: (jnp.minimum((i + 1) * rows8, last8), j)),
                  pl.BlockSpec((8, width), lambda i, j: (0, j))],
        out_specs=pl.BlockSpec((tl, width), lambda i, j: (i, j)),
        out_shape=jax.ShapeDtypeStruct((t, 3 * width), F32),
        compiler_params=_params("arbitrary", "arbitrary"), name="dn_prep",
    )(proj_dn, proj_dn, proj_dn, conv_w_t)


def _softplus(x):
    return jnp.maximum(x, 0.0) + jnp.log(1.0 + jnp.exp(-jnp.abs(x)))


def _gate_prep_body(h_ref, wc_ref, wr_ref, alog_c_ref, dtb_c_ref, alog_r_ref, dtb_r_ref, gc_ref, gr_ref, *, n_heads):
    h = h_ref[...]
    tl = h.shape[0]
    nd = 2 * n_heads
    raw_c = jnp.dot(h, wc_ref[...].astype(BF16), preferred_element_type=F32)
    raw_r = lax.dot_general(wr_ref[...].astype(BF16), h, (((1,), (1,)), ((), ())),
                            preferred_element_type=F32)
    g_c = -jnp.exp(alog_c_ref[...]) * _softplus(raw_c[:, :nd] + dtb_c_ref[...])
    g_r = -jnp.exp(alog_r_ref[...]) * _softplus(raw_r[:nd, :] + dtb_r_ref[...])
    incl_f, _ = _tile_masks(tl, False)
    incl_b, _ = _tile_masks(tl, True)
    one_f = jnp.where(incl_f, 1.0, 0.0)
    one_b = jnp.where(incl_b, 1.0, 0.0)
    cum_c = jnp.concatenate([_dot_exact_lhs01(one_f, g_c[:, :n_heads]),
                             _dot_exact_lhs01(one_b, g_c[:, n_heads:])], axis=1)
    cum_r = jnp.concatenate([_dot_exact_rhs01(g_r[:n_heads, :], one_b),
                             _dot_exact_rhs01(g_r[n_heads:, :], one_f)], axis=0)
    beta_c = _sigmoid(raw_c[:, nd:])
    gc_ref[...] = jnp.concatenate([cum_c, beta_c], axis=1)
    gr_ref[...] = jnp.concatenate([cum_r, jnp.zeros_like(cum_r)], axis=0)


def gate_prep(h_bf, w_gate, a_log, dt_bias, n_heads):
    t, d = h_bf.shape
    tl = SCAN_TILE
    nd = 2 * n_heads
    body = functools.partial(_gate_prep_body, n_heads=n_heads)
    full = lambda shape: pl.BlockSpec(shape, lambda i: (0, 0))
    return pl.pallas_call(
        body, grid=(t // tl,),
        in_specs=[pl.BlockSpec((tl, d), lambda i: (i, 0)), full((d, 2 * nd)), full((2 * nd, d)),
                  full((1, nd)), full((1, nd)), full((nd, 1)), full((nd, 1))],
        out_specs=[pl.BlockSpec((tl, 2 * nd), lambda i: (i, 0)), pl.BlockSpec((2 * nd, tl), lambda i: (0, i))],
        out_shape=[jax.ShapeDtypeStruct((t, 2 * nd), F32), jax.ShapeDtypeStruct((2 * nd, t), F32)],
        compiler_params=_params("arbitrary"), name="gate_prep",
    )(h_bf, w_gate, w_gate.T, a_log.reshape(1, nd), dt_bias.reshape(1, nd),
      a_log.reshape(nd, 1), dt_bias.reshape(nd, 1))


def _select_col(x, idx):
    lane = lax.broadcasted_iota(jnp.int32, x.shape, 1)
    return jnp.sum(jnp.where(lane == idx, x, 0.0), axis=1, keepdims=True)


def _select_row(x, idx):
    row = lax.broadcasted_iota(jnp.int32, x.shape, 0)
    return jnp.sum(jnp.where(row == idx, x, 0.0), axis=0, keepdims=True)


def _run_interleaved(chains):
    chains = list(chains)
    while chains:
        alive = []
        for ch in chains:
            try:
                next(ch)
                alive.append(ch)
            except StopIteration:
                pass
        chains = alive


def _delta_chain(q, k, v, gc_col, gc_row, beta_col, s_ref, o_ref, cols, reverse, inv_passes):
    tl, kd_ = k.shape
    n_chunks = tl // CHUNK
    incl, strict = _tile_masks(tl, reverse)
    decay = jnp.exp(jnp.where(incl, gc_col - gc_row, NEG_BIG))
    kb, qb = k.astype(BF16), q.astype(BF16)
    kkt = _dot_nt(kb, kb)
    qkt = _dot_nt(qb, kb)
    yield
    x = jnp.where(strict, kkt * (-beta_col) * decay, 0.0)
    dot_inv = _dot3 if inv_passes == 3 else _dot
    ri = lax.broadcasted_iota(jnp.int32, (tl, tl), 0)
    ci = lax.broadcasted_iota(jnp.int32, (tl, tl), 1)
    r = jnp.where(ri == ci, 1.0, 0.0) + x
    n_sq = int(math.log2(CHUNK)) - 1
    for _ in range(n_sq):
        x = dot_inv(x, x)
        r = r + dot_inv(r, x)
        yield
    e_g = jnp.exp(gc_col)
    rhs = jnp.concatenate([v * beta_col, k * (beta_col * e_g)], axis=1)
    sol = dot_inv(r, rhs)
    yield
    u0, w = sol[:, :v.shape[1]], sol[:, v.shape[1]:]
    attn = (qkt * decay).astype(BF16)
    a_sol = _dot(attn, sol)
    o0 = a_sol[:, :v.shape[1]]
    qe = q * e_g - a_sol[:, v.shape[1]:]
    tot_rows = []
    for c in range(n_chunks):
        last = c * CHUNK if reverse else c * CHUNK + CHUNK - 1
        tot_rows.append(jnp.broadcast_to(gc_col[last:last + 1, :], (CHUNK, 1)))
    tot = jnp.concatenate(tot_rows, axis=0)
    kdec_t = jnp.transpose(k * jnp.exp(tot - gc_col))
    wu = jnp.concatenate([-w, u0], axis=1).astype(BF16)
    lane = lax.broadcasted_iota(jnp.int32, kdec_t.shape, 1)
    pns = [_dot(jnp.where((lane // CHUNK) == c, kdec_t, 0.0), wu) for c in range(n_chunks)]
    yield
    order = range(n_chunks - 1, -1, -1) if reverse else range(n_chunks)
    for c in order:
        rows = slice(c * CHUNK, (c + 1) * CHUNK)
        pn = pns[c]
        lhs = jnp.concatenate([qe[rows], pn[:, :kd_]], axis=0)
        s = s_ref[...]
        res = _dot(lhs, s)
        o_ref[rows, cols] = o0[rows] + res[:CHUNK]
        last = c * CHUNK if reverse else c * CHUNK + CHUNK - 1
        gl = jnp.exp(gc_col[last:last + 1, :])
        s_ref[...] = gl * s + res[CHUNK:] + pn[:, kd_:]
        yield


def _delta_body(qf_ref, kf_ref, vf_ref, gcf_ref, grf_ref, qb_ref, kb_ref, vb_ref, gcb_ref, grb_ref,
                of_ref, ob_ref, sf_ref, sb_ref, *, n_heads, head_dim, inv_passes):
    hg = pl.program_id(1)
    heads_per_step = sf_ref.shape[0]

    @pl.when(pl.program_id(2) == 0)
    def _():
        sf_ref[...] = jnp.zeros_like(sf_ref)
        sb_ref[...] = jnp.zeros_like(sb_ref)

    chains = []
    for g in range(heads_per_step):
        cols = slice(g * head_dim, (g + 1) * head_dim)
        for reverse, (q_ref, k_ref, v_ref, gc_ref, gr_ref, o_ref, s_ref) in enumerate(
                [(qf_ref, kf_ref, vf_ref, gcf_ref, grf_ref, of_ref, sf_ref),
                 (qb_ref, kb_ref, vb_ref, gcb_ref, grb_ref, ob_ref, sb_ref)]):
            idx = reverse * n_heads + hg * heads_per_step + g
            gcs = gc_ref[...]
            gc_col = _select_col(gcs, idx)
            beta_col = _select_col(gcs, 2 * n_heads + idx)
            gc_row = _select_row(gr_ref[...], idx)
            chains.append(_delta_chain(q_ref[:, cols], k_ref[:, cols], v_ref[:, cols], gc_col, gc_row, beta_col,
                                       s_ref.at[g], o_ref, cols, bool(reverse), inv_passes))
    _run_interleaved(chains)


def delta_scan(qkv, gc, gr, geom, ctx_len, n_heads, head_dim, inv_passes=3, heads_per_step=DELTA_HEADS_PER_STEP):
    n_lat, lat_len, n_batch = geom
    t = qkv.shape[0]
    tl = SCAN_TILE
    nl, nc, ctx0 = lat_len // tl, ctx_len // tl, n_lat // tl
    n_hg = n_heads // heads_per_step
    gw = heads_per_step * head_dim
    tile = functools.partial(_segment_tile, nl=nl, nc=nc, ctx_tile0=ctx0)
    specs = []
    for reverse in (False, True):
        tix = functools.partial(tile, reverse=reverse)
        specs += [pl.BlockSpec((tl, gw), lambda b, h, s, tix=tix: (tix(b, s), h)),
                  pl.BlockSpec((tl, gw), lambda b, h, s, tix=tix: (tix(b, s), n_hg + h)),
                  pl.BlockSpec((tl, gw), lambda b, h, s, tix=tix: (tix(b, s), 2 * n_hg + h)),
                  pl.BlockSpec((tl, 4 * n_heads), lambda b, h, s, tix=tix: (tix(b, s), 0)),
                  pl.BlockSpec((4 * n_heads, tl), lambda b, h, s, tix=tix: (0, tix(b, s)))]
    out_specs = [pl.BlockSpec((tl, gw), lambda b, h, s, tix=functools.partial(tile, reverse=r): (tix(b, s), h))
                 for r in (False, True)]
    body = functools.partial(_delta_body, n_heads=n_heads, head_dim=head_dim, inv_passes=inv_passes)
    return pl.pallas_call(
        body, grid=(n_batch, n_hg, nl + nc), in_specs=specs, out_specs=out_specs,
        out_shape=[jax.ShapeDtypeStruct((t, n_heads * head_dim), F32)] * 2,
        scratch_shapes=[pltpu.VMEM((heads_per_step, head_dim, head_dim), F32)] * 2,
        compiler_params=_params("arbitrary", "arbitrary", "arbitrary"), name="delta_scan",
    )(qkv, qkv, qkv, gc, gr, qkv, qkv, qkv, gc, gr)


def _hgrn_diag(q, k, cum, v, stage_ref, diag_ref, reverse):
    n_sub = q.shape[0] // SUB
    for n, val in enumerate((q, k, cum, v)):
        stage_ref[n] = val

    def slabs(n):
        return [stage_ref.at[n][pl.ds(r, n_sub, stride=SUB), :] for r in range(SUB)]

    q_x, k_x, p_x, v_x = slabs(0), slabs(1), slabs(2), slabs(3)
    pairs = [(i, j) for i in range(SUB) for j in range(SUB) if (j >= i if reverse else j <= i)]
    terms = []
    for i, j in pairs:
        qk = q_x[i] * k_x[j]
        terms.append((qk if i == j else qk * jnp.exp(p_x[i] - p_x[j])).astype(BF16))
    kd_ = terms[0].shape[1]
    a_rep = jnp.dot(jnp.concatenate(terms, axis=0), jnp.ones((kd_, kd_), BF16), preferred_element_type=F32)
    o_x = [None] * SUB
    for n, (i, j) in enumerate(pairs):
        contrib = a_rep[n * n_sub:(n + 1) * n_sub] * v_x[j]
        o_x[i] = contrib if o_x[i] is None else o_x[i] + contrib
    for r in range(SUB):
        diag_ref[pl.ds(r, n_sub, stride=SUB), :] = o_x[r]


def _hgrn_chain(hq_ref, hf_ref, hv_ref, cols, lb, st_ref, o_ref, stage_ref, diag_ref, reverse):
    hq, hf, hv = hq_ref[:, cols], hf_ref[:, cols], hv_ref[:, cols]
    tl, kd_ = hq.shape
    n_chunks = tl // CHUNK
    n_sub = tl // SUB
    sub_per_chunk = CHUNK // SUB
    q = _silu(hq)
    f = lb + (1.0 - lb) * _sigmoid(hf)
    k = 1.0 - f
    lf = jnp.log(f)
    incl, _ = _tile_masks(tl, reverse)
    cum = _dot_exact_lhs01(jnp.where(incl, 1.0, 0.0), lf)
    excl = cum - lf

    def bcast_rows(src, row, n):
        return jnp.broadcast_to(src[row:row + 1, :], (n, kd_))

    chunk_last = [(c * CHUNK if reverse else c * CHUNK + CHUNK - 1) for c in range(n_chunks)]
    tot = jnp.concatenate([bcast_rows(cum, chunk_last[c], CHUNK) for c in range(n_chunks)], axis=0)
    sub_first = [(m * SUB + SUB - 1 if reverse else m * SUB) for m in range(n_sub)]
    r_sub = jnp.concatenate([bcast_rows(excl, sub_first[m], SUB) for m in range(n_sub)], axis=0)
    q_t = q * jnp.exp(cum - r_sub)
    qd = q * jnp.exp(cum)
    kd = k * jnp.exp(tot - cum)
    vb = hv.astype(BF16)

    i = lax.broadcasted_iota(jnp.int32, (tl, tl), 0)
    j = lax.broadcasted_iota(jnp.int32, (tl, tl), 1)
    same = (i // CHUNK) == (j // CHUNK)
    pos_i = (i % CHUNK) // SUB
    pos_j = (j % CHUNK) // SUB
    if reverse:
        pos_i, pos_j = sub_per_chunk - 1 - pos_i, sub_per_chunk - 1 - pos_j
    a_off = jnp.zeros((tl, tl), F32)
    for lvl in range(1, sub_per_chunk):
        ref_rows = []
        for c in range(n_chunks):
            m = c * sub_per_chunk + (sub_per_chunk - 1 - lvl if reverse else lvl)
            ref_rows.append(bcast_rows(excl, sub_first[m], CHUNK))
        r_lvl = jnp.concatenate(ref_rows, axis=0)
        k_t = k * jnp.exp(jnp.minimum(r_lvl - cum, 0.0))
        a_l = _dot_nt(q_t, k_t)
        a_off = a_off + jnp.where(same & (pos_i == lvl) & (pos_j < lvl), a_l, 0.0)
        yield
    o_intra = _dot(a_off, vb)
    yield

    _hgrn_diag(q, k, cum, hv, stage_ref, diag_ref, reverse)
    yield
    o_intra = o_intra + diag_ref[...]

    v_t = jnp.transpose(hv)
    lane = lax.broadcasted_iota(jnp.int32, v_t.shape, 1)
    kdb = kd.astype(BF16)
    n_ts = [_dot(jnp.where((lane // CHUNK) == c, v_t, 0.0), kdb) for c in range(n_chunks)]
    yield
    order = range(n_chunks - 1, -1, -1) if reverse else range(n_chunks)
    for c in order:
        rows = slice(c * CHUNK, (c + 1) * CHUNK)
        st = st_ref[...]
        o_ref[rows, cols] = o_intra[rows] + _dot_nt(qd[rows], st)
        st_ref[...] = st * jnp.exp(cum[chunk_last[c]:chunk_last[c] + 1, :]) + n_ts[c]
        yield


def _hgrn_body(qf_ref, ff_ref, vf_ref, qb_ref, fb_ref, vb_ref, lb_ref, of_ref, ob_ref, sf_ref, sb_ref, stage_ref,
               diag_ref, *, key_dim):
    heads_per_step = sf_ref.shape[0]

    @pl.when(pl.program_id(2) == 0)
    def _():
        sf_ref[...] = jnp.zeros_like(sf_ref)
        sb_ref[...] = jnp.zeros_like(sb_ref)

    chains = []
    for g in range(heads_per_step):
        cols = slice(g * key_dim, (g + 1) * key_dim)
        lb = lb_ref[:, cols]
        chains.append(_hgrn_chain(qf_ref, ff_ref, vf_ref, cols, lb, sf_ref.at[g], of_ref,
                                  stage_ref.at[2 * g], diag_ref.at[2 * g], False))
        chains.append(_hgrn_chain(qb_ref, fb_ref, vb_ref, cols, lb, sb_ref.at[g], ob_ref,
                                  stage_ref.at[2 * g + 1], diag_ref.at[2 * g + 1], True))
    _run_interleaved(chains)


def hgrn_scan(proj_hg, lb, geom, ctx_len, n_heads, key_dim, heads_per_step=HGRN_HEADS_PER_STEP):
    n_lat, lat_len, n_batch = geom
    t = proj_hg.shape[0]
    tl = SCAN_TILE
    nl, nc, ctx0 = lat_len // tl, ctx_len // tl, n_lat // tl
    n_hg = n_heads // heads_per_step
    gw = heads_per_step * key_dim
    tile = functools.partial(_segment_tile, nl=nl, nc=nc, ctx_tile0=ctx0)
    specs = []
    for reverse in (False, True):
        tix = functools.partial(tile, reverse=reverse)
        fcol = (1 + int(reverse)) * n_hg
        specs += [pl.BlockSpec((tl, gw), lambda b, h, s, tix=tix: (tix(b, s), h)),
                  pl.BlockSpec((tl, gw), lambda b, h, s, tix=tix, fcol=fcol: (tix(b, s), fcol + h)),
                  pl.BlockSpec((tl, gw), lambda b, h, s, tix=tix: (tix(b, s), 3 * n_hg + h))]
    specs.append(pl.BlockSpec((1, gw), lambda b, h, s: (0, h)))
    out_specs = [pl.BlockSpec((tl, gw), lambda b, h, s, tix=functools.partial(tile, reverse=r): (tix(b, s), h))
                 for r in (False, True)]
    return pl.pallas_call(
        functools.partial(_hgrn_body, key_dim=key_dim), grid=(n_batch, n_hg, nl + nc),
        in_specs=specs, out_specs=out_specs,
        out_shape=[jax.ShapeDtypeStruct((t, n_heads * key_dim), F32)] * 2,
        scratch_shapes=[pltpu.VMEM((heads_per_step, key_dim, key_dim), F32)] * 2
        + [pltpu.VMEM((2 * heads_per_step, 4, tl, key_dim), F32), pltpu.VMEM((2 * heads_per_step, tl, key_dim), F32)],
        compiler_params=_params("arbitrary", "arbitrary", "arbitrary"), name="hgrn_scan",
    )(proj_hg, proj_hg, proj_hg, proj_hg, proj_hg, proj_hg, lb)


def _mix_out2_body(df_ref, db_ref, hf_ref, hb_ref, z_ref, og_ref, dnw_ref, hgw_ref, o_ref, *, head_dim):
    def normed(o, nw):
        outs = []
        for h in range(o.shape[1] // head_dim):
            oh = o[:, h * head_dim:(h + 1) * head_dim]
            outs.append(oh * lax.rsqrt(jnp.mean(oh * oh, axis=-1, keepdims=True) + NORM_EPS) * nw)
        return jnp.concatenate(outs, axis=1)

    dn = normed(df_ref[...] + db_ref[...], dnw_ref[...]) * _silu(z_ref[...])
    hg = normed(hf_ref[...] + hb_ref[...], hgw_ref[...]) * _sigmoid(og_ref[...])
    half = dn.shape[1]
    o_ref[:, :half] = dn.astype(o_ref.dtype)
    o_ref[:, half:] = hg.astype(o_ref.dtype)


def mix_out(dn_f, dn_b, hg_f, hg_b, z_src, z_blk, og_src, og_blk, dn_norm, hg_norm, head_dim):
    t, w = dn_f.shape
    tl = SCAN_TILE
    row = lambda i: (i, 0)
    body = functools.partial(_mix_out2_body, head_dim=head_dim)
    return pl.pallas_call(
        body, grid=(t // tl,),
        in_specs=[pl.BlockSpec((tl, w), row)] * 4
        + [pl.BlockSpec((tl, w), lambda i: (i, z_blk)), pl.BlockSpec((tl, w), lambda i: (i, og_blk)),
           pl.BlockSpec((1, head_dim), lambda i: (0, 0)), pl.BlockSpec((1, head_dim), lambda i: (0, 0))],
        out_specs=pl.BlockSpec((tl, 2 * w), row),
        out_shape=jax.ShapeDtypeStruct((t, 2 * w), BF16),
        compiler_params=_params("arbitrary"), name="mix_out",
    )(dn_f, dn_b, hg_f, hg_b, z_src, og_src, dn_norm.reshape(1, head_dim), hg_norm.reshape(1, head_dim))


def _first_max(x, ids, n):
    m = jnp.max(x, axis=0, keepdims=True)
    first = jnp.min(jnp.where(x == m, ids, n), axis=0, keepdims=True)
    return m, first


def _route_body(lg_ref, bias_ref, idx_ref, rank_ref, w_ref, cnt_ref, carry_ref, *,
                n_groups, topk_groups, top_k, scale):
    i = pl.program_id(0)

    @pl.when(i == 0)
    def _():
        carry_ref[...] = jnp.zeros_like(carry_ref)

    lg = lg_ref[...]
    n_exp, tl = lg.shape
    per = n_exp // n_groups
    scores = 1.0 / (1.0 + jnp.exp(-lg))
    biased = scores + bias_ref[...]
    sub = lax.broadcasted_iota(jnp.int32, (per, tl), 0)
    g_rows = []
    for g in range(n_groups):
        xg = biased[g * per:(g + 1) * per]
        m1, i1 = _first_max(xg, sub, per)
        m2 = jnp.max(jnp.where(sub == i1, NEG_INF, xg), axis=0, keepdims=True)
        g_rows.append(m1 + m2)
    gscore = jnp.concatenate(g_rows, axis=0)
    gid = lax.broadcasted_iota(jnp.int32, (n_groups, tl), 0)
    gsel = jnp.zeros((n_groups, tl), jnp.bool_)
    for _ in range(topk_groups):
        _, first = _first_max(gscore, gid, n_groups)
        hit = gid == first
        gsel = gsel | hit
        gscore = jnp.where(hit, NEG_INF, gscore)
    eid = lax.broadcasted_iota(jnp.int32, (n_exp, tl), 0)
    gmask = jnp.concatenate([jnp.broadcast_to(gsel[g:g + 1], (per, tl)) for g in range(n_groups)], axis=0)
    masked = jnp.where(gmask, biased, NEG_INF)
    sel = jnp.zeros((n_exp, tl), jnp.bool_)
    hits, firsts = [], []
    for _ in range(top_k):
        _, first = _first_max(masked, eid, n_exp)
        hit = eid == first
        hits.append(hit)
        firsts.append(first)
        sel = sel | hit
        masked = jnp.where(hit, NEG_INF, masked)
    self = jnp.where(sel, 1.0, 0.0)
    ti = lax.broadcasted_iota(jnp.int32, (tl, tl), 0)
    tj = lax.broadcasted_iota(jnp.int32, (tl, tl), 1)
    before = jnp.where(ti < tj, 1.0, 0.0).astype(BF16)
    carry = carry_ref[...]
    rank_full = jnp.dot(self.astype(BF16), before, preferred_element_type=F32) + carry
    w_rows = [jnp.sum(jnp.where(hit, scores, 0.0), axis=0, keepdims=True) for hit in hits]
    r_rows = [jnp.sum(jnp.where(hit, rank_full, 0.0), axis=0, keepdims=True) for hit in hits]
    w8 = jnp.concatenate(w_rows, axis=0)
    idx_ref[...] = jnp.concatenate(firsts, axis=0)
    rank_ref[...] = jnp.concatenate(r_rows, axis=0).astype(jnp.int32)
    w_ref[...] = w8 / jnp.sum(w8, axis=0, keepdims=True) * scale
    carry = carry + jnp.sum(self, axis=1, keepdims=True)
    carry_ref[...] = carry
    cnt_ref[...] = carry.astype(jnp.int32)


def route(logits_t, router_b, tl=256):
    n_exp, t = logits_t.shape
    body = functools.partial(_route_body, n_groups=N_GROUPS, topk_groups=TOPK_GROUPS, top_k=TOP_K,
                             scale=ROUTED_SCALE)
    tok_spec = pl.BlockSpec((TOP_K, tl), lambda i: (0, i))
    return pl.pallas_call(
        body, grid=(t // tl,),
        in_specs=[pl.BlockSpec((n_exp, tl), lambda i: (0, i)), pl.BlockSpec((n_exp, 1), lambda i: (0, 0))],
        out_specs=[tok_spec, tok_spec, tok_spec, pl.BlockSpec((n_exp, 1), lambda i: (0, 0))],
        out_shape=[jax.ShapeDtypeStruct((TOP_K, t), jnp.int32), jax.ShapeDtypeStruct((TOP_K, t), jnp.int32),
                   jax.ShapeDtypeStruct((TOP_K, t), F32), jax.ShapeDtypeStruct((n_exp, 1), jnp.int32)],
        scratch_shapes=[pltpu.VMEM((n_exp, 1), F32)],
        compiler_params=_params("arbitrary"), name="route",
    )(logits_t, router_b.reshape(n_exp, 1))


def _slab_copy(src_ref, src_tok, dst_ref, dst_tok, rows, sem):
    return pltpu.make_async_copy(src_ref.at[pl.ds(pl.multiple_of(src_tok * rows, rows), rows)],
                                 dst_ref.at[pl.ds(pl.multiple_of(dst_tok * rows, rows), rows)], sem)


def _dispatch_body(dest_ref, pad_ref, h_ref, xs_ref, zero_ref, sem, *, top_k, rows):
    tl = h_ref.shape[0] // rows
    n_exp = pad_ref.shape[1]

    def issue(r, carry):
        for k in range(top_k):
            _slab_copy(h_ref, r, xs_ref, dest_ref[r * top_k + k], rows, sem.at[0]).start(priority=k % 2)
        return carry

    lax.fori_loop(0, tl, issue, 0, unroll=DMA_LOOP_UNROLL)

    @pl.when(pl.program_id(0) == 0)
    def _():
        zero_ref[...] = jnp.zeros_like(zero_ref)
        largest = zero_ref.shape[0] // rows
        pieces = [largest >> b for b in range(largest.bit_length())]

        def pad_copies(e, wait):
            first, n_pad = pad_ref[0, e], pad_ref[1, e]
            for size in pieces:
                @pl.when((n_pad & size) != 0)
                def _():
                    slot0 = first + (n_pad & ~(2 * size - 1))
                    cp = pltpu.make_async_copy(
                        zero_ref.at[pl.ds(0, size * rows)],
                        xs_ref.at[pl.ds(pl.multiple_of(slot0 * rows, rows), size * rows)], sem.at[1])
                    if wait:
                        cp.wait()
                    else:
                        cp.start()

        def fill(e, carry):
            pad_copies(e, False)
            return carry

        def drain_pad(e, carry):
            pad_copies(e, True)
            return carry

        lax.fori_loop(0, n_exp, fill, 0)
        lax.fori_loop(0, n_exp, drain_pad, 0)

    def drain(r, carry):
        for k in range(top_k):
            _slab_copy(h_ref, 0, xs_ref, 0, rows, sem.at[0]).wait()
        return carry

    lax.fori_loop(0, tl, drain, 0, unroll=DMA_LOOP_UNROLL)


def dispatch(h_slab, dest_flat, pad_info, n_slot, top_k, rows, tl=256):
    t = h_slab.shape[0] // rows
    return pl.pallas_call(
        functools.partial(_dispatch_body, top_k=top_k, rows=rows), grid=(t // tl,),
        in_specs=[pl.BlockSpec((tl * top_k,), lambda i: (i,), memory_space=pltpu.SMEM),
                  pl.BlockSpec(memory_space=pltpu.SMEM),
                  pl.BlockSpec((tl * rows, LANES), lambda i: (i, 0))],
        out_specs=pl.BlockSpec(memory_space=pl.ANY),
        out_shape=jax.ShapeDtypeStruct((n_slot * rows, LANES), h_slab.dtype),
        scratch_shapes=[pltpu.VMEM((EXPERT_SLOT_BLOCK // 2 * rows, LANES), h_slab.dtype),
                        pltpu.SemaphoreType.DMA((2,))],
        compiler_params=_params("arbitrary"), name="dispatch",
    )(dest_flat, pad_info, h_slab)


def _combine_body(dest_ref, dest_next_ref, w_ref, sh_ref, x_ref, gate_ref, y_ref, o_ref, buf_ref, acc_ref, sem, *,
                  top_k, rows):
    i = pl.program_id(0)
    n = pl.num_programs(0)
    tl = x_ref.shape[0]

    def gather(d_ref, slot):
        def issue(r, carry):
            for k in range(top_k):
                _slab_copy(y_ref, d_ref[r * top_k + k], buf_ref.at[slot], k * tl + r, rows,
                           sem.at[slot]).start(priority=k % 2)
            return carry
        lax.fori_loop(0, tl, issue, 0, unroll=DMA_LOOP_UNROLL)

    @pl.when(i == 0)
    def _():
        gather(dest_ref, 0)

    @pl.when(i + 1 < n)
    def _():
        gather(dest_next_ref, (i + 1) % 2)

    slot = i % 2

    def drain(r, carry):
        for k in range(top_k):
            _slab_copy(y_ref, 0, buf_ref.at[slot], 0, rows, sem.at[slot]).wait()
        return carry

    lax.fori_loop(0, tl, drain, 0, unroll=DMA_LOOP_UNROLL)
    def unpack_f32(pk):
        return (lax.bitcast_convert_type(pk & jnp.uint32(0xFFFF0000), F32),
                lax.bitcast_convert_type(pk << jnp.uint32(16), F32))

    w = w_ref[...]
    acc_hi, acc_lo = unpack_f32(sh_ref[...])
    for k in range(top_k):
        wk = jnp.broadcast_to(w[:, k:k + 1], (tl, LANES))
        wk = jnp.broadcast_to(wk[:, None, :], (tl, rows, LANES)).reshape(tl * rows, LANES)
        y_hi, y_lo = unpack_f32(buf_ref[slot, pl.ds(k * tl * rows, tl * rows), :])
        acc_hi = acc_hi + y_hi * wk
        acc_lo = acc_lo + y_lo * wk
    acc_ref[0] = acc_hi
    acc_ref[1] = acc_lo
    for half in range(2):
        for c in range(rows):
            cols = slice((half * rows + c) * LANES, (half * rows + c + 1) * LANES)
            o_ref[:, cols] = (x_ref[:, cols]
                              + gate_ref[0][:, cols] * acc_ref.at[half][pl.ds(c, tl, stride=rows), :])


def combine(y_slab, dest_flat, w, shared_slab, x, mod3, gate_idx, geom, tl=128):
    t, d = x.shape
    rows = d // 2 // LANES
    top_k = w.shape[1]
    n_lat, lat_len, n_batch = geom
    n_tiles = t // tl
    row = functools.partial(_mod_row, tile_rows=tl, n_lat=n_lat, lat_len=lat_len, n_batch=n_batch)
    return pl.pallas_call(
        functools.partial(_combine_body, top_k=top_k, rows=rows), grid=(n_tiles,),
        in_specs=[pl.BlockSpec((tl * top_k,), lambda i: (i,), memory_space=pltpu.SMEM),
                  pl.BlockSpec((tl * top_k,), lambda i: (jnp.minimum(i + 1, n_tiles - 1),),
                               memory_space=pltpu.SMEM),
                  pl.BlockSpec((tl, top_k), lambda i: (i, 0)),
                  pl.BlockSpec((tl * rows, LANES), lambda i: (i, 0)),
                  pl.BlockSpec((tl, d), lambda i: (i, 0)),
                  pl.BlockSpec((1, 1, d), lambda i: (row(i) * 6 + gate_idx, 0, 0)),
                  pl.BlockSpec(memory_space=pl.ANY)],
        out_specs=pl.BlockSpec((tl, d), lambda i: (i, 0)),
        out_shape=jax.ShapeDtypeStruct((t, d), F32),
        scratch_shapes=[pltpu.VMEM((2, top_k * tl * rows, LANES), jnp.uint32),
                        pltpu.VMEM((2, tl * rows, LANES), F32), pltpu.SemaphoreType.DMA((2,))],
        compiler_params=_params("arbitrary"), name="combine",
    )(dest_flat, dest_flat, w, shared_slab, x, mod3, y_slab)


def moe_ffn(x, h_pk, logits_t, router_b, exp_gate_up, exp_down, shared_gate_up, shared_down, mod3, geom):
    t = x.shape[0]
    slab_rows = h_pk.shape[0] // t
    e_count = exp_gate_up.shape[0]
    blk = EXPERT_SLOT_BLOCK
    idx8, rank8, w8, counts = route(logits_t, router_b)
    counts = counts[:, 0]
    padded = (counts + blk - 1) // blk * blk
    pad_end = jnp.cumsum(padded)
    pad_start = pad_end - padded
    start8 = jnp.sum(jnp.where(idx8[None] == jnp.arange(e_count, dtype=jnp.int32)[:, None, None],
                               pad_start[:, None, None], 0), axis=0)
    dest_flat = (start8 + rank8).T.reshape(-1)
    n_blk = (t * TOP_K + e_count * (blk - 1)) // blk + 1
    n_blk = -(-n_blk // EXPERT_BLOCKS_PER_STEP) * EXPERT_BLOCKS_PER_STEP
    n_slot = n_blk * blk
    blk_starts = jnp.arange(n_blk, dtype=jnp.int32) * blk
    blk_e = jnp.sum((pad_end[None, :] <= blk_starts[:, None]).astype(jnp.int32), axis=1)
    blk_e = jnp.minimum(blk_e, e_count - 1)
    n_used = (pad_end[-1] // blk).astype(jnp.int32).reshape(1)
    pad_info = jnp.stack([pad_start + counts, padded - counts]).astype(jnp.int32)
    x_sorted = dispatch(h_pk, dest_flat, pad_info, n_slot, TOP_K, slab_rows)
    y_slot = expert_blocks(x_sorted, exp_gate_up, exp_down, blk_e, n_used, blk)
    shared = expert_blocks(h_pk, shared_gate_up[None], shared_down[None], jnp.zeros((t // blk,), jnp.int32),
                           jnp.full((1,), t // blk, jnp.int32), blk)
    return combine(y_slot, dest_flat, w8.T, shared, x, mod3, 5, geom)


def kernel(x, c, ctx, c_ctx, hg_lb_logits, l0_mod_w, l0_mod_b, l0_norm1, l0_norm2, l0_w_in, l0_dn_conv, l0_dn_a_log, l0_dn_dt_bias, l0_dn_norm, l0_hg_norm, l0_w_out, l0_router_w, l0_router_b, l0_exp_gate_up, l0_exp_down, l0_shared_gate_up, l0_shared_down, l1_mod_w, l1_mod_b, l1_norm1, l1_norm2, l1_w_in, l1_q_norm, l1_k_norm, l1_lambda, l1_sub_norm, l1_w_out, l1_router_w, l1_router_b, l1_exp_gate_up, l1_exp_down, l1_shared_gate_up, l1_shared_down):
    n_batch, lat_len, d = x.shape
    ctx_len = ctx.shape[1]
    n_lat = n_batch * lat_len
    n_ctx = n_batch * ctx_len
    geom = (n_lat, lat_len, n_batch)
    geom_lat_only = (n_lat, lat_len, n_batch)

    xs = jnp.concatenate([x.reshape(n_lat, d), ctx.reshape(n_ctx, d)], axis=0)
    cond = jnp.concatenate([c, c_ctx[None], jnp.zeros((MOD_ROWS - n_batch - 1, d), F32)], axis=0)

    mod3 = modulation(cond, l0_mod_w, l0_mod_b).reshape(MOD_ROWS * 6, 1, d)
    h = adaln(xs, l0_norm1, mod3, 0, geom)
    n_dn = IN0_SIZES[0] + IN0_SIZES[1]
    n_gate = IN0_SIZES[2] + IN0_SIZES[3]
    proj_dn = matmul(h, l0_w_in, n_cols=n_dn)
    proj_hg = matmul(h, l0_w_in[:, n_dn + n_gate:])
    gc, gr = gate_prep(h, l0_w_in[:, n_dn:n_dn + n_gate], l0_dn_a_log, l0_dn_dt_bias, DN_HEADS)
    conv_t = jnp.concatenate([l0_dn_conv.T, jnp.zeros((8 - l0_dn_conv.shape[1], 3 * DN_WIDTH), F32)], axis=0)
    qkv = dn_prep(proj_dn, conv_t, geom, ctx_len, DN_WIDTH, DN_HEAD_DIM)
    dn_f, dn_b = delta_scan(qkv, gc, gr, geom, ctx_len, DN_HEADS, DN_HEAD_DIM, inv_passes=DELTA_INV_PASSES)
    lb = jnp.cumsum(jax.nn.softmax(hg_lb_logits, axis=0), axis=0)[0:1]
    hg_f, hg_b = hgrn_scan(proj_hg, lb, geom, ctx_len, HG_HEADS, HG_KEY_DIM)
    y = mix_out(dn_f, dn_b, hg_f, hg_b, proj_dn, 3, proj_hg, 4, l0_dn_norm, l0_hg_norm, DN_HEAD_DIM)
    xs = matmul_residual(y, l0_w_out, xs, mod3, 2, geom)
    h, logits_t = adaln(xs, l0_norm2, mod3, 3, geom, router_w=l0_router_w)
    xs = moe_ffn(xs, h, logits_t, l0_router_b, l0_exp_gate_up, l0_exp_down, l0_shared_gate_up, l0_shared_down,
                 mod3, geom)

    mod3 = modulation(cond, l1_mod_w, l1_mod_b).reshape(MOD_ROWS * 6, 1, d)
    h = adaln(xs, l1_norm1, mod3, 0, geom)
    cos_t, sin_t = rope_tables(lat_len, n_batch, n_ctx, DA_HEAD_DIM)
    qkv = matmul_qkv(h, l1_w_in, l1_q_norm, l1_k_norm, cos_t, sin_t)
    lam_init = 0.8 - 0.6 * math.exp(-0.3 * 1)
    lmbda = (jnp.exp(jnp.sum(l1_lambda[0] * l1_lambda[1])) - jnp.exp(jnp.sum(l1_lambda[2] * l1_lambda[3]))
             + lam_init)
    y = diff_attention(qkv, lmbda, l1_sub_norm, n_batch, lat_len, ctx_len, DA_HEADS, DA_HEAD_DIM,
                       1.0 - lam_init)
    xl = matmul_residual(y, l1_w_out, xs, mod3, 2, geom_lat_only)
    h, logits_t = adaln(xl, l1_norm2, mod3, 3, geom_lat_only, router_w=l1_router_w)
    xl = moe_ffn(xl, h, logits_t, l1_router_b, l1_exp_gate_up, l1_exp_down, l1_shared_gate_up, l1_shared_down,
                 mod3, geom_lat_only)
    return xl.reshape(n_batch, lat_len, d)
```

```python
import functools
import math

import jax
import jax.numpy as jnp
from jax import lax
from jax.experimental import pallas as pl
from jax.experimental.pallas import tpu as pltpu

F32 = jnp.float32
BF16 = jnp.bfloat16

NORM_EPS = 1e-6
GRID_W = 64
ROPE_BASE = 10000.0

DN_HEADS = 8
DN_HEAD_DIM = 128
DN_WIDTH = DN_HEADS * DN_HEAD_DIM
HG_HEADS = 8
HG_KEY_DIM = 128
HG_K_WIDTH = HG_HEADS * HG_KEY_DIM
HG_V_WIDTH = HG_K_WIDTH
IN0_SIZES = (3 * DN_WIDTH, DN_WIDTH, 2 * DN_HEADS, 2 * DN_HEADS,
             HG_K_WIDTH, 2 * HG_K_WIDTH, HG_V_WIDTH, HG_V_WIDTH)
DA_HEADS = 8
DA_HEAD_DIM = 128
N_EXPERTS = 64
TOP_K = 8
N_GROUPS = 8
TOPK_GROUPS = 4
ROUTED_SCALE = 2.5

VMEM_LIMIT_BYTES = 56 * 1024 * 1024
LANES = 128
MOD_ROWS = 8
ROW_TILE = 512
MM_TILE_M = 1024
MM_TILE_N = 1024
MM_EPILOGUE_CHUNKS = 4
ATTN_TILE_Q = 1024
ATTN_ROW_CHUNK = 128
EXPERT_SLOT_BLOCK = 256
EXPERT_BLOCKS_PER_STEP = 4
DMA_LOOP_UNROLL = 8
COMBINE_TILE = 256


def _params(*sem):
    return pltpu.CompilerParams(dimension_semantics=sem, vmem_limit_bytes=VMEM_LIMIT_BYTES)


def _sigmoid(x):
    return 0.5 * jnp.tanh(0.5 * x) + 0.5


def _silu(x):
    return x * _sigmoid(x)


def _mod_body(c_ref, w_ref, b_ref, o_ref):
    a = _silu(c_ref[...]).astype(BF16)
    o_ref[...] = jnp.dot(a, w_ref[...].astype(BF16), preferred_element_type=F32) + b_ref[...]


def modulation(cond, w, b, tn=1024):
    m, k = cond.shape
    n = w.shape[1]
    return pl.pallas_call(
        _mod_body, grid=(n // tn,),
        in_specs=[pl.BlockSpec((m, k), lambda j: (0, 0)),
                  pl.BlockSpec((k, tn), lambda j: (0, j)),
                  pl.BlockSpec((1, tn), lambda j: (0, j))],
        out_specs=pl.BlockSpec((m, tn), lambda j: (0, j)),
        out_shape=jax.ShapeDtypeStruct((m, n), F32),
        compiler_params=_params("arbitrary"), name="modulation",
    )(cond, w, b.reshape(1, n))


def _mod_row(tile, tile_rows, n_lat, lat_len, n_batch):
    assert lat_len % tile_rows == 0 and n_lat % tile_rows == 0, (tile_rows, lat_len, n_lat)
    start = tile * tile_rows
    return jnp.where(start < n_lat, start // lat_len, n_batch)


def _pack_bf16_pairs(h):
    half = h.shape[1] // 2
    bits = lax.bitcast_convert_type(h.astype(BF16).astype(F32), jnp.uint32)
    return (bits[:, :half] & jnp.uint32(0xFFFF0000)) | (bits[:, half:] >> jnp.uint32(16))


def _unpack_bf16_pairs(pk):
    hi = lax.bitcast_convert_type(pk & jnp.uint32(0xFFFF0000), F32).astype(BF16)
    lo = lax.bitcast_convert_type(pk << jnp.uint32(16), F32).astype(BF16)
    return hi, lo


def _store_slabs(ref, x, row0=0):
    r, w = x.shape
    c_n = w // LANES
    for c in range(c_n):
        ref[pl.ds(row0 * c_n + c, r, stride=c_n), :] = x[:, c * LANES:(c + 1) * LANES]


def _adaln_body(x_ref, nw_ref, shift_ref, scale_ref, o_ref):
    x = x_ref[...]
    y = x * lax.rsqrt(jnp.mean(x * x, axis=-1, keepdims=True) + NORM_EPS) * nw_ref[...]
    o_ref[...] = (y * (1.0 + scale_ref[0]) + shift_ref[0]).astype(o_ref.dtype)


def _adaln_router_body(x_ref, nw_ref, shift_ref, scale_ref, rw_ref, o_ref, lg_ref):
    x = x_ref[...]
    y = x * lax.rsqrt(jnp.mean(x * x, axis=-1, keepdims=True) + NORM_EPS) * nw_ref[...]
    h = y * (1.0 + scale_ref[0]) + shift_ref[0]
    _store_slabs(o_ref, _pack_bf16_pairs(h))
    lg_ref[...] = lax.dot_general(rw_ref[...], h, (((1,), (1,)), ((), ())), preferred_element_type=F32,
                                  precision=lax.Precision.HIGHEST)


def adaln(x, norm_w, mod3, shift_idx, geom, router_w=None, tl=ROW_TILE):
    t, d = x.shape
    n_lat, lat_len, n_batch = geom
    row = functools.partial(_mod_row, tile_rows=tl, n_lat=n_lat, lat_len=lat_len, n_batch=n_batch)
    in_specs = [pl.BlockSpec((tl, d), lambda i: (i, 0)),
                pl.BlockSpec((1, d), lambda i: (0, 0)),
                pl.BlockSpec((1, 1, d), lambda i: (row(i) * 6 + shift_idx, 0, 0)),
                pl.BlockSpec((1, 1, d), lambda i: (row(i) * 6 + shift_idx + 1, 0, 0))]
    args = [x, norm_w.reshape(1, d), mod3, mod3]
    if router_w is None:
        return pl.pallas_call(
            _adaln_body, grid=(t // tl,), in_specs=in_specs,
            out_specs=pl.BlockSpec((tl, d), lambda i: (i, 0)),
            out_shape=jax.ShapeDtypeStruct((t, d), BF16),
            compiler_params=_params("arbitrary"), name="adaln")(*args)
    e = router_w.shape[1]
    return pl.pallas_call(
        _adaln_router_body, grid=(t // tl,),
        in_specs=in_specs + [pl.BlockSpec((e, d), lambda i: (0, 0))],
        out_specs=[pl.BlockSpec((tl * (d // 2 // LANES), LANES), lambda i: (i, 0)),
                   pl.BlockSpec((e, tl), lambda i: (0, i))],
        out_shape=[jax.ShapeDtypeStruct((t * (d // 2 // LANES), LANES), jnp.uint32),
                   jax.ShapeDtypeStruct((e, t), F32)],
        compiler_params=_params("arbitrary"), name="adaln_router")(*args, router_w.T)


def _mm_body(a_ref, w_ref, o_ref, wb_ref):
    @pl.when(pl.program_id(1) == 0)
    def _():
        wb_ref[...] = w_ref[...].astype(BF16)
    o_ref[...] = jnp.dot(a_ref[...], wb_ref[...], preferred_element_type=F32).astype(o_ref.dtype)


def _mm_res_body(a_ref, w_ref, res_ref, gate_ref, o_ref, wb_ref):
    @pl.when(pl.program_id(1) == 0)
    def _():
        wb_ref[...] = w_ref[...].astype(BF16)
    acc = jnp.dot(a_ref[...], wb_ref[...], preferred_element_type=F32)
    o_ref[...] = res_ref[...] + gate_ref[0] * acc


def matmul(a, w, n_cols=None, col_block0=0, tm=MM_TILE_M, tn=MM_TILE_N, out_dtype=F32):
    m, k = a.shape
    n = w.shape[1] if n_cols is None else n_cols
    return pl.pallas_call(
        _mm_body, grid=(n // tn, m // tm),
        in_specs=[pl.BlockSpec((tm, k), lambda j, i: (i, 0)),
                  pl.BlockSpec((k, tn), lambda j, i: (0, j + col_block0))],
        out_specs=pl.BlockSpec((tm, tn), lambda j, i: (i, j)),
        out_shape=jax.ShapeDtypeStruct((m, n), out_dtype),
        scratch_shapes=[pltpu.VMEM((k, tn), BF16)],
        compiler_params=_params("arbitrary", "arbitrary"), name="matmul",
    )(a, w)


def matmul_residual(a, w, res, mod3, gate_idx, geom, tm=MM_TILE_M, tn=MM_TILE_N):
    m, k = a.shape
    n = w.shape[1]
    n_lat, lat_len, n_batch = geom
    row = functools.partial(_mod_row, tile_rows=tm, n_lat=n_lat, lat_len=lat_len, n_batch=n_batch)
    nb = n // tn
    return pl.pallas_call(
        _mm_res_body, grid=(n // tn, m // tm),
        in_specs=[pl.BlockSpec((tm, k), lambda j, i: (i, 0)),
                  pl.BlockSpec((k, tn), lambda j, i: (0, j)),
                  pl.BlockSpec((tm, tn), lambda j, i: (i, j)),
                  pl.BlockSpec((1, 1, tn), lambda j, i: (row(i) * 6 + gate_idx, 0, j))],
        out_specs=pl.BlockSpec((tm, tn), lambda j, i: (i, j)),
        out_shape=jax.ShapeDtypeStruct((m, n), F32),
        scratch_shapes=[pltpu.VMEM((k, tn), BF16)],
        compiler_params=_params("arbitrary", "arbitrary"), name="matmul_residual",
    )(a, w, res, mod3)


def _expert_body(blk_e_ref, blk_new_ref, next_e_ref, stage_ref, n_used_ref, x_ref, gu_hbm, dn_hbm, o_ref,
                 gu_stage, dn_stage, gub_ref, dnb_ref, sem):
    ff, d = dnb_ref.shape
    half = d // 2
    xc = half // LANES
    blk = x_ref.shape[0] // xc // EXPERT_BLOCKS_PER_STEP

    def weight_copies(e, slot):
        return (pltpu.make_async_copy(gu_hbm.at[e], gu_stage.at[slot], sem.at[0, slot]),
                pltpu.make_async_copy(dn_hbm.at[e], dn_stage.at[slot], sem.at[1, slot]))

    @pl.when(pl.program_id(0) == 0)
    def _():
        for cp in weight_copies(blk_e_ref[0], 0):
            cp.start()

    def one_block(b, sub):
        @pl.when(blk_new_ref[b] == 1)
        def _():
            slot = stage_ref[b]
            for cp in weight_copies(blk_e_ref[b], slot):
                cp.wait()
            gub_ref[...] = gu_stage[slot].astype(BF16)
            dnb_ref[...] = dn_stage[slot].astype(BF16)

            @pl.when(next_e_ref[b] >= 0)
            def _():
                for cp in weight_copies(next_e_ref[b], 1 - slot):
                    cp.start()

        @pl.when(b < n_used_ref[0])
        def _():
            h1 = None
            for p in range(2):
                cs = range(p * xc // 2, (p + 1) * xc // 2)
                pk = jnp.concatenate([x_ref[pl.ds(sub * blk * xc + c, blk, stride=xc), :] for c in cs], axis=1)
                x_hi, x_lo = _unpack_bf16_pairs(pk)
                k0 = p * half // 2
                part = (jnp.dot(x_hi, gub_ref[k0:k0 + half // 2], preferred_element_type=F32)
                        + jnp.dot(x_lo, gub_ref[half + k0:half + k0 + half // 2], preferred_element_type=F32))
                h1 = part if h1 is None else h1 + part
            act = (_silu(h1[:, :ff]) * h1[:, ff:]).astype(BF16)
            y = jnp.concatenate([jnp.dot(act, dnb_ref[:, p * half:(p + 1) * half], preferred_element_type=F32)
                                 for p in range(2)], axis=1)
            _store_slabs(o_ref, _pack_bf16_pairs(y), sub * blk)

        @pl.when(b >= n_used_ref[0])
        def _():
            o_ref[pl.ds(sub * blk * xc, blk * xc), :] = jnp.zeros((blk * xc, LANES), o_ref.dtype)

    for sub in range(EXPERT_BLOCKS_PER_STEP):
        one_block(pl.program_id(0) * EXPERT_BLOCKS_PER_STEP + sub, sub)


def expert_blocks(x, gate_up, down, blk_e, n_used, blk):
    _, d, f2 = gate_up.shape
    xc = yc = d // 2 // LANES
    s = x.shape[0] // xc
    n_blk = s // blk
    assert n_blk % EXPERT_BLOCKS_PER_STEP == 0, (n_blk, EXPERT_BLOCKS_PER_STEP)
    pos = jnp.arange(n_blk, dtype=jnp.int32)
    used = pos < n_used[0]
    blk_new = (jnp.concatenate([jnp.ones((1,), bool), blk_e[1:] != blk_e[:-1]]) & used).astype(jnp.int32)
    stage = (jnp.cumsum(blk_new) - 1) % 2
    first_pos = jnp.where(blk_new == 1, pos, n_blk)
    next_first = jnp.concatenate([lax.cummin(first_pos, reverse=True)[1:], jnp.full((1,), n_blk, jnp.int32)])
    next_e = jnp.where(next_first < n_blk, blk_e[jnp.minimum(next_first, n_blk - 1)], -1).astype(jnp.int32)
    bps = EXPERT_BLOCKS_PER_STEP
    grid_spec = pltpu.PrefetchScalarGridSpec(
        num_scalar_prefetch=5, grid=(n_blk // bps,),
        in_specs=[pl.BlockSpec((bps * blk * xc, LANES),
                               lambda i, be, bn, ne, st, nu: (jnp.minimum(i, (nu[0] - 1) // bps), 0)),
                  pl.BlockSpec(memory_space=pl.ANY),
                  pl.BlockSpec(memory_space=pl.ANY)],
        out_specs=pl.BlockSpec((bps * blk * yc, LANES), lambda i, be, bn, ne, st, nu: (i, 0)),
        scratch_shapes=[pltpu.VMEM((2, d, f2), F32), pltpu.VMEM((2, f2 // 2, d), F32),
                        pltpu.VMEM((d, f2), BF16), pltpu.VMEM((f2 // 2, d), BF16),
                        pltpu.SemaphoreType.DMA((2, 2))])
    return pl.pallas_call(
        _expert_body, grid_spec=grid_spec,
        out_shape=jax.ShapeDtypeStruct((s * yc, LANES), jnp.uint32),
        compiler_params=_params("arbitrary"), name="expert_blocks",
    )(blk_e, blk_new, next_e, stage.astype(jnp.int32), n_used, x, gate_up, down)


def _norm_rope(x, w, cos, sin, first, head_dim):
    y = x * lax.rsqrt(jnp.mean(x * x, axis=-1, keepdims=True) + NORM_EPS) * w
    swapped = jnp.where(first, pltpu.roll(y, head_dim - head_dim // 4, 1), pltpu.roll(y, head_dim // 4, 1))
    return y * cos + swapped * sin


def _mm_qkv_body(a_ref, w_ref, qw_ref, kw_ref, cos_ref, sin_ref, o_ref, wb_ref, *, head_dim, n_q_tiles, n_k_tiles):
    j = pl.program_id(0)

    @pl.when(pl.program_id(1) == 0)
    def _():
        wb_ref[...] = w_ref[...].astype(BF16)

    tm, tn = o_ref.shape
    row_chunk = tm // MM_EPILOGUE_CHUNKS

    def rows_chain(rows, nw_ref):
        acc = jnp.dot(a_ref[rows, :], wb_ref[...], preferred_element_type=F32)
        yield
        if nw_ref is None:
            o_ref[rows, :] = acc.astype(o_ref.dtype)
            return
        cos, sin = cos_ref[rows, :], sin_ref[rows, :]
        lane = lax.broadcasted_iota(jnp.int32, cos.shape, 1)
        first = (lane % (head_dim // 2)) < (head_dim // 4)
        for g in range(tn // head_dim):
            sl = slice(g * head_dim, (g + 1) * head_dim)
            o_ref[rows, sl] = _norm_rope(acc[:, sl], nw_ref[...], cos, sin, first, head_dim).astype(o_ref.dtype)
            yield

    def tile(nw_ref):
        _run_interleaved(rows_chain(slice(r, r + row_chunk), nw_ref) for r in range(0, tm, row_chunk))

    @pl.when(j < n_q_tiles)
    def _():
        tile(qw_ref)

    @pl.when((j >= n_q_tiles) & (j < n_q_tiles + n_k_tiles))
    def _():
        tile(kw_ref)

    @pl.when(j >= n_q_tiles + n_k_tiles)
    def _():
        tile(None)


def matmul_qkv(a, w, q_norm, k_norm, cos_t, sin_t, tm=MM_TILE_M, tn=MM_TILE_N):
    m, k = a.shape
    n = w.shape[1]
    hd = q_norm.shape[0]
    body = functools.partial(_mm_qkv_body, head_dim=hd, n_q_tiles=n // 3 // tn, n_k_tiles=n // 3 // tn)
    return pl.pallas_call(
        body, grid=(n // tn, m // tm),
        in_specs=[pl.BlockSpec((tm, k), lambda j, i: (i, 0)),
                  pl.BlockSpec((k, tn), lambda j, i: (0, j)),
                  pl.BlockSpec((1, hd), lambda j, i: (0, 0)),
                  pl.BlockSpec((1, hd), lambda j, i: (0, 0)),
                  pl.BlockSpec((tm, hd), lambda j, i: (i, 0)),
                  pl.BlockSpec((tm, hd), lambda j, i: (i, 0))],
        out_specs=pl.BlockSpec((tm, tn), lambda j, i: (i, j)),
        out_shape=jax.ShapeDtypeStruct((m, n), BF16),
        scratch_shapes=[pltpu.VMEM((k, tn), BF16)],
        compiler_params=_params("arbitrary", "arbitrary"), name="matmul_qkv",
    )(a, w, q_norm.reshape(1, hd), k_norm.reshape(1, hd), cos_t, sin_t)


def rope_tables(n_lat_tokens_per_sample, n_batch, n_ctx_tokens, head_dim):
    quarter = head_dim // 4
    inv_freq = ROPE_BASE ** (-jnp.arange(quarter, dtype=F32) / quarter)
    rows = n_lat_tokens_per_sample // GRID_W
    row = jnp.repeat(jnp.arange(rows, dtype=F32), GRID_W)
    col = jnp.tile(jnp.arange(GRID_W, dtype=F32), rows)
    ang_r = row[:, None] * inv_freq[None, :]
    ang_c = col[:, None] * inv_freq[None, :]
    cos = jnp.concatenate([jnp.cos(ang_r), jnp.cos(ang_r), jnp.cos(ang_c), jnp.cos(ang_c)], axis=-1)
    sin = jnp.concatenate([-jnp.sin(ang_r), jnp.sin(ang_r), -jnp.sin(ang_c), jnp.sin(ang_c)], axis=-1)
    cos = jnp.concatenate([jnp.tile(cos, (n_batch, 1)), jnp.ones((n_ctx_tokens, head_dim), F32)], axis=0)
    sin = jnp.concatenate([jnp.tile(sin, (n_batch, 1)), jnp.zeros((n_ctx_tokens, head_dim), F32)], axis=0)
    return cos, sin


def _diff_attn_rows(lam, q_ref, k_all, v_all, sw_ref, o_ref, rows, head_dim, out_scale):
    c = head_dim ** -0.5 * math.log2(math.e)
    es, invs = [], []
    for s in range(2):
        sl = slice(s * head_dim, (s + 1) * head_dim)
        sc = lax.dot_general(q_ref[rows, sl], k_all[:, sl], (((1,), (1,)), ((), ())), preferred_element_type=F32)
        yield
        e = jnp.exp2((sc - jnp.max(sc, axis=-1, keepdims=True)) * c)
        invs.append(1.0 / jnp.sum(e, axis=-1, keepdims=True))
        es.append(e.astype(BF16))
        yield
    v = v_all[...]
    o0 = jnp.dot(es[0], v, preferred_element_type=F32)
    o1 = jnp.dot(es[1], v, preferred_element_type=F32)
    yield
    o = o0 * invs[0] - (lam * invs[1]) * o1
    y = o * lax.rsqrt(jnp.mean(o * o, axis=-1, keepdims=True) + NORM_EPS) * sw_ref[...]
    o_ref[rows, :] = (y * out_scale).astype(o_ref.dtype)
    yield


def _diff_attn_body(lam_ref, q_ref, kl_ref, kc_ref, vl_ref, vc_ref, sw_ref, o_ref, k_all, v_all, *,
                    head_dim, out_scale, row_chunk):
    @pl.when(pl.program_id(2) == 0)
    def _():
        n_l = kl_ref.shape[0]
        k_all[:n_l] = kl_ref[...]
        k_all[n_l:] = kc_ref[...]
        v_all[:n_l] = vl_ref[...]
        v_all[n_l:] = vc_ref[...]

    lam = lam_ref[0]
    tq = q_ref.shape[0]
    _run_interleaved(
        _diff_attn_rows(lam, q_ref, k_all, v_all, sw_ref, o_ref, slice(r, r + row_chunk), head_dim, out_scale)
        for r in range(0, tq, row_chunk))


def diff_attention(qkv, lmbda, sub_norm, n_batch, lat_len, ctx_len, n_heads, head_dim, out_scale,
                   tq=ATTN_TILE_Q, row_chunk=ATTN_ROW_CHUNK):
    hw = 2 * head_dim
    nq = lat_len // tq
    ctx_blk0 = n_batch * lat_len // ctx_len
    body = functools.partial(_diff_attn_body, head_dim=head_dim, out_scale=out_scale, row_chunk=row_chunk)
    n_keys = lat_len + ctx_len
    return pl.pallas_call(
        body, grid=(n_batch, n_heads, nq),
        in_specs=[pl.BlockSpec(memory_space=pltpu.SMEM),
                  pl.BlockSpec((tq, hw), lambda b, h, i: (b * nq + i, h)),
                  pl.BlockSpec((lat_len, hw), lambda b, h, i: (b, n_heads + h)),
                  pl.BlockSpec((ctx_len, hw), lambda b, h, i: (ctx_blk0 + b, n_heads + h)),
                  pl.BlockSpec((lat_len, hw), lambda b, h, i: (b, 2 * n_heads + h)),
                  pl.BlockSpec((ctx_len, hw), lambda b, h, i: (ctx_blk0 + b, 2 * n_heads + h)),
                  pl.BlockSpec((1, hw), lambda b, h, i: (0, 0))],
        out_specs=pl.BlockSpec((tq, hw), lambda b, h, i: (b * nq + i, h)),
        out_shape=jax.ShapeDtypeStruct((n_batch * lat_len, n_heads * hw), BF16),
        scratch_shapes=[pltpu.VMEM((n_keys, hw), BF16), pltpu.VMEM((n_keys, hw), BF16)],
        compiler_params=_params("arbitrary", "arbitrary", "arbitrary"), name="diff_attention",
    )(lmbda.reshape(1), qkv, qkv, qkv, qkv, qkv, sub_norm.reshape(1, hw))


SCAN_TILE = 256
CHUNK = 64
SUB = 16
NEG_BIG = -1e30
NEG_INF = float("-inf")
DELTA_INV_PASSES = 1
DELTA_HEADS_PER_STEP = 4
HGRN_HEADS_PER_STEP = 4


def _dot(a, b):
    return jnp.dot(a.astype(BF16), b.astype(BF16), preferred_element_type=F32)


def _dot_nt(a, b):
    return lax.dot_general(a.astype(BF16), b.astype(BF16), (((1,), (1,)), ((), ())),
                           preferred_element_type=F32)


def _split3(x):
    hi = x.astype(BF16)
    r = x - hi.astype(F32)
    mid = r.astype(BF16)
    lo = (r - mid.astype(F32)).astype(BF16)
    return hi, mid, lo


def _dot_exact_lhs01(m01, x):
    hi, mid, lo = _split3(x)
    m = m01.astype(BF16)
    return (jnp.dot(m, hi, preferred_element_type=F32) + jnp.dot(m, mid, preferred_element_type=F32)
            + jnp.dot(m, lo, preferred_element_type=F32))


def _dot_exact_rhs01(x, m01):
    hi, mid, lo = _split3(x)
    m = m01.astype(BF16)
    return (jnp.dot(hi, m, preferred_element_type=F32) + jnp.dot(mid, m, preferred_element_type=F32)
            + jnp.dot(lo, m, preferred_element_type=F32))


def _dot3(a, b):
    ah = a.astype(BF16)
    al = (a - ah.astype(F32)).astype(BF16)
    bh = b.astype(BF16)
    bl = (b - bh.astype(F32)).astype(BF16)
    return (jnp.dot(ah, bh, preferred_element_type=F32) + jnp.dot(ah, bl, preferred_element_type=F32)
            + jnp.dot(al, bh, preferred_element_type=F32))


def _tile_masks(n, reverse):
    i = lax.broadcasted_iota(jnp.int32, (n, n), 0)
    j = lax.broadcasted_iota(jnp.int32, (n, n), 1)
    same = (i // CHUNK) == (j // CHUNK)
    if reverse:
        return same & (i <= j), same & (i < j)
    return same & (i >= j), same & (i > j)


def _segment_tile(b, s, reverse, nl, nc, ctx_tile0):
    if reverse:
        return jnp.where(s < nc, ctx_tile0 + b * nc + (nc - 1 - s), b * nl + (nl - 1 - (s - nc)))
    return jnp.where(s < nc, ctx_tile0 + b * nc + s, b * nl + (s - nc))


def _dn_prep_body(x_ref, prev_ref, next_ref, w_ref, o_ref, *, tiles_per_lat_seg, tiles_per_ctx_seg, n_lat_tiles,
                  head_dim, q_scale):
    i = pl.program_id(0)
    j = pl.program_id(1)
    is_lat = i < n_lat_tiles
    pos = jnp.where(is_lat, i % tiles_per_lat_seg, (i - n_lat_tiles) % tiles_per_ctx_seg)
    seg_first = pos == 0
    seg_last = pos == jnp.where(is_lat, tiles_per_lat_seg, tiles_per_ctx_seg) - 1
    x = x_ref[...]
    tl = x.shape[0]
    prev = jnp.where(seg_first, 0.0, prev_ref[...])
    nxt = jnp.where(seg_last, 0.0, next_ref[...])
    xp = jnp.concatenate([prev, x, nxt], axis=0)
    w = w_ref[...]
    n_taps = 5
    acc = None
    for t in range(n_taps):
        off = 8 + t - n_taps // 2
        term = xp[off:off + tl] * w[t:t + 1]
        acc = term if acc is None else acc + term
    y = _silu(acc)
    scale = jnp.where(j == 0, q_scale, 1.0)
    outs = []
    for h in range(y.shape[1] // head_dim):
        yh = y[:, h * head_dim:(h + 1) * head_dim]
        nrm = lax.rsqrt(jnp.sum(yh * yh, axis=-1, keepdims=True) + 1e-6) * scale
        outs.append(yh * jnp.where(j == 2, 1.0, nrm))
    o_ref[...] = jnp.concatenate(outs, axis=1)


def dn_prep(proj_dn, conv_w_t, geom, ctx_len, width, head_dim):
    n_lat, lat_len, n_batch = geom
    t = proj_dn.shape[0]
    tl = SCAN_TILE
    rows8 = tl // 8
    n_tiles = t // tl
    body = functools.partial(_dn_prep_body, tiles_per_lat_seg=lat_len // tl, tiles_per_ctx_seg=ctx_len // tl,
                             n_lat_tiles=n_lat // tl,
                             head_dim=head_dim, q_scale=head_dim ** -0.5)
    last8 = t // 8 - 1
    return pl.pallas_call(
        body, grid=(n_tiles, 3),
        in_specs=[pl.BlockSpec((tl, width), lambda i, j: (i, j)),
                  pl.BlockSpec((8, width), lambda i, j: (jnp.maximum(i * rows8 - 1, 0), j)),
                  pl.BlockSpec((8, width), lambda i, j: (jnp.minimum((i + 1) * rows8, last8), j)),
                  pl.BlockSpec((8, width), lambda i, j: (0, j))],
        out_specs=pl.BlockSpec((tl, width), lambda i, j: (i, j)),
        out_shape=jax.ShapeDtypeStruct((t, 3 * width), F32),
        compiler_params=_params("arbitrary", "arbitrary"), name="dn_prep",
    )(proj_dn, proj_dn, proj_dn, conv_w_t)


def _softplus(x):
    return jnp.maximum(x, 0.0) + jnp.log(1.0 + jnp.exp(-jnp.abs(x)))


def _gate_prep_body(h_ref, wc_ref, wr_ref, alog_c_ref, dtb_c_ref, alog_r_ref, dtb_r_ref, gc_ref, gr_ref, *, n_heads):
    h = h_ref[...]
    tl = h.shape[0]
    nd = 2 * n_heads
    raw_c = jnp.dot(h, wc_ref[...].astype(BF16), preferred_element_type=F32)
    raw_r = lax.dot_general(wr_ref[...].astype(BF16), h, (((1,), (1,)), ((), ())),
                            preferred_element_type=F32)
    g_c = -jnp.exp(alog_c_ref[...]) * _softplus(raw_c[:, :nd] + dtb_c_ref[...])
    g_r = -jnp.exp(alog_r_ref[...]) * _softplus(raw_r[:nd, :] + dtb_r_ref[...])
    incl_f, _ = _tile_masks(tl, False)
    incl_b, _ = _tile_masks(tl, True)
    one_f = jnp.where(incl_f, 1.0, 0.0)
    one_b = jnp.where(incl_b, 1.0, 0.0)
    cum_c = jnp.concatenate([_dot_exact_lhs01(one_f, g_c[:, :n_heads]),
                             _dot_exact_lhs01(one_b, g_c[:, n_heads:])], axis=1)
    cum_r = jnp.concatenate([_dot_exact_rhs01(g_r[:n_heads, :], one_b),
                             _dot_exact_rhs01(g_r[n_heads:, :], one_f)], axis=0)
    beta_c = _sigmoid(raw_c[:, nd:])
    gc_ref[...] = jnp.concatenate([cum_c, beta_c], axis=1)
    gr_ref[...] = jnp.concatenate([cum_r, jnp.zeros_like(cum_r)], axis=0)


def gate_prep(h_bf, w_gate, a_log, dt_bias, n_heads):
    t, d = h_bf.shape
    tl = SCAN_TILE
    nd = 2 * n_heads
    body = functools.partial(_gate_prep_body, n_heads=n_heads)
    full = lambda shape: pl.BlockSpec(shape, lambda i: (0, 0))
    return pl.pallas_call(
        body, grid=(t // tl,),
        in_specs=[pl.BlockSpec((tl, d), lambda i: (i, 0)), full((d, 2 * nd)), full((2 * nd, d)),
                  full((1, nd)), full((1, nd)), full((nd, 1)), full((nd, 1))],
        out_specs=[pl.BlockSpec((tl, 2 * nd), lambda i: (i, 0)), pl.BlockSpec((2 * nd, tl), lambda i: (0, i))],
        out_shape=[jax.ShapeDtypeStruct((t, 2 * nd), F32), jax.ShapeDtypeStruct((2 * nd, t), F32)],
        compiler_params=_params("arbitrary"), name="gate_prep",
    )(h_bf, w_gate, w_gate.T, a_log.reshape(1, nd), dt_bias.reshape(1, nd),
      a_log.reshape(nd, 1), dt_bias.reshape(nd, 1))


def _select_col(x, idx):
    lane = lax.broadcasted_iota(jnp.int32, x.shape, 1)
    return jnp.sum(jnp.where(lane == idx, x, 0.0), axis=1, keepdims=True)


def _select_row(x, idx):
    row = lax.broadcasted_iota(jnp.int32, x.shape, 0)
    return jnp.sum(jnp.where(row == idx, x, 0.0), axis=0, keepdims=True)


def _run_interleaved(chains):
    chains = list(chains)
    while chains:
        alive = []
        for ch in chains:
            try:
                next(ch)
                alive.append(ch)
            except StopIteration:
                pass
        chains = alive


def _delta_chain(q, k, v, gc_col, gc_row, beta_col, s_ref, o_ref, cols, reverse, inv_passes):
    tl, kd_ = k.shape
    n_chunks = tl // CHUNK
    incl, strict = _tile_masks(tl, reverse)
    decay = jnp.exp(jnp.where(incl, gc_col - gc_row, NEG_BIG))
    kb, qb = k.astype(BF16), q.astype(BF16)
    kkt = _dot_nt(kb, kb)
    qkt = _dot_nt(qb, kb)
    yield
    x = jnp.where(strict, kkt * (-beta_col) * decay, 0.0)
    dot_inv = _dot3 if inv_passes == 3 else _dot
    ri = lax.broadcasted_iota(jnp.int32, (tl, tl), 0)
    ci = lax.broadcasted_iota(jnp.int32, (tl, tl), 1)
    r = jnp.where(ri == ci, 1.0, 0.0) + x
    n_sq = int(math.log2(CHUNK)) - 1
    for _ in range(n_sq):
        x = dot_inv(x, x)
        r = r + dot_inv(r, x)
        yield
    e_g = jnp.exp(gc_col)
    rhs = jnp.concatenate([v * beta_col, k * (beta_col * e_g)], axis=1)
    sol = dot_inv(r, rhs)
    yield
    u0, w = sol[:, :v.shape[1]], sol[:, v.shape[1]:]
    attn = (qkt * decay).astype(BF16)
    a_sol = _dot(attn, sol)
    o0 = a_sol[:, :v.shape[1]]
    qe = q * e_g - a_sol[:, v.shape[1]:]
    tot_rows = []
    for c in range(n_chunks):
        last = c * CHUNK if reverse else c * CHUNK + CHUNK - 1
        tot_rows.append(jnp.broadcast_to(gc_col[last:last + 1, :], (CHUNK, 1)))
    tot = jnp.concatenate(tot_rows, axis=0)
    kdec_t = jnp.transpose(k * jnp.exp(tot - gc_col))
    wu = jnp.concatenate([-w, u0], axis=1).astype(BF16)
    lane = lax.broadcasted_iota(jnp.int32, kdec_t.shape, 1)
    pns = [_dot(jnp.where((lane // CHUNK) == c, kdec_t, 0.0), wu) for c in range(n_chunks)]
    yield
    order = range(n_chunks - 1, -1, -1) if reverse else range(n_chunks)
    for c in order:
        rows = slice(c * CHUNK, (c + 1) * CHUNK)
        pn = pns[c]
        lhs = jnp.concatenate([qe[rows], pn[:, :kd_]], axis=0)
        s = s_ref[...]
        res = _dot(lhs, s)
        o_ref[rows, cols] = o0[rows] + res[:CHUNK]
        last = c * CHUNK if reverse else c * CHUNK + CHUNK - 1
        gl = jnp.exp(gc_col[last:last + 1, :])
        s_ref[...] = gl * s + res[CHUNK:] + pn[:, kd_:]
        yield


def _delta_body(qf_ref, kf_ref, vf_ref, gcf_ref, grf_ref, qb_ref, kb_ref, vb_ref, gcb_ref, grb_ref,
                of_ref, ob_ref, sf_ref, sb_ref, *, n_heads, head_dim, inv_passes):
    hg = pl.program_id(1)
    heads_per_step = sf_ref.shape[0]

    @pl.when(pl.program_id(2) == 0)
    def _():
        sf_ref[...] = jnp.zeros_like(sf_ref)
        sb_ref[...] = jnp.zeros_like(sb_ref)

    chains = []
    for g in range(heads_per_step):
        cols = slice(g * head_dim, (g + 1) * head_dim)
        for reverse, (q_ref, k_ref, v_ref, gc_ref, gr_ref, o_ref, s_ref) in enumerate(
                [(qf_ref, kf_ref, vf_ref, gcf_ref, grf_ref, of_ref, sf_ref),
                 (qb_ref, kb_ref, vb_ref, gcb_ref, grb_ref, ob_ref, sb_ref)]):
            idx = reverse * n_heads + hg * heads_per_step + g
            gcs = gc_ref[...]
            gc_col = _select_col(gcs, idx)
            beta_col = _select_col(gcs, 2 * n_heads + idx)
            gc_row = _select_row(gr_ref[...], idx)
            chains.append(_delta_chain(q_ref[:, cols], k_ref[:, cols], v_ref[:, cols], gc_col, gc_row, beta_col,
                                       s_ref.at[g], o_ref, cols, bool(reverse), inv_passes))
    _run_interleaved(chains)


def delta_scan(qkv, gc, gr, geom, ctx_len, n_heads, head_dim, inv_passes=3, heads_per_step=DELTA_HEADS_PER_STEP):
    n_lat, lat_len, n_batch = geom
    t = qkv.shape[0]
    tl = SCAN_TILE
    nl, nc, ctx0 = lat_len // tl, ctx_len // tl, n_lat // tl
    n_hg = n_heads // heads_per_step
    gw = heads_per_step * head_dim
    tile = functools.partial(_segment_tile, nl=nl, nc=nc, ctx_tile0=ctx0)
    specs = []
    for reverse in (False, True):
        tix = functools.partial(tile, reverse=reverse)
        specs += [pl.BlockSpec((tl, gw), lambda b, h, s, tix=tix: (tix(b, s), h)),
                  pl.BlockSpec((tl, gw), lambda b, h, s, tix=tix: (tix(b, s), n_hg + h)),
                  pl.BlockSpec((tl, gw), lambda b, h, s, tix=tix: (tix(b, s), 2 * n_hg + h)),
                  pl.BlockSpec((tl, 4 * n_heads), lambda b, h, s, tix=tix: (tix(b, s), 0)),
                  pl.BlockSpec((4 * n_heads, tl), lambda b, h, s, tix=tix: (0, tix(b, s)))]
    out_specs = [pl.BlockSpec((tl, gw), lambda b, h, s, tix=functools.partial(tile, reverse=r): (tix(b, s), h))
                 for r in (False, True)]
    body = functools.partial(_delta_body, n_heads=n_heads, head_dim=head_dim, inv_passes=inv_passes)
    return pl.pallas_call(
        body, grid=(n_batch, n_hg, nl + nc), in_specs=specs, out_specs=out_specs,
        out_shape=[jax.ShapeDtypeStruct((t, n_heads * head_dim), F32)] * 2,
        scratch_shapes=[pltpu.VMEM((heads_per_step, head_dim, head_dim), F32)] * 2,
        compiler_params=_params("arbitrary", "arbitrary", "arbitrary"), name="delta_scan",
    )(qkv, qkv, qkv, gc, gr, qkv, qkv, qkv, gc, gr)


def _hgrn_diag(q, k, cum, v, stage_ref, diag_ref, reverse):
    n_sub = q.shape[0] // SUB
    for n, val in enumerate((q, k, cum, v)):
        stage_ref[n] = val

    def slabs(n):
        return [stage_ref.at[n][pl.ds(r, n_sub, stride=SUB), :] for r in range(SUB)]

    q_x, k_x, p_x, v_x = slabs(0), slabs(1), slabs(2), slabs(3)
    pairs = [(i, j) for i in range(SUB) for j in range(SUB) if (j >= i if reverse else j <= i)]
    terms = []
    for i, j in pairs:
        qk = q_x[i] * k_x[j]
        terms.append((qk if i == j else qk * jnp.exp(p_x[i] - p_x[j])).astype(BF16))
    kd_ = terms[0].shape[1]
    a_rep = jnp.dot(jnp.concatenate(terms, axis=0), jnp.ones((kd_, kd_), BF16), preferred_element_type=F32)
    o_x = [None] * SUB
    for n, (i, j) in enumerate(pairs):
        contrib = a_rep[n * n_sub:(n + 1) * n_sub] * v_x[j]
        o_x[i] = contrib if o_x[i] is None else o_x[i] + contrib
    for r in range(SUB):
        diag_ref[pl.ds(r, n_sub, stride=SUB), :] = o_x[r]


def _hgrn_chain(hq_ref, hf_ref, hv_ref, cols, lb, st_ref, o_ref, stage_ref, diag_ref, reverse):
    hq, hf, hv = hq_ref[:, cols], hf_ref[:, cols], hv_ref[:, cols]
    tl, kd_ = hq.shape
    n_chunks = tl // CHUNK
    n_sub = tl // SUB
    sub_per_chunk = CHUNK // SUB
    q = _silu(hq)
    f = lb + (1.0 - lb) * _sigmoid(hf)
    k = 1.0 - f
    lf = jnp.log(f)
    incl, _ = _tile_masks(tl, reverse)
    cum = _dot_exact_lhs01(jnp.where(incl, 1.0, 0.0), lf)
    excl = cum - lf

    def bcast_rows(src, row, n):
        return jnp.broadcast_to(src[row:row + 1, :], (n, kd_))

    chunk_last = [(c * CHUNK if reverse else c * CHUNK + CHUNK - 1) for c in range(n_chunks)]
    tot = jnp.concatenate([bcast_rows(cum, chunk_last[c], CHUNK) for c in range(n_chunks)], axis=0)
    sub_first = [(m * SUB + SUB - 1 if reverse else m * SUB) for m in range(n_sub)]
    r_sub = jnp.concatenate([bcast_rows(excl, sub_first[m], SUB) for m in range(n_sub)], axis=0)
    q_t = q * jnp.exp(cum - r_sub)
    qd = q * jnp.exp(cum)
    kd = k * jnp.exp(tot - cum)
    vb = hv.astype(BF16)

    i = lax.broadcasted_iota(jnp.int32, (tl, tl), 0)
    j = lax.broadcasted_iota(jnp.int32, (tl, tl), 1)
    same = (i // CHUNK) == (j // CHUNK)
    pos_i = (i % CHUNK) // SUB
    pos_j = (j % CHUNK) // SUB
    if reverse:
        pos_i, pos_j = sub_per_chunk - 1 - pos_i, sub_per_chunk - 1 - pos_j
    a_off = jnp.zeros((tl, tl), F32)
    for lvl in range(1, sub_per_chunk):
        ref_rows = []
        for c in range(n_chunks):
            m = c * sub_per_chunk + (sub_per_chunk - 1 - lvl if reverse else lvl)
            ref_rows.append(bcast_rows(excl, sub_first[m], CHUNK))
        r_lvl = jnp.concatenate(ref_rows, axis=0)
        k_t = k * jnp.exp(jnp.minimum(r_lvl - cum, 0.0))
        a_l = _dot_nt(q_t, k_t)
        a_off = a_off + jnp.where(same & (pos_i == lvl) & (pos_j < lvl), a_l, 0.0)
        yield
    o_intra = _dot(a_off, vb)
    yield

    _hgrn_diag(q, k, cum, hv, stage_ref, diag_ref, reverse)
    yield
    o_intra = o_intra + diag_ref[...]

    v_t = jnp.transpose(hv)
    lane = lax.broadcasted_iota(jnp.int32, v_t.shape, 1)
    kdb = kd.astype(BF16)
    n_ts = [_dot(jnp.where((lane // CHUNK) == c, v_t, 0.0), kdb) for c in range(n_chunks)]
    yield
    order = range(n_chunks - 1, -1, -1) if reverse else range(n_chunks)
    for c in order:
        rows = slice(c * CHUNK, (c + 1) * CHUNK)
        st = st_ref[...]
        o_ref[rows, cols] = o_intra[rows] + _dot_nt(qd[rows], st)
        st_ref[...] = st * jnp.exp(cum[chunk_last[c]:chunk_last[c] + 1, :]) + n_ts[c]
        yield


def _hgrn_body(qf_ref, ff_ref, vf_ref, qb_ref, fb_ref, vb_ref, lb_ref, of_ref, ob_ref, sf_ref, sb_ref, stage_ref,
               diag_ref, *, key_dim):
    heads_per_step = sf_ref.shape[0]

    @pl.when(pl.program_id(2) == 0)
    def _():
        sf_ref[...] = jnp.zeros_like(sf_ref)
        sb_ref[...] = jnp.zeros_like(sb_ref)

    chains = []
    for g in range(heads_per_step):
        cols = slice(g * key_dim, (g + 1) * key_dim)
        lb = lb_ref[:, cols]
        chains.append(_hgrn_chain(qf_ref, ff_ref, vf_ref, cols, lb, sf_ref.at[g], of_ref,
                                  stage_ref.at[2 * g], diag_ref.at[2 * g], False))
        chains.append(_hgrn_chain(qb_ref, fb_ref, vb_ref, cols, lb, sb_ref.at[g], ob_ref,
                                  stage_ref.at[2 * g + 1], diag_ref.at[2 * g + 1], True))
    _run_interleaved(chains)


def hgrn_scan(proj_hg, lb, geom, ctx_len, n_heads, key_dim, heads_per_step=HGRN_HEADS_PER_STEP):
    n_lat, lat_len, n_batch = geom
    t = proj_hg.shape[0]
    tl = SCAN_TILE
    nl, nc, ctx0 = lat_len // tl, ctx_len // tl, n_lat // tl
    n_hg = n_heads // heads_per_step
    gw = heads_per_step * key_dim
    tile = functools.partial(_segment_tile, nl=nl, nc=nc, ctx_tile0=ctx0)
    specs = []
    for reverse in (False, True):
        tix = functools.partial(tile, reverse=reverse)
        fcol = (1 + int(reverse)) * n_hg
        specs += [pl.BlockSpec((tl, gw), lambda b, h, s, tix=tix: (tix(b, s), h)),
                  pl.BlockSpec((tl, gw), lambda b, h, s, tix=tix, fcol=fcol: (tix(b, s), fcol + h)),
                  pl.BlockSpec((tl, gw), lambda b, h, s, tix=tix: (tix(b, s), 3 * n_hg + h))]
    specs.append(pl.BlockSpec((1, gw), lambda b, h, s: (0, h)))
    out_specs = [pl.BlockSpec((tl, gw), lambda b, h, s, tix=functools.partial(tile, reverse=r): (tix(b, s), h))
                 for r in (False, True)]
    return pl.pallas_call(
        functools.partial(_hgrn_body, key_dim=key_dim), grid=(n_batch, n_hg, nl + nc),
        in_specs=specs, out_specs=out_specs,
        out_shape=[jax.ShapeDtypeStruct((t, n_heads * key_dim), F32)] * 2,
        scratch_shapes=[pltpu.VMEM((heads_per_step, key_dim, key_dim), F32)] * 2
        + [pltpu.VMEM((2 * heads_per_step, 4, tl, key_dim), F32), pltpu.VMEM((2 * heads_per_step, tl, key_dim), F32)],
        compiler_params=_params("arbitrary", "arbitrary", "arbitrary"), name="hgrn_scan",
    )(proj_hg, proj_hg, proj_hg, proj_hg, proj_hg, proj_hg, lb)


def _mix_out2_body(df_ref, db_ref, hf_ref, hb_ref, z_ref, og_ref, dnw_ref, hgw_ref, o_ref, *, head_dim):
    def normed(o, nw):
        outs = []
        for h in range(o.shape[1] // head_dim):
            oh = o[:, h * head_dim:(h + 1) * head_dim]
            outs.append(oh * lax.rsqrt(jnp.mean(oh * oh, axis=-1, keepdims=True) + NORM_EPS) * nw)
        return jnp.concatenate(outs, axis=1)

    dn = normed(df_ref[...] + db_ref[...], dnw_ref[...]) * _silu(z_ref[...])
    hg = normed(hf_ref[...] + hb_ref[...], hgw_ref[...]) * _sigmoid(og_ref[...])
    half = dn.shape[1]
    o_ref[:, :half] = dn.astype(o_ref.dtype)
    o_ref[:, half:] = hg.astype(o_ref.dtype)


def mix_out(dn_f, dn_b, hg_f, hg_b, z_src, z_blk, og_src, og_blk, dn_norm, hg_norm, head_dim):
    t, w = dn_f.shape
    tl = SCAN_TILE
    row = lambda i: (i, 0)
    body = functools.partial(_mix_out2_body, head_dim=head_dim)
    return pl.pallas_call(
        body, grid=(t // tl,),
        in_specs=[pl.BlockSpec((tl, w), row)] * 4
        + [pl.BlockSpec((tl, w), lambda i: (i, z_blk)), pl.BlockSpec((tl, w), lambda i: (i, og_blk)),
           pl.BlockSpec((1, head_dim), lambda i: (0, 0)), pl.BlockSpec((1, head_dim), lambda i: (0, 0))],
        out_specs=pl.BlockSpec((tl, 2 * w), row),
        out_shape=jax.ShapeDtypeStruct((t, 2 * w), BF16),
        compiler_params=_params("arbitrary"), name="mix_out",
    )(dn_f, dn_b, hg_f, hg_b, z_src, og_src, dn_norm.reshape(1, head_dim), hg_norm.reshape(1, head_dim))


def _first_max(x, ids, n):
    m = jnp.max(x, axis=0, keepdims=True)
    first = jnp.min(jnp.where(x == m, ids, n), axis=0, keepdims=True)
    return m, first


def _route_body(lg_ref, bias_ref, idx_ref, rank_ref, w_ref, cnt_ref, carry_ref, *,
                n_groups, topk_groups, top_k, scale):
    i = pl.program_id(0)

    @pl.when(i == 0)
    def _():
        carry_ref[...] = jnp.zeros_like(carry_ref)

    lg = lg_ref[...]
    n_exp, tl = lg.shape
    per = n_exp // n_groups
    scores = 1.0 / (1.0 + jnp.exp(-lg))
    biased = scores + bias_ref[...]
    sub = lax.broadcasted_iota(jnp.int32, (per, tl), 0)
    g_rows = []
    for g in range(n_groups):
        xg = biased[g * per:(g + 1) * per]
        m1, i1 = _first_max(xg, sub, per)
        m2 = jnp.max(jnp.where(sub == i1, NEG_INF, xg), axis=0, keepdims=True)
        g_rows.append(m1 + m2)
    gscore = jnp.concatenate(g_rows, axis=0)
    gid = lax.broadcasted_iota(jnp.int32, (n_groups, tl), 0)
    gsel = jnp.zeros((n_groups, tl), jnp.bool_)
    for _ in range(topk_groups):
        _, first = _first_max(gscore, gid, n_groups)
        hit = gid == first
        gsel = gsel | hit
        gscore = jnp.where(hit, NEG_INF, gscore)
    eid = lax.broadcasted_iota(jnp.int32, (n_exp, tl), 0)
    gmask = jnp.concatenate([jnp.broadcast_to(gsel[g:g + 1], (per, tl)) for g in range(n_groups)], axis=0)
    masked = jnp.where(gmask, biased, NEG_INF)
    sel = jnp.zeros((n_exp, tl), jnp.bool_)
    hits, firsts = [], []
    for _ in range(top_k):
        _, first = _first_max(masked, eid, n_exp)
        hit = eid == first
        hits.append(hit)
        firsts.append(first)
        sel = sel | hit
        masked = jnp.where(hit, NEG_INF, masked)
    self = jnp.where(sel, 1.0, 0.0)
    ti = lax.broadcasted_iota(jnp.int32, (tl, tl), 0)
    tj = lax.broadcasted_iota(jnp.int32, (tl, tl), 1)
    before = jnp.where(ti < tj, 1.0, 0.0).astype(BF16)
    carry = carry_ref[...]
    rank_full = jnp.dot(self.astype(BF16), before, preferred_element_type=F32) + carry
    w_rows = [jnp.sum(jnp.where(hit, scores, 0.0), axis=0, keepdims=True) for hit in hits]
    r_rows = [jnp.sum(jnp.where(hit, rank_full, 0.0), axis=0, keepdims=True) for hit in hits]
    w8 = jnp.concatenate(w_rows, axis=0)
    idx_ref[...] = jnp.concatenate(firsts, axis=0)
    rank_ref[...] = jnp.concatenate(r_rows, axis=0).astype(jnp.int32)
    w_ref[...] = w8 / jnp.sum(w8, axis=0, keepdims=True) * scale
    carry = carry + jnp.sum(self, axis=1, keepdims=True)
    carry_ref[...] = carry
    cnt_ref[...] = carry.astype(jnp.int32)


def route(logits_t, router_b, tl=256):
    n_exp, t = logits_t.shape
    body = functools.partial(_route_body, n_groups=N_GROUPS, topk_groups=TOPK_GROUPS, top_k=TOP_K,
                             scale=ROUTED_SCALE)
    tok_spec = pl.BlockSpec((TOP_K, tl), lambda i: (0, i))
    return pl.pallas_call(
        body, grid=(t // tl,),
        in_specs=[pl.BlockSpec((n_exp, tl), lambda i: (0, i)), pl.BlockSpec((n_exp, 1), lambda i: (0, 0))],
        out_specs=[tok_spec, tok_spec, tok_spec, pl.BlockSpec((n_exp, 1), lambda i: (0, 0))],
        out_shape=[jax.ShapeDtypeStruct((TOP_K, t), jnp.int32), jax.ShapeDtypeStruct((TOP_K, t), jnp.int32),
                   jax.ShapeDtypeStruct((TOP_K, t), F32), jax.ShapeDtypeStruct((n_exp, 1), jnp.int32)],
        scratch_shapes=[pltpu.VMEM((n_exp, 1), F32)],
        compiler_params=_params("arbitrary"), name="route",
    )(logits_t, router_b.reshape(n_exp, 1))


def _slab_copy(src_ref, src_tok, dst_ref, dst_tok, rows, sem):
    return pltpu.make_async_copy(src_ref.at[pl.ds(pl.multiple_of(src_tok * rows, rows), rows)],
                                 dst_ref.at[pl.ds(pl.multiple_of(dst_tok * rows, rows), rows)], sem)


def _dispatch_body(dest_ref, pad_ref, h_ref, xs_ref, zero_ref, sem, *, top_k, rows):
    tl = h_ref.shape[0] // rows
    n_exp = pad_ref.shape[1]

    def issue(r, carry):
        for k in range(top_k):
            _slab_copy(h_ref, r, xs_ref, dest_ref[r * top_k + k], rows, sem.at[0]).start(priority=k % 2)
        return carry

    lax.fori_loop(0, tl, issue, 0, unroll=DMA_LOOP_UNROLL)

    @pl.when(pl.program_id(0) == 0)
    def _():
        zero_ref[...] = jnp.zeros_like(zero_ref)
        largest = zero_ref.shape[0] // rows
        pieces = [largest >> b for b in range(largest.bit_length())]

        def pad_copies(e, wait):
            first, n_pad = pad_ref[0, e], pad_ref[1, e]
            for size in pieces:
                @pl.when((n_pad & size) != 0)
                def _():
                    slot0 = first + (n_pad & ~(2 * size - 1))
                    cp = pltpu.make_async_copy(
                        zero_ref.at[pl.ds(0, size * rows)],
                        xs_ref.at[pl.ds(pl.multiple_of(slot0 * rows, rows), size * rows)], sem.at[1])
                    if wait:
                        cp.wait()
                    else:
                        cp.start()

        def fill(e, carry):
            pad_copies(e, False)
            return carry

        def drain_pad(e, carry):
            pad_copies(e, True)
            return carry

        lax.fori_loop(0, n_exp, fill, 0)
        lax.fori_loop(0, n_exp, drain_pad, 0)

    def drain(r, carry):
        for k in range(top_k):
            _slab_copy(h_ref, 0, xs_ref, 0, rows, sem.at[0]).wait()
        return carry

    lax.fori_loop(0, tl, drain, 0, unroll=DMA_LOOP_UNROLL)


def dispatch(h_slab, dest_flat, pad_info, n_slot, top_k, rows, tl=256):
    t = h_slab.shape[0] // rows
    return pl.pallas_call(
        functools.partial(_dispatch_body, top_k=top_k, rows=rows), grid=(t // tl,),
        in_specs=[pl.BlockSpec((tl * top_k,), lambda i: (i,), memory_space=pltpu.SMEM),
                  pl.BlockSpec(memory_space=pltpu.SMEM),
                  pl.BlockSpec((tl * rows, LANES), lambda i: (i, 0))],
        out_specs=pl.BlockSpec(memory_space=pl.ANY),
        out_shape=jax.ShapeDtypeStruct((n_slot * rows, LANES), h_slab.dtype),
        scratch_shapes=[pltpu.VMEM((EXPERT_SLOT_BLOCK // 2 * rows, LANES), h_slab.dtype),
                        pltpu.SemaphoreType.DMA((2,))],
        compiler_params=_params("arbitrary"), name="dispatch",
    )(dest_flat, pad_info, h_slab)


def _combine_body(dest_ref, dest_next_ref, w_ref, sh_ref, x_ref, gate_ref, y_ref, o_ref, buf_ref, acc_ref, sem, *,
                  top_k, rows):
    i = pl.program_id(0)
    n = pl.num_programs(0)
    tl = x_ref.shape[0]

    def gather(d_ref, slot):
        def issue(r, carry):
            for k in range(top_k):
                _slab_copy(y_ref, d_ref[r * top_k + k], buf_ref.at[slot], k * tl + r, rows,
                           sem.at[slot]).start(priority=k % 2)
            return carry
        lax.fori_loop(0, tl, issue, 0, unroll=DMA_LOOP_UNROLL)

    @pl.when(i == 0)
    def _():
        gather(dest_ref, 0)

    @pl.when(i + 1 < n)
    def _():
        gather(dest_next_ref, (i + 1) % 2)

    slot = i % 2

    def drain(r, carry):
        for k in range(top_k):
            _slab_copy(y_ref, 0, buf_ref.at[slot], 0, rows, sem.at[slot]).wait()
        return carry

    lax.fori_loop(0, tl, drain, 0, unroll=DMA_LOOP_UNROLL)
    def unpack_f32(pk):
        return (lax.bitcast_convert_type(pk & jnp.uint32(0xFFFF0000), F32),
                lax.bitcast_convert_type(pk << jnp.uint32(16), F32))

    w = w_ref[...]
    acc_hi, acc_lo = unpack_f32(sh_ref[...])
    for k in range(top_k):
        wk = jnp.broadcast_to(w[:, k:k + 1], (tl, LANES))
        wk = jnp.broadcast_to(wk[:, None, :], (tl, rows, LANES)).reshape(tl * rows, LANES)
        y_hi, y_lo = unpack_f32(buf_ref[slot, pl.ds(k * tl * rows, tl * rows), :])
        acc_hi = acc_hi + y_hi * wk
        acc_lo = acc_lo + y_lo * wk
    acc_ref[0] = acc_hi
    acc_ref[1] = acc_lo
    for half in range(2):
        for c in range(rows):
            cols = slice((half * rows + c) * LANES, (half * rows + c + 1) * LANES)
            o_ref[:, cols] = (x_ref[:, cols]
                              + gate_ref[0][:, cols] * acc_ref.at[half][pl.ds(c, tl, stride=rows), :])


def combine(y_slab, dest_flat, w, shared_slab, x, mod3, gate_idx, geom, tl=COMBINE_TILE):
    t, d = x.shape
    rows = d // 2 // LANES
    top_k = w.shape[1]
    n_lat, lat_len, n_batch = geom
    n_tiles = t // tl
    row = functools.partial(_mod_row, tile_rows=tl, n_lat=n_lat, lat_len=lat_len, n_batch=n_batch)
    return pl.pallas_call(
        functools.partial(_combine_body, top_k=top_k, rows=rows), grid=(n_tiles,),
        in_specs=[pl.BlockSpec((tl * top_k,), lambda i: (i,), memory_space=pltpu.SMEM),
                  pl.BlockSpec((tl * top_k,), lambda i: (jnp.minimum(i + 1, n_tiles - 1),),
                               memory_space=pltpu.SMEM),
                  pl.BlockSpec((tl, top_k), lambda i: (i, 0)),
                  pl.BlockSpec((tl * rows, LANES), lambda i: (i, 0)),
                  pl.BlockSpec((tl, d), lambda i: (i, 0)),
                  pl.BlockSpec((1, 1, d), lambda i: (row(i) * 6 + gate_idx, 0, 0)),
                  pl.BlockSpec(memory_space=pl.ANY)],
        out_specs=pl.BlockSpec((tl, d), lambda i: (i, 0)),
        out_shape=jax.ShapeDtypeStruct((t, d), F32),
        scratch_shapes=[pltpu.VMEM((2, top_k * tl * rows, LANES), jnp.uint32),
                        pltpu.VMEM((2, tl * rows, LANES), F32), pltpu.SemaphoreType.DMA((2,))],
        compiler_params=_params("arbitrary"), name="combine",
    )(dest_flat, dest_flat, w, shared_slab, x, mod3, y_slab)


def moe_ffn(x, h_pk, logits_t, router_b, exp_gate_up, exp_down, shared_gate_up, shared_down, mod3, geom):
    t = x.shape[0]
    slab_rows = h_pk.shape[0] // t
    e_count = exp_gate_up.shape[0]
    blk = EXPERT_SLOT_BLOCK
    idx8, rank8, w8, counts = route(logits_t, router_b)
    counts = counts[:, 0]
    padded = (counts + blk - 1) // blk * blk
    pad_end = jnp.cumsum(padded)
    pad_start = pad_end - padded
    start8 = jnp.sum(jnp.where(idx8[None] == jnp.arange(e_count, dtype=jnp.int32)[:, None, None],
                               pad_start[:, None, None], 0), axis=0)
    dest_flat = (start8 + rank8).T.reshape(-1)
    n_blk = (t * TOP_K + e_count * (blk - 1)) // blk + 1
    n_blk = -(-n_blk // EXPERT_BLOCKS_PER_STEP) * EXPERT_BLOCKS_PER_STEP
    n_slot = n_blk * blk
    blk_starts = jnp.arange(n_blk, dtype=jnp.int32) * blk
    blk_e = jnp.sum((pad_end[None, :] <= blk_starts[:, None]).astype(jnp.int32), axis=1)
    blk_e = jnp.minimum(blk_e, e_count - 1)
    n_used = (pad_end[-1] // blk).astype(jnp.int32).reshape(1)
    pad_info = jnp.stack([pad_start + counts, padded - counts]).astype(jnp.int32)
    x_sorted = dispatch(h_pk, dest_flat, pad_info, n_slot, TOP_K, slab_rows)
    y_slot = expert_blocks(x_sorted, exp_gate_up, exp_down, blk_e, n_used, blk)
    shared = expert_blocks(h_pk, shared_gate_up[None], shared_down[None], jnp.zeros((t // blk,), jnp.int32),
                           jnp.full((1,), t // blk, jnp.int32), blk)
    return combine(y_slot, dest_flat, w8.T, shared, x, mod3, 5, geom)


def kernel(x, c, ctx, c_ctx, hg_lb_logits, l0_mod_w, l0_mod_b, l0_norm1, l0_norm2, l0_w_in, l0_dn_conv, l0_dn_a_log, l0_dn_dt_bias, l0_dn_norm, l0_hg_norm, l0_w_out, l0_router_w, l0_router_b, l0_exp_gate_up, l0_exp_down, l0_shared_gate_up, l0_shared_down, l1_mod_w, l1_mod_b, l1_norm1, l1_norm2, l1_w_in, l1_q_norm, l1_k_norm, l1_lambda, l1_sub_norm, l1_w_out, l1_router_w, l1_router_b, l1_exp_gate_up, l1_exp_down, l1_shared_gate_up, l1_shared_down):
    n_batch, lat_len, d = x.shape
    ctx_len = ctx.shape[1]
    n_lat = n_batch * lat_len
    n_ctx = n_batch * ctx_len
    geom = (n_lat, lat_len, n_batch)
    geom_lat_only = (n_lat, lat_len, n_batch)

    xs = jnp.concatenate([x.reshape(n_lat, d), ctx.reshape(n_ctx, d)], axis=0)
    cond = jnp.concatenate([c, c_ctx[None], jnp.zeros((MOD_ROWS - n_batch - 1, d), F32)], axis=0)

    mod3 = modulation(cond, l0_mod_w, l0_mod_b).reshape(MOD_ROWS * 6, 1, d)
    h = adaln(xs, l0_norm1, mod3, 0, geom)
    n_dn = IN0_SIZES[0] + IN0_SIZES[1]
    n_gate = IN0_SIZES[2] + IN0_SIZES[3]
    proj_dn = matmul(h, l0_w_in, n_cols=n_dn)
    proj_hg = matmul(h, l0_w_in[:, n_dn + n_gate:])
    gc, gr = gate_prep(h, l0_w_in[:, n_dn:n_dn + n_gate], l0_dn_a_log, l0_dn_dt_bias, DN_HEADS)
    conv_t = jnp.concatenate([l0_dn_conv.T, jnp.zeros((8 - l0_dn_conv.shape[1], 3 * DN_WIDTH), F32)], axis=0)
    qkv = dn_prep(proj_dn, conv_t, geom, ctx_len, DN_WIDTH, DN_HEAD_DIM)
    dn_f, dn_b = delta_scan(qkv, gc, gr, geom, ctx_len, DN_HEADS, DN_HEAD_DIM, inv_passes=DELTA_INV_PASSES)
    lb = jnp.cumsum(jax.nn.softmax(hg_lb_logits, axis=0), axis=0)[0:1]
    hg_f, hg_b = hgrn_scan(proj_hg, lb, geom, ctx_len, HG_HEADS, HG_KEY_DIM)
    y = mix_out(dn_f, dn_b, hg_f, hg_b, proj_dn, 3, proj_hg, 4, l0_dn_norm, l0_hg_norm, DN_HEAD_DIM)
    xs = matmul_residual(y, l0_w_out, xs, mod3, 2, geom)
    h, logits_t = adaln(xs, l0_norm2, mod3, 3, geom, router_w=l0_router_w)
    xs = moe_ffn(xs, h, logits_t, l0_router_b, l0_exp_gate_up, l0_exp_down, l0_shared_gate_up, l0_shared_down,
                 mod3, geom)

    mod3 = modulation(cond, l1_mod_w, l1_mod_b).reshape(MOD_ROWS * 6, 1, d)
    h = adaln(xs, l1_norm1, mod3, 0, geom)
    cos_t, sin_t = rope_tables(lat_len, n_batch, n_ctx, DA_HEAD_DIM)
    qkv = matmul_qkv(h, l1_w_in, l1_q_norm, l1_k_norm, cos_t, sin_t)
    lam_init = 0.8 - 0.6 * math.exp(-0.3 * 1)
    lmbda = (jnp.exp(jnp.sum(l1_lambda[0] * l1_lambda[1])) - jnp.exp(jnp.sum(l1_lambda[2] * l1_lambda[3]))
             + lam_init)
    y = diff_attention(qkv, lmbda, l1_sub_norm, n_batch, lat_len, ctx_len, DA_HEADS, DA_HEAD_DIM,
                       1.0 - lam_init)
    xl = matmul_residual(y, l1_w_out, xs, mod3, 2, geom_lat_only)
    h, logits_t = adaln(xl, l1_norm2, mod3, 3, geom_lat_only, router_w=l1_router_w)
    xl = moe_ffn(xl, h, logits_t, l1_router_b, l1_exp_gate_up, l1_exp_down, l1_shared_gate_up, l1_shared_down,
                 mod3, geom_lat_only)
    return xl.reshape(n_batch, lat_len, d)
```

```python
import functools
import math

import jax
import jax.numpy as jnp
from jax import lax
from jax.experimental import pallas as pl
from jax.experimental.pallas import tpu as pltpu

F32 = jnp.float32
BF16 = jnp.bfloat16

NORM_EPS = 1e-6
GRID_W = 64
ROPE_BASE = 10000.0

DN_HEADS = 8
DN_HEAD_DIM = 128
DN_WIDTH = DN_HEADS * DN_HEAD_DIM
HG_HEADS = 8
HG_KEY_DIM = 128
HG_K_WIDTH = HG_HEADS * HG_KEY_DIM
HG_V_WIDTH = HG_K_WIDTH
IN0_SIZES = (3 * DN_WIDTH, DN_WIDTH, 2 * DN_HEADS, 2 * DN_HEADS,
             HG_K_WIDTH, 2 * HG_K_WIDTH, HG_V_WIDTH, HG_V_WIDTH)
DA_HEADS = 8
DA_HEAD_DIM = 128
N_EXPERTS = 64
TOP_K = 8
N_GROUPS = 8
TOPK_GROUPS = 4
ROUTED_SCALE = 2.5

VMEM_LIMIT_BYTES = 56 * 1024 * 1024
LANES = 128
MOD_ROWS = 8
ROW_TILE = 512
MM_TILE_M = 1024
MM_TILE_N = 1024
MM_EPILOGUE_CHUNKS = 4
ATTN_TILE_Q = 1024
ATTN_ROW_CHUNK = 128
EXPERT_SLOT_BLOCK = 256
EXPERT_BLOCKS_PER_STEP = 4
DMA_LOOP_UNROLL = 8
COMBINE_TILE = 128


def _params(*sem):
    return pltpu.CompilerParams(dimension_semantics=sem, vmem_limit_bytes=VMEM_LIMIT_BYTES)


def _sigmoid(x):
    return 0.5 * jnp.tanh(0.5 * x) + 0.5


def _silu(x):
    return x * _sigmoid(x)


def _mod_body(c_ref, w_ref, b_ref, o_ref):
    a = _silu(c_ref[...]).astype(BF16)
    o_ref[...] = jnp.dot(a, w_ref[...].astype(BF16), preferred_element_type=F32) + b_ref[...]


def modulation(cond, w, b, tn=1024):
    m, k = cond.shape
    n = w.shape[1]
    return pl.pallas_call(
        _mod_body, grid=(n // tn,),
        in_specs=[pl.BlockSpec((m, k), lambda j: (0, 0)),
                  pl.BlockSpec((k, tn), lambda j: (0, j)),
                  pl.BlockSpec((1, tn), lambda j: (0, j))],
        out_specs=pl.BlockSpec((m, tn), lambda j: (0, j)),
        out_shape=jax.ShapeDtypeStruct((m, n), F32),
        compiler_params=_params("arbitrary"), name="modulation",
    )(cond, w, b.reshape(1, n))


def _mod_row(tile, tile_rows, n_lat, lat_len, n_batch):
    assert lat_len % tile_rows == 0 and n_lat % tile_rows == 0, (tile_rows, lat_len, n_lat)
    start = tile * tile_rows
    return jnp.where(start < n_lat, start // lat_len, n_batch)


def _pack_bf16_pairs(h):
    half = h.shape[1] // 2
    bits = lax.bitcast_convert_type(h.astype(BF16).astype(F32), jnp.uint32)
    return (bits[:, :half] & jnp.uint32(0xFFFF0000)) | (bits[:, half:] >> jnp.uint32(16))


def _unpack_bf16_pairs(pk):
    hi = lax.bitcast_convert_type(pk & jnp.uint32(0xFFFF0000), F32).astype(BF16)
    lo = lax.bitcast_convert_type(pk << jnp.uint32(16), F32).astype(BF16)
    return hi, lo


def _store_slabs(ref, x, row0=0):
    r, w = x.shape
    c_n = w // LANES
    for c in range(c_n):
        ref[pl.ds(row0 * c_n + c, r, stride=c_n), :] = x[:, c * LANES:(c + 1) * LANES]


def _adaln_body(x_ref, nw_ref, shift_ref, scale_ref, o_ref):
    x = x_ref[...]
    y = x * lax.rsqrt(jnp.mean(x * x, axis=-1, keepdims=True) + NORM_EPS) * nw_ref[...]
    o_ref[...] = (y * (1.0 + scale_ref[0]) + shift_ref[0]).astype(o_ref.dtype)


def _adaln_router_body(x_ref, nw_ref, shift_ref, scale_ref, rw_ref, o_ref, lg_ref):
    x = x_ref[...]
    y = x * lax.rsqrt(jnp.mean(x * x, axis=-1, keepdims=True) + NORM_EPS) * nw_ref[...]
    h = y * (1.0 + scale_ref[0]) + shift_ref[0]
    _store_slabs(o_ref, _pack_bf16_pairs(h))
    lg_ref[...] = lax.dot_general(rw_ref[...], h, (((1,), (1,)), ((), ())), preferred_element_type=F32,
                                  precision=lax.Precision.HIGHEST)


def adaln(x, norm_w, mod3, shift_idx, geom, router_w=None, tl=ROW_TILE):
    t, d = x.shape
    n_lat, lat_len, n_batch = geom
    row = functools.partial(_mod_row, tile_rows=tl, n_lat=n_lat, lat_len=lat_len, n_batch=n_batch)
    in_specs = [pl.BlockSpec((tl, d), lambda i: (i, 0)),
                pl.BlockSpec((1, d), lambda i: (0, 0)),
                pl.BlockSpec((1, 1, d), lambda i: (row(i) * 6 + shift_idx, 0, 0)),
                pl.BlockSpec((1, 1, d), lambda i: (row(i) * 6 + shift_idx + 1, 0, 0))]
    args = [x, norm_w.reshape(1, d), mod3, mod3]
    if router_w is None:
        return pl.pallas_call(
            _adaln_body, grid=(t // tl,), in_specs=in_specs,
            out_specs=pl.BlockSpec((tl, d), lambda i: (i, 0)),
            out_shape=jax.ShapeDtypeStruct((t, d), BF16),
            compiler_params=_params("arbitrary"), name="adaln")(*args)
    e = router_w.shape[1]
    return pl.pallas_call(
        _adaln_router_body, grid=(t // tl,),
        in_specs=in_specs + [pl.BlockSpec((e, d), lambda i: (0, 0))],
        out_specs=[pl.BlockSpec((tl * (d // 2 // LANES), LANES), lambda i: (i, 0)),
                   pl.BlockSpec((e, tl), lambda i: (0, i))],
        out_shape=[jax.ShapeDtypeStruct((t * (d // 2 // LANES), LANES), jnp.uint32),
                   jax.ShapeDtypeStruct((e, t), F32)],
        compiler_params=_params("arbitrary"), name="adaln_router")(*args, router_w.T)


def _mm_body(a_ref, w_ref, o_ref, wb_ref):
    @pl.when(pl.program_id(1) == 0)
    def _():
        wb_ref[...] = w_ref[...].astype(BF16)
    o_ref[...] = jnp.dot(a_ref[...], wb_ref[...], preferred_element_type=F32).astype(o_ref.dtype)


def _mm_res_body(a_ref, w_ref, res_ref, gate_ref, o_ref, wb_ref):
    @pl.when(pl.program_id(1) == 0)
    def _():
        wb_ref[...] = w_ref[...].astype(BF16)
    acc = jnp.dot(a_ref[...], wb_ref[...], preferred_element_type=F32)
    o_ref[...] = res_ref[...] + gate_ref[0] * acc


def matmul(a, w, n_cols=None, col_block0=0, tm=MM_TILE_M, tn=MM_TILE_N, out_dtype=F32):
    m, k = a.shape
    n = w.shape[1] if n_cols is None else n_cols
    return pl.pallas_call(
        _mm_body, grid=(n // tn, m // tm),
        in_specs=[pl.BlockSpec((tm, k), lambda j, i: (i, 0)),
                  pl.BlockSpec((k, tn), lambda j, i: (0, j + col_block0))],
        out_specs=pl.BlockSpec((tm, tn), lambda j, i: (i, j)),
        out_shape=jax.ShapeDtypeStruct((m, n), out_dtype),
        scratch_shapes=[pltpu.VMEM((k, tn), BF16)],
        compiler_params=_params("arbitrary", "arbitrary"), name="matmul",
    )(a, w)


def matmul_residual(a, w, res, mod3, gate_idx, geom, tm=MM_TILE_M, tn=MM_TILE_N):
    m, k = a.shape
    n = w.shape[1]
    n_lat, lat_len, n_batch = geom
    row = functools.partial(_mod_row, tile_rows=tm, n_lat=n_lat, lat_len=lat_len, n_batch=n_batch)
    nb = n // tn
    return pl.pallas_call(
        _mm_res_body, grid=(n // tn, m // tm),
        in_specs=[pl.BlockSpec((tm, k), lambda j, i: (i, 0)),
                  pl.BlockSpec((k, tn), lambda j, i: (0, j)),
                  pl.BlockSpec((tm, tn), lambda j, i: (i, j)),
                  pl.BlockSpec((1, 1, tn), lambda j, i: (row(i) * 6 + gate_idx, 0, j))],
        out_specs=pl.BlockSpec((tm, tn), lambda j, i: (i, j)),
        out_shape=jax.ShapeDtypeStruct((m, n), F32),
        scratch_shapes=[pltpu.VMEM((k, tn), BF16)],
        compiler_params=_params("arbitrary", "arbitrary"), name="matmul_residual",
    )(a, w, res, mod3)


def _expert_body(blk_e_ref, blk_new_ref, next_e_ref, stage_ref, n_used_ref, x_ref, gu_hbm, dn_hbm, o_ref,
                 gu_stage, dn_stage, gub_ref, dnb_ref, sem):
    ff, d = dnb_ref.shape
    half = d // 2
    xc = half // LANES
    blk = x_ref.shape[0] // xc // EXPERT_BLOCKS_PER_STEP

    def weight_copies(e, slot):
        return (pltpu.make_async_copy(gu_hbm.at[e], gu_stage.at[slot], sem.at[0, slot]),
                pltpu.make_async_copy(dn_hbm.at[e], dn_stage.at[slot], sem.at[1, slot]))

    @pl.when(pl.program_id(0) == 0)
    def _():
        for cp in weight_copies(blk_e_ref[0], 0):
            cp.start()

    def one_block(b, sub):
        @pl.when(blk_new_ref[b] == 1)
        def _():
            slot = stage_ref[b]
            for cp in weight_copies(blk_e_ref[b], slot):
                cp.wait()
            gub_ref[...] = gu_stage[slot].astype(BF16)
            dnb_ref[...] = dn_stage[slot].astype(BF16)

            @pl.when(next_e_ref[b] >= 0)
            def _():
                for cp in weight_copies(next_e_ref[b], 1 - slot):
                    cp.start()

        @pl.when(b < n_used_ref[0])
        def _():
            h1 = None
            for p in range(2):
                cs = range(p * xc // 2, (p + 1) * xc // 2)
                pk = jnp.concatenate([x_ref[pl.ds(sub * blk * xc + c, blk, stride=xc), :] for c in cs], axis=1)
                x_hi, x_lo = _unpack_bf16_pairs(pk)
                k0 = p * half // 2
                part = (jnp.dot(x_hi, gub_ref[k0:k0 + half // 2], preferred_element_type=F32)
                        + jnp.dot(x_lo, gub_ref[half + k0:half + k0 + half // 2], preferred_element_type=F32))
                h1 = part if h1 is None else h1 + part
            act = (_silu(h1[:, :ff]) * h1[:, ff:]).astype(BF16)
            y = jnp.concatenate([jnp.dot(act, dnb_ref[:, p * half:(p + 1) * half], preferred_element_type=F32)
                                 for p in range(2)], axis=1)
            _store_slabs(o_ref, _pack_bf16_pairs(y), sub * blk)

        @pl.when(b >= n_used_ref[0])
        def _():
            o_ref[pl.ds(sub * blk * xc, blk * xc), :] = jnp.zeros((blk * xc, LANES), o_ref.dtype)

    for sub in range(EXPERT_BLOCKS_PER_STEP):
        one_block(pl.program_id(0) * EXPERT_BLOCKS_PER_STEP + sub, sub)


def expert_blocks(x, gate_up, down, blk_e, n_used, blk):
    _, d, f2 = gate_up.shape
    xc = yc = d // 2 // LANES
    s = x.shape[0] // xc
    n_blk = s // blk
    assert n_blk % EXPERT_BLOCKS_PER_STEP == 0, (n_blk, EXPERT_BLOCKS_PER_STEP)
    pos = jnp.arange(n_blk, dtype=jnp.int32)
    used = pos < n_used[0]
    blk_new = (jnp.concatenate([jnp.ones((1,), bool), blk_e[1:] != blk_e[:-1]]) & used).astype(jnp.int32)
    stage = (jnp.cumsum(blk_new) - 1) % 2
    first_pos = jnp.where(blk_new == 1, pos, n_blk)
    next_first = jnp.concatenate([lax.cummin(first_pos, reverse=True)[1:], jnp.full((1,), n_blk, jnp.int32)])
    next_e = jnp.where(next_first < n_blk, blk_e[jnp.minimum(next_first, n_blk - 1)], -1).astype(jnp.int32)
    bps = EXPERT_BLOCKS_PER_STEP
    grid_spec = pltpu.PrefetchScalarGridSpec(
        num_scalar_prefetch=5, grid=(n_blk // bps,),
        in_specs=[pl.BlockSpec((bps * blk * xc, LANES),
                               lambda i, be, bn, ne, st, nu: (jnp.minimum(i, (nu[0] - 1) // bps), 0)),
                  pl.BlockSpec(memory_space=pl.ANY),
                  pl.BlockSpec(memory_space=pl.ANY)],
        out_specs=pl.BlockSpec((bps * blk * yc, LANES), lambda i, be, bn, ne, st, nu: (i, 0)),
        scratch_shapes=[pltpu.VMEM((2, d, f2), F32), pltpu.VMEM((2, f2 // 2, d), F32),
                        pltpu.VMEM((d, f2), BF16), pltpu.VMEM((f2 // 2, d), BF16),
                        pltpu.SemaphoreType.DMA((2, 2))])
    return pl.pallas_call(
        _expert_body, grid_spec=grid_spec,
        out_shape=jax.ShapeDtypeStruct((s * yc, LANES), jnp.uint32),
        compiler_params=_params("arbitrary"), name="expert_blocks",
    )(blk_e, blk_new, next_e, stage.astype(jnp.int32), n_used, x, gate_up, down)


def _norm_rope(x, w, cos, sin, first, head_dim):
    y = x * lax.rsqrt(jnp.mean(x * x, axis=-1, keepdims=True) + NORM_EPS) * w
    swapped = jnp.where(first, pltpu.roll(y, head_dim - head_dim // 4, 1), pltpu.roll(y, head_dim // 4, 1))
    return y * cos + swapped * sin


def _mm_qkv_body(a_ref, w_ref, qw_ref, kw_ref, cos_ref, sin_ref, o_ref, wb_ref, *, head_dim, n_q_tiles, n_k_tiles):
    j = pl.program_id(0)

    @pl.when(pl.program_id(1) == 0)
    def _():
        wb_ref[...] = w_ref[...].astype(BF16)

    tm, tn = o_ref.shape
    row_chunk = tm // MM_EPILOGUE_CHUNKS

    def rows_chain(rows, nw_ref):
        acc = jnp.dot(a_ref[rows, :], wb_ref[...], preferred_element_type=F32)
        yield
        if nw_ref is None:
            o_ref[rows, :] = acc.astype(o_ref.dtype)
            return
        cos, sin = cos_ref[rows, :], sin_ref[rows, :]
        lane = lax.broadcasted_iota(jnp.int32, cos.shape, 1)
        first = (lane % (head_dim // 2)) < (head_dim // 4)
        for g in range(tn // head_dim):
            sl = slice(g * head_dim, (g + 1) * head_dim)
            o_ref[rows, sl] = _norm_rope(acc[:, sl], nw_ref[...], cos, sin, first, head_dim).astype(o_ref.dtype)
            yield

    def tile(nw_ref):
        _run_interleaved(rows_chain(slice(r, r + row_chunk), nw_ref) for r in range(0, tm, row_chunk))

    @pl.when(j < n_q_tiles)
    def _():
        tile(qw_ref)

    @pl.when((j >= n_q_tiles) & (j < n_q_tiles + n_k_tiles))
    def _():
        tile(kw_ref)

    @pl.when(j >= n_q_tiles + n_k_tiles)
    def _():
        tile(None)


def matmul_qkv(a, w, q_norm, k_norm, cos_t, sin_t, tm=MM_TILE_M, tn=MM_TILE_N):
    m, k = a.shape
    n = w.shape[1]
    hd = q_norm.shape[0]
    body = functools.partial(_mm_qkv_body, head_dim=hd, n_q_tiles=n // 3 // tn, n_k_tiles=n // 3 // tn)
    return pl.pallas_call(
        body, grid=(n // tn, m // tm),
        in_specs=[pl.BlockSpec((tm, k), lambda j, i: (i, 0)),
                  pl.BlockSpec((k, tn), lambda j, i: (0, j)),
                  pl.BlockSpec((1, hd), lambda j, i: (0, 0)),
                  pl.BlockSpec((1, hd), lambda j, i: (0, 0)),
                  pl.BlockSpec((tm, hd), lambda j, i: (i, 0)),
                  pl.BlockSpec((tm, hd), lambda j, i: (i, 0))],
        out_specs=pl.BlockSpec((tm, tn), lambda j, i: (i, j)),
        out_shape=jax.ShapeDtypeStruct((m, n), BF16),
        scratch_shapes=[pltpu.VMEM((k, tn), BF16)],
        compiler_params=_params("arbitrary", "arbitrary"), name="matmul_qkv",
    )(a, w, q_norm.reshape(1, hd), k_norm.reshape(1, hd), cos_t, sin_t)


def rope_tables(n_lat_tokens_per_sample, n_batch, n_ctx_tokens, head_dim):
    quarter = head_dim // 4
    inv_freq = ROPE_BASE ** (-jnp.arange(quarter, dtype=F32) / quarter)
    rows = n_lat_tokens_per_sample // GRID_W
    row = jnp.repeat(jnp.arange(rows, dtype=F32), GRID_W)
    col = jnp.tile(jnp.arange(GRID_W, dtype=F32), rows)
    ang_r = row[:, None] * inv_freq[None, :]
    ang_c = col[:, None] * inv_freq[None, :]
    cos = jnp.concatenate([jnp.cos(ang_r), jnp.cos(ang_r), jnp.cos(ang_c), jnp.cos(ang_c)], axis=-1)
    sin = jnp.concatenate([-jnp.sin(ang_r), jnp.sin(ang_r), -jnp.sin(ang_c), jnp.sin(ang_c)], axis=-1)
    cos = jnp.concatenate([jnp.tile(cos, (n_batch, 1)), jnp.ones((n_ctx_tokens, head_dim), F32)], axis=0)
    sin = jnp.concatenate([jnp.tile(sin, (n_batch, 1)), jnp.zeros((n_ctx_tokens, head_dim), F32)], axis=0)
    return cos, sin


def _diff_attn_rows(lam, q_ref, k_all, v_all, sw_ref, o_ref, rows, head_dim, out_scale):
    c = head_dim ** -0.5 * math.log2(math.e)
    es, invs = [], []
    for s in range(2):
        sl = slice(s * head_dim, (s + 1) * head_dim)
        sc = lax.dot_general(q_ref[rows, sl], k_all[:, sl], (((1,), (1,)), ((), ())), preferred_element_type=F32)
        yield
        e = jnp.exp2((sc - jnp.max(sc, axis=-1, keepdims=True)) * c)
        invs.append(1.0 / jnp.sum(e, axis=-1, keepdims=True))
        es.append(e.astype(BF16))
        yield
    v = v_all[...]
    o0 = jnp.dot(es[0], v, preferred_element_type=F32)
    o1 = jnp.dot(es[1], v, preferred_element_type=F32)
    yield
    o = o0 * invs[0] - (lam * invs[1]) * o1
    y = o * lax.rsqrt(jnp.mean(o * o, axis=-1, keepdims=True) + NORM_EPS) * sw_ref[...]
    o_ref[rows, :] = (y * out_scale).astype(o_ref.dtype)
    yield


def _diff_attn_body(lam_ref, q_ref, kl_ref, kc_ref, vl_ref, vc_ref, sw_ref, o_ref, k_all, v_all, *,
                    head_dim, out_scale, row_chunk):
    @pl.when(pl.program_id(2) == 0)
    def _():
        n_l = kl_ref.shape[0]
        k_all[:n_l] = kl_ref[...]
        k_all[n_l:] = kc_ref[...]
        v_all[:n_l] = vl_ref[...]
        v_all[n_l:] = vc_ref[...]

    lam = lam_ref[0]
    tq = q_ref.shape[0]
    _run_interleaved(
        _diff_attn_rows(lam, q_ref, k_all, v_all, sw_ref, o_ref, slice(r, r + row_chunk), head_dim, out_scale)
        for r in range(0, tq, row_chunk))


def diff_attention(qkv, lmbda, sub_norm, n_batch, lat_len, ctx_len, n_heads, head_dim, out_scale,
                   tq=ATTN_TILE_Q, row_chunk=ATTN_ROW_CHUNK):
    hw = 2 * head_dim
    nq = lat_len // tq
    ctx_blk0 = n_batch * lat_len // ctx_len
    body = functools.partial(_diff_attn_body, head_dim=head_dim, out_scale=out_scale, row_chunk=row_chunk)
    n_keys = lat_len + ctx_len
    return pl.pallas_call(
        body, grid=(n_batch, n_heads, nq),
        in_specs=[pl.BlockSpec(memory_space=pltpu.SMEM),
                  pl.BlockSpec((tq, hw), lambda b, h, i: (b * nq + i, h)),
                  pl.BlockSpec((lat_len, hw), lambda b, h, i: (b, n_heads + h)),
                  pl.BlockSpec((ctx_len, hw), lambda b, h, i: (ctx_blk0 + b, n_heads + h)),
                  pl.BlockSpec((lat_len, hw), lambda b, h, i: (b, 2 * n_heads + h)),
                  pl.BlockSpec((ctx_len, hw), lambda b, h, i: (ctx_blk0 + b, 2 * n_heads + h)),
                  pl.BlockSpec((1, hw), lambda b, h, i: (0, 0))],
        out_specs=pl.BlockSpec((tq, hw), lambda b, h, i: (b * nq + i, h)),
        out_shape=jax.ShapeDtypeStruct((n_batch * lat_len, n_heads * hw), BF16),
        scratch_shapes=[pltpu.VMEM((n_keys, hw), BF16), pltpu.VMEM((n_keys, hw), BF16)],
        compiler_params=_params("arbitrary", "arbitrary", "arbitrary"), name="diff_attention",
    )(lmbda.reshape(1), qkv, qkv, qkv, qkv, qkv, sub_norm.reshape(1, hw))


SCAN_TILE = 256
CHUNK = 64
SUB = 16
NEG_BIG = -1e30
NEG_INF = float("-inf")
DELTA_INV_PASSES = 1
DELTA_HEADS_PER_STEP = 4
HGRN_HEADS_PER_STEP = 4


def _dot(a, b):
    return jnp.dot(a.astype(BF16), b.astype(BF16), preferred_element_type=F32)


def _dot_nt(a, b):
    return lax.dot_general(a.astype(BF16), b.astype(BF16), (((1,), (1,)), ((), ())),
                           preferred_element_type=F32)


def _split3(x):
    hi = x.astype(BF16)
    r = x - hi.astype(F32)
    mid = r.astype(BF16)
    lo = (r - mid.astype(F32)).astype(BF16)
    return hi, mid, lo


def _dot_exact_lhs01(m01, x):
    hi, mid, lo = _split3(x)
    m = m01.astype(BF16)
    return (jnp.dot(m, hi, preferred_element_type=F32) + jnp.dot(m, mid, preferred_element_type=F32)
            + jnp.dot(m, lo, preferred_element_type=F32))


def _dot_exact_rhs01(x, m01):
    hi, mid, lo = _split3(x)
    m = m01.astype(BF16)
    return (jnp.dot(hi, m, preferred_element_type=F32) + jnp.dot(mid, m, preferred_element_type=F32)
            + jnp.dot(lo, m, preferred_element_type=F32))


def _dot3(a, b):
    ah = a.astype(BF16)
    al = (a - ah.astype(F32)).astype(BF16)
    bh = b.astype(BF16)
    bl = (b - bh.astype(F32)).astype(BF16)
    return (jnp.dot(ah, bh, preferred_element_type=F32) + jnp.dot(ah, bl, preferred_element_type=F32)
            + jnp.dot(al, bh, preferred_element_type=F32))


def _tile_masks(n, reverse):
    i = lax.broadcasted_iota(jnp.int32, (n, n), 0)
    j = lax.broadcasted_iota(jnp.int32, (n, n), 1)
    same = (i // CHUNK) == (j // CHUNK)
    if reverse:
        return same & (i <= j), same & (i < j)
    return same & (i >= j), same & (i > j)


def _segment_tile(b, s, reverse, nl, nc, ctx_tile0):
    if reverse:
        return jnp.where(s < nc, ctx_tile0 + b * nc + (nc - 1 - s), b * nl + (nl - 1 - (s - nc)))
    return jnp.where(s < nc, ctx_tile0 + b * nc + s, b * nl + (s - nc))


def _dn_prep_body(x_ref, prev_ref, next_ref, w_ref, o_ref, *, tiles_per_lat_seg, tiles_per_ctx_seg, n_lat_tiles,
                  head_dim, q_scale):
    i = pl.program_id(0)
    j = pl.program_id(1)
    is_lat = i < n_lat_tiles
    pos = jnp.where(is_lat, i % tiles_per_lat_seg, (i - n_lat_tiles) % tiles_per_ctx_seg)
    seg_first = pos == 0
    seg_last = pos == jnp.where(is_lat, tiles_per_lat_seg, tiles_per_ctx_seg) - 1
    x = x_ref[...]
    tl = x.shape[0]
    prev = jnp.where(seg_first, 0.0, prev_ref[...])
    nxt = jnp.where(seg_last, 0.0, next_ref[...])
    xp = jnp.concatenate([prev, x, nxt], axis=0)
    w = w_ref[...]
    n_taps = 5
    acc = None
    for t in range(n_taps):
        off = 8 + t - n_taps // 2
        term = xp[off:off + tl] * w[t:t + 1]
        acc = term if acc is None else acc + term
    y = _silu(acc)
    scale = jnp.where(j == 0, q_scale, 1.0)
    outs = []
    for h in range(y.shape[1] // head_dim):
        yh = y[:, h * head_dim:(h + 1) * head_dim]
        nrm = lax.rsqrt(jnp.sum(yh * yh, axis=-1, keepdims=True) + 1e-6) * scale
        outs.append(yh * jnp.where(j == 2, 1.0, nrm))
    o_ref[...] = jnp.concatenate(outs, axis=1)


def dn_prep(proj_dn, conv_w_t, geom, ctx_len, width, head_dim):
    n_lat, lat_len, n_batch = geom
    t = proj_dn.shape[0]
    tl = SCAN_TILE
    rows8 = tl // 8
    n_tiles = t // tl
    body = functools.partial(_dn_prep_body, tiles_per_lat_seg=lat_len // tl, tiles_per_ctx_seg=ctx_len // tl,
                             n_lat_tiles=n_lat // tl,
                             head_dim=head_dim, q_scale=head_dim ** -0.5)
    last8 = t // 8 - 1
    return pl.pallas_call(
        body, grid=(n_tiles, 3),
        in_specs=[pl.BlockSpec((tl, width), lambda i, j: (i, j)),
                  pl.BlockSpec((8, width), lambda i, j: (jnp.maximum(i * rows8 - 1, 0), j)),
                  pl.BlockSpec((8, width), lambda i, j: (jnp.minimum((i + 1) * rows8, last8), j)),
                  pl.BlockSpec((8, width), lambda i, j: (0, j))],
        out_specs=pl.BlockSpec((tl, width), lambda i, j: (i, j)),
        out_shape=jax.ShapeDtypeStruct((t, 3 * width), F32),
        compiler_params=_params("arbitrary", "arbitrary"), name="dn_prep",
    )(proj_dn, proj_dn, proj_dn, conv_w_t)


def _softplus(x):
    return jnp.maximum(x, 0.0) + jnp.log(1.0 + jnp.exp(-jnp.abs(x)))


def _gate_prep_body(h_ref, wc_ref, wr_ref, alog_c_ref, dtb_c_ref, alog_r_ref, dtb_r_ref, gc_ref, gr_ref, *, n_heads):
    h = h_ref[...]
    tl = h.shape[0]
    nd = 2 * n_heads
    raw_c = jnp.dot(h, wc_ref[...].astype(BF16), preferred_element_type=F32)
    raw_r = lax.dot_general(wr_ref[...].astype(BF16), h, (((1,), (1,)), ((), ())),
                            preferred_element_type=F32)
    g_c = -jnp.exp(alog_c_ref[...]) * _softplus(raw_c[:, :nd] + dtb_c_ref[...])
    g_r = -jnp.exp(alog_r_ref[...]) * _softplus(raw_r[:nd, :] + dtb_r_ref[...])
    incl_f, _ = _tile_masks(tl, False)
    incl_b, _ = _tile_masks(tl, True)
    one_f = jnp.where(incl_f, 1.0, 0.0)
    one_b = jnp.where(incl_b, 1.0, 0.0)
    cum_c = jnp.concatenate([_dot_exact_lhs01(one_f, g_c[:, :n_heads]),
                             _dot_exact_lhs01(one_b, g_c[:, n_heads:])], axis=1)
    cum_r = jnp.concatenate([_dot_exact_rhs01(g_r[:n_heads, :], one_b),
                             _dot_exact_rhs01(g_r[n_heads:, :], one_f)], axis=0)
    beta_c = _sigmoid(raw_c[:, nd:])
    gc_ref[...] = jnp.concatenate([cum_c, beta_c], axis=1)
    gr_ref[...] = jnp.concatenate([cum_r, jnp.zeros_like(cum_r)], axis=0)


def gate_prep(h_bf, w_gate, a_log, dt_bias, n_heads):
    t, d = h_bf.shape
    tl = SCAN_TILE
    nd = 2 * n_heads
    body = functools.partial(_gate_prep_body, n_heads=n_heads)
    full = lambda shape: pl.BlockSpec(shape, lambda i: (0, 0))
    return pl.pallas_call(
        body, grid=(t // tl,),
        in_specs=[pl.BlockSpec((tl, d), lambda i: (i, 0)), full((d, 2 * nd)), full((2 * nd, d)),
                  full((1, nd)), full((1, nd)), full((nd, 1)), full((nd, 1))],
        out_specs=[pl.BlockSpec((tl, 2 * nd), lambda i: (i, 0)), pl.BlockSpec((2 * nd, tl), lambda i: (0, i))],
        out_shape=[jax.ShapeDtypeStruct((t, 2 * nd), F32), jax.ShapeDtypeStruct((2 * nd, t), F32)],
        compiler_params=_params("arbitrary"), name="gate_prep",
    )(h_bf, w_gate, w_gate.T, a_log.reshape(1, nd), dt_bias.reshape(1, nd),
      a_log.reshape(nd, 1), dt_bias.reshape(nd, 1))


def _select_col(x, idx):
    lane = lax.broadcasted_iota(jnp.int32, x.shape, 1)
    return jnp.sum(jnp.where(lane == idx, x, 0.0), axis=1, keepdims=True)


def _select_row(x, idx):
    row = lax.broadcasted_iota(jnp.int32, x.shape, 0)
    return jnp.sum(jnp.where(row == idx, x, 0.0), axis=0, keepdims=True)


def _run_interleaved(chains):
    chains = list(chains)
    while chains:
        alive = []
        for ch in chains:
            try:
                next(ch)
                alive.append(ch)
            except StopIteration:
                pass
        chains = alive


def _delta_chain(q, k, v, gc_col, gc_row, beta_col, s_ref, o_ref, cols, reverse, inv_passes):
    tl, kd_ = k.shape
    n_chunks = tl // CHUNK
    incl, strict = _tile_masks(tl, reverse)
    decay = jnp.exp(jnp.where(incl, gc_col - gc_row, NEG_BIG))
    kb, qb = k.astype(BF16), q.astype(BF16)
    kkt = _dot_nt(kb, kb)
    qkt = _dot_nt(qb, kb)
    yield
    x = jnp.where(strict, kkt * (-beta_col) * decay, 0.0)
    dot_inv = _dot3 if inv_passes == 3 else _dot
    ri = lax.broadcasted_iota(jnp.int32, (tl, tl), 0)
    ci = lax.broadcasted_iota(jnp.int32, (tl, tl), 1)
    r = jnp.where(ri == ci, 1.0, 0.0) + x
    n_sq = int(math.log2(CHUNK)) - 1
    for _ in range(n_sq):
        x = dot_inv(x, x)
        r = r + dot_inv(r, x)
        yield
    e_g = jnp.exp(gc_col)
    rhs = jnp.concatenate([v * beta_col, k * (beta_col * e_g)], axis=1)
    sol = dot_inv(r, rhs)
    yield
    u0, w = sol[:, :v.shape[1]], sol[:, v.shape[1]:]
    attn = (qkt * decay).astype(BF16)
    a_sol = _dot(attn, sol)
    o0 = a_sol[:, :v.shape[1]]
    qe = q * e_g - a_sol[:, v.shape[1]:]
    tot_rows = []
    for c in range(n_chunks):
        last = c * CHUNK if reverse else c * CHUNK + CHUNK - 1
        tot_rows.append(jnp.broadcast_to(gc_col[last:last + 1, :], (CHUNK, 1)))
    tot = jnp.concatenate(tot_rows, axis=0)
    kdec_t = jnp.transpose(k * jnp.exp(tot - gc_col))
    wu = jnp.concatenate([-w, u0], axis=1).astype(BF16)
    lane = lax.broadcasted_iota(jnp.int32, kdec_t.shape, 1)
    pns = [_dot(jnp.where((lane // CHUNK) == c, kdec_t, 0.0), wu) for c in range(n_chunks)]
    yield
    order = range(n_chunks - 1, -1, -1) if reverse else range(n_chunks)
    for c in order:
        rows = slice(c * CHUNK, (c + 1) * CHUNK)
        pn = pns[c]
        lhs = jnp.concatenate([qe[rows], pn[:, :kd_]], axis=0)
        s = s_ref[...]
        res = _dot(lhs, s)
        o_ref[rows, cols] = o0[rows] + res[:CHUNK]
        last = c * CHUNK if reverse else c * CHUNK + CHUNK - 1
        gl = jnp.exp(gc_col[last:last + 1, :])
        s_ref[...] = gl * s + res[CHUNK:] + pn[:, kd_:]
        yield


def _delta_body(qf_ref, kf_ref, vf_ref, gcf_ref, grf_ref, qb_ref, kb_ref, vb_ref, gcb_ref, grb_ref,
                of_ref, ob_ref, sf_ref, sb_ref, *, n_heads, head_dim, inv_passes):
    hg = pl.program_id(1)
    heads_per_step = sf_ref.shape[0]

    @pl.when(pl.program_id(2) == 0)
    def _():
        sf_ref[...] = jnp.zeros_like(sf_ref)
        sb_ref[...] = jnp.zeros_like(sb_ref)

    chains = []
    for g in range(heads_per_step):
        cols = slice(g * head_dim, (g + 1) * head_dim)
        for reverse, (q_ref, k_ref, v_ref, gc_ref, gr_ref, o_ref, s_ref) in enumerate(
                [(qf_ref, kf_ref, vf_ref, gcf_ref, grf_ref, of_ref, sf_ref),
                 (qb_ref, kb_ref, vb_ref, gcb_ref, grb_ref, ob_ref, sb_ref)]):
            idx = reverse * n_heads + hg * heads_per_step + g
            gcs = gc_ref[...]
            gc_col = _select_col(gcs, idx)
            beta_col = _select_col(gcs, 2 * n_heads + idx)
            gc_row = _select_row(gr_ref[...], idx)
            chains.append(_delta_chain(q_ref[:, cols], k_ref[:, cols], v_ref[:, cols], gc_col, gc_row, beta_col,
                                       s_ref.at[g], o_ref, cols, bool(reverse), inv_passes))
    _run_interleaved(chains)


def delta_scan(qkv, gc, gr, geom, ctx_len, n_heads, head_dim, inv_passes=3, heads_per_step=DELTA_HEADS_PER_STEP):
    n_lat, lat_len, n_batch = geom
    t = qkv.shape[0]
    tl = SCAN_TILE
    nl, nc, ctx0 = lat_len // tl, ctx_len // tl, n_lat // tl
    n_hg = n_heads // heads_per_step
    gw = heads_per_step * head_dim
    tile = functools.partial(_segment_tile, nl=nl, nc=nc, ctx_tile0=ctx0)
    specs = []
    for reverse in (False, True):
        tix = functools.partial(tile, reverse=reverse)
        specs += [pl.BlockSpec((tl, gw), lambda b, h, s, tix=tix: (tix(b, s), h)),
                  pl.BlockSpec((tl, gw), lambda b, h, s, tix=tix: (tix(b, s), n_hg + h)),
                  pl.BlockSpec((tl, gw), lambda b, h, s, tix=tix: (tix(b, s), 2 * n_hg + h)),
                  pl.BlockSpec((tl, 4 * n_heads), lambda b, h, s, tix=tix: (tix(b, s), 0)),
                  pl.BlockSpec((4 * n_heads, tl), lambda b, h, s, tix=tix: (0, tix(b, s)))]
    out_specs = [pl.BlockSpec((tl, gw), lambda b, h, s, tix=functools.partial(tile, reverse=r): (tix(b, s), h))
                 for r in (False, True)]
    body = functools.partial(_delta_body, n_heads=n_heads, head_dim=head_dim, inv_passes=inv_passes)
    return pl.pallas_call(
        body, grid=(n_batch, n_hg, nl + nc), in_specs=specs, out_specs=out_specs,
        out_shape=[jax.ShapeDtypeStruct((t, n_heads * head_dim), F32)] * 2,
        scratch_shapes=[pltpu.VMEM((heads_per_step, head_dim, head_dim), F32)] * 2,
        compiler_params=_params("arbitrary", "arbitrary", "arbitrary"), name="delta_scan",
    )(qkv, qkv, qkv, gc, gr, qkv, qkv, qkv, gc, gr)


def _hgrn_diag(q, k, cum, v, stage_ref, diag_ref, reverse):
    n_sub = q.shape[0] // SUB
    for n, val in enumerate((q, k, cum, v)):
        stage_ref[n] = val

    def slabs(n):
        return [stage_ref.at[n][pl.ds(r, n_sub, stride=SUB), :] for r in range(SUB)]

    q_x, k_x, p_x, v_x = slabs(0), slabs(1), slabs(2), slabs(3)
    pairs = [(i, j) for i in range(SUB) for j in range(SUB) if (j >= i if reverse else j <= i)]
    terms = []
    for i, j in pairs:
        qk = q_x[i] * k_x[j]
        terms.append((qk if i == j else qk * jnp.exp(p_x[i] - p_x[j])).astype(BF16))
    kd_ = terms[0].shape[1]
    a_rep = jnp.dot(jnp.concatenate(terms, axis=0), jnp.ones((kd_, kd_), BF16), preferred_element_type=F32)
    o_x = [None] * SUB
    for n, (i, j) in enumerate(pairs):
        contrib = a_rep[n * n_sub:(n + 1) * n_sub] * v_x[j]
        o_x[i] = contrib if o_x[i] is None else o_x[i] + contrib
    for r in range(SUB):
        diag_ref[pl.ds(r, n_sub, stride=SUB), :] = o_x[r]


def _hgrn_chain(hq_ref, hf_ref, hv_ref, cols, lb, st_ref, o_ref, stage_ref, diag_ref, reverse):
    hq, hf, hv = hq_ref[:, cols], hf_ref[:, cols], hv_ref[:, cols]
    tl, kd_ = hq.shape
    n_chunks = tl // CHUNK
    n_sub = tl // SUB
    sub_per_chunk = CHUNK // SUB
    q = _silu(hq)
    f = lb + (1.0 - lb) * _sigmoid(hf)
    k = 1.0 - f
    lf = jnp.log(f)
    incl, _ = _tile_masks(tl, reverse)
    cum = _dot_exact_lhs01(jnp.where(incl, 1.0, 0.0), lf)
    excl = cum - lf

    def bcast_rows(src, row, n):
        return jnp.broadcast_to(src[row:row + 1, :], (n, kd_))

    chunk_last = [(c * CHUNK if reverse else c * CHUNK + CHUNK - 1) for c in range(n_chunks)]
    tot = jnp.concatenate([bcast_rows(cum, chunk_last[c], CHUNK) for c in range(n_chunks)], axis=0)
    sub_first = [(m * SUB + SUB - 1 if reverse else m * SUB) for m in range(n_sub)]
    r_sub = jnp.concatenate([bcast_rows(excl, sub_first[m], SUB) for m in range(n_sub)], axis=0)
    q_t = q * jnp.exp(cum - r_sub)
    qd = q * jnp.exp(cum)
    kd = k * jnp.exp(tot - cum)
    vb = hv.astype(BF16)

    i = lax.broadcasted_iota(jnp.int32, (tl, tl), 0)
    j = lax.broadcasted_iota(jnp.int32, (tl, tl), 1)
    same = (i // CHUNK) == (j // CHUNK)
    pos_i = (i % CHUNK) // SUB
    pos_j = (j % CHUNK) // SUB
    if reverse:
        pos_i, pos_j = sub_per_chunk - 1 - pos_i, sub_per_chunk - 1 - pos_j
    a_off = jnp.zeros((tl, tl), F32)
    for lvl in range(1, sub_per_chunk):
        ref_rows = []
        for c in range(n_chunks):
            m = c * sub_per_chunk + (sub_per_chunk - 1 - lvl if reverse else lvl)
            ref_rows.append(bcast_rows(excl, sub_first[m], CHUNK))
        r_lvl = jnp.concatenate(ref_rows, axis=0)
        k_t = k * jnp.exp(jnp.minimum(r_lvl - cum, 0.0))
        a_l = _dot_nt(q_t, k_t)
        a_off = a_off + jnp.where(same & (pos_i == lvl) & (pos_j < lvl), a_l, 0.0)
        yield
    o_intra = _dot(a_off, vb)
    yield

    _hgrn_diag(q, k, cum, hv, stage_ref, diag_ref, reverse)
    yield
    o_intra = o_intra + diag_ref[...]

    v_t = jnp.transpose(hv)
    lane = lax.broadcasted_iota(jnp.int32, v_t.shape, 1)
    kdb = kd.astype(BF16)
    n_ts = [_dot(jnp.where((lane // CHUNK) == c, v_t, 0.0), kdb) for c in range(n_chunks)]
    yield
    order = range(n_chunks - 1, -1, -1) if reverse else range(n_chunks)
    for c in order:
        rows = slice(c * CHUNK, (c + 1) * CHUNK)
        st = st_ref[...]
        o_ref[rows, cols] = o_intra[rows] + _dot_nt(qd[rows], st)
        st_ref[...] = st * jnp.exp(cum[chunk_last[c]:chunk_last[c] + 1, :]) + n_ts[c]
        yield


def _hgrn_body(qf_ref, ff_ref, vf_ref, qb_ref, fb_ref, vb_ref, lb_ref, of_ref, ob_ref, sf_ref, sb_ref, stage_ref,
               diag_ref, *, key_dim):
    heads_per_step = sf_ref.shape[0]

    @pl.when(pl.program_id(2) == 0)
    def _():
        sf_ref[...] = jnp.zeros_like(sf_ref)
        sb_ref[...] = jnp.zeros_like(sb_ref)

    chains = []
    for g in range(heads_per_step):
        cols = slice(g * key_dim, (g + 1) * key_dim)
        lb = lb_ref[:, cols]
        chains.append(_hgrn_chain(qf_ref, ff_ref, vf_ref, cols, lb, sf_ref.at[g], of_ref,
                                  stage_ref.at[2 * g], diag_ref.at[2 * g], False))
        chains.append(_hgrn_chain(qb_ref, fb_ref, vb_ref, cols, lb, sb_ref.at[g], ob_ref,
                                  stage_ref.at[2 * g + 1], diag_ref.at[2 * g + 1], True))
    _run_interleaved(chains)


def hgrn_scan(proj_hg, lb, geom, ctx_len, n_heads, key_dim, heads_per_step=HGRN_HEADS_PER_STEP):
    n_lat, lat_len, n_batch = geom
    t = proj_hg.shape[0]
    tl = SCAN_TILE
    nl, nc, ctx0 = lat_len // tl, ctx_len // tl, n_lat // tl
    n_hg = n_heads // heads_per_step
    gw = heads_per_step * key_dim
    tile = functools.partial(_segment_tile, nl=nl, nc=nc, ctx_tile0=ctx0)
    specs = []
    for reverse in (False, True):
        tix = functools.partial(tile, reverse=reverse)
        fcol = (1 + int(reverse)) * n_hg
        specs += [pl.BlockSpec((tl, gw), lambda b, h, s, tix=tix: (tix(b, s), h)),
                  pl.BlockSpec((tl, gw), lambda b, h, s, tix=tix, fcol=fcol: (tix(b, s), fcol + h)),
                  pl.BlockSpec((tl, gw), lambda b, h, s, tix=tix: (tix(b, s), 3 * n_hg + h))]
    specs.append(pl.BlockSpec((1, gw), lambda b, h, s: (0, h)))
    out_specs = [pl.BlockSpec((tl, gw), lambda b, h, s, tix=functools.partial(tile, reverse=r): (tix(b, s), h))
                 for r in (False, True)]
    return pl.pallas_call(
        functools.partial(_hgrn_body, key_dim=key_dim), grid=(n_batch, n_hg, nl + nc),
        in_specs=specs, out_specs=out_specs,
        out_shape=[jax.ShapeDtypeStruct((t, n_heads * key_dim), F32)] * 2,
        scratch_shapes=[pltpu.VMEM((heads_per_step, key_dim, key_dim), F32)] * 2
        + [pltpu.VMEM((2 * heads_per_step, 4, tl, key_dim), F32), pltpu.VMEM((2 * heads_per_step, tl, key_dim), F32)],
        compiler_params=_params("arbitrary", "arbitrary", "arbitrary"), name="hgrn_scan",
    )(proj_hg, proj_hg, proj_hg, proj_hg, proj_hg, proj_hg, lb)


def _mix_out2_body(df_ref, db_ref, hf_ref, hb_ref, z_ref, og_ref, dnw_ref, hgw_ref, o_ref, *, head_dim):
    def normed(o, nw):
        outs = []
        for h in range(o.shape[1] // head_dim):
            oh = o[:, h * head_dim:(h + 1) * head_dim]
            outs.append(oh * lax.rsqrt(jnp.mean(oh * oh, axis=-1, keepdims=True) + NORM_EPS) * nw)
        return jnp.concatenate(outs, axis=1)

    dn = normed(df_ref[...] + db_ref[...], dnw_ref[...]) * _silu(z_ref[...])
    hg = normed(hf_ref[...] + hb_ref[...], hgw_ref[...]) * _sigmoid(og_ref[...])
    half = dn.shape[1]
    o_ref[:, :half] = dn.astype(o_ref.dtype)
    o_ref[:, half:] = hg.astype(o_ref.dtype)


def mix_out(dn_f, dn_b, hg_f, hg_b, z_src, z_blk, og_src, og_blk, dn_norm, hg_norm, head_dim):
    t, w = dn_f.shape
    tl = SCAN_TILE
    row = lambda i: (i, 0)
    body = functools.partial(_mix_out2_body, head_dim=head_dim)
    return pl.pallas_call(
        body, grid=(t // tl,),
        in_specs=[pl.BlockSpec((tl, w), row)] * 4
        + [pl.BlockSpec((tl, w), lambda i: (i, z_blk)), pl.BlockSpec((tl, w), lambda i: (i, og_blk)),
           pl.BlockSpec((1, head_dim), lambda i: (0, 0)), pl.BlockSpec((1, head_dim), lambda i: (0, 0))],
        out_specs=pl.BlockSpec((tl, 2 * w), row),
        out_shape=jax.ShapeDtypeStruct((t, 2 * w), BF16),
        compiler_params=_params("arbitrary"), name="mix_out",
    )(dn_f, dn_b, hg_f, hg_b, z_src, og_src, dn_norm.reshape(1, head_dim), hg_norm.reshape(1, head_dim))


def _first_max(x, ids, n):
    m = jnp.max(x, axis=0, keepdims=True)
    first = jnp.min(jnp.where(x == m, ids, n), axis=0, keepdims=True)
    return m, first


def _route_body(lg_ref, bias_ref, idx_ref, rank_ref, w_ref, cnt_ref, carry_ref, *,
                n_groups, topk_groups, top_k, scale):
    i = pl.program_id(0)

    @pl.when(i == 0)
    def _():
        carry_ref[...] = jnp.zeros_like(carry_ref)

    lg = lg_ref[...]
    n_exp, tl = lg.shape
    per = n_exp // n_groups
    scores = 1.0 / (1.0 + jnp.exp(-lg))
    biased = scores + bias_ref[...]
    sub = lax.broadcasted_iota(jnp.int32, (per, tl), 0)
    g_rows = []
    for g in range(n_groups):
        xg = biased[g * per:(g + 1) * per]
        m1, i1 = _first_max(xg, sub, per)
        m2 = jnp.max(jnp.where(sub == i1, NEG_INF, xg), axis=0, keepdims=True)
        g_rows.append(m1 + m2)
    gscore = jnp.concatenate(g_rows, axis=0)
    gid = lax.broadcasted_iota(jnp.int32, (n_groups, tl), 0)
    gsel = jnp.zeros((n_groups, tl), jnp.bool_)
    for _ in range(topk_groups):
        _, first = _first_max(gscore, gid, n_groups)
        hit = gid == first
        gsel = gsel | hit
        gscore = jnp.where(hit, NEG_INF, gscore)
    eid = lax.broadcasted_iota(jnp.int32, (n_exp, tl), 0)
    gmask = jnp.concatenate([jnp.broadcast_to(gsel[g:g + 1], (per, tl)) for g in range(n_groups)], axis=0)
    masked = jnp.where(gmask, biased, NEG_INF)
    sel = jnp.zeros((n_exp, tl), jnp.bool_)
    hits, firsts = [], []
    for _ in range(top_k):
        _, first = _first_max(masked, eid, n_exp)
        hit = eid == first
        hits.append(hit)
        firsts.append(first)
        sel = sel | hit
        masked = jnp.where(hit, NEG_INF, masked)
    self = jnp.where(sel, 1.0, 0.0)
    ti = lax.broadcasted_iota(jnp.int32, (tl, tl), 0)
    tj = lax.broadcasted_iota(jnp.int32, (tl, tl), 1)
    before = jnp.where(ti < tj, 1.0, 0.0).astype(BF16)
    carry = carry_ref[...]
    rank_full = jnp.dot(self.astype(BF16), before, preferred_element_type=F32) + carry
    w_rows = [jnp.sum(jnp.where(hit, scores, 0.0), axis=0, keepdims=True) for hit in hits]
    r_rows = [jnp.sum(jnp.where(hit, rank_full, 0.0), axis=0, keepdims=True) for hit in hits]
    w8 = jnp.concatenate(w_rows, axis=0)
    idx_ref[...] = jnp.concatenate(firsts, axis=0)
    rank_ref[...] = jnp.concatenate(r_rows, axis=0).astype(jnp.int32)
    w_ref[...] = w8 / jnp.sum(w8, axis=0, keepdims=True) * scale
    carry = carry + jnp.sum(self, axis=1, keepdims=True)
    carry_ref[...] = carry
    cnt_ref[...] = carry.astype(jnp.int32)


def route(logits_t, router_b, tl=256):
    n_exp, t = logits_t.shape
    body = functools.partial(_route_body, n_groups=N_GROUPS, topk_groups=TOPK_GROUPS, top_k=TOP_K,
                             scale=ROUTED_SCALE)
    tok_spec = pl.BlockSpec((TOP_K, tl), lambda i: (0, i))
    return pl.pallas_call(
        body, grid=(t // tl,),
        in_specs=[pl.BlockSpec((n_exp, tl), lambda i: (0, i)), pl.BlockSpec((n_exp, 1), lambda i: (0, 0))],
        out_specs=[tok_spec, tok_spec, tok_spec, pl.BlockSpec((n_exp, 1), lambda i: (0, 0))],
        out_shape=[jax.ShapeDtypeStruct((TOP_K, t), jnp.int32), jax.ShapeDtypeStruct((TOP_K, t), jnp.int32),
                   jax.ShapeDtypeStruct((TOP_K, t), F32), jax.ShapeDtypeStruct((n_exp, 1), jnp.int32)],
        scratch_shapes=[pltpu.VMEM((n_exp, 1), F32)],
        compiler_params=_params("arbitrary"), name="route",
    )(logits_t, router_b.reshape(n_exp, 1))


def _slab_copy(src_ref, src_tok, dst_ref, dst_tok, rows, sem):
    return pltpu.make_async_copy(src_ref.at[pl.ds(pl.multiple_of(src_tok * rows, rows), rows)],
                                 dst_ref.at[pl.ds(pl.multiple_of(dst_tok * rows, rows), rows)], sem)


def _dispatch_body(dest_ref, pad_ref, h_ref, xs_ref, zero_ref, sem, *, top_k, rows):
    tl = h_ref.shape[0] // rows
    n_exp = pad_ref.shape[1]

    def issue(r, carry):
        for k in range(top_k):
            _slab_copy(h_ref, r, xs_ref, dest_ref[r * top_k + k], rows, sem.at[0]).start(priority=k % 2)
        return carry

    lax.fori_loop(0, tl, issue, 0, unroll=DMA_LOOP_UNROLL)

    @pl.when(pl.program_id(0) == 0)
    def _():
        zero_ref[...] = jnp.zeros_like(zero_ref)
        largest = zero_ref.shape[0] // rows
        pieces = [largest >> b for b in range(largest.bit_length())]

        def pad_copies(e, wait):
            first, n_pad = pad_ref[0, e], pad_ref[1, e]
            for size in pieces:
                @pl.when((n_pad & size) != 0)
                def _():
                    slot0 = first + (n_pad & ~(2 * size - 1))
                    cp = pltpu.make_async_copy(
                        zero_ref.at[pl.ds(0, size * rows)],
                        xs_ref.at[pl.ds(pl.multiple_of(slot0 * rows, rows), size * rows)], sem.at[1])
                    if wait:
                        cp.wait()
                    else:
                        cp.start()

        def fill(e, carry):
            pad_copies(e, False)
            return carry

        def drain_pad(e, carry):
            pad_copies(e, True)
            return carry

        lax.fori_loop(0, n_exp, fill, 0)
        lax.fori_loop(0, n_exp, drain_pad, 0)

    def drain(r, carry):
        for k in range(top_k):
            _slab_copy(h_ref, 0, xs_ref, 0, rows, sem.at[0]).wait()
        return carry

    lax.fori_loop(0, tl, drain, 0, unroll=DMA_LOOP_UNROLL)


def dispatch(h_slab, dest_flat, pad_info, n_slot, top_k, rows, tl=256):
    t = h_slab.shape[0] // rows
    return pl.pallas_call(
        functools.partial(_dispatch_body, top_k=top_k, rows=rows), grid=(t // tl,),
        in_specs=[pl.BlockSpec((tl * top_k,), lambda i: (i,), memory_space=pltpu.SMEM),
                  pl.BlockSpec(memory_space=pltpu.SMEM),
                  pl.BlockSpec((tl * rows, LANES), lambda i: (i, 0))],
        out_specs=pl.BlockSpec(memory_space=pl.ANY),
        out_shape=jax.ShapeDtypeStruct((n_slot * rows, LANES), h_slab.dtype),
        scratch_shapes=[pltpu.VMEM((EXPERT_SLOT_BLOCK // 2 * rows, LANES), h_slab.dtype),
                        pltpu.SemaphoreType.DMA((2,))],
        compiler_params=_params("arbitrary"), name="dispatch",
    )(dest_flat, pad_info, h_slab)


def _combine_body(dest_ref, dest_next_ref, w_ref, sh_ref, x_ref, gate_ref, y_ref, o_ref, buf_ref, acc_ref, sem, *,
                  top_k, rows):
    i = pl.program_id(0)
    n = pl.num_programs(0)
    tl = x_ref.shape[0]

    def gather(d_ref, slot):
        def issue(r, carry):
            for k in range(top_k):
                _slab_copy(y_ref, d_ref[r * top_k + k], buf_ref.at[slot], k * tl + r, rows,
                           sem.at[slot]).start(priority=k % 2)
            return carry
        lax.fori_loop(0, tl, issue, 0, unroll=DMA_LOOP_UNROLL)

    @pl.when(i == 0)
    def _():
        gather(dest_ref, 0)

    @pl.when(i + 1 < n)
    def _():
        gather(dest_next_ref, (i + 1) % 2)

    slot = i % 2

    def drain(r, carry):
        for k in range(top_k):
            _slab_copy(y_ref, 0, buf_ref.at[slot], 0, rows, sem.at[slot]).wait()
        return carry

    lax.fori_loop(0, tl, drain, 0, unroll=DMA_LOOP_UNROLL)
    def unpack_f32(pk):
        return (lax.bitcast_convert_type(pk & jnp.uint32(0xFFFF0000), F32),
                lax.bitcast_convert_type(pk << jnp.uint32(16), F32))

    w = w_ref[...]
    acc_hi, acc_lo = unpack_f32(sh_ref[...])
    for k in range(top_k):
        wk = jnp.broadcast_to(w[:, k:k + 1], (tl, LANES))
        wk = jnp.broadcast_to(wk[:, None, :], (tl, rows, LANES)).reshape(tl * rows, LANES)
        y_hi, y_lo = unpack_f32(buf_ref[slot, pl.ds(k * tl * rows, tl * rows), :])
        acc_hi = acc_hi + y_hi * wk
        acc_lo = acc_lo + y_lo * wk
    acc_ref[0] = acc_hi
    acc_ref[1] = acc_lo
    for half in range(2):
        for c in range(rows):
            cols = slice((half * rows + c) * LANES, (half * rows + c + 1) * LANES)
            o_ref[:, cols] = (x_ref[:, cols]
                              + gate_ref[0][:, cols] * acc_ref.at[half][pl.ds(c, tl, stride=rows), :])


def combine(y_slab, dest_flat, w, shared_slab, x, mod3, gate_idx, geom, tl=COMBINE_TILE):
    t, d = x.shape
    rows = d // 2 // LANES
    top_k = w.shape[1]
    n_lat, lat_len, n_batch = geom
    n_tiles = t // tl
    row = functools.partial(_mod_row, tile_rows=tl, n_lat=n_lat, lat_len=lat_len, n_batch=n_batch)
    return pl.pallas_call(
        functools.partial(_combine_body, top_k=top_k, rows=rows), grid=(n_tiles,),
        in_specs=[pl.BlockSpec((tl * top_k,), lambda i: (i,), memory_space=pltpu.SMEM),
                  pl.BlockSpec((tl * top_k,), lambda i: (jnp.minimum(i + 1, n_tiles - 1),),
                               memory_space=pltpu.SMEM),
                  pl.BlockSpec((tl, top_k), lambda i: (i, 0)),
                  pl.BlockSpec((tl * rows, LANES), lambda i: (i, 0)),
                  pl.BlockSpec((tl, d), lambda i: (i, 0)),
                  pl.BlockSpec((1, 1, d), lambda i: (row(i) * 6 + gate_idx, 0, 0)),
                  pl.BlockSpec(memory_space=pl.ANY)],
        out_specs=pl.BlockSpec((tl, d), lambda i: (i, 0)),
        out_shape=jax.ShapeDtypeStruct((t, d), F32),
        scratch_shapes=[pltpu.VMEM((2, top_k * tl * rows, LANES), jnp.uint32),
                        pltpu.VMEM((2, tl * rows, LANES), F32), pltpu.SemaphoreType.DMA((2,))],
        compiler_params=_params("arbitrary"), name="combine",
    )(dest_flat, dest_flat, w, shared_slab, x, mod3, y_slab)


def moe_ffn(x, h_pk, logits_t, router_b, exp_gate_up, exp_down, shared_gate_up, shared_down, mod3, geom):
    t = x.shape[0]
    slab_rows = h_pk.shape[0] // t
    e_count = exp_gate_up.shape[0]
    blk = EXPERT_SLOT_BLOCK
    idx8, rank8, w8, counts = route(logits_t, router_b)
    counts = counts[:, 0]
    padded = (counts + blk - 1) // blk * blk
    pad_end = jnp.cumsum(padded)
    pad_start = pad_end - padded
    start8 = jnp.sum(jnp.where(idx8[None] == jnp.arange(e_count, dtype=jnp.int32)[:, None, None],
                               pad_start[:, None, None], 0), axis=0)
    dest_flat = (start8 + rank8).T.reshape(-1)
    n_blk = (t * TOP_K + e_count * (blk - 1)) // blk + 1
    n_blk = -(-n_blk // EXPERT_BLOCKS_PER_STEP) * EXPERT_BLOCKS_PER_STEP
    n_slot = n_blk * blk
    blk_starts = jnp.arange(n_blk, dtype=jnp.int32) * blk
    blk_e = jnp.sum((pad_end[None, :] <= blk_starts[:, None]).astype(jnp.int32), axis=1)
    blk_e = jnp.minimum(blk_e, e_count - 1)
    n_used = (pad_end[-1] // blk).astype(jnp.int32).reshape(1)
    pad_info = jnp.stack([pad_start + counts, padded - counts]).astype(jnp.int32)
    x_sorted = dispatch(h_pk, dest_flat, pad_info, n_slot, TOP_K, slab_rows)
    y_slot = expert_blocks(x_sorted, exp_gate_up, exp_down, blk_e, n_used, blk)
    shared = expert_blocks(h_pk, shared_gate_up[None], shared_down[None], jnp.zeros((t // blk,), jnp.int32),
                           jnp.full((1,), t // blk, jnp.int32), blk)
    return combine(y_slot, dest_flat, w8.T, shared, x, mod3, 5, geom)


def kernel(x, c, ctx, c_ctx, hg_lb_logits, l0_mod_w, l0_mod_b, l0_norm1, l0_norm2, l0_w_in, l0_dn_conv, l0_dn_a_log, l0_dn_dt_bias, l0_dn_norm, l0_hg_norm, l0_w_out, l0_router_w, l0_router_b, l0_exp_gate_up, l0_exp_down, l0_shared_gate_up, l0_shared_down, l1_mod_w, l1_mod_b, l1_norm1, l1_norm2, l1_w_in, l1_q_norm, l1_k_norm, l1_lambda, l1_sub_norm, l1_w_out, l1_router_w, l1_router_b, l1_exp_gate_up, l1_exp_down, l1_shared_gate_up, l1_shared_down):
    n_batch, lat_len, d = x.shape
    ctx_len = ctx.shape[1]
    n_lat = n_batch * lat_len
    n_ctx = n_batch * ctx_len
    geom = (n_lat, lat_len, n_batch)
    geom_lat_only = (n_lat, lat_len, n_batch)

    xs = jnp.concatenate([x.reshape(n_lat, d), ctx.reshape(n_ctx, d)], axis=0)
    cond = jnp.concatenate([c, c_ctx[None], jnp.zeros((MOD_ROWS - n_batch - 1, d), F32)], axis=0)

    mod3 = modulation(cond, l0_mod_w, l0_mod_b).reshape(MOD_ROWS * 6, 1, d)
    h = adaln(xs, l0_norm1, mod3, 0, geom)
    n_dn = IN0_SIZES[0] + IN0_SIZES[1]
    n_gate = IN0_SIZES[2] + IN0_SIZES[3]
    proj_dn = matmul(h, l0_w_in, n_cols=n_dn)
    proj_hg = matmul(h, l0_w_in[:, n_dn + n_gate:])
    gc, gr = gate_prep(h, l0_w_in[:, n_dn:n_dn + n_gate], l0_dn_a_log, l0_dn_dt_bias, DN_HEADS)
    conv_t = jnp.concatenate([l0_dn_conv.T, jnp.zeros((8 - l0_dn_conv.shape[1], 3 * DN_WIDTH), F32)], axis=0)
    qkv = dn_prep(proj_dn, conv_t, geom, ctx_len, DN_WIDTH, DN_HEAD_DIM)
    dn_f, dn_b = delta_scan(qkv, gc, gr, geom, ctx_len, DN_HEADS, DN_HEAD_DIM, inv_passes=DELTA_INV_PASSES)
    lb = jnp.cumsum(jax.nn.softmax(hg_lb_logits, axis=0), axis=0)[0:1]
    hg_f, hg_b = hgrn_scan(proj_hg, lb, geom, ctx_len, HG_HEADS, HG_KEY_DIM)
    y = mix_out(dn_f, dn_b, hg_f, hg_b, proj_dn, 3, proj_hg, 4, l0_dn_norm, l0_hg_norm, DN_HEAD_DIM)
    xs = matmul_residual(y, l0_w_out, xs, mod3, 2, geom)
    h, logits_t = adaln(xs, l0_norm2, mod3, 3, geom, router_w=l0_router_w)
    xs = moe_ffn(xs, h, logits_t, l0_router_b, l0_exp_gate_up, l0_exp_down, l0_shared_gate_up, l0_shared_down,
                 mod3, geom)

    mod3 = modulation(cond, l1_mod_w, l1_mod_b).reshape(MOD_ROWS * 6, 1, d)
    h = adaln(xs, l1_norm1, mod3, 0, geom)
    cos_t, sin_t = rope_tables(lat_len, n_batch, n_ctx, DA_HEAD_DIM)
    qkv = matmul_qkv(h, l1_w_in, l1_q_norm, l1_k_norm, cos_t, sin_t)
    lam_init = 0.8 - 0.6 * math.exp(-0.3 * 1)
    lmbda = (jnp.exp(jnp.sum(l1_lambda[0] * l1_lambda[1])) - jnp.exp(jnp.sum(l1_lambda[2] * l1_lambda[3]))
             + lam_init)
    y = diff_attention(qkv, lmbda, l1_sub_norm, n_batch, lat_len, ctx_len, DA_HEADS, DA_HEAD_DIM,
                       1.0 - lam_init)
    xl = matmul_residual(y, l1_w_out, xs, mod3, 2, geom_lat_only)
    h, logits_t = adaln(xl, l1_norm2, mod3, 3, geom_lat_only, router_w=l1_router_w)
    xl = moe_ffn(xl, h, logits_t, l1_router_b, l1_exp_gate_up, l1_exp_down, l1_shared_gate_up, l1_shared_down,
                 mod3, geom_lat_only)
    return xl.reshape(n_batch, lat_len, d)
```

```python
import functools
import math

import jax
import jax.numpy as jnp
from jax import lax
from jax.experimental import pallas as pl
from jax.experimental.pallas import tpu as pltpu

F32 = jnp.float32
BF16 = jnp.bfloat16

NORM_EPS = 1e-6
GRID_W = 64
ROPE_BASE = 10000.0

DN_HEADS = 8
DN_HEAD_DIM = 128
DN_WIDTH = DN_HEADS * DN_HEAD_DIM
HG_HEADS = 8
HG_KEY_DIM = 128
HG_K_WIDTH = HG_HEADS * HG_KEY_DIM
HG_V_WIDTH = HG_K_WIDTH
IN0_SIZES = (3 * DN_WIDTH, DN_WIDTH, 2 * DN_HEADS, 2 * DN_HEADS,
             HG_K_WIDTH, 2 * HG_K_WIDTH, HG_V_WIDTH, HG_V_WIDTH)
DA_HEADS = 8
DA_HEAD_DIM = 128
N_EXPERTS = 64
TOP_K = 8
N_GROUPS = 8
TOPK_GROUPS = 4
ROUTED_SCALE = 2.5

VMEM_LIMIT_BYTES = 56 * 1024 * 1024
LANES = 128
MOD_ROWS = 8
ROW_TILE = 512
MM_TILE_M = 1024
MM_TILE_N = 1024
MM_EPILOGUE_CHUNKS = 4
ATTN_TILE_Q = 1024
ATTN_ROW_CHUNK = 128
EXPERT_SLOT_BLOCK = 256
EXPERT_BLOCKS_PER_STEP = 4
DMA_LOOP_UNROLL = 8
COMBINE_TILE = 128


def _params(*sem):
    return pltpu.CompilerParams(dimension_semantics=sem, vmem_limit_bytes=VMEM_LIMIT_BYTES)


def _sigmoid(x):
    return 0.5 * jnp.tanh(0.5 * x) + 0.5


def _silu(x):
    return x * _sigmoid(x)


def _mod_body(c_ref, w_ref, b_ref, o_ref):
    a = _silu(c_ref[...]).astype(BF16)
    o_ref[...] = jnp.dot(a, w_ref[...].astype(BF16), preferred_element_type=F32) + b_ref[...]


def modulation(cond, w, b, tn=1024):
    m, k = cond.shape
    n = w.shape[1]
    return pl.pallas_call(
        _mod_body, grid=(n // tn,),
        in_specs=[pl.BlockSpec((m, k), lambda j: (0, 0)),
                  pl.BlockSpec((k, tn), lambda j: (0, j)),
                  pl.BlockSpec((1, tn), lambda j: (0, j))],
        out_specs=pl.BlockSpec((m, tn), lambda j: (0, j)),
        out_shape=jax.ShapeDtypeStruct((m, n), F32),
        compiler_params=_params("arbitrary"), name="modulation",
    )(cond, w, b.reshape(1, n))


def _mod_row(tile, tile_rows, n_lat, lat_len, n_batch):
    assert lat_len % tile_rows == 0 and n_lat % tile_rows == 0, (tile_rows, lat_len, n_lat)
    start = tile * tile_rows
    return jnp.where(start < n_lat, start // lat_len, n_batch)


def _pack_bf16_pairs(h):
    half = h.shape[1] // 2
    bits = lax.bitcast_convert_type(h.astype(BF16).astype(F32), jnp.uint32)
    return (bits[:, :half] & jnp.uint32(0xFFFF0000)) | (bits[:, half:] >> jnp.uint32(16))


def _unpack_bf16_pairs(pk):
    hi = lax.bitcast_convert_type(pk & jnp.uint32(0xFFFF0000), F32).astype(BF16)
    lo = lax.bitcast_convert_type(pk << jnp.uint32(16), F32).astype(BF16)
    return hi, lo


def _store_slabs(ref, x, row0=0):
    r, w = x.shape
    c_n = w // LANES
    for c in range(c_n):
        ref[pl.ds(row0 * c_n + c, r, stride=c_n), :] = x[:, c * LANES:(c + 1) * LANES]


def _adaln_body(x_ref, nw_ref, shift_ref, scale_ref, o_ref):
    x = x_ref[...]
    y = x * lax.rsqrt(jnp.mean(x * x, axis=-1, keepdims=True) + NORM_EPS) * nw_ref[...]
    o_ref[...] = (y * (1.0 + scale_ref[0]) + shift_ref[0]).astype(o_ref.dtype)


def _adaln_router_body(x_ref, nw_ref, shift_ref, scale_ref, rw_ref, o_ref, lg_ref):
    x = x_ref[...]
    y = x * lax.rsqrt(jnp.mean(x * x, axis=-1, keepdims=True) + NORM_EPS) * nw_ref[...]
    h = y * (1.0 + scale_ref[0]) + shift_ref[0]
    _store_slabs(o_ref, _pack_bf16_pairs(h))
    lg_ref[...] = lax.dot_general(rw_ref[...], h, (((1,), (1,)), ((), ())), preferred_element_type=F32,
                                  precision=lax.Precision.HIGHEST)


def adaln(x, norm_w, mod3, shift_idx, geom, router_w=None, tl=ROW_TILE):
    t, d = x.shape
    n_lat, lat_len, n_batch = geom
    row = functools.partial(_mod_row, tile_rows=tl, n_lat=n_lat, lat_len=lat_len, n_batch=n_batch)
    in_specs = [pl.BlockSpec((tl, d), lambda i: (i, 0)),
                pl.BlockSpec((1, d), lambda i: (0, 0)),
                pl.BlockSpec((1, 1, d), lambda i: (row(i) * 6 + shift_idx, 0, 0)),
                pl.BlockSpec((1, 1, d), lambda i: (row(i) * 6 + shift_idx + 1, 0, 0))]
    args = [x, norm_w.reshape(1, d), mod3, mod3]
    if router_w is None:
        return pl.pallas_call(
            _adaln_body, grid=(t // tl,), in_specs=in_specs,
            out_specs=pl.BlockSpec((tl, d), lambda i: (i, 0)),
            out_shape=jax.ShapeDtypeStruct((t, d), BF16),
            compiler_params=_params("arbitrary"), name="adaln")(*args)
    e = router_w.shape[1]
    return pl.pallas_call(
        _adaln_router_body, grid=(t // tl,),
        in_specs=in_specs + [pl.BlockSpec((e, d), lambda i: (0, 0))],
        out_specs=[pl.BlockSpec((tl * (d // 2 // LANES), LANES), lambda i: (i, 0)),
                   pl.BlockSpec((e, tl), lambda i: (0, i))],
        out_shape=[jax.ShapeDtypeStruct((t * (d // 2 // LANES), LANES), jnp.uint32),
                   jax.ShapeDtypeStruct((e, t), F32)],
        compiler_params=_params("arbitrary"), name="adaln_router")(*args, router_w.T)


def _mm_body(a_ref, w_ref, o_ref, wb_ref):
    @pl.when(pl.program_id(1) == 0)
    def _():
        wb_ref[...] = w_ref[...].astype(BF16)
    o_ref[...] = jnp.dot(a_ref[...], wb_ref[...], preferred_element_type=F32).astype(o_ref.dtype)


def _mm_res_body(a_ref, w_ref, res_ref, gate_ref, o_ref, wb_ref):
    @pl.when(pl.program_id(1) == 0)
    def _():
        wb_ref[...] = w_ref[...].astype(BF16)
    acc = jnp.dot(a_ref[...], wb_ref[...], preferred_element_type=F32)
    o_ref[...] = res_ref[...] + gate_ref[0] * acc


def matmul(a, w, n_cols=None, col_block0=0, tm=MM_TILE_M, tn=MM_TILE_N, out_dtype=F32):
    m, k = a.shape
    n = w.shape[1] if n_cols is None else n_cols
    return pl.pallas_call(
        _mm_body, grid=(n // tn, m // tm),
        in_specs=[pl.BlockSpec((tm, k), lambda j, i: (i, 0)),
                  pl.BlockSpec((k, tn), lambda j, i: (0, j + col_block0))],
        out_specs=pl.BlockSpec((tm, tn), lambda j, i: (i, j)),
        out_shape=jax.ShapeDtypeStruct((m, n), out_dtype),
        scratch_shapes=[pltpu.VMEM((k, tn), BF16)],
        compiler_params=_params("arbitrary", "arbitrary"), name="matmul",
    )(a, w)


def matmul_residual(a, w, res, mod3, gate_idx, geom, tm=MM_TILE_M, tn=MM_TILE_N):
    m, k = a.shape
    n = w.shape[1]
    n_lat, lat_len, n_batch = geom
    row = functools.partial(_mod_row, tile_rows=tm, n_lat=n_lat, lat_len=lat_len, n_batch=n_batch)
    nb = n // tn
    return pl.pallas_call(
        _mm_res_body, grid=(n // tn, m // tm),
        in_specs=[pl.BlockSpec((tm, k), lambda j, i: (i, 0)),
                  pl.BlockSpec((k, tn), lambda j, i: (0, j)),
                  pl.BlockSpec((tm, tn), lambda j, i: (i, j)),
                  pl.BlockSpec((1, 1, tn), lambda j, i: (row(i) * 6 + gate_idx, 0, j))],
        out_specs=pl.BlockSpec((tm, tn), lambda j, i: (i, j)),
        out_shape=jax.ShapeDtypeStruct((m, n), F32),
        scratch_shapes=[pltpu.VMEM((k, tn), BF16)],
        compiler_params=_params("arbitrary", "arbitrary"), name="matmul_residual",
    )(a, w, res, mod3)


def _expert_body(blk_e_ref, blk_new_ref, next_e_ref, stage_ref, n_used_ref, x_ref, gu_hbm, dn_hbm, o_ref,
                 gu_stage, dn_stage, gub_ref, dnb_ref, sem):
    ff, d = dnb_ref.shape
    half = d // 2
    xc = half // LANES
    blk = x_ref.shape[0] // xc // EXPERT_BLOCKS_PER_STEP

    def weight_copies(e, slot):
        return (pltpu.make_async_copy(gu_hbm.at[e], gu_stage.at[slot], sem.at[0, slot]),
                pltpu.make_async_copy(dn_hbm.at[e], dn_stage.at[slot], sem.at[1, slot]))

    @pl.when(pl.program_id(0) == 0)
    def _():
        for cp in weight_copies(blk_e_ref[0], 0):
            cp.start()

    def one_block(b, sub):
        @pl.when(blk_new_ref[b] == 1)
        def _():
            slot = stage_ref[b]
            for cp in weight_copies(blk_e_ref[b], slot):
                cp.wait()
            gub_ref[...] = gu_stage[slot].astype(BF16)
            dnb_ref[...] = dn_stage[slot].astype(BF16)

            @pl.when(next_e_ref[b] >= 0)
            def _():
                for cp in weight_copies(next_e_ref[b], 1 - slot):
                    cp.start()

        @pl.when(b < n_used_ref[0])
        def _():
            h1 = None
            for p in range(2):
                cs = range(p * xc // 2, (p + 1) * xc // 2)
                pk = jnp.concatenate([x_ref[pl.ds(sub * blk * xc + c, blk, stride=xc), :] for c in cs], axis=1)
                x_hi, x_lo = _unpack_bf16_pairs(pk)
                k0 = p * half // 2
                part = (jnp.dot(x_hi, gub_ref[k0:k0 + half // 2], preferred_element_type=F32)
                        + jnp.dot(x_lo, gub_ref[half + k0:half + k0 + half // 2], preferred_element_type=F32))
                h1 = part if h1 is None else h1 + part
            act = (_silu(h1[:, :ff]) * h1[:, ff:]).astype(BF16)
            y = jnp.concatenate([jnp.dot(act, dnb_ref[:, p * half:(p + 1) * half], preferred_element_type=F32)
                                 for p in range(2)], axis=1)
            _store_slabs(o_ref, _pack_bf16_pairs(y), sub * blk)

        @pl.when(b >= n_used_ref[0])
        def _():
            o_ref[pl.ds(sub * blk * xc, blk * xc), :] = jnp.zeros((blk * xc, LANES), o_ref.dtype)

    for sub in range(EXPERT_BLOCKS_PER_STEP):
        one_block(pl.program_id(0) * EXPERT_BLOCKS_PER_STEP + sub, sub)


def expert_blocks(x, gate_up, down, blk_e, n_used, blk):
    _, d, f2 = gate_up.shape
    xc = yc = d // 2 // LANES
    s = x.shape[0] // xc
    n_blk = s // blk
    assert n_blk % EXPERT_BLOCKS_PER_STEP == 0, (n_blk, EXPERT_BLOCKS_PER_STEP)
    pos = jnp.arange(n_blk, dtype=jnp.int32)
    used = pos < n_used[0]
    blk_new = (jnp.concatenate([jnp.ones((1,), bool), blk_e[1:] != blk_e[:-1]]) & used).astype(jnp.int32)
    stage = (jnp.cumsum(blk_new) - 1) % 2
    first_pos = jnp.where(blk_new == 1, pos, n_blk)
    next_first = jnp.concatenate([lax.cummin(first_pos, reverse=True)[1:], jnp.full((1,), n_blk, jnp.int32)])
    next_e = jnp.where(next_first < n_blk, blk_e[jnp.minimum(next_first, n_blk - 1)], -1).astype(jnp.int32)
    bps = EXPERT_BLOCKS_PER_STEP
    grid_spec = pltpu.PrefetchScalarGridSpec(
        num_scalar_prefetch=5, grid=(n_blk // bps,),
        in_specs=[pl.BlockSpec((bps * blk * xc, LANES),
                               lambda i, be, bn, ne, st, nu: (jnp.minimum(i, (nu[0] - 1) // bps), 0)),
                  pl.BlockSpec(memory_space=pl.ANY),
                  pl.BlockSpec(memory_space=pl.ANY)],
        out_specs=pl.BlockSpec((bps * blk * yc, LANES), lambda i, be, bn, ne, st, nu: (i, 0)),
        scratch_shapes=[pltpu.VMEM((2, d, f2), F32), pltpu.VMEM((2, f2 // 2, d), F32),
                        pltpu.VMEM((d, f2), BF16), pltpu.VMEM((f2 // 2, d), BF16),
                        pltpu.SemaphoreType.DMA((2, 2))])
    return pl.pallas_call(
        _expert_body, grid_spec=grid_spec,
        out_shape=jax.ShapeDtypeStruct((s * yc, LANES), jnp.uint32),
        compiler_params=_params("arbitrary"), name="expert_blocks",
    )(blk_e, blk_new, next_e, stage.astype(jnp.int32), n_used, x, gate_up, down)


def _norm_rope(x, w, cos, sin, first, head_dim):
    y = x * lax.rsqrt(jnp.mean(x * x, axis=-1, keepdims=True) + NORM_EPS) * w
    swapped = jnp.where(first, pltpu.roll(y, head_dim - head_dim // 4, 1), pltpu.roll(y, head_dim // 4, 1))
    return y * cos + swapped * sin


def _mm_qkv_body(a_ref, w_ref, qw_ref, kw_ref, cos_ref, sin_ref, o_ref, wb_ref, *, head_dim, n_q_tiles, n_k_tiles):
    j = pl.program_id(0)

    @pl.when(pl.program_id(1) == 0)
    def _():
        wb_ref[...] = w_ref[...].astype(BF16)

    tm, tn = o_ref.shape
    row_chunk = tm // MM_EPILOGUE_CHUNKS

    def rows_chain(rows, nw_ref):
        acc = jnp.dot(a_ref[rows, :], wb_ref[...], preferred_element_type=F32)
        yield
        if nw_ref is None:
            o_ref[rows, :] = acc.astype(o_ref.dtype)
            return
        cos, sin = cos_ref[rows, :], sin_ref[rows, :]
        lane = lax.broadcasted_iota(jnp.int32, cos.shape, 1)
        first = (lane % (head_dim // 2)) < (head_dim // 4)
        for g in range(tn // head_dim):
            sl = slice(g * head_dim, (g + 1) * head_dim)
            o_ref[rows, sl] = _norm_rope(acc[:, sl], nw_ref[...], cos, sin, first, head_dim).astype(o_ref.dtype)
            yield

    def tile(nw_ref):
        _run_interleaved(rows_chain(slice(r, r + row_chunk), nw_ref) for r in range(0, tm, row_chunk))

    @pl.when(j < n_q_tiles)
    def _():
        tile(qw_ref)

    @pl.when((j >= n_q_tiles) & (j < n_q_tiles + n_k_tiles))
    def _():
        tile(kw_ref)

    @pl.when(j >= n_q_tiles + n_k_tiles)
    def _():
        tile(None)


def matmul_qkv(a, w, q_norm, k_norm, cos_t, sin_t, tm=MM_TILE_M, tn=MM_TILE_N):
    m, k = a.shape
    n = w.shape[1]
    hd = q_norm.shape[0]
    body = functools.partial(_mm_qkv_body, head_dim=hd, n_q_tiles=n // 3 // tn, n_k_tiles=n // 3 // tn)
    return pl.pallas_call(
        body, grid=(n // tn, m // tm),
        in_specs=[pl.BlockSpec((tm, k), lambda j, i: (i, 0)),
                  pl.BlockSpec((k, tn), lambda j, i: (0, j)),
                  pl.BlockSpec((1, hd), lambda j, i: (0, 0)),
                  pl.BlockSpec((1, hd), lambda j, i: (0, 0)),
                  pl.BlockSpec((tm, hd), lambda j, i: (i, 0)),
                  pl.BlockSpec((tm, hd), lambda j, i: (i, 0))],
        out_specs=pl.BlockSpec((tm, tn), lambda j, i: (i, j)),
        out_shape=jax.ShapeDtypeStruct((m, n), BF16),
        scratch_shapes=[pltpu.VMEM((k, tn), BF16)],
        compiler_params=_params("arbitrary", "arbitrary"), name="matmul_qkv",
    )(a, w, q_norm.reshape(1, hd), k_norm.reshape(1, hd), cos_t, sin_t)


def rope_tables(n_lat_tokens_per_sample, n_batch, n_ctx_tokens, head_dim):
    quarter = head_dim // 4
    inv_freq = ROPE_BASE ** (-jnp.arange(quarter, dtype=F32) / quarter)
    rows = n_lat_tokens_per_sample // GRID_W
    row = jnp.repeat(jnp.arange(rows, dtype=F32), GRID_W)
    col = jnp.tile(jnp.arange(GRID_W, dtype=F32), rows)
    ang_r = row[:, None] * inv_freq[None, :]
    ang_c = col[:, None] * inv_freq[None, :]
    cos = jnp.concatenate([jnp.cos(ang_r), jnp.cos(ang_r), jnp.cos(ang_c), jnp.cos(ang_c)], axis=-1)
    sin = jnp.concatenate([-jnp.sin(ang_r), jnp.sin(ang_r), -jnp.sin(ang_c), jnp.sin(ang_c)], axis=-1)
    cos = jnp.concatenate([jnp.tile(cos, (n_batch, 1)), jnp.ones((n_ctx_tokens, head_dim), F32)], axis=0)
    sin = jnp.concatenate([jnp.tile(sin, (n_batch, 1)), jnp.zeros((n_ctx_tokens, head_dim), F32)], axis=0)
    return cos, sin


def _diff_attn_rows(lam, q_ref, k_all, v_all, sw_ref, o_ref, rows, head_dim, out_scale):
    c = head_dim ** -0.5 * math.log2(math.e)
    es, invs = [], []
    for s in range(2):
        sl = slice(s * head_dim, (s + 1) * head_dim)
        sc = lax.dot_general(q_ref[rows, sl], k_all[:, sl], (((1,), (1,)), ((), ())), preferred_element_type=F32)
        yield
        e = jnp.exp2((sc - jnp.max(sc, axis=-1, keepdims=True)) * c)
        invs.append(1.0 / jnp.sum(e, axis=-1, keepdims=True))
        es.append(e.astype(BF16))
        yield
    v = v_all[...]
    o0 = jnp.dot(es[0], v, preferred_element_type=F32)
    o1 = jnp.dot(es[1], v, preferred_element_type=F32)
    yield
    o = o0 * invs[0] - (lam * invs[1]) * o1
    y = o * lax.rsqrt(jnp.mean(o * o, axis=-1, keepdims=True) + NORM_EPS) * sw_ref[...]
    o_ref[rows, :] = (y * out_scale).astype(o_ref.dtype)
    yield


def _diff_attn_body(lam_ref, q_ref, kl_ref, kc_ref, vl_ref, vc_ref, sw_ref, o_ref, k_all, v_all, *,
                    head_dim, out_scale, row_chunk):
    @pl.when(pl.program_id(2) == 0)
    def _():
        n_l = kl_ref.shape[0]
        k_all[:n_l] = kl_ref[...]
        k_all[n_l:] = kc_ref[...]
        v_all[:n_l] = vl_ref[...]
        v_all[n_l:] = vc_ref[...]

    lam = lam_ref[0]
    tq = q_ref.shape[0]
    _run_interleaved(
        _diff_attn_rows(lam, q_ref, k_all, v_all, sw_ref, o_ref, slice(r, r + row_chunk), head_dim, out_scale)
        for r in range(0, tq, row_chunk))


def diff_attention(qkv, lmbda, sub_norm, n_batch, lat_len, ctx_len, n_heads, head_dim, out_scale,
                   tq=ATTN_TILE_Q, row_chunk=ATTN_ROW_CHUNK):
    hw = 2 * head_dim
    nq = lat_len // tq
    ctx_blk0 = n_batch * lat_len // ctx_len
    body = functools.partial(_diff_attn_body, head_dim=head_dim, out_scale=out_scale, row_chunk=row_chunk)
    n_keys = lat_len + ctx_len
    return pl.pallas_call(
        body, grid=(n_batch, n_heads, nq),
        in_specs=[pl.BlockSpec(memory_space=pltpu.SMEM),
                  pl.BlockSpec((tq, hw), lambda b, h, i: (b * nq + i, h)),
                  pl.BlockSpec((lat_len, hw), lambda b, h, i: (b, n_heads + h)),
                  pl.BlockSpec((ctx_len, hw), lambda b, h, i: (ctx_blk0 + b, n_heads + h)),
                  pl.BlockSpec((lat_len, hw), lambda b, h, i: (b, 2 * n_heads + h)),
                  pl.BlockSpec((ctx_len, hw), lambda b, h, i: (ctx_blk0 + b, 2 * n_heads + h)),
                  pl.BlockSpec((1, hw), lambda b, h, i: (0, 0))],
        out_specs=pl.BlockSpec((tq, hw), lambda b, h, i: (b * nq + i, h)),
        out_shape=jax.ShapeDtypeStruct((n_batch * lat_len, n_heads * hw), BF16),
        scratch_shapes=[pltpu.VMEM((n_keys, hw), BF16), pltpu.VMEM((n_keys, hw), BF16)],
        compiler_params=_params("arbitrary", "arbitrary", "arbitrary"), name="diff_attention",
    )(lmbda.reshape(1), qkv, qkv, qkv, qkv, qkv, sub_norm.reshape(1, hw))


SCAN_TILE = 256
CHUNK = 64
SUB = 16
NEG_BIG = -1e30
NEG_INF = float("-inf")
DELTA_INV_PASSES = 1
DELTA_HEADS_PER_STEP = 4
HGRN_HEADS_PER_STEP = 4


def _dot(a, b):
    return jnp.dot(a.astype(BF16), b.astype(BF16), preferred_element_type=F32)


def _dot_nt(a, b):
    return lax.dot_general(a.astype(BF16), b.astype(BF16), (((1,), (1,)), ((), ())),
                           preferred_element_type=F32)


def _split3(x):
    hi = x.astype(BF16)
    r = x - hi.astype(F32)
    mid = r.astype(BF16)
    lo = (r - mid.astype(F32)).astype(BF16)
    return hi, mid, lo


def _dot_exact_lhs01(m01, x):
    hi, mid, lo = _split3(x)
    m = m01.astype(BF16)
    return (jnp.dot(m, hi, preferred_element_type=F32) + jnp.dot(m, mid, preferred_element_type=F32)
            + jnp.dot(m, lo, preferred_element_type=F32))


def _dot_exact_rhs01(x, m01):
    hi, mid, lo = _split3(x)
    m = m01.astype(BF16)
    return (jnp.dot(hi, m, preferred_element_type=F32) + jnp.dot(mid, m, preferred_element_type=F32)
            + jnp.dot(lo, m, preferred_element_type=F32))


def _dot3(a, b):
    ah = a.astype(BF16)
    al = (a - ah.astype(F32)).astype(BF16)
    bh = b.astype(BF16)
    bl = (b - bh.astype(F32)).astype(BF16)
    return (jnp.dot(ah, bh, preferred_element_type=F32) + jnp.dot(ah, bl, preferred_element_type=F32)
            + jnp.dot(al, bh, preferred_element_type=F32))


def _tile_masks(n, reverse):
    i = lax.broadcasted_iota(jnp.int32, (n, n), 0)
    j = lax.broadcasted_iota(jnp.int32, (n, n), 1)
    same = (i // CHUNK) == (j // CHUNK)
    if reverse:
        return same & (i <= j), same & (i < j)
    return same & (i >= j), same & (i > j)


def _segment_tile(b, s, reverse, nl, nc, ctx_tile0):
    if reverse:
        return jnp.where(s < nc, ctx_tile0 + b * nc + (nc - 1 - s), b * nl + (nl - 1 - (s - nc)))
    return jnp.where(s < nc, ctx_tile0 + b * nc + s, b * nl + (s - nc))


def _dn_prep_body(x_ref, prev_ref, next_ref, w_ref, o_ref, *, tiles_per_lat_seg, tiles_per_ctx_seg, n_lat_tiles,
                  head_dim, q_scale):
    i = pl.program_id(0)
    j = pl.program_id(1)
    is_lat = i < n_lat_tiles
    pos = jnp.where(is_lat, i % tiles_per_lat_seg, (i - n_lat_tiles) % tiles_per_ctx_seg)
    seg_first = pos == 0
    seg_last = pos == jnp.where(is_lat, tiles_per_lat_seg, tiles_per_ctx_seg) - 1
    x = x_ref[...]
    tl = x.shape[0]
    prev = jnp.where(seg_first, 0.0, prev_ref[...])
    nxt = jnp.where(seg_last, 0.0, next_ref[...])
    xp = jnp.concatenate([prev, x, nxt], axis=0)
    w = w_ref[...]
    n_taps = 5
    acc = None
    for t in range(n_taps):
        off = 8 + t - n_taps // 2
        term = xp[off:off + tl] * w[t:t + 1]
        acc = term if acc is None else acc + term
    y = _silu(acc)
    scale = jnp.where(j == 0, q_scale, 1.0)
    outs = []
    for h in range(y.shape[1] // head_dim):
        yh = y[:, h * head_dim:(h + 1) * head_dim]
        nrm = lax.rsqrt(jnp.sum(yh * yh, axis=-1, keepdims=True) + 1e-6) * scale
        outs.append(yh * jnp.where(j == 2, 1.0, nrm))
    o_ref[...] = jnp.concatenate(outs, axis=1)


def dn_prep(proj_dn, conv_w_t, geom, ctx_len, width, head_dim):
    n_lat, lat_len, n_batch = geom
    t = proj_dn.shape[0]
    tl = SCAN_TILE
    rows8 = tl // 8
    n_tiles = t // tl
    body = functools.partial(_dn_prep_body, tiles_per_lat_seg=lat_len // tl, tiles_per_ctx_seg=ctx_len // tl,
                             n_lat_tiles=n_lat // tl,
                             head_dim=head_dim, q_scale=head_dim ** -0.5)
    last8 = t // 8 - 1
    return pl.pallas_call(
        body, grid=(n_tiles, 3),
        in_specs=[pl.BlockSpec((tl, width), lambda i, j: (i, j)),
                  pl.BlockSpec((8, width), lambda i, j: (jnp.maximum(i * rows8 - 1, 0), j)),
                  pl.BlockSpec((8, width), lambda i, j: (jnp.minimum((i + 1) * rows8, last8), j)),
                  pl.BlockSpec((8, width), lambda i, j: (0, j))],
        out_specs=pl.BlockSpec((tl, width), lambda i, j: (i, j)),
        out_shape=jax.ShapeDtypeStruct((t, 3 * width), F32),
        compiler_params=_params("arbitrary", "arbitrary"), name="dn_prep",
    )(proj_dn, proj_dn, proj_dn, conv_w_t)


def _softplus(x):
    return jnp.maximum(x, 0.0) + jnp.log(1.0 + jnp.exp(-jnp.abs(x)))


def _gate_prep_body(h_ref, wc_ref, wr_ref, alog_c_ref, dtb_c_ref, alog_r_ref, dtb_r_ref, gc_ref, gr_ref, *, n_heads):
    h = h_ref[...]
    tl = h.shape[0]
    nd = 2 * n_heads
    raw_c = jnp.dot(h, wc_ref[...].astype(BF16), preferred_element_type=F32)
    raw_r = lax.dot_general(wr_ref[...].astype(BF16), h, (((1,), (1,)), ((), ())),
                            preferred_element_type=F32)
    g_c = -jnp.exp(alog_c_ref[...]) * _softplus(raw_c[:, :nd] + dtb_c_ref[...])
    g_r = -jnp.exp(alog_r_ref[...]) * _softplus(raw_r[:nd, :] + dtb_r_ref[...])
    incl_f, _ = _tile_masks(tl, False)
    incl_b, _ = _tile_masks(tl, True)
    one_f = jnp.where(incl_f, 1.0, 0.0)
    one_b = jnp.where(incl_b, 1.0, 0.0)
    cum_c = jnp.concatenate([_dot_exact_lhs01(one_f, g_c[:, :n_heads]),
                             _dot_exact_lhs01(one_b, g_c[:, n_heads:])], axis=1)
    cum_r = jnp.concatenate([_dot_exact_rhs01(g_r[:n_heads, :], one_b),
                             _dot_exact_rhs01(g_r[n_heads:, :], one_f)], axis=0)
    beta_c = _sigmoid(raw_c[:, nd:])
    gc_ref[...] = jnp.concatenate([cum_c, beta_c], axis=1)
    gr_ref[...] = jnp.concatenate([cum_r, jnp.zeros_like(cum_r)], axis=0)


def gate_prep(h_bf, w_gate, a_log, dt_bias, n_heads):
    t, d = h_bf.shape
    tl = SCAN_TILE
    nd = 2 * n_heads
    body = functools.partial(_gate_prep_body, n_heads=n_heads)
    full = lambda shape: pl.BlockSpec(shape, lambda i: (0, 0))
    return pl.pallas_call(
        body, grid=(t // tl,),
        in_specs=[pl.BlockSpec((tl, d), lambda i: (i, 0)), full((d, 2 * nd)), full((2 * nd, d)),
                  full((1, nd)), full((1, nd)), full((nd, 1)), full((nd, 1))],
        out_specs=[pl.BlockSpec((tl, 2 * nd), lambda i: (i, 0)), pl.BlockSpec((2 * nd, tl), lambda i: (0, i))],
        out_shape=[jax.ShapeDtypeStruct((t, 2 * nd), F32), jax.ShapeDtypeStruct((2 * nd, t), F32)],
        compiler_params=_params("arbitrary"), name="gate_prep",
    )(h_bf, w_gate, w_gate.T, a_log.reshape(1, nd), dt_bias.reshape(1, nd),
      a_log.reshape(nd, 1), dt_bias.reshape(nd, 1))


def _select_col(x, idx):
    lane = lax.broadcasted_iota(jnp.int32, x.shape, 1)
    return jnp.sum(jnp.where(lane == idx, x, 0.0), axis=1, keepdims=True)


def _select_row(x, idx):
    row = lax.broadcasted_iota(jnp.int32, x.shape, 0)
    return jnp.sum(jnp.where(row == idx, x, 0.0), axis=0, keepdims=True)


def _run_interleaved(chains):
    chains = list(chains)
    while chains:
        alive = []
        for ch in chains:
            try:
                next(ch)
                alive.append(ch)
            except StopIteration:
                pass
        chains = alive


def _delta_chain(q, k, v, gc_col, gc_row, beta_col, s_ref, o_ref, cols, reverse, inv_passes):
    tl, kd_ = k.shape
    n_chunks = tl // CHUNK
    incl, strict = _tile_masks(tl, reverse)
    decay = jnp.exp(jnp.where(incl, gc_col - gc_row, NEG_BIG))
    kb, qb = k.astype(BF16), q.astype(BF16)
    kkt = _dot_nt(kb, kb)
    qkt = _dot_nt(qb, kb)
    yield
    x = jnp.where(strict, kkt * (-beta_col) * decay, 0.0)
    dot_inv = _dot3 if inv_passes == 3 else _dot
    ri = lax.broadcasted_iota(jnp.int32, (tl, tl), 0)
    ci = lax.broadcasted_iota(jnp.int32, (tl, tl), 1)
    r = jnp.where(ri == ci, 1.0, 0.0) + x
    n_sq = int(math.log2(CHUNK)) - 1
    for _ in range(n_sq):
        x = dot_inv(x, x)
        r = r + dot_inv(r, x)
        yield
    e_g = jnp.exp(gc_col)
    rhs = jnp.concatenate([v * beta_col, k * (beta_col * e_g)], axis=1)
    sol = dot_inv(r, rhs)
    yield
    u0, w = sol[:, :v.shape[1]], sol[:, v.shape[1]:]
    attn = (qkt * decay).astype(BF16)
    a_sol = _dot(attn, sol)
    o0 = a_sol[:, :v.shape[1]]
    qe = q * e_g - a_sol[:, v.shape[1]:]
    tot_rows = []
    for c in range(n_chunks):
        last = c * CHUNK if reverse else c * CHUNK + CHUNK - 1
        tot_rows.append(jnp.broadcast_to(gc_col[last:last + 1, :], (CHUNK, 1)))
    tot = jnp.concatenate(tot_rows, axis=0)
    kdec_t = jnp.transpose(k * jnp.exp(tot - gc_col))
    wu = jnp.concatenate([-w, u0], axis=1).astype(BF16)
    lane = lax.broadcasted_iota(jnp.int32, kdec_t.shape, 1)
    pns = [_dot(jnp.where((lane // CHUNK) == c, kdec_t, 0.0), wu) for c in range(n_chunks)]
    yield
    order = range(n_chunks - 1, -1, -1) if reverse else range(n_chunks)
    for c in order:
        rows = slice(c * CHUNK, (c + 1) * CHUNK)
        pn = pns[c]
        lhs = jnp.concatenate([qe[rows], pn[:, :kd_]], axis=0)
        s = s_ref[...]
        res = _dot(lhs, s)
        o_ref[rows, cols] = o0[rows] + res[:CHUNK]
        last = c * CHUNK if reverse else c * CHUNK + CHUNK - 1
        gl = jnp.exp(gc_col[last:last + 1, :])
        s_ref[...] = gl * s + res[CHUNK:] + pn[:, kd_:]
        yield


def _delta_body(qf_ref, kf_ref, vf_ref, gcf_ref, grf_ref, qb_ref, kb_ref, vb_ref, gcb_ref, grb_ref,
                of_ref, ob_ref, sf_ref, sb_ref, *, n_heads, head_dim, inv_passes):
    hg = pl.program_id(1)
    heads_per_step = sf_ref.shape[0]

    @pl.when(pl.program_id(2) == 0)
    def _():
        sf_ref[...] = jnp.zeros_like(sf_ref)
        sb_ref[...] = jnp.zeros_like(sb_ref)

    chains = []
    for g in range(heads_per_step):
        cols = slice(g * head_dim, (g + 1) * head_dim)
        for reverse, (q_ref, k_ref, v_ref, gc_ref, gr_ref, o_ref, s_ref) in enumerate(
                [(qf_ref, kf_ref, vf_ref, gcf_ref, grf_ref, of_ref, sf_ref),
                 (qb_ref, kb_ref, vb_ref, gcb_ref, grb_ref, ob_ref, sb_ref)]):
            idx = reverse * n_heads + hg * heads_per_step + g
            gcs = gc_ref[...]
            gc_col = _select_col(gcs, idx)
            beta_col = _select_col(gcs, 2 * n_heads + idx)
            gc_row = _select_row(gr_ref[...], idx)
            chains.append(_delta_chain(q_ref[:, cols], k_ref[:, cols], v_ref[:, cols], gc_col, gc_row, beta_col,
                                       s_ref.at[g], o_ref, cols, bool(reverse), inv_passes))
    _run_interleaved(chains)


def delta_scan(qkv, gc, gr, geom, ctx_len, n_heads, head_dim, inv_passes=3, heads_per_step=DELTA_HEADS_PER_STEP):
    n_lat, lat_len, n_batch = geom
    t = qkv.shape[0]
    tl = SCAN_TILE
    nl, nc, ctx0 = lat_len // tl, ctx_len // tl, n_lat // tl
    n_hg = n_heads // heads_per_step
    gw = heads_per_step * head_dim
    tile = functools.partial(_segment_tile, nl=nl, nc=nc, ctx_tile0=ctx0)
    specs = []
    for reverse in (False, True):
        tix = functools.partial(tile, reverse=reverse)
        specs += [pl.BlockSpec((tl, gw), lambda b, h, s, tix=tix: (tix(b, s), h)),
                  pl.BlockSpec((tl, gw), lambda b, h, s, tix=tix: (tix(b, s), n_hg + h)),
                  pl.BlockSpec((tl, gw), lambda b, h, s, tix=tix: (tix(b, s), 2 * n_hg + h)),
                  pl.BlockSpec((tl, 4 * n_heads), lambda b, h, s, tix=tix: (tix(b, s), 0)),
                  pl.BlockSpec((4 * n_heads, tl), lambda b, h, s, tix=tix: (0, tix(b, s)))]
    out_specs = [pl.BlockSpec((tl, gw), lambda b, h, s, tix=functools.partial(tile, reverse=r): (tix(b, s), h))
                 for r in (False, True)]
    body = functools.partial(_delta_body, n_heads=n_heads, head_dim=head_dim, inv_passes=inv_passes)
    return pl.pallas_call(
        body, grid=(n_batch, n_hg, nl + nc), in_specs=specs, out_specs=out_specs,
        out_shape=[jax.ShapeDtypeStruct((t, n_heads * head_dim), F32)] * 2,
        scratch_shapes=[pltpu.VMEM((heads_per_step, head_dim, head_dim), F32)] * 2,
        compiler_params=_params("arbitrary", "arbitrary", "arbitrary"), name="delta_scan",
    )(qkv, qkv, qkv, gc, gr, qkv, qkv, qkv, gc, gr)


def _hgrn_diag(q, k, cum, v, stage_ref, diag_ref, reverse):
    n_sub = q.shape[0] // SUB
    for n, val in enumerate((q, k, cum, v)):
        stage_ref[n] = val

    def slabs(n):
        return [stage_ref.at[n][pl.ds(r, n_sub, stride=SUB), :] for r in range(SUB)]

    q_x, k_x, p_x, v_x = slabs(0), slabs(1), slabs(2), slabs(3)
    pairs = [(i, j) for i in range(SUB) for j in range(SUB) if (j >= i if reverse else j <= i)]
    terms = []
    for i, j in pairs:
        qk = q_x[i] * k_x[j]
        terms.append((qk if i == j else qk * jnp.exp(p_x[i] - p_x[j])).astype(BF16))
    kd_ = terms[0].shape[1]
    a_rep = jnp.dot(jnp.concatenate(terms, axis=0), jnp.ones((kd_, kd_), BF16), preferred_element_type=F32)
    o_x = [None] * SUB
    for n, (i, j) in enumerate(pairs):
        contrib = a_rep[n * n_sub:(n + 1) * n_sub] * v_x[j]
        o_x[i] = contrib if o_x[i] is None else o_x[i] + contrib
    for r in range(SUB):
        diag_ref[pl.ds(r, n_sub, stride=SUB), :] = o_x[r]


def _hgrn_chain(hq_ref, hf_ref, hv_ref, cols, lb, st_ref, o_ref, stage_ref, diag_ref, reverse):
    hq, hf, hv = hq_ref[:, cols], hf_ref[:, cols], hv_ref[:, cols]
    tl, kd_ = hq.shape
    n_chunks = tl // CHUNK
    n_sub = tl // SUB
    sub_per_chunk = CHUNK // SUB
    q = _silu(hq)
    f = lb + (1.0 - lb) * _sigmoid(hf)
    k = 1.0 - f
    lf = jnp.log(f)
    incl, _ = _tile_masks(tl, reverse)
    cum = _dot_exact_lhs01(jnp.where(incl, 1.0, 0.0), lf)
    excl = cum - lf

    def bcast_rows(src, row, n):
        return jnp.broadcast_to(src[row:row + 1, :], (n, kd_))

    chunk_last = [(c * CHUNK if reverse else c * CHUNK + CHUNK - 1) for c in range(n_chunks)]
    tot = jnp.concatenate([bcast_rows(cum, chunk_last[c], CHUNK) for c in range(n_chunks)], axis=0)
    sub_first = [(m * SUB + SUB - 1 if reverse else m * SUB) for m in range(n_sub)]
    r_sub = jnp.concatenate([bcast_rows(excl, sub_first[m], SUB) for m in range(n_sub)], axis=0)
    q_t = q * jnp.exp(cum - r_sub)
    qd = q * jnp.exp(cum)
    kd = k * jnp.exp(tot - cum)
    vb = hv.astype(BF16)

    i = lax.broadcasted_iota(jnp.int32, (tl, tl), 0)
    j = lax.broadcasted_iota(jnp.int32, (tl, tl), 1)
    same = (i // CHUNK) == (j // CHUNK)
    pos_i = (i % CHUNK) // SUB
    pos_j = (j % CHUNK) // SUB
    if reverse:
        pos_i, pos_j = sub_per_chunk - 1 - pos_i, sub_per_chunk - 1 - pos_j
    a_off = jnp.zeros((tl, tl), F32)
    for lvl in range(1, sub_per_chunk):
        ref_rows = []
        for c in range(n_chunks):
            m = c * sub_per_chunk + (sub_per_chunk - 1 - lvl if reverse else lvl)
            ref_rows.append(bcast_rows(excl, sub_first[m], CHUNK))
        r_lvl = jnp.concatenate(ref_rows, axis=0)
        k_t = k * jnp.exp(jnp.minimum(r_lvl - cum, 0.0))
        a_l = _dot_nt(q_t, k_t)
        a_off = a_off + jnp.where(same & (pos_i == lvl) & (pos_j < lvl), a_l, 0.0)
        yield
    o_intra = _dot(a_off, vb)
    yield

    _hgrn_diag(q, k, cum, hv, stage_ref, diag_ref, reverse)
    yield
    o_intra = o_intra + diag_ref[...]

    v_t = jnp.transpose(hv)
    lane = lax.broadcasted_iota(jnp.int32, v_t.shape, 1)
    kdb = kd.astype(BF16)
    n_ts = [_dot(jnp.where((lane // CHUNK) == c, v_t, 0.0), kdb) for c in range(n_chunks)]
    yield
    order = range(n_chunks - 1, -1, -1) if reverse else range(n_chunks)
    for c in order:
        rows = slice(c * CHUNK, (c + 1) * CHUNK)
        st = st_ref[...]
        o_ref[rows, cols] = o_intra[rows] + _dot_nt(qd[rows], st)
        st_ref[...] = st * jnp.exp(cum[chunk_last[c]:chunk_last[c] + 1, :]) + n_ts[c]
        yield


def _hgrn_body(qf_ref, ff_ref, vf_ref, qb_ref, fb_ref, vb_ref, lb_ref, of_ref, ob_ref, sf_ref, sb_ref, stage_ref,
               diag_ref, *, key_dim):
    heads_per_step = sf_ref.shape[0]

    @pl.when(pl.program_id(2) == 0)
    def _():
        sf_ref[...] = jnp.zeros_like(sf_ref)
        sb_ref[...] = jnp.zeros_like(sb_ref)

    chains = []
    for g in range(heads_per_step):
        cols = slice(g * key_dim, (g + 1) * key_dim)
        lb = lb_ref[:, cols]
        chains.append(_hgrn_chain(qf_ref, ff_ref, vf_ref, cols, lb, sf_ref.at[g], of_ref,
                                  stage_ref.at[2 * g], diag_ref.at[2 * g], False))
        chains.append(_hgrn_chain(qb_ref, fb_ref, vb_ref, cols, lb, sb_ref.at[g], ob_ref,
                                  stage_ref.at[2 * g + 1], diag_ref.at[2 * g + 1], True))
    _run_interleaved(chains)


def hgrn_scan(proj_hg, lb, geom, ctx_len, n_heads, key_dim, heads_per_step=HGRN_HEADS_PER_STEP):
    n_lat, lat_len, n_batch = geom
    t = proj_hg.shape[0]
    tl = SCAN_TILE
    nl, nc, ctx0 = lat_len // tl, ctx_len // tl, n_lat // tl
    n_hg = n_heads // heads_per_step
    gw = heads_per_step * key_dim
    tile = functools.partial(_segment_tile, nl=nl, nc=nc, ctx_tile0=ctx0)
    specs = []
    for reverse in (False, True):
        tix = functools.partial(tile, reverse=reverse)
        fcol = (1 + int(reverse)) * n_hg
        specs += [pl.BlockSpec((tl, gw), lambda b, h, s, tix=tix: (tix(b, s), h)),
                  pl.BlockSpec((tl, gw), lambda b, h, s, tix=tix, fcol=fcol: (tix(b, s), fcol + h)),
                  pl.BlockSpec((tl, gw), lambda b, h, s, tix=tix: (tix(b, s), 3 * n_hg + h))]
    specs.append(pl.BlockSpec((1, gw), lambda b, h, s: (0, h)))
    out_specs = [pl.BlockSpec((tl, gw), lambda b, h, s, tix=functools.partial(tile, reverse=r): (tix(b, s), h))
                 for r in (False, True)]
    return pl.pallas_call(
        functools.partial(_hgrn_body, key_dim=key_dim), grid=(n_batch, n_hg, nl + nc),
        in_specs=specs, out_specs=out_specs,
        out_shape=[jax.ShapeDtypeStruct((t, n_heads * key_dim), F32)] * 2,
        scratch_shapes=[pltpu.VMEM((heads_per_step, key_dim, key_dim), F32)] * 2
        + [pltpu.VMEM((2 * heads_per_step, 4, tl, key_dim), F32), pltpu.VMEM((2 * heads_per_step, tl, key_dim), F32)],
        compiler_params=_params("arbitrary", "arbitrary", "arbitrary"), name="hgrn_scan",
    )(proj_hg, proj_hg, proj_hg, proj_hg, proj_hg, proj_hg, lb)


def _mix_out_body(df_ref, db_ref, hf_ref, hb_ref, z_ref, og_ref, dnw_ref, hgw_ref, o_ref, *, head_dim):
    def normed(o, nw):
        outs = []
        for h in range(o.shape[1] // head_dim):
            oh = o[:, h * head_dim:(h + 1) * head_dim]
            outs.append(oh * lax.rsqrt(jnp.mean(oh * oh, axis=-1, keepdims=True) + NORM_EPS) * nw)
        return jnp.concatenate(outs, axis=1)

    dn = normed(df_ref[...] + db_ref[...], dnw_ref[...]) * _silu(z_ref[...])
    hg = normed(hf_ref[...] + hb_ref[...], hgw_ref[...]) * _sigmoid(og_ref[...])
    half = dn.shape[1]
    o_ref[:, :half] = dn.astype(o_ref.dtype)
    o_ref[:, half:] = hg.astype(o_ref.dtype)


def mix_out(dn_f, dn_b, hg_f, hg_b, z_src, z_blk, og_src, og_blk, dn_norm, hg_norm, head_dim):
    t, w = dn_f.shape
    tl = SCAN_TILE
    row = lambda i: (i, 0)
    body = functools.partial(_mix_out_body, head_dim=head_dim)
    return pl.pallas_call(
        body, grid=(t // tl,),
        in_specs=[pl.BlockSpec((tl, w), row)] * 4
        + [pl.BlockSpec((tl, w), lambda i: (i, z_blk)), pl.BlockSpec((tl, w), lambda i: (i, og_blk)),
           pl.BlockSpec((1, head_dim), lambda i: (0, 0)), pl.BlockSpec((1, head_dim), lambda i: (0, 0))],
        out_specs=pl.BlockSpec((tl, 2 * w), row),
        out_shape=jax.ShapeDtypeStruct((t, 2 * w), BF16),
        compiler_params=_params("arbitrary"), name="mix_out",
    )(dn_f, dn_b, hg_f, hg_b, z_src, og_src, dn_norm.reshape(1, head_dim), hg_norm.reshape(1, head_dim))


def _first_max(x, ids, n):
    m = jnp.max(x, axis=0, keepdims=True)
    first = jnp.min(jnp.where(x == m, ids, n), axis=0, keepdims=True)
    return m, first


def _route_body(lg_ref, bias_ref, idx_ref, rank_ref, w_ref, cnt_ref, carry_ref, *,
                n_groups, topk_groups, top_k, scale):
    i = pl.program_id(0)

    @pl.when(i == 0)
    def _():
        carry_ref[...] = jnp.zeros_like(carry_ref)

    lg = lg_ref[...]
    n_exp, tl = lg.shape
    per = n_exp // n_groups
    scores = 1.0 / (1.0 + jnp.exp(-lg))
    biased = scores + bias_ref[...]
    sub = lax.broadcasted_iota(jnp.int32, (per, tl), 0)
    g_rows = []
    for g in range(n_groups):
        xg = biased[g * per:(g + 1) * per]
        m1, i1 = _first_max(xg, sub, per)
        m2 = jnp.max(jnp.where(sub == i1, NEG_INF, xg), axis=0, keepdims=True)
        g_rows.append(m1 + m2)
    gscore = jnp.concatenate(g_rows, axis=0)
    gid = lax.broadcasted_iota(jnp.int32, (n_groups, tl), 0)
    gsel = jnp.zeros((n_groups, tl), jnp.bool_)
    for _ in range(topk_groups):
        _, first = _first_max(gscore, gid, n_groups)
        hit = gid == first
        gsel = gsel | hit
        gscore = jnp.where(hit, NEG_INF, gscore)
    eid = lax.broadcasted_iota(jnp.int32, (n_exp, tl), 0)
    gmask = jnp.concatenate([jnp.broadcast_to(gsel[g:g + 1], (per, tl)) for g in range(n_groups)], axis=0)
    masked = jnp.where(gmask, biased, NEG_INF)
    sel = jnp.zeros((n_exp, tl), jnp.bool_)
    hits, firsts = [], []
    for _ in range(top_k):
        _, first = _first_max(masked, eid, n_exp)
        hit = eid == first
        hits.append(hit)
        firsts.append(first)
        sel = sel | hit
        masked = jnp.where(hit, NEG_INF, masked)
    self = jnp.where(sel, 1.0, 0.0)
    ti = lax.broadcasted_iota(jnp.int32, (tl, tl), 0)
    tj = lax.broadcasted_iota(jnp.int32, (tl, tl), 1)
    before = jnp.where(ti < tj, 1.0, 0.0).astype(BF16)
    carry = carry_ref[...]
    rank_full = jnp.dot(self.astype(BF16), before, preferred_element_type=F32) + carry
    w_rows = [jnp.sum(jnp.where(hit, scores, 0.0), axis=0, keepdims=True) for hit in hits]
    r_rows = [jnp.sum(jnp.where(hit, rank_full, 0.0), axis=0, keepdims=True) for hit in hits]
    w8 = jnp.concatenate(w_rows, axis=0)
    idx_ref[...] = jnp.concatenate(firsts, axis=0)
    rank_ref[...] = jnp.concatenate(r_rows, axis=0).astype(jnp.int32)
    w_ref[...] = w8 / jnp.sum(w8, axis=0, keepdims=True) * scale
    carry = carry + jnp.sum(self, axis=1, keepdims=True)
    carry_ref[...] = carry
    cnt_ref[...] = carry.astype(jnp.int32)


def route(logits_t, router_b, tl=256):
    n_exp, t = logits_t.shape
    body = functools.partial(_route_body, n_groups=N_GROUPS, topk_groups=TOPK_GROUPS, top_k=TOP_K,
                             scale=ROUTED_SCALE)
    tok_spec = pl.BlockSpec((TOP_K, tl), lambda i: (0, i))
    return pl.pallas_call(
        body, grid=(t // tl,),
        in_specs=[pl.BlockSpec((n_exp, tl), lambda i: (0, i)), pl.BlockSpec((n_exp, 1), lambda i: (0, 0))],
        out_specs=[tok_spec, tok_spec, tok_spec, pl.BlockSpec((n_exp, 1), lambda i: (0, 0))],
        out_shape=[jax.ShapeDtypeStruct((TOP_K, t), jnp.int32), jax.ShapeDtypeStruct((TOP_K, t), jnp.int32),
                   jax.ShapeDtypeStruct((TOP_K, t), F32), jax.ShapeDtypeStruct((n_exp, 1), jnp.int32)],
        scratch_shapes=[pltpu.VMEM((n_exp, 1), F32)],
        compiler_params=_params("arbitrary"), name="route",
    )(logits_t, router_b.reshape(n_exp, 1))


def _slab_copy(src_ref, src_tok, dst_ref, dst_tok, rows, sem):
    return pltpu.make_async_copy(src_ref.at[pl.ds(pl.multiple_of(src_tok * rows, rows), rows)],
                                 dst_ref.at[pl.ds(pl.multiple_of(dst_tok * rows, rows), rows)], sem)


def _dispatch_body(dest_ref, pad_ref, h_ref, xs_ref, zero_ref, sem, *, top_k, rows):
    tl = h_ref.shape[0] // rows
    n_exp = pad_ref.shape[1]

    def issue(r, carry):
        for k in range(top_k):
            _slab_copy(h_ref, r, xs_ref, dest_ref[r * top_k + k], rows, sem.at[0]).start(priority=k % 2)
        return carry

    lax.fori_loop(0, tl, issue, 0, unroll=DMA_LOOP_UNROLL)

    @pl.when(pl.program_id(0) == 0)
    def _():
        zero_ref[...] = jnp.zeros_like(zero_ref)
        largest = zero_ref.shape[0] // rows
        pieces = [largest >> b for b in range(largest.bit_length())]

        def pad_copies(e, wait):
            first, n_pad = pad_ref[0, e], pad_ref[1, e]
            for size in pieces:
                @pl.when((n_pad & size) != 0)
                def _():
                    slot0 = first + (n_pad & ~(2 * size - 1))
                    cp = pltpu.make_async_copy(
                        zero_ref.at[pl.ds(0, size * rows)],
                        xs_ref.at[pl.ds(pl.multiple_of(slot0 * rows, rows), size * rows)], sem.at[1])
                    if wait:
                        cp.wait()
                    else:
                        cp.start()

        def fill(e, carry):
            pad_copies(e, False)
            return carry

        def drain_pad(e, carry):
            pad_copies(e, True)
            return carry

        lax.fori_loop(0, n_exp, fill, 0)
        lax.fori_loop(0, n_exp, drain_pad, 0)

    def drain(r, carry):
        for k in range(top_k):
            _slab_copy(h_ref, 0, xs_ref, 0, rows, sem.at[0]).wait()
        return carry

    lax.fori_loop(0, tl, drain, 0, unroll=DMA_LOOP_UNROLL)


def dispatch(h_slab, dest_flat, pad_info, n_slot, top_k, rows, tl=256):
    t = h_slab.shape[0] // rows
    return pl.pallas_call(
        functools.partial(_dispatch_body, top_k=top_k, rows=rows), grid=(t // tl,),
        in_specs=[pl.BlockSpec((tl * top_k,), lambda i: (i,), memory_space=pltpu.SMEM),
                  pl.BlockSpec(memory_space=pltpu.SMEM),
                  pl.BlockSpec((tl * rows, LANES), lambda i: (i, 0))],
        out_specs=pl.BlockSpec(memory_space=pl.ANY),
        out_shape=jax.ShapeDtypeStruct((n_slot * rows, LANES), h_slab.dtype),
        scratch_shapes=[pltpu.VMEM((EXPERT_SLOT_BLOCK // 2 * rows, LANES), h_slab.dtype),
                        pltpu.SemaphoreType.DMA((2,))],
        compiler_params=_params("arbitrary"), name="dispatch",
    )(dest_flat, pad_info, h_slab)


def _combine_body(dest_ref, dest_next_ref, w_ref, sh_ref, x_ref, gate_ref, y_ref, o_ref, buf_ref, acc_ref, sem, *,
                  top_k, rows):
    i = pl.program_id(0)
    n = pl.num_programs(0)
    tl = x_ref.shape[0]

    def gather(d_ref, slot):
        def issue(r, carry):
            for k in range(top_k):
                _slab_copy(y_ref, d_ref[r * top_k + k], buf_ref.at[slot], k * tl + r, rows,
                           sem.at[slot]).start(priority=k % 2)
            return carry
        lax.fori_loop(0, tl, issue, 0, unroll=DMA_LOOP_UNROLL)

    @pl.when(i == 0)
    def _():
        gather(dest_ref, 0)

    @pl.when(i + 1 < n)
    def _():
        gather(dest_next_ref, (i + 1) % 2)

    slot = i % 2

    def drain(r, carry):
        for k in range(top_k):
            _slab_copy(y_ref, 0, buf_ref.at[slot], 0, rows, sem.at[slot]).wait()
        return carry

    lax.fori_loop(0, tl, drain, 0, unroll=DMA_LOOP_UNROLL)
    def unpack_f32(pk):
        return (lax.bitcast_convert_type(pk & jnp.uint32(0xFFFF0000), F32),
                lax.bitcast_convert_type(pk << jnp.uint32(16), F32))

    w = w_ref[...]
    acc_hi, acc_lo = unpack_f32(sh_ref[...])
    for k in range(top_k):
        wk = jnp.broadcast_to(w[:, k:k + 1], (tl, LANES))
        wk = jnp.broadcast_to(wk[:, None, :], (tl, rows, LANES)).reshape(tl * rows, LANES)
        y_hi, y_lo = unpack_f32(buf_ref[slot, pl.ds(k * tl * rows, tl * rows), :])
        acc_hi = acc_hi + y_hi * wk
        acc_lo = acc_lo + y_lo * wk
    acc_ref[0] = acc_hi
    acc_ref[1] = acc_lo
    for half in range(2):
        for c in range(rows):
            cols = slice((half * rows + c) * LANES, (half * rows + c + 1) * LANES)
            o_ref[:, cols] = (x_ref[:, cols]
                              + gate_ref[0][:, cols] * acc_ref.at[half][pl.ds(c, tl, stride=rows), :])


def combine(y_slab, dest_flat, w, shared_slab, x, mod3, gate_idx, geom, tl=COMBINE_TILE):
    t, d = x.shape
    rows = d // 2 // LANES
    top_k = w.shape[1]
    n_lat, lat_len, n_batch = geom
    n_tiles = t // tl
    row = functools.partial(_mod_row, tile_rows=tl, n_lat=n_lat, lat_len=lat_len, n_batch=n_batch)
    return pl.pallas_call(
        functools.partial(_combine_body, top_k=top_k, rows=rows), grid=(n_tiles,),
        in_specs=[pl.BlockSpec((tl * top_k,), lambda i: (i,), memory_space=pltpu.SMEM),
                  pl.BlockSpec((tl * top_k,), lambda i: (jnp.minimum(i + 1, n_tiles - 1),),
                               memory_space=pltpu.SMEM),
                  pl.BlockSpec((tl, top_k), lambda i: (i, 0)),
                  pl.BlockSpec((tl * rows, LANES), lambda i: (i, 0)),
                  pl.BlockSpec((tl, d), lambda i: (i, 0)),
                  pl.BlockSpec((1, 1, d), lambda i: (row(i) * 6 + gate_idx, 0, 0)),
                  pl.BlockSpec(memory_space=pl.ANY)],
        out_specs=pl.BlockSpec((tl, d), lambda i: (i, 0)),
        out_shape=jax.ShapeDtypeStruct((t, d), F32),
        scratch_shapes=[pltpu.VMEM((2, top_k * tl * rows, LANES), jnp.uint32),
                        pltpu.VMEM((2, tl * rows, LANES), F32), pltpu.SemaphoreType.DMA((2,))],
        compiler_params=_params("arbitrary"), name="combine",
    )(dest_flat, dest_flat, w, shared_slab, x, mod3, y_slab)


def moe_ffn(x, h_pk, logits_t, router_b, exp_gate_up, exp_down, shared_gate_up, shared_down, mod3, geom):
    t = x.shape[0]
    slab_rows = h_pk.shape[0] // t
    e_count = exp_gate_up.shape[0]
    blk = EXPERT_SLOT_BLOCK
    idx8, rank8, w8, counts = route(logits_t, router_b)
    counts = counts[:, 0]
    padded = (counts + blk - 1) // blk * blk
    pad_end = jnp.cumsum(padded)
    pad_start = pad_end - padded
    start8 = jnp.sum(jnp.where(idx8[None] == jnp.arange(e_count, dtype=jnp.int32)[:, None, None],
                               pad_start[:, None, None], 0), axis=0)
    dest_flat = (start8 + rank8).T.reshape(-1)
    n_blk = (t * TOP_K + e_count * (blk - 1)) // blk + 1
    n_blk = -(-n_blk // EXPERT_BLOCKS_PER_STEP) * EXPERT_BLOCKS_PER_STEP
    n_slot = n_blk * blk
    blk_starts = jnp.arange(n_blk, dtype=jnp.int32) * blk
    blk_e = jnp.sum((pad_end[None, :] <= blk_starts[:, None]).astype(jnp.int32), axis=1)
    blk_e = jnp.minimum(blk_e, e_count - 1)
    n_used = (pad_end[-1] // blk).astype(jnp.int32).reshape(1)
    pad_info = jnp.stack([pad_start + counts, padded - counts]).astype(jnp.int32)
    x_sorted = dispatch(h_pk, dest_flat, pad_info, n_slot, TOP_K, slab_rows)
    y_slot = expert_blocks(x_sorted, exp_gate_up, exp_down, blk_e, n_used, blk)
    shared = expert_blocks(h_pk, shared_gate_up[None], shared_down[None], jnp.zeros((t // blk,), jnp.int32),
                           jnp.full((1,), t // blk, jnp.int32), blk)
    return combine(y_slot, dest_flat, w8.T, shared, x, mod3, 5, geom)


def kernel(x, c, ctx, c_ctx, hg_lb_logits, l0_mod_w, l0_mod_b, l0_norm1, l0_norm2, l0_w_in, l0_dn_conv, l0_dn_a_log, l0_dn_dt_bias, l0_dn_norm, l0_hg_norm, l0_w_out, l0_router_w, l0_router_b, l0_exp_gate_up, l0_exp_down, l0_shared_gate_up, l0_shared_down, l1_mod_w, l1_mod_b, l1_norm1, l1_norm2, l1_w_in, l1_q_norm, l1_k_norm, l1_lambda, l1_sub_norm, l1_w_out, l1_router_w, l1_router_b, l1_exp_gate_up, l1_exp_down, l1_shared_gate_up, l1_shared_down):
    n_batch, lat_len, d = x.shape
    ctx_len = ctx.shape[1]
    n_lat = n_batch * lat_len
    n_ctx = n_batch * ctx_len
    geom = (n_lat, lat_len, n_batch)
    geom_lat_only = (n_lat, lat_len, n_batch)

    xs = jnp.concatenate([x.reshape(n_lat, d), ctx.reshape(n_ctx, d)], axis=0)
    cond = jnp.concatenate([c, c_ctx[None], jnp.zeros((MOD_ROWS - n_batch - 1, d), F32)], axis=0)

    mod3 = modulation(cond, l0_mod_w, l0_mod_b).reshape(MOD_ROWS * 6, 1, d)
    h = adaln(xs, l0_norm1, mod3, 0, geom)
    n_dn = IN0_SIZES[0] + IN0_SIZES[1]
    n_gate = IN0_SIZES[2] + IN0_SIZES[3]
    proj_dn = matmul(h, l0_w_in, n_cols=n_dn)
    proj_hg = matmul(h, l0_w_in[:, n_dn + n_gate:])
    gc, gr = gate_prep(h, l0_w_in[:, n_dn:n_dn + n_gate], l0_dn_a_log, l0_dn_dt_bias, DN_HEADS)
    conv_t = jnp.concatenate([l0_dn_conv.T, jnp.zeros((8 - l0_dn_conv.shape[1], 3 * DN_WIDTH), F32)], axis=0)
    qkv = dn_prep(proj_dn, conv_t, geom, ctx_len, DN_WIDTH, DN_HEAD_DIM)
    dn_f, dn_b = delta_scan(qkv, gc, gr, geom, ctx_len, DN_HEADS, DN_HEAD_DIM, inv_passes=DELTA_INV_PASSES)
    lb = jnp.cumsum(jax.nn.softmax(hg_lb_logits, axis=0), axis=0)[0:1]
    hg_f, hg_b = hgrn_scan(proj_hg, lb, geom, ctx_len, HG_HEADS, HG_KEY_DIM)
    y = mix_out(dn_f, dn_b, hg_f, hg_b, proj_dn, 3, proj_hg, 4, l0_dn_norm, l0_hg_norm, DN_HEAD_DIM)
    xs = matmul_residual(y, l0_w_out, xs, mod3, 2, geom)
    h, logits_t = adaln(xs, l0_norm2, mod3, 3, geom, router_w=l0_router_w)
    xs = moe_ffn(xs, h, logits_t, l0_router_b, l0_exp_gate_up, l0_exp_down, l0_shared_gate_up, l0_shared_down,
                 mod3, geom)

    mod3 = modulation(cond, l1_mod_w, l1_mod_b).reshape(MOD_ROWS * 6, 1, d)
    h = adaln(xs, l1_norm1, mod3, 0, geom)
    cos_t, sin_t = rope_tables(lat_len, n_batch, n_ctx, DA_HEAD_DIM)
    qkv = matmul_qkv(h, l1_w_in, l1_q_norm, l1_k_norm, cos_t, sin_t)
    lam_init = 0.8 - 0.6 * math.exp(-0.3 * 1)
    lmbda = (jnp.exp(jnp.sum(l1_lambda[0] * l1_lambda[1])) - jnp.exp(jnp.sum(l1_lambda[2] * l1_lambda[3]))
             + lam_init)
    y = diff_attention(qkv, lmbda, l1_sub_norm, n_batch, lat_len, ctx_len, DA_HEADS, DA_HEAD_DIM,
                       1.0 - lam_init)
    xl = matmul_residual(y, l1_w_out, xs, mod3, 2, geom_lat_only)
    h, logits_t = adaln(xl, l1_norm2, mod3, 3, geom_lat_only, router_w=l1_router_w)
    xl = moe_ffn(xl, h, logits_t, l1_router_b, l1_exp_gate_up, l1_exp_down, l1_shared_gate_up, l1_shared_down,
                 mod3, geom_lat_only)
    return xl.reshape(n_batch, lat_len, d)
```

```python
import functools
import math

import jax
import jax.numpy as jnp
from jax import lax
from jax.experimental import pallas as pl
from jax.experimental.pallas import tpu as pltpu

F32 = jnp.float32
BF16 = jnp.bfloat16

NORM_EPS = 1e-6
GRID_W = 64
ROPE_BASE = 10000.0

DN_HEADS = 8
DN_HEAD_DIM = 128
DN_WIDTH = DN_HEADS * DN_HEAD_DIM
HG_HEADS = 8
HG_KEY_DIM = 128
HG_K_WIDTH = HG_HEADS * HG_KEY_DIM
HG_V_WIDTH = HG_K_WIDTH
IN0_SIZES = (3 * DN_WIDTH, DN_WIDTH, 2 * DN_HEADS, 2 * DN_HEADS,
             HG_K_WIDTH, 2 * HG_K_WIDTH, HG_V_WIDTH, HG_V_WIDTH)
DA_HEADS = 8
DA_HEAD_DIM = 128
N_EXPERTS = 64
TOP_K = 8
N_GROUPS = 8
TOPK_GROUPS = 4
ROUTED_SCALE = 2.5

VMEM_LIMIT_BYTES = 56 * 1024 * 1024
LANES = 128
MOD_ROWS = 8
ROW_TILE = 512
MM_TILE_M = 1024
MM_TILE_N = 1024
MM_EPILOGUE_CHUNKS = 4
ATTN_TILE_Q = 1024
ATTN_ROW_CHUNK = 128
EXPERT_SLOT_BLOCK = 256
EXPERT_BLOCKS_PER_STEP = 4
DMA_LOOP_UNROLL = 8
COMBINE_TILE = 128


def _params(*sem):
    return pltpu.CompilerParams(dimension_semantics=sem, vmem_limit_bytes=VMEM_LIMIT_BYTES)


def _sigmoid(x):
    return 0.5 * jnp.tanh(0.5 * x) + 0.5


def _silu(x):
    return x * _sigmoid(x)


def _mod_body(c_ref, w_ref, b_ref, o_ref):
    a = _silu(c_ref[...]).astype(BF16)
    o_ref[...] = jnp.dot(a, w_ref[...].astype(BF16), preferred_element_type=F32) + b_ref[...]


def modulation(cond, w, b, tn=1024):
    m, k = cond.shape
    n = w.shape[1]
    return pl.pallas_call(
        _mod_body, grid=(n // tn,),
        in_specs=[pl.BlockSpec((m, k), lambda j: (0, 0)),
                  pl.BlockSpec((k, tn), lambda j: (0, j)),
                  pl.BlockSpec((1, tn), lambda j: (0, j))],
        out_specs=pl.BlockSpec((m, tn), lambda j: (0, j)),
        out_shape=jax.ShapeDtypeStruct((m, n), F32),
        compiler_params=_params("arbitrary"), name="modulation",
    )(cond, w, b.reshape(1, n))


def _mod_row(tile, tile_rows, n_lat, lat_len, n_batch):
    assert lat_len % tile_rows == 0 and n_lat % tile_rows == 0, (tile_rows, lat_len, n_lat)
    start = tile * tile_rows
    return jnp.where(start < n_lat, start // lat_len, n_batch)


def _pack_bf16_pairs(h):
    half = h.shape[1] // 2
    bits = lax.bitcast_convert_type(h.astype(BF16).astype(F32), jnp.uint32)
    return (bits[:, :half] & jnp.uint32(0xFFFF0000)) | (bits[:, half:] >> jnp.uint32(16))


def _unpack_bf16_pairs(pk):
    hi = lax.bitcast_convert_type(pk & jnp.uint32(0xFFFF0000), F32).astype(BF16)
    lo = lax.bitcast_convert_type(pk << jnp.uint32(16), F32).astype(BF16)
    return hi, lo


def _store_slabs(ref, x, row0=0):
    r, w = x.shape
    c_n = w // LANES
    for c in range(c_n):
        ref[pl.ds(row0 * c_n + c, r, stride=c_n), :] = x[:, c * LANES:(c + 1) * LANES]


def _adaln_body(x_ref, nw_ref, shift_ref, scale_ref, o_ref):
    x = x_ref[...]
    y = x * lax.rsqrt(jnp.mean(x * x, axis=-1, keepdims=True) + NORM_EPS) * nw_ref[...]
    o_ref[...] = (y * (1.0 + scale_ref[0]) + shift_ref[0]).astype(o_ref.dtype)


def _adaln_router_body(x_ref, nw_ref, shift_ref, scale_ref, rw_ref, o_ref, lg_ref):
    x = x_ref[...]
    y = x * lax.rsqrt(jnp.mean(x * x, axis=-1, keepdims=True) + NORM_EPS) * nw_ref[...]
    h = y * (1.0 + scale_ref[0]) + shift_ref[0]
    _store_slabs(o_ref, _pack_bf16_pairs(h))
    lg_ref[...] = lax.dot_general(rw_ref[...], h, (((1,), (1,)), ((), ())), preferred_element_type=F32,
                                  precision=lax.Precision.HIGHEST)


def adaln(x, norm_w, mod3, shift_idx, geom, router_w=None, tl=ROW_TILE):
    t, d = x.shape
    n_lat, lat_len, n_batch = geom
    row = functools.partial(_mod_row, tile_rows=tl, n_lat=n_lat, lat_len=lat_len, n_batch=n_batch)
    in_specs = [pl.BlockSpec((tl, d), lambda i: (i, 0)),
                pl.BlockSpec((1, d), lambda i: (0, 0)),
                pl.BlockSpec((1, 1, d), lambda i: (row(i) * 6 + shift_idx, 0, 0)),
                pl.BlockSpec((1, 1, d), lambda i: (row(i) * 6 + shift_idx + 1, 0, 0))]
    args = [x, norm_w.reshape(1, d), mod3, mod3]
    if router_w is None:
        return pl.pallas_call(
            _adaln_body, grid=(t // tl,), in_specs=in_specs,
            out_specs=pl.BlockSpec((tl, d), lambda i: (i, 0)),
            out_shape=jax.ShapeDtypeStruct((t, d), BF16),
            compiler_params=_params("arbitrary"), name="adaln")(*args)
    e = router_w.shape[1]
    return pl.pallas_call(
        _adaln_router_body, grid=(t // tl,),
        in_specs=in_specs + [pl.BlockSpec((e, d), lambda i: (0, 0))],
        out_specs=[pl.BlockSpec((tl * (d // 2 // LANES), LANES), lambda i: (i, 0)),
                   pl.BlockSpec((e, tl), lambda i: (0, i))],
        out_shape=[jax.ShapeDtypeStruct((t * (d // 2 // LANES), LANES), jnp.uint32),
                   jax.ShapeDtypeStruct((e, t), F32)],
        compiler_params=_params("arbitrary"), name="adaln_router")(*args, router_w.T)


def _mm_body(a_ref, w_ref, o_ref, wb_ref):
    @pl.when(pl.program_id(1) == 0)
    def _():
        wb_ref[...] = w_ref[...].astype(BF16)
    o_ref[...] = jnp.dot(a_ref[...], wb_ref[...], preferred_element_type=F32).astype(o_ref.dtype)


def _mm_res_body(a_ref, w_ref, res_ref, gate_ref, o_ref, wb_ref):
    @pl.when(pl.program_id(1) == 0)
    def _():
        wb_ref[...] = w_ref[...].astype(BF16)
    acc = jnp.dot(a_ref[...], wb_ref[...], preferred_element_type=F32)
    o_ref[...] = res_ref[...] + gate_ref[0] * acc


def matmul(a, w, n_cols=None, col_block0=0, tm=MM_TILE_M, tn=MM_TILE_N, out_dtype=F32):
    m, k = a.shape
    n = w.shape[1] if n_cols is None else n_cols
    return pl.pallas_call(
        _mm_body, grid=(n // tn, m // tm),
        in_specs=[pl.BlockSpec((tm, k), lambda j, i: (i, 0)),
                  pl.BlockSpec((k, tn), lambda j, i: (0, j + col_block0))],
        out_specs=pl.BlockSpec((tm, tn), lambda j, i: (i, j)),
        out_shape=jax.ShapeDtypeStruct((m, n), out_dtype),
        scratch_shapes=[pltpu.VMEM((k, tn), BF16)],
        compiler_params=_params("arbitrary", "arbitrary"), name="matmul",
    )(a, w)


def matmul_residual(a, w, res, mod3, gate_idx, geom, tm=MM_TILE_M, tn=MM_TILE_N):
    m, k = a.shape
    n = w.shape[1]
    n_lat, lat_len, n_batch = geom
    row = functools.partial(_mod_row, tile_rows=tm, n_lat=n_lat, lat_len=lat_len, n_batch=n_batch)
    nb = n // tn
    return pl.pallas_call(
        _mm_res_body, grid=(n // tn, m // tm),
        in_specs=[pl.BlockSpec((tm, k), lambda j, i: (i, 0)),
                  pl.BlockSpec((k, tn), lambda j, i: (0, j)),
                  pl.BlockSpec((tm, tn), lambda j, i: (i, j)),
                  pl.BlockSpec((1, 1, tn), lambda j, i: (row(i) * 6 + gate_idx, 0, j))],
        out_specs=pl.BlockSpec((tm, tn), lambda j, i: (i, j)),
        out_shape=jax.ShapeDtypeStruct((m, n), F32),
        scratch_shapes=[pltpu.VMEM((k, tn), BF16)],
        compiler_params=_params("arbitrary", "arbitrary"), name="matmul_residual",
    )(a, w, res, mod3)


def _expert_body(blk_e_ref, blk_new_ref, next_e_ref, stage_ref, n_used_ref, x_ref, gu_hbm, dn_hbm, o_ref,
                 gu_stage, dn_stage, gub_ref, dnb_ref, sem):
    ff, d = dnb_ref.shape
    half = d // 2
    xc = half // LANES
    blk = x_ref.shape[0] // xc // EXPERT_BLOCKS_PER_STEP

    def weight_copies(e, slot):
        return (pltpu.make_async_copy(gu_hbm.at[e], gu_stage.at[slot], sem.at[0, slot]),
                pltpu.make_async_copy(dn_hbm.at[e], dn_stage.at[slot], sem.at[1, slot]))

    @pl.when(pl.program_id(0) == 0)
    def _():
        for cp in weight_copies(blk_e_ref[0], 0):
            cp.start()

    def one_block(b, sub):
        @pl.when(blk_new_ref[b] == 1)
        def _():
            slot = stage_ref[b]
            for cp in weight_copies(blk_e_ref[b], slot):
                cp.wait()
            gub_ref[...] = gu_stage[slot].astype(BF16)
            dnb_ref[...] = dn_stage[slot].astype(BF16)

            @pl.when(next_e_ref[b] >= 0)
            def _():
                for cp in weight_copies(next_e_ref[b], 1 - slot):
                    cp.start()

        @pl.when(b < n_used_ref[0])
        def _():
            h1 = None
            for p in range(2):
                cs = range(p * xc // 2, (p + 1) * xc // 2)
                pk = jnp.concatenate([x_ref[pl.ds(sub * blk * xc + c, blk, stride=xc), :] for c in cs], axis=1)
                x_hi, x_lo = _unpack_bf16_pairs(pk)
                k0 = p * half // 2
                part = (jnp.dot(x_hi, gub_ref[k0:k0 + half // 2], preferred_element_type=F32)
                        + jnp.dot(x_lo, gub_ref[half + k0:half + k0 + half // 2], preferred_element_type=F32))
                h1 = part if h1 is None else h1 + part
            act = (_silu(h1[:, :ff]) * h1[:, ff:]).astype(BF16)
            y = jnp.concatenate([jnp.dot(act, dnb_ref[:, p * half:(p + 1) * half], preferred_element_type=F32)
                                 for p in range(2)], axis=1)
            _store_slabs(o_ref, _pack_bf16_pairs(y), sub * blk)

        @pl.when(b >= n_used_ref[0])
        def _():
            o_ref[pl.ds(sub * blk * xc, blk * xc), :] = jnp.zeros((blk * xc, LANES), o_ref.dtype)

    for sub in range(EXPERT_BLOCKS_PER_STEP):
        one_block(pl.program_id(0) * EXPERT_BLOCKS_PER_STEP + sub, sub)


def expert_blocks(x, gate_up, down, blk_e, n_used, blk):
    _, d, f2 = gate_up.shape
    xc = yc = d // 2 // LANES
    s = x.shape[0] // xc
    n_blk = s // blk
    assert n_blk % EXPERT_BLOCKS_PER_STEP == 0, (n_blk, EXPERT_BLOCKS_PER_STEP)
    pos = jnp.arange(n_blk, dtype=jnp.int32)
    used = pos < n_used[0]
    blk_new = (jnp.concatenate([jnp.ones((1,), bool), blk_e[1:] != blk_e[:-1]]) & used).astype(jnp.int32)
    stage = (jnp.cumsum(blk_new) - 1) % 2
    first_pos = jnp.where(blk_new == 1, pos, n_blk)
    next_first = jnp.concatenate([lax.cummin(first_pos, reverse=True)[1:], jnp.full((1,), n_blk, jnp.int32)])
    next_e = jnp.where(next_first < n_blk, blk_e[jnp.minimum(next_first, n_blk - 1)], -1).astype(jnp.int32)
    bps = EXPERT_BLOCKS_PER_STEP
    grid_spec = pltpu.PrefetchScalarGridSpec(
        num_scalar_prefetch=5, grid=(n_blk // bps,),
        in_specs=[pl.BlockSpec((bps * blk * xc, LANES),
                               lambda i, be, bn, ne, st, nu: (jnp.minimum(i, (nu[0] - 1) // bps), 0)),
                  pl.BlockSpec(memory_space=pl.ANY),
                  pl.BlockSpec(memory_space=pl.ANY)],
        out_specs=pl.BlockSpec((bps * blk * yc, LANES), lambda i, be, bn, ne, st, nu: (i, 0)),
        scratch_shapes=[pltpu.VMEM((2, d, f2), F32), pltpu.VMEM((2, f2 // 2, d), F32),
                        pltpu.VMEM((d, f2), BF16), pltpu.VMEM((f2 // 2, d), BF16),
                        pltpu.SemaphoreType.DMA((2, 2))])
    return pl.pallas_call(
        _expert_body, grid_spec=grid_spec,
        out_shape=jax.ShapeDtypeStruct((s * yc, LANES), jnp.uint32),
        compiler_params=_params("arbitrary"), name="expert_blocks",
    )(blk_e, blk_new, next_e, stage.astype(jnp.int32), n_used, x, gate_up, down)


def _norm_rope(x, w, cos, sin, first, head_dim):
    y = x * lax.rsqrt(jnp.mean(x * x, axis=-1, keepdims=True) + NORM_EPS) * w
    swapped = jnp.where(first, pltpu.roll(y, head_dim - head_dim // 4, 1), pltpu.roll(y, head_dim // 4, 1))
    return y * cos + swapped * sin


def _mm_qkv_body(a_ref, w_ref, qw_ref, kw_ref, cos_ref, sin_ref, o_ref, wb_ref, *, head_dim, n_q_tiles, n_k_tiles):
    j = pl.program_id(0)

    @pl.when(pl.program_id(1) == 0)
    def _():
        wb_ref[...] = w_ref[...].astype(BF16)

    tm, tn = o_ref.shape
    row_chunk = tm // MM_EPILOGUE_CHUNKS

    def rows_chain(rows, nw_ref):
        acc = jnp.dot(a_ref[rows, :], wb_ref[...], preferred_element_type=F32)
        yield
        if nw_ref is None:
            o_ref[rows, :] = acc.astype(o_ref.dtype)
            return
        cos, sin = cos_ref[rows, :], sin_ref[rows, :]
        lane = lax.broadcasted_iota(jnp.int32, cos.shape, 1)
        first = (lane % (head_dim // 2)) < (head_dim // 4)
        for g in range(tn // head_dim):
            sl = slice(g * head_dim, (g + 1) * head_dim)
            o_ref[rows, sl] = _norm_rope(acc[:, sl], nw_ref[...], cos, sin, first, head_dim).astype(o_ref.dtype)
            yield

    def tile(nw_ref):
        _run_interleaved(rows_chain(slice(r, r + row_chunk), nw_ref) for r in range(0, tm, row_chunk))

    @pl.when(j < n_q_tiles)
    def _():
        tile(qw_ref)

    @pl.when((j >= n_q_tiles) & (j < n_q_tiles + n_k_tiles))
    def _():
        tile(kw_ref)

    @pl.when(j >= n_q_tiles + n_k_tiles)
    def _():
        tile(None)


def matmul_qkv(a, w, q_norm, k_norm, cos_t, sin_t, tm=MM_TILE_M, tn=MM_TILE_N):
    m, k = a.shape
    n = w.shape[1]
    hd = q_norm.shape[0]
    body = functools.partial(_mm_qkv_body, head_dim=hd, n_q_tiles=n // 3 // tn, n_k_tiles=n // 3 // tn)
    return pl.pallas_call(
        body, grid=(n // tn, m // tm),
        in_specs=[pl.BlockSpec((tm, k), lambda j, i: (i, 0)),
                  pl.BlockSpec((k, tn), lambda j, i: (0, j)),
                  pl.BlockSpec((1, hd), lambda j, i: (0, 0)),
                  pl.BlockSpec((1, hd), lambda j, i: (0, 0)),
                  pl.BlockSpec((tm, hd), lambda j, i: (i, 0)),
                  pl.BlockSpec((tm, hd), lambda j, i: (i, 0))],
        out_specs=pl.BlockSpec((tm, tn), lambda j, i: (i, j)),
        out_shape=jax.ShapeDtypeStruct((m, n), BF16),
        scratch_shapes=[pltpu.VMEM((k, tn), BF16)],
        compiler_params=_params("arbitrary", "arbitrary"), name="matmul_qkv",
    )(a, w, q_norm.reshape(1, hd), k_norm.reshape(1, hd), cos_t, sin_t)


def rope_tables(n_lat_tokens_per_sample, n_batch, n_ctx_tokens, head_dim):
    quarter = head_dim // 4
    inv_freq = ROPE_BASE ** (-jnp.arange(quarter, dtype=F32) / quarter)
    rows = n_lat_tokens_per_sample // GRID_W
    row = jnp.repeat(jnp.arange(rows, dtype=F32), GRID_W)
    col = jnp.tile(jnp.arange(GRID_W, dtype=F32), rows)
    ang_r = row[:, None] * inv_freq[None, :]
    ang_c = col[:, None] * inv_freq[None, :]
    cos = jnp.concatenate([jnp.cos(ang_r), jnp.cos(ang_r), jnp.cos(ang_c), jnp.cos(ang_c)], axis=-1)
    sin = jnp.concatenate([-jnp.sin(ang_r), jnp.sin(ang_r), -jnp.sin(ang_c), jnp.sin(ang_c)], axis=-1)
    cos = jnp.concatenate([jnp.tile(cos, (n_batch, 1)), jnp.ones((n_ctx_tokens, head_dim), F32)], axis=0)
    sin = jnp.concatenate([jnp.tile(sin, (n_batch, 1)), jnp.zeros((n_ctx_tokens, head_dim), F32)], axis=0)
    return cos, sin


def _diff_attn_rows(lam, q_ref, k_all, v_all, sw_ref, o_ref, rows, head_dim, out_scale):
    c = head_dim ** -0.5 * math.log2(math.e)
    es, invs = [], []
    for s in range(2):
        sl = slice(s * head_dim, (s + 1) * head_dim)
        sc = lax.dot_general(q_ref[rows, sl], k_all[:, sl], (((1,), (1,)), ((), ())), preferred_element_type=F32)
        yield
        e = jnp.exp2((sc - jnp.max(sc, axis=-1, keepdims=True)) * c)
        invs.append(1.0 / jnp.sum(e, axis=-1, keepdims=True))
        es.append(e.astype(BF16))
        yield
    v = v_all[...]
    o0 = jnp.dot(es[0], v, preferred_element_type=F32)
    o1 = jnp.dot(es[1], v, preferred_element_type=F32)
    yield
    o = o0 * invs[0] - (lam * invs[1]) * o1
    y = o * lax.rsqrt(jnp.mean(o * o, axis=-1, keepdims=True) + NORM_EPS) * sw_ref[...]
    o_ref[rows, :] = (y * out_scale).astype(o_ref.dtype)
    yield


def _diff_attn_body(lam_ref, q_ref, kl_ref, kc_ref, vl_ref, vc_ref, sw_ref, o_ref, k_all, v_all, *,
                    head_dim, out_scale, row_chunk):
    @pl.when(pl.program_id(2) == 0)
    def _():
        n_l = kl_ref.shape[0]
        k_all[:n_l] = kl_ref[...]
        k_all[n_l:] = kc_ref[...]
        v_all[:n_l] = vl_ref[...]
        v_all[n_l:] = vc_ref[...]

    lam = lam_ref[0]
    tq = q_ref.shape[0]
    _run_interleaved(
        _diff_attn_rows(lam, q_ref, k_all, v_all, sw_ref, o_ref, slice(r, r + row_chunk), head_dim, out_scale)
        for r in range(0, tq, row_chunk))


def diff_attention(qkv, lmbda, sub_norm, n_batch, lat_len, ctx_len, n_heads, head_dim, out_scale,
                   tq=ATTN_TILE_Q, row_chunk=ATTN_ROW_CHUNK):
    hw = 2 * head_dim
    nq = lat_len // tq
    ctx_blk0 = n_batch * lat_len // ctx_len
    body = functools.partial(_diff_attn_body, head_dim=head_dim, out_scale=out_scale, row_chunk=row_chunk)
    n_keys = lat_len + ctx_len
    return pl.pallas_call(
        body, grid=(n_batch, n_heads, nq),
        in_specs=[pl.BlockSpec(memory_space=pltpu.SMEM),
                  pl.BlockSpec((tq, hw), lambda b, h, i: (b * nq + i, h)),
                  pl.BlockSpec((lat_len, hw), lambda b, h, i: (b, n_heads + h)),
                  pl.BlockSpec((ctx_len, hw), lambda b, h, i: (ctx_blk0 + b, n_heads + h)),
                  pl.BlockSpec((lat_len, hw), lambda b, h, i: (b, 2 * n_heads + h)),
                  pl.BlockSpec((ctx_len, hw), lambda b, h, i: (ctx_blk0 + b, 2 * n_heads + h)),
                  pl.BlockSpec((1, hw), lambda b, h, i: (0, 0))],
        out_specs=pl.BlockSpec((tq, hw), lambda b, h, i: (b * nq + i, h)),
        out_shape=jax.ShapeDtypeStruct((n_batch * lat_len, n_heads * hw), BF16),
        scratch_shapes=[pltpu.VMEM((n_keys, hw), BF16), pltpu.VMEM((n_keys, hw), BF16)],
        compiler_params=_params("arbitrary", "arbitrary", "arbitrary"), name="diff_attention",
    )(lmbda.reshape(1), qkv, qkv, qkv, qkv, qkv, sub_norm.reshape(1, hw))


SCAN_TILE = 256
CHUNK = 64
SUB = 16
NEG_BIG = -1e30
NEG_INF = float("-inf")
DELTA_INV_PASSES = 1
DELTA_HEADS_PER_STEP = 4
HGRN_HEADS_PER_STEP = 4


def _dot(a, b):
    return jnp.dot(a.astype(BF16), b.astype(BF16), preferred_element_type=F32)


def _dot_nt(a, b):
    return lax.dot_general(a.astype(BF16), b.astype(BF16), (((1,), (1,)), ((), ())),
                           preferred_element_type=F32)


def _split3(x):
    hi = x.astype(BF16)
    r = x - hi.astype(F32)
    mid = r.astype(BF16)
    lo = (r - mid.astype(F32)).astype(BF16)
    return hi, mid, lo


def _dot_exact_lhs01(m01, x):
    hi, mid, lo = _split3(x)
    m = m01.astype(BF16)
    return (jnp.dot(m, hi, preferred_element_type=F32) + jnp.dot(m, mid, preferred_element_type=F32)
            + jnp.dot(m, lo, preferred_element_type=F32))


def _dot_exact_rhs01(x, m01):
    hi, mid, lo = _split3(x)
    m = m01.astype(BF16)
    return (jnp.dot(hi, m, preferred_element_type=F32) + jnp.dot(mid, m, preferred_element_type=F32)
            + jnp.dot(lo, m, preferred_element_type=F32))


def _dot3(a, b):
    ah = a.astype(BF16)
    al = (a - ah.astype(F32)).astype(BF16)
    bh = b.astype(BF16)
    bl = (b - bh.astype(F32)).astype(BF16)
    return (jnp.dot(ah, bh, preferred_element_type=F32) + jnp.dot(ah, bl, preferred_element_type=F32)
            + jnp.dot(al, bh, preferred_element_type=F32))


def _tile_masks(n, reverse):
    i = lax.broadcasted_iota(jnp.int32, (n, n), 0)
    j = lax.broadcasted_iota(jnp.int32, (n, n), 1)
    same = (i // CHUNK) == (j // CHUNK)
    if reverse:
        return same & (i <= j), same & (i < j)
    return same & (i >= j), same & (i > j)


def _segment_tile(b, s, reverse, nl, nc, ctx_tile0):
    if reverse:
        return jnp.where(s < nc, ctx_tile0 + b * nc + (nc - 1 - s), b * nl + (nl - 1 - (s - nc)))
    return jnp.where(s < nc, ctx_tile0 + b * nc + s, b * nl + (s - nc))


def _dn_prep_body(x_ref, prev_ref, next_ref, w_ref, o_ref, *, tiles_per_lat_seg, tiles_per_ctx_seg, n_lat_tiles,
                  head_dim, q_scale):
    i = pl.program_id(0)
    j = pl.program_id(1)
    is_lat = i < n_lat_tiles
    pos = jnp.where(is_lat, i % tiles_per_lat_seg, (i - n_lat_tiles) % tiles_per_ctx_seg)
    seg_first = pos == 0
    seg_last = pos == jnp.where(is_lat, tiles_per_lat_seg, tiles_per_ctx_seg) - 1
    x = x_ref[...]
    tl = x.shape[0]
    prev = jnp.where(seg_first, 0.0, prev_ref[...])
    nxt = jnp.where(seg_last, 0.0, next_ref[...])
    xp = jnp.concatenate([prev, x, nxt], axis=0)
    w = w_ref[...]
    n_taps = 5
    acc = None
    for t in range(n_taps):
        off = 8 + t - n_taps // 2
        term = xp[off:off + tl] * w[t:t + 1]
        acc = term if acc is None else acc + term
    y = _silu(acc)
    scale = jnp.where(j == 0, q_scale, 1.0)
    outs = []
    for h in range(y.shape[1] // head_dim):
        yh = y[:, h * head_dim:(h + 1) * head_dim]
        nrm = lax.rsqrt(jnp.sum(yh * yh, axis=-1, keepdims=True) + 1e-6) * scale
        outs.append(yh * jnp.where(j == 2, 1.0, nrm))
    o_ref[...] = jnp.concatenate(outs, axis=1)


def dn_prep(proj_dn, conv_w_t, geom, ctx_len, width, head_dim):
    n_lat, lat_len, n_batch = geom
    t = proj_dn.shape[0]
    tl = SCAN_TILE
    rows8 = tl // 8
    n_tiles = t // tl
    body = functools.partial(_dn_prep_body, tiles_per_lat_seg=lat_len // tl, tiles_per_ctx_seg=ctx_len // tl,
                             n_lat_tiles=n_lat // tl,
                             head_dim=head_dim, q_scale=head_dim ** -0.5)
    last8 = t // 8 - 1
    return pl.pallas_call(
        body, grid=(n_tiles, 3),
        in_specs=[pl.BlockSpec((tl, width), lambda i, j: (i, j)),
                  pl.BlockSpec((8, width), lambda i, j: (jnp.maximum(i * rows8 - 1, 0), j)),
                  pl.BlockSpec((8, width), lambda i, j: (jnp.minimum((i + 1) * rows8, last8), j)),
                  pl.BlockSpec((8, width), lambda i, j: (0, j))],
        out_specs=pl.BlockSpec((tl, width), lambda i, j: (i, j)),
        out_shape=jax.ShapeDtypeStruct((t, 3 * width), F32),
        compiler_params=_params("arbitrary", "arbitrary"), name="dn_prep",
    )(proj_dn, proj_dn, proj_dn, conv_w_t)


def _softplus(x):
    return jnp.maximum(x, 0.0) + jnp.log(1.0 + jnp.exp(-jnp.abs(x)))


def _gate_prep_body(h_ref, wc_ref, wr_ref, alog_c_ref, dtb_c_ref, alog_r_ref, dtb_r_ref, gc_ref, gr_ref, *, n_heads):
    h = h_ref[...]
    tl = h.shape[0]
    nd = 2 * n_heads
    raw_c = jnp.dot(h, wc_ref[...].astype(BF16), preferred_element_type=F32)
    raw_r = lax.dot_general(wr_ref[...].astype(BF16), h, (((1,), (1,)), ((), ())),
                            preferred_element_type=F32)
    g_c = -jnp.exp(alog_c_ref[...]) * _softplus(raw_c[:, :nd] + dtb_c_ref[...])
    g_r = -jnp.exp(alog_r_ref[...]) * _softplus(raw_r[:nd, :] + dtb_r_ref[...])
    incl_f, _ = _tile_masks(tl, False)
    incl_b, _ = _tile_masks(tl, True)
    one_f = jnp.where(incl_f, 1.0, 0.0)
    one_b = jnp.where(incl_b, 1.0, 0.0)
    cum_c = jnp.concatenate([_dot_exact_lhs01(one_f, g_c[:, :n_heads]),
                             _dot_exact_lhs01(one_b, g_c[:, n_heads:])], axis=1)
    cum_r = jnp.concatenate([_dot_exact_rhs01(g_r[:n_heads, :], one_b),
                             _dot_exact_rhs01(g_r[n_heads:, :], one_f)], axis=0)
    beta_c = _sigmoid(raw_c[:, nd:])
    gc_ref[...] = jnp.concatenate([cum_c, beta_c], axis=1)
    gr_ref[...] = jnp.concatenate([cum_r, jnp.zeros_like(cum_r)], axis=0)


def gate_prep(h_bf, w_gate, a_log, dt_bias, n_heads):
    t, d = h_bf.shape
    tl = SCAN_TILE
    nd = 2 * n_heads
    body = functools.partial(_gate_prep_body, n_heads=n_heads)
    full = lambda shape: pl.BlockSpec(shape, lambda i: (0, 0))
    return pl.pallas_call(
        body, grid=(t // tl,),
        in_specs=[pl.BlockSpec((tl, d), lambda i: (i, 0)), full((d, 2 * nd)), full((2 * nd, d)),
                  full((1, nd)), full((1, nd)), full((nd, 1)), full((nd, 1))],
        out_specs=[pl.BlockSpec((tl, 2 * nd), lambda i: (i, 0)), pl.BlockSpec((2 * nd, tl), lambda i: (0, i))],
        out_shape=[jax.ShapeDtypeStruct((t, 2 * nd), F32), jax.ShapeDtypeStruct((2 * nd, t), F32)],
        compiler_params=_params("arbitrary"), name="gate_prep",
    )(h_bf, w_gate, w_gate.T, a_log.reshape(1, nd), dt_bias.reshape(1, nd),
      a_log.reshape(nd, 1), dt_bias.reshape(nd, 1))


def _select_col(x, idx):
    lane = lax.broadcasted_iota(jnp.int32, x.shape, 1)
    return jnp.sum(jnp.where(lane == idx, x, 0.0), axis=1, keepdims=True)


def _select_row(x, idx):
    row = lax.broadcasted_iota(jnp.int32, x.shape, 0)
    return jnp.sum(jnp.where(row == idx, x, 0.0), axis=0, keepdims=True)


def _run_interleaved(chains):
    chains = list(chains)
    while chains:
        alive = []
        for ch in chains:
            try:
                next(ch)
                alive.append(ch)
            except StopIteration:
                pass
        chains = alive


def _delta_chain(q, k, v, gc_col, gc_row, beta_col, s_ref, o_ref, cols, reverse, inv_passes):
    tl, kd_ = k.shape
    n_chunks = tl // CHUNK
    incl, strict = _tile_masks(tl, reverse)
    decay = jnp.exp(jnp.where(incl, gc_col - gc_row, NEG_BIG))
    kb, qb = k.astype(BF16), q.astype(BF16)
    kkt = _dot_nt(kb, kb)
    qkt = _dot_nt(qb, kb)
    yield
    x = jnp.where(strict, kkt * (-beta_col) * decay, 0.0)
    dot_inv = _dot3 if inv_passes == 3 else _dot
    ri = lax.broadcasted_iota(jnp.int32, (tl, tl), 0)
    ci = lax.broadcasted_iota(jnp.int32, (tl, tl), 1)
    r = jnp.where(ri == ci, 1.0, 0.0) + x
    n_sq = int(math.log2(CHUNK)) - 1
    for _ in range(n_sq):
        x = dot_inv(x, x)
        r = r + dot_inv(r, x)
        yield
    e_g = jnp.exp(gc_col)
    rhs = jnp.concatenate([v * beta_col, k * (beta_col * e_g)], axis=1)
    sol = dot_inv(r, rhs)
    yield
    u0, w = sol[:, :v.shape[1]], sol[:, v.shape[1]:]
    attn = (qkt * decay).astype(BF16)
    a_sol = _dot(attn, sol)
    o0 = a_sol[:, :v.shape[1]]
    qe = q * e_g - a_sol[:, v.shape[1]:]
    tot_rows = []
    for c in range(n_chunks):
        last = c * CHUNK if reverse else c * CHUNK + CHUNK - 1
        tot_rows.append(jnp.broadcast_to(gc_col[last:last + 1, :], (CHUNK, 1)))
    tot = jnp.concatenate(tot_rows, axis=0)
    kdec_t = jnp.transpose(k * jnp.exp(tot - gc_col))
    wu = jnp.concatenate([-w, u0], axis=1).astype(BF16)
    lane = lax.broadcasted_iota(jnp.int32, kdec_t.shape, 1)
    pns = [_dot(jnp.where((lane // CHUNK) == c, kdec_t, 0.0), wu) for c in range(n_chunks)]
    yield
    order = range(n_chunks - 1, -1, -1) if reverse else range(n_chunks)
    for c in order:
        rows = slice(c * CHUNK, (c + 1) * CHUNK)
        pn = pns[c]
        lhs = jnp.concatenate([qe[rows], pn[:, :kd_]], axis=0)
        s = s_ref[...]
        res = _dot(lhs, s)
        o_ref[rows, cols] = o0[rows] + res[:CHUNK]
        last = c * CHUNK if reverse else c * CHUNK + CHUNK - 1
        gl = jnp.exp(gc_col[last:last + 1, :])
        s_ref[...] = gl * s + res[CHUNK:] + pn[:, kd_:]
        yield


def _delta_body(qf_ref, kf_ref, vf_ref, gcf_ref, grf_ref, qb_ref, kb_ref, vb_ref, gcb_ref, grb_ref,
                of_ref, ob_ref, sf_ref, sb_ref, *, n_heads, head_dim, inv_passes):
    hg = pl.program_id(1)
    heads_per_step = sf_ref.shape[0]

    @pl.when(pl.program_id(2) == 0)
    def _():
        sf_ref[...] = jnp.zeros_like(sf_ref)
        sb_ref[...] = jnp.zeros_like(sb_ref)

    chains = []
    for g in range(heads_per_step):
        cols = slice(g * head_dim, (g + 1) * head_dim)
        for reverse, (q_ref, k_ref, v_ref, gc_ref, gr_ref, o_ref, s_ref) in enumerate(
                [(qf_ref, kf_ref, vf_ref, gcf_ref, grf_ref, of_ref, sf_ref),
                 (qb_ref, kb_ref, vb_ref, gcb_ref, grb_ref, ob_ref, sb_ref)]):
            idx = reverse * n_heads + hg * heads_per_step + g
            gcs = gc_ref[...]
            gc_col = _select_col(gcs, idx)
            beta_col = _select_col(gcs, 2 * n_heads + idx)
            gc_row = _select_row(gr_ref[...], idx)
            chains.append(_delta_chain(q_ref[:, cols], k_ref[:, cols], v_ref[:, cols], gc_col, gc_row, beta_col,
                                       s_ref.at[g], o_ref, cols, bool(reverse), inv_passes))
    _run_interleaved(chains)


def delta_scan(qkv, gc, gr, geom, ctx_len, n_heads, head_dim, inv_passes=3, heads_per_step=DELTA_HEADS_PER_STEP):
    n_lat, lat_len, n_batch = geom
    t = qkv.shape[0]
    tl = SCAN_TILE
    nl, nc, ctx0 = lat_len // tl, ctx_len // tl, n_lat // tl
    n_hg = n_heads // heads_per_step
    gw = heads_per_step * head_dim
    tile = functools.partial(_segment_tile, nl=nl, nc=nc, ctx_tile0=ctx0)
    specs = []
    for reverse in (False, True):
        tix = functools.partial(tile, reverse=reverse)
        specs += [pl.BlockSpec((tl, gw), lambda b, h, s, tix=tix: (tix(b, s), h)),
                  pl.BlockSpec((tl, gw), lambda b, h, s, tix=tix: (tix(b, s), n_hg + h)),
                  pl.BlockSpec((tl, gw), lambda b, h, s, tix=tix: (tix(b, s), 2 * n_hg + h)),
                  pl.BlockSpec((tl, 4 * n_heads), lambda b, h, s, tix=tix: (tix(b, s), 0)),
                  pl.BlockSpec((4 * n_heads, tl), lambda b, h, s, tix=tix: (0, tix(b, s)))]
    out_specs = [pl.BlockSpec((tl, gw), lambda b, h, s, tix=functools.partial(tile, reverse=r): (tix(b, s), h))
                 for r in (False, True)]
    body = functools.partial(_delta_body, n_heads=n_heads, head_dim=head_dim, inv_passes=inv_passes)
    return pl.pallas_call(
        body, grid=(n_batch, n_hg, nl + nc), in_specs=specs, out_specs=out_specs,
        out_shape=[jax.ShapeDtypeStruct((t, n_heads * head_dim), F32)] * 2,
        scratch_shapes=[pltpu.VMEM((heads_per_step, head_dim, head_dim), F32)] * 2,
        compiler_params=_params("arbitrary", "arbitrary", "arbitrary"), name="delta_scan",
    )(qkv, qkv, qkv, gc, gr, qkv, qkv, qkv, gc, gr)


def _hgrn_diag(q, k, cum, v, stage_ref, diag_ref, reverse):
    n_sub = q.shape[0] // SUB
    for n, val in enumerate((q, k, cum, v)):
        stage_ref[n] = val

    def slabs(n):
        return [stage_ref.at[n][pl.ds(r, n_sub, stride=SUB), :] for r in range(SUB)]

    q_x, k_x, p_x, v_x = slabs(0), slabs(1), slabs(2), slabs(3)
    pairs = [(i, j) for i in range(SUB) for j in range(SUB) if (j >= i if reverse else j <= i)]
    terms = []
    for i, j in pairs:
        qk = q_x[i] * k_x[j]
        terms.append((qk if i == j else qk * jnp.exp(p_x[i] - p_x[j])).astype(BF16))
    kd_ = terms[0].shape[1]
    a_rep = jnp.dot(jnp.concatenate(terms, axis=0), jnp.ones((kd_, kd_), BF16), preferred_element_type=F32)
    o_x = [None] * SUB
    for n, (i, j) in enumerate(pairs):
        contrib = a_rep[n * n_sub:(n + 1) * n_sub] * v_x[j]
        o_x[i] = contrib if o_x[i] is None else o_x[i] + contrib
    for r in range(SUB):
        diag_ref[pl.ds(r, n_sub, stride=SUB), :] = o_x[r]


def _hgrn_chain(hq_ref, hf_ref, hv_ref, cols, lb, st_ref, o_ref, stage_ref, diag_ref, reverse):
    hq, hf, hv = hq_ref[:, cols], hf_ref[:, cols], hv_ref[:, cols]
    tl, kd_ = hq.shape
    n_chunks = tl // CHUNK
    n_sub = tl // SUB
    sub_per_chunk = CHUNK // SUB
    q = _silu(hq)
    f = lb + (1.0 - lb) * _sigmoid(hf)
    k = 1.0 - f
    lf = jnp.log(f)
    incl, _ = _tile_masks(tl, reverse)
    cum = _dot_exact_lhs01(jnp.where(incl, 1.0, 0.0), lf)
    excl = cum - lf

    def bcast_rows(src, row, n):
        return jnp.broadcast_to(src[row:row + 1, :], (n, kd_))

    chunk_last = [(c * CHUNK if reverse else c * CHUNK + CHUNK - 1) for c in range(n_chunks)]
    tot = jnp.concatenate([bcast_rows(cum, chunk_last[c], CHUNK) for c in range(n_chunks)], axis=0)
    sub_first = [(m * SUB + SUB - 1 if reverse else m * SUB) for m in range(n_sub)]
    r_sub = jnp.concatenate([bcast_rows(excl, sub_first[m], SUB) for m in range(n_sub)], axis=0)
    q_t = q * jnp.exp(cum - r_sub)
    qd = q * jnp.exp(cum)
    kd = k * jnp.exp(tot - cum)
    vb = hv.astype(BF16)

    i = lax.broadcasted_iota(jnp.int32, (tl, tl), 0)
    j = lax.broadcasted_iota(jnp.int32, (tl, tl), 1)
    same = (i // CHUNK) == (j // CHUNK)
    pos_i = (i % CHUNK) // SUB
    pos_j = (j % CHUNK) // SUB
    if reverse:
        pos_i, pos_j = sub_per_chunk - 1 - pos_i, sub_per_chunk - 1 - pos_j
    a_off = jnp.zeros((tl, tl), F32)
    for lvl in range(1, sub_per_chunk):
        ref_rows = []
        for c in range(n_chunks):
            m = c * sub_per_chunk + (sub_per_chunk - 1 - lvl if reverse else lvl)
            ref_rows.append(bcast_rows(excl, sub_first[m], CHUNK))
        r_lvl = jnp.concatenate(ref_rows, axis=0)
        k_t = k * jnp.exp(jnp.minimum(r_lvl - cum, 0.0))
        a_l = _dot_nt(q_t, k_t)
        a_off = a_off + jnp.where(same & (pos_i == lvl) & (pos_j < lvl), a_l, 0.0)
        yield
    o_intra = _dot(a_off, vb)
    yield

    _hgrn_diag(q, k, cum, hv, stage_ref, diag_ref, reverse)
    yield
    o_intra = o_intra + diag_ref[...]

    v_t = jnp.transpose(hv)
    lane = lax.broadcasted_iota(jnp.int32, v_t.shape, 1)
    kdb = kd.astype(BF16)
    n_ts = [_dot(jnp.where((lane // CHUNK) == c, v_t, 0.0), kdb) for c in range(n_chunks)]
    yield
    order = range(n_chunks - 1, -1, -1) if reverse else range(n_chunks)
    for c in order:
        rows = slice(c * CHUNK, (c + 1) * CHUNK)
        st = st_ref[...]
        o_ref[rows, cols] = o_intra[rows] + _dot_nt(qd[rows], st)
        st_ref[...] = st * jnp.exp(cum[chunk_last[c]:chunk_last[c] + 1, :]) + n_ts[c]
        yield


def _hgrn_body(qf_ref, ff_ref, vf_ref, qb_ref, fb_ref, vb_ref, lb_ref, of_ref, ob_ref, sf_ref, sb_ref, stage_ref,
               diag_ref, *, key_dim):
    heads_per_step = sf_ref.shape[0]

    @pl.when(pl.program_id(2) == 0)
    def _():
        sf_ref[...] = jnp.zeros_like(sf_ref)
        sb_ref[...] = jnp.zeros_like(sb_ref)

    chains = []
    for g in range(heads_per_step):
        cols = slice(g * key_dim, (g + 1) * key_dim)
        lb = lb_ref[:, cols]
        chains.append(_hgrn_chain(qf_ref, ff_ref, vf_ref, cols, lb, sf_ref.at[g], of_ref,
                                  stage_ref.at[2 * g], diag_ref.at[2 * g], False))
        chains.append(_hgrn_chain(qb_ref, fb_ref, vb_ref, cols, lb, sb_ref.at[g], ob_ref,
                                  stage_ref.at[2 * g + 1], diag_ref.at[2 * g + 1], True))
    _run_interleaved(chains)


def hgrn_scan(proj_hg, lb, geom, ctx_len, n_heads, key_dim, heads_per_step=HGRN_HEADS_PER_STEP):
    n_lat, lat_len, n_batch = geom
    t = proj_hg.shape[0]
    tl = SCAN_TILE
    nl, nc, ctx0 = lat_len // tl, ctx_len // tl, n_lat // tl
    n_hg = n_heads // heads_per_step
    gw = heads_per_step * key_dim
    tile = functools.partial(_segment_tile, nl=nl, nc=nc, ctx_tile0=ctx0)
    specs = []
    for reverse in (False, True):
        tix = functools.partial(tile, reverse=reverse)
        fcol = (1 + int(reverse)) * n_hg
        specs += [pl.BlockSpec((tl, gw), lambda b, h, s, tix=tix: (tix(b, s), h)),
                  pl.BlockSpec((tl, gw), lambda b, h, s, tix=tix, fcol=fcol: (tix(b, s), fcol + h)),
                  pl.BlockSpec((tl, gw), lambda b, h, s, tix=tix: (tix(b, s), 3 * n_hg + h))]
    specs.append(pl.BlockSpec((1, gw), lambda b, h, s: (0, h)))
    out_specs = [pl.BlockSpec((tl, gw), lambda b, h, s, tix=functools.partial(tile, reverse=r): (tix(b, s), h))
                 for r in (False, True)]
    return pl.pallas_call(
        functools.partial(_hgrn_body, key_dim=key_dim), grid=(n_batch, n_hg, nl + nc),
        in_specs=specs, out_specs=out_specs,
        out_shape=[jax.ShapeDtypeStruct((t, n_heads * key_dim), F32)] * 2,
        scratch_shapes=[pltpu.VMEM((heads_per_step, key_dim, key_dim), F32)] * 2
        + [pltpu.VMEM((2 * heads_per_step, 4, tl, key_dim), F32), pltpu.VMEM((2 * heads_per_step, tl, key_dim), F32)],
        compiler_params=_params("arbitrary", "arbitrary", "arbitrary"), name="hgrn_scan",
    )(proj_hg, proj_hg, proj_hg, proj_hg, proj_hg, proj_hg, lb)


def _mix_out_body(df_ref, db_ref, hf_ref, hb_ref, z_ref, og_ref, dnw_ref, hgw_ref, o_ref, *, head_dim):
    def normed(o, nw):
        outs = []
        for h in range(o.shape[1] // head_dim):
            oh = o[:, h * head_dim:(h + 1) * head_dim]
            outs.append(oh * lax.rsqrt(jnp.mean(oh * oh, axis=-1, keepdims=True) + NORM_EPS) * nw)
        return jnp.concatenate(outs, axis=1)

    dn = normed(df_ref[...] + db_ref[...], dnw_ref[...]) * _silu(z_ref[...])
    hg = normed(hf_ref[...] + hb_ref[...], hgw_ref[...]) * _sigmoid(og_ref[...])
    half = dn.shape[1]
    o_ref[:, :half] = dn.astype(o_ref.dtype)
    o_ref[:, half:] = hg.astype(o_ref.dtype)


def mix_out(dn_f, dn_b, hg_f, hg_b, z_src, z_blk, og_src, og_blk, dn_norm, hg_norm, head_dim):
    t, w = dn_f.shape
    tl = SCAN_TILE
    row = lambda i: (i, 0)
    body = functools.partial(_mix_out_body, head_dim=head_dim)
    return pl.pallas_call(
        body, grid=(t // tl,),
        in_specs=[pl.BlockSpec((tl, w), row)] * 4
        + [pl.BlockSpec((tl, w), lambda i: (i, z_blk)), pl.BlockSpec((tl, w), lambda i: (i, og_blk)),
           pl.BlockSpec((1, head_dim), lambda i: (0, 0)), pl.BlockSpec((1, head_dim), lambda i: (0, 0))],
        out_specs=pl.BlockSpec((tl, 2 * w), row),
        out_shape=jax.ShapeDtypeStruct((t, 2 * w), BF16),
        compiler_params=_params("arbitrary"), name="mix_out",
    )(dn_f, dn_b, hg_f, hg_b, z_src, og_src, dn_norm.reshape(1, head_dim), hg_norm.reshape(1, head_dim))


def _first_max(x, ids, n):
    m = jnp.max(x, axis=0, keepdims=True)
    first = jnp.min(jnp.where(x == m, ids, n), axis=0, keepdims=True)
    return m, first


def _route_body(lg_ref, bias_ref, idx_ref, rank_ref, w_ref, cnt_ref, carry_ref, *,
                n_groups, topk_groups, top_k, scale):
    i = pl.program_id(0)

    @pl.when(i == 0)
    def _():
        carry_ref[...] = jnp.zeros_like(carry_ref)

    lg = lg_ref[...]
    n_exp, tl = lg.shape
    per = n_exp // n_groups
    scores = 1.0 / (1.0 + jnp.exp(-lg))
    biased = scores + bias_ref[...]
    sub = lax.broadcasted_iota(jnp.int32, (per, tl), 0)
    g_rows = []
    for g in range(n_groups):
        xg = biased[g * per:(g + 1) * per]
        m1, i1 = _first_max(xg, sub, per)
        m2 = jnp.max(jnp.where(sub == i1, NEG_INF, xg), axis=0, keepdims=True)
        g_rows.append(m1 + m2)
    gscore = jnp.concatenate(g_rows, axis=0)
    gid = lax.broadcasted_iota(jnp.int32, (n_groups, tl), 0)
    gsel = jnp.zeros((n_groups, tl), jnp.bool_)
    for _ in range(topk_groups):
        _, first = _first_max(gscore, gid, n_groups)
        hit = gid == first
        gsel = gsel | hit
        gscore = jnp.where(hit, NEG_INF, gscore)
    eid = lax.broadcasted_iota(jnp.int32, (n_exp, tl), 0)
    gmask = jnp.concatenate([jnp.broadcast_to(gsel[g:g + 1], (per, tl)) for g in range(n_groups)], axis=0)
    masked = jnp.where(gmask, biased, NEG_INF)
    sel = jnp.zeros((n_exp, tl), jnp.bool_)
    hits, firsts = [], []
    for _ in range(top_k):
        _, first = _first_max(masked, eid, n_exp)
        hit = eid == first
        hits.append(hit)
        firsts.append(first)
        sel = sel | hit
        masked = jnp.where(hit, NEG_INF, masked)
    self = jnp.where(sel, 1.0, 0.0)
    ti = lax.broadcasted_iota(jnp.int32, (tl, tl), 0)
    tj = lax.broadcasted_iota(jnp.int32, (tl, tl), 1)
    before = jnp.where(ti < tj, 1.0, 0.0).astype(BF16)
    carry = carry_ref[...]
    rank_full = jnp.dot(self.astype(BF16), before, preferred_element_type=F32) + carry
    w_rows = [jnp.sum(jnp.where(hit, scores, 0.0), axis=0, keepdims=True) for hit in hits]
    r_rows = [jnp.sum(jnp.where(hit, rank_full, 0.0), axis=0, keepdims=True) for hit in hits]
    w8 = jnp.concatenate(w_rows, axis=0)
    idx_ref[...] = jnp.concatenate(firsts, axis=0)
    rank_ref[...] = jnp.concatenate(r_rows, axis=0).astype(jnp.int32)
    w_ref[...] = w8 / jnp.sum(w8, axis=0, keepdims=True) * scale
    carry = carry + jnp.sum(self, axis=1, keepdims=True)
    carry_ref[...] = carry
    cnt_ref[...] = carry.astype(jnp.int32)


def route(logits_t, router_b, tl=256):
    n_exp, t = logits_t.shape
    body = functools.partial(_route_body, n_groups=N_GROUPS, topk_groups=TOPK_GROUPS, top_k=TOP_K,
                             scale=ROUTED_SCALE)
    tok_spec = pl.BlockSpec((TOP_K, tl), lambda i: (0, i))
    return pl.pallas_call(
        body, grid=(t // tl,),
        in_specs=[pl.BlockSpec((n_exp, tl), lambda i: (0, i)), pl.BlockSpec((n_exp, 1), lambda i: (0, 0))],
        out_specs=[tok_spec, tok_spec, tok_spec, pl.BlockSpec((n_exp, 1), lambda i: (0, 0))],
        out_shape=[jax.ShapeDtypeStruct((TOP_K, t), jnp.int32), jax.ShapeDtypeStruct((TOP_K, t), jnp.int32),
                   jax.ShapeDtypeStruct((TOP_K, t), F32), jax.ShapeDtypeStruct((n_exp, 1), jnp.int32)],
        scratch_shapes=[pltpu.VMEM((n_exp, 1), F32)],
        compiler_params=_params("arbitrary"), name="route",
    )(logits_t, router_b.reshape(n_exp, 1))


def _slab_copy(src_ref, src_tok, dst_ref, dst_tok, rows, sem):
    return pltpu.make_async_copy(src_ref.at[pl.ds(pl.multiple_of(src_tok * rows, rows), rows)],
                                 dst_ref.at[pl.ds(pl.multiple_of(dst_tok * rows, rows), rows)], sem)


def _dispatch_body(dest_ref, pad_ref, h_ref, xs_ref, zero_ref, sem, *, top_k, rows):
    tl = h_ref.shape[0] // rows
    n_exp = pad_ref.shape[1]

    def issue(r, carry):
        for k in range(top_k):
            _slab_copy(h_ref, r, xs_ref, dest_ref[r * top_k + k], rows, sem.at[0]).start(priority=k % 2)
        return carry

    lax.fori_loop(0, tl, issue, 0, unroll=DMA_LOOP_UNROLL)

    @pl.when(pl.program_id(0) == 0)
    def _():
        zero_ref[...] = jnp.zeros_like(zero_ref)
        largest = zero_ref.shape[0] // rows
        pieces = [largest >> b for b in range(largest.bit_length())]

        def pad_copies(e, wait):
            first, n_pad = pad_ref[0, e], pad_ref[1, e]
            for size in pieces:
                @pl.when((n_pad & size) != 0)
                def _():
                    slot0 = first + (n_pad & ~(2 * size - 1))
                    cp = pltpu.make_async_copy(
                        zero_ref.at[pl.ds(0, size * rows)],
                        xs_ref.at[pl.ds(pl.multiple_of(slot0 * rows, rows), size * rows)], sem.at[1])
                    if wait:
                        cp.wait()
                    else:
                        cp.start()

        def fill(e, carry):
            pad_copies(e, False)
            return carry

        def drain_pad(e, carry):
            pad_copies(e, True)
            return carry

        lax.fori_loop(0, n_exp, fill, 0)
        lax.fori_loop(0, n_exp, drain_pad, 0)

    def drain(r, carry):
        for k in range(top_k):
            _slab_copy(h_ref, 0, xs_ref, 0, rows, sem.at[0]).wait()
        return carry

    lax.fori_loop(0, tl, drain, 0, unroll=DMA_LOOP_UNROLL)


def dispatch(h_slab, dest_flat, pad_info, n_slot, top_k, rows, tl=ROW_TILE):
    t = h_slab.shape[0] // rows
    return pl.pallas_call(
        functools.partial(_dispatch_body, top_k=top_k, rows=rows), grid=(t // tl,),
        in_specs=[pl.BlockSpec((tl * top_k,), lambda i: (i,), memory_space=pltpu.SMEM),
                  pl.BlockSpec(memory_space=pltpu.SMEM),
                  pl.BlockSpec((tl * rows, LANES), lambda i: (i, 0))],
        out_specs=pl.BlockSpec(memory_space=pl.ANY),
        out_shape=jax.ShapeDtypeStruct((n_slot * rows, LANES), h_slab.dtype),
        scratch_shapes=[pltpu.VMEM((EXPERT_SLOT_BLOCK // 2 * rows, LANES), h_slab.dtype),
                        pltpu.SemaphoreType.DMA((2,))],
        compiler_params=_params("arbitrary"), name="dispatch",
    )(dest_flat, pad_info, h_slab)


def _combine_body(dest_ref, dest_next_ref, w_ref, sh_ref, x_ref, gate_ref, y_ref, o_ref, buf_ref, acc_ref, sem, *,
                  top_k, rows):
    i = pl.program_id(0)
    n = pl.num_programs(0)
    tl = x_ref.shape[0]

    def gather(d_ref, slot):
        def issue(r, carry):
            for k in range(top_k):
                _slab_copy(y_ref, d_ref[r * top_k + k], buf_ref.at[slot], k * tl + r, rows,
                           sem.at[slot]).start(priority=k % 2)
            return carry
        lax.fori_loop(0, tl, issue, 0, unroll=DMA_LOOP_UNROLL)

    @pl.when(i == 0)
    def _():
        gather(dest_ref, 0)

    @pl.when(i + 1 < n)
    def _():
        gather(dest_next_ref, (i + 1) % 2)

    slot = i % 2

    def drain(r, carry):
        for k in range(top_k):
            _slab_copy(y_ref, 0, buf_ref.at[slot], 0, rows, sem.at[slot]).wait()
        return carry

    lax.fori_loop(0, tl, drain, 0, unroll=DMA_LOOP_UNROLL)
    def unpack_f32(pk):
        return (lax.bitcast_convert_type(pk & jnp.uint32(0xFFFF0000), F32),
                lax.bitcast_convert_type(pk << jnp.uint32(16), F32))

    w = w_ref[...]
    acc_hi, acc_lo = unpack_f32(sh_ref[...])
    for k in range(top_k):
        wk = jnp.broadcast_to(w[:, k:k + 1], (tl, LANES))
        wk = jnp.broadcast_to(wk[:, None, :], (tl, rows, LANES)).reshape(tl * rows, LANES)
        y_hi, y_lo = unpack_f32(buf_ref[slot, pl.ds(k * tl * rows, tl * rows), :])
        acc_hi = acc_hi + y_hi * wk
        acc_lo = acc_lo + y_lo * wk
    acc_ref[0] = acc_hi
    acc_ref[1] = acc_lo
    for half in range(2):
        for c in range(rows):
            cols = slice((half * rows + c) * LANES, (half * rows + c + 1) * LANES)
            o_ref[:, cols] = (x_ref[:, cols]
                              + gate_ref[0][:, cols] * acc_ref.at[half][pl.ds(c, tl, stride=rows), :])


def combine(y_slab, dest_flat, w, shared_slab, x, mod3, gate_idx, geom, tl=COMBINE_TILE):
    t, d = x.shape
    rows = d // 2 // LANES
    top_k = w.shape[1]
    n_lat, lat_len, n_batch = geom
    n_tiles = t // tl
    row = functools.partial(_mod_row, tile_rows=tl, n_lat=n_lat, lat_len=lat_len, n_batch=n_batch)
    return pl.pallas_call(
        functools.partial(_combine_body, top_k=top_k, rows=rows), grid=(n_tiles,),
        in_specs=[pl.BlockSpec((tl * top_k,), lambda i: (i,), memory_space=pltpu.SMEM),
                  pl.BlockSpec((tl * top_k,), lambda i: (jnp.minimum(i + 1, n_tiles - 1),),
                               memory_space=pltpu.SMEM),
                  pl.BlockSpec((tl, top_k), lambda i: (i, 0)),
                  pl.BlockSpec((tl * rows, LANES), lambda i: (i, 0)),
                  pl.BlockSpec((tl, d), lambda i: (i, 0)),
                  pl.BlockSpec((1, 1, d), lambda i: (row(i) * 6 + gate_idx, 0, 0)),
                  pl.BlockSpec(memory_space=pl.ANY)],
        out_specs=pl.BlockSpec((tl, d), lambda i: (i, 0)),
        out_shape=jax.ShapeDtypeStruct((t, d), F32),
        scratch_shapes=[pltpu.VMEM((2, top_k * tl * rows, LANES), jnp.uint32),
                        pltpu.VMEM((2, tl * rows, LANES), F32), pltpu.SemaphoreType.DMA((2,))],
        compiler_params=_params("arbitrary"), name="combine",
    )(dest_flat, dest_flat, w, shared_slab, x, mod3, y_slab)


def moe_ffn(x, h_pk, logits_t, router_b, exp_gate_up, exp_down, shared_gate_up, shared_down, mod3, geom):
    t = x.shape[0]
    slab_rows = h_pk.shape[0] // t
    e_count = exp_gate_up.shape[0]
    blk = EXPERT_SLOT_BLOCK
    idx8, rank8, w8, counts = route(logits_t, router_b)
    counts = counts[:, 0]
    padded = (counts + blk - 1) // blk * blk
    pad_end = jnp.cumsum(padded)
    pad_start = pad_end - padded
    start8 = jnp.sum(jnp.where(idx8[None] == jnp.arange(e_count, dtype=jnp.int32)[:, None, None],
                               pad_start[:, None, None], 0), axis=0)
    dest_flat = (start8 + rank8).T.reshape(-1)
    n_blk = (t * TOP_K + e_count * (blk - 1)) // blk + 1
    n_blk = -(-n_blk // EXPERT_BLOCKS_PER_STEP) * EXPERT_BLOCKS_PER_STEP
    n_slot = n_blk * blk
    blk_starts = jnp.arange(n_blk, dtype=jnp.int32) * blk
    blk_e = jnp.sum((pad_end[None, :] <= blk_starts[:, None]).astype(jnp.int32), axis=1)
    blk_e = jnp.minimum(blk_e, e_count - 1)
    n_used = (pad_end[-1] // blk).astype(jnp.int32).reshape(1)
    pad_info = jnp.stack([pad_start + counts, padded - counts]).astype(jnp.int32)
    x_sorted = dispatch(h_pk, dest_flat, pad_info, n_slot, TOP_K, slab_rows)
    y_slot = expert_blocks(x_sorted, exp_gate_up, exp_down, blk_e, n_used, blk)
    shared = expert_blocks(h_pk, shared_gate_up[None], shared_down[None], jnp.zeros((t // blk,), jnp.int32),
                           jnp.full((1,), t // blk, jnp.int32), blk)
    return combine(y_slot, dest_flat, w8.T, shared, x, mod3, 5, geom)


def kernel(x, c, ctx, c_ctx, hg_lb_logits, l0_mod_w, l0_mod_b, l0_norm1, l0_norm2, l0_w_in, l0_dn_conv, l0_dn_a_log, l0_dn_dt_bias, l0_dn_norm, l0_hg_norm, l0_w_out, l0_router_w, l0_router_b, l0_exp_gate_up, l0_exp_down, l0_shared_gate_up, l0_shared_down, l1_mod_w, l1_mod_b, l1_norm1, l1_norm2, l1_w_in, l1_q_norm, l1_k_norm, l1_lambda, l1_sub_norm, l1_w_out, l1_router_w, l1_router_b, l1_exp_gate_up, l1_exp_down, l1_shared_gate_up, l1_shared_down):
    n_batch, lat_len, d = x.shape
    ctx_len = ctx.shape[1]
    n_lat = n_batch * lat_len
    n_ctx = n_batch * ctx_len
    geom = (n_lat, lat_len, n_batch)
    geom_lat_only = (n_lat, lat_len, n_batch)

    xs = jnp.concatenate([x.reshape(n_lat, d), ctx.reshape(n_ctx, d)], axis=0)
    cond = jnp.concatenate([c, c_ctx[None], jnp.zeros((MOD_ROWS - n_batch - 1, d), F32)], axis=0)

    mod3 = modulation(cond, l0_mod_w, l0_mod_b).reshape(MOD_ROWS * 6, 1, d)
    h = adaln(xs, l0_norm1, mod3, 0, geom)
    n_dn = IN0_SIZES[0] + IN0_SIZES[1]
    n_gate = IN0_SIZES[2] + IN0_SIZES[3]
    proj_dn = matmul(h, l0_w_in, n_cols=n_dn)
    proj_hg = matmul(h, l0_w_in[:, n_dn + n_gate:])
    gc, gr = gate_prep(h, l0_w_in[:, n_dn:n_dn + n_gate], l0_dn_a_log, l0_dn_dt_bias, DN_HEADS)
    conv_t = jnp.concatenate([l0_dn_conv.T, jnp.zeros((8 - l0_dn_conv.shape[1], 3 * DN_WIDTH), F32)], axis=0)
    qkv = dn_prep(proj_dn, conv_t, geom, ctx_len, DN_WIDTH, DN_HEAD_DIM)
    dn_f, dn_b = delta_scan(qkv, gc, gr, geom, ctx_len, DN_HEADS, DN_HEAD_DIM, inv_passes=DELTA_INV_PASSES)
    lb = jnp.cumsum(jax.nn.softmax(hg_lb_logits, axis=0), axis=0)[0:1]
    hg_f, hg_b = hgrn_scan(proj_hg, lb, geom, ctx_len, HG_HEADS, HG_KEY_DIM)
    y = mix_out(dn_f, dn_b, hg_f, hg_b, proj_dn, 3, proj_hg, 4, l0_dn_norm, l0_hg_norm, DN_HEAD_DIM)
    xs = matmul_residual(y, l0_w_out, xs, mod3, 2, geom)
    h, logits_t = adaln(xs, l0_norm2, mod3, 3, geom, router_w=l0_router_w)
    xs = moe_ffn(xs, h, logits_t, l0_router_b, l0_exp_gate_up, l0_exp_down, l0_shared_gate_up, l0_shared_down,
                 mod3, geom)

    mod3 = modulation(cond, l1_mod_w, l1_mod_b).reshape(MOD_ROWS * 6, 1, d)
    h = adaln(xs, l1_norm1, mod3, 0, geom)
    cos_t, sin_t = rope_tables(lat_len, n_batch, n_ctx, DA_HEAD_DIM)
    qkv = matmul_qkv(h, l1_w_in, l1_q_norm, l1_k_norm, cos_t, sin_t)
    lam_init = 0.8 - 0.6 * math.exp(-0.3 * 1)
    lmbda = (jnp.exp(jnp.sum(l1_lambda[0] * l1_lambda[1])) - jnp.exp(jnp.sum(l1_lambda[2] * l1_lambda[3]))
             + lam_init)
    y = diff_attention(qkv, lmbda, l1_sub_norm, n_batch, lat_len, ctx_len, DA_HEADS, DA_HEAD_DIM,
                       1.0 - lam_init)
    xl = matmul_residual(y, l1_w_out, xs, mod3, 2, geom_lat_only)
    h, logits_t = adaln(xl, l1_norm2, mod3, 3, geom_lat_only, router_w=l1_router_w)
    xl = moe_ffn(xl, h, logits_t, l1_router_b, l1_exp_gate_up, l1_exp_down, l1_shared_gate_up, l1_shared_down,
                 mod3, geom_lat_only)
    return xl.reshape(n_batch, lat_len, d)
```

```python
import functools
import math

import jax
import jax.numpy as jnp
from jax import lax
from jax.experimental import pallas as pl
from jax.experimental.pallas import tpu as pltpu

F32 = jnp.float32
BF16 = jnp.bfloat16

NORM_EPS = 1e-6
GRID_W = 64
ROPE_BASE = 10000.0

DN_HEADS = 8
DN_HEAD_DIM = 128
DN_WIDTH = DN_HEADS * DN_HEAD_DIM
HG_HEADS = 8
HG_KEY_DIM = 128
HG_K_WIDTH = HG_HEADS * HG_KEY_DIM
HG_V_WIDTH = HG_K_WIDTH
IN0_SIZES = (3 * DN_WIDTH, DN_WIDTH, 2 * DN_HEADS, 2 * DN_HEADS,
             HG_K_WIDTH, 2 * HG_K_WIDTH, HG_V_WIDTH, HG_V_WIDTH)
DA_HEADS = 8
DA_HEAD_DIM = 128
N_EXPERTS = 64
TOP_K = 8
N_GROUPS = 8
TOPK_GROUPS = 4
ROUTED_SCALE = 2.5

VMEM_LIMIT_BYTES = 56 * 1024 * 1024
LANES = 128
MOD_ROWS = 8
ROW_TILE = 512
MM_TILE_M = 1024
MM_TILE_N = 1024
MM_EPILOGUE_CHUNKS = 4
ATTN_TILE_Q = 1024
ATTN_ROW_CHUNK = 128
EXPERT_SLOT_BLOCK = 256
EXPERT_BLOCKS_PER_STEP = 4
DMA_LOOP_UNROLL = 8
COMBINE_TILE = 128


def _params(*sem):
    return pltpu.CompilerParams(dimension_semantics=sem, vmem_limit_bytes=VMEM_LIMIT_BYTES)


def _sigmoid(x):
    return 0.5 * jnp.tanh(0.5 * x) + 0.5


def _silu(x):
    return x * _sigmoid(x)


def _mod_body(c_ref, w_ref, b_ref, o_ref):
    a = _silu(c_ref[...]).astype(BF16)
    o_ref[...] = jnp.dot(a, w_ref[...].astype(BF16), preferred_element_type=F32) + b_ref[...]


def modulation(cond, w, b, tn=1024):
    m, k = cond.shape
    n = w.shape[1]
    return pl.pallas_call(
        _mod_body, grid=(n // tn,),
        in_specs=[pl.BlockSpec((m, k), lambda j: (0, 0)),
                  pl.BlockSpec((k, tn), lambda j: (0, j)),
                  pl.BlockSpec((1, tn), lambda j: (0, j))],
        out_specs=pl.BlockSpec((m, tn), lambda j: (0, j)),
        out_shape=jax.ShapeDtypeStruct((m, n), F32),
        compiler_params=_params("arbitrary"), name="modulation",
    )(cond, w, b.reshape(1, n))


def _mod_row(tile, tile_rows, n_lat, lat_len, n_batch):
    assert lat_len % tile_rows == 0 and n_lat % tile_rows == 0, (tile_rows, lat_len, n_lat)
    start = tile * tile_rows
    return jnp.where(start < n_lat, start // lat_len, n_batch)


def _pack_bf16_pairs(h):
    half = h.shape[1] // 2
    bits = lax.bitcast_convert_type(h.astype(BF16).astype(F32), jnp.uint32)
    return (bits[:, :half] & jnp.uint32(0xFFFF0000)) | (bits[:, half:] >> jnp.uint32(16))


def _unpack_bf16_pairs(pk):
    hi = lax.bitcast_convert_type(pk & jnp.uint32(0xFFFF0000), F32).astype(BF16)
    lo = lax.bitcast_convert_type(pk << jnp.uint32(16), F32).astype(BF16)
    return hi, lo


def _store_slabs(ref, x, row0=0):
    r, w = x.shape
    c_n = w // LANES
    for c in range(c_n):
        ref[pl.ds(row0 * c_n + c, r, stride=c_n), :] = x[:, c * LANES:(c + 1) * LANES]


def _adaln_body(x_ref, nw_ref, shift_ref, scale_ref, o_ref):
    x = x_ref[...]
    y = x * lax.rsqrt(jnp.mean(x * x, axis=-1, keepdims=True) + NORM_EPS) * nw_ref[...]
    o_ref[...] = (y * (1.0 + scale_ref[0]) + shift_ref[0]).astype(o_ref.dtype)


def _adaln_router_body(x_ref, nw_ref, shift_ref, scale_ref, rw_ref, o_ref, lg_ref):
    x = x_ref[...]
    y = x * lax.rsqrt(jnp.mean(x * x, axis=-1, keepdims=True) + NORM_EPS) * nw_ref[...]
    h = y * (1.0 + scale_ref[0]) + shift_ref[0]
    _store_slabs(o_ref, _pack_bf16_pairs(h))
    lg_ref[...] = lax.dot_general(rw_ref[...], h, (((1,), (1,)), ((), ())), preferred_element_type=F32,
                                  precision=lax.Precision.HIGHEST)


def adaln(x, norm_w, mod3, shift_idx, geom, router_w=None, tl=ROW_TILE):
    t, d = x.shape
    n_lat, lat_len, n_batch = geom
    row = functools.partial(_mod_row, tile_rows=tl, n_lat=n_lat, lat_len=lat_len, n_batch=n_batch)
    in_specs = [pl.BlockSpec((tl, d), lambda i: (i, 0)),
                pl.BlockSpec((1, d), lambda i: (0, 0)),
                pl.BlockSpec((1, 1, d), lambda i: (row(i) * 6 + shift_idx, 0, 0)),
                pl.BlockSpec((1, 1, d), lambda i: (row(i) * 6 + shift_idx + 1, 0, 0))]
    args = [x, norm_w.reshape(1, d), mod3, mod3]
    if router_w is None:
        return pl.pallas_call(
            _adaln_body, grid=(t // tl,), in_specs=in_specs,
            out_specs=pl.BlockSpec((tl, d), lambda i: (i, 0)),
            out_shape=jax.ShapeDtypeStruct((t, d), BF16),
            compiler_params=_params("arbitrary"), name="adaln")(*args)
    e = router_w.shape[1]
    return pl.pallas_call(
        _adaln_router_body, grid=(t // tl,),
        in_specs=in_specs + [pl.BlockSpec((e, d), lambda i: (0, 0))],
        out_specs=[pl.BlockSpec((tl * (d // 2 // LANES), LANES), lambda i: (i, 0)),
                   pl.BlockSpec((e, tl), lambda i: (0, i))],
        out_shape=[jax.ShapeDtypeStruct((t * (d // 2 // LANES), LANES), jnp.uint32),
                   jax.ShapeDtypeStruct((e, t), F32)],
        compiler_params=_params("arbitrary"), name="adaln_router")(*args, router_w.T)


def _mm_body(a_ref, w_ref, o_ref, wb_ref):
    @pl.when(pl.program_id(1) == 0)
    def _():
        wb_ref[...] = w_ref[...].astype(BF16)
    o_ref[...] = jnp.dot(a_ref[...], wb_ref[...], preferred_element_type=F32).astype(o_ref.dtype)


def _mm_res_body(a_ref, w_ref, res_ref, gate_ref, o_ref, wb_ref):
    @pl.when(pl.program_id(1) == 0)
    def _():
        wb_ref[...] = w_ref[...].astype(BF16)
    acc = jnp.dot(a_ref[...], wb_ref[...], preferred_element_type=F32)
    o_ref[...] = res_ref[...] + gate_ref[0] * acc


def matmul(a, w, n_cols=None, col_block0=0, tm=MM_TILE_M, tn=MM_TILE_N, out_dtype=F32):
    m, k = a.shape
    n = w.shape[1] if n_cols is None else n_cols
    return pl.pallas_call(
        _mm_body, grid=(n // tn, m // tm),
        in_specs=[pl.BlockSpec((tm, k), lambda j, i: (i, 0)),
                  pl.BlockSpec((k, tn), lambda j, i: (0, j + col_block0))],
        out_specs=pl.BlockSpec((tm, tn), lambda j, i: (i, j)),
        out_shape=jax.ShapeDtypeStruct((m, n), out_dtype),
        scratch_shapes=[pltpu.VMEM((k, tn), BF16)],
        compiler_params=_params("arbitrary", "arbitrary"), name="matmul",
    )(a, w)


def matmul_residual(a, w, res, mod3, gate_idx, geom, tm=MM_TILE_M, tn=MM_TILE_N):
    m, k = a.shape
    n = w.shape[1]
    n_lat, lat_len, n_batch = geom
    row = functools.partial(_mod_row, tile_rows=tm, n_lat=n_lat, lat_len=lat_len, n_batch=n_batch)
    nb = n // tn
    return pl.pallas_call(
        _mm_res_body, grid=(n // tn, m // tm),
        in_specs=[pl.BlockSpec((tm, k), lambda j, i: (i, 0)),
                  pl.BlockSpec((k, tn), lambda j, i: (0, j)),
                  pl.BlockSpec((tm, tn), lambda j, i: (i, j)),
                  pl.BlockSpec((1, 1, tn), lambda j, i: (row(i) * 6 + gate_idx, 0, j))],
        out_specs=pl.BlockSpec((tm, tn), lambda j, i: (i, j)),
        out_shape=jax.ShapeDtypeStruct((m, n), F32),
        scratch_shapes=[pltpu.VMEM((k, tn), BF16)],
        compiler_params=_params("arbitrary", "arbitrary"), name="matmul_residual",
    )(a, w, res, mod3)


def _expert_body(blk_e_ref, blk_new_ref, next_e_ref, stage_ref, n_used_ref, x_ref, gu_hbm, dn_hbm, o_ref,
                 gu_stage, dn_stage, gub_ref, dnb_ref, sem):
    ff, d = dnb_ref.shape
    half = d // 2
    xc = half // LANES
    blk = x_ref.shape[0] // xc // EXPERT_BLOCKS_PER_STEP

    def weight_copies(e, slot):
        return (pltpu.make_async_copy(gu_hbm.at[e], gu_stage.at[slot], sem.at[0, slot]),
                pltpu.make_async_copy(dn_hbm.at[e], dn_stage.at[slot], sem.at[1, slot]))

    @pl.when(pl.program_id(0) == 0)
    def _():
        for cp in weight_copies(blk_e_ref[0], 0):
            cp.start()

    def one_block(b, sub):
        @pl.when(blk_new_ref[b] == 1)
        def _():
            slot = stage_ref[b]
            for cp in weight_copies(blk_e_ref[b], slot):
                cp.wait()
            gub_ref[...] = gu_stage[slot].astype(BF16)
            dnb_ref[...] = dn_stage[slot].astype(BF16)

            @pl.when(next_e_ref[b] >= 0)
            def _():
                for cp in weight_copies(next_e_ref[b], 1 - slot):
                    cp.start()

        @pl.when(b < n_used_ref[0])
        def _():
            h1 = None
            for p in range(2):
                cs = range(p * xc // 2, (p + 1) * xc // 2)
                pk = jnp.concatenate([x_ref[pl.ds(sub * blk * xc + c, blk, stride=xc), :] for c in cs], axis=1)
                x_hi, x_lo = _unpack_bf16_pairs(pk)
                k0 = p * half // 2
                part = (jnp.dot(x_hi, gub_ref[k0:k0 + half // 2], preferred_element_type=F32)
                        + jnp.dot(x_lo, gub_ref[half + k0:half + k0 + half // 2], preferred_element_type=F32))
                h1 = part if h1 is None else h1 + part
            act = (_silu(h1[:, :ff]) * h1[:, ff:]).astype(BF16)
            y = jnp.concatenate([jnp.dot(act, dnb_ref[:, p * half:(p + 1) * half], preferred_element_type=F32)
                                 for p in range(2)], axis=1)
            _store_slabs(o_ref, _pack_bf16_pairs(y), sub * blk)

        @pl.when(b >= n_used_ref[0])
        def _():
            o_ref[pl.ds(sub * blk * xc, blk * xc), :] = jnp.zeros((blk * xc, LANES), o_ref.dtype)

    for sub in range(EXPERT_BLOCKS_PER_STEP):
        one_block(pl.program_id(0) * EXPERT_BLOCKS_PER_STEP + sub, sub)


def expert_blocks(x, gate_up, down, blk_e, n_used, blk):
    _, d, f2 = gate_up.shape
    xc = yc = d // 2 // LANES
    s = x.shape[0] // xc
    n_blk = s // blk
    assert n_blk % EXPERT_BLOCKS_PER_STEP == 0, (n_blk, EXPERT_BLOCKS_PER_STEP)
    pos = jnp.arange(n_blk, dtype=jnp.int32)
    used = pos < n_used[0]
    blk_new = (jnp.concatenate([jnp.ones((1,), bool), blk_e[1:] != blk_e[:-1]]) & used).astype(jnp.int32)
    stage = (jnp.cumsum(blk_new) - 1) % 2
    first_pos = jnp.where(blk_new == 1, pos, n_blk)
    next_first = jnp.concatenate([lax.cummin(first_pos, reverse=True)[1:], jnp.full((1,), n_blk, jnp.int32)])
    next_e = jnp.where(next_first < n_blk, blk_e[jnp.minimum(next_first, n_blk - 1)], -1).astype(jnp.int32)
    bps = EXPERT_BLOCKS_PER_STEP
    grid_spec = pltpu.PrefetchScalarGridSpec(
        num_scalar_prefetch=5, grid=(n_blk // bps,),
        in_specs=[pl.BlockSpec((bps * blk * xc, LANES),
                               lambda i, be, bn, ne, st, nu: (jnp.minimum(i, (nu[0] - 1) // bps), 0)),
                  pl.BlockSpec(memory_space=pl.ANY),
                  pl.BlockSpec(memory_space=pl.ANY)],
        out_specs=pl.BlockSpec((bps * blk * yc, LANES), lambda i, be, bn, ne, st, nu: (i, 0)),
        scratch_shapes=[pltpu.VMEM((2, d, f2), F32), pltpu.VMEM((2, f2 // 2, d), F32),
                        pltpu.VMEM((d, f2), BF16), pltpu.VMEM((f2 // 2, d), BF16),
                        pltpu.SemaphoreType.DMA((2, 2))])
    return pl.pallas_call(
        _expert_body, grid_spec=grid_spec,
        out_shape=jax.ShapeDtypeStruct((s * yc, LANES), jnp.uint32),
        compiler_params=_params("arbitrary"), name="expert_blocks",
    )(blk_e, blk_new, next_e, stage.astype(jnp.int32), n_used, x, gate_up, down)


def _norm_rope(x, w, cos, sin, first, head_dim):
    y = x * lax.rsqrt(jnp.mean(x * x, axis=-1, keepdims=True) + NORM_EPS) * w
    swapped = jnp.where(first, pltpu.roll(y, head_dim - head_dim // 4, 1), pltpu.roll(y, head_dim // 4, 1))
    return y * cos + swapped * sin


def _mm_qkv_body(a_ref, w_ref, qw_ref, kw_ref, cos_ref, sin_ref, o_ref, wb_ref, *, head_dim, n_q_tiles, n_k_tiles):
    j = pl.program_id(0)

    @pl.when(pl.program_id(1) == 0)
    def _():
        wb_ref[...] = w_ref[...].astype(BF16)

    tm, tn = o_ref.shape
    row_chunk = tm // MM_EPILOGUE_CHUNKS

    def rows_chain(rows, nw_ref):
        acc = jnp.dot(a_ref[rows, :], wb_ref[...], preferred_element_type=F32)
        yield
        if nw_ref is None:
            o_ref[rows, :] = acc.astype(o_ref.dtype)
            return
        cos, sin = cos_ref[rows, :], sin_ref[rows, :]
        lane = lax.broadcasted_iota(jnp.int32, cos.shape, 1)
        first = (lane % (head_dim // 2)) < (head_dim // 4)
        for g in range(tn // head_dim):
            sl = slice(g * head_dim, (g + 1) * head_dim)
            o_ref[rows, sl] = _norm_rope(acc[:, sl], nw_ref[...], cos, sin, first, head_dim).astype(o_ref.dtype)
            yield

    def tile(nw_ref):
        _run_interleaved(rows_chain(slice(r, r + row_chunk), nw_ref) for r in range(0, tm, row_chunk))

    @pl.when(j < n_q_tiles)
    def _():
        tile(qw_ref)

    @pl.when((j >= n_q_tiles) & (j < n_q_tiles + n_k_tiles))
    def _():
        tile(kw_ref)

    @pl.when(j >= n_q_tiles + n_k_tiles)
    def _():
        tile(None)


def matmul_qkv(a, w, q_norm, k_norm, cos_t, sin_t, tm=MM_TILE_M, tn=MM_TILE_N):
    m, k = a.shape
    n = w.shape[1]
    hd = q_norm.shape[0]
    body = functools.partial(_mm_qkv_body, head_dim=hd, n_q_tiles=n // 3 // tn, n_k_tiles=n // 3 // tn)
    return pl.pallas_call(
        body, grid=(n // tn, m // tm),
        in_specs=[pl.BlockSpec((tm, k), lambda j, i: (i, 0)),
                  pl.BlockSpec((k, tn), lambda j, i: (0, j)),
                  pl.BlockSpec((1, hd), lambda j, i: (0, 0)),
                  pl.BlockSpec((1, hd), lambda j, i: (0, 0)),
                  pl.BlockSpec((tm, hd), lambda j, i: (i, 0)),
                  pl.BlockSpec((tm, hd), lambda j, i: (i, 0))],
        out_specs=pl.BlockSpec((tm, tn), lambda j, i: (i, j)),
        out_shape=jax.ShapeDtypeStruct((m, n), BF16),
        scratch_shapes=[pltpu.VMEM((k, tn), BF16)],
        compiler_params=_params("arbitrary", "arbitrary"), name="matmul_qkv",
    )(a, w, q_norm.reshape(1, hd), k_norm.reshape(1, hd), cos_t, sin_t)


def rope_tables(n_lat_tokens_per_sample, n_batch, n_ctx_tokens, head_dim):
    quarter = head_dim // 4
    inv_freq = ROPE_BASE ** (-jnp.arange(quarter, dtype=F32) / quarter)
    rows = n_lat_tokens_per_sample // GRID_W
    row = jnp.repeat(jnp.arange(rows, dtype=F32), GRID_W)
    col = jnp.tile(jnp.arange(GRID_W, dtype=F32), rows)
    ang_r = row[:, None] * inv_freq[None, :]
    ang_c = col[:, None] * inv_freq[None, :]
    cos = jnp.concatenate([jnp.cos(ang_r), jnp.cos(ang_r), jnp.cos(ang_c), jnp.cos(ang_c)], axis=-1)
    sin = jnp.concatenate([-jnp.sin(ang_r), jnp.sin(ang_r), -jnp.sin(ang_c), jnp.sin(ang_c)], axis=-1)
    cos = jnp.concatenate([jnp.tile(cos, (n_batch, 1)), jnp.ones((n_ctx_tokens, head_dim), F32)], axis=0)
    sin = jnp.concatenate([jnp.tile(sin, (n_batch, 1)), jnp.zeros((n_ctx_tokens, head_dim), F32)], axis=0)
    return cos, sin


def _diff_attn_rows(lam, q_ref, k_all, v_all, sw_ref, o_ref, rows, head_dim, out_scale):
    c = head_dim ** -0.5 * math.log2(math.e)
    es, invs = [], []
    for s in range(2):
        sl = slice(s * head_dim, (s + 1) * head_dim)
        sc = lax.dot_general(q_ref[rows, sl], k_all[:, sl], (((1,), (1,)), ((), ())), preferred_element_type=F32)
        yield
        e = jnp.exp2((sc - jnp.max(sc, axis=-1, keepdims=True)) * c)
        invs.append(1.0 / jnp.sum(e, axis=-1, keepdims=True))
        es.append(e.astype(BF16))
        yield
    v = v_all[...]
    o0 = jnp.dot(es[0], v, preferred_element_type=F32)
    o1 = jnp.dot(es[1], v, preferred_element_type=F32)
    yield
    o = o0 * invs[0] - (lam * invs[1]) * o1
    y = o * lax.rsqrt(jnp.mean(o * o, axis=-1, keepdims=True) + NORM_EPS) * sw_ref[...]
    o_ref[rows, :] = (y * out_scale).astype(o_ref.dtype)
    yield


def _diff_attn_body(lam_ref, q_ref, kl_ref, kc_ref, vl_ref, vc_ref, sw_ref, o_ref, k_all, v_all, *,
                    head_dim, out_scale, row_chunk):
    @pl.when(pl.program_id(2) == 0)
    def _():
        n_l = kl_ref.shape[0]
        k_all[:n_l] = kl_ref[...]
        k_all[n_l:] = kc_ref[...]
        v_all[:n_l] = vl_ref[...]
        v_all[n_l:] = vc_ref[...]

    lam = lam_ref[0]
    tq = q_ref.shape[0]
    _run_interleaved(
        _diff_attn_rows(lam, q_ref, k_all, v_all, sw_ref, o_ref, slice(r, r + row_chunk), head_dim, out_scale)
        for r in range(0, tq, row_chunk))


def diff_attention(qkv, lmbda, sub_norm, n_batch, lat_len, ctx_len, n_heads, head_dim, out_scale,
                   tq=ATTN_TILE_Q, row_chunk=ATTN_ROW_CHUNK):
    hw = 2 * head_dim
    nq = lat_len // tq
    ctx_blk0 = n_batch * lat_len // ctx_len
    body = functools.partial(_diff_attn_body, head_dim=head_dim, out_scale=out_scale, row_chunk=row_chunk)
    n_keys = lat_len + ctx_len
    return pl.pallas_call(
        body, grid=(n_batch, n_heads, nq),
        in_specs=[pl.BlockSpec(memory_space=pltpu.SMEM),
                  pl.BlockSpec((tq, hw), lambda b, h, i: (b * nq + i, h)),
                  pl.BlockSpec((lat_len, hw), lambda b, h, i: (b, n_heads + h)),
                  pl.BlockSpec((ctx_len, hw), lambda b, h, i: (ctx_blk0 + b, n_heads + h)),
                  pl.BlockSpec((lat_len, hw), lambda b, h, i: (b, 2 * n_heads + h)),
                  pl.BlockSpec((ctx_len, hw), lambda b, h, i: (ctx_blk0 + b, 2 * n_heads + h)),
                  pl.BlockSpec((1, hw), lambda b, h, i: (0, 0))],
        out_specs=pl.BlockSpec((tq, hw), lambda b, h, i: (b * nq + i, h)),
        out_shape=jax.ShapeDtypeStruct((n_batch * lat_len, n_heads * hw), BF16),
        scratch_shapes=[pltpu.VMEM((n_keys, hw), BF16), pltpu.VMEM((n_keys, hw), BF16)],
        compiler_params=_params("arbitrary", "arbitrary", "arbitrary"), name="diff_attention",
    )(lmbda.reshape(1), qkv, qkv, qkv, qkv, qkv, sub_norm.reshape(1, hw))


SCAN_TILE = 256
CHUNK = 64
SUB = 16
NEG_BIG = -1e30
NEG_INF = float("-inf")
DELTA_INV_PASSES = 1
DELTA_HEADS_PER_STEP = 8
HGRN_HEADS_PER_STEP = 4


def _dot(a, b):
    return jnp.dot(a.astype(BF16), b.astype(BF16), preferred_element_type=F32)


def _dot_nt(a, b):
    return lax.dot_general(a.astype(BF16), b.astype(BF16), (((1,), (1,)), ((), ())),
                           preferred_element_type=F32)


def _split3(x):
    hi = x.astype(BF16)
    r = x - hi.astype(F32)
    mid = r.astype(BF16)
    lo = (r - mid.astype(F32)).astype(BF16)
    return hi, mid, lo


def _dot_exact_lhs01(m01, x):
    hi, mid, lo = _split3(x)
    m = m01.astype(BF16)
    return (jnp.dot(m, hi, preferred_element_type=F32) + jnp.dot(m, mid, preferred_element_type=F32)
            + jnp.dot(m, lo, preferred_element_type=F32))


def _dot_exact_rhs01(x, m01):
    hi, mid, lo = _split3(x)
    m = m01.astype(BF16)
    return (jnp.dot(hi, m, preferred_element_type=F32) + jnp.dot(mid, m, preferred_element_type=F32)
            + jnp.dot(lo, m, preferred_element_type=F32))


def _dot3(a, b):
    ah = a.astype(BF16)
    al = (a - ah.astype(F32)).astype(BF16)
    bh = b.astype(BF16)
    bl = (b - bh.astype(F32)).astype(BF16)
    return (jnp.dot(ah, bh, preferred_element_type=F32) + jnp.dot(ah, bl, preferred_element_type=F32)
            + jnp.dot(al, bh, preferred_element_type=F32))


def _tile_masks(n, reverse):
    i = lax.broadcasted_iota(jnp.int32, (n, n), 0)
    j = lax.broadcasted_iota(jnp.int32, (n, n), 1)
    same = (i // CHUNK) == (j // CHUNK)
    if reverse:
        return same & (i <= j), same & (i < j)
    return same & (i >= j), same & (i > j)


def _segment_tile(b, s, reverse, nl, nc, ctx_tile0):
    if reverse:
        return jnp.where(s < nc, ctx_tile0 + b * nc + (nc - 1 - s), b * nl + (nl - 1 - (s - nc)))
    return jnp.where(s < nc, ctx_tile0 + b * nc + s, b * nl + (s - nc))


def _dn_prep_body(x_ref, prev_ref, next_ref, w_ref, o_ref, *, tiles_per_lat_seg, tiles_per_ctx_seg, n_lat_tiles,
                  head_dim, q_scale):
    i = pl.program_id(0)
    j = pl.program_id(1)
    is_lat = i < n_lat_tiles
    pos = jnp.where(is_lat, i % tiles_per_lat_seg, (i - n_lat_tiles) % tiles_per_ctx_seg)
    seg_first = pos == 0
    seg_last = pos == jnp.where(is_lat, tiles_per_lat_seg, tiles_per_ctx_seg) - 1
    x = x_ref[...]
    tl = x.shape[0]
    prev = jnp.where(seg_first, 0.0, prev_ref[...])
    nxt = jnp.where(seg_last, 0.0, next_ref[...])
    xp = jnp.concatenate([prev, x, nxt], axis=0)
    w = w_ref[...]
    n_taps = 5
    acc = None
    for t in range(n_taps):
        off = 8 + t - n_taps // 2
        term = xp[off:off + tl] * w[t:t + 1]
        acc = term if acc is None else acc + term
    y = _silu(acc)
    scale = jnp.where(j == 0, q_scale, 1.0)
    outs = []
    for h in range(y.shape[1] // head_dim):
        yh = y[:, h * head_dim:(h + 1) * head_dim]
        nrm = lax.rsqrt(jnp.sum(yh * yh, axis=-1, keepdims=True) + 1e-6) * scale
        outs.append(yh * jnp.where(j == 2, 1.0, nrm))
    o_ref[...] = jnp.concatenate(outs, axis=1)


def dn_prep(proj_dn, conv_w_t, geom, ctx_len, width, head_dim):
    n_lat, lat_len, n_batch = geom
    t = proj_dn.shape[0]
    tl = SCAN_TILE
    rows8 = tl // 8
    n_tiles = t // tl
    body = functools.partial(_dn_prep_body, tiles_per_lat_seg=lat_len // tl, tiles_per_ctx_seg=ctx_len // tl,
                             n_lat_tiles=n_lat // tl,
                             head_dim=head_dim, q_scale=head_dim ** -0.5)
    last8 = t // 8 - 1
    return pl.pallas_call(
        body, grid=(n_tiles, 3),
        in_specs=[pl.BlockSpec((tl, width), lambda i, j: (i, j)),
                  pl.BlockSpec((8, width), lambda i, j: (jnp.maximum(i * rows8 - 1, 0), j)),
                  pl.BlockSpec((8, width), lambda i, j: (jnp.minimum((i + 1) * rows8, last8), j)),
                  pl.BlockSpec((8, width), lambda i, j: (0, j))],
        out_specs=pl.BlockSpec((tl, width), lambda i, j: (i, j)),
        out_shape=jax.ShapeDtypeStruct((t, 3 * width), F32),
        compiler_params=_params("arbitrary", "arbitrary"), name="dn_prep",
    )(proj_dn, proj_dn, proj_dn, conv_w_t)


def _softplus(x):
    return jnp.maximum(x, 0.0) + jnp.log(1.0 + jnp.exp(-jnp.abs(x)))


def _gate_prep_body(h_ref, wc_ref, wr_ref, alog_c_ref, dtb_c_ref, alog_r_ref, dtb_r_ref, gc_ref, gr_ref, *, n_heads):
    h = h_ref[...]
    tl = h.shape[0]
    nd = 2 * n_heads
    raw_c = jnp.dot(h, wc_ref[...].astype(BF16), preferred_element_type=F32)
    raw_r = lax.dot_general(wr_ref[...].astype(BF16), h, (((1,), (1,)), ((), ())),
                            preferred_element_type=F32)
    g_c = -jnp.exp(alog_c_ref[...]) * _softplus(raw_c[:, :nd] + dtb_c_ref[...])
    g_r = -jnp.exp(alog_r_ref[...]) * _softplus(raw_r[:nd, :] + dtb_r_ref[...])
    incl_f, _ = _tile_masks(tl, False)
    incl_b, _ = _tile_masks(tl, True)
    one_f = jnp.where(incl_f, 1.0, 0.0)
    one_b = jnp.where(incl_b, 1.0, 0.0)
    cum_c = jnp.concatenate([_dot_exact_lhs01(one_f, g_c[:, :n_heads]),
                             _dot_exact_lhs01(one_b, g_c[:, n_heads:])], axis=1)
    cum_r = jnp.concatenate([_dot_exact_rhs01(g_r[:n_heads, :], one_b),
                             _dot_exact_rhs01(g_r[n_heads:, :], one_f)], axis=0)
    beta_c = _sigmoid(raw_c[:, nd:])
    gc_ref[...] = jnp.concatenate([cum_c, beta_c], axis=1)
    gr_ref[...] = jnp.concatenate([cum_r, jnp.zeros_like(cum_r)], axis=0)


def gate_prep(h_bf, w_gate, a_log, dt_bias, n_heads):
    t, d = h_bf.shape
    tl = SCAN_TILE
    nd = 2 * n_heads
    body = functools.partial(_gate_prep_body, n_heads=n_heads)
    full = lambda shape: pl.BlockSpec(shape, lambda i: (0, 0))
    return pl.pallas_call(
        body, grid=(t // tl,),
        in_specs=[pl.BlockSpec((tl, d), lambda i: (i, 0)), full((d, 2 * nd)), full((2 * nd, d)),
                  full((1, nd)), full((1, nd)), full((nd, 1)), full((nd, 1))],
        out_specs=[pl.BlockSpec((tl, 2 * nd), lambda i: (i, 0)), pl.BlockSpec((2 * nd, tl), lambda i: (0, i))],
        out_shape=[jax.ShapeDtypeStruct((t, 2 * nd), F32), jax.ShapeDtypeStruct((2 * nd, t), F32)],
        compiler_params=_params("arbitrary"), name="gate_prep",
    )(h_bf, w_gate, w_gate.T, a_log.reshape(1, nd), dt_bias.reshape(1, nd),
      a_log.reshape(nd, 1), dt_bias.reshape(nd, 1))


def _select_col(x, idx):
    lane = lax.broadcasted_iota(jnp.int32, x.shape, 1)
    return jnp.sum(jnp.where(lane == idx, x, 0.0), axis=1, keepdims=True)


def _select_row(x, idx):
    row = lax.broadcasted_iota(jnp.int32, x.shape, 0)
    return jnp.sum(jnp.where(row == idx, x, 0.0), axis=0, keepdims=True)


def _run_interleaved(chains):
    chains = list(chains)
    while chains:
        alive = []
        for ch in chains:
            try:
                next(ch)
                alive.append(ch)
            except StopIteration:
                pass
        chains = alive


def _delta_chain(q, k, v, gc_col, gc_row, beta_col, s_ref, o_ref, cols, reverse, inv_passes):
    tl, kd_ = k.shape
    n_chunks = tl // CHUNK
    incl, strict = _tile_masks(tl, reverse)
    decay = jnp.exp(jnp.where(incl, gc_col - gc_row, NEG_BIG))
    kb, qb = k.astype(BF16), q.astype(BF16)
    kkt = _dot_nt(kb, kb)
    qkt = _dot_nt(qb, kb)
    yield
    x = jnp.where(strict, kkt * (-beta_col) * decay, 0.0)
    dot_inv = _dot3 if inv_passes == 3 else _dot
    ri = lax.broadcasted_iota(jnp.int32, (tl, tl), 0)
    ci = lax.broadcasted_iota(jnp.int32, (tl, tl), 1)
    r = jnp.where(ri == ci, 1.0, 0.0) + x
    n_sq = int(math.log2(CHUNK)) - 1
    for _ in range(n_sq):
        x = dot_inv(x, x)
        r = r + dot_inv(r, x)
        yield
    e_g = jnp.exp(gc_col)
    rhs = jnp.concatenate([v * beta_col, k * (beta_col * e_g)], axis=1)
    sol = dot_inv(r, rhs)
    yield
    u0, w = sol[:, :v.shape[1]], sol[:, v.shape[1]:]
    attn = (qkt * decay).astype(BF16)
    a_sol = _dot(attn, sol)
    o0 = a_sol[:, :v.shape[1]]
    qe = q * e_g - a_sol[:, v.shape[1]:]
    tot_rows = []
    for c in range(n_chunks):
        last = c * CHUNK if reverse else c * CHUNK + CHUNK - 1
        tot_rows.append(jnp.broadcast_to(gc_col[last:last + 1, :], (CHUNK, 1)))
    tot = jnp.concatenate(tot_rows, axis=0)
    kdec_t = jnp.transpose(k * jnp.exp(tot - gc_col))
    wu = jnp.concatenate([-w, u0], axis=1).astype(BF16)
    lane = lax.broadcasted_iota(jnp.int32, kdec_t.shape, 1)
    pns = [_dot(jnp.where((lane // CHUNK) == c, kdec_t, 0.0), wu) for c in range(n_chunks)]
    yield
    order = range(n_chunks - 1, -1, -1) if reverse else range(n_chunks)
    for c in order:
        rows = slice(c * CHUNK, (c + 1) * CHUNK)
        pn = pns[c]
        lhs = jnp.concatenate([qe[rows], pn[:, :kd_]], axis=0)
        s = s_ref[...]
        res = _dot(lhs, s)
        o_ref[rows, cols] = o0[rows] + res[:CHUNK]
        last = c * CHUNK if reverse else c * CHUNK + CHUNK - 1
        gl = jnp.exp(gc_col[last:last + 1, :])
        s_ref[...] = gl * s + res[CHUNK:] + pn[:, kd_:]
        yield


def _delta_body(qf_ref, kf_ref, vf_ref, gcf_ref, grf_ref, qb_ref, kb_ref, vb_ref, gcb_ref, grb_ref,
                of_ref, ob_ref, sf_ref, sb_ref, *, n_heads, head_dim, inv_passes):
    hg = pl.program_id(1)
    heads_per_step = sf_ref.shape[0]

    @pl.when(pl.program_id(2) == 0)
    def _():
        sf_ref[...] = jnp.zeros_like(sf_ref)
        sb_ref[...] = jnp.zeros_like(sb_ref)

    chains = []
    for g in range(heads_per_step):
        cols = slice(g * head_dim, (g + 1) * head_dim)
        for reverse, (q_ref, k_ref, v_ref, gc_ref, gr_ref, o_ref, s_ref) in enumerate(
                [(qf_ref, kf_ref, vf_ref, gcf_ref, grf_ref, of_ref, sf_ref),
                 (qb_ref, kb_ref, vb_ref, gcb_ref, grb_ref, ob_ref, sb_ref)]):
            idx = reverse * n_heads + hg * heads_per_step + g
            gcs = gc_ref[...]
            gc_col = _select_col(gcs, idx)
            beta_col = _select_col(gcs, 2 * n_heads + idx)
            gc_row = _select_row(gr_ref[...], idx)
            chains.append(_delta_chain(q_ref[:, cols], k_ref[:, cols], v_ref[:, cols], gc_col, gc_row, beta_col,
                                       s_ref.at[g], o_ref, cols, bool(reverse), inv_passes))
    _run_interleaved(chains)


def delta_scan(qkv, gc, gr, geom, ctx_len, n_heads, head_dim, inv_passes=3, heads_per_step=DELTA_HEADS_PER_STEP):
    n_lat, lat_len, n_batch = geom
    t = qkv.shape[0]
    tl = SCAN_TILE
    nl, nc, ctx0 = lat_len // tl, ctx_len // tl, n_lat // tl
    n_hg = n_heads // heads_per_step
    gw = heads_per_step * head_dim
    tile = functools.partial(_segment_tile, nl=nl, nc=nc, ctx_tile0=ctx0)
    specs = []
    for reverse in (False, True):
        tix = functools.partial(tile, reverse=reverse)
        specs += [pl.BlockSpec((tl, gw), lambda b, h, s, tix=tix: (tix(b, s), h)),
                  pl.BlockSpec((tl, gw), lambda b, h, s, tix=tix: (tix(b, s), n_hg + h)),
                  pl.BlockSpec((tl, gw), lambda b, h, s, tix=tix: (tix(b, s), 2 * n_hg + h)),
                  pl.BlockSpec((tl, 4 * n_heads), lambda b, h, s, tix=tix: (tix(b, s), 0)),
                  pl.BlockSpec((4 * n_heads, tl), lambda b, h, s, tix=tix: (0, tix(b, s)))]
    out_specs = [pl.BlockSpec((tl, gw), lambda b, h, s, tix=functools.partial(tile, reverse=r): (tix(b, s), h))
                 for r in (False, True)]
    body = functools.partial(_delta_body, n_heads=n_heads, head_dim=head_dim, inv_passes=inv_passes)
    return pl.pallas_call(
        body, grid=(n_batch, n_hg, nl + nc), in_specs=specs, out_specs=out_specs,
        out_shape=[jax.ShapeDtypeStruct((t, n_heads * head_dim), F32)] * 2,
        scratch_shapes=[pltpu.VMEM((heads_per_step, head_dim, head_dim), F32)] * 2,
        compiler_params=_params("arbitrary", "arbitrary", "arbitrary"), name="delta_scan",
    )(qkv, qkv, qkv, gc, gr, qkv, qkv, qkv, gc, gr)


def _hgrn_diag(q, k, cum, v, stage_ref, diag_ref, reverse):
    n_sub = q.shape[0] // SUB
    for n, val in enumerate((q, k, cum, v)):
        stage_ref[n] = val

    def slabs(n):
        return [stage_ref.at[n][pl.ds(r, n_sub, stride=SUB), :] for r in range(SUB)]

    q_x, k_x, p_x, v_x = slabs(0), slabs(1), slabs(2), slabs(3)
    pairs = [(i, j) for i in range(SUB) for j in range(SUB) if (j >= i if reverse else j <= i)]
    terms = []
    for i, j in pairs:
        qk = q_x[i] * k_x[j]
        terms.append((qk if i == j else qk * jnp.exp(p_x[i] - p_x[j])).astype(BF16))
    kd_ = terms[0].shape[1]
    a_rep = jnp.dot(jnp.concatenate(terms, axis=0), jnp.ones((kd_, kd_), BF16), preferred_element_type=F32)
    o_x = [None] * SUB
    for n, (i, j) in enumerate(pairs):
        contrib = a_rep[n * n_sub:(n + 1) * n_sub] * v_x[j]
        o_x[i] = contrib if o_x[i] is None else o_x[i] + contrib
    for r in range(SUB):
        diag_ref[pl.ds(r, n_sub, stride=SUB), :] = o_x[r]


def _hgrn_chain(hq_ref, hf_ref, hv_ref, cols, lb, st_ref, o_ref, stage_ref, diag_ref, reverse):
    hq, hf, hv = hq_ref[:, cols], hf_ref[:, cols], hv_ref[:, cols]
    tl, kd_ = hq.shape
    n_chunks = tl // CHUNK
    n_sub = tl // SUB
    sub_per_chunk = CHUNK // SUB
    q = _silu(hq)
    f = lb + (1.0 - lb) * _sigmoid(hf)
    k = 1.0 - f
    lf = jnp.log(f)
    incl, _ = _tile_masks(tl, reverse)
    cum = _dot_exact_lhs01(jnp.where(incl, 1.0, 0.0), lf)
    excl = cum - lf

    def bcast_rows(src, row, n):
        return jnp.broadcast_to(src[row:row + 1, :], (n, kd_))

    chunk_last = [(c * CHUNK if reverse else c * CHUNK + CHUNK - 1) for c in range(n_chunks)]
    tot = jnp.concatenate([bcast_rows(cum, chunk_last[c], CHUNK) for c in range(n_chunks)], axis=0)
    sub_first = [(m * SUB + SUB - 1 if reverse else m * SUB) for m in range(n_sub)]
    r_sub = jnp.concatenate([bcast_rows(excl, sub_first[m], SUB) for m in range(n_sub)], axis=0)
    q_t = q * jnp.exp(cum - r_sub)
    qd = q * jnp.exp(cum)
    kd = k * jnp.exp(tot - cum)
    vb = hv.astype(BF16)

    i = lax.broadcasted_iota(jnp.int32, (tl, tl), 0)
    j = lax.broadcasted_iota(jnp.int32, (tl, tl), 1)
    same = (i // CHUNK) == (j // CHUNK)
    pos_i = (i % CHUNK) // SUB
    pos_j = (j % CHUNK) // SUB
    if reverse:
        pos_i, pos_j = sub_per_chunk - 1 - pos_i, sub_per_chunk - 1 - pos_j
    a_off = jnp.zeros((tl, tl), F32)
    for lvl in range(1, sub_per_chunk):
        ref_rows = []
        for c in range(n_chunks):
            m = c * sub_per_chunk + (sub_per_chunk - 1 - lvl if reverse else lvl)
            ref_rows.append(bcast_rows(excl, sub_first[m], CHUNK))
        r_lvl = jnp.concatenate(ref_rows, axis=0)
        k_t = k * jnp.exp(jnp.minimum(r_lvl - cum, 0.0))
        a_l = _dot_nt(q_t, k_t)
        a_off = a_off + jnp.where(same & (pos_i == lvl) & (pos_j < lvl), a_l, 0.0)
        yield
    o_intra = _dot(a_off, vb)
    yield

    _hgrn_diag(q, k, cum, hv, stage_ref, diag_ref, reverse)
    yield
    o_intra = o_intra + diag_ref[...]

    v_t = jnp.transpose(hv)
    lane = lax.broadcasted_iota(jnp.int32, v_t.shape, 1)
    kdb = kd.astype(BF16)
    n_ts = [_dot(jnp.where((lane // CHUNK) == c, v_t, 0.0), kdb) for c in range(n_chunks)]
    yield
    order = range(n_chunks - 1, -1, -1) if reverse else range(n_chunks)
    for c in order:
        rows = slice(c * CHUNK, (c + 1) * CHUNK)
        st = st_ref[...]
        o_ref[rows, cols] = o_intra[rows] + _dot_nt(qd[rows], st)
        st_ref[...] = st * jnp.exp(cum[chunk_last[c]:chunk_last[c] + 1, :]) + n_ts[c]
        yield


def _hgrn_body(qf_ref, ff_ref, vf_ref, qb_ref, fb_ref, vb_ref, lb_ref, of_ref, ob_ref, sf_ref, sb_ref, stage_ref,
               diag_ref, *, key_dim):
    heads_per_step = sf_ref.shape[0]

    @pl.when(pl.program_id(2) == 0)
    def _():
        sf_ref[...] = jnp.zeros_like(sf_ref)
        sb_ref[...] = jnp.zeros_like(sb_ref)

    chains = []
    for g in range(heads_per_step):
        cols = slice(g * key_dim, (g + 1) * key_dim)
        lb = lb_ref[:, cols]
        chains.append(_hgrn_chain(qf_ref, ff_ref, vf_ref, cols, lb, sf_ref.at[g], of_ref,
                                  stage_ref.at[2 * g], diag_ref.at[2 * g], False))
        chains.append(_hgrn_chain(qb_ref, fb_ref, vb_ref, cols, lb, sb_ref.at[g], ob_ref,
                                  stage_ref.at[2 * g + 1], diag_ref.at[2 * g + 1], True))
    _run_interleaved(chains)


def hgrn_scan(proj_hg, lb, geom, ctx_len, n_heads, key_dim, heads_per_step=HGRN_HEADS_PER_STEP):
    n_lat, lat_len, n_batch = geom
    t = proj_hg.shape[0]
    tl = SCAN_TILE
    nl, nc, ctx0 = lat_len // tl, ctx_len // tl, n_lat // tl
    n_hg = n_heads // heads_per_step
    gw = heads_per_step * key_dim
    tile = functools.partial(_segment_tile, nl=nl, nc=nc, ctx_tile0=ctx0)
    specs = []
    for reverse in (False, True):
        tix = functools.partial(tile, reverse=reverse)
        fcol = (1 + int(reverse)) * n_hg
        specs += [pl.BlockSpec((tl, gw), lambda b, h, s, tix=tix: (tix(b, s), h)),
                  pl.BlockSpec((tl, gw), lambda b, h, s, tix=tix, fcol=fcol: (tix(b, s), fcol + h)),
                  pl.BlockSpec((tl, gw), lambda b, h, s, tix=tix: (tix(b, s), 3 * n_hg + h))]
    specs.append(pl.BlockSpec((1, gw), lambda b, h, s: (0, h)))
    out_specs = [pl.BlockSpec((tl, gw), lambda b, h, s, tix=functools.partial(tile, reverse=r): (tix(b, s), h))
                 for r in (False, True)]
    return pl.pallas_call(
        functools.partial(_hgrn_body, key_dim=key_dim), grid=(n_batch, n_hg, nl + nc),
        in_specs=specs, out_specs=out_specs,
        out_shape=[jax.ShapeDtypeStruct((t, n_heads * key_dim), F32)] * 2,
        scratch_shapes=[pltpu.VMEM((heads_per_step, key_dim, key_dim), F32)] * 2
        + [pltpu.VMEM((2 * heads_per_step, 4, tl, key_dim), F32), pltpu.VMEM((2 * heads_per_step, tl, key_dim), F32)],
        compiler_params=_params("arbitrary", "arbitrary", "arbitrary"), name="hgrn_scan",
    )(proj_hg, proj_hg, proj_hg, proj_hg, proj_hg, proj_hg, lb)


def _mix_out_body(df_ref, db_ref, hf_ref, hb_ref, z_ref, og_ref, dnw_ref, hgw_ref, o_ref, *, head_dim):
    def normed(o, nw):
        outs = []
        for h in range(o.shape[1] // head_dim):
            oh = o[:, h * head_dim:(h + 1) * head_dim]
            outs.append(oh * lax.rsqrt(jnp.mean(oh * oh, axis=-1, keepdims=True) + NORM_EPS) * nw)
        return jnp.concatenate(outs, axis=1)

    dn = normed(df_ref[...] + db_ref[...], dnw_ref[...]) * _silu(z_ref[...])
    hg = normed(hf_ref[...] + hb_ref[...], hgw_ref[...]) * _sigmoid(og_ref[...])
    half = dn.shape[1]
    o_ref[:, :half] = dn.astype(o_ref.dtype)
    o_ref[:, half:] = hg.astype(o_ref.dtype)


def mix_out(dn_f, dn_b, hg_f, hg_b, z_src, z_blk, og_src, og_blk, dn_norm, hg_norm, head_dim):
    t, w = dn_f.shape
    tl = SCAN_TILE
    row = lambda i: (i, 0)
    body = functools.partial(_mix_out_body, head_dim=head_dim)
    return pl.pallas_call(
        body, grid=(t // tl,),
        in_specs=[pl.BlockSpec((tl, w), row)] * 4
        + [pl.BlockSpec((tl, w), lambda i: (i, z_blk)), pl.BlockSpec((tl, w), lambda i: (i, og_blk)),
           pl.BlockSpec((1, head_dim), lambda i: (0, 0)), pl.BlockSpec((1, head_dim), lambda i: (0, 0))],
        out_specs=pl.BlockSpec((tl, 2 * w), row),
        out_shape=jax.ShapeDtypeStruct((t, 2 * w), BF16),
        compiler_params=_params("arbitrary"), name="mix_out",
    )(dn_f, dn_b, hg_f, hg_b, z_src, og_src, dn_norm.reshape(1, head_dim), hg_norm.reshape(1, head_dim))


def _first_max(x, ids, n):
    m = jnp.max(x, axis=0, keepdims=True)
    first = jnp.min(jnp.where(x == m, ids, n), axis=0, keepdims=True)
    return m, first


def _route_body(lg_ref, bias_ref, idx_ref, rank_ref, w_ref, cnt_ref, carry_ref, *,
                n_groups, topk_groups, top_k, scale):
    i = pl.program_id(0)

    @pl.when(i == 0)
    def _():
        carry_ref[...] = jnp.zeros_like(carry_ref)

    lg = lg_ref[...]
    n_exp, tl = lg.shape
    per = n_exp // n_groups
    scores = 1.0 / (1.0 + jnp.exp(-lg))
    biased = scores + bias_ref[...]
    sub = lax.broadcasted_iota(jnp.int32, (per, tl), 0)
    g_rows = []
    for g in range(n_groups):
        xg = biased[g * per:(g + 1) * per]
        m1, i1 = _first_max(xg, sub, per)
        m2 = jnp.max(jnp.where(sub == i1, NEG_INF, xg), axis=0, keepdims=True)
        g_rows.append(m1 + m2)
    gscore = jnp.concatenate(g_rows, axis=0)
    gid = lax.broadcasted_iota(jnp.int32, (n_groups, tl), 0)
    gsel = jnp.zeros((n_groups, tl), jnp.bool_)
    for _ in range(topk_groups):
        _, first = _first_max(gscore, gid, n_groups)
        hit = gid == first
        gsel = gsel | hit
        gscore = jnp.where(hit, NEG_INF, gscore)
    eid = lax.broadcasted_iota(jnp.int32, (n_exp, tl), 0)
    gmask = jnp.concatenate([jnp.broadcast_to(gsel[g:g + 1], (per, tl)) for g in range(n_groups)], axis=0)
    masked = jnp.where(gmask, biased, NEG_INF)
    sel = jnp.zeros((n_exp, tl), jnp.bool_)
    hits, firsts = [], []
    for _ in range(top_k):
        _, first = _first_max(masked, eid, n_exp)
        hit = eid == first
        hits.append(hit)
        firsts.append(first)
        sel = sel | hit
        masked = jnp.where(hit, NEG_INF, masked)
    self = jnp.where(sel, 1.0, 0.0)
    ti = lax.broadcasted_iota(jnp.int32, (tl, tl), 0)
    tj = lax.broadcasted_iota(jnp.int32, (tl, tl), 1)
    before = jnp.where(ti < tj, 1.0, 0.0).astype(BF16)
    carry = carry_ref[...]
    rank_full = jnp.dot(self.astype(BF16), before, preferred_element_type=F32) + carry
    w_rows = [jnp.sum(jnp.where(hit, scores, 0.0), axis=0, keepdims=True) for hit in hits]
    r_rows = [jnp.sum(jnp.where(hit, rank_full, 0.0), axis=0, keepdims=True) for hit in hits]
    w8 = jnp.concatenate(w_rows, axis=0)
    idx_ref[...] = jnp.concatenate(firsts, axis=0)
    rank_ref[...] = jnp.concatenate(r_rows, axis=0).astype(jnp.int32)
    w_ref[...] = w8 / jnp.sum(w8, axis=0, keepdims=True) * scale
    carry = carry + jnp.sum(self, axis=1, keepdims=True)
    carry_ref[...] = carry
    cnt_ref[...] = carry.astype(jnp.int32)


def route(logits_t, router_b, tl=ROW_TILE):
    n_exp, t = logits_t.shape
    body = functools.partial(_route_body, n_groups=N_GROUPS, topk_groups=TOPK_GROUPS, top_k=TOP_K,
                             scale=ROUTED_SCALE)
    tok_spec = pl.BlockSpec((TOP_K, tl), lambda i: (0, i))
    return pl.pallas_call(
        body, grid=(t // tl,),
        in_specs=[pl.BlockSpec((n_exp, tl), lambda i: (0, i)), pl.BlockSpec((n_exp, 1), lambda i: (0, 0))],
        out_specs=[tok_spec, tok_spec, tok_spec, pl.BlockSpec((n_exp, 1), lambda i: (0, 0))],
        out_shape=[jax.ShapeDtypeStruct((TOP_K, t), jnp.int32), jax.ShapeDtypeStruct((TOP_K, t), jnp.int32),
                   jax.ShapeDtypeStruct((TOP_K, t), F32), jax.ShapeDtypeStruct((n_exp, 1), jnp.int32)],
        scratch_shapes=[pltpu.VMEM((n_exp, 1), F32)],
        compiler_params=_params("arbitrary"), name="route",
    )(logits_t, router_b.reshape(n_exp, 1))


def _slab_copy(src_ref, src_tok, dst_ref, dst_tok, rows, sem):
    return pltpu.make_async_copy(src_ref.at[pl.ds(pl.multiple_of(src_tok * rows, rows), rows)],
                                 dst_ref.at[pl.ds(pl.multiple_of(dst_tok * rows, rows), rows)], sem)


def _dispatch_body(dest_ref, pad_ref, h_ref, xs_ref, zero_ref, sem, *, top_k, rows):
    tl = h_ref.shape[0] // rows
    n_exp = pad_ref.shape[1]

    def issue(r, carry):
        for k in range(top_k):
            _slab_copy(h_ref, r, xs_ref, dest_ref[r * top_k + k], rows, sem.at[0]).start(priority=k % 2)
        return carry

    lax.fori_loop(0, tl, issue, 0, unroll=DMA_LOOP_UNROLL)

    @pl.when(pl.program_id(0) == 0)
    def _():
        zero_ref[...] = jnp.zeros_like(zero_ref)
        largest = zero_ref.shape[0] // rows
        pieces = [largest >> b for b in range(largest.bit_length())]

        def pad_copies(e, wait):
            first, n_pad = pad_ref[0, e], pad_ref[1, e]
            for size in pieces:
                @pl.when((n_pad & size) != 0)
                def _():
                    slot0 = first + (n_pad & ~(2 * size - 1))
                    cp = pltpu.make_async_copy(
                        zero_ref.at[pl.ds(0, size * rows)],
                        xs_ref.at[pl.ds(pl.multiple_of(slot0 * rows, rows), size * rows)], sem.at[1])
                    if wait:
                        cp.wait()
                    else:
                        cp.start()

        def fill(e, carry):
            pad_copies(e, False)
            return carry

        def drain_pad(e, carry):
            pad_copies(e, True)
            return carry

        lax.fori_loop(0, n_exp, fill, 0)
        lax.fori_loop(0, n_exp, drain_pad, 0)

    def drain(r, carry):
        for k in range(top_k):
            _slab_copy(h_ref, 0, xs_ref, 0, rows, sem.at[0]).wait()
        return carry

    lax.fori_loop(0, tl, drain, 0, unroll=DMA_LOOP_UNROLL)


def dispatch(h_slab, dest_flat, pad_info, n_slot, top_k, rows, tl=ROW_TILE):
    t = h_slab.shape[0] // rows
    return pl.pallas_call(
        functools.partial(_dispatch_body, top_k=top_k, rows=rows), grid=(t // tl,),
        in_specs=[pl.BlockSpec((tl * top_k,), lambda i: (i,), memory_space=pltpu.SMEM),
                  pl.BlockSpec(memory_space=pltpu.SMEM),
                  pl.BlockSpec((tl * rows, LANES), lambda i: (i, 0))],
        out_specs=pl.BlockSpec(memory_space=pl.ANY),
        out_shape=jax.ShapeDtypeStruct((n_slot * rows, LANES), h_slab.dtype),
        scratch_shapes=[pltpu.VMEM((EXPERT_SLOT_BLOCK // 2 * rows, LANES), h_slab.dtype),
                        pltpu.SemaphoreType.DMA((2,))],
        compiler_params=_params("arbitrary"), name="dispatch",
    )(dest_flat, pad_info, h_slab)


def _combine_body(dest_ref, dest_next_ref, w_ref, sh_ref, x_ref, gate_ref, y_ref, o_ref, buf_ref, acc_ref, sem, *,
                  top_k, rows):
    i = pl.program_id(0)
    n = pl.num_programs(0)
    tl = x_ref.shape[0]

    def gather(d_ref, slot):
        def issue(r, carry):
            for k in range(top_k):
                _slab_copy(y_ref, d_ref[r * top_k + k], buf_ref.at[slot], k * tl + r, rows,
                           sem.at[slot]).start(priority=k % 2)
            return carry
        lax.fori_loop(0, tl, issue, 0, unroll=DMA_LOOP_UNROLL)

    @pl.when(i == 0)
    def _():
        gather(dest_ref, 0)

    @pl.when(i + 1 < n)
    def _():
        gather(dest_next_ref, (i + 1) % 2)

    slot = i % 2

    def drain(r, carry):
        for k in range(top_k):
            _slab_copy(y_ref, 0, buf_ref.at[slot], 0, rows, sem.at[slot]).wait()
        return carry

    lax.fori_loop(0, tl, drain, 0, unroll=DMA_LOOP_UNROLL)
    def unpack_f32(pk):
        return (lax.bitcast_convert_type(pk & jnp.uint32(0xFFFF0000), F32),
                lax.bitcast_convert_type(pk << jnp.uint32(16), F32))

    w = w_ref[...]
    acc_hi, acc_lo = unpack_f32(sh_ref[...])
    for k in range(top_k):
        wk = jnp.broadcast_to(w[:, k:k + 1], (tl, LANES))
        wk = jnp.broadcast_to(wk[:, None, :], (tl, rows, LANES)).reshape(tl * rows, LANES)
        y_hi, y_lo = unpack_f32(buf_ref[slot, pl.ds(k * tl * rows, tl * rows), :])
        acc_hi = acc_hi + y_hi * wk
        acc_lo = acc_lo + y_lo * wk
    acc_ref[0] = acc_hi
    acc_ref[1] = acc_lo
    for half in range(2):
        for c in range(rows):
            cols = slice((half * rows + c) * LANES, (half * rows + c + 1) * LANES)
            o_ref[:, cols] = (x_ref[:, cols]
                              + gate_ref[0][:, cols] * acc_ref.at[half][pl.ds(c, tl, stride=rows), :])


def combine(y_slab, dest_flat, w, shared_slab, x, mod3, gate_idx, geom, tl=COMBINE_TILE):
    t, d = x.shape
    rows = d // 2 // LANES
    top_k = w.shape[1]
    n_lat, lat_len, n_batch = geom
    n_tiles = t // tl
    row = functools.partial(_mod_row, tile_rows=tl, n_lat=n_lat, lat_len=lat_len, n_batch=n_batch)
    return pl.pallas_call(
        functools.partial(_combine_body, top_k=top_k, rows=rows), grid=(n_tiles,),
        in_specs=[pl.BlockSpec((tl * top_k,), lambda i: (i,), memory_space=pltpu.SMEM),
                  pl.BlockSpec((tl * top_k,), lambda i: (jnp.minimum(i + 1, n_tiles - 1),),
                               memory_space=pltpu.SMEM),
                  pl.BlockSpec((tl, top_k), lambda i: (i, 0)),
                  pl.BlockSpec((tl * rows, LANES), lambda i: (i, 0)),
                  pl.BlockSpec((tl, d), lambda i: (i, 0)),
                  pl.BlockSpec((1, 1, d), lambda i: (row(i) * 6 + gate_idx, 0, 0)),
                  pl.BlockSpec(memory_space=pl.ANY)],
        out_specs=pl.BlockSpec((tl, d), lambda i: (i, 0)),
        out_shape=jax.ShapeDtypeStruct((t, d), F32),
        scratch_shapes=[pltpu.VMEM((2, top_k * tl * rows, LANES), jnp.uint32),
                        pltpu.VMEM((2, tl * rows, LANES), F32), pltpu.SemaphoreType.DMA((2,))],
        compiler_params=_params("arbitrary"), name="combine",
    )(dest_flat, dest_flat, w, shared_slab, x, mod3, y_slab)


def moe_ffn(x, h_pk, logits_t, router_b, exp_gate_up, exp_down, shared_gate_up, shared_down, mod3, geom):
    t = x.shape[0]
    slab_rows = h_pk.shape[0] // t
    e_count = exp_gate_up.shape[0]
    blk = EXPERT_SLOT_BLOCK
    idx8, rank8, w8, counts = route(logits_t, router_b)
    counts = counts[:, 0]
    padded = (counts + blk - 1) // blk * blk
    pad_end = jnp.cumsum(padded)
    pad_start = pad_end - padded
    start8 = jnp.sum(jnp.where(idx8[None] == jnp.arange(e_count, dtype=jnp.int32)[:, None, None],
                               pad_start[:, None, None], 0), axis=0)
    dest_flat = (start8 + rank8).T.reshape(-1)
    n_blk = (t * TOP_K + e_count * (blk - 1)) // blk + 1
    n_blk = -(-n_blk // EXPERT_BLOCKS_PER_STEP) * EXPERT_BLOCKS_PER_STEP
    n_slot = n_blk * blk
    blk_starts = jnp.arange(n_blk, dtype=jnp.int32) * blk
    blk_e = jnp.sum((pad_end[None, :] <= blk_starts[:, None]).astype(jnp.int32), axis=1)
    blk_e = jnp.minimum(blk_e, e_count - 1)
    n_used = (pad_end[-1] // blk).astype(jnp.int32).reshape(1)
    pad_info = jnp.stack([pad_start + counts, padded - counts]).astype(jnp.int32)
    x_sorted = dispatch(h_pk, dest_flat, pad_info, n_slot, TOP_K, slab_rows)
    y_slot = expert_blocks(x_sorted, exp_gate_up, exp_down, blk_e, n_used, blk)
    shared = expert_blocks(h_pk, shared_gate_up[None], shared_down[None], jnp.zeros((t // blk,), jnp.int32),
                           jnp.full((1,), t // blk, jnp.int32), blk)
    return combine(y_slot, dest_flat, w8.T, shared, x, mod3, 5, geom)


def kernel(x, c, ctx, c_ctx, hg_lb_logits, l0_mod_w, l0_mod_b, l0_norm1, l0_norm2, l0_w_in, l0_dn_conv, l0_dn_a_log, l0_dn_dt_bias, l0_dn_norm, l0_hg_norm, l0_w_out, l0_router_w, l0_router_b, l0_exp_gate_up, l0_exp_down, l0_shared_gate_up, l0_shared_down, l1_mod_w, l1_mod_b, l1_norm1, l1_norm2, l1_w_in, l1_q_norm, l1_k_norm, l1_lambda, l1_sub_norm, l1_w_out, l1_router_w, l1_router_b, l1_exp_gate_up, l1_exp_down, l1_shared_gate_up, l1_shared_down):
    n_batch, lat_len, d = x.shape
    ctx_len = ctx.shape[1]
    n_lat = n_batch * lat_len
    n_ctx = n_batch * ctx_len
    geom = (n_lat, lat_len, n_batch)
    geom_lat_only = (n_lat, lat_len, n_batch)

    xs = jnp.concatenate([x.reshape(n_lat, d), ctx.reshape(n_ctx, d)], axis=0)
    cond = jnp.concatenate([c, c_ctx[None], jnp.zeros((MOD_ROWS - n_batch - 1, d), F32)], axis=0)

    mod3 = modulation(cond, l0_mod_w, l0_mod_b).reshape(MOD_ROWS * 6, 1, d)
    h = adaln(xs, l0_norm1, mod3, 0, geom)
    n_dn = IN0_SIZES[0] + IN0_SIZES[1]
    n_gate = IN0_SIZES[2] + IN0_SIZES[3]
    proj_dn = matmul(h, l0_w_in, n_cols=n_dn)
    proj_hg = matmul(h, l0_w_in[:, n_dn + n_gate:])
    gc, gr = gate_prep(h, l0_w_in[:, n_dn:n_dn + n_gate], l0_dn_a_log, l0_dn_dt_bias, DN_HEADS)
    conv_t = jnp.concatenate([l0_dn_conv.T, jnp.zeros((8 - l0_dn_conv.shape[1], 3 * DN_WIDTH), F32)], axis=0)
    qkv = dn_prep(proj_dn, conv_t, geom, ctx_len, DN_WIDTH, DN_HEAD_DIM)
    dn_f, dn_b = delta_scan(qkv, gc, gr, geom, ctx_len, DN_HEADS, DN_HEAD_DIM, inv_passes=DELTA_INV_PASSES)
    lb = jnp.cumsum(jax.nn.softmax(hg_lb_logits, axis=0), axis=0)[0:1]
    hg_f, hg_b = hgrn_scan(proj_hg, lb, geom, ctx_len, HG_HEADS, HG_KEY_DIM)
    y = mix_out(dn_f, dn_b, hg_f, hg_b, proj_dn, 3, proj_hg, 4, l0_dn_norm, l0_hg_norm, DN_HEAD_DIM)
    xs = matmul_residual(y, l0_w_out, xs, mod3, 2, geom)
    h, logits_t = adaln(xs, l0_norm2, mod3, 3, geom, router_w=l0_router_w)
    xs = moe_ffn(xs, h, logits_t, l0_router_b, l0_exp_gate_up, l0_exp_down, l0_shared_gate_up, l0_shared_down,
                 mod3, geom)

    mod3 = modulation(cond, l1_mod_w, l1_mod_b).reshape(MOD_ROWS * 6, 1, d)
    h = adaln(xs, l1_norm1, mod3, 0, geom)
    cos_t, sin_t = rope_tables(lat_len, n_batch, n_ctx, DA_HEAD_DIM)
    qkv = matmul_qkv(h, l1_w_in, l1_q_norm, l1_k_norm, cos_t, sin_t)
    lam_init = 0.8 - 0.6 * math.exp(-0.3 * 1)
    lmbda = (jnp.exp(jnp.sum(l1_lambda[0] * l1_lambda[1])) - jnp.exp(jnp.sum(l1_lambda[2] * l1_lambda[3]))
             + lam_init)
    y = diff_attention(qkv, lmbda, l1_sub_norm, n_batch, lat_len, ctx_len, DA_HEADS, DA_HEAD_DIM,
                       1.0 - lam_init)
    xl = matmul_residual(y, l1_w_out, xs, mod3, 2, geom_lat_only)
    h, logits_t = adaln(xl, l1_norm2, mod3, 3, geom_lat_only, router_w=l1_router_w)
    xl = moe_ffn(xl, h, logits_t, l1_router_b, l1_exp_gate_up, l1_exp_down, l1_shared_gate_up, l1_shared_down,
                 mod3, geom_lat_only)
    return xl.reshape(n_batch, lat_len, d)
```
